```python
import jax, jax.numpy as jnp
from jax import lax
import numpy as np

D_MODEL = 1024
BATCH = 8
SEQ = 4096
DEPTH = 2

LRU_WIDTH = D_MODEL
LRU_HEADS = 8
LRU_HEAD_DIM = LRU_WIDTH // LRU_HEADS
LRU_C = 8.0
CONV_WIDTH = 4
DN_HEADS = 8
DN_HEAD_K = 128
DN_HEAD_V = 128
DN_KEY_DIM = DN_HEADS * DN_HEAD_K
DN_VALUE_DIM = DN_HEADS * DN_HEAD_V
DN_CONV_DIM = 2 * DN_KEY_DIM + DN_VALUE_DIM
CHUNK = 64
D_MIX = LRU_WIDTH + DN_VALUE_DIM
D_IN = 2 * LRU_WIDTH + DN_CONV_DIM + DN_VALUE_DIM + 2 * DN_HEADS
SPLITS = [LRU_WIDTH, 2 * LRU_WIDTH, 2 * LRU_WIDTH + DN_CONV_DIM,
          2 * LRU_WIDTH + DN_CONV_DIM + DN_VALUE_DIM,
          2 * LRU_WIDTH + DN_CONV_DIM + DN_VALUE_DIM + DN_HEADS]
EPS = 1e-6

kernel_name = "hymba_rglru_gated_deltanet_trunk"


def rmsnorm(x, w):
    xf = x.astype(jnp.float32)
    y = xf * lax.rsqrt(jnp.mean(xf * xf, axis=-1, keepdims=True) + EPS)
    return (y * w.astype(jnp.float32)).astype(x.dtype)


def gated_rmsnorm(o, z, w):
    of = o.astype(jnp.float32)
    y = of * lax.rsqrt(jnp.mean(of * of, axis=-1, keepdims=True) + EPS)
    return (y * w.astype(jnp.float32) * jax.nn.silu(z.astype(jnp.float32))).astype(o.dtype)


def l2norm(x):
    return x * lax.rsqrt(jnp.sum(x * x, axis=-1, keepdims=True) + EPS)


def causal_dwconv(x, w):
    K = w.shape[0]
    S = x.shape[1]
    xp = jnp.pad(x, ((0, 0), (K - 1, 0), (0, 0)))
    y = xp[:, 0:S] * w[0]
    for j in range(1, K):
        y = y + xp[:, j:j + S] * w[j]
    return y


def rg_lru(x, wa, ba, wx, bx, lam):
    B_, S_, W = x.shape
    xh = x.reshape(B_, S_, LRU_HEADS, LRU_HEAD_DIM)
    r = jax.nn.sigmoid(jnp.einsum('bshi,hij->bshj', xh, wa).reshape(B_, S_, W) + ba)
    i = jax.nn.sigmoid(jnp.einsum('bshi,hij->bshj', xh, wx).reshape(B_, S_, W) + bx)
    log_a = -LRU_C * r * jax.nn.softplus(-lam)
    a = jnp.exp(log_a)
    mult = jnp.sqrt(-jnp.expm1(2.0 * log_a))
    b = mult * (i * x)

    def combine(left, right):
        a_l, b_l = left
        a_r, b_r = right
        return a_l * a_r, a_r * b_l + b_r

    _, h = lax.associative_scan(combine, (a, b), axis=1)
    return h


def gated_delta_rule(q, k, v, g, beta):
    B_, S_, H, dk = q.shape
    dv = v.shape[-1]
    N = S_ // CHUNK

    def to_chunks(t):
        return t.reshape(B_, N, CHUNK, H, t.shape[-1]).transpose(0, 3, 1, 2, 4)

    q, k, v = to_chunks(q), to_chunks(k), to_chunks(v)
    g = g.reshape(B_, N, CHUNK, H).transpose(0, 3, 1, 2)
    beta = beta.reshape(B_, N, CHUNK, H).transpose(0, 3, 1, 2)
    k_beta = k * beta[..., None]
    v_beta = v * beta[..., None]
    gc = jnp.cumsum(g, axis=-1)
    tri = jnp.tril(jnp.ones((CHUNK, CHUNK), dtype=bool))
    strict = jnp.tril(jnp.ones((CHUNK, CHUNK), dtype=bool), -1)
    decay_mask = jnp.exp(jnp.where(tri, gc[..., :, None] - gc[..., None, :], -jnp.inf))
    A = jnp.where(strict, jnp.einsum('bhncd,bhnjd->bhncj', k_beta, k) * decay_mask, 0.0)
    T = A + jnp.eye(CHUNK, dtype=A.dtype)
    u = lax.linalg.triangular_solve(T, v_beta, left_side=True, lower=True, unit_diagonal=True)
    w = lax.linalg.triangular_solve(T, k_beta * jnp.exp(gc)[..., None],
                                    left_side=True, lower=True, unit_diagonal=True)
    attn = jnp.where(tri, jnp.einsum('bhncd,bhnjd->bhncj', q, k) * decay_mask, 0.0)

    def step(state, xs):
        q_i, k_i, u_i, w_i, attn_i, gc_i = xs
        v_new = u_i - jnp.einsum('bhcd,bhde->bhce', w_i, state)
        o_i = (jnp.einsum('bhcd,bhde->bhce', q_i * jnp.exp(gc_i)[..., None], state)
               + jnp.einsum('bhcj,bhje->bhce', attn_i, v_new))
        g_last = gc_i[..., -1]
        state = (state * jnp.exp(g_last)[..., None, None]
                 + jnp.einsum('bhcd,bhce->bhde', k_i * jnp.exp(g_last[..., None] - gc_i)[..., None], v_new))
        return state, o_i

    xs = tuple(jnp.moveaxis(t, 2, 0) for t in (q, k, u, w, attn, gc))
    state0 = jnp.zeros((B_, H, dk, dv), dtype=q.dtype)
    _, o = lax.scan(step, state0, xs)
    return o.transpose(1, 0, 3, 2, 4).reshape(B_, S_, H, dv)


def hybrid_layer(x, norm_w, w_in, lru_conv_w, lru_conv_b, lru_wa, lru_ba, lru_wx, lru_bx,
                 lru_lambda, lru_norm_w, dn_conv_w, dn_A_log, dn_dt_bias, dn_norm_w, w_out):
    B_, S_, _ = x.shape
    f32 = jnp.float32
    h = rmsnorm(x, norm_w)
    proj = jnp.einsum('bsd,de->bse', h, w_in)
    lru_x, lru_z, dn_qkv, dn_z, dn_b, dn_a = jnp.split(proj, SPLITS, axis=-1)

    xc = (causal_dwconv(lru_x, lru_conv_w) + lru_conv_b).astype(f32)
    hl = rg_lru(xc, lru_wa.astype(f32), lru_ba.astype(f32), lru_wx.astype(f32),
                lru_bx.astype(f32), lru_lambda.astype(f32))
    y_lru = gated_rmsnorm(hl.reshape(B_, S_, LRU_HEADS, LRU_HEAD_DIM),
                          lru_z.reshape(B_, S_, LRU_HEADS, LRU_HEAD_DIM),
                          lru_norm_w.reshape(LRU_HEADS, LRU_HEAD_DIM))

    qkv = jax.nn.silu(causal_dwconv(dn_qkv, dn_conv_w)).astype(f32)
    q, k, v = jnp.split(qkv, [DN_KEY_DIM, 2 * DN_KEY_DIM], axis=-1)
    q = l2norm(q.reshape(B_, S_, DN_HEADS, DN_HEAD_K)) * (DN_HEAD_K ** -0.5)
    k = l2norm(k.reshape(B_, S_, DN_HEADS, DN_HEAD_K))
    v = v.reshape(B_, S_, DN_HEADS, DN_HEAD_V)
    beta = jax.nn.sigmoid(dn_b.astype(f32))
    g = -jnp.exp(dn_A_log.astype(f32)) * jax.nn.softplus(dn_a.astype(f32) + dn_dt_bias.astype(f32))
    o = gated_delta_rule(q, k, v, g, beta)
    y_dn = gated_rmsnorm(o, dn_z.reshape(B_, S_, DN_HEADS, DN_HEAD_V), dn_norm_w)

    y = jnp.concatenate([y_lru.reshape(B_, S_, LRU_WIDTH), y_dn.reshape(B_, S_, DN_VALUE_DIM)],
                        axis=-1).astype(x.dtype)
    return x + jnp.einsum('bse,ed->bsd', y, w_out).astype(x.dtype)


def _fwd_setup_inputs(seed: int = 0) -> dict:
    key = jax.random.key(seed)
    ks = jax.random.split(key, 20)
    nrm = jax.random.normal
    L = DEPTH
    x = nrm(ks[0], (BATCH, SEQ, D_MODEL), jnp.float32)
    norm_w = 1.0 + 0.01 * nrm(ks[1], (L, D_MODEL), jnp.float32)
    w_in = nrm(ks[2], (L, D_MODEL, D_IN), jnp.float32) * D_MODEL ** -0.5
    lru_conv_w = nrm(ks[3], (L, CONV_WIDTH, LRU_WIDTH), jnp.float32) * CONV_WIDTH ** -0.5
    lru_conv_b = 0.01 * nrm(ks[4], (L, LRU_WIDTH), jnp.float32)
    lru_wa = nrm(ks[5], (L, LRU_HEADS, LRU_HEAD_DIM, LRU_HEAD_DIM), jnp.float32) * LRU_HEAD_DIM ** -0.5
    lru_ba = 0.01 * nrm(ks[6], (L, LRU_WIDTH), jnp.float32)
    lru_wx = nrm(ks[7], (L, LRU_HEADS, LRU_HEAD_DIM, LRU_HEAD_DIM), jnp.float32) * LRU_HEAD_DIM ** -0.5
    lru_bx = 0.01 * nrm(ks[8], (L, LRU_WIDTH), jnp.float32)
    a0 = jax.random.uniform(ks[9], (L, LRU_WIDTH), jnp.float32, 0.9, 0.999)
    s = a0 ** (1.0 / LRU_C)
    lru_lambda = jnp.log(s) - jnp.log1p(-s)
    lru_norm_w = 1.0 + 0.01 * nrm(ks[10], (L, LRU_WIDTH), jnp.float32)
    dn_conv_w = nrm(ks[11], (L, CONV_WIDTH, DN_CONV_DIM), jnp.float32) * CONV_WIDTH ** -0.5
    dn_A_log = jnp.log(jax.random.uniform(ks[12], (L, DN_HEADS), jnp.float32, 1.0, 16.0))
    dt = jnp.exp(jax.random.uniform(ks[13], (L, DN_HEADS), jnp.float32,
                                    float(np.log(1e-3)), float(np.log(1e-1))))
    dn_dt_bias = dt + jnp.log(-jnp.expm1(-dt))
    dn_norm_w = 1.0 + 0.01 * nrm(ks[14], (L, DN_HEAD_V), jnp.float32)
    w_out = nrm(ks[15], (L, D_MIX, D_MODEL), jnp.float32) * D_MIX ** -0.5
    final_norm_w = 1.0 + 0.01 * nrm(ks[16], (D_MODEL,), jnp.float32)
    return {"x": x, "norm_w": norm_w, "w_in": w_in, "lru_conv_w": lru_conv_w,
            "lru_conv_b": lru_conv_b, "lru_wa": lru_wa, "lru_ba": lru_ba, "lru_wx": lru_wx,
            "lru_bx": lru_bx, "lru_lambda": lru_lambda, "lru_norm_w": lru_norm_w,
            "dn_conv_w": dn_conv_w, "dn_A_log": dn_A_log, "dn_dt_bias": dn_dt_bias,
            "dn_norm_w": dn_norm_w, "w_out": w_out, "final_norm_w": final_norm_w}


def _fwd_reference(x, norm_w, w_in, lru_conv_w, lru_conv_b, lru_wa, lru_ba, lru_wx, lru_bx,
              lru_lambda, lru_norm_w, dn_conv_w, dn_A_log, dn_dt_bias, dn_norm_w, w_out,
              final_norm_w):
    h = x
    for l in range(DEPTH):
        h = hybrid_layer(h, norm_w[l], w_in[l], lru_conv_w[l], lru_conv_b[l], lru_wa[l], lru_ba[l],
                         lru_wx[l], lru_bx[l], lru_lambda[l], lru_norm_w[l], dn_conv_w[l],
                         dn_A_log[l], dn_dt_bias[l], dn_norm_w[l], w_out[l])
    return rmsnorm(h, final_norm_w)


import jax as _jax
import jax.numpy as _jnp

TWIN_FORMAT = 'train_step'
FWD_PARAMS = ['x', 'norm_w', 'w_in', 'lru_conv_w', 'lru_conv_b', 'lru_wa', 'lru_ba', 'lru_wx', 'lru_bx', 'lru_lambda', 'lru_norm_w', 'dn_conv_w', 'dn_A_log', 'dn_dt_bias', 'dn_norm_w', 'w_out', 'final_norm_w']
TWIN_WEIGHTS = ['norm_w', 'w_in', 'lru_conv_w', 'lru_conv_b', 'lru_wa', 'lru_ba', 'lru_wx', 'lru_bx', 'lru_lambda', 'lru_norm_w', 'dn_conv_w', 'dn_A_log', 'dn_dt_bias', 'dn_norm_w', 'w_out', 'final_norm_w']
TWIN_DIFF_INPUT = 'x'
TWIN_INPUTS = ['x', 'norm_w', 'w_in', 'lru_conv_w', 'lru_conv_b', 'lru_wa', 'lru_ba', 'lru_wx', 'lru_bx', 'lru_lambda', 'lru_norm_w', 'dn_conv_w', 'dn_A_log', 'dn_dt_bias', 'dn_norm_w', 'w_out', 'final_norm_w', 'loss_target', 'm_norm_w', 'm_w_in', 'm_lru_conv_w', 'm_lru_conv_b', 'm_lru_wa', 'm_lru_ba', 'm_lru_wx', 'm_lru_bx', 'm_lru_lambda', 'm_lru_norm_w', 'm_dn_conv_w', 'm_dn_A_log', 'm_dn_dt_bias', 'm_dn_norm_w', 'm_w_out', 'm_final_norm_w', 'v_norm_w', 'v_w_in', 'v_lru_conv_w', 'v_lru_conv_b', 'v_lru_wa', 'v_lru_ba', 'v_lru_wx', 'v_lru_bx', 'v_lru_lambda', 'v_lru_norm_w', 'v_dn_conv_w', 'v_dn_A_log', 'v_dn_dt_bias', 'v_dn_norm_w', 'v_w_out', 'v_final_norm_w']
TWIN_OUTPUTS = ['loss', 'grad_x', 'grad_norm_w', 'grad_w_in', 'grad_lru_conv_w', 'grad_lru_conv_b', 'grad_lru_wa', 'grad_lru_ba', 'grad_lru_wx', 'grad_lru_bx', 'grad_lru_lambda', 'grad_lru_norm_w', 'grad_dn_conv_w', 'grad_dn_A_log', 'grad_dn_dt_bias', 'grad_dn_norm_w', 'grad_w_out', 'grad_final_norm_w', 'delta_norm_w', 'delta_w_in', 'delta_lru_conv_w', 'delta_lru_conv_b', 'delta_lru_wa', 'delta_lru_ba', 'delta_lru_wx', 'delta_lru_bx', 'delta_lru_lambda', 'delta_lru_norm_w', 'delta_dn_conv_w', 'delta_dn_A_log', 'delta_dn_dt_bias', 'delta_dn_norm_w', 'delta_w_out', 'delta_final_norm_w', 'new_m_norm_w', 'new_m_w_in', 'new_m_lru_conv_w', 'new_m_lru_conv_b', 'new_m_lru_wa', 'new_m_lru_ba', 'new_m_lru_wx', 'new_m_lru_bx', 'new_m_lru_lambda', 'new_m_lru_norm_w', 'new_m_dn_conv_w', 'new_m_dn_A_log', 'new_m_dn_dt_bias', 'new_m_dn_norm_w', 'new_m_w_out', 'new_m_final_norm_w', 'new_v_norm_w', 'new_v_w_in', 'new_v_lru_conv_w', 'new_v_lru_conv_b', 'new_v_lru_wa', 'new_v_lru_ba', 'new_v_lru_wx', 'new_v_lru_bx', 'new_v_lru_lambda', 'new_v_lru_norm_w', 'new_v_dn_conv_w', 'new_v_dn_A_log', 'new_v_dn_dt_bias', 'new_v_dn_norm_w', 'new_v_w_out', 'new_v_final_norm_w']
TWIN_LEAF_KINDS = {'loss': 'loss', 'grad_x': 'grad_x', 'grad_norm_w': 'grad_w', 'grad_w_in': 'grad_w', 'grad_lru_conv_w': 'grad_w', 'grad_lru_conv_b': 'grad_w', 'grad_lru_wa': 'grad_w', 'grad_lru_ba': 'grad_w', 'grad_lru_wx': 'grad_w', 'grad_lru_bx': 'grad_w', 'grad_lru_lambda': 'grad_w', 'grad_lru_norm_w': 'grad_w', 'grad_dn_conv_w': 'grad_w', 'grad_dn_A_log': 'grad_w', 'grad_dn_dt_bias': 'grad_w', 'grad_dn_norm_w': 'grad_w', 'grad_w_out': 'grad_w', 'grad_final_norm_w': 'grad_w', 'delta_norm_w': 'delta_w', 'delta_w_in': 'delta_w', 'delta_lru_conv_w': 'delta_w', 'delta_lru_conv_b': 'delta_w', 'delta_lru_wa': 'delta_w', 'delta_lru_ba': 'delta_w', 'delta_lru_wx': 'delta_w', 'delta_lru_bx': 'delta_w', 'delta_lru_lambda': 'delta_w', 'delta_lru_norm_w': 'delta_w', 'delta_dn_conv_w': 'delta_w', 'delta_dn_A_log': 'delta_w', 'delta_dn_dt_bias': 'delta_w', 'delta_dn_norm_w': 'delta_w', 'delta_w_out': 'delta_w', 'delta_final_norm_w': 'delta_w', 'new_m_norm_w': 'new_m', 'new_m_w_in': 'new_m', 'new_m_lru_conv_w': 'new_m', 'new_m_lru_conv_b': 'new_m', 'new_m_lru_wa': 'new_m', 'new_m_lru_ba': 'new_m', 'new_m_lru_wx': 'new_m', 'new_m_lru_bx': 'new_m', 'new_m_lru_lambda': 'new_m', 'new_m_lru_norm_w': 'new_m', 'new_m_dn_conv_w': 'new_m', 'new_m_dn_A_log': 'new_m', 'new_m_dn_dt_bias': 'new_m', 'new_m_dn_norm_w': 'new_m', 'new_m_w_out': 'new_m', 'new_m_final_norm_w': 'new_m', 'new_v_norm_w': 'new_v', 'new_v_w_in': 'new_v', 'new_v_lru_conv_w': 'new_v', 'new_v_lru_conv_b': 'new_v', 'new_v_lru_wa': 'new_v', 'new_v_lru_ba': 'new_v', 'new_v_lru_wx': 'new_v', 'new_v_lru_bx': 'new_v', 'new_v_lru_lambda': 'new_v', 'new_v_lru_norm_w': 'new_v', 'new_v_dn_conv_w': 'new_v', 'new_v_dn_A_log': 'new_v', 'new_v_dn_dt_bias': 'new_v', 'new_v_dn_norm_w': 'new_v', 'new_v_w_out': 'new_v', 'new_v_final_norm_w': 'new_v'}


def _forward(args):
    return _fwd_reference(*[args[k] for k in FWD_PARAMS])


def _output_shape():
    out = _jax.eval_shape(lambda: _forward(_fwd_setup_inputs(0)))
    return out.shape, out.dtype

N_MICROBATCH = 1
ADAM_LR = 0.001
ADAM_B1 = 0.9
ADAM_B2 = 0.999
ADAM_EPS = 1e-08
ADAM_WD = 0.01
ADAM_STEP = 10
PER_EXAMPLE_BATCH_AXIS = {'x': 0, 'loss_target': 0}
SHARED_INPUTS = []
_WEIGHT_DTYPES = {'norm_w': _jnp.float32, 'w_in': _jnp.float32, 'lru_conv_w': _jnp.float32, 'lru_conv_b': _jnp.float32, 'lru_wa': _jnp.float32, 'lru_ba': _jnp.float32, 'lru_wx': _jnp.float32, 'lru_bx': _jnp.float32, 'lru_lambda': _jnp.float32, 'lru_norm_w': _jnp.float32, 'dn_conv_w': _jnp.float32, 'dn_A_log': _jnp.float32, 'dn_dt_bias': _jnp.float32, 'dn_norm_w': _jnp.float32, 'w_out': _jnp.float32, 'final_norm_w': _jnp.float32}
MOMENT_SCALE = {'norm_w': 1.509076e-01, 'w_in': 5.929156e-02, 'lru_conv_w': 7.489177e-02, 'lru_conv_b': 8.021660e-01, 'lru_wa': 2.163298e-02, 'lru_ba': 1.930323e-02, 'lru_wx': 3.874120e-02, 'lru_bx': 2.431357e-02, 'lru_lambda': 3.622238e-02, 'lru_norm_w': 7.608926e-02, 'dn_conv_w': 4.908996e-02, 'dn_A_log': 7.500476e-01, 'dn_dt_bias': 7.067760e-01, 'dn_norm_w': 1.878284e-01, 'w_out': 9.491285e-02, 'final_norm_w': 3.198388e+01}


def _to_microbatches(a, axis):
    t = _jnp.moveaxis(a, axis, 0)
    t = t.reshape((N_MICROBATCH, t.shape[0] // N_MICROBATCH) + t.shape[1:])
    return _jnp.moveaxis(t, 1, axis + 1)


def setup_inputs(seed: int = 0) -> dict:
    inp = _fwd_setup_inputs(seed)
    key = _jax.random.fold_in(_jax.random.key(seed), 7919)
    shape, _ = _output_shape()
    out = dict(inp)
    out["loss_target"] = _jax.random.normal(_jax.random.fold_in(key, 0), shape, _jnp.float32)
    for i, name in enumerate(TWIN_WEIGHTS):
        w = inp[name].astype(_jnp.float32)
        if MOMENT_SCALE is None:
            s = _jnp.sqrt(_jnp.mean(_jnp.square(w)) + 1e-30)
        else:
            s = MOMENT_SCALE[name]
        km, kv = _jax.random.split(_jax.random.fold_in(key, i + 1))
        out[name] = w
        out["m_" + name] = s * _jax.random.normal(km, w.shape, _jnp.float32)
        out["v_" + name] = (s * s) * _jax.random.uniform(kv, w.shape, _jnp.float32, 0.5, 1.5)
    if N_MICROBATCH > 1:
        for name, axis in PER_EXAMPLE_BATCH_AXIS.items():
            out[name] = _to_microbatches(out[name], axis)
    return {'x': out['x'], 'norm_w': out['norm_w'], 'w_in': out['w_in'], 'lru_conv_w': out['lru_conv_w'], 'lru_conv_b': out['lru_conv_b'], 'lru_wa': out['lru_wa'], 'lru_ba': out['lru_ba'], 'lru_wx': out['lru_wx'], 'lru_bx': out['lru_bx'], 'lru_lambda': out['lru_lambda'], 'lru_norm_w': out['lru_norm_w'], 'dn_conv_w': out['dn_conv_w'], 'dn_A_log': out['dn_A_log'], 'dn_dt_bias': out['dn_dt_bias'], 'dn_norm_w': out['dn_norm_w'], 'w_out': out['w_out'], 'final_norm_w': out['final_norm_w'], 'loss_target': out['loss_target'], 'm_norm_w': out['m_norm_w'], 'm_w_in': out['m_w_in'], 'm_lru_conv_w': out['m_lru_conv_w'], 'm_lru_conv_b': out['m_lru_conv_b'], 'm_lru_wa': out['m_lru_wa'], 'm_lru_ba': out['m_lru_ba'], 'm_lru_wx': out['m_lru_wx'], 'm_lru_bx': out['m_lru_bx'], 'm_lru_lambda': out['m_lru_lambda'], 'm_lru_norm_w': out['m_lru_norm_w'], 'm_dn_conv_w': out['m_dn_conv_w'], 'm_dn_A_log': out['m_dn_A_log'], 'm_dn_dt_bias': out['m_dn_dt_bias'], 'm_dn_norm_w': out['m_dn_norm_w'], 'm_w_out': out['m_w_out'], 'm_final_norm_w': out['m_final_norm_w'], 'v_norm_w': out['v_norm_w'], 'v_w_in': out['v_w_in'], 'v_lru_conv_w': out['v_lru_conv_w'], 'v_lru_conv_b': out['v_lru_conv_b'], 'v_lru_wa': out['v_lru_wa'], 'v_lru_ba': out['v_lru_ba'], 'v_lru_wx': out['v_lru_wx'], 'v_lru_bx': out['v_lru_bx'], 'v_lru_lambda': out['v_lru_lambda'], 'v_lru_norm_w': out['v_lru_norm_w'], 'v_dn_conv_w': out['v_dn_conv_w'], 'v_dn_A_log': out['v_dn_A_log'], 'v_dn_dt_bias': out['v_dn_dt_bias'], 'v_dn_norm_w': out['v_dn_norm_w'], 'v_w_out': out['v_w_out'], 'v_final_norm_w': out['v_final_norm_w']}


def _loss(weights, diff, rest, loss_target):
    with _jax.named_scope("forward"):
        args = {**rest, TWIN_DIFF_INPUT: diff, **{k: w.astype(_WEIGHT_DTYPES[k]) for k, w in weights.items()}}
        y = _forward(args)
    with _jax.named_scope("loss_head"):
        err = _jnp.square(y.astype(_jnp.float32) - loss_target)
        return 0.5 * _jnp.sum(_jnp.mean(err, axis=-1)) if err.ndim else 0.5 * err


def _adamw(w, g, m, v):
    m = ADAM_B1 * m + (1.0 - ADAM_B1) * g
    v = ADAM_B2 * v + (1.0 - ADAM_B2) * _jnp.square(g)
    m_hat = m / (1.0 - ADAM_B1 ** ADAM_STEP)
    v_hat = v / (1.0 - ADAM_B2 ** ADAM_STEP)
    delta = -ADAM_LR * (m_hat / (_jnp.sqrt(v_hat) + ADAM_EPS) + ADAM_WD * w)
    return delta, m, v


def reference(x, norm_w, w_in, lru_conv_w, lru_conv_b, lru_wa, lru_ba, lru_wx, lru_bx, lru_lambda, lru_norm_w, dn_conv_w, dn_A_log, dn_dt_bias, dn_norm_w, w_out, final_norm_w, loss_target, m_norm_w, m_w_in, m_lru_conv_w, m_lru_conv_b, m_lru_wa, m_lru_ba, m_lru_wx, m_lru_bx, m_lru_lambda, m_lru_norm_w, m_dn_conv_w, m_dn_A_log, m_dn_dt_bias, m_dn_norm_w, m_w_out, m_final_norm_w, v_norm_w, v_w_in, v_lru_conv_w, v_lru_conv_b, v_lru_wa, v_lru_ba, v_lru_wx, v_lru_bx, v_lru_lambda, v_lru_norm_w, v_dn_conv_w, v_dn_A_log, v_dn_dt_bias, v_dn_norm_w, v_w_out, v_final_norm_w):
    given = dict(x=x, norm_w=norm_w, w_in=w_in, lru_conv_w=lru_conv_w, lru_conv_b=lru_conv_b, lru_wa=lru_wa, lru_ba=lru_ba, lru_wx=lru_wx, lru_bx=lru_bx, lru_lambda=lru_lambda, lru_norm_w=lru_norm_w, dn_conv_w=dn_conv_w, dn_A_log=dn_A_log, dn_dt_bias=dn_dt_bias, dn_norm_w=dn_norm_w, w_out=w_out, final_norm_w=final_norm_w, loss_target=loss_target, m_norm_w=m_norm_w, m_w_in=m_w_in, m_lru_conv_w=m_lru_conv_w, m_lru_conv_b=m_lru_conv_b, m_lru_wa=m_lru_wa, m_lru_ba=m_lru_ba, m_lru_wx=m_lru_wx, m_lru_bx=m_lru_bx, m_lru_lambda=m_lru_lambda, m_lru_norm_w=m_lru_norm_w, m_dn_conv_w=m_dn_conv_w, m_dn_A_log=m_dn_A_log, m_dn_dt_bias=m_dn_dt_bias, m_dn_norm_w=m_dn_norm_w, m_w_out=m_w_out, m_final_norm_w=m_final_norm_w, v_norm_w=v_norm_w, v_w_in=v_w_in, v_lru_conv_w=v_lru_conv_w, v_lru_conv_b=v_lru_conv_b, v_lru_wa=v_lru_wa, v_lru_ba=v_lru_ba, v_lru_wx=v_lru_wx, v_lru_bx=v_lru_bx, v_lru_lambda=v_lru_lambda, v_lru_norm_w=v_lru_norm_w, v_dn_conv_w=v_dn_conv_w, v_dn_A_log=v_dn_A_log, v_dn_dt_bias=v_dn_dt_bias, v_dn_norm_w=v_dn_norm_w, v_w_out=v_w_out, v_final_norm_w=v_final_norm_w)
    weights = {n: given[n] for n in TWIN_WEIGHTS}
    shared = {n: given[n] for n in SHARED_INPUTS}
    per_example = {n: given[n] for n in ['x']}
    grad_fn = _jax.value_and_grad(_loss, argnums=(0, 1))

    def one_microbatch(ex, loss_target):
        ex = dict(ex)
        diff = ex.pop(TWIN_DIFF_INPUT)
        return grad_fn(weights, diff, {**shared, **ex}, loss_target)

    if N_MICROBATCH == 1:
        loss, (grad_w, grad_x) = one_microbatch(per_example, given["loss_target"])
    else:
        def body(carry, xs):
            loss_sum, grad_sum = carry
            l_k, (gw_k, gx_k) = one_microbatch(xs[0], xs[1])
            with _jax.named_scope("update"):
                return (loss_sum + l_k, _jax.tree.map(_jnp.add, grad_sum, gw_k)), gx_k

        init = (_jnp.zeros((), _jnp.float32), _jax.tree.map(_jnp.zeros_like, weights))
        (loss, grad_w), grad_x = _jax.lax.scan(body, init, (per_example, given["loss_target"]))
    with _jax.named_scope("update"):
        delta_w, new_m, new_v = {}, {}, {}
        for n in TWIN_WEIGHTS:
            delta_w[n], new_m[n], new_v[n] = _adamw(weights[n], grad_w[n], given["m_" + n], given["v_" + n])
    return (loss, grad_x, *[grad_w[n] for n in TWIN_WEIGHTS], *[delta_w[n] for n in TWIN_WEIGHTS],
            *[new_m[n] for n in TWIN_WEIGHTS], *[new_v[n] for n in TWIN_WEIGHTS])
```

```python
import jax
import jax.numpy as jnp
from jax import lax
from jax.experimental import pallas as pl
from jax.experimental.pallas import tpu as pltpu

f32 = jnp.float32
bf16 = jnp.bfloat16
HI = lax.Precision.HIGHEST
MESH = pl.DeviceIdType.MESH

D_MODEL = 1024
HEADS = 8
HD = 128
CHUNK = 64
LRU_T = 128
SEQ_BLOCK = 1024
LRU_C = 8.0
D_IN = 6160
D_MAIN = 6144
D_INP = 6272
D_MIX = 2048
LRU_W = 2 * HD
DN_W = 4 * HD
DN_BLK0 = (HEADS * LRU_W) // DN_W
BA_BLK = D_MAIN // HD
EPS = 1e-6
QSCALE = HD ** -0.5
N_DEV = 8
VMEM_LIMIT = 56 * 1024 * 1024

ADAM_LR, ADAM_B1, ADAM_B2, ADAM_EPS, ADAM_WD, ADAM_STEP = 0.001, 0.9, 0.999, 1e-08, 0.01, 10


def _sig(x):
    return 1.0 / (1.0 + jnp.exp(-x))


def _log1p(u):
    w = 1.0 + u
    d = w - 1.0
    return jnp.where(d == 0.0, u, jnp.log(w) * (u / jnp.where(d == 0.0, 1.0, d)))


def _softplus(x):
    return jnp.maximum(x, 0.0) + _log1p(jnp.exp(-jnp.abs(x)))


def _expm1(x):
    t = x * (1.0 + x * (0.5 + x * (1.0 / 6 + x * (1.0 / 24 + x * (1.0 / 120 + x * (1.0 / 720))))))
    return jnp.where(jnp.abs(x) < 0.2, t, jnp.exp(x) - 1.0)


def _dot(a, b, prec=None):
    return jnp.dot(a, b, precision=prec, preferred_element_type=f32)


def _dot_nt(a, b, prec=None):
    return lax.dot_general(a, b, (((1,), (1,)), ((), ())), precision=prec, preferred_element_type=f32)


def _dot_tn(a, b, prec=None):
    return lax.dot_general(a, b, (((0,), (0,)), ((), ())), precision=prec, preferred_element_type=f32)


def _b(x):
    return x.astype(bf16)


def _sum0(x):
    return jnp.sum(x, axis=0, keepdims=True)


def _sum1(x):
    return jnp.sum(x, axis=1, keepdims=True)


def _rowmean(x):
    return jnp.mean(x, axis=-1, keepdims=True)


def _dsilu(z, sg):
    return sg * (1.0 + z * (1.0 - sg))


def _conv_taps(x, halo):
    xx = jnp.concatenate([halo, x], axis=0)
    return [pltpu.roll(xx, 3, axis=0)[8:], pltpu.roll(xx, 2, axis=0)[8:], pltpu.roll(xx, 1, axis=0)[8:], x]


def _conv(xs, cw):
    return cw[0:1] * xs[0] + cw[1:2] * xs[1] + cw[2:3] * xs[2] + cw[3:4] * xs[3]


def _rows_before(ref, halo_ref, lanes, t0, c, sblk):
    tprev = pl.multiple_of(jnp.maximum(t0 - 8, 0), 8)
    return jnp.where(c > 0, ref[pl.ds(tprev, 8), lanes], jnp.where(sblk > 0, halo_ref[:, lanes], 0.0))


def _conv_back(dxc, halo_after, cw):
    rows = dxc.shape[0]
    dd = jnp.concatenate([dxc, halo_after], axis=0)
    out = cw[3:4] * dxc
    for s in (1, 2, 3):
        out = out + cw[3 - s:4 - s] * pltpu.roll(dd, rows + 8 - s, axis=0)[:rows]
    return out


def _params(sem=None):
    return pltpu.CompilerParams(dimension_semantics=sem, vmem_limit_bytes=VMEM_LIMIT)


def _seq_specs(sb, nsb, width, colblk, reverse):
    blk = (lambda i: nsb - 1 - i) if reverse else (lambda i: i)
    main = pl.BlockSpec((sb, width), lambda h, i: (blk(i), colblk(h)))
    halo = pl.BlockSpec((8, width), lambda h, i: (jnp.maximum(blk(i) * (sb // 8) - 1, 0), colblk(h)))
    return main, halo


def _matmul(a, b, *, ta=False, tb=False, tm, tn, tk, res=None, out_dtype=f32, name):
    if ta:
        kdim, m = a.shape
    else:
        m, kdim = a.shape
    n = b.shape[0] if tb else b.shape[1]
    tm, tn, tk = min(tm, m), min(tn, n), min(tk, kdim)
    assert m % tm == 0 and n % tn == 0 and kdim % tk == 0, (name, m, n, kdim, tm, tn, tk)
    nk = kdim // tk
    dims = (((0 if ta else 1,), (1 if tb else 0,)), ((), ()))
    has_res = res is not None

    def body(*refs):
        a_ref, b_ref = refs[:2]
        r_ref = refs[2] if has_res else None
        o_ref = refs[3] if has_res else refs[2]
        acc_ref = refs[-1] if nk > 1 else None
        part = lax.dot_general(_b(a_ref[...]), _b(b_ref[...]), dims, preferred_element_type=f32)

        def finish(total):
            if has_res:
                total = total + r_ref[...]
            o_ref[...] = total.astype(out_dtype)

        if nk == 1:
            finish(part)
        else:
            k = pl.program_id(2)

            @pl.when(k == 0)
            def _():
                acc_ref[...] = part

            @pl.when(k > 0)
            def _():
                acc_ref[...] += part

            @pl.when(k == nk - 1)
            def _():
                finish(acc_ref[...])

    a_spec = pl.BlockSpec((tk, tm), lambda i, j, k: (k, i)) if ta else pl.BlockSpec((tm, tk), lambda i, j, k: (i, k))
    b_spec = pl.BlockSpec((tn, tk), lambda i, j, k: (j, k)) if tb else pl.BlockSpec((tk, tn), lambda i, j, k: (k, j))
    o_spec = pl.BlockSpec((tm, tn), lambda i, j, k: (i, j))
    in_specs, args = [a_spec, b_spec], [a, b]
    if has_res:
        in_specs.append(o_spec)
        args.append(res)
    return pl.pallas_call(
        body, name=name, grid=(m // tm, n // tn, nk), in_specs=in_specs, out_specs=o_spec,
        out_shape=jax.ShapeDtypeStruct((m, n), out_dtype),
        scratch_shapes=[pltpu.VMEM((tm, tn), f32)] if nk > 1 else [],
        compiler_params=_params(("parallel", "parallel", "arbitrary")),
    )(*args)


def _rms_fwd(x, w, name):
    s = x.shape[0]
    tr = min(512, s)

    def body(x_ref, w_ref, h_ref):
        xv = x_ref[...]
        r = lax.rsqrt(_rowmean(xv * xv) + EPS)
        h_ref[...] = (xv * r * w_ref[...]).astype(bf16)

    return pl.pallas_call(
        body, name=name, grid=(s // tr,),
        in_specs=[pl.BlockSpec((tr, D_MODEL), lambda i: (i, 0)), pl.BlockSpec((1, D_MODEL), lambda i: (0, 0))],
        out_specs=pl.BlockSpec((tr, D_MODEL), lambda i: (i, 0)),
        out_shape=jax.ShapeDtypeStruct((s, D_MODEL), bf16), compiler_params=_params(("parallel",)),
    )(x, w)


def _rms_bwd(dh, x, w, dres, name):
    s = x.shape[0]
    tr = min(512, s)

    def body(dh_ref, x_ref, w_ref, dres_ref, dx_ref, dw_ref):
        xv = x_ref[...]
        r = lax.rsqrt(_rowmean(xv * xv) + EPS)
        xn = xv * r
        d = dh_ref[...]
        dxn = d * w_ref[...]
        dx_ref[...] = dres_ref[...] + r * (dxn - xn * _rowmean(dxn * xn))
        part = _sum0(d * xn)

        @pl.when(pl.program_id(0) == 0)
        def _():
            dw_ref[...] = part

        @pl.when(pl.program_id(0) > 0)
        def _():
            dw_ref[...] += part

    row = pl.BlockSpec((tr, D_MODEL), lambda i: (i, 0))
    vec = pl.BlockSpec((1, D_MODEL), lambda i: (0, 0))
    return pl.pallas_call(
        body, name=name, grid=(s // tr,), in_specs=[row, row, vec, row], out_specs=[row, vec],
        out_shape=[jax.ShapeDtypeStruct((s, D_MODEL), f32), jax.ShapeDtypeStruct((1, D_MODEL), f32)],
        compiler_params=_params(("arbitrary",)),
    )(dh, x, w, dres)


def _final_loss(x, w, target, name):
    s = x.shape[0]
    tr = min(512, s)

    def body(x_ref, w_ref, t_ref, dx_ref, dw_ref, loss_ref):
        xv = x_ref[...]
        wv = w_ref[...]
        r = lax.rsqrt(_rowmean(xv * xv) + EPS)
        xn = xv * r
        e = xn * wv - t_ref[...]
        lb = jnp.broadcast_to(0.5 * _sum0(_rowmean(e * e)), (1, HD))
        dy = e * (1.0 / D_MODEL)
        dxn = dy * wv
        dx_ref[...] = r * (dxn - xn * _rowmean(dxn * xn))
        part = _sum0(dy * xn)

        @pl.when(pl.program_id(0) == 0)
        def _():
            dw_ref[...] = part
            loss_ref[...] = lb

        @pl.when(pl.program_id(0) > 0)
        def _():
            dw_ref[...] += part
            loss_ref[...] += lb

    row = pl.BlockSpec((tr, D_MODEL), lambda i: (i, 0))
    vec = pl.BlockSpec((1, D_MODEL), lambda i: (0, 0))
    lsp = pl.BlockSpec((1, HD), lambda i: (0, 0))
    return pl.pallas_call(
        body, name=name, grid=(s // tr,), in_specs=[row, vec, row], out_specs=[row, vec, lsp],
        out_shape=[jax.ShapeDtypeStruct((s, D_MODEL), f32), jax.ShapeDtypeStruct((1, D_MODEL), f32),
                   jax.ShapeDtypeStruct((1, HD), f32)],
        compiler_params=_params(("arbitrary",)),
    )(x, w, target)


LX = slice(0, HD)
LZ = slice(HD, 2 * HD)


def _lru_gates(xc, vec, wa, wx):
    sp = _softplus(-vec[3:4])
    xcb = _b(xc)
    r = _sig(_dot(xcb, wa) + vec[1:2])
    i = _sig(_dot(xcb, wx) + vec[2:3])
    la = (-LRU_C) * r * sp
    a = jnp.exp(la)
    mult = jnp.sqrt(-_expm1(2.0 * la))
    return r, i, a, mult, sp, xcb


def _lru_fwd(proj, cw, vec, wa, wx, name):
    s = proj.shape[0]
    sb = min(SEQ_BLOCK, s)
    nsb = s // sb
    t = min(LRU_T, sb)
    nc = sb // t

    def body(p_ref, ph_ref, cw_ref, vec_ref, wa_ref, wx_ref, y_ref, hl_ref, hc_ref):
        sblk = pl.program_id(1)
        cwv = cw_ref[...]
        vec_v = vec_ref[...]
        wa_v = _b(wa_ref[...])
        wx_v = _b(wx_ref[...])
        row = lax.broadcasted_iota(jnp.int32, (t, HD), 0)

        @pl.when(sblk == 0)
        def _():
            hc_ref[...] = jnp.zeros((8, HD), f32)

        def chunk(c, hcarry):
            t0 = pl.multiple_of(c * t, t)
            xs = _conv_taps(p_ref[pl.ds(t0, t), LX], _rows_before(p_ref, ph_ref, LX, t0, c, sblk))
            xc = vec_v[0:1] + _conv(xs, cwv)
            r, i, a, mult, _, _ = _lru_gates(xc, vec_v, wa_v, wx_v)
            bb = mult * (i * xc)
            d = 1
            while d < t:
                m = row >= d
                bb = jnp.where(m, a * pltpu.roll(bb, d, axis=0) + bb, bb)
                a = jnp.where(m, a * pltpu.roll(a, d, axis=0), a)
                d *= 2
            hl = bb + a * hcarry
            hl_ref[pl.ds(t0, t), :] = hl
            z = p_ref[pl.ds(t0, t), LZ]
            nrm = lax.rsqrt(_rowmean(hl * hl) + EPS)
            y_ref[pl.ds(t0, t), :] = (hl * nrm * vec_v[4:5] * (z * _sig(z))).astype(bf16)
            return hl[t - 1:t, :]

        hlast = lax.fori_loop(0, nc, chunk, hc_ref[0:1, :])
        hc_ref[...] = jnp.broadcast_to(hlast, (8, HD))

    main, halo = _seq_specs(sb, nsb, LRU_W, lambda h: h, False)
    hcol = lambda h, i: (0, h)
    wsp = pl.BlockSpec((None, HD, HD), lambda h, i: (h, 0, 0))
    osp = pl.BlockSpec((sb, HD), lambda h, i: (i, h))
    return pl.pallas_call(
        body, name=name, grid=(HEADS, nsb),
        in_specs=[main, halo, pl.BlockSpec((4, HD), hcol), pl.BlockSpec((8, HD), hcol), wsp, wsp],
        out_specs=[osp, osp],
        out_shape=[jax.ShapeDtypeStruct((s, D_MODEL), bf16), jax.ShapeDtypeStruct((s, D_MODEL), f32)],
        scratch_shapes=[pltpu.VMEM((8, HD), f32)],
        compiler_params=_params(("parallel", "arbitrary")),
    )(proj, proj, cw, vec, wa, wx)


def _lru_bwd(dy, proj, hl, cw, vec, wa, wx, name):
    s = proj.shape[0]
    sb = min(SEQ_BLOCK, s)
    nsb = s // sb
    t = min(LRU_T, sb)
    nc = sb // t

    def body(dy_ref, p_ref, ph_ref, hl_ref, hlh_ref, cw_ref, vec_ref, wa_ref, wx_ref,
             dp_ref, dcw_ref, dvec_ref, dwa_ref, dwx_ref, acc_ref, dxh_ref):
        step = pl.program_id(1)
        sblk = nsb - 1 - step
        cwv = cw_ref[...]
        vec_v = vec_ref[...]
        wa_v = _b(wa_ref[...])
        wx_v = _b(wx_ref[...])
        lnw = vec_v[4:5]
        row = lax.broadcasted_iota(jnp.int32, (t, HD), 0)

        @pl.when(step == 0)
        def _():
            acc_ref[...] = jnp.zeros((16, HD), f32)
            dxh_ref[...] = jnp.zeros((8, HD), f32)
            dwa_ref[...] = jnp.zeros((HD, HD), f32)
            dwx_ref[...] = jnp.zeros((HD, HD), f32)

        def chunk(ci, carry):
            mu, dxc_halo, dcw0, dcw1, dcw2, dcw3, dcb, dba, dbx, dsp, dlnw = carry
            c = nc - 1 - ci
            t0 = pl.multiple_of(c * t, t)
            xs = _conv_taps(p_ref[pl.ds(t0, t), LX], _rows_before(p_ref, ph_ref, LX, t0, c, sblk))
            xc = vec_v[0:1] + _conv(xs, cwv)
            r, i, a, mult, sp, xcb = _lru_gates(xc, vec_v, wa_v, wx_v)
            hv = hl_ref[pl.ds(t0, t), :]
            hhalo = _rows_before(hl_ref, hlh_ref, slice(None), t0, c, sblk)
            hprev = pltpu.roll(jnp.concatenate([hhalo, hv], axis=0), 1, axis=0)[8:]
            z = p_ref[pl.ds(t0, t), LZ]
            sgz = _sig(z)
            sz = z * sgz
            dyv = dy_ref[pl.ds(t0, t), :]
            nrm = lax.rsqrt(_rowmean(hv * hv) + EPS)
            hn = hv * nrm
            dlnw = dlnw + _sum0(dyv * hn * sz)
            dp_ref[pl.ds(t0, t), LZ] = dyv * hn * lnw * _dsilu(z, sgz)
            dhn = dyv * lnw * sz
            lam = nrm * (dhn - hn * _rowmean(dhn * hn))
            cf = jnp.where(row == t - 1, 1.0, pltpu.roll(a, t - 1, axis=0))
            d = 1
            while d < t:
                m = row < t - d
                lam = jnp.where(m, lam + cf * pltpu.roll(lam, t - d, axis=0), lam)
                cf = jnp.where(m, cf * pltpu.roll(cf, t - d, axis=0), cf)
                d *= 2
            lam = lam + cf * mu
            mu = a[0:1] * lam[0:1]
            da = lam * hprev
            dmult = lam * (i * xc)
            di = lam * mult * xc
            dxc = lam * mult * i
            dla = da * a - dmult * (a * a) / mult
            dr = dla * (-LRU_C * sp)
            dsp = dsp + _sum0(dla * (-LRU_C * r))
            dra = dr * r * (1.0 - r)
            dia = di * i * (1.0 - i)
            dba = dba + _sum0(dra)
            dbx = dbx + _sum0(dia)
            drab, diab = _b(dra), _b(dia)
            dwa_ref[...] += _dot_tn(xcb, drab)
            dwx_ref[...] += _dot_tn(xcb, diab)
            dxc = dxc + _dot_nt(drab, wa_v) + _dot_nt(diab, wx_v)
            dcb = dcb + _sum0(dxc)
            dcw0 = dcw0 + _sum0(dxc * xs[0])
            dcw1 = dcw1 + _sum0(dxc * xs[1])
            dcw2 = dcw2 + _sum0(dxc * xs[2])
            dcw3 = dcw3 + _sum0(dxc * xs[3])
            dp_ref[pl.ds(t0, t), LX] = _conv_back(dxc, dxc_halo, cwv)
            return (mu, dxc[0:8], dcw0, dcw1, dcw2, dcw3, dcb, dba, dbx, dsp, dlnw)

        acc = acc_ref[...]
        init = (acc[9:10], dxh_ref[...]) + tuple(acc[k:k + 1] for k in range(9))
        mu, dxh, dcw0, dcw1, dcw2, dcw3, dcb, dba, dbx, dsp, dlnw = lax.fori_loop(0, nc, chunk, init)
        zero = jnp.zeros((1, HD), f32)
        acc_ref[...] = jnp.concatenate([dcw0, dcw1, dcw2, dcw3, dcb, dba, dbx, dsp, dlnw, mu] + [zero] * 6, axis=0)
        dxh_ref[...] = dxh

        @pl.when(step == nsb - 1)
        def _():
            dcw_ref[...] = jnp.concatenate([dcw0, dcw1, dcw2, dcw3], axis=0)
            dlam = -dsp * _sig(-vec_v[3:4])
            dvec_ref[...] = jnp.concatenate([dcb, dba, dbx, dlam, dlnw, zero, zero, zero], axis=0)

    main, halo = _seq_specs(sb, nsb, LRU_W, lambda h: h, True)
    hmain, hhalo = _seq_specs(sb, nsb, HD, lambda h: h, True)
    hcol = lambda h, i: (0, h)
    wsp = pl.BlockSpec((None, HD, HD), lambda h, i: (h, 0, 0))
    return pl.pallas_call(
        body, name=name, grid=(HEADS, nsb),
        in_specs=[hmain, main, halo, hmain, hhalo, pl.BlockSpec((4, HD), hcol), pl.BlockSpec((8, HD), hcol), wsp, wsp],
        out_specs=[main, pl.BlockSpec((4, HD), hcol), pl.BlockSpec((8, HD), hcol), wsp, wsp],
        out_shape=[jax.ShapeDtypeStruct((s, D_MAIN), f32), jax.ShapeDtypeStruct((4, D_MODEL), f32),
                   jax.ShapeDtypeStruct((8, D_MODEL), f32),
                   jax.ShapeDtypeStruct((HEADS, HD, HD), f32), jax.ShapeDtypeStruct((HEADS, HD, HD), f32)],
        scratch_shapes=[pltpu.VMEM((16, HD), f32), pltpu.VMEM((8, HD), f32)],
        compiler_params=_params(("parallel", "arbitrary")),
    )(dy, proj, proj, hl, hl, cw, vec, wa, wx)


QKV = slice(0, 3 * HD)
DZ = slice(3 * HD, 4 * HD)


def _lane_pick(x, lane, idx):
    return _sum1(jnp.where(lane == idx, x, 0.0))


def _inv_unit_lower(a, eye):
    x = eye - a
    p = _dot(a, a, HI)
    for k in range(5):
        x = x + _dot(x, p, HI)
        if k < 4:
            p = _dot(p, p, HI)
    return x


def _dn_chunk(p_ref, ph_ref, ba_ref, cw3, dnv, h, t0, c, sblk, masks):
    lane, ri, ci, eye, ltri = masks
    xs = _conv_taps(p_ref[pl.ds(t0, CHUNK), QKV], _rows_before(p_ref, ph_ref, QKV, t0, c, sblk))
    cpre = _conv(xs, cw3)
    sg = _sig(cpre)
    act = cpre * sg
    q_raw, k_raw, v = act[:, 0:HD], act[:, HD:2 * HD], act[:, 2 * HD:3 * HD]
    rq = lax.rsqrt(_sum1(q_raw * q_raw) + EPS)
    rk = lax.rsqrt(_sum1(k_raw * k_raw) + EPS)
    qn = q_raw * rq
    k = k_raw * rk
    q = qn * QSCALE
    bav = ba_ref[pl.ds(t0, CHUNK), :]
    b_in = jnp.broadcast_to(_lane_pick(bav, lane, h), (CHUNK, HD))
    a_in = jnp.broadcast_to(_lane_pick(bav, lane, HEADS + h), (CHUNK, HD))
    lane1 = lane[0:1]
    ea = jnp.exp(_lane_pick(dnv[0:1], lane1, h))
    dt = _lane_pick(dnv[1:2], lane1, h)
    beta = _sig(b_in)
    sp = _softplus(a_in + dt)
    g = -ea * sp
    gc = _dot(ltri, g, HI)
    gc64 = gc[:, 0:CHUNK]
    grow = _sum0(gc64 * eye)
    dm = jnp.exp(jnp.where(ri >= ci, gc64 - grow, -1e30))
    ds_ = jnp.where(ri > ci, dm, 0.0)
    eg = jnp.exp(gc)
    glast = gc[CHUNK - 1:CHUNK, :]
    ek = jnp.exp(glast - gc)
    kb = k * beta
    kbf = _b(k)
    amat = _dot_nt(_b(kb), kbf) * ds_
    tinv = _inv_unit_lower(amat, eye)
    pmat = _dot_nt(_b(q), kbf) * dm
    return dict(xs=xs, cpre=cpre, sg=sg, rq=rq, rk=rk, qn=qn, q=q, k=k, v=v, kbf=kbf, b_in=b_in, a_in=a_in, ea=ea,
                dt=dt, beta=beta, g=g, gc=gc, dm=dm, ds=ds_, eg=eg, glast=glast, ek=ek, kb=kb, amat=amat,
                tinv=tinv, pmat=pmat)


def _dn_masks():
    lane = lax.broadcasted_iota(jnp.int32, (CHUNK, HD), 1)
    ri = lax.broadcasted_iota(jnp.int32, (CHUNK, CHUNK), 0)
    ci = lax.broadcasted_iota(jnp.int32, (CHUNK, CHUNK), 1)
    eye = (ri == ci).astype(f32)
    ltri = (ri >= ci).astype(f32)
    return lane, ri, ci, eye, ltri


def _dn_fwd(proj, cw, dnv, name):
    s = proj.shape[0]
    sb = min(SEQ_BLOCK, s)
    nsb = s // sb
    nc = sb // CHUNK

    def body(p_ref, ph_ref, ba_ref, cwq_ref, cwk_ref, cwv_ref, dnv_ref, y_ref, o_ref, st_ref, s_ref):
        h = pl.program_id(0)
        sblk = pl.program_id(1)
        masks = _dn_masks()
        cw3 = jnp.concatenate([cwq_ref[...], cwk_ref[...], cwv_ref[...]], axis=1)
        dnv_v = dnv_ref[...]

        @pl.when(sblk == 0)
        def _():
            s_ref[...] = jnp.zeros((HD, HD), f32)

        def chunk(c, carry):
            t0 = pl.multiple_of(c * CHUNK, CHUNK)
            f = _dn_chunk(p_ref, ph_ref, ba_ref, cw3, dnv_v, h, t0, c, sblk, masks)
            st = s_ref[...]
            st_ref[c] = st
            sbf = _b(st)
            rhs = f["beta"] * (f["v"] - _dot(_b(f["k"] * f["eg"]), sbf))
            vn = _dot(f["tinv"], rhs, HI)
            vnb = _b(vn)
            o = _dot(_b(f["q"] * f["eg"]), sbf) + _dot(_b(f["pmat"]), vnb)
            s_ref[...] = st * jnp.exp(f["glast"]) + _dot_tn(_b(f["k"] * f["ek"]), vnb)
            o_ref[pl.ds(t0, CHUNK), :] = o
            z = p_ref[pl.ds(t0, CHUNK), DZ]
            nrm = lax.rsqrt(_rowmean(o * o) + EPS)
            y_ref[pl.ds(t0, CHUNK), :] = (o * nrm * dnv_v[2:3] * (z * _sig(z))).astype(bf16)
            return carry

        lax.fori_loop(0, nc, chunk, 0)

    main, halo = _seq_specs(sb, nsb, DN_W, lambda h: DN_BLK0 + h, False)
    basp = pl.BlockSpec((sb, HD), lambda h, i: (i, BA_BLK))
    cws = lambda off: pl.BlockSpec((4, HD), lambda h, i: (0, off + h))
    osp = pl.BlockSpec((sb, HD), lambda h, i: (i, h))
    return pl.pallas_call(
        body, name=name, grid=(HEADS, nsb),
        in_specs=[main, halo, basp, cws(0), cws(HEADS), cws(2 * HEADS), pl.BlockSpec((8, HD), lambda h, i: (0, 0))],
        out_specs=[osp, osp, pl.BlockSpec((None, nc, HD, HD), lambda h, i: (h, i, 0, 0))],
        out_shape=[jax.ShapeDtypeStruct((s, D_MODEL), bf16), jax.ShapeDtypeStruct((s, D_MODEL), f32),
                   jax.ShapeDtypeStruct((HEADS, s // CHUNK, HD, HD), f32)],
        scratch_shapes=[pltpu.VMEM((HD, HD), f32)],
        compiler_params=_params(("parallel", "arbitrary")),
    )(proj, proj, proj, cw, cw, cw, dnv)


def _dn_bwd(dy, proj, o, states, cw, dnv, dproj, name):
    s = proj.shape[0]
    sb = min(SEQ_BLOCK, s)
    nsb = s // sb
    nc = sb // CHUNK

    def body(dy_ref, p_ref, ph_ref, ba_ref, o_ref, st_ref, cwq_ref, cwk_ref, cwv_ref, dnv_ref, dpin_ref,
             dp_ref, dba_ref, dcw_ref, dsc_ref, ds_ref, dch_ref, cwacc_ref, sacc_ref):
        del dpin_ref
        h = pl.program_id(0)
        step = pl.program_id(1)
        sblk = nsb - 1 - step
        masks = _dn_masks()
        lane, ri, ci, eye, ltri = masks
        utri = (ri <= ci).astype(f32)
        cw3 = jnp.concatenate([cwq_ref[...], cwk_ref[...], cwv_ref[...]], axis=1)
        dnv_v = dnv_ref[...]
        dnw = dnv_v[2:3]
        row64 = lax.broadcasted_iota(jnp.int32, (CHUNK, HD), 0)

        @pl.when(step == 0)
        def _():
            ds_ref[...] = jnp.zeros((HD, HD), f32)
            dch_ref[...] = jnp.zeros((8, 3 * HD), f32)
            cwacc_ref[...] = jnp.zeros((8, 3 * HD), f32)
            sacc_ref[0:2, :] = jnp.zeros((2, HD), f32)

        @pl.when((step == 0) & (h == 0))
        def _():
            sacc_ref[2:8, :] = jnp.zeros((6, HD), f32)

        def chunk(cidx, carry):
            c = nc - 1 - cidx
            t0 = pl.multiple_of(c * CHUNK, CHUNK)
            f = _dn_chunk(p_ref, ph_ref, ba_ref, cw3, dnv_v, h, t0, c, sblk, masks)
            q, k, v, beta, eg, ek, kbf = f["q"], f["k"], f["v"], f["beta"], f["eg"], f["ek"], f["kbf"]
            st = st_ref[c]
            sbf = _b(st)
            dst = ds_ref[...]
            dstb = _b(dst)
            qe = q * eg
            keg = k * eg
            kd = k * ek
            kegb, qeb, kdb = _b(keg), _b(qe), _b(kd)
            e = v - _dot(kegb, sbf)
            rhs = beta * e
            vn = _dot(f["tinv"], rhs, HI)
            vnb = _b(vn)
            pb = _b(f["pmat"])
            ov = o_ref[pl.ds(t0, CHUNK), :]
            z = p_ref[pl.ds(t0, CHUNK), DZ]
            sgz = _sig(z)
            sz = z * sgz
            dyv = dy_ref[pl.ds(t0, CHUNK), :]
            nrm = lax.rsqrt(_rowmean(ov * ov) + EPS)
            on = ov * nrm
            ddnw = _sum0(dyv * on * sz)
            dp_ref[pl.ds(t0, CHUNK), DZ] = dyv * on * dnw * _dsilu(z, sgz)
            don = dyv * dnw * sz
            do = nrm * (don - on * _rowmean(don * on))
            dob = _b(do)
            dvn = _dot_tn(pb, dob) + _dot(kdb, dstb)
            dpm = jnp.where(ri >= ci, _dot_nt(dob, vnb), 0.0)
            dqe = _dot_nt(dob, sbf)
            dkd = _dot_nt(vnb, dstb)
            drhs = _dot_tn(f["tinv"], dvn, HI)
            dam = jnp.where(ri > ci, -_dot_nt(drhs, vn, HI), 0.0)
            de = drhs * beta
            deb = _b(de)
            dbeta = _sum1(drhs * e)
            dkeg = -_dot_nt(deb, sbf)
            ds_ref[...] = _dot_tn(qeb, dob) + dst * jnp.exp(f["glast"]) - _dot_tn(kegb, deb)
            dqk = _b(dpm * f["dm"])
            dq = _dot(dqk, kbf) + dqe * eg
            dkk = _b(dam * f["ds"])
            dkb = _dot(dkk, kbf)
            dk = _dot_tn(dqk, _b(q)) + _dot_tn(dkk, _b(f["kb"])) + dkb * beta + dkeg * eg + dkd * ek
            dbeta = dbeta + _sum1(dkb * k)
            mm = dpm * f["pmat"] + dam * f["amat"]
            dglast = _sum1(_sum0(dst * st)) * jnp.exp(f["glast"]) + _sum1(_sum0(dkd * kd))
            dgc = (_sum1(mm) - _sum1(eye * _sum0(mm)) + _sum1(dqe * qe) + _sum1(dkeg * keg) - _sum1(dkd * kd))
            dgc = jnp.broadcast_to(dgc, (CHUNK, HD)) + jnp.where(row64 == CHUNK - 1, dglast, 0.0)
            dg = _dot(utri, dgc, HI)
            dbin = jnp.broadcast_to(dbeta, (CHUNK, HD)) * beta * (1.0 - beta)
            dain = dg * (-f["ea"]) * _sig(f["a_in"] + f["dt"])
            dba_ref[pl.ds(t0, CHUNK), :] = jnp.where(lane == 0, dbin, 0.0) + jnp.where(lane == 1, dain, 0.0)
            sacc_ref[0:1, :] += _sum0(dg * f["g"])
            sacc_ref[1:2, :] += _sum0(dain)
            sacc_ref[2:3, :] += ddnw
            dq_raw = (QSCALE * f["rq"]) * (dq - f["qn"] * _sum1(f["qn"] * dq))
            dk_raw = f["rk"] * (dk - k * _sum1(k * dk))
            dact = jnp.concatenate([dq_raw, dk_raw, de], axis=1)
            dc = dact * _dsilu(f["cpre"], f["sg"])
            xs = f["xs"]
            cwacc_ref[0:4, :] += jnp.concatenate([_sum0(dc * xs[0]), _sum0(dc * xs[1]), _sum0(dc * xs[2]),
                                                  _sum0(dc * xs[3])], axis=0)
            dp_ref[pl.ds(t0, CHUNK), QKV] = _conv_back(dc, dch_ref[...], cw3)
            dch_ref[...] = dc[0:8]
            return carry

        lax.fori_loop(0, nc, chunk, 0)

        @pl.when(step == nsb - 1)
        def _():
            dcw_ref[...] = cwacc_ref[0:4, :]
            lane8 = lax.broadcasted_iota(jnp.int32, (8, HD), 1)
            row8 = lax.broadcasted_iota(jnp.int32, (8, HD), 0)
            sa = sacc_ref[...]
            mine = jnp.where((lane8 == h) & (row8 < 2), sa, 0.0)

            @pl.when(h == 0)
            def _():
                dsc_ref[...] = mine

            @pl.when(h > 0)
            def _():
                dsc_ref[...] += mine

            @pl.when(h == HEADS - 1)
            def _():
                dsc_ref[2:3, :] = sa[2:3]

    main, halo = _seq_specs(sb, nsb, DN_W, lambda h: DN_BLK0 + h, True)
    rblk = lambda i: nsb - 1 - i
    basp = pl.BlockSpec((sb, HD), lambda h, i: (rblk(i), BA_BLK))
    cws = lambda off: pl.BlockSpec((4, HD), lambda h, i: (0, off + h))
    return pl.pallas_call(
        body, name=name, grid=(HEADS, nsb),
        in_specs=[pl.BlockSpec((sb, HD), lambda h, i: (rblk(i), HEADS + h)), main, halo, basp,
                  pl.BlockSpec((sb, HD), lambda h, i: (rblk(i), h)),
                  pl.BlockSpec((None, nc, HD, HD), lambda h, i: (h, rblk(i), 0, 0)),
                  cws(0), cws(HEADS), cws(2 * HEADS), pl.BlockSpec((8, HD), lambda h, i: (0, 0)),
                  pl.BlockSpec(memory_space=pl.ANY)],
        out_specs=[main, pl.BlockSpec((sb, HD), lambda h, i: (rblk(i), h)),
                   pl.BlockSpec((None, 4, 3 * HD), lambda h, i: (h, 0, 0)), pl.BlockSpec((8, HD), lambda h, i: (0, 0))],
        out_shape=[jax.ShapeDtypeStruct((s, D_MAIN), f32), jax.ShapeDtypeStruct((s, D_MODEL), f32),
                   jax.ShapeDtypeStruct((HEADS, 4, 3 * HD), f32), jax.ShapeDtypeStruct((8, HD), f32)],
        scratch_shapes=[pltpu.VMEM((HD, HD), f32), pltpu.VMEM((8, 3 * HD), f32), pltpu.VMEM((8, 3 * HD), f32),
                        pltpu.VMEM((8, HD), f32)],
        input_output_aliases={10: 0},
        compiler_params=_params(("arbitrary", "arbitrary")),
    )(dy, proj, proj, proj, o, states, cw, cw, cw, dnv, dproj)


ANY = pl.BlockSpec(memory_space=pl.ANY)
CHIP_MASKS = ((1, 0, 0), (0, 1, 0), (1, 1, 0))
ALL_MASKS = tuple((m >> 2 & 1, m >> 1 & 1, m & 1) for m in range(1, N_DEV))


def _me():
    return lax.axis_index("x"), lax.axis_index("y"), lax.axis_index("c")


def _flip(me, mask):
    return tuple((1 - v) if m else v for v, m in zip(me, mask))


def _dev_index(dev):
    return 4 * dev[0] + 2 * dev[1] + dev[2]


def _exchange(name, arrays, out_shapes, masks, src_of, dst_of, own_of):
    na, nm = len(arrays), len(masks)

    def body(*refs):
        srcs, dsts = refs[:na], refs[na:2 * na]
        send_sems, recv_sems, loc_sems = refs[2 * na:]
        me = _me()
        local = []
        for a in range(na):
            own = own_of(a, srcs[a], dsts[a], me)
            if own is not None:
                cp = pltpu.make_async_copy(own[0], own[1], loc_sems.at[a])
                cp.start()
                local.append(cp)
        sends = []
        for a in range(na):
            for mi, mask in enumerate(masks):
                peer = _flip(me, mask)
                cp = pltpu.make_async_remote_copy(
                    src_ref=src_of(a, srcs[a], me, peer), dst_ref=dst_of(a, dsts[a], me),
                    send_sem=send_sems.at[a * nm + mi], recv_sem=recv_sems.at[a * nm + mi],
                    device_id=peer, device_id_type=MESH)
                cp.start()
                sends.append(cp)
        for a in range(na):
            for mi, mask in enumerate(masks):
                peer = _flip(me, mask)
                pltpu.make_async_remote_copy(
                    src_ref=src_of(a, srcs[a], peer, me), dst_ref=dst_of(a, dsts[a], peer),
                    send_sem=send_sems.at[a * nm + mi], recv_sem=recv_sems.at[a * nm + mi],
                    device_id=peer, device_id_type=MESH).wait_recv()
        for cp in sends:
            cp.wait_send()
        for cp in local:
            cp.wait()

    return pl.pallas_call(
        body, name=name, in_specs=[ANY] * na, out_specs=[ANY] * na,
        out_shape=[jax.ShapeDtypeStruct(shp, arr.dtype) for shp, arr in zip(out_shapes, arrays)],
        scratch_shapes=[pltpu.SemaphoreType.DMA((na * nm,)), pltpu.SemaphoreType.DMA((na * nm,)),
                        pltpu.SemaphoreType.DMA((na,))],
        compiler_params=pltpu.CompilerParams(has_side_effects=True),
    )(*arrays)


def _gather_over_chips(arrays, name):
    chip = lambda dev: 2 * dev[0] + dev[1]
    return _exchange(
        name, arrays, [(4,) + a.shape for a in arrays], CHIP_MASKS,
        src_of=lambda a, ref, me, peer: ref,
        dst_of=lambda a, ref, sender: ref.at[chip(sender)],
        own_of=lambda a, src, dst, me: (src, dst.at[chip(me)]))


def _scatter_parts(arrays, name):
    return _exchange(
        name, arrays, [a.shape for a in arrays], ALL_MASKS,
        src_of=lambda a, ref, me, peer: ref.at[_dev_index(peer)],
        dst_of=lambda a, ref, sender: ref.at[_dev_index(sender)],
        own_of=lambda a, src, dst, me: (src.at[_dev_index(me)], dst.at[_dev_index(me)]))


def _share_reduced(small, pair_arrays, name):
    arrays = [small] + list(pair_arrays)
    na = len(arrays)
    nm = len(ALL_MASKS)

    def body(*refs):
        srcs, dsts = refs[:na], refs[na:2 * na]
        send_sems, recv_sems, loc_sems = refs[2 * na:]
        me = _me()
        sib = _flip(me, (0, 0, 1))
        local = []
        for a in range(na):
            slot = _dev_index(me) if a == 0 else me[2]
            cp = pltpu.make_async_copy(srcs[a], dsts[a].at[slot], loc_sems.at[a])
            cp.start()
            local.append(cp)

        def copy(a, k, sender, to):
            slot = _dev_index(sender) if a == 0 else sender[2]
            return pltpu.make_async_remote_copy(
                src_ref=srcs[a], dst_ref=dsts[a].at[slot], send_sem=send_sems.at[k], recv_sem=recv_sems.at[k],
                device_id=to, device_id_type=MESH)

        sends = [copy(0, mi, me, _flip(me, mask)) for mi, mask in enumerate(ALL_MASKS)]
        sends += [copy(a, nm + a - 1, me, sib) for a in range(1, na)]
        for cp in sends:
            cp.start()
        for mi, mask in enumerate(ALL_MASKS):
            copy(0, mi, _flip(me, mask), me).wait_recv()
        for a in range(1, na):
            copy(a, nm + a - 1, sib, me).wait_recv()
        for cp in sends:
            cp.wait_send()
        for cp in local:
            cp.wait()

    nsem = nm + na - 1
    return pl.pallas_call(
        body, name=name, in_specs=[ANY] * na, out_specs=[ANY] * na,
        out_shape=[jax.ShapeDtypeStruct((N_DEV,) + small.shape, small.dtype)]
        + [jax.ShapeDtypeStruct((2,) + a.shape, a.dtype) for a in pair_arrays],
        scratch_shapes=[pltpu.SemaphoreType.DMA((nsem,)), pltpu.SemaphoreType.DMA((nsem,)),
                        pltpu.SemaphoreType.DMA((na,))],
        compiler_params=pltpu.CompilerParams(has_side_effects=True),
    )(*arrays)


def _sum_parts(parts, name):
    _, r, c = parts.shape
    tr = r
    while tr * c * 4 * N_DEV > (8 << 20) and tr % 16 == 0:
        tr //= 2

    def body(p_ref, o_ref):
        total = p_ref[0]
        for d in range(1, N_DEV):
            total = total + p_ref[d]
        o_ref[...] = total

    return pl.pallas_call(
        body, name=name, grid=(r // tr,), in_specs=[pl.BlockSpec((N_DEV, tr, c), lambda i: (0, i, 0))],
        out_specs=pl.BlockSpec((tr, c), lambda i: (i, 0)), out_shape=jax.ShapeDtypeStruct((r, c), f32),
        compiler_params=_params(("parallel",)),
    )(parts)


def _adamw(w, g, m, v, name):
    shape = w.shape
    cols = shape[-1]
    rows = w.size // cols
    tr = rows
    while tr * cols * 4 > (1 << 20) and tr % 16 == 0:
        tr //= 2
    c1 = 1.0 / (1.0 - ADAM_B1 ** ADAM_STEP)
    c2 = 1.0 / (1.0 - ADAM_B2 ** ADAM_STEP)

    def body(w_ref, g_ref, m_ref, v_ref, d_ref, nm_ref, nv_ref):
        gv = g_ref[...]
        mv = ADAM_B1 * m_ref[...] + (1.0 - ADAM_B1) * gv
        vv = ADAM_B2 * v_ref[...] + (1.0 - ADAM_B2) * (gv * gv)
        nm_ref[...] = mv
        nv_ref[...] = vv
        d_ref[...] = -ADAM_LR * ((mv * c1) / (jnp.sqrt(vv * c2) + ADAM_EPS) + ADAM_WD * w_ref[...])

    spec = pl.BlockSpec((tr, cols), lambda i: (i, 0))
    outs = pl.pallas_call(
        body, name=name, grid=(rows // tr,), in_specs=[spec] * 4, out_specs=[spec] * 3,
        out_shape=[jax.ShapeDtypeStruct((rows, cols), f32)] * 3, compiler_params=_params(("parallel",)),
    )(*[t.reshape(rows, cols) for t in (w, g, m, v)])
    return tuple(t.reshape(shape) for t in outs)


def _to_heads(w, parts):
    r = w.shape[0]
    return w.reshape(r, parts, HEADS, HD).transpose(0, 2, 1, 3).reshape(r, parts * HEADS * HD)


def _from_heads(w, parts):
    r = w.shape[0]
    return w.reshape(r, HEADS, parts, HD).transpose(0, 2, 1, 3).reshape(r, parts * HEADS * HD)


def _pad_lanes(t, n=HD):
    return jnp.pad(t, [(0, 0)] * (t.ndim - 1) + [(0, n - t.shape[-1])])


def _rows128(t):
    return t.reshape(-1, HD)


def kernel(x, norm_w, w_in, lru_conv_w, lru_conv_b, lru_wa, lru_ba, lru_wx, lru_bx, lru_lambda, lru_norm_w, dn_conv_w, dn_A_log, dn_dt_bias, dn_norm_w, w_out, final_norm_w, loss_target, m_norm_w, m_w_in, m_lru_conv_w, m_lru_conv_b, m_lru_wa, m_lru_ba, m_lru_wx, m_lru_bx, m_lru_lambda, m_lru_norm_w, m_dn_conv_w, m_dn_A_log, m_dn_dt_bias, m_dn_norm_w, m_w_out, m_final_norm_w, v_norm_w, v_w_in, v_lru_conv_w, v_lru_conv_b, v_lru_wa, v_lru_ba, v_lru_wx, v_lru_bx, v_lru_lambda, v_lru_norm_w, v_dn_conv_w, v_dn_A_log, v_dn_dt_bias, v_dn_norm_w, v_w_out, v_final_norm_w):
    depth = norm_w.shape[0]
    xs = x[0]
    target = loss_target[0]
    chip = 2 * lax.axis_index("x") + lax.axis_index("y")

    g_in, g_out, g_lcw, g_dcw = _gather_over_chips(
        [w_in.astype(bf16), w_out.astype(bf16), lru_conv_w, dn_conv_w], "gather_weights")
    lcw_full = g_lcw.transpose(1, 2, 0, 3).reshape(depth, 4, D_MODEL)
    dcw_full = g_dcw.transpose(1, 2, 0, 3).reshape(depth, 4, 3 * D_MODEL)
    wp, wo, vecs, dnvs = [], [], [], []
    zrow = jnp.zeros((D_MODEL,), f32)
    for l in range(depth):
        wfull = jnp.concatenate([g_in[j, l] for j in range(4)], axis=1)
        wp.append(jnp.concatenate([_to_heads(wfull[:, :2 * D_MODEL], 2), _to_heads(wfull[:, 2 * D_MODEL:D_MAIN], 4),
                                   _pad_lanes(wfull[:, D_MAIN:])], axis=1))
        wo.append(g_out[:, l].reshape(D_MIX, D_MODEL))
        vecs.append(jnp.stack([lru_conv_b[l], lru_ba[l], lru_bx[l], lru_lambda[l], lru_norm_w[l], zrow, zrow, zrow]))
        dnvs.append(jnp.concatenate([_pad_lanes(dn_A_log[l][None]), _pad_lanes(dn_dt_bias[l][None]),
                                     dn_norm_w[l][None], jnp.zeros((5, HD), f32)], axis=0))

    loss_part, grad_x, d_fnw, g_nw, g_vec, g_wa, g_wx, g_lcw_, g_dcw_, g_dsc, g_win, g_wout = _local_step(
        xs, target, norm_w, final_norm_w, wp, wo, lcw_full, dcw_full, vecs, dnvs, lru_wa, lru_wx)
    loss = lax.psum(loss_part[0, 0], ("x", "y", "c"))
    grads = _reduce_grads(chip, d_fnw, g_nw, g_vec, g_wa, g_wx, g_lcw_, g_dcw_, g_dsc, g_win, g_wout)

    names = ["norm_w", "w_in", "lru_conv_w", "lru_conv_b", "lru_wa", "lru_ba", "lru_wx", "lru_bx", "lru_lambda",
             "lru_norm_w", "dn_conv_w", "dn_A_log", "dn_dt_bias", "dn_norm_w", "w_out", "final_norm_w"]
    ws = dict(zip(names, [norm_w, w_in, lru_conv_w, lru_conv_b, lru_wa, lru_ba, lru_wx, lru_bx, lru_lambda, lru_norm_w,
                          dn_conv_w, dn_A_log, dn_dt_bias, dn_norm_w, w_out, final_norm_w]))
    ms = dict(zip(names, [m_norm_w, m_w_in, m_lru_conv_w, m_lru_conv_b, m_lru_wa, m_lru_ba, m_lru_wx, m_lru_bx,
                          m_lru_lambda, m_lru_norm_w, m_dn_conv_w, m_dn_A_log, m_dn_dt_bias, m_dn_norm_w, m_w_out,
                          m_final_norm_w]))
    vs = dict(zip(names, [v_norm_w, v_w_in, v_lru_conv_w, v_lru_conv_b, v_lru_wa, v_lru_ba, v_lru_wx, v_lru_bx,
                          v_lru_lambda, v_lru_norm_w, v_dn_conv_w, v_dn_A_log, v_dn_dt_bias, v_dn_norm_w, v_w_out,
                          v_final_norm_w]))
    deltas, new_m, new_v = [], [], []
    for n in names:
        d, nm_, nv_ = _adamw(ws[n], grads[n], ms[n], vs[n], f"adamw_{n}")
        deltas.append(d)
        new_m.append(nm_)
        new_v.append(nv_)
    return (loss, grad_x[None], *[grads[n] for n in names], *deltas, *new_m, *new_v)


def _local_step(xs, target, norm_w, final_norm_w, wp, wo, lcw_full, dcw_full, vecs, dnvs, lru_wa, lru_wx):
    depth = norm_w.shape[0]
    saved = []
    h = xs
    for l in range(depth):
        hn = _rms_fwd(h, norm_w[l][None], f"rms_fwd{l}")
        proj = _matmul(hn, wp[l], tm=512, tn=896, tk=D_MODEL, name=f"in_proj{l}")
        y_lru, hl = _lru_fwd(proj, lcw_full[l], vecs[l], lru_wa[l], lru_wx[l], f"lru_fwd{l}")
        y_dn, o, states = _dn_fwd(proj, dcw_full[l], dnvs[l], f"dn_fwd{l}")
        y = jnp.concatenate([y_lru, y_dn], axis=1)
        out = _matmul(y, wo[l], tm=512, tn=D_MODEL, tk=D_MIX, res=h, name=f"out_proj{l}")
        saved.append((h, hn, proj, hl, o, states, y))
        h = out

    dh, d_fnw, loss_part = _final_loss(h, final_norm_w[None], target, "final_loss")

    g_nw, g_vec, g_wa, g_wx, g_lcw_, g_dcw_, g_dsc, g_win, g_wout = ([None] * depth for _ in range(9))
    for l in reversed(range(depth)):
        hin, hn, proj, hl, o, states, y = saved[l]
        dy = _matmul(dh, wo[l], tb=True, tm=512, tn=1024, tk=D_MODEL, name=f"d_y{l}")
        g_wout[l] = _matmul(y, dh, ta=True, tm=512, tn=D_MODEL, tk=512, name=f"d_wout{l}")
        dproj, g_lcw_[l], g_vec[l], g_wa[l], g_wx[l] = _lru_bwd(dy, proj, hl, lcw_full[l], vecs[l], lru_wa[l], lru_wx[l],
                                                               f"lru_bwd{l}")
        dproj, dba8, dcw8, g_dsc[l] = _dn_bwd(dy, proj, o, states, dcw_full[l], dnvs[l], dproj, f"dn_bwd{l}")
        g_dcw_[l] = dcw8.reshape(HEADS, 4, 3, HD).transpose(1, 2, 0, 3).reshape(4, 3 * D_MODEL)
        dba = _pad_lanes(jnp.concatenate([dba8[:, 0::HD], dba8[:, 1::HD]], axis=1))
        dhn = _matmul(dproj, wp[l], tb=True, tm=512, tn=D_MODEL, tk=768, name=f"d_hn_main{l}")
        dhn = _matmul(dba, wp[l][:, D_MAIN:], tb=True, tm=512, tn=D_MODEL, tk=HD, res=dhn, name=f"d_hn_ba{l}")
        dw_main = _matmul(hn, dproj, ta=True, tm=D_MODEL, tn=768, tk=512, name=f"d_win_main{l}")
        dw_ba = _matmul(hn, dba, ta=True, tm=D_MODEL, tn=HD, tk=512, name=f"d_win_ba{l}")
        g_win[l] = jnp.concatenate([_from_heads(dw_main[:, :2 * D_MODEL], 2), _from_heads(dw_main[:, 2 * D_MODEL:], 4),
                                    dw_ba[:, :D_IN - D_MAIN]], axis=1)
        dh, g_nw[l] = _rms_bwd(dhn, hin, norm_w[l][None], dh, f"rms_bwd{l}")
    return loss_part, dh, d_fnw, g_nw, g_vec, g_wa, g_wx, g_lcw_, g_dcw_, g_dsc, g_win, g_wout


def _reduce_grads(chip, d_fnw, g_nw, g_vec, g_wa, g_wx, g_lcw_, g_dcw_, g_dsc, g_win, g_wout):
    depth = len(g_nw)
    shard_in = D_IN // 4
    win_parts = jnp.stack([g_win[l].reshape(D_MODEL, 4, shard_in).transpose(1, 0, 2) for l in range(depth)])
    win_parts = win_parts.transpose(1, 0, 2, 3).reshape(N_DEV, D_MODEL, shard_in)
    wout_parts = jnp.stack([g_wout[l].reshape(4, D_MIX // 4, D_MODEL) for l in range(depth)])
    wout_parts = wout_parts.transpose(1, 0, 2, 3).reshape(N_DEV, D_MIX // 4, D_MODEL)
    vec_row = lambda k: jnp.concatenate([_rows128(g_vec[l][k]) for l in range(depth)])
    small = [
        jnp.concatenate([_rows128(g_nw[l]) for l in range(depth)]),
        vec_row(0), vec_row(1), vec_row(2), vec_row(3), vec_row(4),
        jnp.concatenate([_rows128(g_wa[l]) for l in range(depth)]),
        jnp.concatenate([_rows128(g_wx[l]) for l in range(depth)]),
        jnp.concatenate([g_dsc[l][0:1] for l in range(depth)]),
        jnp.concatenate([g_dsc[l][1:2] for l in range(depth)]),
        jnp.concatenate([g_dsc[l][2:3] for l in range(depth)]),
        _rows128(d_fnw),
        jnp.concatenate([_rows128(g_lcw_[l]) for l in range(depth)]),
        jnp.concatenate([_rows128(g_dcw_[l]) for l in range(depth)]),
    ]
    counts = [t.shape[0] for t in small]
    total = sum(counts)
    per = -(-total // (8 * N_DEV)) * 8
    small = jnp.concatenate(small + [jnp.zeros((per * N_DEV - total, HD), f32)]).reshape(N_DEV, per, HD)

    r_small, r_win, r_wout = _scatter_parts([small, win_parts, wout_parts], "scatter_grads")
    s_small = _sum_parts(r_small, "sum_small")
    s_win = _sum_parts(r_win, "sum_win")
    s_wout = _sum_parts(r_wout, "sum_wout")
    a_small, grad_w_in, grad_w_out = _share_reduced(s_small, [s_win, s_wout], "share_grads")
    a_small = a_small.reshape(N_DEV * per, HD)
    offs = [0]
    for c in counts:
        offs.append(offs[-1] + c)
    piece = lambda k: a_small[offs[k]:offs[k + 1]]
    grads = {
        "norm_w": piece(0).reshape(depth, D_MODEL), "lru_conv_b": piece(1).reshape(depth, D_MODEL),
        "lru_ba": piece(2).reshape(depth, D_MODEL), "lru_bx": piece(3).reshape(depth, D_MODEL),
        "lru_lambda": piece(4).reshape(depth, D_MODEL), "lru_norm_w": piece(5).reshape(depth, D_MODEL),
        "lru_wa": piece(6).reshape(depth, HEADS, HD, HD), "lru_wx": piece(7).reshape(depth, HEADS, HD, HD),
        "dn_A_log": piece(8)[:, :HEADS], "dn_dt_bias": piece(9)[:, :HEADS], "dn_norm_w": piece(10),
        "final_norm_w": piece(11).reshape(D_MODEL),
        "lru_conv_w": lax.dynamic_slice_in_dim(piece(12).reshape(depth, 4, D_MODEL), chip * (D_MODEL // 4), D_MODEL // 4, 2),
        "dn_conv_w": lax.dynamic_slice_in_dim(piece(13).reshape(depth, 4, 3 * D_MODEL), chip * (3 * D_MODEL // 4),
                                              3 * D_MODEL // 4, 2),
        "w_in": grad_w_in, "w_out": grad_w_out,
    }
    return grads
```

```python
import jax
import jax.numpy as jnp
from jax import lax
from jax.experimental import pallas as pl
from jax.experimental.pallas import tpu as pltpu

f32 = jnp.float32
bf16 = jnp.bfloat16
HI = lax.Precision.HIGHEST
MESH = pl.DeviceIdType.MESH

D_MODEL = 1024
HEADS = 8
HD = 128
CHUNK = 64
LRU_T = 128
SEQ_BLOCK = 1024
DN_SEQ_BLOCK = 512
DN_HP = 4
LRU_C = 8.0
D_IN = 6160
D_MAIN = 6144
D_INP = 6272
D_MIX = 2048
LRU_W = 2 * HD
DN_W = 4 * HD
DN_BLK0 = (HEADS * LRU_W) // DN_W
BA_BLK = D_MAIN // HD
EPS = 1e-6
QSCALE = HD ** -0.5
N_DEV = 8
VMEM_LIMIT = 56 * 1024 * 1024

ADAM_LR, ADAM_B1, ADAM_B2, ADAM_EPS, ADAM_WD, ADAM_STEP = 0.001, 0.9, 0.999, 1e-08, 0.01, 10


def _sig(x):
    return 1.0 / (1.0 + jnp.exp(-x))


def _log1p(u):
    w = 1.0 + u
    d = w - 1.0
    return jnp.where(d == 0.0, u, jnp.log(w) * (u / jnp.where(d == 0.0, 1.0, d)))


def _softplus(x):
    return jnp.maximum(x, 0.0) + _log1p(jnp.exp(-jnp.abs(x)))


def _expm1(x):
    t = x * (1.0 + x * (0.5 + x * (1.0 / 6 + x * (1.0 / 24 + x * (1.0 / 120 + x * (1.0 / 720))))))
    return jnp.where(jnp.abs(x) < 0.2, t, jnp.exp(x) - 1.0)


def _dot(a, b, prec=None):
    return jnp.dot(a, b, precision=prec, preferred_element_type=f32)


def _dot_nt(a, b, prec=None):
    return lax.dot_general(a, b, (((1,), (1,)), ((), ())), precision=prec, preferred_element_type=f32)


def _dot_tn(a, b, prec=None):
    return lax.dot_general(a, b, (((0,), (0,)), ((), ())), precision=prec, preferred_element_type=f32)


def _b(x):
    return x.astype(bf16)


def _sum0(x):
    return jnp.sum(x, axis=0, keepdims=True)


def _sum1(x):
    return jnp.sum(x, axis=1, keepdims=True)


def _rowmean(x):
    return jnp.mean(x, axis=-1, keepdims=True)


def _dsilu(z, sg):
    return sg * (1.0 + z * (1.0 - sg))


def _conv_taps(x, halo):
    xx = jnp.concatenate([halo, x], axis=0)
    return [pltpu.roll(xx, 3, axis=0)[8:], pltpu.roll(xx, 2, axis=0)[8:], pltpu.roll(xx, 1, axis=0)[8:], x]


def _conv(xs, cw):
    return cw[0:1] * xs[0] + cw[1:2] * xs[1] + cw[2:3] * xs[2] + cw[3:4] * xs[3]


def _rows_before(ref, halo_ref, lanes, t0, c, sblk):
    tprev = pl.multiple_of(jnp.maximum(t0 - 8, 0), 8)
    return jnp.where(c > 0, ref[pl.ds(tprev, 8), lanes], jnp.where(sblk > 0, halo_ref[:, lanes], 0.0))


def _conv_back(dxc, halo_after, cw):
    rows = dxc.shape[0]
    dd = jnp.concatenate([dxc, halo_after], axis=0)
    out = cw[3:4] * dxc
    for s in (1, 2, 3):
        out = out + cw[3 - s:4 - s] * pltpu.roll(dd, rows + 8 - s, axis=0)[:rows]
    return out


def _params(sem=None):
    return pltpu.CompilerParams(dimension_semantics=sem, vmem_limit_bytes=VMEM_LIMIT)


def _seq_specs(sb, nsb, width, colblk, reverse):
    blk = (lambda i: nsb - 1 - i) if reverse else (lambda i: i)
    main = pl.BlockSpec((sb, width), lambda h, i: (blk(i), colblk(h)))
    halo = pl.BlockSpec((8, width), lambda h, i: (jnp.maximum(blk(i) * (sb // 8) - 1, 0), colblk(h)))
    return main, halo


def _matmul(a, b, *, ta=False, tb=False, tm, tn, tk, res=None, out_dtype=f32, name):
    if ta:
        kdim, m = a.shape
    else:
        m, kdim = a.shape
    n = b.shape[0] if tb else b.shape[1]
    tm, tn, tk = min(tm, m), min(tn, n), min(tk, kdim)
    assert m % tm == 0 and n % tn == 0 and kdim % tk == 0, (name, m, n, kdim, tm, tn, tk)
    nk = kdim // tk
    dims = (((0 if ta else 1,), (1 if tb else 0,)), ((), ()))
    has_res = res is not None

    def body(*refs):
        a_ref, b_ref = refs[:2]
        r_ref = refs[2] if has_res else None
        o_ref = refs[3] if has_res else refs[2]
        acc_ref = refs[-1] if nk > 1 else None
        part = lax.dot_general(_b(a_ref[...]), _b(b_ref[...]), dims, preferred_element_type=f32)

        def finish(total):
            if has_res:
                total = total + r_ref[...]
            o_ref[...] = total.astype(out_dtype)

        if nk == 1:
            finish(part)
        else:
            k = pl.program_id(2)

            @pl.when(k == 0)
            def _():
                acc_ref[...] = part

            @pl.when(k > 0)
            def _():
                acc_ref[...] += part

            @pl.when(k == nk - 1)
            def _():
                finish(acc_ref[...])

    a_spec = pl.BlockSpec((tk, tm), lambda i, j, k: (k, i)) if ta else pl.BlockSpec((tm, tk), lambda i, j, k: (i, k))
    b_spec = pl.BlockSpec((tn, tk), lambda i, j, k: (j, k)) if tb else pl.BlockSpec((tk, tn), lambda i, j, k: (k, j))
    o_spec = pl.BlockSpec((tm, tn), lambda i, j, k: (i, j))
    in_specs, args = [a_spec, b_spec], [a, b]
    if has_res:
        in_specs.append(o_spec)
        args.append(res)
    return pl.pallas_call(
        body, name=name, grid=(m // tm, n // tn, nk), in_specs=in_specs, out_specs=o_spec,
        out_shape=jax.ShapeDtypeStruct((m, n), out_dtype),
        scratch_shapes=[pltpu.VMEM((tm, tn), f32)] if nk > 1 else [],
        compiler_params=_params(("parallel", "parallel", "arbitrary")),
    )(*args)


def _rms_fwd(x, w, name):
    s = x.shape[0]
    tr = min(512, s)

    def body(x_ref, w_ref, h_ref):
        xv = x_ref[...]
        r = lax.rsqrt(_rowmean(xv * xv) + EPS)
        h_ref[...] = (xv * r * w_ref[...]).astype(bf16)

    return pl.pallas_call(
        body, name=name, grid=(s // tr,),
        in_specs=[pl.BlockSpec((tr, D_MODEL), lambda i: (i, 0)), pl.BlockSpec((1, D_MODEL), lambda i: (0, 0))],
        out_specs=pl.BlockSpec((tr, D_MODEL), lambda i: (i, 0)),
        out_shape=jax.ShapeDtypeStruct((s, D_MODEL), bf16), compiler_params=_params(("parallel",)),
    )(x, w)


def _rms_bwd(dh, x, w, dres, name):
    s = x.shape[0]
    tr = min(512, s)

    def body(dh_ref, x_ref, w_ref, dres_ref, dx_ref, dw_ref):
        xv = x_ref[...]
        r = lax.rsqrt(_rowmean(xv * xv) + EPS)
        xn = xv * r
        d = dh_ref[...]
        dxn = d * w_ref[...]
        dx_ref[...] = dres_ref[...] + r * (dxn - xn * _rowmean(dxn * xn))
        part = _sum0(d * xn)

        @pl.when(pl.program_id(0) == 0)
        def _():
            dw_ref[...] = part

        @pl.when(pl.program_id(0) > 0)
        def _():
            dw_ref[...] += part

    row = pl.BlockSpec((tr, D_MODEL), lambda i: (i, 0))
    vec = pl.BlockSpec((1, D_MODEL), lambda i: (0, 0))
    return pl.pallas_call(
        body, name=name, grid=(s // tr,), in_specs=[row, row, vec, row], out_specs=[row, vec],
        out_shape=[jax.ShapeDtypeStruct((s, D_MODEL), f32), jax.ShapeDtypeStruct((1, D_MODEL), f32)],
        compiler_params=_params(("arbitrary",)),
    )(dh, x, w, dres)


def _final_loss(x, w, target, name):
    s = x.shape[0]
    tr = min(512, s)

    def body(x_ref, w_ref, t_ref, dx_ref, dw_ref, loss_ref):
        xv = x_ref[...]
        wv = w_ref[...]
        r = lax.rsqrt(_rowmean(xv * xv) + EPS)
        xn = xv * r
        e = xn * wv - t_ref[...]
        lb = jnp.broadcast_to(0.5 * _sum0(_rowmean(e * e)), (1, HD))
        dy = e * (1.0 / D_MODEL)
        dxn = dy * wv
        dx_ref[...] = r * (dxn - xn * _rowmean(dxn * xn))
        part = _sum0(dy * xn)

        @pl.when(pl.program_id(0) == 0)
        def _():
            dw_ref[...] = part
            loss_ref[...] = lb

        @pl.when(pl.program_id(0) > 0)
        def _():
            dw_ref[...] += part
            loss_ref[...] += lb

    row = pl.BlockSpec((tr, D_MODEL), lambda i: (i, 0))
    vec = pl.BlockSpec((1, D_MODEL), lambda i: (0, 0))
    lsp = pl.BlockSpec((1, HD), lambda i: (0, 0))
    return pl.pallas_call(
        body, name=name, grid=(s // tr,), in_specs=[row, vec, row], out_specs=[row, vec, lsp],
        out_shape=[jax.ShapeDtypeStruct((s, D_MODEL), f32), jax.ShapeDtypeStruct((1, D_MODEL), f32),
                   jax.ShapeDtypeStruct((1, HD), f32)],
        compiler_params=_params(("arbitrary",)),
    )(x, w, target)


LX = slice(0, HD)
LZ = slice(HD, 2 * HD)


def _lru_gates(xc, vec, wa, wx):
    sp = _softplus(-vec[3:4])
    xcb = _b(xc)
    r = _sig(_dot(xcb, wa) + vec[1:2])
    i = _sig(_dot(xcb, wx) + vec[2:3])
    la = (-LRU_C) * r * sp
    a = jnp.exp(la)
    mult = jnp.sqrt(-_expm1(2.0 * la))
    return r, i, a, mult, sp, xcb


def _lru_fwd(proj, cw, vec, wa, wx, name):
    s = proj.shape[0]
    sb = min(SEQ_BLOCK, s)
    nsb = s // sb
    t = min(LRU_T, sb)
    nc = sb // t

    def body(p_ref, ph_ref, cw_ref, vec_ref, wa_ref, wx_ref, y_ref, hl_ref, hc_ref):
        sblk = pl.program_id(1)
        cwv = cw_ref[...]
        vec_v = vec_ref[...]
        wa_v = _b(wa_ref[...])
        wx_v = _b(wx_ref[...])
        row = lax.broadcasted_iota(jnp.int32, (t, HD), 0)

        @pl.when(sblk == 0)
        def _():
            hc_ref[...] = jnp.zeros((8, HD), f32)

        def chunk(c, hcarry):
            t0 = pl.multiple_of(c * t, t)
            xs = _conv_taps(p_ref[pl.ds(t0, t), LX], _rows_before(p_ref, ph_ref, LX, t0, c, sblk))
            xc = vec_v[0:1] + _conv(xs, cwv)
            r, i, a, mult, _, _ = _lru_gates(xc, vec_v, wa_v, wx_v)
            bb = mult * (i * xc)
            d = 1
            while d < t:
                m = row >= d
                bb = jnp.where(m, a * pltpu.roll(bb, d, axis=0) + bb, bb)
                a = jnp.where(m, a * pltpu.roll(a, d, axis=0), a)
                d *= 2
            hl = bb + a * hcarry
            hl_ref[pl.ds(t0, t), :] = hl
            z = p_ref[pl.ds(t0, t), LZ]
            nrm = lax.rsqrt(_rowmean(hl * hl) + EPS)
            y_ref[pl.ds(t0, t), :] = (hl * nrm * vec_v[4:5] * (z * _sig(z))).astype(bf16)
            return hl[t - 1:t, :]

        hlast = lax.fori_loop(0, nc, chunk, hc_ref[0:1, :])
        hc_ref[...] = jnp.broadcast_to(hlast, (8, HD))

    main, halo = _seq_specs(sb, nsb, LRU_W, lambda h: h, False)
    hcol = lambda h, i: (0, h)
    wsp = pl.BlockSpec((None, HD, HD), lambda h, i: (h, 0, 0))
    osp = pl.BlockSpec((sb, HD), lambda h, i: (i, h))
    return pl.pallas_call(
        body, name=name, grid=(HEADS, nsb),
        in_specs=[main, halo, pl.BlockSpec((4, HD), hcol), pl.BlockSpec((8, HD), hcol), wsp, wsp],
        out_specs=[osp, osp],
        out_shape=[jax.ShapeDtypeStruct((s, D_MODEL), bf16), jax.ShapeDtypeStruct((s, D_MODEL), f32)],
        scratch_shapes=[pltpu.VMEM((8, HD), f32)],
        compiler_params=_params(("parallel", "arbitrary")),
    )(proj, proj, cw, vec, wa, wx)


def _lru_bwd(dy, proj, hl, cw, vec, wa, wx, name):
    s = proj.shape[0]
    sb = min(SEQ_BLOCK, s)
    nsb = s // sb
    t = min(LRU_T, sb)
    nc = sb // t

    def body(dy_ref, p_ref, ph_ref, hl_ref, hlh_ref, cw_ref, vec_ref, wa_ref, wx_ref,
             dp_ref, dcw_ref, dvec_ref, dwa_ref, dwx_ref, acc_ref, dxh_ref):
        step = pl.program_id(1)
        sblk = nsb - 1 - step
        cwv = cw_ref[...]
        vec_v = vec_ref[...]
        wa_v = _b(wa_ref[...])
        wx_v = _b(wx_ref[...])
        lnw = vec_v[4:5]
        row = lax.broadcasted_iota(jnp.int32, (t, HD), 0)

        @pl.when(step == 0)
        def _():
            acc_ref[...] = jnp.zeros((16, HD), f32)
            dxh_ref[...] = jnp.zeros((8, HD), f32)
            dwa_ref[...] = jnp.zeros((HD, HD), f32)
            dwx_ref[...] = jnp.zeros((HD, HD), f32)

        def chunk(ci, carry):
            mu, dxc_halo, dcw0, dcw1, dcw2, dcw3, dcb, dba, dbx, dsp, dlnw = carry
            c = nc - 1 - ci
            t0 = pl.multiple_of(c * t, t)
            xs = _conv_taps(p_ref[pl.ds(t0, t), LX], _rows_before(p_ref, ph_ref, LX, t0, c, sblk))
            xc = vec_v[0:1] + _conv(xs, cwv)
            r, i, a, mult, sp, xcb = _lru_gates(xc, vec_v, wa_v, wx_v)
            hv = hl_ref[pl.ds(t0, t), :]
            hhalo = _rows_before(hl_ref, hlh_ref, slice(None), t0, c, sblk)
            hprev = pltpu.roll(jnp.concatenate([hhalo, hv], axis=0), 1, axis=0)[8:]
            z = p_ref[pl.ds(t0, t), LZ]
            sgz = _sig(z)
            sz = z * sgz
            dyv = dy_ref[pl.ds(t0, t), :]
            nrm = lax.rsqrt(_rowmean(hv * hv) + EPS)
            hn = hv * nrm
            dlnw = dlnw + _sum0(dyv * hn * sz)
            dp_ref[pl.ds(t0, t), LZ] = dyv * hn * lnw * _dsilu(z, sgz)
            dhn = dyv * lnw * sz
            lam = nrm * (dhn - hn * _rowmean(dhn * hn))
            cf = jnp.where(row == t - 1, 1.0, pltpu.roll(a, t - 1, axis=0))
            d = 1
            while d < t:
                m = row < t - d
                lam = jnp.where(m, lam + cf * pltpu.roll(lam, t - d, axis=0), lam)
                cf = jnp.where(m, cf * pltpu.roll(cf, t - d, axis=0), cf)
                d *= 2
            lam = lam + cf * mu
            mu = a[0:1] * lam[0:1]
            da = lam * hprev
            dmult = lam * (i * xc)
            di = lam * mult * xc
            dxc = lam * mult * i
            dla = da * a - dmult * (a * a) / mult
            dr = dla * (-LRU_C * sp)
            dsp = dsp + _sum0(dla * (-LRU_C * r))
            dra = dr * r * (1.0 - r)
            dia = di * i * (1.0 - i)
            dba = dba + _sum0(dra)
            dbx = dbx + _sum0(dia)
            drab, diab = _b(dra), _b(dia)
            dwa_ref[...] += _dot_tn(xcb, drab)
            dwx_ref[...] += _dot_tn(xcb, diab)
            dxc = dxc + _dot_nt(drab, wa_v) + _dot_nt(diab, wx_v)
            dcb = dcb + _sum0(dxc)
            dcw0 = dcw0 + _sum0(dxc * xs[0])
            dcw1 = dcw1 + _sum0(dxc * xs[1])
            dcw2 = dcw2 + _sum0(dxc * xs[2])
            dcw3 = dcw3 + _sum0(dxc * xs[3])
            dp_ref[pl.ds(t0, t), LX] = _conv_back(dxc, dxc_halo, cwv)
            return (mu, dxc[0:8], dcw0, dcw1, dcw2, dcw3, dcb, dba, dbx, dsp, dlnw)

        acc = acc_ref[...]
        init = (acc[9:10], dxh_ref[...]) + tuple(acc[k:k + 1] for k in range(9))
        mu, dxh, dcw0, dcw1, dcw2, dcw3, dcb, dba, dbx, dsp, dlnw = lax.fori_loop(0, nc, chunk, init)
        zero = jnp.zeros((1, HD), f32)
        acc_ref[...] = jnp.concatenate([dcw0, dcw1, dcw2, dcw3, dcb, dba, dbx, dsp, dlnw, mu] + [zero] * 6, axis=0)
        dxh_ref[...] = dxh

        @pl.when(step == nsb - 1)
        def _():
            dcw_ref[...] = jnp.concatenate([dcw0, dcw1, dcw2, dcw3], axis=0)
            dlam = -dsp * _sig(-vec_v[3:4])
            dvec_ref[...] = jnp.concatenate([dcb, dba, dbx, dlam, dlnw, zero, zero, zero], axis=0)

    main, halo = _seq_specs(sb, nsb, LRU_W, lambda h: h, True)
    hmain, hhalo = _seq_specs(sb, nsb, HD, lambda h: h, True)
    hcol = lambda h, i: (0, h)
    wsp = pl.BlockSpec((None, HD, HD), lambda h, i: (h, 0, 0))
    return pl.pallas_call(
        body, name=name, grid=(HEADS, nsb),
        in_specs=[hmain, main, halo, hmain, hhalo, pl.BlockSpec((4, HD), hcol), pl.BlockSpec((8, HD), hcol), wsp, wsp],
        out_specs=[main, pl.BlockSpec((4, HD), hcol), pl.BlockSpec((8, HD), hcol), wsp, wsp],
        out_shape=[jax.ShapeDtypeStruct((s, D_MAIN), f32), jax.ShapeDtypeStruct((4, D_MODEL), f32),
                   jax.ShapeDtypeStruct((8, D_MODEL), f32),
                   jax.ShapeDtypeStruct((HEADS, HD, HD), f32), jax.ShapeDtypeStruct((HEADS, HD, HD), f32)],
        scratch_shapes=[pltpu.VMEM((16, HD), f32), pltpu.VMEM((8, HD), f32)],
        compiler_params=_params(("parallel", "arbitrary")),
    )(dy, proj, proj, hl, hl, cw, vec, wa, wx)


def _lane_pick(x, lane, idx):
    return _sum1(jnp.where(lane == idx, x, 0.0))


def _round_robin(gens):
    results = [None] * len(gens)
    live = list(range(len(gens)))
    while live:
        for i in list(live):
            try:
                next(gens[i])
            except StopIteration as done:
                results[i] = done.value
                live.remove(i)
    return results


def _sdot(a, b):
    return _dot(_b(a), _b(b))


def _sdot_tn(a, b):
    return _dot_tn(_b(a), _b(b))


def _sdot_nt(a, b):
    return _dot_nt(_b(a), _b(b))


def _inv_unit_lower(a):
    n = -a
    p = _sdot(a, a)
    yield
    for k in range(5):
        n = n + p + _sdot(n, p)
        if k < 4:
            p = _sdot(p, p)
        yield
    return n


def _dn_chunk(x3, halo3, bav, cw3, dnv, h, masks):
    lane, ri, ci, eye, ltri = masks
    xs = _conv_taps(x3, halo3)
    cpre = _conv(xs, cw3)
    sg = _sig(cpre)
    act = cpre * sg
    q_raw, k_raw, v = act[:, 0:HD], act[:, HD:2 * HD], act[:, 2 * HD:3 * HD]
    rq = lax.rsqrt(_sum1(q_raw * q_raw) + EPS)
    rk = lax.rsqrt(_sum1(k_raw * k_raw) + EPS)
    qn = q_raw * rq
    k = k_raw * rk
    q = qn * QSCALE
    b_in = jnp.broadcast_to(_lane_pick(bav, lane, h), (CHUNK, HD))
    a_in = jnp.broadcast_to(_lane_pick(bav, lane, HEADS + h), (CHUNK, HD))
    lane1 = lane[0:1]
    ea = jnp.exp(_lane_pick(dnv[0:1], lane1, h))
    dt = _lane_pick(dnv[1:2], lane1, h)
    beta = _sig(b_in)
    sp = _softplus(a_in + dt)
    g = -ea * sp
    gc = _dot(ltri, g, HI)
    yield
    gc64 = gc[:, 0:CHUNK]
    grow = _sum0(gc64 * eye)
    dm = jnp.exp(jnp.where(ri >= ci, gc64 - grow, -1e30))
    ds_ = jnp.where(ri > ci, dm, 0.0)
    eg = jnp.exp(gc)
    glast = gc[CHUNK - 1:CHUNK, :]
    ek = jnp.exp(glast - gc)
    kb = k * beta
    kbf = _b(k)
    amat = _dot_nt(_b(kb), kbf) * ds_
    pmat = _dot_nt(_b(q), kbf) * dm
    yield
    tinv = yield from _inv_unit_lower(amat)
    return dict(xs=xs, cpre=cpre, sg=sg, rq=rq, rk=rk, qn=qn, q=q, k=k, v=v, kbf=kbf, b_in=b_in, a_in=a_in, ea=ea,
                dt=dt, beta=beta, g=g, gc=gc, dm=dm, ds=ds_, eg=eg, glast=glast, ek=ek, kb=kb, amat=amat,
                tinv=tinv, pmat=pmat)


def _dn_masks():
    lane = lax.broadcasted_iota(jnp.int32, (CHUNK, HD), 1)
    ri = lax.broadcasted_iota(jnp.int32, (CHUNK, CHUNK), 0)
    ci = lax.broadcasted_iota(jnp.int32, (CHUNK, CHUNK), 1)
    eye = (ri == ci).astype(f32)
    ltri = (ri >= ci).astype(f32)
    return lane, ri, ci, eye, ltri


def _dn_fwd(proj, cw, dnv, name):
    s = proj.shape[0]
    sb = min(DN_SEQ_BLOCK, s)
    nsb = s // sb
    nc = sb // CHUNK
    hp = DN_HP

    def body(p_ref, ph_ref, ba_ref, cwq_ref, cwk_ref, cwv_ref, dnv_ref, y_ref, o_ref, st_ref, s_ref):
        grp = pl.program_id(0)
        sblk = pl.program_id(1)
        masks = _dn_masks()
        dnv_v = dnv_ref[...]
        cw3 = [jnp.concatenate([r[:, j * HD:(j + 1) * HD] for r in (cwq_ref, cwk_ref, cwv_ref)], axis=1)
               for j in range(hp)]

        @pl.when(sblk == 0)
        def _():
            s_ref[...] = jnp.zeros((hp, HD, HD), f32)

        def one_head(j, x3, halo3, bav, z, st):
            f = yield from _dn_chunk(x3, halo3, bav, cw3[j], dnv_v, grp * hp + j, masks)
            sbf = _b(st)
            qs = _dot(_b(f["q"] * f["eg"]), sbf)
            rhs = f["beta"] * (f["v"] - _dot(_b(f["k"] * f["eg"]), sbf))
            yield
            vn = rhs + _sdot(f["tinv"], rhs)
            yield
            vnb = _b(vn)
            o = qs + _dot(_b(f["pmat"]), vnb)
            st_new = st * jnp.exp(f["glast"]) + _dot_tn(_b(f["k"] * f["ek"]), vnb)
            yield
            nrm = lax.rsqrt(_rowmean(o * o) + EPS)
            y = (o * nrm * dnv_v[2:3] * (z * _sig(z))).astype(bf16)
            return o, y, st_new

        def chunk(c, states):
            t0 = pl.multiple_of(c * CHUNK, CHUNK)
            rows = pl.ds(t0, CHUNK)
            bav = ba_ref[rows, :]
            ins = []
            for j in range(hp):
                qkv = slice(j * DN_W, j * DN_W + 3 * HD)
                ins.append((p_ref[rows, qkv], _rows_before(p_ref, ph_ref, qkv, t0, c, sblk),
                            p_ref[rows, j * DN_W + 3 * HD:(j + 1) * DN_W]))
            outs = _round_robin([one_head(j, ins[j][0], ins[j][1], bav, ins[j][2], states[j]) for j in range(hp)])
            for j in range(hp):
                hl = slice(j * HD, (j + 1) * HD)
                st_ref[j, c] = states[j]
                o_ref[rows, hl] = outs[j][0]
                y_ref[rows, hl] = outs[j][1]
            return tuple(out[2] for out in outs)

        final = lax.fori_loop(0, nc, chunk, tuple(s_ref[j] for j in range(hp)))
        for j in range(hp):
            s_ref[j] = final[j]

    main, halo = _seq_specs(sb, nsb, hp * DN_W, lambda g: DN_BLK0 // hp + g, False)
    basp = pl.BlockSpec((sb, HD), lambda g, i: (i, BA_BLK))
    cws = lambda off: pl.BlockSpec((4, hp * HD), lambda g, i: (0, off // hp + g))
    osp = pl.BlockSpec((sb, hp * HD), lambda g, i: (i, g))
    return pl.pallas_call(
        body, name=name, grid=(HEADS // hp, nsb),
        in_specs=[main, halo, basp, cws(0), cws(HEADS), cws(2 * HEADS), pl.BlockSpec((8, HD), lambda g, i: (0, 0))],
        out_specs=[osp, osp, pl.BlockSpec((hp, nc, HD, HD), lambda g, i: (g, i, 0, 0))],
        out_shape=[jax.ShapeDtypeStruct((s, D_MODEL), bf16), jax.ShapeDtypeStruct((s, D_MODEL), f32),
                   jax.ShapeDtypeStruct((HEADS, s // CHUNK, HD, HD), f32)],
        scratch_shapes=[pltpu.VMEM((hp, HD, HD), f32)],
        compiler_params=_params(("parallel", "arbitrary")),
    )(proj, proj, proj, cw, cw, cw, dnv)


def _dn_bwd(dy, proj, o, states, cw, dnv, dproj, name):
    s = proj.shape[0]
    sb = min(DN_SEQ_BLOCK, s)
    nsb = s // sb
    nc = sb // CHUNK
    hp = DN_HP

    def body(dy_ref, p_ref, ph_ref, ba_ref, o_ref, st_ref, cwq_ref, cwk_ref, cwv_ref, dnv_ref, dpin_ref,
             dp_ref, dba_ref, dcw_ref, dsc_ref, ds_ref, dch_ref, cwacc_ref, sacc_ref, nacc_ref):
        del dpin_ref
        grp = pl.program_id(0)
        step = pl.program_id(1)
        sblk = nsb - 1 - step
        masks = _dn_masks()
        lane, ri, ci, eye, ltri = masks
        utri = (ri <= ci).astype(f32)
        cw3s = [jnp.concatenate([r[:, j * HD:(j + 1) * HD] for r in (cwq_ref, cwk_ref, cwv_ref)], axis=1)
                for j in range(hp)]
        dnv_v = dnv_ref[...]
        dnw = dnv_v[2:3]
        row64 = lax.broadcasted_iota(jnp.int32, (CHUNK, HD), 0)

        @pl.when(step == 0)
        def _():
            ds_ref[...] = jnp.zeros((hp, HD, HD), f32)
            dch_ref[...] = jnp.zeros((hp, 8, 3 * HD), f32)
            cwacc_ref[...] = jnp.zeros((hp, 8, 3 * HD), f32)
            sacc_ref[...] = jnp.zeros((hp, 8, HD), f32)

        @pl.when((step == 0) & (grp == 0))
        def _():
            nacc_ref[...] = jnp.zeros((8, HD), f32)

        def one_head(j, x3, halo3, bav, z, st, ov, dyv, carry):
            dst, dch, cwacc, sa0, sa1 = carry
            cw3 = cw3s[j]
            f = yield from _dn_chunk(x3, halo3, bav, cw3, dnv_v, grp * hp + j, masks)
            q, k, v, beta, eg, ek, kbf = f["q"], f["k"], f["v"], f["beta"], f["eg"], f["ek"], f["kbf"]
            sbf = _b(st)
            dstb = _b(dst)
            qe = q * eg
            keg = k * eg
            kd = k * ek
            kegb, qeb, kdb = _b(keg), _b(qe), _b(kd)
            e = v - _dot(kegb, sbf)
            yield
            rhs = beta * e
            vn = rhs + _sdot(f["tinv"], rhs)
            yield
            vnb = _b(vn)
            pb = _b(f["pmat"])
            sgz = _sig(z)
            sz = z * sgz
            nrm = lax.rsqrt(_rowmean(ov * ov) + EPS)
            on = ov * nrm
            ddnw = _sum0(dyv * on * sz)
            dpz = dyv * on * dnw * _dsilu(z, sgz)
            don = dyv * dnw * sz
            do = nrm * (don - on * _rowmean(don * on))
            dob = _b(do)
            dvn = _dot_tn(pb, dob) + _dot(kdb, dstb)
            dpm = jnp.where(ri >= ci, _dot_nt(dob, vnb), 0.0)
            dqe = _dot_nt(dob, sbf)
            dkd = _dot_nt(vnb, dstb)
            qtdo = _dot_tn(qeb, dob)
            yield
            drhs = dvn + _sdot_tn(f["tinv"], dvn)
            dqk = _b(dpm * f["dm"])
            dq = _dot(dqk, kbf) + dqe * eg
            dk_p = _dot_tn(dqk, _b(q))
            yield
            dam = jnp.where(ri > ci, -_sdot_nt(drhs, vn), 0.0)
            de = drhs * beta
            deb = _b(de)
            dbeta = _sum1(drhs * e)
            dkeg = -_dot_nt(deb, sbf)
            dst_new = qtdo + dst * jnp.exp(f["glast"]) - _dot_tn(kegb, deb)
            yield
            dkk = _b(dam * f["ds"])
            dkb = _dot(dkk, kbf)
            dk = dk_p + _dot_tn(dkk, _b(f["kb"])) + dkb * beta + dkeg * eg + dkd * ek
            yield
            dbeta = dbeta + _sum1(dkb * k)
            mm = dpm * f["pmat"] + dam * f["amat"]
            dglast = _sum1(_sum0(dst * st)) * jnp.exp(f["glast"]) + _sum1(_sum0(dkd * kd))
            dgc = (_sum1(mm) - _sum1(eye * _sum0(mm)) + _sum1(dqe * qe) + _sum1(dkeg * keg) - _sum1(dkd * kd))
            dgc = jnp.broadcast_to(dgc, (CHUNK, HD)) + jnp.where(row64 == CHUNK - 1, dglast, 0.0)
            dg = _dot(utri, dgc, HI)
            yield
            dbin = jnp.broadcast_to(dbeta, (CHUNK, HD)) * beta * (1.0 - beta)
            dain = dg * (-f["ea"]) * _sig(f["a_in"] + f["dt"])
            dba = jnp.where(lane == 0, dbin, 0.0) + jnp.where(lane == 1, dain, 0.0)
            sa0 = sa0 + _sum0(dg * f["g"])
            sa1 = sa1 + _sum0(dain)
            dq_raw = (QSCALE * f["rq"]) * (dq - f["qn"] * _sum1(f["qn"] * dq))
            dk_raw = f["rk"] * (dk - k * _sum1(k * dk))
            dact = jnp.concatenate([dq_raw, dk_raw, de], axis=1)
            dc = dact * _dsilu(f["cpre"], f["sg"])
            xs = f["xs"]
            cwacc = cwacc + jnp.concatenate([_sum0(dc * xs[0]), _sum0(dc * xs[1]), _sum0(dc * xs[2]),
                                             _sum0(dc * xs[3])], axis=0)
            dpqkv = _conv_back(dc, dch, cw3)
            return dpz, dpqkv, dba, ddnw, (dst_new, dc[0:8], cwacc, sa0, sa1)

        def chunk(cidx, carry):
            heads, nacc = carry
            c = nc - 1 - cidx
            t0 = pl.multiple_of(c * CHUNK, CHUNK)
            rows = pl.ds(t0, CHUNK)
            bav = ba_ref[rows, :]
            ins = []
            for j in range(hp):
                hl = slice(j * HD, (j + 1) * HD)
                qkv = slice(j * DN_W, j * DN_W + 3 * HD)
                ins.append((p_ref[rows, qkv], _rows_before(p_ref, ph_ref, qkv, t0, c, sblk), bav,
                            p_ref[rows, j * DN_W + 3 * HD:(j + 1) * DN_W], st_ref[j, c], o_ref[rows, hl],
                            dy_ref[rows, hl]))
            outs = _round_robin([one_head(j, *ins[j], heads[j]) for j in range(hp)])
            for j in range(hp):
                dpz, dpqkv, dba, ddnw, _ = outs[j]
                dp_ref[rows, j * DN_W:j * DN_W + 3 * HD] = dpqkv
                dp_ref[rows, j * DN_W + 3 * HD:(j + 1) * DN_W] = dpz
                dba_ref[rows, j * HD:(j + 1) * HD] = dba
                nacc = nacc + ddnw
            return tuple(out[4] for out in outs), nacc

        init = tuple((ds_ref[j], dch_ref[j], cwacc_ref[j, 0:4, :], sacc_ref[j, 0:1, :], sacc_ref[j, 1:2, :])
                     for j in range(hp))
        heads, nacc = lax.fori_loop(0, nc, chunk, (init, nacc_ref[0:1, :]))
        nacc_ref[0:1, :] = nacc
        for j in range(hp):
            ds_ref[j], dch_ref[j], cwacc_ref[j, 0:4, :], sacc_ref[j, 0:1, :], sacc_ref[j, 1:2, :] = heads[j]

        @pl.when(step == nsb - 1)
        def _():
            lane8 = lax.broadcasted_iota(jnp.int32, (8, HD), 1)
            row8 = lax.broadcasted_iota(jnp.int32, (8, HD), 0)
            mine = jnp.zeros((8, HD), f32)
            for j in range(hp):
                dcw_ref[j] = cwacc_ref[j, 0:4, :]
                mine = mine + jnp.where((lane8 == grp * hp + j) & (row8 < 2), sacc_ref[j], 0.0)

            @pl.when(grp == 0)
            def _():
                dsc_ref[...] = mine

            @pl.when(grp > 0)
            def _():
                dsc_ref[...] += mine

            @pl.when(grp == HEADS // hp - 1)
            def _():
                dsc_ref[2:3, :] = nacc_ref[0:1, :]

    main, halo = _seq_specs(sb, nsb, hp * DN_W, lambda g: DN_BLK0 // hp + g, True)
    rblk = lambda i: nsb - 1 - i
    basp = pl.BlockSpec((sb, HD), lambda g, i: (rblk(i), BA_BLK))
    cws = lambda off: pl.BlockSpec((4, hp * HD), lambda g, i: (0, off // hp + g))
    hsp = lambda off: pl.BlockSpec((sb, hp * HD), lambda g, i: (rblk(i), off // hp + g))
    return pl.pallas_call(
        body, name=name, grid=(HEADS // hp, nsb),
        in_specs=[hsp(HEADS), main, halo, basp, hsp(0),
                  pl.BlockSpec((hp, nc, HD, HD), lambda g, i: (g, rblk(i), 0, 0)),
                  cws(0), cws(HEADS), cws(2 * HEADS), pl.BlockSpec((8, HD), lambda g, i: (0, 0)),
                  pl.BlockSpec(memory_space=pl.ANY)],
        out_specs=[main, hsp(0),
                   pl.BlockSpec((hp, 4, 3 * HD), lambda g, i: (g, 0, 0)), pl.BlockSpec((8, HD), lambda g, i: (0, 0))],
        out_shape=[jax.ShapeDtypeStruct((s, D_MAIN), f32), jax.ShapeDtypeStruct((s, D_MODEL), f32),
                   jax.ShapeDtypeStruct((HEADS, 4, 3 * HD), f32), jax.ShapeDtypeStruct((8, HD), f32)],
        scratch_shapes=[pltpu.VMEM((hp, HD, HD), f32), pltpu.VMEM((hp, 8, 3 * HD), f32),
                        pltpu.VMEM((hp, 8, 3 * HD), f32), pltpu.VMEM((hp, 8, HD), f32), pltpu.VMEM((8, HD), f32)],
        input_output_aliases={10: 0},
        compiler_params=_params(("arbitrary", "arbitrary")),
    )(dy, proj, proj, proj, o, states, cw, cw, cw, dnv, dproj)


ANY = pl.BlockSpec(memory_space=pl.ANY)
CHIP_MASKS = ((1, 0, 0), (0, 1, 0), (1, 1, 0))
ALL_MASKS = tuple((m >> 2 & 1, m >> 1 & 1, m & 1) for m in range(1, N_DEV))


def _me():
    return lax.axis_index("x"), lax.axis_index("y"), lax.axis_index("c")


def _flip(me, mask):
    return tuple((1 - v) if m else v for v, m in zip(me, mask))


def _dev_index(dev):
    return 4 * dev[0] + 2 * dev[1] + dev[2]


def _exchange(name, arrays, out_shapes, masks, src_of, dst_of, own_of):
    na, nm = len(arrays), len(masks)

    def body(*refs):
        srcs, dsts = refs[:na], refs[na:2 * na]
        send_sems, recv_sems, loc_sems = refs[2 * na:]
        me = _me()
        local = []
        for a in range(na):
            own = own_of(a, srcs[a], dsts[a], me)
            if own is not None:
                cp = pltpu.make_async_copy(own[0], own[1], loc_sems.at[a])
                cp.start()
                local.append(cp)
        sends = []
        for a in range(na):
            for mi, mask in enumerate(masks):
                peer = _flip(me, mask)
                cp = pltpu.make_async_remote_copy(
                    src_ref=src_of(a, srcs[a], me, peer), dst_ref=dst_of(a, dsts[a], me),
                    send_sem=send_sems.at[a * nm + mi], recv_sem=recv_sems.at[a * nm + mi],
                    device_id=peer, device_id_type=MESH)
                cp.start()
                sends.append(cp)
        for a in range(na):
            for mi, mask in enumerate(masks):
                peer = _flip(me, mask)
                pltpu.make_async_remote_copy(
                    src_ref=src_of(a, srcs[a], peer, me), dst_ref=dst_of(a, dsts[a], peer),
                    send_sem=send_sems.at[a * nm + mi], recv_sem=recv_sems.at[a * nm + mi],
                    device_id=peer, device_id_type=MESH).wait_recv()
        for cp in sends:
            cp.wait_send()
        for cp in local:
            cp.wait()

    return pl.pallas_call(
        body, name=name, in_specs=[ANY] * na, out_specs=[ANY] * na,
        out_shape=[jax.ShapeDtypeStruct(shp, arr.dtype) for shp, arr in zip(out_shapes, arrays)],
        scratch_shapes=[pltpu.SemaphoreType.DMA((na * nm,)), pltpu.SemaphoreType.DMA((na * nm,)),
                        pltpu.SemaphoreType.DMA((na,))],
        compiler_params=pltpu.CompilerParams(has_side_effects=True),
    )(*arrays)


def _gather_over_chips(arrays, name):
    chip = lambda dev: 2 * dev[0] + dev[1]
    return _exchange(
        name, arrays, [(4,) + a.shape for a in arrays], CHIP_MASKS,
        src_of=lambda a, ref, me, peer: ref,
        dst_of=lambda a, ref, sender: ref.at[chip(sender)],
        own_of=lambda a, src, dst, me: (src, dst.at[chip(me)]))


def _scatter_parts(arrays, name):
    return _exchange(
        name, arrays, [a.shape for a in arrays], ALL_MASKS,
        src_of=lambda a, ref, me, peer: ref.at[_dev_index(peer)],
        dst_of=lambda a, ref, sender: ref.at[_dev_index(sender)],
        own_of=lambda a, src, dst, me: (src.at[_dev_index(me)], dst.at[_dev_index(me)]))


def _share_reduced(small, pair_arrays, name):
    arrays = [small] + list(pair_arrays)
    na = len(arrays)
    nm = len(ALL_MASKS)

    def body(*refs):
        srcs, dsts = refs[:na], refs[na:2 * na]
        send_sems, recv_sems, loc_sems = refs[2 * na:]
        me = _me()
        sib = _flip(me, (0, 0, 1))
        local = []
        for a in range(na):
            slot = _dev_index(me) if a == 0 else me[2]
            cp = pltpu.make_async_copy(srcs[a], dsts[a].at[slot], loc_sems.at[a])
            cp.start()
            local.append(cp)

        def copy(a, k, sender, to):
            slot = _dev_index(sender) if a == 0 else sender[2]
            return pltpu.make_async_remote_copy(
                src_ref=srcs[a], dst_ref=dsts[a].at[slot], send_sem=send_sems.at[k], recv_sem=recv_sems.at[k],
                device_id=to, device_id_type=MESH)

        sends = [copy(0, mi, me, _flip(me, mask)) for mi, mask in enumerate(ALL_MASKS)]
        sends += [copy(a, nm + a - 1, me, sib) for a in range(1, na)]
        for cp in sends:
            cp.start()
        for mi, mask in enumerate(ALL_MASKS):
            copy(0, mi, _flip(me, mask), me).wait_recv()
        for a in range(1, na):
            copy(a, nm + a - 1, sib, me).wait_recv()
        for cp in sends:
            cp.wait_send()
        for cp in local:
            cp.wait()

    nsem = nm + na - 1
    return pl.pallas_call(
        body, name=name, in_specs=[ANY] * na, out_specs=[ANY] * na,
        out_shape=[jax.ShapeDtypeStruct((N_DEV,) + small.shape, small.dtype)]
        + [jax.ShapeDtypeStruct((2,) + a.shape, a.dtype) for a in pair_arrays],
        scratch_shapes=[pltpu.SemaphoreType.DMA((nsem,)), pltpu.SemaphoreType.DMA((nsem,)),
                        pltpu.SemaphoreType.DMA((na,))],
        compiler_params=pltpu.CompilerParams(has_side_effects=True),
    )(*arrays)


def _sum_parts(parts, name):
    _, r, c = parts.shape
    tr = r
    while tr * c * 4 * N_DEV > (8 << 20) and tr % 16 == 0:
        tr //= 2

    def body(p_ref, o_ref):
        total = p_ref[0]
        for d in range(1, N_DEV):
            total = total + p_ref[d]
        o_ref[...] = total

    return pl.pallas_call(
        body, name=name, grid=(r // tr,), in_specs=[pl.BlockSpec((N_DEV, tr, c), lambda i: (0, i, 0))],
        out_specs=pl.BlockSpec((tr, c), lambda i: (i, 0)), out_shape=jax.ShapeDtypeStruct((r, c), f32),
        compiler_params=_params(("parallel",)),
    )(parts)


def _adamw(w, g, m, v, name):
    shape = w.shape
    cols = shape[-1]
    rows = w.size // cols
    tr = rows
    while tr * cols * 4 > (1 << 20) and tr % 16 == 0:
        tr //= 2
    c1 = 1.0 / (1.0 - ADAM_B1 ** ADAM_STEP)
    c2 = 1.0 / (1.0 - ADAM_B2 ** ADAM_STEP)

    def body(w_ref, g_ref, m_ref, v_ref, d_ref, nm_ref, nv_ref):
        gv = g_ref[...]
        mv = ADAM_B1 * m_ref[...] + (1.0 - ADAM_B1) * gv
        vv = ADAM_B2 * v_ref[...] + (1.0 - ADAM_B2) * (gv * gv)
        nm_ref[...] = mv
        nv_ref[...] = vv
        d_ref[...] = -ADAM_LR * ((mv * c1) / (jnp.sqrt(vv * c2) + ADAM_EPS) + ADAM_WD * w_ref[...])

    spec = pl.BlockSpec((tr, cols), lambda i: (i, 0))
    outs = pl.pallas_call(
        body, name=name, grid=(rows // tr,), in_specs=[spec] * 4, out_specs=[spec] * 3,
        out_shape=[jax.ShapeDtypeStruct((rows, cols), f32)] * 3, compiler_params=_params(("parallel",)),
    )(*[t.reshape(rows, cols) for t in (w, g, m, v)])
    return tuple(t.reshape(shape) for t in outs)


def _to_heads(w, parts):
    r = w.shape[0]
    return w.reshape(r, parts, HEADS, HD).transpose(0, 2, 1, 3).reshape(r, parts * HEADS * HD)


def _from_heads(w, parts):
    r = w.shape[0]
    return w.reshape(r, HEADS, parts, HD).transpose(0, 2, 1, 3).reshape(r, parts * HEADS * HD)


def _pad_lanes(t, n=HD):
    return jnp.pad(t, [(0, 0)] * (t.ndim - 1) + [(0, n - t.shape[-1])])


def _rows128(t):
    return t.reshape(-1, HD)


def kernel(x, norm_w, w_in, lru_conv_w, lru_conv_b, lru_wa, lru_ba, lru_wx, lru_bx, lru_lambda, lru_norm_w, dn_conv_w, dn_A_log, dn_dt_bias, dn_norm_w, w_out, final_norm_w, loss_target, m_norm_w, m_w_in, m_lru_conv_w, m_lru_conv_b, m_lru_wa, m_lru_ba, m_lru_wx, m_lru_bx, m_lru_lambda, m_lru_norm_w, m_dn_conv_w, m_dn_A_log, m_dn_dt_bias, m_dn_norm_w, m_w_out, m_final_norm_w, v_norm_w, v_w_in, v_lru_conv_w, v_lru_conv_b, v_lru_wa, v_lru_ba, v_lru_wx, v_lru_bx, v_lru_lambda, v_lru_norm_w, v_dn_conv_w, v_dn_A_log, v_dn_dt_bias, v_dn_norm_w, v_w_out, v_final_norm_w):
    depth = norm_w.shape[0]
    xs = x[0]
    target = loss_target[0]
    chip = 2 * lax.axis_index("x") + lax.axis_index("y")

    g_in, g_out, g_lcw, g_dcw = _gather_over_chips(
        [w_in.astype(bf16), w_out.astype(bf16), lru_conv_w, dn_conv_w], "gather_weights")
    lcw_full = g_lcw.transpose(1, 2, 0, 3).reshape(depth, 4, D_MODEL)
    dcw_full = g_dcw.transpose(1, 2, 0, 3).reshape(depth, 4, 3 * D_MODEL)
    wp, wo, vecs, dnvs = [], [], [], []
    zrow = jnp.zeros((D_MODEL,), f32)
    for l in range(depth):
        wfull = jnp.concatenate([g_in[j, l] for j in range(4)], axis=1)
        wp.append(jnp.concatenate([_to_heads(wfull[:, :2 * D_MODEL], 2), _to_heads(wfull[:, 2 * D_MODEL:D_MAIN], 4),
                                   _pad_lanes(wfull[:, D_MAIN:])], axis=1))
        wo.append(g_out[:, l].reshape(D_MIX, D_MODEL))
        vecs.append(jnp.stack([lru_conv_b[l], lru_ba[l], lru_bx[l], lru_lambda[l], lru_norm_w[l], zrow, zrow, zrow]))
        dnvs.append(jnp.concatenate([_pad_lanes(dn_A_log[l][None]), _pad_lanes(dn_dt_bias[l][None]),
                                     dn_norm_w[l][None], jnp.zeros((5, HD), f32)], axis=0))

    loss_part, grad_x, d_fnw, g_nw, g_vec, g_wa, g_wx, g_lcw_, g_dcw_, g_dsc, g_win, g_wout = _local_step(
        xs, target, norm_w, final_norm_w, wp, wo, lcw_full, dcw_full, vecs, dnvs, lru_wa, lru_wx)
    loss = lax.psum(loss_part[0, 0], ("x", "y", "c"))
    grads = _reduce_grads(chip, d_fnw, g_nw, g_vec, g_wa, g_wx, g_lcw_, g_dcw_, g_dsc, g_win, g_wout)

    names = ["norm_w", "w_in", "lru_conv_w", "lru_conv_b", "lru_wa", "lru_ba", "lru_wx", "lru_bx", "lru_lambda",
             "lru_norm_w", "dn_conv_w", "dn_A_log", "dn_dt_bias", "dn_norm_w", "w_out", "final_norm_w"]
    ws = dict(zip(names, [norm_w, w_in, lru_conv_w, lru_conv_b, lru_wa, lru_ba, lru_wx, lru_bx, lru_lambda, lru_norm_w,
                          dn_conv_w, dn_A_log, dn_dt_bias, dn_norm_w, w_out, final_norm_w]))
    ms = dict(zip(names, [m_norm_w, m_w_in, m_lru_conv_w, m_lru_conv_b, m_lru_wa, m_lru_ba, m_lru_wx, m_lru_bx,
                          m_lru_lambda, m_lru_norm_w, m_dn_conv_w, m_dn_A_log, m_dn_dt_bias, m_dn_norm_w, m_w_out,
                          m_final_norm_w]))
    vs = dict(zip(names, [v_norm_w, v_w_in, v_lru_conv_w, v_lru_conv_b, v_lru_wa, v_lru_ba, v_lru_wx, v_lru_bx,
                          v_lru_lambda, v_lru_norm_w, v_dn_conv_w, v_dn_A_log, v_dn_dt_bias, v_dn_norm_w, v_w_out,
                          v_final_norm_w]))
    deltas, new_m, new_v = [], [], []
    for n in names:
        d, nm_, nv_ = _adamw(ws[n], grads[n], ms[n], vs[n], f"adamw_{n}")
        deltas.append(d)
        new_m.append(nm_)
        new_v.append(nv_)
    return (loss, grad_x[None], *[grads[n] for n in names], *deltas, *new_m, *new_v)


def _local_step(xs, target, norm_w, final_norm_w, wp, wo, lcw_full, dcw_full, vecs, dnvs, lru_wa, lru_wx):
    depth = norm_w.shape[0]
    saved = []
    h = xs
    for l in range(depth):
        hn = _rms_fwd(h, norm_w[l][None], f"rms_fwd{l}")
        proj = _matmul(hn, wp[l], tm=512, tn=896, tk=D_MODEL, name=f"in_proj{l}")
        y_lru, hl = _lru_fwd(proj, lcw_full[l], vecs[l], lru_wa[l], lru_wx[l], f"lru_fwd{l}")
        y_dn, o, states = _dn_fwd(proj, dcw_full[l], dnvs[l], f"dn_fwd{l}")
        y = jnp.concatenate([y_lru, y_dn], axis=1)
        out = _matmul(y, wo[l], tm=512, tn=D_MODEL, tk=D_MIX, res=h, name=f"out_proj{l}")
        saved.append((h, hn, proj, hl, o, states, y))
        h = out

    dh, d_fnw, loss_part = _final_loss(h, final_norm_w[None], target, "final_loss")

    g_nw, g_vec, g_wa, g_wx, g_lcw_, g_dcw_, g_dsc, g_win, g_wout = ([None] * depth for _ in range(9))
    for l in reversed(range(depth)):
        hin, hn, proj, hl, o, states, y = saved[l]
        dy = _matmul(dh, wo[l], tb=True, tm=512, tn=1024, tk=D_MODEL, name=f"d_y{l}")
        g_wout[l] = _matmul(y, dh, ta=True, tm=512, tn=D_MODEL, tk=512, name=f"d_wout{l}")
        dproj, g_lcw_[l], g_vec[l], g_wa[l], g_wx[l] = _lru_bwd(dy, proj, hl, lcw_full[l], vecs[l], lru_wa[l], lru_wx[l],
                                                               f"lru_bwd{l}")
        dproj, dba8, dcw8, g_dsc[l] = _dn_bwd(dy, proj, o, states, dcw_full[l], dnvs[l], dproj, f"dn_bwd{l}")
        g_dcw_[l] = dcw8.reshape(HEADS, 4, 3, HD).transpose(1, 2, 0, 3).reshape(4, 3 * D_MODEL)
        dba = _pad_lanes(jnp.concatenate([dba8[:, 0::HD], dba8[:, 1::HD]], axis=1))
        dhn = _matmul(dproj, wp[l], tb=True, tm=512, tn=D_MODEL, tk=768, name=f"d_hn_main{l}")
        dhn = _matmul(dba, wp[l][:, D_MAIN:], tb=True, tm=512, tn=D_MODEL, tk=HD, res=dhn, name=f"d_hn_ba{l}")
        dw_main = _matmul(hn, dproj, ta=True, tm=D_MODEL, tn=768, tk=512, name=f"d_win_main{l}")
        dw_ba = _matmul(hn, dba, ta=True, tm=D_MODEL, tn=HD, tk=512, name=f"d_win_ba{l}")
        g_win[l] = jnp.concatenate([_from_heads(dw_main[:, :2 * D_MODEL], 2), _from_heads(dw_main[:, 2 * D_MODEL:], 4),
                                    dw_ba[:, :D_IN - D_MAIN]], axis=1)
        dh, g_nw[l] = _rms_bwd(dhn, hin, norm_w[l][None], dh, f"rms_bwd{l}")
    return loss_part, dh, d_fnw, g_nw, g_vec, g_wa, g_wx, g_lcw_, g_dcw_, g_dsc, g_win, g_wout


def _reduce_grads(chip, d_fnw, g_nw, g_vec, g_wa, g_wx, g_lcw_, g_dcw_, g_dsc, g_win, g_wout):
    depth = len(g_nw)
    shard_in = D_IN // 4
    win_parts = jnp.stack([g_win[l].reshape(D_MODEL, 4, shard_in).transpose(1, 0, 2) for l in range(depth)])
    win_parts = win_parts.transpose(1, 0, 2, 3).reshape(N_DEV, D_MODEL, shard_in)
    wout_parts = jnp.stack([g_wout[l].reshape(4, D_MIX // 4, D_MODEL) for l in range(depth)])
    wout_parts = wout_parts.transpose(1, 0, 2, 3).reshape(N_DEV, D_MIX // 4, D_MODEL)
    vec_row = lambda k: jnp.concatenate([_rows128(g_vec[l][k]) for l in range(depth)])
    small = [
        jnp.concatenate([_rows128(g_nw[l]) for l in range(depth)]),
        vec_row(0), vec_row(1), vec_row(2), vec_row(3), vec_row(4),
        jnp.concatenate([_rows128(g_wa[l]) for l in range(depth)]),
        jnp.concatenate([_rows128(g_wx[l]) for l in range(depth)]),
        jnp.concatenate([g_dsc[l][0:1] for l in range(depth)]),
        jnp.concatenate([g_dsc[l][1:2] for l in range(depth)]),
        jnp.concatenate([g_dsc[l][2:3] for l in range(depth)]),
        _rows128(d_fnw),
        jnp.concatenate([_rows128(g_lcw_[l]) for l in range(depth)]),
        jnp.concatenate([_rows128(g_dcw_[l]) for l in range(depth)]),
    ]
    counts = [t.shape[0] for t in small]
    total = sum(counts)
    per = -(-total // (8 * N_DEV)) * 8
    small = jnp.concatenate(small + [jnp.zeros((per * N_DEV - total, HD), f32)]).reshape(N_DEV, per, HD)

    r_small, r_win, r_wout = _scatter_parts([small, win_parts, wout_parts], "scatter_grads")
    s_small = _sum_parts(r_small, "sum_small")
    s_win = _sum_parts(r_win, "sum_win")
    s_wout = _sum_parts(r_wout, "sum_wout")
    a_small, grad_w_in, grad_w_out = _share_reduced(s_small, [s_win, s_wout], "share_grads")
    a_small = a_small.reshape(N_DEV * per, HD)
    offs = [0]
    for c in counts:
        offs.append(offs[-1] + c)
    piece = lambda k: a_small[offs[k]:offs[k + 1]]
    grads = {
        "norm_w": piece(0).reshape(depth, D_MODEL), "lru_conv_b": piece(1).reshape(depth, D_MODEL),
        "lru_ba": piece(2).reshape(depth, D_MODEL), "lru_bx": piece(3).reshape(depth, D_MODEL),
        "lru_lambda": piece(4).reshape(depth, D_MODEL), "lru_norm_w": piece(5).reshape(depth, D_MODEL),
        "lru_wa": piece(6).reshape(depth, HEADS, HD, HD), "lru_wx": piece(7).reshape(depth, HEADS, HD, HD),
        "dn_A_log": piece(8)[:, :HEADS], "dn_dt_bias": piece(9)[:, :HEADS], "dn_norm_w": piece(10),
        "final_norm_w": piece(11).reshape(D_MODEL),
        "lru_conv_w": lax.dynamic_slice_in_dim(piece(12).reshape(depth, 4, D_MODEL), chip * (D_MODEL // 4), D_MODEL // 4, 2),
        "dn_conv_w": lax.dynamic_slice_in_dim(piece(13).reshape(depth, 4, 3 * D_MODEL), chip * (3 * D_MODEL // 4),
                                              3 * D_MODEL // 4, 2),
        "w_in": grad_w_in, "w_out": grad_w_out,
    }
    return grads
```

```python
import jax
import jax.numpy as jnp
from jax import lax
from jax.experimental import pallas as pl
from jax.experimental.pallas import tpu as pltpu

f32 = jnp.float32
bf16 = jnp.bfloat16
HI = lax.Precision.HIGHEST
MESH = pl.DeviceIdType.MESH

D_MODEL = 1024
HEADS = 8
HD = 128
CHUNK = 64
LRU_T = 128
SEQ_BLOCK = 1024
DN_SEQ_BLOCK = 512
DN_HP = 4
LRU_C = 8.0
D_IN = 6160
D_MAIN = 6144
D_INP = 6272
D_MIX = 2048
LRU_W = 2 * HD
DN_W = 4 * HD
DN_BLK0 = (HEADS * LRU_W) // DN_W
BA_BLK = D_MAIN // HD
EPS = 1e-6
QSCALE = HD ** -0.5
N_DEV = 8
VMEM_LIMIT = 56 * 1024 * 1024

ADAM_LR, ADAM_B1, ADAM_B2, ADAM_EPS, ADAM_WD, ADAM_STEP = 0.001, 0.9, 0.999, 1e-08, 0.01, 10


def _sig(x):
    return 1.0 / (1.0 + jnp.exp(-x))


def _log1p(u):
    w = 1.0 + u
    d = w - 1.0
    return jnp.where(d == 0.0, u, jnp.log(w) * (u / jnp.where(d == 0.0, 1.0, d)))


def _softplus(x):
    return jnp.maximum(x, 0.0) + _log1p(jnp.exp(-jnp.abs(x)))


def _expm1(x):
    t = x * (1.0 + x * (0.5 + x * (1.0 / 6 + x * (1.0 / 24 + x * (1.0 / 120 + x * (1.0 / 720))))))
    return jnp.where(jnp.abs(x) < 0.2, t, jnp.exp(x) - 1.0)


def _dot(a, b, prec=None):
    return jnp.dot(a, b, precision=prec, preferred_element_type=f32)


def _dot_nt(a, b, prec=None):
    return lax.dot_general(a, b, (((1,), (1,)), ((), ())), precision=prec, preferred_element_type=f32)


def _dot_tn(a, b, prec=None):
    return lax.dot_general(a, b, (((0,), (0,)), ((), ())), precision=prec, preferred_element_type=f32)


def _b(x):
    return x.astype(bf16)


def _sum0(x):
    return jnp.sum(x, axis=0, keepdims=True)


def _sum1(x):
    return jnp.sum(x, axis=1, keepdims=True)


def _rowmean(x):
    return jnp.mean(x, axis=-1, keepdims=True)


def _dsilu(z, sg):
    return sg * (1.0 + z * (1.0 - sg))


def _conv_taps(x, halo):
    xx = jnp.concatenate([halo, x], axis=0)
    return [pltpu.roll(xx, 3, axis=0)[8:], pltpu.roll(xx, 2, axis=0)[8:], pltpu.roll(xx, 1, axis=0)[8:], x]


def _conv(xs, cw):
    return cw[0:1] * xs[0] + cw[1:2] * xs[1] + cw[2:3] * xs[2] + cw[3:4] * xs[3]


def _rows_before(ref, halo_ref, lanes, t0, c, sblk):
    tprev = pl.multiple_of(jnp.maximum(t0 - 8, 0), 8)
    return jnp.where(c > 0, ref[pl.ds(tprev, 8), lanes], jnp.where(sblk > 0, halo_ref[:, lanes], 0.0))


def _conv_back(dxc, halo_after, cw):
    rows = dxc.shape[0]
    dd = jnp.concatenate([dxc, halo_after], axis=0)
    out = cw[3:4] * dxc
    for s in (1, 2, 3):
        out = out + cw[3 - s:4 - s] * pltpu.roll(dd, rows + 8 - s, axis=0)[:rows]
    return out


def _params(sem=None):
    return pltpu.CompilerParams(dimension_semantics=sem, vmem_limit_bytes=VMEM_LIMIT)


def _seq_specs(sb, nsb, width, colblk, reverse):
    blk = (lambda i: nsb - 1 - i) if reverse else (lambda i: i)
    main = pl.BlockSpec((sb, width), lambda h, i: (blk(i), colblk(h)))
    halo = pl.BlockSpec((8, width), lambda h, i: (jnp.maximum(blk(i) * (sb // 8) - 1, 0), colblk(h)))
    return main, halo


def _matmul(a, b, *, ta=False, tb=False, tm, tn, tk, res=None, out_dtype=f32, name):
    if ta:
        kdim, m = a.shape
    else:
        m, kdim = a.shape
    n = b.shape[0] if tb else b.shape[1]
    tm, tn, tk = min(tm, m), min(tn, n), min(tk, kdim)
    assert m % tm == 0 and n % tn == 0 and kdim % tk == 0, (name, m, n, kdim, tm, tn, tk)
    nk = kdim // tk
    dims = (((0 if ta else 1,), (1 if tb else 0,)), ((), ()))
    has_res = res is not None

    def body(*refs):
        a_ref, b_ref = refs[:2]
        r_ref = refs[2] if has_res else None
        o_ref = refs[3] if has_res else refs[2]
        acc_ref = refs[-1] if nk > 1 else None
        part = lax.dot_general(_b(a_ref[...]), _b(b_ref[...]), dims, preferred_element_type=f32)

        def finish(total):
            if has_res:
                total = total + r_ref[...]
            o_ref[...] = total.astype(out_dtype)

        if nk == 1:
            finish(part)
        else:
            k = pl.program_id(2)

            @pl.when(k == 0)
            def _():
                acc_ref[...] = part

            @pl.when(k > 0)
            def _():
                acc_ref[...] += part

            @pl.when(k == nk - 1)
            def _():
                finish(acc_ref[...])

    a_spec = pl.BlockSpec((tk, tm), lambda i, j, k: (k, i)) if ta else pl.BlockSpec((tm, tk), lambda i, j, k: (i, k))
    b_spec = pl.BlockSpec((tn, tk), lambda i, j, k: (j, k)) if tb else pl.BlockSpec((tk, tn), lambda i, j, k: (k, j))
    o_spec = pl.BlockSpec((tm, tn), lambda i, j, k: (i, j))
    in_specs, args = [a_spec, b_spec], [a, b]
    if has_res:
        in_specs.append(o_spec)
        args.append(res)
    return pl.pallas_call(
        body, name=name, grid=(m // tm, n // tn, nk), in_specs=in_specs, out_specs=o_spec,
        out_shape=jax.ShapeDtypeStruct((m, n), out_dtype),
        scratch_shapes=[pltpu.VMEM((tm, tn), f32)] if nk > 1 else [],
        compiler_params=_params(("parallel", "parallel", "arbitrary")),
    )(*args)


def _rms_fwd(x, w, name):
    s = x.shape[0]
    tr = min(512, s)

    def body(x_ref, w_ref, h_ref):
        xv = x_ref[...]
        r = lax.rsqrt(_rowmean(xv * xv) + EPS)
        h_ref[...] = (xv * r * w_ref[...]).astype(bf16)

    return pl.pallas_call(
        body, name=name, grid=(s // tr,),
        in_specs=[pl.BlockSpec((tr, D_MODEL), lambda i: (i, 0)), pl.BlockSpec((1, D_MODEL), lambda i: (0, 0))],
        out_specs=pl.BlockSpec((tr, D_MODEL), lambda i: (i, 0)),
        out_shape=jax.ShapeDtypeStruct((s, D_MODEL), bf16), compiler_params=_params(("parallel",)),
    )(x, w)


def _rms_bwd(dh, x, w, dres, name):
    s = x.shape[0]
    tr = min(512, s)

    def body(dh_ref, x_ref, w_ref, dres_ref, dx_ref, dw_ref):
        xv = x_ref[...]
        r = lax.rsqrt(_rowmean(xv * xv) + EPS)
        xn = xv * r
        d = dh_ref[...]
        dxn = d * w_ref[...]
        dx_ref[...] = dres_ref[...] + r * (dxn - xn * _rowmean(dxn * xn))
        part = _sum0(d * xn)

        @pl.when(pl.program_id(0) == 0)
        def _():
            dw_ref[...] = part

        @pl.when(pl.program_id(0) > 0)
        def _():
            dw_ref[...] += part

    row = pl.BlockSpec((tr, D_MODEL), lambda i: (i, 0))
    vec = pl.BlockSpec((1, D_MODEL), lambda i: (0, 0))
    return pl.pallas_call(
        body, name=name, grid=(s // tr,), in_specs=[row, row, vec, row], out_specs=[row, vec],
        out_shape=[jax.ShapeDtypeStruct((s, D_MODEL), f32), jax.ShapeDtypeStruct((1, D_MODEL), f32)],
        compiler_params=_params(("arbitrary",)),
    )(dh, x, w, dres)


def _final_loss(x, w, target, name):
    s = x.shape[0]
    tr = min(512, s)

    def body(x_ref, w_ref, t_ref, dx_ref, dw_ref, loss_ref):
        xv = x_ref[...]
        wv = w_ref[...]
        r = lax.rsqrt(_rowmean(xv * xv) + EPS)
        xn = xv * r
        e = xn * wv - t_ref[...]
        lb = jnp.broadcast_to(0.5 * _sum0(_rowmean(e * e)), (1, HD))
        dy = e * (1.0 / D_MODEL)
        dxn = dy * wv
        dx_ref[...] = r * (dxn - xn * _rowmean(dxn * xn))
        part = _sum0(dy * xn)

        @pl.when(pl.program_id(0) == 0)
        def _():
            dw_ref[...] = part
            loss_ref[...] = lb

        @pl.when(pl.program_id(0) > 0)
        def _():
            dw_ref[...] += part
            loss_ref[...] += lb

    row = pl.BlockSpec((tr, D_MODEL), lambda i: (i, 0))
    vec = pl.BlockSpec((1, D_MODEL), lambda i: (0, 0))
    lsp = pl.BlockSpec((1, HD), lambda i: (0, 0))
    return pl.pallas_call(
        body, name=name, grid=(s // tr,), in_specs=[row, vec, row], out_specs=[row, vec, lsp],
        out_shape=[jax.ShapeDtypeStruct((s, D_MODEL), f32), jax.ShapeDtypeStruct((1, D_MODEL), f32),
                   jax.ShapeDtypeStruct((1, HD), f32)],
        compiler_params=_params(("arbitrary",)),
    )(x, w, target)


LX = slice(0, HD)
LZ = slice(HD, 2 * HD)


def _lru_gates(xc, vec, wa, wx):
    sp = _softplus(-vec[3:4])
    xcb = _b(xc)
    r = _sig(_dot(xcb, wa) + vec[1:2])
    i = _sig(_dot(xcb, wx) + vec[2:3])
    la = (-LRU_C) * r * sp
    a = jnp.exp(la)
    mult = jnp.sqrt(-_expm1(2.0 * la))
    return r, i, a, mult, sp, xcb


def _lru_fwd(proj, cw, vec, wa, wx, name):
    s = proj.shape[0]
    sb = min(SEQ_BLOCK, s)
    nsb = s // sb
    t = min(LRU_T, sb)
    nc = sb // t

    def body(p_ref, ph_ref, cw_ref, vec_ref, wa_ref, wx_ref, y_ref, hl_ref, hc_ref):
        sblk = pl.program_id(1)
        cwv = cw_ref[...]
        vec_v = vec_ref[...]
        wa_v = _b(wa_ref[...])
        wx_v = _b(wx_ref[...])
        row = lax.broadcasted_iota(jnp.int32, (t, HD), 0)

        @pl.when(sblk == 0)
        def _():
            hc_ref[...] = jnp.zeros((8, HD), f32)

        def chunk(c, hcarry):
            t0 = pl.multiple_of(c * t, t)
            xs = _conv_taps(p_ref[pl.ds(t0, t), LX], _rows_before(p_ref, ph_ref, LX, t0, c, sblk))
            xc = vec_v[0:1] + _conv(xs, cwv)
            r, i, a, mult, _, _ = _lru_gates(xc, vec_v, wa_v, wx_v)
            bb = mult * (i * xc)
            d = 1
            while d < t:
                m = row >= d
                bb = jnp.where(m, a * pltpu.roll(bb, d, axis=0) + bb, bb)
                a = jnp.where(m, a * pltpu.roll(a, d, axis=0), a)
                d *= 2
            hl = bb + a * hcarry
            hl_ref[pl.ds(t0, t), :] = hl
            z = p_ref[pl.ds(t0, t), LZ]
            nrm = lax.rsqrt(_rowmean(hl * hl) + EPS)
            y_ref[pl.ds(t0, t), :] = (hl * nrm * vec_v[4:5] * (z * _sig(z))).astype(bf16)
            return hl[t - 1:t, :]

        hlast = lax.fori_loop(0, nc, chunk, hc_ref[0:1, :])
        hc_ref[...] = jnp.broadcast_to(hlast, (8, HD))

    main, halo = _seq_specs(sb, nsb, LRU_W, lambda h: h, False)
    hcol = lambda h, i: (0, h)
    wsp = pl.BlockSpec((None, HD, HD), lambda h, i: (h, 0, 0))
    osp = pl.BlockSpec((sb, HD), lambda h, i: (i, h))
    return pl.pallas_call(
        body, name=name, grid=(HEADS, nsb),
        in_specs=[main, halo, pl.BlockSpec((4, HD), hcol), pl.BlockSpec((8, HD), hcol), wsp, wsp],
        out_specs=[osp, osp],
        out_shape=[jax.ShapeDtypeStruct((s, D_MIX), bf16), jax.ShapeDtypeStruct((s, D_MODEL), f32)],
        scratch_shapes=[pltpu.VMEM((8, HD), f32)],
        compiler_params=_params(("parallel", "arbitrary")),
    )(proj, proj, cw, vec, wa, wx)


def _lru_bwd(dy, proj, hl, cw, vec, wa, wx, name):
    s = proj.shape[0]
    sb = min(SEQ_BLOCK, s)
    nsb = s // sb
    t = min(LRU_T, sb)
    nc = sb // t

    def body(dy_ref, p_ref, ph_ref, hl_ref, hlh_ref, cw_ref, vec_ref, wa_ref, wx_ref,
             dp_ref, dcw_ref, dvec_ref, dwa_ref, dwx_ref, acc_ref, dxh_ref):
        step = pl.program_id(1)
        sblk = nsb - 1 - step
        cwv = cw_ref[...]
        vec_v = vec_ref[...]
        wa_v = _b(wa_ref[...])
        wx_v = _b(wx_ref[...])
        lnw = vec_v[4:5]
        row = lax.broadcasted_iota(jnp.int32, (t, HD), 0)

        @pl.when(step == 0)
        def _():
            acc_ref[...] = jnp.zeros((16, HD), f32)
            dxh_ref[...] = jnp.zeros((8, HD), f32)
            dwa_ref[...] = jnp.zeros((HD, HD), f32)
            dwx_ref[...] = jnp.zeros((HD, HD), f32)

        def chunk(ci, carry):
            mu, dxc_halo, dcw0, dcw1, dcw2, dcw3, dcb, dba, dbx, dsp, dlnw = carry
            c = nc - 1 - ci
            t0 = pl.multiple_of(c * t, t)
            xs = _conv_taps(p_ref[pl.ds(t0, t), LX], _rows_before(p_ref, ph_ref, LX, t0, c, sblk))
            xc = vec_v[0:1] + _conv(xs, cwv)
            r, i, a, mult, sp, xcb = _lru_gates(xc, vec_v, wa_v, wx_v)
            hv = hl_ref[pl.ds(t0, t), :]
            hhalo = _rows_before(hl_ref, hlh_ref, slice(None), t0, c, sblk)
            hprev = pltpu.roll(jnp.concatenate([hhalo, hv], axis=0), 1, axis=0)[8:]
            z = p_ref[pl.ds(t0, t), LZ]
            sgz = _sig(z)
            sz = z * sgz
            dyv = dy_ref[pl.ds(t0, t), :]
            nrm = lax.rsqrt(_rowmean(hv * hv) + EPS)
            hn = hv * nrm
            dlnw = dlnw + _sum0(dyv * hn * sz)
            dp_ref[pl.ds(t0, t), LZ] = dyv * hn * lnw * _dsilu(z, sgz)
            dhn = dyv * lnw * sz
            lam = nrm * (dhn - hn * _rowmean(dhn * hn))
            cf = jnp.where(row == t - 1, 1.0, pltpu.roll(a, t - 1, axis=0))
            d = 1
            while d < t:
                m = row < t - d
                lam = jnp.where(m, lam + cf * pltpu.roll(lam, t - d, axis=0), lam)
                cf = jnp.where(m, cf * pltpu.roll(cf, t - d, axis=0), cf)
                d *= 2
            lam = lam + cf * mu
            mu = a[0:1] * lam[0:1]
            da = lam * hprev
            dmult = lam * (i * xc)
            di = lam * mult * xc
            dxc = lam * mult * i
            dla = da * a - dmult * (a * a) / mult
            dr = dla * (-LRU_C * sp)
            dsp = dsp + _sum0(dla * (-LRU_C * r))
            dra = dr * r * (1.0 - r)
            dia = di * i * (1.0 - i)
            dba = dba + _sum0(dra)
            dbx = dbx + _sum0(dia)
            drab, diab = _b(dra), _b(dia)
            dwa_ref[...] += _dot_tn(xcb, drab)
            dwx_ref[...] += _dot_tn(xcb, diab)
            dxc = dxc + _dot_nt(drab, wa_v) + _dot_nt(diab, wx_v)
            dcb = dcb + _sum0(dxc)
            dcw0 = dcw0 + _sum0(dxc * xs[0])
            dcw1 = dcw1 + _sum0(dxc * xs[1])
            dcw2 = dcw2 + _sum0(dxc * xs[2])
            dcw3 = dcw3 + _sum0(dxc * xs[3])
            dp_ref[pl.ds(t0, t), LX] = _conv_back(dxc, dxc_halo, cwv)
            return (mu, dxc[0:8], dcw0, dcw1, dcw2, dcw3, dcb, dba, dbx, dsp, dlnw)

        acc = acc_ref[...]
        init = (acc[9:10], dxh_ref[...]) + tuple(acc[k:k + 1] for k in range(9))
        mu, dxh, dcw0, dcw1, dcw2, dcw3, dcb, dba, dbx, dsp, dlnw = lax.fori_loop(0, nc, chunk, init)
        zero = jnp.zeros((1, HD), f32)
        acc_ref[...] = jnp.concatenate([dcw0, dcw1, dcw2, dcw3, dcb, dba, dbx, dsp, dlnw, mu] + [zero] * 6, axis=0)
        dxh_ref[...] = dxh

        @pl.when(step == nsb - 1)
        def _():
            dcw_ref[...] = jnp.concatenate([dcw0, dcw1, dcw2, dcw3], axis=0)
            dlam = -dsp * _sig(-vec_v[3:4])
            dvec_ref[...] = jnp.concatenate([dcb, dba, dbx, dlam, dlnw, zero, zero, zero], axis=0)

    main, halo = _seq_specs(sb, nsb, LRU_W, lambda h: h, True)
    hmain, hhalo = _seq_specs(sb, nsb, HD, lambda h: h, True)
    hcol = lambda h, i: (0, h)
    wsp = pl.BlockSpec((None, HD, HD), lambda h, i: (h, 0, 0))
    return pl.pallas_call(
        body, name=name, grid=(HEADS, nsb),
        in_specs=[hmain, main, halo, hmain, hhalo, pl.BlockSpec((4, HD), hcol), pl.BlockSpec((8, HD), hcol), wsp, wsp],
        out_specs=[main, pl.BlockSpec((4, HD), hcol), pl.BlockSpec((8, HD), hcol), wsp, wsp],
        out_shape=[jax.ShapeDtypeStruct((s, D_MAIN), f32), jax.ShapeDtypeStruct((4, D_MODEL), f32),
                   jax.ShapeDtypeStruct((8, D_MODEL), f32),
                   jax.ShapeDtypeStruct((HEADS, HD, HD), f32), jax.ShapeDtypeStruct((HEADS, HD, HD), f32)],
        scratch_shapes=[pltpu.VMEM((16, HD), f32), pltpu.VMEM((8, HD), f32)],
        compiler_params=_params(("parallel", "arbitrary")),
    )(dy, proj, proj, hl, hl, cw, vec, wa, wx)


def _lane_pick(x, lane, idx):
    return _sum1(jnp.where(lane == idx, x, 0.0))


def _round_robin(gens):
    results = [None] * len(gens)
    live = list(range(len(gens)))
    while live:
        for i in list(live):
            try:
                next(gens[i])
            except StopIteration as done:
                results[i] = done.value
                live.remove(i)
    return results


def _sdot(a, b):
    return _dot(_b(a), _b(b))


def _sdot_tn(a, b):
    return _dot_tn(_b(a), _b(b))


def _sdot_nt(a, b):
    return _dot_nt(_b(a), _b(b))


def _inv_unit_lower(a):
    n = -a
    p = _sdot(a, a)
    yield
    for k in range(5):
        n = n + p + _sdot(n, p)
        if k < 4:
            p = _sdot(p, p)
        yield
    return n


def _dn_chunk(x3, halo3, bav, cw3, dnv, h, masks):
    lane, ri, ci, eye, ltri = masks
    xs = _conv_taps(x3, halo3)
    cpre = _conv(xs, cw3)
    sg = _sig(cpre)
    act = cpre * sg
    q_raw, k_raw, v = act[:, 0:HD], act[:, HD:2 * HD], act[:, 2 * HD:3 * HD]
    rq = lax.rsqrt(_sum1(q_raw * q_raw) + EPS)
    rk = lax.rsqrt(_sum1(k_raw * k_raw) + EPS)
    qn = q_raw * rq
    k = k_raw * rk
    q = qn * QSCALE
    b_in = jnp.broadcast_to(_lane_pick(bav, lane, h), (CHUNK, HD))
    a_in = jnp.broadcast_to(_lane_pick(bav, lane, HEADS + h), (CHUNK, HD))
    lane1 = lane[0:1]
    ea = jnp.exp(_lane_pick(dnv[0:1], lane1, h))
    dt = _lane_pick(dnv[1:2], lane1, h)
    beta = _sig(b_in)
    sp = _softplus(a_in + dt)
    g = -ea * sp
    gc = _dot(ltri, g, HI)
    yield
    gc64 = gc[:, 0:CHUNK]
    grow = _sum0(gc64 * eye)
    dm = jnp.exp(jnp.where(ri >= ci, gc64 - grow, -1e30))
    ds_ = jnp.where(ri > ci, dm, 0.0)
    eg = jnp.exp(gc)
    glast = gc[CHUNK - 1:CHUNK, :]
    ek = jnp.exp(glast - gc)
    kb = k * beta
    kbf = _b(k)
    amat = _dot_nt(_b(kb), kbf) * ds_
    pmat = _dot_nt(_b(q), kbf) * dm
    yield
    tinv = yield from _inv_unit_lower(amat)
    return dict(xs=xs, cpre=cpre, sg=sg, rq=rq, rk=rk, qn=qn, q=q, k=k, v=v, kbf=kbf, b_in=b_in, a_in=a_in, ea=ea,
                dt=dt, beta=beta, g=g, gc=gc, dm=dm, ds=ds_, eg=eg, glast=glast, ek=ek, kb=kb, amat=amat,
                tinv=tinv, pmat=pmat)


def _dn_masks():
    lane = lax.broadcasted_iota(jnp.int32, (CHUNK, HD), 1)
    ri = lax.broadcasted_iota(jnp.int32, (CHUNK, CHUNK), 0)
    ci = lax.broadcasted_iota(jnp.int32, (CHUNK, CHUNK), 1)
    eye = (ri == ci).astype(f32)
    ltri = (ri >= ci).astype(f32)
    return lane, ri, ci, eye, ltri


def _dn_fwd(proj, cw, dnv, y, name):
    s = proj.shape[0]
    sb = min(DN_SEQ_BLOCK, s)
    nsb = s // sb
    nc = sb // CHUNK
    hp = DN_HP

    def body(p_ref, ph_ref, ba_ref, cwq_ref, cwk_ref, cwv_ref, dnv_ref, yin_ref, y_ref, o_ref, st_ref, s_ref):
        del yin_ref
        grp = pl.program_id(0)
        sblk = pl.program_id(1)
        masks = _dn_masks()
        dnv_v = dnv_ref[...]
        cw3 = [jnp.concatenate([r[:, j * HD:(j + 1) * HD] for r in (cwq_ref, cwk_ref, cwv_ref)], axis=1)
               for j in range(hp)]

        @pl.when(sblk == 0)
        def _():
            s_ref[...] = jnp.zeros((hp, HD, HD), f32)

        def one_head(j, x3, halo3, bav, z, st):
            f = yield from _dn_chunk(x3, halo3, bav, cw3[j], dnv_v, grp * hp + j, masks)
            sbf = _b(st)
            qs = _dot(_b(f["q"] * f["eg"]), sbf)
            rhs = f["beta"] * (f["v"] - _dot(_b(f["k"] * f["eg"]), sbf))
            yield
            vn = rhs + _sdot(f["tinv"], rhs)
            yield
            vnb = _b(vn)
            o = qs + _dot(_b(f["pmat"]), vnb)
            st_new = st * jnp.exp(f["glast"]) + _dot_tn(_b(f["k"] * f["ek"]), vnb)
            yield
            nrm = lax.rsqrt(_rowmean(o * o) + EPS)
            y = (o * nrm * dnv_v[2:3] * (z * _sig(z))).astype(bf16)
            return o, y, st_new

        def chunk(c, states):
            t0 = pl.multiple_of(c * CHUNK, CHUNK)
            rows = pl.ds(t0, CHUNK)
            bav = ba_ref[rows, :]
            ins = []
            for j in range(hp):
                qkv = slice(j * DN_W, j * DN_W + 3 * HD)
                ins.append((p_ref[rows, qkv], _rows_before(p_ref, ph_ref, qkv, t0, c, sblk),
                            p_ref[rows, j * DN_W + 3 * HD:(j + 1) * DN_W]))
            outs = _round_robin([one_head(j, ins[j][0], ins[j][1], bav, ins[j][2], states[j]) for j in range(hp)])
            for j in range(hp):
                hl = slice(j * HD, (j + 1) * HD)
                st_ref[j, c] = states[j]
                o_ref[rows, hl] = outs[j][0]
                y_ref[rows, hl] = outs[j][1]
            return tuple(out[2] for out in outs)

        final = lax.fori_loop(0, nc, chunk, tuple(s_ref[j] for j in range(hp)))
        for j in range(hp):
            s_ref[j] = final[j]

    main, halo = _seq_specs(sb, nsb, hp * DN_W, lambda g: DN_BLK0 // hp + g, False)
    basp = pl.BlockSpec((sb, HD), lambda g, i: (i, BA_BLK))
    cws = lambda off: pl.BlockSpec((4, hp * HD), lambda g, i: (0, off // hp + g))
    osp = lambda off: pl.BlockSpec((sb, hp * HD), lambda g, i: (i, off // hp + g))
    return pl.pallas_call(
        body, name=name, grid=(HEADS // hp, nsb),
        in_specs=[main, halo, basp, cws(0), cws(HEADS), cws(2 * HEADS), pl.BlockSpec((8, HD), lambda g, i: (0, 0)),
                  pl.BlockSpec(memory_space=pl.ANY)],
        out_specs=[osp(HEADS), osp(0), pl.BlockSpec((hp, nc, HD, HD), lambda g, i: (g, i, 0, 0))],
        out_shape=[jax.ShapeDtypeStruct((s, D_MIX), bf16), jax.ShapeDtypeStruct((s, D_MODEL), f32),
                   jax.ShapeDtypeStruct((HEADS, s // CHUNK, HD, HD), f32)],
        scratch_shapes=[pltpu.VMEM((hp, HD, HD), f32)],
        input_output_aliases={7: 0},
        compiler_params=_params(("parallel", "arbitrary")),
    )(proj, proj, proj, cw, cw, cw, dnv, y)


def _dn_bwd(dy, proj, o, states, cw, dnv, dproj, name):
    s = proj.shape[0]
    sb = min(DN_SEQ_BLOCK, s)
    nsb = s // sb
    nc = sb // CHUNK
    hp = DN_HP

    def body(dy_ref, p_ref, ph_ref, ba_ref, o_ref, st_ref, cwq_ref, cwk_ref, cwv_ref, dnv_ref, dpin_ref,
             dp_ref, dba_ref, dcw_ref, dsc_ref, ds_ref, dch_ref, cwacc_ref, sacc_ref, nacc_ref):
        del dpin_ref
        grp = pl.program_id(0)
        step = pl.program_id(1)
        sblk = nsb - 1 - step
        masks = _dn_masks()
        lane, ri, ci, eye, ltri = masks
        utri = (ri <= ci).astype(f32)
        cw3s = [jnp.concatenate([r[:, j * HD:(j + 1) * HD] for r in (cwq_ref, cwk_ref, cwv_ref)], axis=1)
                for j in range(hp)]
        dnv_v = dnv_ref[...]
        dnw = dnv_v[2:3]
        row64 = lax.broadcasted_iota(jnp.int32, (CHUNK, HD), 0)

        @pl.when(step == 0)
        def _():
            ds_ref[...] = jnp.zeros((hp, HD, HD), f32)
            dch_ref[...] = jnp.zeros((hp, 8, 3 * HD), f32)
            cwacc_ref[...] = jnp.zeros((hp, 8, 3 * HD), f32)
            sacc_ref[...] = jnp.zeros((hp, 8, HD), f32)

        @pl.when((step == 0) & (grp == 0))
        def _():
            nacc_ref[...] = jnp.zeros((8, HD), f32)

        def one_head(j, x3, halo3, bav, z, st, ov, dyv, carry):
            dst, dch, cwacc, sa0, sa1 = carry
            cw3 = cw3s[j]
            f = yield from _dn_chunk(x3, halo3, bav, cw3, dnv_v, grp * hp + j, masks)
            q, k, v, beta, eg, ek, kbf = f["q"], f["k"], f["v"], f["beta"], f["eg"], f["ek"], f["kbf"]
            sbf = _b(st)
            dstb = _b(dst)
            qe = q * eg
            keg = k * eg
            kd = k * ek
            kegb, qeb, kdb = _b(keg), _b(qe), _b(kd)
            e = v - _dot(kegb, sbf)
            yield
            rhs = beta * e
            vn = rhs + _sdot(f["tinv"], rhs)
            yield
            vnb = _b(vn)
            pb = _b(f["pmat"])
            sgz = _sig(z)
            sz = z * sgz
            nrm = lax.rsqrt(_rowmean(ov * ov) + EPS)
            on = ov * nrm
            ddnw = _sum0(dyv * on * sz)
            dpz = dyv * on * dnw * _dsilu(z, sgz)
            don = dyv * dnw * sz
            do = nrm * (don - on * _rowmean(don * on))
            dob = _b(do)
            dvn = _dot_tn(pb, dob) + _dot(kdb, dstb)
            dpm = jnp.where(ri >= ci, _dot_nt(dob, vnb), 0.0)
            dqe = _dot_nt(dob, sbf)
            dkd = _dot_nt(vnb, dstb)
            qtdo = _dot_tn(qeb, dob)
            yield
            drhs = dvn + _sdot_tn(f["tinv"], dvn)
            dqk = _b(dpm * f["dm"])
            dq = _dot(dqk, kbf) + dqe * eg
            dk_p = _dot_tn(dqk, _b(q))
            yield
            dam = jnp.where(ri > ci, -_sdot_nt(drhs, vn), 0.0)
            de = drhs * beta
            deb = _b(de)
            dbeta = _sum1(drhs * e)
            dkeg = -_dot_nt(deb, sbf)
            dst_new = qtdo + dst * jnp.exp(f["glast"]) - _dot_tn(kegb, deb)
            yield
            dkk = _b(dam * f["ds"])
            dkb = _dot(dkk, kbf)
            dk = dk_p + _dot_tn(dkk, _b(f["kb"])) + dkb * beta + dkeg * eg + dkd * ek
            yield
            dbeta = dbeta + _sum1(dkb * k)
            mm = dpm * f["pmat"] + dam * f["amat"]
            dglast = _sum1(_sum0(dst * st)) * jnp.exp(f["glast"]) + _sum1(_sum0(dkd * kd))
            dgc = (_sum1(mm) - _sum1(eye * _sum0(mm)) + _sum1(dqe * qe) + _sum1(dkeg * keg) - _sum1(dkd * kd))
            dgc = jnp.broadcast_to(dgc, (CHUNK, HD)) + jnp.where(row64 == CHUNK - 1, dglast, 0.0)
            dg = _dot(utri, dgc, HI)
            yield
            dbin = jnp.broadcast_to(dbeta, (CHUNK, HD)) * beta * (1.0 - beta)
            dain = dg * (-f["ea"]) * _sig(f["a_in"] + f["dt"])
            dba = jnp.where(lane == 0, dbin, 0.0) + jnp.where(lane == 1, dain, 0.0)
            sa0 = sa0 + _sum0(dg * f["g"])
            sa1 = sa1 + _sum0(dain)
            dq_raw = (QSCALE * f["rq"]) * (dq - f["qn"] * _sum1(f["qn"] * dq))
            dk_raw = f["rk"] * (dk - k * _sum1(k * dk))
            dact = jnp.concatenate([dq_raw, dk_raw, de], axis=1)
            dc = dact * _dsilu(f["cpre"], f["sg"])
            xs = f["xs"]
            cwacc = cwacc + jnp.concatenate([_sum0(dc * xs[0]), _sum0(dc * xs[1]), _sum0(dc * xs[2]),
                                             _sum0(dc * xs[3])], axis=0)
            dpqkv = _conv_back(dc, dch, cw3)
            return dpz, dpqkv, dba, ddnw, (dst_new, dc[0:8], cwacc, sa0, sa1)

        def chunk(cidx, carry):
            heads, nacc = carry
            c = nc - 1 - cidx
            t0 = pl.multiple_of(c * CHUNK, CHUNK)
            rows = pl.ds(t0, CHUNK)
            bav = ba_ref[rows, :]
            ins = []
            for j in range(hp):
                hl = slice(j * HD, (j + 1) * HD)
                qkv = slice(j * DN_W, j * DN_W + 3 * HD)
                ins.append((p_ref[rows, qkv], _rows_before(p_ref, ph_ref, qkv, t0, c, sblk), bav,
                            p_ref[rows, j * DN_W + 3 * HD:(j + 1) * DN_W], st_ref[j, c], o_ref[rows, hl],
                            dy_ref[rows, hl]))
            outs = _round_robin([one_head(j, *ins[j], heads[j]) for j in range(hp)])
            for j in range(hp):
                dpz, dpqkv, dba, ddnw, _ = outs[j]
                dp_ref[rows, j * DN_W:j * DN_W + 3 * HD] = dpqkv
                dp_ref[rows, j * DN_W + 3 * HD:(j + 1) * DN_W] = dpz
                dba_ref[rows, j * HD:(j + 1) * HD] = dba
                nacc = nacc + ddnw
            return tuple(out[4] for out in outs), nacc

        init = tuple((ds_ref[j], dch_ref[j], cwacc_ref[j, 0:4, :], sacc_ref[j, 0:1, :], sacc_ref[j, 1:2, :])
                     for j in range(hp))
        heads, nacc = lax.fori_loop(0, nc, chunk, (init, nacc_ref[0:1, :]))
        nacc_ref[0:1, :] = nacc
        for j in range(hp):
            ds_ref[j], dch_ref[j], cwacc_ref[j, 0:4, :], sacc_ref[j, 0:1, :], sacc_ref[j, 1:2, :] = heads[j]

        @pl.when(step == nsb - 1)
        def _():
            lane8 = lax.broadcasted_iota(jnp.int32, (8, HD), 1)
            row8 = lax.broadcasted_iota(jnp.int32, (8, HD), 0)
            mine = jnp.zeros((8, HD), f32)
            for j in range(hp):
                dcw_ref[j] = cwacc_ref[j, 0:4, :]
                mine = mine + jnp.where((lane8 == grp * hp + j) & (row8 < 2), sacc_ref[j], 0.0)

            @pl.when(grp == 0)
            def _():
                dsc_ref[...] = mine

            @pl.when(grp > 0)
            def _():
                dsc_ref[...] += mine

            @pl.when(grp == HEADS // hp - 1)
            def _():
                dsc_ref[2:3, :] = nacc_ref[0:1, :]

    main, halo = _seq_specs(sb, nsb, hp * DN_W, lambda g: DN_BLK0 // hp + g, True)
    rblk = lambda i: nsb - 1 - i
    basp = pl.BlockSpec((sb, HD), lambda g, i: (rblk(i), BA_BLK))
    cws = lambda off: pl.BlockSpec((4, hp * HD), lambda g, i: (0, off // hp + g))
    hsp = lambda off: pl.BlockSpec((sb, hp * HD), lambda g, i: (rblk(i), off // hp + g))
    return pl.pallas_call(
        body, name=name, grid=(HEADS // hp, nsb),
        in_specs=[hsp(HEADS), main, halo, basp, hsp(0),
                  pl.BlockSpec((hp, nc, HD, HD), lambda g, i: (g, rblk(i), 0, 0)),
                  cws(0), cws(HEADS), cws(2 * HEADS), pl.BlockSpec((8, HD), lambda g, i: (0, 0)),
                  pl.BlockSpec(memory_space=pl.ANY)],
        out_specs=[main, hsp(0),
                   pl.BlockSpec((hp, 4, 3 * HD), lambda g, i: (g, 0, 0)), pl.BlockSpec((8, HD), lambda g, i: (0, 0))],
        out_shape=[jax.ShapeDtypeStruct((s, D_MAIN), f32), jax.ShapeDtypeStruct((s, D_MODEL), f32),
                   jax.ShapeDtypeStruct((HEADS, 4, 3 * HD), f32), jax.ShapeDtypeStruct((8, HD), f32)],
        scratch_shapes=[pltpu.VMEM((hp, HD, HD), f32), pltpu.VMEM((hp, 8, 3 * HD), f32),
                        pltpu.VMEM((hp, 8, 3 * HD), f32), pltpu.VMEM((hp, 8, HD), f32), pltpu.VMEM((8, HD), f32)],
        input_output_aliases={10: 0},
        compiler_params=_params(("arbitrary", "arbitrary")),
    )(dy, proj, proj, proj, o, states, cw, cw, cw, dnv, dproj)


ANY = pl.BlockSpec(memory_space=pl.ANY)
CHIP_MASKS = ((1, 0, 0), (0, 1, 0), (1, 1, 0))
ALL_MASKS = tuple((m >> 2 & 1, m >> 1 & 1, m & 1) for m in range(1, N_DEV))


def _me():
    return lax.axis_index("x"), lax.axis_index("y"), lax.axis_index("c")


def _flip(me, mask):
    return tuple((1 - v) if m else v for v, m in zip(me, mask))


def _dev_index(dev):
    return 4 * dev[0] + 2 * dev[1] + dev[2]


def _exchange(name, arrays, out_shapes, masks, src_of, dst_of, own_of):
    na, nm = len(arrays), len(masks)

    def body(*refs):
        srcs, dsts = refs[:na], refs[na:2 * na]
        send_sems, recv_sems, loc_sems = refs[2 * na:]
        me = _me()
        local = []
        for a in range(na):
            own = own_of(a, srcs[a], dsts[a], me)
            if own is not None:
                cp = pltpu.make_async_copy(own[0], own[1], loc_sems.at[a])
                cp.start()
                local.append(cp)
        sends = []
        for a in range(na):
            for mi, mask in enumerate(masks):
                peer = _flip(me, mask)
                cp = pltpu.make_async_remote_copy(
                    src_ref=src_of(a, srcs[a], me, peer), dst_ref=dst_of(a, dsts[a], me),
                    send_sem=send_sems.at[a * nm + mi], recv_sem=recv_sems.at[a * nm + mi],
                    device_id=peer, device_id_type=MESH)
                cp.start()
                sends.append(cp)
        for a in range(na):
            for mi, mask in enumerate(masks):
                peer = _flip(me, mask)
                pltpu.make_async_remote_copy(
                    src_ref=src_of(a, srcs[a], peer, me), dst_ref=dst_of(a, dsts[a], peer),
                    send_sem=send_sems.at[a * nm + mi], recv_sem=recv_sems.at[a * nm + mi],
                    device_id=peer, device_id_type=MESH).wait_recv()
        for cp in sends:
            cp.wait_send()
        for cp in local:
            cp.wait()

    return pl.pallas_call(
        body, name=name, in_specs=[ANY] * na, out_specs=[ANY] * na,
        out_shape=[jax.ShapeDtypeStruct(shp, arr.dtype) for shp, arr in zip(out_shapes, arrays)],
        scratch_shapes=[pltpu.SemaphoreType.DMA((na * nm,)), pltpu.SemaphoreType.DMA((na * nm,)),
                        pltpu.SemaphoreType.DMA((na,))],
        compiler_params=pltpu.CompilerParams(has_side_effects=True),
    )(*arrays)


def _gather_over_chips(arrays, name):
    chip = lambda dev: 2 * dev[0] + dev[1]
    return _exchange(
        name, arrays, [(4,) + a.shape for a in arrays], CHIP_MASKS,
        src_of=lambda a, ref, me, peer: ref,
        dst_of=lambda a, ref, sender: ref.at[chip(sender)],
        own_of=lambda a, src, dst, me: (src, dst.at[chip(me)]))


def _scatter_parts(arrays, name):
    return _exchange(
        name, arrays, [a.shape for a in arrays], ALL_MASKS,
        src_of=lambda a, ref, me, peer: ref.at[_dev_index(peer)],
        dst_of=lambda a, ref, sender: ref.at[_dev_index(sender)],
        own_of=lambda a, src, dst, me: (src.at[_dev_index(me)], dst.at[_dev_index(me)]))


def _share_reduced(small, pair_arrays, name):
    arrays = [small] + list(pair_arrays)
    na = len(arrays)
    nm = len(ALL_MASKS)

    def body(*refs):
        srcs, dsts = refs[:na], refs[na:2 * na]
        send_sems, recv_sems, loc_sems = refs[2 * na:]
        me = _me()
        sib = _flip(me, (0, 0, 1))
        local = []
        for a in range(na):
            slot = _dev_index(me) if a == 0 else me[2]
            cp = pltpu.make_async_copy(srcs[a], dsts[a].at[slot], loc_sems.at[a])
            cp.start()
            local.append(cp)

        def copy(a, k, sender, to):
            slot = _dev_index(sender) if a == 0 else sender[2]
            return pltpu.make_async_remote_copy(
                src_ref=srcs[a], dst_ref=dsts[a].at[slot], send_sem=send_sems.at[k], recv_sem=recv_sems.at[k],
                device_id=to, device_id_type=MESH)

        sends = [copy(0, mi, me, _flip(me, mask)) for mi, mask in enumerate(ALL_MASKS)]
        sends += [copy(a, nm + a - 1, me, sib) for a in range(1, na)]
        for cp in sends:
            cp.start()
        for mi, mask in enumerate(ALL_MASKS):
            copy(0, mi, _flip(me, mask), me).wait_recv()
        for a in range(1, na):
            copy(a, nm + a - 1, sib, me).wait_recv()
        for cp in sends:
            cp.wait_send()
        for cp in local:
            cp.wait()

    nsem = nm + na - 1
    return pl.pallas_call(
        body, name=name, in_specs=[ANY] * na, out_specs=[ANY] * na,
        out_shape=[jax.ShapeDtypeStruct((N_DEV,) + small.shape, small.dtype)]
        + [jax.ShapeDtypeStruct((2,) + a.shape, a.dtype) for a in pair_arrays],
        scratch_shapes=[pltpu.SemaphoreType.DMA((nsem,)), pltpu.SemaphoreType.DMA((nsem,)),
                        pltpu.SemaphoreType.DMA((na,))],
        compiler_params=pltpu.CompilerParams(has_side_effects=True),
    )(*arrays)


def _sum_parts(parts, name):
    _, r, c = parts.shape
    tr = r
    while tr * c * 4 * N_DEV > (8 << 20) and tr % 16 == 0:
        tr //= 2

    def body(p_ref, o_ref):
        total = p_ref[0].astype(f32)
        for d in range(1, N_DEV):
            total = total + p_ref[d].astype(f32)
        o_ref[...] = total

    return pl.pallas_call(
        body, name=name, grid=(r // tr,), in_specs=[pl.BlockSpec((N_DEV, tr, c), lambda i: (0, i, 0))],
        out_specs=pl.BlockSpec((tr, c), lambda i: (i, 0)), out_shape=jax.ShapeDtypeStruct((r, c), f32),
        compiler_params=_params(("parallel",)),
    )(parts)


def _adamw(w, g, m, v, name):
    shape = w.shape
    cols = shape[-1]
    rows = w.size // cols
    tr = rows
    while tr * cols * 4 > (1 << 20) and tr % 16 == 0:
        tr //= 2
    c1 = 1.0 / (1.0 - ADAM_B1 ** ADAM_STEP)
    c2 = 1.0 / (1.0 - ADAM_B2 ** ADAM_STEP)

    def body(w_ref, g_ref, m_ref, v_ref, d_ref, nm_ref, nv_ref):
        gv = g_ref[...]
        mv = ADAM_B1 * m_ref[...] + (1.0 - ADAM_B1) * gv
        vv = ADAM_B2 * v_ref[...] + (1.0 - ADAM_B2) * (gv * gv)
        nm_ref[...] = mv
        nv_ref[...] = vv
        d_ref[...] = -ADAM_LR * ((mv * c1) / (jnp.sqrt(vv * c2) + ADAM_EPS) + ADAM_WD * w_ref[...])

    spec = pl.BlockSpec((tr, cols), lambda i: (i, 0))
    outs = pl.pallas_call(
        body, name=name, grid=(rows // tr,), in_specs=[spec] * 4, out_specs=[spec] * 3,
        out_shape=[jax.ShapeDtypeStruct((rows, cols), f32)] * 3, compiler_params=_params(("parallel",)),
    )(*[t.reshape(rows, cols) for t in (w, g, m, v)])
    return tuple(t.reshape(shape) for t in outs)


def _to_heads(w, parts):
    r = w.shape[0]
    return w.reshape(r, parts, HEADS, HD).transpose(0, 2, 1, 3).reshape(r, parts * HEADS * HD)


def _from_heads(w, parts):
    r = w.shape[0]
    return w.reshape(r, HEADS, parts, HD).transpose(0, 2, 1, 3).reshape(r, parts * HEADS * HD)


def _pad_lanes(t, n=HD):
    return jnp.pad(t, [(0, 0)] * (t.ndim - 1) + [(0, n - t.shape[-1])])


def _rows128(t):
    return t.reshape(-1, HD)


def kernel(x, norm_w, w_in, lru_conv_w, lru_conv_b, lru_wa, lru_ba, lru_wx, lru_bx, lru_lambda, lru_norm_w, dn_conv_w, dn_A_log, dn_dt_bias, dn_norm_w, w_out, final_norm_w, loss_target, m_norm_w, m_w_in, m_lru_conv_w, m_lru_conv_b, m_lru_wa, m_lru_ba, m_lru_wx, m_lru_bx, m_lru_lambda, m_lru_norm_w, m_dn_conv_w, m_dn_A_log, m_dn_dt_bias, m_dn_norm_w, m_w_out, m_final_norm_w, v_norm_w, v_w_in, v_lru_conv_w, v_lru_conv_b, v_lru_wa, v_lru_ba, v_lru_wx, v_lru_bx, v_lru_lambda, v_lru_norm_w, v_dn_conv_w, v_dn_A_log, v_dn_dt_bias, v_dn_norm_w, v_w_out, v_final_norm_w):
    depth = norm_w.shape[0]
    xs = x[0]
    target = loss_target[0]
    chip = 2 * lax.axis_index("x") + lax.axis_index("y")

    g_in, g_out, g_lcw, g_dcw = _gather_over_chips(
        [w_in.astype(bf16), w_out.astype(bf16), lru_conv_w, dn_conv_w], "gather_weights")
    lcw_full = g_lcw.transpose(1, 2, 0, 3).reshape(depth, 4, D_MODEL)
    dcw_full = g_dcw.transpose(1, 2, 0, 3).reshape(depth, 4, 3 * D_MODEL)
    wp, wo, vecs, dnvs = [], [], [], []
    zrow = jnp.zeros((D_MODEL,), f32)
    for l in range(depth):
        wfull = jnp.concatenate([g_in[j, l] for j in range(4)], axis=1)
        wp.append(jnp.concatenate([_to_heads(wfull[:, :2 * D_MODEL], 2), _to_heads(wfull[:, 2 * D_MODEL:D_MAIN], 4),
                                   _pad_lanes(wfull[:, D_MAIN:])], axis=1))
        wo.append(g_out[:, l].reshape(D_MIX, D_MODEL))
        vecs.append(jnp.stack([lru_conv_b[l], lru_ba[l], lru_bx[l], lru_lambda[l], lru_norm_w[l], zrow, zrow, zrow]))
        dnvs.append(jnp.concatenate([_pad_lanes(dn_A_log[l][None]), _pad_lanes(dn_dt_bias[l][None]),
                                     dn_norm_w[l][None], jnp.zeros((5, HD), f32)], axis=0))

    loss_part, grad_x, d_fnw, g_nw, g_vec, g_wa, g_wx, g_lcw_, g_dcw_, g_dsc, g_win, g_wout = _local_step(
        xs, target, norm_w, final_norm_w, wp, wo, lcw_full, dcw_full, vecs, dnvs, lru_wa, lru_wx)
    loss = lax.psum(loss_part[0, 0], ("x", "y", "c"))
    grads = _reduce_grads(chip, d_fnw, g_nw, g_vec, g_wa, g_wx, g_lcw_, g_dcw_, g_dsc, g_win, g_wout)

    names = ["norm_w", "w_in", "lru_conv_w", "lru_conv_b", "lru_wa", "lru_ba", "lru_wx", "lru_bx", "lru_lambda",
             "lru_norm_w", "dn_conv_w", "dn_A_log", "dn_dt_bias", "dn_norm_w", "w_out", "final_norm_w"]
    ws = dict(zip(names, [norm_w, w_in, lru_conv_w, lru_conv_b, lru_wa, lru_ba, lru_wx, lru_bx, lru_lambda, lru_norm_w,
                          dn_conv_w, dn_A_log, dn_dt_bias, dn_norm_w, w_out, final_norm_w]))
    ms = dict(zip(names, [m_norm_w, m_w_in, m_lru_conv_w, m_lru_conv_b, m_lru_wa, m_lru_ba, m_lru_wx, m_lru_bx,
                          m_lru_lambda, m_lru_norm_w, m_dn_conv_w, m_dn_A_log, m_dn_dt_bias, m_dn_norm_w, m_w_out,
                          m_final_norm_w]))
    vs = dict(zip(names, [v_norm_w, v_w_in, v_lru_conv_w, v_lru_conv_b, v_lru_wa, v_lru_ba, v_lru_wx, v_lru_bx,
                          v_lru_lambda, v_lru_norm_w, v_dn_conv_w, v_dn_A_log, v_dn_dt_bias, v_dn_norm_w, v_w_out,
                          v_final_norm_w]))
    deltas, new_m, new_v = [], [], []
    for n in names:
        d, nm_, nv_ = _adamw(ws[n], grads[n], ms[n], vs[n], f"adamw_{n}")
        deltas.append(d)
        new_m.append(nm_)
        new_v.append(nv_)
    return (loss, grad_x[None], *[grads[n] for n in names], *deltas, *new_m, *new_v)


def _local_step(xs, target, norm_w, final_norm_w, wp, wo, lcw_full, dcw_full, vecs, dnvs, lru_wa, lru_wx):
    depth = norm_w.shape[0]
    saved = []
    h = xs
    for l in range(depth):
        hn = _rms_fwd(h, norm_w[l][None], f"rms_fwd{l}")
        proj = _matmul(hn, wp[l], tm=512, tn=896, tk=D_MODEL, name=f"in_proj{l}")
        y, hl = _lru_fwd(proj, lcw_full[l], vecs[l], lru_wa[l], lru_wx[l], f"lru_fwd{l}")
        y, o, states = _dn_fwd(proj, dcw_full[l], dnvs[l], y, f"dn_fwd{l}")
        out = _matmul(y, wo[l], tm=512, tn=D_MODEL, tk=D_MIX, res=h, name=f"out_proj{l}")
        saved.append((h, hn, proj, hl, o, states, y))
        h = out

    dh, d_fnw, loss_part = _final_loss(h, final_norm_w[None], target, "final_loss")

    g_nw, g_vec, g_wa, g_wx, g_lcw_, g_dcw_, g_dsc, g_win, g_wout = ([None] * depth for _ in range(9))
    for l in reversed(range(depth)):
        hin, hn, proj, hl, o, states, y = saved[l]
        dy = _matmul(dh, wo[l], tb=True, tm=512, tn=1024, tk=D_MODEL, name=f"d_y{l}")
        g_wout[l] = _matmul(y, dh, ta=True, tm=512, tn=D_MODEL, tk=512, name=f"d_wout{l}")
        dproj, g_lcw_[l], g_vec[l], g_wa[l], g_wx[l] = _lru_bwd(dy, proj, hl, lcw_full[l], vecs[l], lru_wa[l], lru_wx[l],
                                                               f"lru_bwd{l}")
        dproj, dba8, dcw8, g_dsc[l] = _dn_bwd(dy, proj, o, states, dcw_full[l], dnvs[l], dproj, f"dn_bwd{l}")
        g_dcw_[l] = dcw8.reshape(HEADS, 4, 3, HD).transpose(1, 2, 0, 3).reshape(4, 3 * D_MODEL)
        dba = _pad_lanes(jnp.concatenate([dba8[:, 0::HD], dba8[:, 1::HD]], axis=1))
        dhn = _matmul(dproj, wp[l], tb=True, tm=512, tn=D_MODEL, tk=768, name=f"d_hn_main{l}")
        dhn = _matmul(dba, wp[l][:, D_MAIN:], tb=True, tm=512, tn=D_MODEL, tk=HD, res=dhn, name=f"d_hn_ba{l}")
        dw_main = _matmul(hn, dproj, ta=True, tm=D_MODEL, tn=768, tk=512, name=f"d_win_main{l}")
        dw_ba = _matmul(hn, dba, ta=True, tm=D_MODEL, tn=HD, tk=512, name=f"d_win_ba{l}")
        g_win[l] = jnp.concatenate([_from_heads(dw_main[:, :2 * D_MODEL], 2), _from_heads(dw_main[:, 2 * D_MODEL:], 4),
                                    dw_ba[:, :D_IN - D_MAIN]], axis=1)
        dh, g_nw[l] = _rms_bwd(dhn, hin, norm_w[l][None], dh, f"rms_bwd{l}")
    return loss_part, dh, d_fnw, g_nw, g_vec, g_wa, g_wx, g_lcw_, g_dcw_, g_dsc, g_win, g_wout


def _reduce_grads(chip, d_fnw, g_nw, g_vec, g_wa, g_wx, g_lcw_, g_dcw_, g_dsc, g_win, g_wout):
    depth = len(g_nw)
    shard_in = D_IN // 4
    win_parts = jnp.stack([g_win[l].reshape(D_MODEL, 4, shard_in).transpose(1, 0, 2) for l in range(depth)])
    win_parts = win_parts.transpose(1, 0, 2, 3).reshape(N_DEV, D_MODEL, shard_in)
    wout_parts = jnp.stack([g_wout[l].reshape(4, D_MIX // 4, D_MODEL) for l in range(depth)])
    wout_parts = wout_parts.transpose(1, 0, 2, 3).reshape(N_DEV, D_MIX // 4, D_MODEL)
    vec_row = lambda k: jnp.concatenate([_rows128(g_vec[l][k]) for l in range(depth)])
    small = [
        jnp.concatenate([_rows128(g_nw[l]) for l in range(depth)]),
        vec_row(0), vec_row(1), vec_row(2), vec_row(3), vec_row(4),
        jnp.concatenate([_rows128(g_wa[l]) for l in range(depth)]),
        jnp.concatenate([_rows128(g_wx[l]) for l in range(depth)]),
        jnp.concatenate([g_dsc[l][0:1] for l in range(depth)]),
        jnp.concatenate([g_dsc[l][1:2] for l in range(depth)]),
        jnp.concatenate([g_dsc[l][2:3] for l in range(depth)]),
        _rows128(d_fnw),
        jnp.concatenate([_rows128(g_lcw_[l]) for l in range(depth)]),
        jnp.concatenate([_rows128(g_dcw_[l]) for l in range(depth)]),
    ]
    counts = [t.shape[0] for t in small]
    total = sum(counts)
    per = -(-total // (8 * N_DEV)) * 8
    small = jnp.concatenate(small + [jnp.zeros((per * N_DEV - total, HD), f32)]).reshape(N_DEV, per, HD)

    r_small, r_win, r_wout = _scatter_parts([small, win_parts.astype(bf16), wout_parts.astype(bf16)], "scatter_grads")
    s_small = _sum_parts(r_small, "sum_small")
    s_win = _sum_parts(r_win, "sum_win")
    s_wout = _sum_parts(r_wout, "sum_wout")
    a_small, grad_w_in, grad_w_out = _share_reduced(s_small, [s_win, s_wout], "share_grads")
    a_small = a_small.reshape(N_DEV * per, HD)
    offs = [0]
    for c in counts:
        offs.append(offs[-1] + c)
    piece = lambda k: a_small[offs[k]:offs[k + 1]]
    grads = {
        "norm_w": piece(0).reshape(depth, D_MODEL), "lru_conv_b": piece(1).reshape(depth, D_MODEL),
        "lru_ba": piece(2).reshape(depth, D_MODEL), "lru_bx": piece(3).reshape(depth, D_MODEL),
        "lru_lambda": piece(4).reshape(depth, D_MODEL), "lru_norm_w": piece(5).reshape(depth, D_MODEL),
        "lru_wa": piece(6).reshape(depth, HEADS, HD, HD), "lru_wx": piece(7).reshape(depth, HEADS, HD, HD),
        "dn_A_log": piece(8)[:, :HEADS], "dn_dt_bias": piece(9)[:, :HEADS], "dn_norm_w": piece(10),
        "final_norm_w": piece(11).reshape(D_MODEL),
        "lru_conv_w": lax.dynamic_slice_in_dim(piece(12).reshape(depth, 4, D_MODEL), chip * (D_MODEL // 4), D_MODEL // 4, 2),
        "dn_conv_w": lax.dynamic_slice_in_dim(piece(13).reshape(depth, 4, 3 * D_MODEL), chip * (3 * D_MODEL // 4),
                                              3 * D_MODEL // 4, 2),
        "w_in": grad_w_in, "w_out": grad_w_out,
    }
    return grads
```

```python
import jax
import jax.numpy as jnp
from jax import lax
from jax.experimental import pallas as pl
from jax.experimental.pallas import tpu as pltpu

f32 = jnp.float32
bf16 = jnp.bfloat16
HI = lax.Precision.HIGHEST
MESH = pl.DeviceIdType.MESH

D_MODEL = 1024
HEADS = 8
HD = 128
CHUNK = 64
LRU_T = 128
SEQ_BLOCK = 1024
DN_SEQ_BLOCK = 512
DN_HP = 4
LRU_C = 8.0
D_IN = 6160
D_INP = 6272
D_MIX = 2048
N_GROUPS = 6
BLK_BA = 48
SHARD_IN = D_IN // 4
WIN_W = 1664
WIN_STRIDE = 1536
EPS = 1e-6
QSCALE = HD ** -0.5
N_DEV = 8
VMEM_LIMIT = 56 * 1024 * 1024

ADAM_LR, ADAM_B1, ADAM_B2, ADAM_EPS, ADAM_WD, ADAM_STEP = 0.001, 0.9, 0.999, 1e-08, 0.01, 10


def _sig(x):
    return 1.0 / (1.0 + jnp.exp(-x))


def _log1p(u):
    w = 1.0 + u
    d = w - 1.0
    return jnp.where(d == 0.0, u, jnp.log(w) * (u / jnp.where(d == 0.0, 1.0, d)))


def _softplus(x):
    return jnp.maximum(x, 0.0) + _log1p(jnp.exp(-jnp.abs(x)))


def _expm1(x):
    t = x * (1.0 + x * (0.5 + x * (1.0 / 6 + x * (1.0 / 24 + x * (1.0 / 120 + x * (1.0 / 720))))))
    return jnp.where(jnp.abs(x) < 0.2, t, jnp.exp(x) - 1.0)


def _dot(a, b, prec=None):
    return jnp.dot(a, b, precision=prec, preferred_element_type=f32)


def _dot_nt(a, b, prec=None):
    return lax.dot_general(a, b, (((1,), (1,)), ((), ())), precision=prec, preferred_element_type=f32)


def _dot_tn(a, b, prec=None):
    return lax.dot_general(a, b, (((0,), (0,)), ((), ())), precision=prec, preferred_element_type=f32)


def _b(x):
    return x.astype(bf16)


def _sum0(x):
    return jnp.sum(x, axis=0, keepdims=True)


def _sum1(x):
    return jnp.sum(x, axis=1, keepdims=True)


def _rowmean(x):
    return jnp.mean(x, axis=-1, keepdims=True)


def _dsilu(z, sg):
    return sg * (1.0 + z * (1.0 - sg))


def _conv_taps(x, halo):
    xx = jnp.concatenate([halo, x], axis=0)
    return [pltpu.roll(xx, 3, axis=0)[8:], pltpu.roll(xx, 2, axis=0)[8:], pltpu.roll(xx, 1, axis=0)[8:], x]


def _conv(xs, cw):
    return cw[0:1] * xs[0] + cw[1:2] * xs[1] + cw[2:3] * xs[2] + cw[3:4] * xs[3]


def _rows_before(ref, halo_ref, lanes, t0, c, sblk):
    tprev = pl.multiple_of(jnp.maximum(t0 - 8, 0), 8)
    return jnp.where(c > 0, ref[pl.ds(tprev, 8), lanes], jnp.where(sblk > 0, halo_ref[:, lanes], 0.0))


def _conv_back(dxc, halo_after, cw):
    rows = dxc.shape[0]
    dd = jnp.concatenate([dxc, halo_after], axis=0)
    out = cw[3:4] * dxc
    for s in (1, 2, 3):
        out = out + cw[3 - s:4 - s] * pltpu.roll(dd, rows + 8 - s, axis=0)[:rows]
    return out


def _params(sem=None):
    return pltpu.CompilerParams(dimension_semantics=sem, vmem_limit_bytes=VMEM_LIMIT)


def _seq_specs(sb, width, rowblk, colblk):
    main = pl.BlockSpec((sb, width), lambda a, b: (rowblk(a, b), colblk(a, b)))
    halo = pl.BlockSpec((8, width), lambda a, b: (jnp.maximum(rowblk(a, b) * (sb // 8) - 1, 0), colblk(a, b)))
    return main, halo


def _matmul(a, b, *, ta=False, tb=False, tm, tn, tk, res=None, out_dtype=f32, name):
    if ta:
        kdim, m = a.shape
    else:
        m, kdim = a.shape
    n = b.shape[0] if tb else b.shape[1]
    tm, tn, tk = min(tm, m), min(tn, n), min(tk, kdim)
    assert m % tm == 0 and n % tn == 0 and kdim % tk == 0, (name, m, n, kdim, tm, tn, tk)
    nk = kdim // tk
    dims = (((0 if ta else 1,), (1 if tb else 0,)), ((), ()))
    has_res = res is not None

    def body(*refs):
        a_ref, b_ref = refs[:2]
        r_ref = refs[2] if has_res else None
        o_ref = refs[3] if has_res else refs[2]
        acc_ref = refs[-1] if nk > 1 else None
        part = lax.dot_general(_b(a_ref[...]), _b(b_ref[...]), dims, preferred_element_type=f32)

        def finish(total):
            if has_res:
                total = total + r_ref[...]
            o_ref[...] = total.astype(out_dtype)

        if nk == 1:
            finish(part)
        else:
            k = pl.program_id(2)

            @pl.when(k == 0)
            def _():
                acc_ref[...] = part

            @pl.when(k > 0)
            def _():
                acc_ref[...] += part

            @pl.when(k == nk - 1)
            def _():
                finish(acc_ref[...])

    a_spec = pl.BlockSpec((tk, tm), lambda i, j, k: (k, i)) if ta else pl.BlockSpec((tm, tk), lambda i, j, k: (i, k))
    b_spec = pl.BlockSpec((tn, tk), lambda i, j, k: (j, k)) if tb else pl.BlockSpec((tk, tn), lambda i, j, k: (k, j))
    o_spec = pl.BlockSpec((tm, tn), lambda i, j, k: (i, j))
    in_specs, args = [a_spec, b_spec], [a, b]
    if has_res:
        in_specs.append(o_spec)
        args.append(res)
    return pl.pallas_call(
        body, name=name, grid=(m // tm, n // tn, nk), in_specs=in_specs, out_specs=o_spec,
        out_shape=jax.ShapeDtypeStruct((m, n), out_dtype),
        scratch_shapes=[pltpu.VMEM((tm, tn), f32)] if nk > 1 else [],
        compiler_params=_params(("parallel", "parallel", "arbitrary")),
    )(*args)


def _d_hn(dxs, dba, wp, name):
    s = dxs[0].shape[0]
    tm = min(512, s)

    def body(*refs):
        dx_refs = refs[:N_GROUPS]
        dba_ref, w_ref, wba_ref, o_ref, acc_ref = refs[N_GROUPS:]
        k = pl.program_id(1)
        for g in range(N_GROUPS):
            @pl.when(k == g)
            def _(g=g):
                part = _dot_nt(_b(dx_refs[g][...]), w_ref[...])
                if g == 0:
                    acc_ref[...] = part + _dot_nt(_b(dba_ref[...]), wba_ref[...])
                else:
                    acc_ref[...] += part

        @pl.when(k == N_GROUPS - 1)
        def _():
            o_ref[...] = acc_ref[...]

    row = lambda w: pl.BlockSpec((tm, w), lambda i, k: (i, 0))
    return pl.pallas_call(
        body, name=name, grid=(s // tm, N_GROUPS),
        in_specs=[row(D_MODEL)] * N_GROUPS + [row(HD), pl.BlockSpec((D_MODEL, D_MODEL), lambda i, k: (0, k)),
                                              pl.BlockSpec((D_MODEL, HD), lambda i, k: (0, BLK_BA))],
        out_specs=row(D_MODEL), out_shape=jax.ShapeDtypeStruct((s, D_MODEL), f32),
        scratch_shapes=[pltpu.VMEM((tm, D_MODEL), f32)],
        compiler_params=_params(("parallel", "arbitrary")),
    )(*dxs, dba, wp, wp)


def _d_win(hn, dxs, dba, name):
    s = hn.shape[0]
    tk = min(512, s)
    tn = 512
    per = D_MODEL // tn

    def body(*refs):
        hn_ref = refs[0]
        dx_refs = refs[1:1 + N_GROUPS]
        o_ref, acc_ref = refs[1 + N_GROUPS:]
        j, k = pl.program_id(0), pl.program_id(1)
        for g in range(N_GROUPS):
            @pl.when(j // per == g)
            def _(g=g):
                part = _dot_tn(hn_ref[...], _b(dx_refs[g][...]))

                @pl.when(k == 0)
                def _():
                    acc_ref[...] = part

                @pl.when(k > 0)
                def _():
                    acc_ref[...] += part

        @pl.when(k == s // tk - 1)
        def _():
            o_ref[...] = acc_ref[...].astype(bf16)

    def dx_spec(g):
        on = lambda j: (j // per == g).astype(jnp.int32)
        return pl.BlockSpec((tk, tn), lambda j, k: (k * on(j), (j % per) * on(j)))

    main = pl.pallas_call(
        body, name=name, grid=(N_GROUPS * per, s // tk),
        in_specs=[pl.BlockSpec((tk, D_MODEL), lambda j, k: (k, 0))] + [dx_spec(g) for g in range(N_GROUPS)],
        out_specs=pl.BlockSpec((D_MODEL, tn), lambda j, k: (0, j)),
        out_shape=jax.ShapeDtypeStruct((D_MODEL, D_INP), bf16),
        scratch_shapes=[pltpu.VMEM((D_MODEL, tn), f32)],
        compiler_params=_params(("parallel", "arbitrary")),
    )(hn, *dxs)

    def tail(hn_ref, dba_ref, full_ref, o_ref, acc_ref):
        del full_ref
        k = pl.program_id(0)
        part = _dot_tn(hn_ref[...], _b(dba_ref[...]))

        @pl.when(k == 0)
        def _():
            acc_ref[...] = part

        @pl.when(k > 0)
        def _():
            acc_ref[...] += part

        @pl.when(k == s // tk - 1)
        def _():
            o_ref[...] = acc_ref[...].astype(bf16)

    return pl.pallas_call(
        tail, name=name + "_ba", grid=(s // tk,),
        in_specs=[pl.BlockSpec((tk, D_MODEL), lambda k: (k, 0)), pl.BlockSpec((tk, HD), lambda k: (k, 0)),
                  pl.BlockSpec(memory_space=pl.ANY)],
        out_specs=pl.BlockSpec((D_MODEL, HD), lambda k: (0, BLK_BA)),
        out_shape=jax.ShapeDtypeStruct((D_MODEL, D_INP), bf16),
        scratch_shapes=[pltpu.VMEM((D_MODEL, HD), f32)],
        input_output_aliases={2: 0},
        compiler_params=_params(("arbitrary",)),
    )(hn, dba, main)


def _rms_fwd(x, w, name):
    s = x.shape[0]
    tr = min(512, s)

    def body(x_ref, w_ref, h_ref):
        xv = x_ref[...]
        r = lax.rsqrt(_rowmean(xv * xv) + EPS)
        h_ref[...] = (xv * r * w_ref[...]).astype(bf16)

    return pl.pallas_call(
        body, name=name, grid=(s // tr,),
        in_specs=[pl.BlockSpec((tr, D_MODEL), lambda i: (i, 0)), pl.BlockSpec((1, D_MODEL), lambda i: (0, 0))],
        out_specs=pl.BlockSpec((tr, D_MODEL), lambda i: (i, 0)),
        out_shape=jax.ShapeDtypeStruct((s, D_MODEL), bf16), compiler_params=_params(("parallel",)),
    )(x, w)


def _rms_bwd(dh, x, w, dres, name):
    s = x.shape[0]
    tr = min(512, s)

    def body(dh_ref, x_ref, w_ref, dres_ref, dx_ref, dw_ref):
        xv = x_ref[...]
        r = lax.rsqrt(_rowmean(xv * xv) + EPS)
        xn = xv * r
        d = dh_ref[...]
        dxn = d * w_ref[...]
        dx_ref[...] = dres_ref[...] + r * (dxn - xn * _rowmean(dxn * xn))
        part = _sum0(d * xn)

        @pl.when(pl.program_id(0) == 0)
        def _():
            dw_ref[...] = part

        @pl.when(pl.program_id(0) > 0)
        def _():
            dw_ref[...] += part

    row = pl.BlockSpec((tr, D_MODEL), lambda i: (i, 0))
    vec = pl.BlockSpec((1, D_MODEL), lambda i: (0, 0))
    return pl.pallas_call(
        body, name=name, grid=(s // tr,), in_specs=[row, row, vec, row], out_specs=[row, vec],
        out_shape=[jax.ShapeDtypeStruct((s, D_MODEL), f32), jax.ShapeDtypeStruct((1, D_MODEL), f32)],
        compiler_params=_params(("arbitrary",)),
    )(dh, x, w, dres)


def _final_loss(x, w, target, name):
    s = x.shape[0]
    tr = min(512, s)

    def body(x_ref, w_ref, t_ref, dx_ref, dw_ref, loss_ref):
        xv = x_ref[...]
        wv = w_ref[...]
        r = lax.rsqrt(_rowmean(xv * xv) + EPS)
        xn = xv * r
        e = xn * wv - t_ref[...]
        lb = jnp.broadcast_to(0.5 * _sum0(_rowmean(e * e)), (1, HD))
        dy = e * (1.0 / D_MODEL)
        dxn = dy * wv
        dx_ref[...] = r * (dxn - xn * _rowmean(dxn * xn))
        part = _sum0(dy * xn)

        @pl.when(pl.program_id(0) == 0)
        def _():
            dw_ref[...] = part
            loss_ref[...] = lb

        @pl.when(pl.program_id(0) > 0)
        def _():
            dw_ref[...] += part
            loss_ref[...] += lb

    row = pl.BlockSpec((tr, D_MODEL), lambda i: (i, 0))
    vec = pl.BlockSpec((1, D_MODEL), lambda i: (0, 0))
    lsp = pl.BlockSpec((1, HD), lambda i: (0, 0))
    return pl.pallas_call(
        body, name=name, grid=(s // tr,), in_specs=[row, vec, row], out_specs=[row, vec, lsp],
        out_shape=[jax.ShapeDtypeStruct((s, D_MODEL), f32), jax.ShapeDtypeStruct((1, D_MODEL), f32),
                   jax.ShapeDtypeStruct((1, HD), f32)],
        compiler_params=_params(("arbitrary",)),
    )(x, w, target)


ALL = slice(None)


def _lru_gates(xc, vec, wa, wx):
    sp = _softplus(-vec[3:4])
    xcb = _b(xc)
    r = _sig(_dot(xcb, wa) + vec[1:2])
    i = _sig(_dot(xcb, wx) + vec[2:3])
    la = (-LRU_C) * r * sp
    a = jnp.exp(la)
    mult = jnp.sqrt(-_expm1(2.0 * la))
    return r, i, a, mult, sp, xcb


def _lru_fwd(proj, cw, vec, wa, wx, name):
    s = proj.shape[0]
    sb = min(SEQ_BLOCK, s)
    nsb = s // sb
    t = min(LRU_T, sb)
    nc = sb // t

    def body(x_ref, xh_ref, z_ref, cw_ref, vec_ref, wa_ref, wx_ref, y_ref, hl_ref, hc_ref):
        sblk = pl.program_id(1)
        cwv = cw_ref[...]
        vec_v = vec_ref[...]
        wa_v = _b(wa_ref[...])
        wx_v = _b(wx_ref[...])
        row = lax.broadcasted_iota(jnp.int32, (t, HD), 0)

        @pl.when(sblk == 0)
        def _():
            hc_ref[...] = jnp.zeros((8, HD), f32)

        def chunk(c, hcarry):
            t0 = pl.multiple_of(c * t, t)
            xs = _conv_taps(x_ref[pl.ds(t0, t), :], _rows_before(x_ref, xh_ref, ALL, t0, c, sblk))
            xc = vec_v[0:1] + _conv(xs, cwv)
            r, i, a, mult, _, _ = _lru_gates(xc, vec_v, wa_v, wx_v)
            bb = mult * (i * xc)
            d = 1
            while d < t:
                m = row >= d
                bb = jnp.where(m, a * pltpu.roll(bb, d, axis=0) + bb, bb)
                a = jnp.where(m, a * pltpu.roll(a, d, axis=0), a)
                d *= 2
            hl = bb + a * hcarry
            hl_ref[pl.ds(t0, t), :] = hl
            z = z_ref[pl.ds(t0, t), :]
            nrm = lax.rsqrt(_rowmean(hl * hl) + EPS)
            y_ref[pl.ds(t0, t), :] = (hl * nrm * vec_v[4:5] * (z * _sig(z))).astype(bf16)
            return hl[t - 1:t, :]

        hlast = lax.fori_loop(0, nc, chunk, hc_ref[0:1, :])
        hc_ref[...] = jnp.broadcast_to(hlast, (8, HD))

    xsp, xhalo = _seq_specs(sb, HD, lambda h, i: i, lambda h, i: h)
    zsp, _ = _seq_specs(sb, HD, lambda h, i: i, lambda h, i: HEADS + h)
    hcol = lambda h, i: (0, h)
    wsp = pl.BlockSpec((None, HD, HD), lambda h, i: (h, 0, 0))
    osp = pl.BlockSpec((sb, HD), lambda h, i: (i, h))
    return pl.pallas_call(
        body, name=name, grid=(HEADS, nsb),
        in_specs=[xsp, xhalo, zsp, pl.BlockSpec((4, HD), hcol), pl.BlockSpec((8, HD), hcol), wsp, wsp],
        out_specs=[osp, osp],
        out_shape=[jax.ShapeDtypeStruct((s, D_MIX), bf16), jax.ShapeDtypeStruct((s, D_MODEL), f32)],
        scratch_shapes=[pltpu.VMEM((8, HD), f32)],
        compiler_params=_params(("parallel", "arbitrary")),
    )(proj, proj, proj, cw, vec, wa, wx)


def _lru_bwd(dy, proj, hl, cw, vec, wa, wx, name):
    s = proj.shape[0]
    sb = min(SEQ_BLOCK, s)
    nsb = s // sb
    t = min(LRU_T, sb)
    nc = sb // t

    def body(dy_ref, x_ref, xh_ref, z_ref, hl_ref, hlh_ref, cw_ref, vec_ref, wa_ref, wx_ref,
             dx_ref, dz_ref, dcw_ref, dvec_ref, dwa_ref, dwx_ref, acc_ref, dxh_ref):
        step = pl.program_id(1)
        sblk = nsb - 1 - step
        cwv = cw_ref[...]
        vec_v = vec_ref[...]
        wa_v = _b(wa_ref[...])
        wx_v = _b(wx_ref[...])
        lnw = vec_v[4:5]
        row = lax.broadcasted_iota(jnp.int32, (t, HD), 0)

        @pl.when(step == 0)
        def _():
            acc_ref[...] = jnp.zeros((16, HD), f32)
            dxh_ref[...] = jnp.zeros((8, HD), f32)
            dwa_ref[...] = jnp.zeros((HD, HD), f32)
            dwx_ref[...] = jnp.zeros((HD, HD), f32)

        def chunk(ci, carry):
            mu, dxc_halo, dcw0, dcw1, dcw2, dcw3, dcb, dba, dbx, dsp, dlnw = carry
            c = nc - 1 - ci
            t0 = pl.multiple_of(c * t, t)
            xs = _conv_taps(x_ref[pl.ds(t0, t), :], _rows_before(x_ref, xh_ref, ALL, t0, c, sblk))
            xc = vec_v[0:1] + _conv(xs, cwv)
            r, i, a, mult, sp, xcb = _lru_gates(xc, vec_v, wa_v, wx_v)
            hv = hl_ref[pl.ds(t0, t), :]
            hhalo = _rows_before(hl_ref, hlh_ref, ALL, t0, c, sblk)
            hprev = pltpu.roll(jnp.concatenate([hhalo, hv], axis=0), 1, axis=0)[8:]
            z = z_ref[pl.ds(t0, t), :]
            sgz = _sig(z)
            sz = z * sgz
            dyv = dy_ref[pl.ds(t0, t), :]
            nrm = lax.rsqrt(_rowmean(hv * hv) + EPS)
            hn = hv * nrm
            dlnw = dlnw + _sum0(dyv * hn * sz)
            dz_ref[pl.ds(t0, t), :] = dyv * hn * lnw * _dsilu(z, sgz)
            dhn = dyv * lnw * sz
            lam = nrm * (dhn - hn * _rowmean(dhn * hn))
            cf = jnp.where(row == t - 1, 1.0, pltpu.roll(a, t - 1, axis=0))
            d = 1
            while d < t:
                m = row < t - d
                lam = jnp.where(m, lam + cf * pltpu.roll(lam, t - d, axis=0), lam)
                cf = jnp.where(m, cf * pltpu.roll(cf, t - d, axis=0), cf)
                d *= 2
            lam = lam + cf * mu
            mu = a[0:1] * lam[0:1]
            da = lam * hprev
            dmult = lam * (i * xc)
            di = lam * mult * xc
            dxc = lam * mult * i
            dla = da * a - dmult * (a * a) / mult
            dr = dla * (-LRU_C * sp)
            dsp = dsp + _sum0(dla * (-LRU_C * r))
            dra = dr * r * (1.0 - r)
            dia = di * i * (1.0 - i)
            dba = dba + _sum0(dra)
            dbx = dbx + _sum0(dia)
            drab, diab = _b(dra), _b(dia)
            dwa_ref[...] += _dot_tn(xcb, drab)
            dwx_ref[...] += _dot_tn(xcb, diab)
            dxc = dxc + _dot_nt(drab, wa_v) + _dot_nt(diab, wx_v)
            dcb = dcb + _sum0(dxc)
            dcw0 = dcw0 + _sum0(dxc * xs[0])
            dcw1 = dcw1 + _sum0(dxc * xs[1])
            dcw2 = dcw2 + _sum0(dxc * xs[2])
            dcw3 = dcw3 + _sum0(dxc * xs[3])
            dx_ref[pl.ds(t0, t), :] = _conv_back(dxc, dxc_halo, cwv)
            return (mu, dxc[0:8], dcw0, dcw1, dcw2, dcw3, dcb, dba, dbx, dsp, dlnw)

        acc = acc_ref[...]
        init = (acc[9:10], dxh_ref[...]) + tuple(acc[k:k + 1] for k in range(9))
        mu, dxh, dcw0, dcw1, dcw2, dcw3, dcb, dba, dbx, dsp, dlnw = lax.fori_loop(0, nc, chunk, init)
        zero = jnp.zeros((1, HD), f32)
        acc_ref[...] = jnp.concatenate([dcw0, dcw1, dcw2, dcw3, dcb, dba, dbx, dsp, dlnw, mu] + [zero] * 6, axis=0)
        dxh_ref[...] = dxh

        @pl.when(step == nsb - 1)
        def _():
            dcw_ref[...] = jnp.concatenate([dcw0, dcw1, dcw2, dcw3], axis=0)
            dlam = -dsp * _sig(-vec_v[3:4])
            dvec_ref[...] = jnp.concatenate([dcb, dba, dbx, dlam, dlnw, zero, zero, zero], axis=0)

    rblk = lambda h, i: nsb - 1 - i
    xsp, xhalo = _seq_specs(sb, HD, rblk, lambda h, i: h)
    zsp, _ = _seq_specs(sb, HD, rblk, lambda h, i: HEADS + h)
    hcol = lambda h, i: (0, h)
    wsp = pl.BlockSpec((None, HD, HD), lambda h, i: (h, 0, 0))
    return pl.pallas_call(
        body, name=name, grid=(HEADS, nsb),
        in_specs=[xsp, xsp, xhalo, zsp, xsp, xhalo, pl.BlockSpec((4, HD), hcol), pl.BlockSpec((8, HD), hcol), wsp, wsp],
        out_specs=[xsp, xsp, pl.BlockSpec((4, HD), hcol), pl.BlockSpec((8, HD), hcol), wsp, wsp],
        out_shape=[jax.ShapeDtypeStruct((s, D_MODEL), f32), jax.ShapeDtypeStruct((s, D_MODEL), f32),
                   jax.ShapeDtypeStruct((4, D_MODEL), f32), jax.ShapeDtypeStruct((8, D_MODEL), f32),
                   jax.ShapeDtypeStruct((HEADS, HD, HD), f32), jax.ShapeDtypeStruct((HEADS, HD, HD), f32)],
        scratch_shapes=[pltpu.VMEM((16, HD), f32), pltpu.VMEM((8, HD), f32)],
        compiler_params=_params(("parallel", "arbitrary")),
    )(dy, proj, proj, proj, hl, hl, cw, vec, wa, wx)


def _lane_pick(x, lane, idx):
    return _sum1(jnp.where(lane == idx, x, 0.0))


def _round_robin(gens):
    results = [None] * len(gens)
    live = list(range(len(gens)))
    while live:
        for i in list(live):
            try:
                next(gens[i])
            except StopIteration as done:
                results[i] = done.value
                live.remove(i)
    return results


def _sdot(a, b):
    return _dot(_b(a), _b(b))


def _sdot_tn(a, b):
    return _dot_tn(_b(a), _b(b))


def _sdot_nt(a, b):
    return _dot_nt(_b(a), _b(b))


def _inv_unit_lower(a):
    n = -a
    p = _sdot(a, a)
    yield
    for k in range(5):
        n = n + p + _sdot(n, p)
        if k < 4:
            p = _sdot(p, p)
        yield
    return n


def _dn_chunk(x3, halo3, bav, cw3, dnv, h, masks):
    lane, ri, ci, eye, ltri = masks
    xs = _conv_taps(x3, halo3)
    cpre = _conv(xs, cw3)
    sg = _sig(cpre)
    act = cpre * sg
    q_raw, k_raw, v = act[:, 0:HD], act[:, HD:2 * HD], act[:, 2 * HD:3 * HD]
    rq = lax.rsqrt(_sum1(q_raw * q_raw) + EPS)
    rk = lax.rsqrt(_sum1(k_raw * k_raw) + EPS)
    qn = q_raw * rq
    k = k_raw * rk
    q = qn * QSCALE
    b_in = jnp.broadcast_to(_lane_pick(bav, lane, h), (CHUNK, HD))
    a_in = jnp.broadcast_to(_lane_pick(bav, lane, HEADS + h), (CHUNK, HD))
    lane1 = lane[0:1]
    ea = jnp.exp(_lane_pick(dnv[0:1], lane1, h))
    dt = _lane_pick(dnv[1:2], lane1, h)
    beta = _sig(b_in)
    sp = _softplus(a_in + dt)
    g = -ea * sp
    gc = _dot(ltri, g, HI)
    yield
    gc64 = gc[:, 0:CHUNK]
    grow = _sum0(gc64 * eye)
    dm = jnp.exp(jnp.where(ri >= ci, gc64 - grow, -1e30))
    ds_ = jnp.where(ri > ci, dm, 0.0)
    eg = jnp.exp(gc)
    glast = gc[CHUNK - 1:CHUNK, :]
    ek = jnp.exp(glast - gc)
    kb = k * beta
    kbf = _b(k)
    amat = _dot_nt(_b(kb), kbf) * ds_
    pmat = _dot_nt(_b(q), kbf) * dm
    yield
    tinv = yield from _inv_unit_lower(amat)
    return dict(xs=xs, cpre=cpre, sg=sg, rq=rq, rk=rk, qn=qn, q=q, k=k, v=v, kbf=kbf, b_in=b_in, a_in=a_in, ea=ea,
                dt=dt, beta=beta, g=g, gc=gc, dm=dm, ds=ds_, eg=eg, glast=glast, ek=ek, kb=kb, amat=amat,
                tinv=tinv, pmat=pmat)


def _dn_masks():
    lane = lax.broadcasted_iota(jnp.int32, (CHUNK, HD), 1)
    ri = lax.broadcasted_iota(jnp.int32, (CHUNK, CHUNK), 0)
    ci = lax.broadcasted_iota(jnp.int32, (CHUNK, CHUNK), 1)
    eye = (ri == ci).astype(f32)
    ltri = (ri >= ci).astype(f32)
    return lane, ri, ci, eye, ltri


def _dn_load_qkv(q_ref, qh_ref, k_ref, kh_ref, v_ref, vh_ref, hl, rows, t0, c, sblk):
    x3 = jnp.concatenate([q_ref[rows, hl], k_ref[rows, hl], v_ref[rows, hl]], axis=1)
    halo3 = jnp.concatenate([_rows_before(q_ref, qh_ref, hl, t0, c, sblk), _rows_before(k_ref, kh_ref, hl, t0, c, sblk),
                             _rows_before(v_ref, vh_ref, hl, t0, c, sblk)], axis=1)
    return x3, halo3


def _dn_cw3(cwq_ref, cwk_ref, cwv_ref, j):
    return jnp.concatenate([r[:, j * HD:(j + 1) * HD] for r in (cwq_ref, cwk_ref, cwv_ref)], axis=1)


def _dn_fwd(proj, cw, dnv, y, name):
    s = proj.shape[0]
    sb = min(DN_SEQ_BLOCK, s)
    nsb = s // sb
    nc = sb // CHUNK
    hp = DN_HP

    def body(q_ref, qh_ref, k_ref, kh_ref, v_ref, vh_ref, z_ref, ba_ref, cwq_ref, cwk_ref, cwv_ref, dnv_ref, yin_ref,
             y_ref, o_ref, st_ref, s_ref):
        del yin_ref
        grp = pl.program_id(0)
        sblk = pl.program_id(1)
        masks = _dn_masks()
        dnv_v = dnv_ref[...]
        cw3 = [_dn_cw3(cwq_ref, cwk_ref, cwv_ref, j) for j in range(hp)]

        @pl.when(sblk == 0)
        def _():
            s_ref[...] = jnp.zeros((hp, HD, HD), f32)

        def one_head(j, x3, halo3, bav, z, st):
            f = yield from _dn_chunk(x3, halo3, bav, cw3[j], dnv_v, grp * hp + j, masks)
            sbf = _b(st)
            qs = _dot(_b(f["q"] * f["eg"]), sbf)
            rhs = f["beta"] * (f["v"] - _dot(_b(f["k"] * f["eg"]), sbf))
            yield
            vn = rhs + _sdot(f["tinv"], rhs)
            yield
            vnb = _b(vn)
            o = qs + _dot(_b(f["pmat"]), vnb)
            st_new = st * jnp.exp(f["glast"]) + _dot_tn(_b(f["k"] * f["ek"]), vnb)
            yield
            nrm = lax.rsqrt(_rowmean(o * o) + EPS)
            yv = (o * nrm * dnv_v[2:3] * (z * _sig(z))).astype(bf16)
            return o, yv, st_new

        def chunk(c, states):
            t0 = pl.multiple_of(c * CHUNK, CHUNK)
            rows = pl.ds(t0, CHUNK)
            bav = ba_ref[rows, :]
            ins = []
            for j in range(hp):
                hl = slice(j * HD, (j + 1) * HD)
                ins.append(_dn_load_qkv(q_ref, qh_ref, k_ref, kh_ref, v_ref, vh_ref, hl, rows, t0, c, sblk)
                           + (bav, z_ref[rows, hl]))
            outs = _round_robin([one_head(j, *ins[j], states[j]) for j in range(hp)])
            for j in range(hp):
                hl = slice(j * HD, (j + 1) * HD)
                st_ref[j, c] = states[j]
                o_ref[rows, hl] = outs[j][0]
                y_ref[rows, hl] = outs[j][1]
            return tuple(out[2] for out in outs)

        final = lax.fori_loop(0, nc, chunk, tuple(s_ref[j] for j in range(hp)))
        for j in range(hp):
            s_ref[j] = final[j]

    wide = hp * HD
    grp_specs = lambda off: _seq_specs(sb, wide, lambda g, i: i, lambda g, i: off // hp + g)
    (qsp, qhalo), (ksp, khalo), (vsp, vhalo) = grp_specs(2 * HEADS), grp_specs(3 * HEADS), grp_specs(4 * HEADS)
    zsp, _ = grp_specs(5 * HEADS)
    basp = pl.BlockSpec((sb, HD), lambda g, i: (i, BLK_BA))
    cws = lambda off: pl.BlockSpec((4, wide), lambda g, i: (0, off // hp + g))
    osp = lambda off: pl.BlockSpec((sb, wide), lambda g, i: (i, off // hp + g))
    return pl.pallas_call(
        body, name=name, grid=(HEADS // hp, nsb),
        in_specs=[qsp, qhalo, ksp, khalo, vsp, vhalo, zsp, basp, cws(0), cws(HEADS), cws(2 * HEADS),
                  pl.BlockSpec((8, HD), lambda g, i: (0, 0)), pl.BlockSpec(memory_space=pl.ANY)],
        out_specs=[osp(HEADS), osp(0), pl.BlockSpec((hp, nc, HD, HD), lambda g, i: (g, i, 0, 0))],
        out_shape=[jax.ShapeDtypeStruct((s, D_MIX), bf16), jax.ShapeDtypeStruct((s, D_MODEL), f32),
                   jax.ShapeDtypeStruct((HEADS, s // CHUNK, HD, HD), f32)],
        scratch_shapes=[pltpu.VMEM((hp, HD, HD), f32)],
        input_output_aliases={12: 0},
        compiler_params=_params(("parallel", "arbitrary")),
    )(proj, proj, proj, proj, proj, proj, proj, proj, cw, cw, cw, dnv, y)


def _dn_bwd(dy, proj, o, states, cw, dnv, name):
    s = proj.shape[0]
    sb = min(DN_SEQ_BLOCK, s)
    nsb = s // sb
    nc = sb // CHUNK
    hp = DN_HP
    ngrp = HEADS // hp

    def body(dy_ref, q_ref, qh_ref, k_ref, kh_ref, v_ref, vh_ref, z_ref, ba_ref, o_ref, st_ref,
             cwq_ref, cwk_ref, cwv_ref, dnv_ref,
             dq_ref, dk_ref, dv_ref, dz_ref, dba_ref, dcw_ref, dsc_ref,
             ds_ref, dch_ref, cwacc_ref, sacc_ref, nacc_ref):
        step = pl.program_id(0)
        grp = pl.program_id(1)
        sblk = nsb - 1 - step
        h0 = grp * hp
        masks = _dn_masks()
        lane, ri, ci, eye, ltri = masks
        utri = (ri <= ci).astype(f32)
        cw3s = [_dn_cw3(cwq_ref, cwk_ref, cwv_ref, j) for j in range(hp)]
        dnv_v = dnv_ref[...]
        dnw = dnv_v[2:3]
        row64 = lax.broadcasted_iota(jnp.int32, (CHUNK, HD), 0)

        @pl.when(step == 0)
        def _():
            for j in range(hp):
                ds_ref[h0 + j] = jnp.zeros((HD, HD), f32)
                dch_ref[h0 + j] = jnp.zeros((8, 3 * HD), f32)
                cwacc_ref[h0 + j] = jnp.zeros((8, 3 * HD), f32)
                sacc_ref[h0 + j] = jnp.zeros((8, HD), f32)

        @pl.when((step == 0) & (grp == 0))
        def _():
            nacc_ref[...] = jnp.zeros((8, HD), f32)

        def one_head(j, x3, halo3, bav, z, st, ov, dyv, carry):
            dst, dch, cwacc, sa0, sa1 = carry
            hd = h0 + j
            cw3 = cw3s[j]
            f = yield from _dn_chunk(x3, halo3, bav, cw3, dnv_v, hd, masks)
            q, k, v, beta, eg, ek, kbf = f["q"], f["k"], f["v"], f["beta"], f["eg"], f["ek"], f["kbf"]
            sbf = _b(st)
            dstb = _b(dst)
            qe = q * eg
            keg = k * eg
            kd = k * ek
            kegb, qeb, kdb = _b(keg), _b(qe), _b(kd)
            e = v - _dot(kegb, sbf)
            yield
            rhs = beta * e
            vn = rhs + _sdot(f["tinv"], rhs)
            yield
            vnb = _b(vn)
            pb = _b(f["pmat"])
            sgz = _sig(z)
            sz = z * sgz
            nrm = lax.rsqrt(_rowmean(ov * ov) + EPS)
            on = ov * nrm
            ddnw = _sum0(dyv * on * sz)
            dpz = dyv * on * dnw * _dsilu(z, sgz)
            don = dyv * dnw * sz
            do = nrm * (don - on * _rowmean(don * on))
            dob = _b(do)
            dvn = _dot_tn(pb, dob) + _dot(kdb, dstb)
            dpm = jnp.where(ri >= ci, _dot_nt(dob, vnb), 0.0)
            dqe = _dot_nt(dob, sbf)
            dkd = _dot_nt(vnb, dstb)
            qtdo = _dot_tn(qeb, dob)
            yield
            drhs = dvn + _sdot_tn(f["tinv"], dvn)
            dqk = _b(dpm * f["dm"])
            dq = _dot(dqk, kbf) + dqe * eg
            dk_p = _dot_tn(dqk, _b(q))
            yield
            dam = jnp.where(ri > ci, -_sdot_nt(drhs, vn), 0.0)
            de = drhs * beta
            deb = _b(de)
            dbeta = _sum1(drhs * e)
            dkeg = -_dot_nt(deb, sbf)
            dst_new = qtdo + dst * jnp.exp(f["glast"]) - _dot_tn(kegb, deb)
            yield
            dkk = _b(dam * f["ds"])
            dkb = _dot(dkk, kbf)
            dk = dk_p + _dot_tn(dkk, _b(f["kb"])) + dkb * beta + dkeg * eg + dkd * ek
            yield
            dbeta = dbeta + _sum1(dkb * k)
            mm = dpm * f["pmat"] + dam * f["amat"]
            dglast = _sum1(_sum0(dst * st)) * jnp.exp(f["glast"]) + _sum1(_sum0(dkd * kd))
            dgc = (_sum1(mm) - _sum1(eye * _sum0(mm)) + _sum1(dqe * qe) + _sum1(dkeg * keg) - _sum1(dkd * kd))
            dgc = jnp.broadcast_to(dgc, (CHUNK, HD)) + jnp.where(row64 == CHUNK - 1, dglast, 0.0)
            dg = _dot(utri, dgc, HI)
            yield
            dbin = jnp.broadcast_to(dbeta, (CHUNK, HD)) * beta * (1.0 - beta)
            dain = dg * (-f["ea"]) * _sig(f["a_in"] + f["dt"])
            dba = jnp.where(lane == hd, dbin, 0.0) + jnp.where(lane == HEADS + hd, dain, 0.0)
            sa0 = sa0 + _sum0(dg * f["g"])
            sa1 = sa1 + _sum0(dain)
            dq_raw = (QSCALE * f["rq"]) * (dq - f["qn"] * _sum1(f["qn"] * dq))
            dk_raw = f["rk"] * (dk - k * _sum1(k * dk))
            dact = jnp.concatenate([dq_raw, dk_raw, de], axis=1)
            dc = dact * _dsilu(f["cpre"], f["sg"])
            xs = f["xs"]
            cwacc = cwacc + jnp.concatenate([_sum0(dc * xs[0]), _sum0(dc * xs[1]), _sum0(dc * xs[2]),
                                             _sum0(dc * xs[3])], axis=0)
            dqkv = _conv_back(dc, dch, cw3)
            return dpz, dqkv, dba, ddnw, (dst_new, dc[0:8], cwacc, sa0, sa1)

        def chunk(cidx, carry):
            heads, nacc = carry
            c = nc - 1 - cidx
            t0 = pl.multiple_of(c * CHUNK, CHUNK)
            rows = pl.ds(t0, CHUNK)
            bav = ba_ref[rows, :]
            ins = []
            for j in range(hp):
                hl = slice(j * HD, (j + 1) * HD)
                ins.append(_dn_load_qkv(q_ref, qh_ref, k_ref, kh_ref, v_ref, vh_ref, hl, rows, t0, c, sblk)
                           + (bav, z_ref[rows, hl], st_ref[j, c], o_ref[rows, hl], dy_ref[rows, hl]))
            outs = _round_robin([one_head(j, *ins[j], heads[j]) for j in range(hp)])
            dba = outs[0][2]
            for j in range(hp):
                hl = slice(j * HD, (j + 1) * HD)
                dpz, dqkv, dba_j, ddnw, _ = outs[j]
                dq_ref[rows, hl] = dqkv[:, 0:HD]
                dk_ref[rows, hl] = dqkv[:, HD:2 * HD]
                dv_ref[rows, hl] = dqkv[:, 2 * HD:3 * HD]
                dz_ref[rows, hl] = dpz
                nacc = nacc + ddnw
                if j > 0:
                    dba = dba + dba_j

            @pl.when(grp == 0)
            def _():
                dba_ref[rows, :] = dba

            @pl.when(grp > 0)
            def _():
                dba_ref[rows, :] += dba

            return tuple(out[4] for out in outs), nacc

        init = tuple((ds_ref[h0 + j], dch_ref[h0 + j], cwacc_ref[h0 + j][0:4], sacc_ref[h0 + j][0:1],
                      sacc_ref[h0 + j][1:2]) for j in range(hp))
        heads, nacc = lax.fori_loop(0, nc, chunk, (init, nacc_ref[0:1, :]))
        nacc_ref[0:1, :] = nacc
        zpad = jnp.zeros((4, 3 * HD), f32)
        zrow = jnp.zeros((6, HD), f32)
        for j in range(hp):
            dst, dch, cwacc, sa0, sa1 = heads[j]
            ds_ref[h0 + j] = dst
            dch_ref[h0 + j] = dch
            cwacc_ref[h0 + j] = jnp.concatenate([cwacc, zpad], axis=0)
            sacc_ref[h0 + j] = jnp.concatenate([sa0, sa1, zrow], axis=0)

        @pl.when(step == nsb - 1)
        def _():
            lane8 = lax.broadcasted_iota(jnp.int32, (8, HD), 1)
            row8 = lax.broadcasted_iota(jnp.int32, (8, HD), 0)
            mine = jnp.zeros((8, HD), f32)
            for j in range(hp):
                dcw_ref[h0 + j] = heads[j][2]
                sa = jnp.concatenate([heads[j][3], heads[j][4], zrow], axis=0)
                mine = mine + jnp.where((lane8 == h0 + j) & (row8 < 2), sa, 0.0)

            @pl.when(grp == 0)
            def _():
                dsc_ref[...] = mine

            @pl.when(grp > 0)
            def _():
                dsc_ref[...] += mine

            @pl.when(grp == ngrp - 1)
            def _():
                dsc_ref[2:3, :] = nacc

    wide = hp * HD
    rblk = lambda i, g: nsb - 1 - i
    grp_specs = lambda off: _seq_specs(sb, wide, rblk, lambda i, g: off // hp + g)
    (qsp, qhalo), (ksp, khalo), (vsp, vhalo) = grp_specs(2 * HEADS), grp_specs(3 * HEADS), grp_specs(4 * HEADS)
    zsp, _ = grp_specs(5 * HEADS)
    hsp = lambda off: pl.BlockSpec((sb, wide), lambda i, g: (rblk(i, g), off // hp + g))
    basp = lambda col: pl.BlockSpec((sb, HD), lambda i, g: (rblk(i, g), col))
    cws = lambda off: pl.BlockSpec((4, wide), lambda i, g: (0, off // hp + g))
    whole = lambda shape: pl.BlockSpec(shape, lambda i, g: (0,) * len(shape))
    return pl.pallas_call(
        body, name=name, grid=(nsb, ngrp),
        in_specs=[hsp(HEADS), qsp, qhalo, ksp, khalo, vsp, vhalo, zsp, basp(BLK_BA), hsp(0),
                  pl.BlockSpec((hp, nc, HD, HD), lambda i, g: (g, rblk(i, g), 0, 0)),
                  cws(0), cws(HEADS), cws(2 * HEADS), whole((8, HD))],
        out_specs=[hsp(0), hsp(0), hsp(0), hsp(0), basp(0), whole((HEADS, 4, 3 * HD)), whole((8, HD))],
        out_shape=[jax.ShapeDtypeStruct((s, D_MODEL), f32)] * 4
        + [jax.ShapeDtypeStruct((s, HD), f32), jax.ShapeDtypeStruct((HEADS, 4, 3 * HD), f32),
           jax.ShapeDtypeStruct((8, HD), f32)],
        scratch_shapes=[pltpu.VMEM((HEADS, HD, HD), f32), pltpu.VMEM((HEADS, 8, 3 * HD), f32),
                        pltpu.VMEM((HEADS, 8, 3 * HD), f32), pltpu.VMEM((HEADS, 8, HD), f32), pltpu.VMEM((8, HD), f32)],
        compiler_params=_params(("arbitrary", "arbitrary")),
    )(dy, proj, proj, proj, proj, proj, proj, proj, proj, o, states, cw, cw, cw, dnv)


ANY = pl.BlockSpec(memory_space=pl.ANY)
CHIP_MASKS = ((1, 0, 0), (0, 1, 0), (1, 1, 0))
ALL_MASKS = tuple((m >> 2 & 1, m >> 1 & 1, m & 1) for m in range(1, N_DEV))


def _me():
    return lax.axis_index("x"), lax.axis_index("y"), lax.axis_index("c")


def _flip(me, mask):
    return tuple((1 - v) if m else v for v, m in zip(me, mask))


def _dev_index(dev):
    return 4 * dev[0] + 2 * dev[1] + dev[2]


def _chip(dev):
    return 2 * dev[0] + dev[1]


def _comm_call(name, body, arrays, out_shapes, nsem, nloc):
    return pl.pallas_call(
        body, name=name, in_specs=[ANY] * len(arrays), out_specs=[ANY] * len(out_shapes), out_shape=out_shapes,
        scratch_shapes=[pltpu.SemaphoreType.DMA((nsem,)), pltpu.SemaphoreType.DMA((nsem,)),
                        pltpu.SemaphoreType.DMA((nloc,))],
        compiler_params=pltpu.CompilerParams(has_side_effects=True),
    )(*arrays)


def _gather_halves(arrays, whole_arrays, name):
    na, nw = len(arrays), len(whole_arrays)
    nm = len(CHIP_MASKS)

    def body(*refs):
        srcs, dsts = refs[:na + nw], refs[na + nw:2 * (na + nw)]
        send_sems, recv_sems, loc_sems = refs[2 * (na + nw):]
        me = _me()
        sib = _flip(me, (0, 0, 1))
        c = me[2]

        def half(a, ref, core):
            rows = arrays[a].shape[0] // 2
            return ref.at[pl.ds(pl.multiple_of(core * rows, rows), rows)]

        local = []
        for a in range(na + nw):
            cp = pltpu.make_async_copy(srcs[a], dsts[a].at[_chip(me)], loc_sems.at[a])
            cp.start()
            local.append(cp)

        def over_ici(a, mi, sender, to):
            if a < na:
                src, dst = half(a, srcs[a], sender[2]), half(a, dsts[a].at[_chip(sender)], sender[2])
            else:
                src, dst = srcs[a], dsts[a].at[_chip(sender)]
            return pltpu.make_async_remote_copy(src_ref=src, dst_ref=dst, send_sem=send_sems.at[a * nm + mi],
                                                recv_sem=recv_sems.at[a * nm + mi], device_id=to, device_id_type=MESH)

        def over_d2d(a, mi, origin, sender, to):
            k = (na + nw) * nm + a * nm + mi
            blk = half(a, dsts[a].at[_chip(origin)], sender[2])
            return pltpu.make_async_remote_copy(src_ref=blk, dst_ref=blk, send_sem=send_sems.at[k],
                                                recv_sem=recv_sems.at[k], device_id=to, device_id_type=MESH)

        sends = []
        for a in range(na + nw):
            for mi, mask in enumerate(CHIP_MASKS):
                cp = over_ici(a, mi, me, _flip(me, mask))
                cp.start()
                sends.append(cp)
        for a in range(na + nw):
            for mi, mask in enumerate(CHIP_MASKS):
                peer = _flip(me, mask)
                over_ici(a, mi, peer, me).wait_recv()
                if a < na:
                    cp = over_d2d(a, mi, peer, me, sib)
                    cp.start()
                    sends.append(cp)
        for a in range(na):
            for mi, mask in enumerate(CHIP_MASKS):
                over_d2d(a, mi, _flip(sib, mask), sib, me).wait_recv()
        for cp in sends:
            cp.wait_send()
        for cp in local:
            cp.wait()

    every = list(arrays) + list(whole_arrays)
    return _comm_call(name, body, every, [jax.ShapeDtypeStruct((4,) + t.shape, t.dtype) for t in every],
                      (2 * na + nw) * nm, na + nw)


def _scatter_layer(win, wout, small, name):
    arrays = [win, wout] + ([small] if small is not None else [])
    na = len(arrays)
    half_in, half_out = D_MODEL // 2, D_MIX // 8

    def piece(a, ref, d):
        j, r = d >> 1, d & 1
        if a == 0:
            return ref.at[pl.ds(half_in * r, half_in), pl.ds(WIN_STRIDE * j, WIN_W)]
        if a == 1:
            return ref.at[pl.ds(2 * half_out * j + half_out * r, half_out)]
        return ref.at[d]

    def body(*refs):
        srcs, dsts = refs[:na], refs[na:2 * na]
        send_sems, recv_sems, loc_sems = refs[2 * na:]
        me = _me()
        my = _dev_index(me)
        for d in range(N_DEV):
            dev = (d >> 2, (d >> 1) & 1, d & 1)
            for a in range(na):
                @pl.when(my != d)
                def _(a=a, d=d, dev=dev):
                    pltpu.make_async_remote_copy(
                        src_ref=piece(a, srcs[a], d), dst_ref=dsts[a].at[my], send_sem=send_sems.at[a * N_DEV + d],
                        recv_sem=recv_sems.at[a * N_DEV + my], device_id=dev, device_id_type=MESH).start()

                @pl.when(my == d)
                def _(a=a, d=d):
                    pltpu.make_async_copy(piece(a, srcs[a], d), dsts[a].at[d], loc_sems.at[a]).start()
        for s in range(N_DEV):
            for a in range(na):
                @pl.when(my != s)
                def _(a=a, s=s):
                    pltpu.make_async_remote_copy(
                        src_ref=piece(a, srcs[a], s), dst_ref=dsts[a].at[s], send_sem=send_sems.at[a * N_DEV + s],
                        recv_sem=recv_sems.at[a * N_DEV + s], device_id=me, device_id_type=MESH).wait_recv()
        for d in range(N_DEV):
            for a in range(na):
                @pl.when(my != d)
                def _(a=a, d=d):
                    pltpu.make_async_remote_copy(
                        src_ref=piece(a, srcs[a], d), dst_ref=dsts[a].at[d], send_sem=send_sems.at[a * N_DEV + d],
                        recv_sem=recv_sems.at[a * N_DEV + d], device_id=me, device_id_type=MESH).wait_send()

                @pl.when(my == d)
                def _(a=a, d=d):
                    pltpu.make_async_copy(piece(a, srcs[a], d), dsts[a].at[d], loc_sems.at[a]).wait()

    shapes = [jax.ShapeDtypeStruct((N_DEV, half_in, WIN_W), win.dtype),
              jax.ShapeDtypeStruct((N_DEV, half_out, D_MODEL), wout.dtype)]
    if small is not None:
        shapes.append(jax.ShapeDtypeStruct(small.shape, small.dtype))
    return _comm_call(name, body, arrays, shapes, na * N_DEV, na)


def _share_reduced(pair_arrays, small, name):
    arrays = list(pair_arrays) + ([small] if small is not None else [])
    na, npair = len(arrays), len(pair_arrays)
    nm = len(ALL_MASKS)

    def body(*refs):
        srcs, dsts = refs[:na], refs[na:2 * na]
        send_sems, recv_sems, loc_sems = refs[2 * na:]
        me = _me()
        sib = _flip(me, (0, 0, 1))
        local = []
        for a in range(na):
            slot = me[2] if a < npair else _dev_index(me)
            cp = pltpu.make_async_copy(srcs[a], dsts[a].at[slot], loc_sems.at[a])
            cp.start()
            local.append(cp)

        def copy(a, k, sender, to):
            slot = sender[2] if a < npair else _dev_index(sender)
            return pltpu.make_async_remote_copy(
                src_ref=srcs[a], dst_ref=dsts[a].at[slot], send_sem=send_sems.at[k], recv_sem=recv_sems.at[k],
                device_id=to, device_id_type=MESH)

        sends = [copy(a, a, me, sib) for a in range(npair)]
        if small is not None:
            sends += [copy(npair, npair + mi, me, _flip(me, mask)) for mi, mask in enumerate(ALL_MASKS)]
        for cp in sends:
            cp.start()
        for a in range(npair):
            copy(a, a, sib, me).wait_recv()
        if small is not None:
            for mi, mask in enumerate(ALL_MASKS):
                copy(npair, npair + mi, _flip(me, mask), me).wait_recv()
        for cp in sends:
            cp.wait_send()
        for cp in local:
            cp.wait()

    shapes = [jax.ShapeDtypeStruct((2,) + t.shape, t.dtype) for t in pair_arrays]
    if small is not None:
        shapes.append(jax.ShapeDtypeStruct((N_DEV,) + small.shape, small.dtype))
    return _comm_call(name, body, arrays, shapes, npair + (nm if small is not None else 0), na)


def _sum_parts(parts, name):
    _, r, c = parts.shape
    tr = r
    while tr * c * 4 * N_DEV > (8 << 20) and tr % 32 == 0:
        tr //= 2

    def body(p_ref, o_ref):
        total = p_ref[0].astype(f32)
        for d in range(1, N_DEV):
            total = total + p_ref[d].astype(f32)
        o_ref[...] = total

    return pl.pallas_call(
        body, name=name, grid=(r // tr,), in_specs=[pl.BlockSpec((N_DEV, tr, c), lambda i: (0, i, 0))],
        out_specs=pl.BlockSpec((tr, c), lambda i: (i, 0)), out_shape=jax.ShapeDtypeStruct((r, c), f32),
        compiler_params=_params(("parallel",)),
    )(parts)


def _adamw(w, g, m, v, name):
    shape = w.shape
    cols = shape[-1]
    rows = w.size // cols
    tr = rows
    while tr * cols * 4 > (1 << 20) and tr % 16 == 0:
        tr //= 2
    c1 = 1.0 / (1.0 - ADAM_B1 ** ADAM_STEP)
    c2 = 1.0 / (1.0 - ADAM_B2 ** ADAM_STEP)

    def body(w_ref, g_ref, m_ref, v_ref, d_ref, nm_ref, nv_ref):
        gv = g_ref[...]
        mv = ADAM_B1 * m_ref[...] + (1.0 - ADAM_B1) * gv
        vv = ADAM_B2 * v_ref[...] + (1.0 - ADAM_B2) * (gv * gv)
        nm_ref[...] = mv
        nv_ref[...] = vv
        d_ref[...] = -ADAM_LR * ((mv * c1) / (jnp.sqrt(vv * c2) + ADAM_EPS) + ADAM_WD * w_ref[...])

    spec = pl.BlockSpec((tr, cols), lambda i: (i, 0))
    outs = pl.pallas_call(
        body, name=name, grid=(rows // tr,), in_specs=[spec] * 4, out_specs=[spec] * 3,
        out_shape=[jax.ShapeDtypeStruct((rows, cols), f32)] * 3, compiler_params=_params(("parallel",)),
    )(*[t.reshape(rows, cols) for t in (w, g, m, v)])
    return tuple(t.reshape(shape) for t in outs)


def _pad_lanes(t, n=HD):
    return jnp.pad(t, [(0, 0)] * (t.ndim - 1) + [(0, n - t.shape[-1])])


def _rows128(t, rows=None):
    t = t.reshape(-1, HD)
    rows = rows or -(-t.shape[0] // 8) * 8
    return jnp.pad(t, ((0, rows - t.shape[0]), (0, 0)))


def kernel(x, norm_w, w_in, lru_conv_w, lru_conv_b, lru_wa, lru_ba, lru_wx, lru_bx, lru_lambda, lru_norm_w, dn_conv_w, dn_A_log, dn_dt_bias, dn_norm_w, w_out, final_norm_w, loss_target, m_norm_w, m_w_in, m_lru_conv_w, m_lru_conv_b, m_lru_wa, m_lru_ba, m_lru_wx, m_lru_bx, m_lru_lambda, m_lru_norm_w, m_dn_conv_w, m_dn_A_log, m_dn_dt_bias, m_dn_norm_w, m_w_out, m_final_norm_w, v_norm_w, v_w_in, v_lru_conv_w, v_lru_conv_b, v_lru_wa, v_lru_ba, v_lru_wx, v_lru_bx, v_lru_lambda, v_lru_norm_w, v_dn_conv_w, v_dn_A_log, v_dn_dt_bias, v_dn_norm_w, v_w_out, v_final_norm_w):
    depth = norm_w.shape[0]
    xs = x[0]
    target = loss_target[0]
    chip = 2 * lax.axis_index("x") + lax.axis_index("y")

    win_b, wout_b = w_in.astype(bf16), w_out.astype(bf16)
    wp, wo = [], []
    for l in range(depth):
        small_shards = [lru_conv_w, dn_conv_w] if l == 0 else []
        got = _gather_halves([win_b[l], wout_b[l]], small_shards, f"gather_weights{l}")
        wp.append(_pad_lanes(jnp.concatenate([got[0][j] for j in range(4)], axis=1), D_INP))
        wo.append(got[1].reshape(D_MIX, D_MODEL))
        if l == 0:
            lcw_full = got[2].transpose(1, 2, 0, 3).reshape(depth, 4, D_MODEL)
            dcw_full = got[3].transpose(1, 2, 0, 3).reshape(depth, 4, 3 * D_MODEL)
    vecs, dnvs = [], []
    zrow = jnp.zeros((D_MODEL,), f32)
    for l in range(depth):
        vecs.append(jnp.stack([lru_conv_b[l], lru_ba[l], lru_bx[l], lru_lambda[l], lru_norm_w[l], zrow, zrow, zrow]))
        dnvs.append(jnp.concatenate([_pad_lanes(dn_A_log[l][None]), _pad_lanes(dn_dt_bias[l][None]),
                                     dn_norm_w[l][None], jnp.zeros((5, HD), f32)], axis=0))

    loss_part, grad_x, d_fnw, g_nw, g_vec, g_wa, g_wx, g_lcw_, g_dcw_, g_dsc, g_win, g_wout = _local_step(
        xs, target, norm_w, final_norm_w, wp, wo, lcw_full, dcw_full, vecs, dnvs, lru_wa, lru_wx)
    loss = lax.psum(loss_part[0, 0], ("x", "y", "c"))
    grads = _reduce_grads(chip, d_fnw, g_nw, g_vec, g_wa, g_wx, g_lcw_, g_dcw_, g_dsc, g_win, g_wout)

    names = ["norm_w", "w_in", "lru_conv_w", "lru_conv_b", "lru_wa", "lru_ba", "lru_wx", "lru_bx", "lru_lambda",
             "lru_norm_w", "dn_conv_w", "dn_A_log", "dn_dt_bias", "dn_norm_w", "w_out", "final_norm_w"]
    ws = dict(zip(names, [norm_w, w_in, lru_conv_w, lru_conv_b, lru_wa, lru_ba, lru_wx, lru_bx, lru_lambda, lru_norm_w,
                          dn_conv_w, dn_A_log, dn_dt_bias, dn_norm_w, w_out, final_norm_w]))
    ms = dict(zip(names, [m_norm_w, m_w_in, m_lru_conv_w, m_lru_conv_b, m_lru_wa, m_lru_ba, m_lru_wx, m_lru_bx,
                          m_lru_lambda, m_lru_norm_w, m_dn_conv_w, m_dn_A_log, m_dn_dt_bias, m_dn_norm_w, m_w_out,
                          m_final_norm_w]))
    vs = dict(zip(names, [v_norm_w, v_w_in, v_lru_conv_w, v_lru_conv_b, v_lru_wa, v_lru_ba, v_lru_wx, v_lru_bx,
                          v_lru_lambda, v_lru_norm_w, v_dn_conv_w, v_dn_A_log, v_dn_dt_bias, v_dn_norm_w, v_w_out,
                          v_final_norm_w]))
    deltas, new_m, new_v = [], [], []
    for n in names:
        d, nm_, nv_ = _adamw(ws[n], grads[n], ms[n], vs[n], f"adamw_{n}")
        deltas.append(d)
        new_m.append(nm_)
        new_v.append(nv_)
    return (loss, grad_x[None], *[grads[n] for n in names], *deltas, *new_m, *new_v)


def _local_step(xs, target, norm_w, final_norm_w, wp, wo, lcw_full, dcw_full, vecs, dnvs, lru_wa, lru_wx):
    depth = norm_w.shape[0]
    saved = []
    h = xs
    for l in range(depth):
        hn = _rms_fwd(h, norm_w[l][None], f"rms_fwd{l}")
        proj = _matmul(hn, wp[l], tm=512, tn=896, tk=D_MODEL, name=f"in_proj{l}")
        y, hl = _lru_fwd(proj, lcw_full[l], vecs[l], lru_wa[l], lru_wx[l], f"lru_fwd{l}")
        y, o, states = _dn_fwd(proj, dcw_full[l], dnvs[l], y, f"dn_fwd{l}")
        out = _matmul(y, wo[l], tm=512, tn=D_MODEL, tk=D_MIX, res=h, name=f"out_proj{l}")
        saved.append((h, hn, proj, hl, o, states, y))
        h = out

    dh, d_fnw, loss_part = _final_loss(h, final_norm_w[None], target, "final_loss")

    g_nw, g_vec, g_wa, g_wx, g_lcw_, g_dcw_, g_dsc, g_win, g_wout = ([None] * depth for _ in range(9))
    for l in reversed(range(depth)):
        hin, hn, proj, hl, o, states, y = saved[l]
        dy = _matmul(dh, wo[l], tb=True, tm=512, tn=1024, tk=D_MODEL, name=f"d_y{l}")
        g_wout[l] = _matmul(y, dh, ta=True, tm=512, tn=D_MODEL, tk=512, out_dtype=bf16, name=f"d_wout{l}")
        dlx, dlz, g_lcw_[l], g_vec[l], g_wa[l], g_wx[l] = _lru_bwd(dy, proj, hl, lcw_full[l], vecs[l], lru_wa[l],
                                                                 lru_wx[l], f"lru_bwd{l}")
        dq, dk, dv, dz, dba, dcw8, g_dsc[l] = _dn_bwd(dy, proj, o, states, dcw_full[l], dnvs[l], f"dn_bwd{l}")
        g_dcw_[l] = dcw8.reshape(HEADS, 4, 3, HD).transpose(1, 2, 0, 3).reshape(4, 3 * D_MODEL)
        dxs = [dlx, dlz, dq, dk, dv, dz]
        dhn = _d_hn(dxs, dba, wp[l], f"d_hn{l}")
        g_win[l] = _d_win(hn, dxs, dba, f"d_win{l}")
        dh, g_nw[l] = _rms_bwd(dhn, hin, norm_w[l][None], dh, f"rms_bwd{l}")
    return loss_part, dh, d_fnw, g_nw, g_vec, g_wa, g_wx, g_lcw_, g_dcw_, g_dsc, g_win, g_wout


def _reduce_grads(chip, d_fnw, g_nw, g_vec, g_wa, g_wx, g_lcw_, g_dcw_, g_dsc, g_win, g_wout):
    depth = len(g_nw)
    both = lambda f: jnp.concatenate([f(l) for l in range(depth)])
    small = [
        both(lambda l: _rows128(g_nw[l])),
        both(lambda l: _rows128(g_vec[l][0])), both(lambda l: _rows128(g_vec[l][1])),
        both(lambda l: _rows128(g_vec[l][2])), both(lambda l: _rows128(g_vec[l][3])),
        both(lambda l: _rows128(g_vec[l][4])),
        both(lambda l: _rows128(g_wa[l])), both(lambda l: _rows128(g_wx[l])),
        both(lambda l: _rows128(g_dsc[l][0:1])), both(lambda l: _rows128(g_dsc[l][1:2])),
        both(lambda l: _rows128(g_dsc[l][2:3])),
        _rows128(d_fnw),
        both(lambda l: _rows128(g_lcw_[l])), both(lambda l: _rows128(g_dcw_[l])),
    ]
    counts = [t.shape[0] for t in small]
    total = sum(counts)
    per = -(-total // (8 * N_DEV)) * 8
    small = jnp.concatenate(small + [jnp.zeros((per * N_DEV - total, HD), f32)]).reshape(N_DEV, per, HD)

    pairs = [None] * depth
    a_small = None
    for l in reversed(range(depth)):
        got = _scatter_layer(g_win[l], g_wout[l], small if l == 0 else None, f"scatter_grads{l}")
        sums = [_sum_parts(got[0], f"sum_win{l}"), _sum_parts(got[1], f"sum_wout{l}")]
        if l == 0:
            shared = _share_reduced(sums, _sum_parts(got[2], "sum_small"), f"share_grads{l}")
            a_small = shared[2].reshape(N_DEV * per, HD)
        else:
            shared = _share_reduced(sums, None, f"share_grads{l}")
        pairs[l] = shared
    grad_w_in = jnp.stack([lax.dynamic_slice_in_dim(pairs[l][0].reshape(D_MODEL, WIN_W), 4 * chip, SHARD_IN, 1)
                           for l in range(depth)])
    grad_w_out = jnp.stack([pairs[l][1].reshape(D_MIX // 4, D_MODEL) for l in range(depth)])

    offs = [0]
    for c in counts:
        offs.append(offs[-1] + c)

    def piece(k, rows_per_layer=None):
        t = a_small[offs[k]:offs[k + 1]]
        if rows_per_layer is None:
            return t
        return t.reshape(depth, -1, HD)[:, :rows_per_layer]

    vec = lambda k: piece(k).reshape(depth, D_MODEL)
    return {
        "norm_w": vec(0), "lru_conv_b": vec(1), "lru_ba": vec(2), "lru_bx": vec(3), "lru_lambda": vec(4),
        "lru_norm_w": vec(5),
        "lru_wa": piece(6).reshape(depth, HEADS, HD, HD), "lru_wx": piece(7).reshape(depth, HEADS, HD, HD),
        "dn_A_log": piece(8, 1)[:, 0, :HEADS], "dn_dt_bias": piece(9, 1)[:, 0, :HEADS], "dn_norm_w": piece(10, 1)[:, 0],
        "final_norm_w": piece(11).reshape(D_MODEL),
        "lru_conv_w": lax.dynamic_slice_in_dim(piece(12).reshape(depth, 4, D_MODEL), chip * (D_MODEL // 4), D_MODEL // 4, 2),
        "dn_conv_w": lax.dynamic_slice_in_dim(piece(13).reshape(depth, 4, 3 * D_MODEL), chip * (3 * D_MODEL // 4),
                                              3 * D_MODEL // 4, 2),
        "w_in": grad_w_in, "w_out": grad_w_out,
    }
```

```python
import jax
import jax.numpy as jnp
from jax import lax
from jax.experimental import pallas as pl
from jax.experimental.pallas import tpu as pltpu

f32 = jnp.float32
bf16 = jnp.bfloat16
HI = lax.Precision.HIGHEST
MESH = pl.DeviceIdType.MESH

D_MODEL = 1024
HEADS = 8
HD = 128
CHUNK = 64
LRU_T = 128
SEQ_BLOCK = 1024
DN_SEQ_BLOCK = 512
DN_HP = 4
LRU_C = 8.0
D_IN = 6160
D_INP = 6272
D_MIX = 2048
N_GROUPS = 6
BLK_BA = 48
SHARD_IN = D_IN // 4
WIN_W = 1664
WIN_STRIDE = 1536
EPS = 1e-6
QSCALE = HD ** -0.5
N_DEV = 8
VMEM_LIMIT = 56 * 1024 * 1024

ADAM_LR, ADAM_B1, ADAM_B2, ADAM_EPS, ADAM_WD, ADAM_STEP = 0.001, 0.9, 0.999, 1e-08, 0.01, 10


def _sig(x):
    return 1.0 / (1.0 + jnp.exp(-x))


def _log1p(u):
    w = 1.0 + u
    d = w - 1.0
    return jnp.where(d == 0.0, u, jnp.log(w) * (u / jnp.where(d == 0.0, 1.0, d)))


def _softplus(x):
    return jnp.maximum(x, 0.0) + _log1p(jnp.exp(-jnp.abs(x)))


def _expm1(x):
    t = x * (1.0 + x * (0.5 + x * (1.0 / 6 + x * (1.0 / 24 + x * (1.0 / 120 + x * (1.0 / 720))))))
    return jnp.where(jnp.abs(x) < 0.2, t, jnp.exp(x) - 1.0)


def _dot(a, b, prec=None):
    return jnp.dot(a, b, precision=prec, preferred_element_type=f32)


def _dot_nt(a, b, prec=None):
    return lax.dot_general(a, b, (((1,), (1,)), ((), ())), precision=prec, preferred_element_type=f32)


def _dot_tn(a, b, prec=None):
    return lax.dot_general(a, b, (((0,), (0,)), ((), ())), precision=prec, preferred_element_type=f32)


def _b(x):
    return x.astype(bf16)


def _sum0(x):
    return jnp.sum(x, axis=0, keepdims=True)


def _sum1(x):
    return jnp.sum(x, axis=1, keepdims=True)


def _rowmean(x):
    return jnp.mean(x, axis=-1, keepdims=True)


def _dsilu(z, sg):
    return sg * (1.0 + z * (1.0 - sg))


def _conv_taps(x, halo):
    xx = jnp.concatenate([halo, x], axis=0)
    return [pltpu.roll(xx, 3, axis=0)[8:], pltpu.roll(xx, 2, axis=0)[8:], pltpu.roll(xx, 1, axis=0)[8:], x]


def _conv(xs, cw):
    return cw[0:1] * xs[0] + cw[1:2] * xs[1] + cw[2:3] * xs[2] + cw[3:4] * xs[3]


def _rows_before(ref, halo_ref, lanes, t0, c, sblk):
    tprev = pl.multiple_of(jnp.maximum(t0 - 8, 0), 8)
    return jnp.where(c > 0, ref[pl.ds(tprev, 8), lanes], jnp.where(sblk > 0, halo_ref[:, lanes], 0.0))


def _conv_back(dxc, halo_after, cw):
    rows = dxc.shape[0]
    dd = jnp.concatenate([dxc, halo_after], axis=0)
    out = cw[3:4] * dxc
    for s in (1, 2, 3):
        out = out + cw[3 - s:4 - s] * pltpu.roll(dd, rows + 8 - s, axis=0)[:rows]
    return out


def _params(sem=None):
    return pltpu.CompilerParams(dimension_semantics=sem, vmem_limit_bytes=VMEM_LIMIT)


def _seq_specs(sb, width, rowblk, colblk):
    main = pl.BlockSpec((sb, width), lambda a, b: (rowblk(a, b), colblk(a, b)))
    halo = pl.BlockSpec((8, width), lambda a, b: (jnp.maximum(rowblk(a, b) * (sb // 8) - 1, 0), colblk(a, b)))
    return main, halo


def _matmul(a, b, *, ta=False, tb=False, tm, tn, tk, res=None, out_dtype=f32, name):
    if ta:
        kdim, m = a.shape
    else:
        m, kdim = a.shape
    n = b.shape[0] if tb else b.shape[1]
    tm, tn, tk = min(tm, m), min(tn, n), min(tk, kdim)
    assert m % tm == 0 and n % tn == 0 and kdim % tk == 0, (name, m, n, kdim, tm, tn, tk)
    nk = kdim // tk
    dims = (((0 if ta else 1,), (1 if tb else 0,)), ((), ()))
    has_res = res is not None

    def body(*refs):
        a_ref, b_ref = refs[:2]
        r_ref = refs[2] if has_res else None
        o_ref = refs[3] if has_res else refs[2]
        acc_ref = refs[-1] if nk > 1 else None
        part = lax.dot_general(_b(a_ref[...]), _b(b_ref[...]), dims, preferred_element_type=f32)

        def finish(total):
            if has_res:
                total = total + r_ref[...]
            o_ref[...] = total.astype(out_dtype)

        if nk == 1:
            finish(part)
        else:
            k = pl.program_id(2)

            @pl.when(k == 0)
            def _():
                acc_ref[...] = part

            @pl.when(k > 0)
            def _():
                acc_ref[...] += part

            @pl.when(k == nk - 1)
            def _():
                finish(acc_ref[...])

    a_spec = pl.BlockSpec((tk, tm), lambda i, j, k: (k, i)) if ta else pl.BlockSpec((tm, tk), lambda i, j, k: (i, k))
    b_spec = pl.BlockSpec((tn, tk), lambda i, j, k: (j, k)) if tb else pl.BlockSpec((tk, tn), lambda i, j, k: (k, j))
    o_spec = pl.BlockSpec((tm, tn), lambda i, j, k: (i, j))
    in_specs, args = [a_spec, b_spec], [a, b]
    if has_res:
        in_specs.append(o_spec)
        args.append(res)
    return pl.pallas_call(
        body, name=name, grid=(m // tm, n // tn, nk), in_specs=in_specs, out_specs=o_spec,
        out_shape=jax.ShapeDtypeStruct((m, n), out_dtype),
        scratch_shapes=[pltpu.VMEM((tm, tn), f32)] if nk > 1 else [],
        compiler_params=_params(("parallel", "parallel", "arbitrary")),
    )(*args)


def _d_hn(dxs, dba, wp, name):
    s = dxs[0].shape[0]
    tm = min(512, s)

    def body(*refs):
        dx_refs = refs[:N_GROUPS]
        dba_ref, w_ref, wba_ref, o_ref, acc_ref = refs[N_GROUPS:]
        k = pl.program_id(1)
        for g in range(N_GROUPS):
            @pl.when(k == g)
            def _(g=g):
                part = _dot_nt(_b(dx_refs[g][...]), w_ref[...])
                if g == 0:
                    acc_ref[...] = part + _dot_nt(_b(dba_ref[...]), wba_ref[...])
                else:
                    acc_ref[...] += part

        @pl.when(k == N_GROUPS - 1)
        def _():
            o_ref[...] = acc_ref[...]

    row = lambda w: pl.BlockSpec((tm, w), lambda i, k: (i, 0))
    return pl.pallas_call(
        body, name=name, grid=(s // tm, N_GROUPS),
        in_specs=[row(D_MODEL)] * N_GROUPS + [row(HD), pl.BlockSpec((D_MODEL, D_MODEL), lambda i, k: (0, k)),
                                              pl.BlockSpec((D_MODEL, HD), lambda i, k: (0, BLK_BA))],
        out_specs=row(D_MODEL), out_shape=jax.ShapeDtypeStruct((s, D_MODEL), f32),
        scratch_shapes=[pltpu.VMEM((tm, D_MODEL), f32)],
        compiler_params=_params(("parallel", "arbitrary")),
    )(*dxs, dba, wp, wp)


def _d_win(hn, dxs, dba, name):
    s = hn.shape[0]
    tk = min(512, s)
    tn = 512
    per = D_MODEL // tn

    def body(*refs):
        hn_ref = refs[0]
        dx_refs = refs[1:1 + N_GROUPS]
        o_ref, acc_ref = refs[1 + N_GROUPS:]
        j, k = pl.program_id(0), pl.program_id(1)
        for g in range(N_GROUPS):
            @pl.when(j // per == g)
            def _(g=g):
                part = _dot_tn(hn_ref[...], _b(dx_refs[g][...]))

                @pl.when(k == 0)
                def _():
                    acc_ref[...] = part

                @pl.when(k > 0)
                def _():
                    acc_ref[...] += part

        @pl.when(k == s // tk - 1)
        def _():
            o_ref[...] = acc_ref[...].astype(bf16)

    def dx_spec(g):
        on = lambda j: (j // per == g).astype(jnp.int32)
        return pl.BlockSpec((tk, tn), lambda j, k: (k * on(j), (j % per) * on(j)))

    main = pl.pallas_call(
        body, name=name, grid=(N_GROUPS * per, s // tk),
        in_specs=[pl.BlockSpec((tk, D_MODEL), lambda j, k: (k, 0))] + [dx_spec(g) for g in range(N_GROUPS)],
        out_specs=pl.BlockSpec((D_MODEL, tn), lambda j, k: (0, j)),
        out_shape=jax.ShapeDtypeStruct((D_MODEL, D_INP), bf16),
        scratch_shapes=[pltpu.VMEM((D_MODEL, tn), f32)],
        compiler_params=_params(("parallel", "arbitrary")),
    )(hn, *dxs)

    def tail(hn_ref, dba_ref, full_ref, o_ref, acc_ref):
        del full_ref
        k = pl.program_id(0)
        part = _dot_tn(hn_ref[...], _b(dba_ref[...]))

        @pl.when(k == 0)
        def _():
            acc_ref[...] = part

        @pl.when(k > 0)
        def _():
            acc_ref[...] += part

        @pl.when(k == s // tk - 1)
        def _():
            o_ref[...] = acc_ref[...].astype(bf16)

    return pl.pallas_call(
        tail, name=name + "_ba", grid=(s // tk,),
        in_specs=[pl.BlockSpec((tk, D_MODEL), lambda k: (k, 0)), pl.BlockSpec((tk, HD), lambda k: (k, 0)),
                  pl.BlockSpec(memory_space=pl.ANY)],
        out_specs=pl.BlockSpec((D_MODEL, HD), lambda k: (0, BLK_BA)),
        out_shape=jax.ShapeDtypeStruct((D_MODEL, D_INP), bf16),
        scratch_shapes=[pltpu.VMEM((D_MODEL, HD), f32)],
        input_output_aliases={2: 0},
        compiler_params=_params(("arbitrary",)),
    )(hn, dba, main)


def _rms_fwd(x, w, name):
    s = x.shape[0]
    tr = min(512, s)

    def body(x_ref, w_ref, h_ref):
        xv = x_ref[...]
        r = lax.rsqrt(_rowmean(xv * xv) + EPS)
        h_ref[...] = (xv * r * w_ref[...]).astype(bf16)

    return pl.pallas_call(
        body, name=name, grid=(s // tr,),
        in_specs=[pl.BlockSpec((tr, D_MODEL), lambda i: (i, 0)), pl.BlockSpec((1, D_MODEL), lambda i: (0, 0))],
        out_specs=pl.BlockSpec((tr, D_MODEL), lambda i: (i, 0)),
        out_shape=jax.ShapeDtypeStruct((s, D_MODEL), bf16), compiler_params=_params(("parallel",)),
    )(x, w)


def _rms_bwd(dh, x, w, dres, name):
    s = x.shape[0]
    tr = min(512, s)

    def body(dh_ref, x_ref, w_ref, dres_ref, dx_ref, dw_ref):
        xv = x_ref[...]
        r = lax.rsqrt(_rowmean(xv * xv) + EPS)
        xn = xv * r
        d = dh_ref[...]
        dxn = d * w_ref[...]
        dx_ref[...] = dres_ref[...] + r * (dxn - xn * _rowmean(dxn * xn))
        part = _sum0(d * xn)

        @pl.when(pl.program_id(0) == 0)
        def _():
            dw_ref[...] = part

        @pl.when(pl.program_id(0) > 0)
        def _():
            dw_ref[...] += part

    row = pl.BlockSpec((tr, D_MODEL), lambda i: (i, 0))
    vec = pl.BlockSpec((1, D_MODEL), lambda i: (0, 0))
    return pl.pallas_call(
        body, name=name, grid=(s // tr,), in_specs=[row, row, vec, row], out_specs=[row, vec],
        out_shape=[jax.ShapeDtypeStruct((s, D_MODEL), f32), jax.ShapeDtypeStruct((1, D_MODEL), f32)],
        compiler_params=_params(("arbitrary",)),
    )(dh, x, w, dres)


def _final_loss(x, w, target, name):
    s = x.shape[0]
    tr = min(512, s)

    def body(x_ref, w_ref, t_ref, dx_ref, dw_ref, loss_ref):
        xv = x_ref[...]
        wv = w_ref[...]
        r = lax.rsqrt(_rowmean(xv * xv) + EPS)
        xn = xv * r
        e = xn * wv - t_ref[...]
        lb = jnp.broadcast_to(0.5 * _sum0(_rowmean(e * e)), (1, HD))
        dy = e * (1.0 / D_MODEL)
        dxn = dy * wv
        dx_ref[...] = r * (dxn - xn * _rowmean(dxn * xn))
        part = _sum0(dy * xn)

        @pl.when(pl.program_id(0) == 0)
        def _():
            dw_ref[...] = part
            loss_ref[...] = lb

        @pl.when(pl.program_id(0) > 0)
        def _():
            dw_ref[...] += part
            loss_ref[...] += lb

    row = pl.BlockSpec((tr, D_MODEL), lambda i: (i, 0))
    vec = pl.BlockSpec((1, D_MODEL), lambda i: (0, 0))
    lsp = pl.BlockSpec((1, HD), lambda i: (0, 0))
    return pl.pallas_call(
        body, name=name, grid=(s // tr,), in_specs=[row, vec, row], out_specs=[row, vec, lsp],
        out_shape=[jax.ShapeDtypeStruct((s, D_MODEL), f32), jax.ShapeDtypeStruct((1, D_MODEL), f32),
                   jax.ShapeDtypeStruct((1, HD), f32)],
        compiler_params=_params(("arbitrary",)),
    )(x, w, target)


ALL = slice(None)


def _lru_gates(xc, vec, wa, wx):
    sp = _softplus(-vec[3:4])
    xcb = _b(xc)
    r = _sig(_dot(xcb, wa) + vec[1:2])
    i = _sig(_dot(xcb, wx) + vec[2:3])
    la = (-LRU_C) * r * sp
    a = jnp.exp(la)
    mult = jnp.sqrt(-_expm1(2.0 * la))
    return r, i, a, mult, sp, xcb


def _lru_fwd(proj, cw, vec, wa, wx, name):
    s = proj.shape[0]
    sb = min(SEQ_BLOCK, s)
    nsb = s // sb
    t = min(LRU_T, sb)
    nc = sb // t

    def body(x_ref, xh_ref, z_ref, cw_ref, vec_ref, wa_ref, wx_ref, y_ref, hl_ref, hc_ref):
        sblk = pl.program_id(1)
        cwv = cw_ref[...]
        vec_v = vec_ref[...]
        wa_v = _b(wa_ref[...])
        wx_v = _b(wx_ref[...])
        row = lax.broadcasted_iota(jnp.int32, (t, HD), 0)

        @pl.when(sblk == 0)
        def _():
            hc_ref[...] = jnp.zeros((8, HD), f32)

        def chunk(c, hcarry):
            t0 = pl.multiple_of(c * t, t)
            xs = _conv_taps(x_ref[pl.ds(t0, t), :], _rows_before(x_ref, xh_ref, ALL, t0, c, sblk))
            xc = vec_v[0:1] + _conv(xs, cwv)
            r, i, a, mult, _, _ = _lru_gates(xc, vec_v, wa_v, wx_v)
            bb = mult * (i * xc)
            d = 1
            while d < t:
                m = row >= d
                bb = jnp.where(m, a * pltpu.roll(bb, d, axis=0) + bb, bb)
                a = jnp.where(m, a * pltpu.roll(a, d, axis=0), a)
                d *= 2
            hl = bb + a * hcarry
            hl_ref[pl.ds(t0, t), :] = hl
            z = z_ref[pl.ds(t0, t), :]
            nrm = lax.rsqrt(_rowmean(hl * hl) + EPS)
            y_ref[pl.ds(t0, t), :] = (hl * nrm * vec_v[4:5] * (z * _sig(z))).astype(bf16)
            return hl[t - 1:t, :]

        hlast = lax.fori_loop(0, nc, chunk, hc_ref[0:1, :])
        hc_ref[...] = jnp.broadcast_to(hlast, (8, HD))

    xsp, xhalo = _seq_specs(sb, HD, lambda h, i: i, lambda h, i: h)
    zsp, _ = _seq_specs(sb, HD, lambda h, i: i, lambda h, i: HEADS + h)
    hcol = lambda h, i: (0, h)
    wsp = pl.BlockSpec((None, HD, HD), lambda h, i: (h, 0, 0))
    osp = pl.BlockSpec((sb, HD), lambda h, i: (i, h))
    return pl.pallas_call(
        body, name=name, grid=(HEADS, nsb),
        in_specs=[xsp, xhalo, zsp, pl.BlockSpec((4, HD), hcol), pl.BlockSpec((8, HD), hcol), wsp, wsp],
        out_specs=[osp, osp],
        out_shape=[jax.ShapeDtypeStruct((s, D_MIX), bf16), jax.ShapeDtypeStruct((s, D_MODEL), f32)],
        scratch_shapes=[pltpu.VMEM((8, HD), f32)],
        compiler_params=_params(("parallel", "arbitrary")),
    )(proj, proj, proj, cw, vec, wa, wx)


def _lru_bwd(dy, proj, hl, cw, vec, wa, wx, name):
    s = proj.shape[0]
    sb = min(SEQ_BLOCK, s)
    nsb = s // sb
    t = min(LRU_T, sb)
    nc = sb // t

    def body(dy_ref, x_ref, xh_ref, z_ref, hl_ref, hlh_ref, cw_ref, vec_ref, wa_ref, wx_ref,
             dx_ref, dz_ref, dcw_ref, dvec_ref, dwa_ref, dwx_ref, acc_ref, dxh_ref):
        step = pl.program_id(1)
        sblk = nsb - 1 - step
        cwv = cw_ref[...]
        vec_v = vec_ref[...]
        wa_v = _b(wa_ref[...])
        wx_v = _b(wx_ref[...])
        lnw = vec_v[4:5]
        row = lax.broadcasted_iota(jnp.int32, (t, HD), 0)

        @pl.when(step == 0)
        def _():
            acc_ref[...] = jnp.zeros((16, HD), f32)
            dxh_ref[...] = jnp.zeros((8, HD), f32)
            dwa_ref[...] = jnp.zeros((HD, HD), f32)
            dwx_ref[...] = jnp.zeros((HD, HD), f32)

        def chunk(ci, carry):
            mu, dxc_halo, dcw0, dcw1, dcw2, dcw3, dcb, dba, dbx, dsp, dlnw = carry
            c = nc - 1 - ci
            t0 = pl.multiple_of(c * t, t)
            xs = _conv_taps(x_ref[pl.ds(t0, t), :], _rows_before(x_ref, xh_ref, ALL, t0, c, sblk))
            xc = vec_v[0:1] + _conv(xs, cwv)
            r, i, a, mult, sp, xcb = _lru_gates(xc, vec_v, wa_v, wx_v)
            hv = hl_ref[pl.ds(t0, t), :]
            hhalo = _rows_before(hl_ref, hlh_ref, ALL, t0, c, sblk)
            hprev = pltpu.roll(jnp.concatenate([hhalo, hv], axis=0), 1, axis=0)[8:]
            z = z_ref[pl.ds(t0, t), :]
            sgz = _sig(z)
            sz = z * sgz
            dyv = dy_ref[pl.ds(t0, t), :]
            nrm = lax.rsqrt(_rowmean(hv * hv) + EPS)
            hn = hv * nrm
            dlnw = dlnw + _sum0(dyv * hn * sz)
            dz_ref[pl.ds(t0, t), :] = dyv * hn * lnw * _dsilu(z, sgz)
            dhn = dyv * lnw * sz
            lam = nrm * (dhn - hn * _rowmean(dhn * hn))
            cf = jnp.where(row == t - 1, 1.0, pltpu.roll(a, t - 1, axis=0))
            d = 1
            while d < t:
                m = row < t - d
                lam = jnp.where(m, lam + cf * pltpu.roll(lam, t - d, axis=0), lam)
                cf = jnp.where(m, cf * pltpu.roll(cf, t - d, axis=0), cf)
                d *= 2
            lam = lam + cf * mu
            mu = a[0:1] * lam[0:1]
            da = lam * hprev
            dmult = lam * (i * xc)
            di = lam * mult * xc
            dxc = lam * mult * i
            dla = da * a - dmult * (a * a) / mult
            dr = dla * (-LRU_C * sp)
            dsp = dsp + _sum0(dla * (-LRU_C * r))
            dra = dr * r * (1.0 - r)
            dia = di * i * (1.0 - i)
            dba = dba + _sum0(dra)
            dbx = dbx + _sum0(dia)
            drab, diab = _b(dra), _b(dia)
            dwa_ref[...] += _dot_tn(xcb, drab)
            dwx_ref[...] += _dot_tn(xcb, diab)
            dxc = dxc + _dot_nt(drab, wa_v) + _dot_nt(diab, wx_v)
            dcb = dcb + _sum0(dxc)
            dcw0 = dcw0 + _sum0(dxc * xs[0])
            dcw1 = dcw1 + _sum0(dxc * xs[1])
            dcw2 = dcw2 + _sum0(dxc * xs[2])
            dcw3 = dcw3 + _sum0(dxc * xs[3])
            dx_ref[pl.ds(t0, t), :] = _conv_back(dxc, dxc_halo, cwv)
            return (mu, dxc[0:8], dcw0, dcw1, dcw2, dcw3, dcb, dba, dbx, dsp, dlnw)

        acc = acc_ref[...]
        init = (acc[9:10], dxh_ref[...]) + tuple(acc[k:k + 1] for k in range(9))
        mu, dxh, dcw0, dcw1, dcw2, dcw3, dcb, dba, dbx, dsp, dlnw = lax.fori_loop(0, nc, chunk, init)
        zero = jnp.zeros((1, HD), f32)
        acc_ref[...] = jnp.concatenate([dcw0, dcw1, dcw2, dcw3, dcb, dba, dbx, dsp, dlnw, mu] + [zero] * 6, axis=0)
        dxh_ref[...] = dxh

        @pl.when(step == nsb - 1)
        def _():
            dcw_ref[...] = jnp.concatenate([dcw0, dcw1, dcw2, dcw3], axis=0)
            dlam = -dsp * _sig(-vec_v[3:4])
            dvec_ref[...] = jnp.concatenate([dcb, dba, dbx, dlam, dlnw, zero, zero, zero], axis=0)

    rblk = lambda h, i: nsb - 1 - i
    xsp, xhalo = _seq_specs(sb, HD, rblk, lambda h, i: h)
    zsp, _ = _seq_specs(sb, HD, rblk, lambda h, i: HEADS + h)
    hcol = lambda h, i: (0, h)
    wsp = pl.BlockSpec((None, HD, HD), lambda h, i: (h, 0, 0))
    return pl.pallas_call(
        body, name=name, grid=(HEADS, nsb),
        in_specs=[xsp, xsp, xhalo, zsp, xsp, xhalo, pl.BlockSpec((4, HD), hcol), pl.BlockSpec((8, HD), hcol), wsp, wsp],
        out_specs=[xsp, xsp, pl.BlockSpec((4, HD), hcol), pl.BlockSpec((8, HD), hcol), wsp, wsp],
        out_shape=[jax.ShapeDtypeStruct((s, D_MODEL), f32), jax.ShapeDtypeStruct((s, D_MODEL), f32),
                   jax.ShapeDtypeStruct((4, D_MODEL), f32), jax.ShapeDtypeStruct((8, D_MODEL), f32),
                   jax.ShapeDtypeStruct((HEADS, HD, HD), f32), jax.ShapeDtypeStruct((HEADS, HD, HD), f32)],
        scratch_shapes=[pltpu.VMEM((16, HD), f32), pltpu.VMEM((8, HD), f32)],
        compiler_params=_params(("parallel", "arbitrary")),
    )(dy, proj, proj, proj, hl, hl, cw, vec, wa, wx)


def _lane_pick(x, lane, idx):
    return _sum1(jnp.where(lane == idx, x, 0.0))


def _round_robin(gens):
    results = [None] * len(gens)
    live = list(range(len(gens)))
    while live:
        for i in list(live):
            try:
                next(gens[i])
            except StopIteration as done:
                results[i] = done.value
                live.remove(i)
    return results


def _sdot(a, b):
    return _dot(_b(a), _b(b))


def _sdot_tn(a, b):
    return _dot_tn(_b(a), _b(b))


def _sdot_nt(a, b):
    return _dot_nt(_b(a), _b(b))


def _inv_unit_lower(a):
    n = -a
    p = _sdot(a, a)
    yield
    for k in range(5):
        n = n + p + _sdot(n, p)
        if k < 4:
            p = _sdot(p, p)
        yield
    return n


def _dn_chunk(x3, halo3, bav, cw3, dnv, h, masks):
    lane, ri, ci, eye, ltri = masks
    xs = _conv_taps(x3, halo3)
    cpre = _conv(xs, cw3)
    sg = _sig(cpre)
    act = cpre * sg
    q_raw, k_raw, v = act[:, 0:HD], act[:, HD:2 * HD], act[:, 2 * HD:3 * HD]
    rq = lax.rsqrt(_sum1(q_raw * q_raw) + EPS)
    rk = lax.rsqrt(_sum1(k_raw * k_raw) + EPS)
    qn = q_raw * rq
    k = k_raw * rk
    q = qn * QSCALE
    b_in = jnp.broadcast_to(_lane_pick(bav, lane, h), (CHUNK, HD))
    a_in = jnp.broadcast_to(_lane_pick(bav, lane, HEADS + h), (CHUNK, HD))
    lane1 = lane[0:1]
    ea = jnp.exp(_lane_pick(dnv[0:1], lane1, h))
    dt = _lane_pick(dnv[1:2], lane1, h)
    beta = _sig(b_in)
    sp = _softplus(a_in + dt)
    g = -ea * sp
    gc = _dot(ltri, g, HI)
    yield
    gc64 = gc[:, 0:CHUNK]
    grow = _sum0(gc64 * eye)
    dm = jnp.exp(jnp.where(ri >= ci, gc64 - grow, -1e30))
    ds_ = jnp.where(ri > ci, dm, 0.0)
    eg = jnp.exp(gc)
    glast = gc[CHUNK - 1:CHUNK, :]
    ek = jnp.exp(glast - gc)
    kb = k * beta
    kbf = _b(k)
    amat = _dot_nt(_b(kb), kbf) * ds_
    pmat = _dot_nt(_b(q), kbf) * dm
    yield
    tinv = yield from _inv_unit_lower(amat)
    return dict(xs=xs, cpre=cpre, sg=sg, rq=rq, rk=rk, qn=qn, q=q, k=k, v=v, kbf=kbf, b_in=b_in, a_in=a_in, ea=ea,
                dt=dt, beta=beta, g=g, gc=gc, dm=dm, ds=ds_, eg=eg, glast=glast, ek=ek, kb=kb, amat=amat,
                tinv=tinv, pmat=pmat)


def _dn_masks():
    lane = lax.broadcasted_iota(jnp.int32, (CHUNK, HD), 1)
    ri = lax.broadcasted_iota(jnp.int32, (CHUNK, CHUNK), 0)
    ci = lax.broadcasted_iota(jnp.int32, (CHUNK, CHUNK), 1)
    eye = (ri == ci).astype(f32)
    ltri = (ri >= ci).astype(f32)
    return lane, ri, ci, eye, ltri


def _dn_load_qkv(q_ref, qh_ref, k_ref, kh_ref, v_ref, vh_ref, hl, rows, t0, c, sblk):
    x3 = jnp.concatenate([q_ref[rows, hl], k_ref[rows, hl], v_ref[rows, hl]], axis=1)
    halo3 = jnp.concatenate([_rows_before(q_ref, qh_ref, hl, t0, c, sblk), _rows_before(k_ref, kh_ref, hl, t0, c, sblk),
                             _rows_before(v_ref, vh_ref, hl, t0, c, sblk)], axis=1)
    return x3, halo3


def _dn_cw3(cwq_ref, cwk_ref, cwv_ref, j):
    return jnp.concatenate([r[:, j * HD:(j + 1) * HD] for r in (cwq_ref, cwk_ref, cwv_ref)], axis=1)


def _dn_fwd(proj, cw, dnv, y, name):
    s = proj.shape[0]
    sb = min(DN_SEQ_BLOCK, s)
    nsb = s // sb
    nc = sb // CHUNK
    hp = DN_HP

    def body(q_ref, qh_ref, k_ref, kh_ref, v_ref, vh_ref, z_ref, ba_ref, cwq_ref, cwk_ref, cwv_ref, dnv_ref, yin_ref,
             y_ref, o_ref, st_ref, s_ref):
        del yin_ref
        grp = pl.program_id(0)
        sblk = pl.program_id(1)
        masks = _dn_masks()
        dnv_v = dnv_ref[...]
        cw3 = [_dn_cw3(cwq_ref, cwk_ref, cwv_ref, j) for j in range(hp)]

        @pl.when(sblk == 0)
        def _():
            s_ref[...] = jnp.zeros((hp, HD, HD), f32)

        def one_head(j, x3, halo3, bav, z, st):
            f = yield from _dn_chunk(x3, halo3, bav, cw3[j], dnv_v, grp * hp + j, masks)
            sbf = _b(st)
            qs = _dot(_b(f["q"] * f["eg"]), sbf)
            rhs = f["beta"] * (f["v"] - _dot(_b(f["k"] * f["eg"]), sbf))
            yield
            vn = rhs + _sdot(f["tinv"], rhs)
            yield
            vnb = _b(vn)
            o = qs + _dot(_b(f["pmat"]), vnb)
            st_new = st * jnp.exp(f["glast"]) + _dot_tn(_b(f["k"] * f["ek"]), vnb)
            yield
            nrm = lax.rsqrt(_rowmean(o * o) + EPS)
            yv = (o * nrm * dnv_v[2:3] * (z * _sig(z))).astype(bf16)
            return o, yv, st_new

        def chunk(c, states):
            t0 = pl.multiple_of(c * CHUNK, CHUNK)
            rows = pl.ds(t0, CHUNK)
            bav = ba_ref[rows, :]
            ins = []
            for j in range(hp):
                hl = slice(j * HD, (j + 1) * HD)
                ins.append(_dn_load_qkv(q_ref, qh_ref, k_ref, kh_ref, v_ref, vh_ref, hl, rows, t0, c, sblk)
                           + (bav, z_ref[rows, hl]))
            outs = _round_robin([one_head(j, *ins[j], states[j]) for j in range(hp)])
            for j in range(hp):
                hl = slice(j * HD, (j + 1) * HD)
                st_ref[j, c] = states[j]
                o_ref[rows, hl] = outs[j][0]
                y_ref[rows, hl] = outs[j][1]
            return tuple(out[2] for out in outs)

        final = lax.fori_loop(0, nc, chunk, tuple(s_ref[j] for j in range(hp)))
        for j in range(hp):
            s_ref[j] = final[j]

    wide = hp * HD
    grp_specs = lambda off: _seq_specs(sb, wide, lambda g, i: i, lambda g, i: off // hp + g)
    (qsp, qhalo), (ksp, khalo), (vsp, vhalo) = grp_specs(2 * HEADS), grp_specs(3 * HEADS), grp_specs(4 * HEADS)
    zsp, _ = grp_specs(5 * HEADS)
    basp = pl.BlockSpec((sb, HD), lambda g, i: (i, BLK_BA))
    cws = lambda off: pl.BlockSpec((4, wide), lambda g, i: (0, off // hp + g))
    osp = lambda off: pl.BlockSpec((sb, wide), lambda g, i: (i, off // hp + g))
    return pl.pallas_call(
        body, name=name, grid=(HEADS // hp, nsb),
        in_specs=[qsp, qhalo, ksp, khalo, vsp, vhalo, zsp, basp, cws(0), cws(HEADS), cws(2 * HEADS),
                  pl.BlockSpec((8, HD), lambda g, i: (0, 0)), pl.BlockSpec(memory_space=pl.ANY)],
        out_specs=[osp(HEADS), osp(0), pl.BlockSpec((hp, nc, HD, HD), lambda g, i: (g, i, 0, 0))],
        out_shape=[jax.ShapeDtypeStruct((s, D_MIX), bf16), jax.ShapeDtypeStruct((s, D_MODEL), f32),
                   jax.ShapeDtypeStruct((HEADS, s // CHUNK, HD, HD), f32)],
        scratch_shapes=[pltpu.VMEM((hp, HD, HD), f32)],
        input_output_aliases={12: 0},
        compiler_params=_params(("parallel", "arbitrary")),
    )(proj, proj, proj, proj, proj, proj, proj, proj, cw, cw, cw, dnv, y)


def _dn_bwd(dy, proj, o, states, cw, dnv, name):
    s = proj.shape[0]
    sb = min(DN_SEQ_BLOCK, s)
    nsb = s // sb
    nc = sb // CHUNK
    hp = DN_HP
    ngrp = HEADS // hp

    def body(dy_ref, q_ref, qh_ref, k_ref, kh_ref, v_ref, vh_ref, z_ref, ba_ref, o_ref, st_ref,
             cwq_ref, cwk_ref, cwv_ref, dnv_ref,
             dq_ref, dk_ref, dv_ref, dz_ref, dba_ref, dcw_ref, dsc_ref,
             ds_ref, dch_ref, cwacc_ref, sacc_ref, nacc_ref):
        step = pl.program_id(0)
        grp = pl.program_id(1)
        sblk = nsb - 1 - step
        h0 = grp * hp
        masks = _dn_masks()
        lane, ri, ci, eye, ltri = masks
        utri = (ri <= ci).astype(f32)
        cw3s = [_dn_cw3(cwq_ref, cwk_ref, cwv_ref, j) for j in range(hp)]
        dnv_v = dnv_ref[...]
        dnw = dnv_v[2:3]
        row64 = lax.broadcasted_iota(jnp.int32, (CHUNK, HD), 0)

        @pl.when(step == 0)
        def _():
            for j in range(hp):
                ds_ref[h0 + j] = jnp.zeros((HD, HD), f32)
                dch_ref[h0 + j] = jnp.zeros((8, 3 * HD), f32)
                cwacc_ref[h0 + j] = jnp.zeros((8, 3 * HD), f32)
                sacc_ref[h0 + j] = jnp.zeros((8, HD), f32)

        @pl.when((step == 0) & (grp == 0))
        def _():
            nacc_ref[...] = jnp.zeros((8, HD), f32)

        def one_head(j, x3, halo3, bav, z, st, ov, dyv, carry):
            dst, dch, cwacc, sa0, sa1 = carry
            hd = h0 + j
            cw3 = cw3s[j]
            f = yield from _dn_chunk(x3, halo3, bav, cw3, dnv_v, hd, masks)
            q, k, v, beta, eg, ek, kbf = f["q"], f["k"], f["v"], f["beta"], f["eg"], f["ek"], f["kbf"]
            sbf = _b(st)
            dstb = _b(dst)
            qe = q * eg
            keg = k * eg
            kd = k * ek
            kegb, qeb, kdb = _b(keg), _b(qe), _b(kd)
            e = v - _dot(kegb, sbf)
            yield
            rhs = beta * e
            vn = rhs + _sdot(f["tinv"], rhs)
            yield
            vnb = _b(vn)
            pb = _b(f["pmat"])
            sgz = _sig(z)
            sz = z * sgz
            nrm = lax.rsqrt(_rowmean(ov * ov) + EPS)
            on = ov * nrm
            ddnw = _sum0(dyv * on * sz)
            dpz = dyv * on * dnw * _dsilu(z, sgz)
            don = dyv * dnw * sz
            do = nrm * (don - on * _rowmean(don * on))
            dob = _b(do)
            dvn = _dot_tn(pb, dob) + _dot(kdb, dstb)
            dpm = jnp.where(ri >= ci, _dot_nt(dob, vnb), 0.0)
            dqe = _dot_nt(dob, sbf)
            dkd = _dot_nt(vnb, dstb)
            qtdo = _dot_tn(qeb, dob)
            yield
            drhs = dvn + _sdot_tn(f["tinv"], dvn)
            dqk = _b(dpm * f["dm"])
            dq = _dot(dqk, kbf) + dqe * eg
            dk_p = _dot_tn(dqk, _b(q))
            yield
            dam = jnp.where(ri > ci, -_sdot_nt(drhs, vn), 0.0)
            de = drhs * beta
            deb = _b(de)
            dbeta = _sum1(drhs * e)
            dkeg = -_dot_nt(deb, sbf)
            dst_new = qtdo + dst * jnp.exp(f["glast"]) - _dot_tn(kegb, deb)
            yield
            dkk = _b(dam * f["ds"])
            dkb = _dot(dkk, kbf)
            dk = dk_p + _dot_tn(dkk, _b(f["kb"])) + dkb * beta + dkeg * eg + dkd * ek
            yield
            dbeta = dbeta + _sum1(dkb * k)
            mm = dpm * f["pmat"] + dam * f["amat"]
            dglast = _sum1(_sum0(dst * st)) * jnp.exp(f["glast"]) + _sum1(_sum0(dkd * kd))
            dgc = (_sum1(mm) - _sum1(eye * _sum0(mm)) + _sum1(dqe * qe) + _sum1(dkeg * keg) - _sum1(dkd * kd))
            dgc = jnp.broadcast_to(dgc, (CHUNK, HD)) + jnp.where(row64 == CHUNK - 1, dglast, 0.0)
            dg = _dot(utri, dgc, HI)
            yield
            dbin = jnp.broadcast_to(dbeta, (CHUNK, HD)) * beta * (1.0 - beta)
            dain = dg * (-f["ea"]) * _sig(f["a_in"] + f["dt"])
            dba = jnp.where(lane == hd, dbin, 0.0) + jnp.where(lane == HEADS + hd, dain, 0.0)
            sa0 = sa0 + _sum0(dg * f["g"])
            sa1 = sa1 + _sum0(dain)
            dq_raw = (QSCALE * f["rq"]) * (dq - f["qn"] * _sum1(f["qn"] * dq))
            dk_raw = f["rk"] * (dk - k * _sum1(k * dk))
            dact = jnp.concatenate([dq_raw, dk_raw, de], axis=1)
            dc = dact * _dsilu(f["cpre"], f["sg"])
            xs = f["xs"]
            cwacc = cwacc + jnp.concatenate([_sum0(dc * xs[0]), _sum0(dc * xs[1]), _sum0(dc * xs[2]),
                                             _sum0(dc * xs[3])], axis=0)
            dqkv = _conv_back(dc, dch, cw3)
            return dpz, dqkv, dba, ddnw, (dst_new, dc[0:8], cwacc, sa0, sa1)

        def chunk(cidx, carry):
            heads, nacc = carry
            c = nc - 1 - cidx
            t0 = pl.multiple_of(c * CHUNK, CHUNK)
            rows = pl.ds(t0, CHUNK)
            bav = ba_ref[rows, :]
            ins = []
            for j in range(hp):
                hl = slice(j * HD, (j + 1) * HD)
                ins.append(_dn_load_qkv(q_ref, qh_ref, k_ref, kh_ref, v_ref, vh_ref, hl, rows, t0, c, sblk)
                           + (bav, z_ref[rows, hl], st_ref[j, c], o_ref[rows, hl], dy_ref[rows, hl]))
            outs = _round_robin([one_head(j, *ins[j], heads[j]) for j in range(hp)])
            dba = outs[0][2]
            for j in range(hp):
                hl = slice(j * HD, (j + 1) * HD)
                dpz, dqkv, dba_j, ddnw, _ = outs[j]
                dq_ref[rows, hl] = dqkv[:, 0:HD]
                dk_ref[rows, hl] = dqkv[:, HD:2 * HD]
                dv_ref[rows, hl] = dqkv[:, 2 * HD:3 * HD]
                dz_ref[rows, hl] = dpz
                nacc = nacc + ddnw
                if j > 0:
                    dba = dba + dba_j

            @pl.when(grp == 0)
            def _():
                dba_ref[rows, :] = dba

            @pl.when(grp > 0)
            def _():
                dba_ref[rows, :] += dba

            return tuple(out[4] for out in outs), nacc

        init = tuple((ds_ref[h0 + j], dch_ref[h0 + j], cwacc_ref[h0 + j][0:4], sacc_ref[h0 + j][0:1],
                      sacc_ref[h0 + j][1:2]) for j in range(hp))
        heads, nacc = lax.fori_loop(0, nc, chunk, (init, nacc_ref[0:1, :]))
        nacc_ref[0:1, :] = nacc
        zpad = jnp.zeros((4, 3 * HD), f32)
        zrow = jnp.zeros((6, HD), f32)
        for j in range(hp):
            dst, dch, cwacc, sa0, sa1 = heads[j]
            ds_ref[h0 + j] = dst
            dch_ref[h0 + j] = dch
            cwacc_ref[h0 + j] = jnp.concatenate([cwacc, zpad], axis=0)
            sacc_ref[h0 + j] = jnp.concatenate([sa0, sa1, zrow], axis=0)

        @pl.when(step == nsb - 1)
        def _():
            lane8 = lax.broadcasted_iota(jnp.int32, (8, HD), 1)
            row8 = lax.broadcasted_iota(jnp.int32, (8, HD), 0)
            mine = jnp.zeros((8, HD), f32)
            for j in range(hp):
                dcw_ref[h0 + j] = heads[j][2]
                sa = jnp.concatenate([heads[j][3], heads[j][4], zrow], axis=0)
                mine = mine + jnp.where((lane8 == h0 + j) & (row8 < 2), sa, 0.0)

            @pl.when(grp == 0)
            def _():
                dsc_ref[...] = mine

            @pl.when(grp > 0)
            def _():
                dsc_ref[...] += mine

            @pl.when(grp == ngrp - 1)
            def _():
                dsc_ref[2:3, :] = nacc

    wide = hp * HD
    rblk = lambda i, g: nsb - 1 - i
    grp_specs = lambda off: _seq_specs(sb, wide, rblk, lambda i, g: off // hp + g)
    (qsp, qhalo), (ksp, khalo), (vsp, vhalo) = grp_specs(2 * HEADS), grp_specs(3 * HEADS), grp_specs(4 * HEADS)
    zsp, _ = grp_specs(5 * HEADS)
    hsp = lambda off: pl.BlockSpec((sb, wide), lambda i, g: (rblk(i, g), off // hp + g))
    basp = lambda col: pl.BlockSpec((sb, HD), lambda i, g: (rblk(i, g), col))
    cws = lambda off: pl.BlockSpec((4, wide), lambda i, g: (0, off // hp + g))
    whole = lambda shape: pl.BlockSpec(shape, lambda i, g: (0,) * len(shape))
    return pl.pallas_call(
        body, name=name, grid=(nsb, ngrp),
        in_specs=[hsp(HEADS), qsp, qhalo, ksp, khalo, vsp, vhalo, zsp, basp(BLK_BA), hsp(0),
                  pl.BlockSpec((hp, nc, HD, HD), lambda i, g: (g, rblk(i, g), 0, 0)),
                  cws(0), cws(HEADS), cws(2 * HEADS), whole((8, HD))],
        out_specs=[hsp(0), hsp(0), hsp(0), hsp(0), basp(0), whole((HEADS, 4, 3 * HD)), whole((8, HD))],
        out_shape=[jax.ShapeDtypeStruct((s, D_MODEL), f32)] * 4
        + [jax.ShapeDtypeStruct((s, HD), f32), jax.ShapeDtypeStruct((HEADS, 4, 3 * HD), f32),
           jax.ShapeDtypeStruct((8, HD), f32)],
        scratch_shapes=[pltpu.VMEM((HEADS, HD, HD), f32), pltpu.VMEM((HEADS, 8, 3 * HD), f32),
                        pltpu.VMEM((HEADS, 8, 3 * HD), f32), pltpu.VMEM((HEADS, 8, HD), f32), pltpu.VMEM((8, HD), f32)],
        compiler_params=_params(("arbitrary", "arbitrary")),
    )(dy, proj, proj, proj, proj, proj, proj, proj, proj, o, states, cw, cw, cw, dnv)


ANY = pl.BlockSpec(memory_space=pl.ANY)
CHIP_MASKS = ((1, 0, 0), (0, 1, 0), (1, 1, 0))
ALL_MASKS = tuple((m >> 2 & 1, m >> 1 & 1, m & 1) for m in range(1, N_DEV))


def _me():
    return lax.axis_index("x"), lax.axis_index("y"), lax.axis_index("c")


def _flip(me, mask):
    return tuple((1 - v) if m else v for v, m in zip(me, mask))


def _dev_index(dev):
    return 4 * dev[0] + 2 * dev[1] + dev[2]


def _chip(dev):
    return 2 * dev[0] + dev[1]


def _comm_call(name, body, arrays, out_shapes, nsem, nloc=0):
    scratch = [pltpu.SemaphoreType.DMA((nsem,)), pltpu.SemaphoreType.DMA((nsem,))]
    if nloc:
        scratch.append(pltpu.SemaphoreType.DMA((nloc,)))
    return pl.pallas_call(
        body, name=name, in_specs=[ANY] * len(arrays), out_specs=[ANY] * len(out_shapes), out_shape=out_shapes,
        scratch_shapes=scratch, compiler_params=pltpu.CompilerParams(has_side_effects=True),
    )(*arrays)


def _put_own(gathered, own, slot):
    return lax.dynamic_update_index_in_dim(gathered, own, slot, 0)


def _gather_halves(arrays, whole_arrays, name):
    na, nw = len(arrays), len(whole_arrays)
    nm = len(CHIP_MASKS)

    def body(*refs):
        srcs, dsts = refs[:na + nw], refs[na + nw:2 * (na + nw)]
        send_sems, recv_sems = refs[2 * (na + nw):]
        me = _me()
        sib = _flip(me, (0, 0, 1))

        def half(a, ref, core):
            rows = arrays[a].shape[0] // 2
            return ref.at[pl.ds(pl.multiple_of(core * rows, rows), rows)]

        def over_ici(a, mi, sender, to):
            if a < na:
                src, dst = half(a, srcs[a], sender[2]), half(a, dsts[a].at[_chip(sender)], sender[2])
            else:
                src, dst = srcs[a], dsts[a].at[_chip(sender)]
            return pltpu.make_async_remote_copy(src_ref=src, dst_ref=dst, send_sem=send_sems.at[a * nm + mi],
                                                recv_sem=recv_sems.at[a * nm + mi], device_id=to, device_id_type=MESH)

        def over_d2d(a, mi, origin, sender, to):
            k = (na + nw) * nm + a * nm + mi
            blk = half(a, dsts[a].at[_chip(origin)], sender[2])
            return pltpu.make_async_remote_copy(src_ref=blk, dst_ref=blk, send_sem=send_sems.at[k],
                                                recv_sem=recv_sems.at[k], device_id=to, device_id_type=MESH)

        sends = []
        for a in range(na + nw):
            for mi, mask in enumerate(CHIP_MASKS):
                cp = over_ici(a, mi, me, _flip(me, mask))
                cp.start()
                sends.append(cp)
        for a in range(na + nw):
            for mi, mask in enumerate(CHIP_MASKS):
                peer = _flip(me, mask)
                over_ici(a, mi, peer, me).wait_recv()
                if a < na:
                    cp = over_d2d(a, mi, peer, me, sib)
                    cp.start()
                    sends.append(cp)
        for a in range(na):
            for mi, mask in enumerate(CHIP_MASKS):
                over_d2d(a, mi, _flip(sib, mask), sib, me).wait_recv()
        for cp in sends:
            cp.wait_send()

    every = list(arrays) + list(whole_arrays)
    got = _comm_call(name, body, every, [jax.ShapeDtypeStruct((4,) + t.shape, t.dtype) for t in every],
                     (2 * na + nw) * nm)
    chip = 2 * lax.axis_index("x") + lax.axis_index("y")
    return [_put_own(g, t, chip) for g, t in zip(got, every)]


def _scatter_layer(win, wout, small, name):
    arrays = [win, wout] + ([small] if small is not None else [])
    na = len(arrays)
    half_in, half_out = D_MODEL // 2, D_MIX // 8

    def piece(a, ref, d):
        j, r = d >> 1, d & 1
        if a == 0:
            return ref.at[pl.ds(half_in * r, half_in), pl.ds(WIN_STRIDE * j, WIN_W)]
        if a == 1:
            return ref.at[pl.ds(2 * half_out * j + half_out * r, half_out)]
        return ref.at[d]

    def body(*refs):
        srcs, dsts = refs[:na], refs[na:2 * na]
        send_sems, recv_sems, loc_sems = refs[2 * na:]
        me = _me()
        my = _dev_index(me)
        for d in range(N_DEV):
            dev = (d >> 2, (d >> 1) & 1, d & 1)
            for a in range(na):
                @pl.when(my != d)
                def _(a=a, d=d, dev=dev):
                    pltpu.make_async_remote_copy(
                        src_ref=piece(a, srcs[a], d), dst_ref=dsts[a].at[my], send_sem=send_sems.at[a * N_DEV + d],
                        recv_sem=recv_sems.at[a * N_DEV + my], device_id=dev, device_id_type=MESH).start()

                @pl.when(my == d)
                def _(a=a, d=d):
                    pltpu.make_async_copy(piece(a, srcs[a], d), dsts[a].at[d], loc_sems.at[a]).start()
        for s in range(N_DEV):
            for a in range(na):
                @pl.when(my != s)
                def _(a=a, s=s):
                    pltpu.make_async_remote_copy(
                        src_ref=piece(a, srcs[a], s), dst_ref=dsts[a].at[s], send_sem=send_sems.at[a * N_DEV + s],
                        recv_sem=recv_sems.at[a * N_DEV + s], device_id=me, device_id_type=MESH).wait_recv()
        for d in range(N_DEV):
            for a in range(na):
                @pl.when(my != d)
                def _(a=a, d=d):
                    pltpu.make_async_remote_copy(
                        src_ref=piece(a, srcs[a], d), dst_ref=dsts[a].at[d], send_sem=send_sems.at[a * N_DEV + d],
                        recv_sem=recv_sems.at[a * N_DEV + d], device_id=me, device_id_type=MESH).wait_send()

                @pl.when(my == d)
                def _(a=a, d=d):
                    pltpu.make_async_copy(piece(a, srcs[a], d), dsts[a].at[d], loc_sems.at[a]).wait()

    shapes = [jax.ShapeDtypeStruct((N_DEV, half_in, WIN_W), win.dtype),
              jax.ShapeDtypeStruct((N_DEV, half_out, D_MODEL), wout.dtype)]
    if small is not None:
        shapes.append(jax.ShapeDtypeStruct(small.shape, small.dtype))
    return _comm_call(name, body, arrays, shapes, na * N_DEV, na)


def _share_reduced(pair_arrays, small, name):
    arrays = list(pair_arrays) + ([small] if small is not None else [])
    na, npair = len(arrays), len(pair_arrays)
    nm = len(ALL_MASKS)

    def body(*refs):
        srcs, dsts = refs[:na], refs[na:2 * na]
        send_sems, recv_sems = refs[2 * na:]
        me = _me()
        sib = _flip(me, (0, 0, 1))

        def copy(a, k, sender, to):
            slot = sender[2] if a < npair else _dev_index(sender)
            return pltpu.make_async_remote_copy(
                src_ref=srcs[a], dst_ref=dsts[a].at[slot], send_sem=send_sems.at[k], recv_sem=recv_sems.at[k],
                device_id=to, device_id_type=MESH)

        sends = [copy(a, a, me, sib) for a in range(npair)]
        if small is not None:
            sends += [copy(npair, npair + mi, me, _flip(me, mask)) for mi, mask in enumerate(ALL_MASKS)]
        for cp in sends:
            cp.start()
        for a in range(npair):
            copy(a, a, sib, me).wait_recv()
        if small is not None:
            for mi, mask in enumerate(ALL_MASKS):
                copy(npair, npair + mi, _flip(me, mask), me).wait_recv()
        for cp in sends:
            cp.wait_send()

    shapes = [jax.ShapeDtypeStruct((2,) + t.shape, t.dtype) for t in pair_arrays]
    if small is not None:
        shapes.append(jax.ShapeDtypeStruct((N_DEV,) + small.shape, small.dtype))
    got = _comm_call(name, body, arrays, shapes, npair + (nm if small is not None else 0))
    x, y, c = _me()
    slots = [c] * npair + [4 * x + 2 * y + c]
    return [_put_own(g, t, slot) for g, t, slot in zip(got, arrays, slots)]


def _sum_parts(parts, name):
    _, r, c = parts.shape
    tr = r
    while tr * c * 4 * N_DEV > (8 << 20) and tr % 32 == 0:
        tr //= 2

    def body(p_ref, o_ref):
        total = p_ref[0].astype(f32)
        for d in range(1, N_DEV):
            total = total + p_ref[d].astype(f32)
        o_ref[...] = total

    return pl.pallas_call(
        body, name=name, grid=(r // tr,), in_specs=[pl.BlockSpec((N_DEV, tr, c), lambda i: (0, i, 0))],
        out_specs=pl.BlockSpec((tr, c), lambda i: (i, 0)), out_shape=jax.ShapeDtypeStruct((r, c), f32),
        compiler_params=_params(("parallel",)),
    )(parts)


def _adamw(w, g, m, v, name):
    shape = w.shape
    cols = shape[-1]
    rows = w.size // cols
    tr = rows
    while tr * cols * 4 > (1 << 20) and tr % 16 == 0:
        tr //= 2
    c1 = 1.0 / (1.0 - ADAM_B1 ** ADAM_STEP)
    c2 = 1.0 / (1.0 - ADAM_B2 ** ADAM_STEP)

    def body(w_ref, g_ref, m_ref, v_ref, d_ref, nm_ref, nv_ref):
        gv = g_ref[...]
        mv = ADAM_B1 * m_ref[...] + (1.0 - ADAM_B1) * gv
        vv = ADAM_B2 * v_ref[...] + (1.0 - ADAM_B2) * (gv * gv)
        nm_ref[...] = mv
        nv_ref[...] = vv
        d_ref[...] = -ADAM_LR * ((mv * c1) / (jnp.sqrt(vv * c2) + ADAM_EPS) + ADAM_WD * w_ref[...])

    spec = pl.BlockSpec((tr, cols), lambda i: (i, 0))
    outs = pl.pallas_call(
        body, name=name, grid=(rows // tr,), in_specs=[spec] * 4, out_specs=[spec] * 3,
        out_shape=[jax.ShapeDtypeStruct((rows, cols), f32)] * 3, compiler_params=_params(("parallel",)),
    )(*[t.reshape(rows, cols) for t in (w, g, m, v)])
    return tuple(t.reshape(shape) for t in outs)


def _pad_lanes(t, n=HD):
    return jnp.pad(t, [(0, 0)] * (t.ndim - 1) + [(0, n - t.shape[-1])])


def _rows128(t, rows=None):
    t = t.reshape(-1, HD)
    rows = rows or -(-t.shape[0] // 8) * 8
    return jnp.pad(t, ((0, rows - t.shape[0]), (0, 0)))


def kernel(x, norm_w, w_in, lru_conv_w, lru_conv_b, lru_wa, lru_ba, lru_wx, lru_bx, lru_lambda, lru_norm_w, dn_conv_w, dn_A_log, dn_dt_bias, dn_norm_w, w_out, final_norm_w, loss_target, m_norm_w, m_w_in, m_lru_conv_w, m_lru_conv_b, m_lru_wa, m_lru_ba, m_lru_wx, m_lru_bx, m_lru_lambda, m_lru_norm_w, m_dn_conv_w, m_dn_A_log, m_dn_dt_bias, m_dn_norm_w, m_w_out, m_final_norm_w, v_norm_w, v_w_in, v_lru_conv_w, v_lru_conv_b, v_lru_wa, v_lru_ba, v_lru_wx, v_lru_bx, v_lru_lambda, v_lru_norm_w, v_dn_conv_w, v_dn_A_log, v_dn_dt_bias, v_dn_norm_w, v_w_out, v_final_norm_w):
    depth = norm_w.shape[0]
    xs = x[0]
    target = loss_target[0]
    chip = 2 * lax.axis_index("x") + lax.axis_index("y")

    win_b, wout_b = w_in.astype(bf16), w_out.astype(bf16)
    wp, wo = [], []
    for l in range(depth):
        small_shards = [lru_conv_w, dn_conv_w] if l == 0 else []
        got = _gather_halves([win_b[l], wout_b[l]], small_shards, f"gather_weights{l}")
        wp.append(_pad_lanes(jnp.concatenate([got[0][j] for j in range(4)], axis=1), D_INP))
        wo.append(got[1].reshape(D_MIX, D_MODEL))
        if l == 0:
            lcw_full = got[2].transpose(1, 2, 0, 3).reshape(depth, 4, D_MODEL)
            dcw_full = got[3].transpose(1, 2, 0, 3).reshape(depth, 4, 3 * D_MODEL)
    vecs, dnvs = [], []
    zrow = jnp.zeros((D_MODEL,), f32)
    for l in range(depth):
        vecs.append(jnp.stack([lru_conv_b[l], lru_ba[l], lru_bx[l], lru_lambda[l], lru_norm_w[l], zrow, zrow, zrow]))
        dnvs.append(jnp.concatenate([_pad_lanes(dn_A_log[l][None]), _pad_lanes(dn_dt_bias[l][None]),
                                     dn_norm_w[l][None], jnp.zeros((5, HD), f32)], axis=0))

    loss_part, grad_x, d_fnw, g_nw, g_vec, g_wa, g_wx, g_lcw_, g_dcw_, g_dsc, g_win, g_wout = _local_step(
        xs, target, norm_w, final_norm_w, wp, wo, lcw_full, dcw_full, vecs, dnvs, lru_wa, lru_wx)
    loss = lax.psum(loss_part[0, 0], ("x", "y", "c"))
    grads = _reduce_grads(chip, d_fnw, g_nw, g_vec, g_wa, g_wx, g_lcw_, g_dcw_, g_dsc, g_win, g_wout)

    names = ["norm_w", "w_in", "lru_conv_w", "lru_conv_b", "lru_wa", "lru_ba", "lru_wx", "lru_bx", "lru_lambda",
             "lru_norm_w", "dn_conv_w", "dn_A_log", "dn_dt_bias", "dn_norm_w", "w_out", "final_norm_w"]
    ws = dict(zip(names, [norm_w, w_in, lru_conv_w, lru_conv_b, lru_wa, lru_ba, lru_wx, lru_bx, lru_lambda, lru_norm_w,
                          dn_conv_w, dn_A_log, dn_dt_bias, dn_norm_w, w_out, final_norm_w]))
    ms = dict(zip(names, [m_norm_w, m_w_in, m_lru_conv_w, m_lru_conv_b, m_lru_wa, m_lru_ba, m_lru_wx, m_lru_bx,
                          m_lru_lambda, m_lru_norm_w, m_dn_conv_w, m_dn_A_log, m_dn_dt_bias, m_dn_norm_w, m_w_out,
                          m_final_norm_w]))
    vs = dict(zip(names, [v_norm_w, v_w_in, v_lru_conv_w, v_lru_conv_b, v_lru_wa, v_lru_ba, v_lru_wx, v_lru_bx,
                          v_lru_lambda, v_lru_norm_w, v_dn_conv_w, v_dn_A_log, v_dn_dt_bias, v_dn_norm_w, v_w_out,
                          v_final_norm_w]))
    deltas, new_m, new_v = [], [], []
    for n in names:
        d, nm_, nv_ = _adamw(ws[n], grads[n], ms[n], vs[n], f"adamw_{n}")
        deltas.append(d)
        new_m.append(nm_)
        new_v.append(nv_)
    return (loss, grad_x[None], *[grads[n] for n in names], *deltas, *new_m, *new_v)


def _local_step(xs, target, norm_w, final_norm_w, wp, wo, lcw_full, dcw_full, vecs, dnvs, lru_wa, lru_wx):
    depth = norm_w.shape[0]
    saved = []
    h = xs
    for l in range(depth):
        hn = _rms_fwd(h, norm_w[l][None], f"rms_fwd{l}")
        proj = _matmul(hn, wp[l], tm=512, tn=896, tk=D_MODEL, name=f"in_proj{l}")
        y, hl = _lru_fwd(proj, lcw_full[l], vecs[l], lru_wa[l], lru_wx[l], f"lru_fwd{l}")
        y, o, states = _dn_fwd(proj, dcw_full[l], dnvs[l], y, f"dn_fwd{l}")
        out = _matmul(y, wo[l], tm=512, tn=D_MODEL, tk=D_MIX, res=h, name=f"out_proj{l}")
        saved.append((h, hn, proj, hl, o, states, y))
        h = out

    dh, d_fnw, loss_part = _final_loss(h, final_norm_w[None], target, "final_loss")

    g_nw, g_vec, g_wa, g_wx, g_lcw_, g_dcw_, g_dsc, g_win, g_wout = ([None] * depth for _ in range(9))
    for l in reversed(range(depth)):
        hin, hn, proj, hl, o, states, y = saved[l]
        dy = _matmul(dh, wo[l], tb=True, tm=512, tn=1024, tk=D_MODEL, name=f"d_y{l}")
        g_wout[l] = _matmul(y, dh, ta=True, tm=512, tn=D_MODEL, tk=512, out_dtype=bf16, name=f"d_wout{l}")
        dlx, dlz, g_lcw_[l], g_vec[l], g_wa[l], g_wx[l] = _lru_bwd(dy, proj, hl, lcw_full[l], vecs[l], lru_wa[l],
                                                                 lru_wx[l], f"lru_bwd{l}")
        dq, dk, dv, dz, dba, dcw8, g_dsc[l] = _dn_bwd(dy, proj, o, states, dcw_full[l], dnvs[l], f"dn_bwd{l}")
        g_dcw_[l] = dcw8.reshape(HEADS, 4, 3, HD).transpose(1, 2, 0, 3).reshape(4, 3 * D_MODEL)
        dxs = [dlx, dlz, dq, dk, dv, dz]
        dhn = _d_hn(dxs, dba, wp[l], f"d_hn{l}")
        g_win[l] = _d_win(hn, dxs, dba, f"d_win{l}")
        dh, g_nw[l] = _rms_bwd(dhn, hin, norm_w[l][None], dh, f"rms_bwd{l}")
    return loss_part, dh, d_fnw, g_nw, g_vec, g_wa, g_wx, g_lcw_, g_dcw_, g_dsc, g_win, g_wout


def _reduce_grads(chip, d_fnw, g_nw, g_vec, g_wa, g_wx, g_lcw_, g_dcw_, g_dsc, g_win, g_wout):
    depth = len(g_nw)
    both = lambda f: jnp.concatenate([f(l) for l in range(depth)])
    small = [
        both(lambda l: _rows128(g_nw[l])),
        both(lambda l: _rows128(g_vec[l][0])), both(lambda l: _rows128(g_vec[l][1])),
        both(lambda l: _rows128(g_vec[l][2])), both(lambda l: _rows128(g_vec[l][3])),
        both(lambda l: _rows128(g_vec[l][4])),
        both(lambda l: _rows128(g_wa[l])), both(lambda l: _rows128(g_wx[l])),
        both(lambda l: _rows128(g_dsc[l][0:1])), both(lambda l: _rows128(g_dsc[l][1:2])),
        both(lambda l: _rows128(g_dsc[l][2:3])),
        _rows128(d_fnw),
        both(lambda l: _rows128(g_lcw_[l])), both(lambda l: _rows128(g_dcw_[l])),
    ]
    counts = [t.shape[0] for t in small]
    total = sum(counts)
    per = -(-total // (8 * N_DEV)) * 8
    small = jnp.concatenate(small + [jnp.zeros((per * N_DEV - total, HD), f32)]).reshape(N_DEV, per, HD)

    pairs = [None] * depth
    a_small = None
    for l in reversed(range(depth)):
        got = _scatter_layer(g_win[l], g_wout[l], small if l == 0 else None, f"scatter_grads{l}")
        sums = [_sum_parts(got[0], f"sum_win{l}"), _sum_parts(got[1], f"sum_wout{l}")]
        if l == 0:
            shared = _share_reduced(sums, _sum_parts(got[2], "sum_small"), f"share_grads{l}")
            a_small = shared[2].reshape(N_DEV * per, HD)
        else:
            shared = _share_reduced(sums, None, f"share_grads{l}")
        pairs[l] = shared
    grad_w_in = jnp.stack([lax.dynamic_slice_in_dim(pairs[l][0].reshape(D_MODEL, WIN_W), 4 * chip, SHARD_IN, 1)
                           for l in range(depth)])
    grad_w_out = jnp.stack([pairs[l][1].reshape(D_MIX // 4, D_MODEL) for l in range(depth)])

    offs = [0]
    for c in counts:
        offs.append(offs[-1] + c)

    def piece(k, rows_per_layer=None):
        t = a_small[offs[k]:offs[k + 1]]
        if rows_per_layer is None:
            return t
        return t.reshape(depth, -1, HD)[:, :rows_per_layer]

    vec = lambda k: piece(k).reshape(depth, D_MODEL)
    return {
        "norm_w": vec(0), "lru_conv_b": vec(1), "lru_ba": vec(2), "lru_bx": vec(3), "lru_lambda": vec(4),
        "lru_norm_w": vec(5),
        "lru_wa": piece(6).reshape(depth, HEADS, HD, HD), "lru_wx": piece(7).reshape(depth, HEADS, HD, HD),
        "dn_A_log": piece(8, 1)[:, 0, :HEADS], "dn_dt_bias": piece(9, 1)[:, 0, :HEADS], "dn_norm_w": piece(10, 1)[:, 0],
        "final_norm_w": piece(11).reshape(D_MODEL),
        "lru_conv_w": lax.dynamic_slice_in_dim(piece(12).reshape(depth, 4, D_MODEL), chip * (D_MODEL // 4), D_MODEL // 4, 2),
        "dn_conv_w": lax.dynamic_slice_in_dim(piece(13).reshape(depth, 4, 3 * D_MODEL), chip * (3 * D_MODEL // 4),
                                              3 * D_MODEL // 4, 2),
        "w_in": grad_w_in, "w_out": grad_w_out,
    }
```

```python
import jax
import jax.numpy as jnp
from jax import lax
from jax.experimental import pallas as pl
from jax.experimental.pallas import tpu as pltpu

f32 = jnp.float32
bf16 = jnp.bfloat16
HI = lax.Precision.HIGHEST
MESH = pl.DeviceIdType.MESH

D_MODEL = 1024
HEADS = 8
HD = 128
CHUNK = 64
LRU_T = 128
SEQ_BLOCK = 1024
DN_SEQ_BLOCK = 512
DN_HP = 4
LRU_C = 8.0
D_IN = 6160
D_INP = 6272
D_MIX = 2048
N_GROUPS = 6
BLK_BA = 48
SHARD_IN = D_IN // 4
WIN_W = 1664
WIN_STRIDE = 1536
EPS = 1e-6
QSCALE = HD ** -0.5
N_DEV = 8
VMEM_LIMIT = 56 * 1024 * 1024

ADAM_LR, ADAM_B1, ADAM_B2, ADAM_EPS, ADAM_WD, ADAM_STEP = 0.001, 0.9, 0.999, 1e-08, 0.01, 10


def _sig(x):
    return 1.0 / (1.0 + jnp.exp(-x))


def _log1p(u):
    w = 1.0 + u
    d = w - 1.0
    return jnp.where(d == 0.0, u, jnp.log(w) * (u / jnp.where(d == 0.0, 1.0, d)))


def _softplus(x):
    return jnp.maximum(x, 0.0) + _log1p(jnp.exp(-jnp.abs(x)))


def _expm1(x):
    t = x * (1.0 + x * (0.5 + x * (1.0 / 6 + x * (1.0 / 24 + x * (1.0 / 120 + x * (1.0 / 720))))))
    return jnp.where(jnp.abs(x) < 0.2, t, jnp.exp(x) - 1.0)


def _dot(a, b, prec=None):
    return jnp.dot(a, b, precision=prec, preferred_element_type=f32)


def _dot_nt(a, b, prec=None):
    return lax.dot_general(a, b, (((1,), (1,)), ((), ())), precision=prec, preferred_element_type=f32)


def _dot_tn(a, b, prec=None):
    return lax.dot_general(a, b, (((0,), (0,)), ((), ())), precision=prec, preferred_element_type=f32)


def _b(x):
    return x.astype(bf16)


def _sum0(x):
    return jnp.sum(x, axis=0, keepdims=True)


def _sum1(x):
    return jnp.sum(x, axis=1, keepdims=True)


def _rowmean(x):
    return jnp.mean(x, axis=-1, keepdims=True)


def _dsilu(z, sg):
    return sg * (1.0 + z * (1.0 - sg))


def _conv_taps(x, halo):
    xx = jnp.concatenate([halo, x], axis=0)
    return [pltpu.roll(xx, 3, axis=0)[8:], pltpu.roll(xx, 2, axis=0)[8:], pltpu.roll(xx, 1, axis=0)[8:], x]


def _conv(xs, cw):
    return cw[0:1] * xs[0] + cw[1:2] * xs[1] + cw[2:3] * xs[2] + cw[3:4] * xs[3]


def _rows_before(ref, halo_ref, lanes, t0, c, sblk):
    tprev = pl.multiple_of(jnp.maximum(t0 - 8, 0), 8)
    return jnp.where(c > 0, ref[pl.ds(tprev, 8), lanes], jnp.where(sblk > 0, halo_ref[:, lanes], 0.0))


def _conv_back(dxc, halo_after, cw):
    rows = dxc.shape[0]
    dd = jnp.concatenate([dxc, halo_after], axis=0)
    out = cw[3:4] * dxc
    for s in (1, 2, 3):
        out = out + cw[3 - s:4 - s] * pltpu.roll(dd, rows + 8 - s, axis=0)[:rows]
    return out


def _params(sem=None):
    return pltpu.CompilerParams(dimension_semantics=sem, vmem_limit_bytes=VMEM_LIMIT)


def _seq_specs(sb, width, rowblk, colblk):
    main = pl.BlockSpec((sb, width), lambda a, b: (rowblk(a, b), colblk(a, b)))
    halo = pl.BlockSpec((8, width), lambda a, b: (jnp.maximum(rowblk(a, b) * (sb // 8) - 1, 0), colblk(a, b)))
    return main, halo


def _matmul(a, b, *, ta=False, tb=False, tm, tn, tk, res=None, out_dtype=f32, name):
    if ta:
        kdim, m = a.shape
    else:
        m, kdim = a.shape
    n = b.shape[0] if tb else b.shape[1]
    tm, tn, tk = min(tm, m), min(tn, n), min(tk, kdim)
    assert m % tm == 0 and n % tn == 0 and kdim % tk == 0, (name, m, n, kdim, tm, tn, tk)
    nk = kdim // tk
    dims = (((0 if ta else 1,), (1 if tb else 0,)), ((), ()))
    has_res = res is not None

    def body(*refs):
        a_ref, b_ref = refs[:2]
        r_ref = refs[2] if has_res else None
        o_ref = refs[3] if has_res else refs[2]
        acc_ref = refs[-1] if nk > 1 else None
        part = lax.dot_general(_b(a_ref[...]), _b(b_ref[...]), dims, preferred_element_type=f32)

        def finish(total):
            if has_res:
                total = total + r_ref[...]
            o_ref[...] = total.astype(out_dtype)

        if nk == 1:
            finish(part)
        else:
            k = pl.program_id(2)

            @pl.when(k == 0)
            def _():
                acc_ref[...] = part

            @pl.when(k > 0)
            def _():
                acc_ref[...] += part

            @pl.when(k == nk - 1)
            def _():
                finish(acc_ref[...])

    a_spec = pl.BlockSpec((tk, tm), lambda i, j, k: (k, i)) if ta else pl.BlockSpec((tm, tk), lambda i, j, k: (i, k))
    b_spec = pl.BlockSpec((tn, tk), lambda i, j, k: (j, k)) if tb else pl.BlockSpec((tk, tn), lambda i, j, k: (k, j))
    o_spec = pl.BlockSpec((tm, tn), lambda i, j, k: (i, j))
    in_specs, args = [a_spec, b_spec], [a, b]
    if has_res:
        in_specs.append(o_spec)
        args.append(res)
    return pl.pallas_call(
        body, name=name, grid=(m // tm, n // tn, nk), in_specs=in_specs, out_specs=o_spec,
        out_shape=jax.ShapeDtypeStruct((m, n), out_dtype),
        scratch_shapes=[pltpu.VMEM((tm, tn), f32)] if nk > 1 else [],
        compiler_params=_params(("parallel", "parallel", "arbitrary")),
    )(*args)


def _transpose(x, name):
    r, c = x.shape
    tr, tc = min(512, r), min(512, c)

    def body(x_ref, o_ref):
        o_ref[...] = x_ref[...].T

    return pl.pallas_call(
        body, name=name, grid=(r // tr, c // tc), in_specs=[pl.BlockSpec((tr, tc), lambda i, j: (i, j))],
        out_specs=pl.BlockSpec((tc, tr), lambda i, j: (j, i)), out_shape=jax.ShapeDtypeStruct((c, r), x.dtype),
        compiler_params=_params(("parallel", "parallel")),
    )(x)


def _d_hn(dxs, dba, wp, name):
    s = dxs[0].shape[0]
    tm = min(1024, s)

    def body(*refs):
        dx_refs = refs[:N_GROUPS]
        dba_ref, w_ref, wba_ref, o_ref, acc_ref = refs[N_GROUPS:]
        k = pl.program_id(1)
        for g in range(N_GROUPS):
            @pl.when(k == g)
            def _(g=g):
                part = _dot_nt(_b(dx_refs[g][...]), w_ref[...])
                if g == 0:
                    acc_ref[...] = part + _dot_nt(_b(dba_ref[...]), wba_ref[...])
                else:
                    acc_ref[...] += part

        @pl.when(k == N_GROUPS - 1)
        def _():
            o_ref[...] = acc_ref[...]

    row = lambda w: pl.BlockSpec((tm, w), lambda i, k: (i, 0))
    return pl.pallas_call(
        body, name=name, grid=(s // tm, N_GROUPS),
        in_specs=[row(D_MODEL)] * N_GROUPS + [row(HD), pl.BlockSpec((D_MODEL, D_MODEL), lambda i, k: (0, k)),
                                              pl.BlockSpec((D_MODEL, HD), lambda i, k: (0, BLK_BA))],
        out_specs=row(D_MODEL), out_shape=jax.ShapeDtypeStruct((s, D_MODEL), f32),
        scratch_shapes=[pltpu.VMEM((tm, D_MODEL), f32)],
        compiler_params=_params(("parallel", "arbitrary")),
    )(*dxs, dba, wp, wp)


def _d_win(hnt, dxs, dba, name):
    s = hnt.shape[1]
    tn = 256
    per = D_MODEL // tn

    def body(*refs):
        hnt_ref = refs[0]
        dx_refs = refs[1:1 + N_GROUPS]
        o_ref = refs[1 + N_GROUPS]
        j = pl.program_id(0)
        for g in range(N_GROUPS):
            @pl.when(j // per == g)
            def _(g=g):
                o_ref[...] = _dot(hnt_ref[...], _b(dx_refs[g][...])).astype(bf16)

    def dx_spec(g):
        on = lambda j: (j // per == g).astype(jnp.int32)
        return pl.BlockSpec((s, tn), lambda j: (0, (j % per) * on(j)))

    main = pl.pallas_call(
        body, name=name, grid=(N_GROUPS * per,),
        in_specs=[pl.BlockSpec((D_MODEL, s), lambda j: (0, 0))] + [dx_spec(g) for g in range(N_GROUPS)],
        out_specs=pl.BlockSpec((D_MODEL, tn), lambda j: (0, j)),
        out_shape=jax.ShapeDtypeStruct((D_MODEL, D_INP), bf16),
        compiler_params=_params(("parallel",)),
    )(hnt, *dxs)

    def tail(hnt_ref, dba_ref, full_ref, o_ref):
        del full_ref
        o_ref[...] = _dot(hnt_ref[...], _b(dba_ref[...])).astype(bf16)

    return pl.pallas_call(
        tail, name=name + "_ba", grid=(1,),
        in_specs=[pl.BlockSpec((D_MODEL, s), lambda k: (0, 0)), pl.BlockSpec((s, HD), lambda k: (0, 0)),
                  pl.BlockSpec(memory_space=pl.ANY)],
        out_specs=pl.BlockSpec((D_MODEL, HD), lambda k: (0, BLK_BA)),
        out_shape=jax.ShapeDtypeStruct((D_MODEL, D_INP), bf16),
        input_output_aliases={2: 0},
        compiler_params=_params(("arbitrary",)),
    )(hnt, dba, main)


def _rms_fwd(x, w, name):
    s = x.shape[0]
    tr = min(512, s)

    def body(x_ref, w_ref, h_ref):
        xv = x_ref[...]
        r = lax.rsqrt(_rowmean(xv * xv) + EPS)
        h_ref[...] = (xv * r * w_ref[...]).astype(bf16)

    return pl.pallas_call(
        body, name=name, grid=(s // tr,),
        in_specs=[pl.BlockSpec((tr, D_MODEL), lambda i: (i, 0)), pl.BlockSpec((1, D_MODEL), lambda i: (0, 0))],
        out_specs=pl.BlockSpec((tr, D_MODEL), lambda i: (i, 0)),
        out_shape=jax.ShapeDtypeStruct((s, D_MODEL), bf16), compiler_params=_params(("parallel",)),
    )(x, w)


def _rms_bwd(dh, x, w, dres, name):
    s = x.shape[0]
    tr = min(512, s)

    def body(dh_ref, x_ref, w_ref, dres_ref, dx_ref, dw_ref):
        xv = x_ref[...]
        r = lax.rsqrt(_rowmean(xv * xv) + EPS)
        xn = xv * r
        d = dh_ref[...]
        dxn = d * w_ref[...]
        dx_ref[...] = dres_ref[...] + r * (dxn - xn * _rowmean(dxn * xn))
        part = _sum0(d * xn)

        @pl.when(pl.program_id(0) == 0)
        def _():
            dw_ref[...] = part

        @pl.when(pl.program_id(0) > 0)
        def _():
            dw_ref[...] += part

    row = pl.BlockSpec((tr, D_MODEL), lambda i: (i, 0))
    vec = pl.BlockSpec((1, D_MODEL), lambda i: (0, 0))
    return pl.pallas_call(
        body, name=name, grid=(s // tr,), in_specs=[row, row, vec, row], out_specs=[row, vec],
        out_shape=[jax.ShapeDtypeStruct((s, D_MODEL), f32), jax.ShapeDtypeStruct((1, D_MODEL), f32)],
        compiler_params=_params(("arbitrary",)),
    )(dh, x, w, dres)


def _final_loss(x, w, target, name):
    s = x.shape[0]
    tr = min(512, s)

    def body(x_ref, w_ref, t_ref, dx_ref, dw_ref, loss_ref):
        xv = x_ref[...]
        wv = w_ref[...]
        r = lax.rsqrt(_rowmean(xv * xv) + EPS)
        xn = xv * r
        e = xn * wv - t_ref[...]
        lb = jnp.broadcast_to(0.5 * _sum0(_rowmean(e * e)), (1, HD))
        dy = e * (1.0 / D_MODEL)
        dxn = dy * wv
        dx_ref[...] = r * (dxn - xn * _rowmean(dxn * xn))
        part = _sum0(dy * xn)

        @pl.when(pl.program_id(0) == 0)
        def _():
            dw_ref[...] = part
            loss_ref[...] = lb

        @pl.when(pl.program_id(0) > 0)
        def _():
            dw_ref[...] += part
            loss_ref[...] += lb

    row = pl.BlockSpec((tr, D_MODEL), lambda i: (i, 0))
    vec = pl.BlockSpec((1, D_MODEL), lambda i: (0, 0))
    lsp = pl.BlockSpec((1, HD), lambda i: (0, 0))
    return pl.pallas_call(
        body, name=name, grid=(s // tr,), in_specs=[row, vec, row], out_specs=[row, vec, lsp],
        out_shape=[jax.ShapeDtypeStruct((s, D_MODEL), f32), jax.ShapeDtypeStruct((1, D_MODEL), f32),
                   jax.ShapeDtypeStruct((1, HD), f32)],
        compiler_params=_params(("arbitrary",)),
    )(x, w, target)


ALL = slice(None)


def _lru_gates(xc, vec, wa, wx):
    sp = _softplus(-vec[3:4])
    xcb = _b(xc)
    r = _sig(_dot(xcb, wa) + vec[1:2])
    i = _sig(_dot(xcb, wx) + vec[2:3])
    la = (-LRU_C) * r * sp
    a = jnp.exp(la)
    mult = jnp.sqrt(-_expm1(2.0 * la))
    return r, i, a, mult, sp, xcb


def _lru_fwd(proj, cw, vec, wa, wx, name):
    s = proj.shape[0]
    sb = min(SEQ_BLOCK, s)
    nsb = s // sb
    t = min(LRU_T, sb)
    nc = sb // t

    def body(x_ref, xh_ref, z_ref, cw_ref, vec_ref, wa_ref, wx_ref, y_ref, hl_ref, hc_ref):
        sblk = pl.program_id(1)
        cwv = cw_ref[...]
        vec_v = vec_ref[...]
        wa_v = _b(wa_ref[...])
        wx_v = _b(wx_ref[...])
        row = lax.broadcasted_iota(jnp.int32, (t, HD), 0)

        @pl.when(sblk == 0)
        def _():
            hc_ref[...] = jnp.zeros((8, HD), f32)

        def chunk(c, hcarry):
            t0 = pl.multiple_of(c * t, t)
            xs = _conv_taps(x_ref[pl.ds(t0, t), :], _rows_before(x_ref, xh_ref, ALL, t0, c, sblk))
            xc = vec_v[0:1] + _conv(xs, cwv)
            r, i, a, mult, _, _ = _lru_gates(xc, vec_v, wa_v, wx_v)
            bb = mult * (i * xc)
            d = 1
            while d < t:
                m = row >= d
                bb = jnp.where(m, a * pltpu.roll(bb, d, axis=0) + bb, bb)
                a = jnp.where(m, a * pltpu.roll(a, d, axis=0), a)
                d *= 2
            hl = bb + a * hcarry
            hl_ref[pl.ds(t0, t), :] = hl
            z = z_ref[pl.ds(t0, t), :]
            nrm = lax.rsqrt(_rowmean(hl * hl) + EPS)
            y_ref[pl.ds(t0, t), :] = (hl * nrm * vec_v[4:5] * (z * _sig(z))).astype(bf16)
            return hl[t - 1:t, :]

        hlast = lax.fori_loop(0, nc, chunk, hc_ref[0:1, :])
        hc_ref[...] = jnp.broadcast_to(hlast, (8, HD))

    xsp, xhalo = _seq_specs(sb, HD, lambda h, i: i, lambda h, i: h)
    zsp, _ = _seq_specs(sb, HD, lambda h, i: i, lambda h, i: HEADS + h)
    hcol = lambda h, i: (0, h)
    wsp = pl.BlockSpec((None, HD, HD), lambda h, i: (h, 0, 0))
    osp = pl.BlockSpec((sb, HD), lambda h, i: (i, h))
    return pl.pallas_call(
        body, name=name, grid=(HEADS, nsb),
        in_specs=[xsp, xhalo, zsp, pl.BlockSpec((4, HD), hcol), pl.BlockSpec((8, HD), hcol), wsp, wsp],
        out_specs=[osp, osp],
        out_shape=[jax.ShapeDtypeStruct((s, D_MIX), bf16), jax.ShapeDtypeStruct((s, D_MODEL), f32)],
        scratch_shapes=[pltpu.VMEM((8, HD), f32)],
        compiler_params=_params(("parallel", "arbitrary")),
    )(proj, proj, proj, cw, vec, wa, wx)


def _lru_bwd(dy, proj, hl, cw, vec, wa, wx, name):
    s = proj.shape[0]
    sb = min(SEQ_BLOCK, s)
    nsb = s // sb
    t = min(LRU_T, sb)
    nc = sb // t

    def body(dy_ref, x_ref, xh_ref, z_ref, hl_ref, hlh_ref, cw_ref, vec_ref, wa_ref, wx_ref,
             dx_ref, dz_ref, dcw_ref, dvec_ref, dwa_ref, dwx_ref, acc_ref, dxh_ref):
        step = pl.program_id(1)
        sblk = nsb - 1 - step
        cwv = cw_ref[...]
        vec_v = vec_ref[...]
        wa_v = _b(wa_ref[...])
        wx_v = _b(wx_ref[...])
        lnw = vec_v[4:5]
        row = lax.broadcasted_iota(jnp.int32, (t, HD), 0)

        @pl.when(step == 0)
        def _():
            acc_ref[...] = jnp.zeros((16, HD), f32)
            dxh_ref[...] = jnp.zeros((8, HD), f32)
            dwa_ref[...] = jnp.zeros((HD, HD), f32)
            dwx_ref[...] = jnp.zeros((HD, HD), f32)

        def chunk(ci, carry):
            mu, dxc_halo, dcw0, dcw1, dcw2, dcw3, dcb, dba, dbx, dsp, dlnw = carry
            c = nc - 1 - ci
            t0 = pl.multiple_of(c * t, t)
            xs = _conv_taps(x_ref[pl.ds(t0, t), :], _rows_before(x_ref, xh_ref, ALL, t0, c, sblk))
            xc = vec_v[0:1] + _conv(xs, cwv)
            r, i, a, mult, sp, xcb = _lru_gates(xc, vec_v, wa_v, wx_v)
            hv = hl_ref[pl.ds(t0, t), :]
            hhalo = _rows_before(hl_ref, hlh_ref, ALL, t0, c, sblk)
            hprev = pltpu.roll(jnp.concatenate([hhalo, hv], axis=0), 1, axis=0)[8:]
            z = z_ref[pl.ds(t0, t), :]
            sgz = _sig(z)
            sz = z * sgz
            dyv = dy_ref[pl.ds(t0, t), :]
            nrm = lax.rsqrt(_rowmean(hv * hv) + EPS)
            hn = hv * nrm
            dlnw = dlnw + _sum0(dyv * hn * sz)
            dz_ref[pl.ds(t0, t), :] = _b(dyv * hn * lnw * _dsilu(z, sgz))
            dhn = dyv * lnw * sz
            lam = nrm * (dhn - hn * _rowmean(dhn * hn))
            cf = jnp.where(row == t - 1, 1.0, pltpu.roll(a, t - 1, axis=0))
            d = 1
            while d < t:
                m = row < t - d
                lam = jnp.where(m, lam + cf * pltpu.roll(lam, t - d, axis=0), lam)
                cf = jnp.where(m, cf * pltpu.roll(cf, t - d, axis=0), cf)
                d *= 2
            lam = lam + cf * mu
            mu = a[0:1] * lam[0:1]
            da = lam * hprev
            dmult = lam * (i * xc)
            di = lam * mult * xc
            dxc = lam * mult * i
            dla = da * a - dmult * (a * a) / mult
            dr = dla * (-LRU_C * sp)
            dsp = dsp + _sum0(dla * (-LRU_C * r))
            dra = dr * r * (1.0 - r)
            dia = di * i * (1.0 - i)
            dba = dba + _sum0(dra)
            dbx = dbx + _sum0(dia)
            drab, diab = _b(dra), _b(dia)
            dwa_ref[...] += _dot_tn(xcb, drab)
            dwx_ref[...] += _dot_tn(xcb, diab)
            dxc = dxc + _dot_nt(drab, wa_v) + _dot_nt(diab, wx_v)
            dcb = dcb + _sum0(dxc)
            dcw0 = dcw0 + _sum0(dxc * xs[0])
            dcw1 = dcw1 + _sum0(dxc * xs[1])
            dcw2 = dcw2 + _sum0(dxc * xs[2])
            dcw3 = dcw3 + _sum0(dxc * xs[3])
            dx_ref[pl.ds(t0, t), :] = _b(_conv_back(dxc, dxc_halo, cwv))
            return (mu, dxc[0:8], dcw0, dcw1, dcw2, dcw3, dcb, dba, dbx, dsp, dlnw)

        acc = acc_ref[...]
        init = (acc[9:10], dxh_ref[...]) + tuple(acc[k:k + 1] for k in range(9))
        mu, dxh, dcw0, dcw1, dcw2, dcw3, dcb, dba, dbx, dsp, dlnw = lax.fori_loop(0, nc, chunk, init)
        zero = jnp.zeros((1, HD), f32)
        acc_ref[...] = jnp.concatenate([dcw0, dcw1, dcw2, dcw3, dcb, dba, dbx, dsp, dlnw, mu] + [zero] * 6, axis=0)
        dxh_ref[...] = dxh

        @pl.when(step == nsb - 1)
        def _():
            dcw_ref[...] = jnp.concatenate([dcw0, dcw1, dcw2, dcw3], axis=0)
            dlam = -dsp * _sig(-vec_v[3:4])
            dvec_ref[...] = jnp.concatenate([dcb, dba, dbx, dlam, dlnw, zero, zero, zero], axis=0)

    rblk = lambda h, i: nsb - 1 - i
    xsp, xhalo = _seq_specs(sb, HD, rblk, lambda h, i: h)
    zsp, _ = _seq_specs(sb, HD, rblk, lambda h, i: HEADS + h)
    hcol = lambda h, i: (0, h)
    wsp = pl.BlockSpec((None, HD, HD), lambda h, i: (h, 0, 0))
    return pl.pallas_call(
        body, name=name, grid=(HEADS, nsb),
        in_specs=[xsp, xsp, xhalo, zsp, xsp, xhalo, pl.BlockSpec((4, HD), hcol), pl.BlockSpec((8, HD), hcol), wsp, wsp],
        out_specs=[xsp, xsp, pl.BlockSpec((4, HD), hcol), pl.BlockSpec((8, HD), hcol), wsp, wsp],
        out_shape=[jax.ShapeDtypeStruct((s, D_MODEL), bf16), jax.ShapeDtypeStruct((s, D_MODEL), bf16),
                   jax.ShapeDtypeStruct((4, D_MODEL), f32), jax.ShapeDtypeStruct((8, D_MODEL), f32),
                   jax.ShapeDtypeStruct((HEADS, HD, HD), f32), jax.ShapeDtypeStruct((HEADS, HD, HD), f32)],
        scratch_shapes=[pltpu.VMEM((16, HD), f32), pltpu.VMEM((8, HD), f32)],
        compiler_params=_params(("parallel", "arbitrary")),
    )(dy, proj, proj, proj, hl, hl, cw, vec, wa, wx)


def _lane_pick(x, lane, idx):
    return _sum1(jnp.where(lane == idx, x, 0.0))


def _round_robin(gens):
    results = [None] * len(gens)
    live = list(range(len(gens)))
    while live:
        for i in list(live):
            try:
                next(gens[i])
            except StopIteration as done:
                results[i] = done.value
                live.remove(i)
    return results


def _sdot(a, b):
    return _dot(_b(a), _b(b))


def _sdot_tn(a, b):
    return _dot_tn(_b(a), _b(b))


def _sdot_nt(a, b):
    return _dot_nt(_b(a), _b(b))


def _inv_unit_lower(a):
    n = -a
    p = _sdot(a, a)
    yield
    for k in range(5):
        n = n + p + _sdot(n, p)
        if k < 4:
            p = _sdot(p, p)
        yield
    return n


def _dn_chunk(x3, halo3, bav, cw3, dnv, h, masks):
    lane, ri, ci, eye, ltri = masks
    xs = _conv_taps(x3, halo3)
    cpre = _conv(xs, cw3)
    sg = _sig(cpre)
    act = cpre * sg
    q_raw, k_raw, v = act[:, 0:HD], act[:, HD:2 * HD], act[:, 2 * HD:3 * HD]
    rq = lax.rsqrt(_sum1(q_raw * q_raw) + EPS)
    rk = lax.rsqrt(_sum1(k_raw * k_raw) + EPS)
    qn = q_raw * rq
    k = k_raw * rk
    q = qn * QSCALE
    b_in = jnp.broadcast_to(_lane_pick(bav, lane, h), (CHUNK, HD))
    a_in = jnp.broadcast_to(_lane_pick(bav, lane, HEADS + h), (CHUNK, HD))
    lane1 = lane[0:1]
    ea = jnp.exp(_lane_pick(dnv[0:1], lane1, h))
    dt = _lane_pick(dnv[1:2], lane1, h)
    beta = _sig(b_in)
    sp = _softplus(a_in + dt)
    g = -ea * sp
    gc = _dot(ltri, g, HI)
    yield
    gc64 = gc[:, 0:CHUNK]
    grow = _sum0(gc64 * eye)
    dm = jnp.exp(jnp.where(ri >= ci, gc64 - grow, -1e30))
    ds_ = jnp.where(ri > ci, dm, 0.0)
    eg = jnp.exp(gc)
    glast = gc[CHUNK - 1:CHUNK, :]
    ek = jnp.exp(glast - gc)
    kb = k * beta
    kbf = _b(k)
    amat = _dot_nt(_b(kb), kbf) * ds_
    pmat = _dot_nt(_b(q), kbf) * dm
    yield
    tinv = yield from _inv_unit_lower(amat)
    return dict(xs=xs, cpre=cpre, sg=sg, rq=rq, rk=rk, qn=qn, q=q, k=k, v=v, kbf=kbf, b_in=b_in, a_in=a_in, ea=ea,
                dt=dt, beta=beta, g=g, gc=gc, dm=dm, ds=ds_, eg=eg, glast=glast, ek=ek, kb=kb, amat=amat,
                tinv=tinv, pmat=pmat)


def _dn_masks():
    lane = lax.broadcasted_iota(jnp.int32, (CHUNK, HD), 1)
    ri = lax.broadcasted_iota(jnp.int32, (CHUNK, CHUNK), 0)
    ci = lax.broadcasted_iota(jnp.int32, (CHUNK, CHUNK), 1)
    eye = (ri == ci).astype(f32)
    ltri = (ri >= ci).astype(f32)
    return lane, ri, ci, eye, ltri


def _dn_load_qkv(q_ref, qh_ref, k_ref, kh_ref, v_ref, vh_ref, hl, rows, t0, c, sblk):
    x3 = jnp.concatenate([q_ref[rows, hl], k_ref[rows, hl], v_ref[rows, hl]], axis=1)
    halo3 = jnp.concatenate([_rows_before(q_ref, qh_ref, hl, t0, c, sblk), _rows_before(k_ref, kh_ref, hl, t0, c, sblk),
                             _rows_before(v_ref, vh_ref, hl, t0, c, sblk)], axis=1)
    return x3, halo3


def _dn_cw3(cwq_ref, cwk_ref, cwv_ref, j):
    return jnp.concatenate([r[:, j * HD:(j + 1) * HD] for r in (cwq_ref, cwk_ref, cwv_ref)], axis=1)


def _dn_fwd(proj, cw, dnv, y, name):
    s = proj.shape[0]
    sb = min(DN_SEQ_BLOCK, s)
    nsb = s // sb
    nc = sb // CHUNK
    hp = DN_HP

    def body(q_ref, qh_ref, k_ref, kh_ref, v_ref, vh_ref, z_ref, ba_ref, cwq_ref, cwk_ref, cwv_ref, dnv_ref, yin_ref,
             y_ref, o_ref, st_ref, s_ref):
        del yin_ref
        grp = pl.program_id(0)
        sblk = pl.program_id(1)
        masks = _dn_masks()
        dnv_v = dnv_ref[...]
        cw3 = [_dn_cw3(cwq_ref, cwk_ref, cwv_ref, j) for j in range(hp)]

        @pl.when(sblk == 0)
        def _():
            s_ref[...] = jnp.zeros((hp, HD, HD), f32)

        def one_head(j, x3, halo3, bav, z, st):
            f = yield from _dn_chunk(x3, halo3, bav, cw3[j], dnv_v, grp * hp + j, masks)
            sbf = _b(st)
            qs = _dot(_b(f["q"] * f["eg"]), sbf)
            rhs = f["beta"] * (f["v"] - _dot(_b(f["k"] * f["eg"]), sbf))
            yield
            vn = rhs + _sdot(f["tinv"], rhs)
            yield
            vnb = _b(vn)
            o = qs + _dot(_b(f["pmat"]), vnb)
            st_new = st * jnp.exp(f["glast"]) + _dot_tn(_b(f["k"] * f["ek"]), vnb)
            yield
            nrm = lax.rsqrt(_rowmean(o * o) + EPS)
            yv = (o * nrm * dnv_v[2:3] * (z * _sig(z))).astype(bf16)
            return o, yv, st_new

        def chunk(c, states):
            t0 = pl.multiple_of(c * CHUNK, CHUNK)
            rows = pl.ds(t0, CHUNK)
            bav = ba_ref[rows, :]
            ins = []
            for j in range(hp):
                hl = slice(j * HD, (j + 1) * HD)
                ins.append(_dn_load_qkv(q_ref, qh_ref, k_ref, kh_ref, v_ref, vh_ref, hl, rows, t0, c, sblk)
                           + (bav, z_ref[rows, hl]))
            outs = _round_robin([one_head(j, *ins[j], states[j]) for j in range(hp)])
            for j in range(hp):
                hl = slice(j * HD, (j + 1) * HD)
                st_ref[j, c] = states[j]
                o_ref[rows, hl] = outs[j][0]
                y_ref[rows, hl] = outs[j][1]
            return tuple(out[2] for out in outs)

        final = lax.fori_loop(0, nc, chunk, tuple(s_ref[j] for j in range(hp)))
        for j in range(hp):
            s_ref[j] = final[j]

    wide = hp * HD
    grp_specs = lambda off: _seq_specs(sb, wide, lambda g, i: i, lambda g, i: off // hp + g)
    (qsp, qhalo), (ksp, khalo), (vsp, vhalo) = grp_specs(2 * HEADS), grp_specs(3 * HEADS), grp_specs(4 * HEADS)
    zsp, _ = grp_specs(5 * HEADS)
    basp = pl.BlockSpec((sb, HD), lambda g, i: (i, BLK_BA))
    cws = lambda off: pl.BlockSpec((4, wide), lambda g, i: (0, off // hp + g))
    osp = lambda off: pl.BlockSpec((sb, wide), lambda g, i: (i, off // hp + g))
    return pl.pallas_call(
        body, name=name, grid=(HEADS // hp, nsb),
        in_specs=[qsp, qhalo, ksp, khalo, vsp, vhalo, zsp, basp, cws(0), cws(HEADS), cws(2 * HEADS),
                  pl.BlockSpec((8, HD), lambda g, i: (0, 0)), pl.BlockSpec(memory_space=pl.ANY)],
        out_specs=[osp(HEADS), osp(0), pl.BlockSpec((hp, nc, HD, HD), lambda g, i: (g, i, 0, 0))],
        out_shape=[jax.ShapeDtypeStruct((s, D_MIX), bf16), jax.ShapeDtypeStruct((s, D_MODEL), f32),
                   jax.ShapeDtypeStruct((HEADS, s // CHUNK, HD, HD), f32)],
        scratch_shapes=[pltpu.VMEM((hp, HD, HD), f32)],
        input_output_aliases={12: 0},
        compiler_params=_params(("parallel", "arbitrary")),
    )(proj, proj, proj, proj, proj, proj, proj, proj, cw, cw, cw, dnv, y)


def _dn_bwd(dy, proj, o, states, cw, dnv, name):
    s = proj.shape[0]
    sb = min(DN_SEQ_BLOCK, s)
    nsb = s // sb
    nc = sb // CHUNK
    hp = DN_HP
    ngrp = HEADS // hp

    def body(dy_ref, q_ref, qh_ref, k_ref, kh_ref, v_ref, vh_ref, z_ref, ba_ref, o_ref, st_ref,
             cwq_ref, cwk_ref, cwv_ref, dnv_ref,
             dq_ref, dk_ref, dv_ref, dz_ref, dba_ref, dcw_ref, dsc_ref,
             ds_ref, dch_ref, cwacc_ref, sacc_ref, nacc_ref):
        step = pl.program_id(0)
        grp = pl.program_id(1)
        sblk = nsb - 1 - step
        h0 = grp * hp
        masks = _dn_masks()
        lane, ri, ci, eye, ltri = masks
        utri = (ri <= ci).astype(f32)
        cw3s = [_dn_cw3(cwq_ref, cwk_ref, cwv_ref, j) for j in range(hp)]
        dnv_v = dnv_ref[...]
        dnw = dnv_v[2:3]
        row64 = lax.broadcasted_iota(jnp.int32, (CHUNK, HD), 0)

        @pl.when(step == 0)
        def _():
            for j in range(hp):
                ds_ref[h0 + j] = jnp.zeros((HD, HD), f32)
                dch_ref[h0 + j] = jnp.zeros((8, 3 * HD), f32)
                cwacc_ref[h0 + j] = jnp.zeros((8, 3 * HD), f32)
                sacc_ref[h0 + j] = jnp.zeros((8, HD), f32)

        @pl.when((step == 0) & (grp == 0))
        def _():
            nacc_ref[...] = jnp.zeros((8, HD), f32)

        def one_head(j, x3, halo3, bav, z, st, ov, dyv, carry):
            dst, dch, cwacc, sa0, sa1 = carry
            hd = h0 + j
            cw3 = cw3s[j]
            f = yield from _dn_chunk(x3, halo3, bav, cw3, dnv_v, hd, masks)
            q, k, v, beta, eg, ek, kbf = f["q"], f["k"], f["v"], f["beta"], f["eg"], f["ek"], f["kbf"]
            sbf = _b(st)
            dstb = _b(dst)
            qe = q * eg
            keg = k * eg
            kd = k * ek
            kegb, qeb, kdb = _b(keg), _b(qe), _b(kd)
            e = v - _dot(kegb, sbf)
            yield
            rhs = beta * e
            vn = rhs + _sdot(f["tinv"], rhs)
            yield
            vnb = _b(vn)
            pb = _b(f["pmat"])
            sgz = _sig(z)
            sz = z * sgz
            nrm = lax.rsqrt(_rowmean(ov * ov) + EPS)
            on = ov * nrm
            ddnw = _sum0(dyv * on * sz)
            dpz = dyv * on * dnw * _dsilu(z, sgz)
            don = dyv * dnw * sz
            do = nrm * (don - on * _rowmean(don * on))
            dob = _b(do)
            dvn = _dot_tn(pb, dob) + _dot(kdb, dstb)
            dpm = jnp.where(ri >= ci, _dot_nt(dob, vnb), 0.0)
            dqe = _dot_nt(dob, sbf)
            dkd = _dot_nt(vnb, dstb)
            qtdo = _dot_tn(qeb, dob)
            yield
            drhs = dvn + _sdot_tn(f["tinv"], dvn)
            dqk = _b(dpm * f["dm"])
            dq = _dot(dqk, kbf) + dqe * eg
            dk_p = _dot_tn(dqk, _b(q))
            yield
            dam = jnp.where(ri > ci, -_sdot_nt(drhs, vn), 0.0)
            de = drhs * beta
            deb = _b(de)
            dbeta = _sum1(drhs * e)
            dkeg = -_dot_nt(deb, sbf)
            dst_new = qtdo + dst * jnp.exp(f["glast"]) - _dot_tn(kegb, deb)
            yield
            dkk = _b(dam * f["ds"])
            dkb = _dot(dkk, kbf)
            dk = dk_p + _dot_tn(dkk, _b(f["kb"])) + dkb * beta + dkeg * eg + dkd * ek
            yield
            dbeta = dbeta + _sum1(dkb * k)
            mm = dpm * f["pmat"] + dam * f["amat"]
            dglast = _sum1(_sum0(dst * st)) * jnp.exp(f["glast"]) + _sum1(_sum0(dkd * kd))
            dgc = (_sum1(mm) - _sum1(eye * _sum0(mm)) + _sum1(dqe * qe) + _sum1(dkeg * keg) - _sum1(dkd * kd))
            dgc = jnp.broadcast_to(dgc, (CHUNK, HD)) + jnp.where(row64 == CHUNK - 1, dglast, 0.0)
            dg = _dot(utri, dgc, HI)
            yield
            dbin = jnp.broadcast_to(dbeta, (CHUNK, HD)) * beta * (1.0 - beta)
            dain = dg * (-f["ea"]) * _sig(f["a_in"] + f["dt"])
            dba = jnp.where(lane == hd, dbin, 0.0) + jnp.where(lane == HEADS + hd, dain, 0.0)
            sa0 = sa0 + _sum0(dg * f["g"])
            sa1 = sa1 + _sum0(dain)
            dq_raw = (QSCALE * f["rq"]) * (dq - f["qn"] * _sum1(f["qn"] * dq))
            dk_raw = f["rk"] * (dk - k * _sum1(k * dk))
            dact = jnp.concatenate([dq_raw, dk_raw, de], axis=1)
            dc = dact * _dsilu(f["cpre"], f["sg"])
            xs = f["xs"]
            cwacc = cwacc + jnp.concatenate([_sum0(dc * xs[0]), _sum0(dc * xs[1]), _sum0(dc * xs[2]),
                                             _sum0(dc * xs[3])], axis=0)
            dqkv = _conv_back(dc, dch, cw3)
            return dpz, dqkv, dba, ddnw, (dst_new, dc[0:8], cwacc, sa0, sa1)

        def chunk(cidx, carry):
            heads, nacc = carry
            c = nc - 1 - cidx
            t0 = pl.multiple_of(c * CHUNK, CHUNK)
            rows = pl.ds(t0, CHUNK)
            bav = ba_ref[rows, :]
            ins = []
            for j in range(hp):
                hl = slice(j * HD, (j + 1) * HD)
                ins.append(_dn_load_qkv(q_ref, qh_ref, k_ref, kh_ref, v_ref, vh_ref, hl, rows, t0, c, sblk)
                           + (bav, z_ref[rows, hl], st_ref[j, c], o_ref[rows, hl], dy_ref[rows, hl]))
            outs = _round_robin([one_head(j, *ins[j], heads[j]) for j in range(hp)])
            dba = outs[0][2]
            for j in range(hp):
                hl = slice(j * HD, (j + 1) * HD)
                dpz, dqkv, dba_j, ddnw, _ = outs[j]
                dq_ref[rows, hl] = _b(dqkv[:, 0:HD])
                dk_ref[rows, hl] = _b(dqkv[:, HD:2 * HD])
                dv_ref[rows, hl] = _b(dqkv[:, 2 * HD:3 * HD])
                dz_ref[rows, hl] = _b(dpz)
                nacc = nacc + ddnw
                if j > 0:
                    dba = dba + dba_j

            @pl.when(grp == 0)
            def _():
                dba_ref[rows, :] = dba

            @pl.when(grp > 0)
            def _():
                dba_ref[rows, :] += dba

            return tuple(out[4] for out in outs), nacc

        init = tuple((ds_ref[h0 + j], dch_ref[h0 + j], cwacc_ref[h0 + j][0:4], sacc_ref[h0 + j][0:1],
                      sacc_ref[h0 + j][1:2]) for j in range(hp))
        heads, nacc = lax.fori_loop(0, nc, chunk, (init, nacc_ref[0:1, :]))
        nacc_ref[0:1, :] = nacc
        zpad = jnp.zeros((4, 3 * HD), f32)
        zrow = jnp.zeros((6, HD), f32)
        for j in range(hp):
            dst, dch, cwacc, sa0, sa1 = heads[j]
            ds_ref[h0 + j] = dst
            dch_ref[h0 + j] = dch
            cwacc_ref[h0 + j] = jnp.concatenate([cwacc, zpad], axis=0)
            sacc_ref[h0 + j] = jnp.concatenate([sa0, sa1, zrow], axis=0)

        @pl.when(step == nsb - 1)
        def _():
            lane8 = lax.broadcasted_iota(jnp.int32, (8, HD), 1)
            row8 = lax.broadcasted_iota(jnp.int32, (8, HD), 0)
            mine = jnp.zeros((8, HD), f32)
            for j in range(hp):
                dcw_ref[h0 + j] = heads[j][2]
                sa = jnp.concatenate([heads[j][3], heads[j][4], zrow], axis=0)
                mine = mine + jnp.where((lane8 == h0 + j) & (row8 < 2), sa, 0.0)

            @pl.when(grp == 0)
            def _():
                dsc_ref[...] = mine

            @pl.when(grp > 0)
            def _():
                dsc_ref[...] += mine

            @pl.when(grp == ngrp - 1)
            def _():
                dsc_ref[2:3, :] = nacc

    wide = hp * HD
    rblk = lambda i, g: nsb - 1 - i
    grp_specs = lambda off: _seq_specs(sb, wide, rblk, lambda i, g: off // hp + g)
    (qsp, qhalo), (ksp, khalo), (vsp, vhalo) = grp_specs(2 * HEADS), grp_specs(3 * HEADS), grp_specs(4 * HEADS)
    zsp, _ = grp_specs(5 * HEADS)
    hsp = lambda off: pl.BlockSpec((sb, wide), lambda i, g: (rblk(i, g), off // hp + g))
    basp = lambda col: pl.BlockSpec((sb, HD), lambda i, g: (rblk(i, g), col))
    cws = lambda off: pl.BlockSpec((4, wide), lambda i, g: (0, off // hp + g))
    whole = lambda shape: pl.BlockSpec(shape, lambda i, g: (0,) * len(shape))
    return pl.pallas_call(
        body, name=name, grid=(nsb, ngrp),
        in_specs=[hsp(HEADS), qsp, qhalo, ksp, khalo, vsp, vhalo, zsp, basp(BLK_BA), hsp(0),
                  pl.BlockSpec((hp, nc, HD, HD), lambda i, g: (g, rblk(i, g), 0, 0)),
                  cws(0), cws(HEADS), cws(2 * HEADS), whole((8, HD))],
        out_specs=[hsp(0), hsp(0), hsp(0), hsp(0), basp(0), whole((HEADS, 4, 3 * HD)), whole((8, HD))],
        out_shape=[jax.ShapeDtypeStruct((s, D_MODEL), bf16)] * 4
        + [jax.ShapeDtypeStruct((s, HD), f32), jax.ShapeDtypeStruct((HEADS, 4, 3 * HD), f32),
           jax.ShapeDtypeStruct((8, HD), f32)],
        scratch_shapes=[pltpu.VMEM((HEADS, HD, HD), f32), pltpu.VMEM((HEADS, 8, 3 * HD), f32),
                        pltpu.VMEM((HEADS, 8, 3 * HD), f32), pltpu.VMEM((HEADS, 8, HD), f32), pltpu.VMEM((8, HD), f32)],
        compiler_params=_params(("arbitrary", "arbitrary")),
    )(dy, proj, proj, proj, proj, proj, proj, proj, proj, o, states, cw, cw, cw, dnv)


ANY = pl.BlockSpec(memory_space=pl.ANY)
CHIP_MASKS = ((1, 0, 0), (0, 1, 0), (1, 1, 0))
ALL_MASKS = tuple((m >> 2 & 1, m >> 1 & 1, m & 1) for m in range(1, N_DEV))


def _me():
    return lax.axis_index("x"), lax.axis_index("y"), lax.axis_index("c")


def _flip(me, mask):
    return tuple((1 - v) if m else v for v, m in zip(me, mask))


def _dev_index(dev):
    return 4 * dev[0] + 2 * dev[1] + dev[2]


def _chip(dev):
    return 2 * dev[0] + dev[1]


def _comm_call(name, body, arrays, out_shapes, nsem, nloc=0):
    scratch = [pltpu.SemaphoreType.DMA((nsem,)), pltpu.SemaphoreType.DMA((nsem,))]
    if nloc:
        scratch.append(pltpu.SemaphoreType.DMA((nloc,)))
    return pl.pallas_call(
        body, name=name, in_specs=[ANY] * len(arrays), out_specs=[ANY] * len(out_shapes), out_shape=out_shapes,
        scratch_shapes=scratch, compiler_params=pltpu.CompilerParams(has_side_effects=True),
    )(*arrays)


def _put_own(gathered, own, slot):
    return lax.dynamic_update_index_in_dim(gathered, own, slot, 0)


def _gather_halves(arrays, whole_arrays, name):
    na, nw = len(arrays), len(whole_arrays)
    nm = len(CHIP_MASKS)

    def body(*refs):
        srcs, dsts = refs[:na + nw], refs[na + nw:2 * (na + nw)]
        send_sems, recv_sems = refs[2 * (na + nw):]
        me = _me()
        sib = _flip(me, (0, 0, 1))

        def half(a, ref, core):
            rows = arrays[a].shape[0] // 2
            return ref.at[pl.ds(pl.multiple_of(core * rows, rows), rows)]

        def over_ici(a, mi, sender, to):
            if a < na:
                src, dst = half(a, srcs[a], sender[2]), half(a, dsts[a].at[_chip(sender)], sender[2])
            else:
                src, dst = srcs[a], dsts[a].at[_chip(sender)]
            return pltpu.make_async_remote_copy(src_ref=src, dst_ref=dst, send_sem=send_sems.at[a * nm + mi],
                                                recv_sem=recv_sems.at[a * nm + mi], device_id=to, device_id_type=MESH)

        def over_d2d(a, mi, origin, sender, to):
            k = (na + nw) * nm + a * nm + mi
            blk = half(a, dsts[a].at[_chip(origin)], sender[2])
            return pltpu.make_async_remote_copy(src_ref=blk, dst_ref=blk, send_sem=send_sems.at[k],
                                                recv_sem=recv_sems.at[k], device_id=to, device_id_type=MESH)

        sends = []
        for a in range(na + nw):
            for mi, mask in enumerate(CHIP_MASKS):
                cp = over_ici(a, mi, me, _flip(me, mask))
                cp.start()
                sends.append(cp)
        for a in range(na + nw):
            for mi, mask in enumerate(CHIP_MASKS):
                peer = _flip(me, mask)
                over_ici(a, mi, peer, me).wait_recv()
                if a < na:
                    cp = over_d2d(a, mi, peer, me, sib)
                    cp.start()
                    sends.append(cp)
        for a in range(na):
            for mi, mask in enumerate(CHIP_MASKS):
                over_d2d(a, mi, _flip(sib, mask), sib, me).wait_recv()
        for cp in sends:
            cp.wait_send()

    every = list(arrays) + list(whole_arrays)
    got = _comm_call(name, body, every, [jax.ShapeDtypeStruct((4,) + t.shape, t.dtype) for t in every],
                     (2 * na + nw) * nm)
    chip = 2 * lax.axis_index("x") + lax.axis_index("y")
    return [_put_own(g, t, chip) for g, t in zip(got, every)]


def _scatter_layer(win, wout, small, name):
    arrays = [win, wout] + ([small] if small is not None else [])
    na = len(arrays)
    half_in, half_out = D_MODEL // 2, D_MIX // 8

    def piece(a, ref, d):
        j, r = d >> 1, d & 1
        if a == 0:
            return ref.at[pl.ds(half_in * r, half_in), pl.ds(WIN_STRIDE * j, WIN_W)]
        if a == 1:
            return ref.at[pl.ds(2 * half_out * j + half_out * r, half_out)]
        return ref.at[d]

    def body(*refs):
        srcs, dsts = refs[:na], refs[na:2 * na]
        send_sems, recv_sems = refs[2 * na:]
        me = _me()
        my = _dev_index(me)
        for d in range(N_DEV):
            dev = (d >> 2, (d >> 1) & 1, d & 1)
            for a in range(na):
                @pl.when(my != d)
                def _(a=a, d=d, dev=dev):
                    pltpu.make_async_remote_copy(
                        src_ref=piece(a, srcs[a], d), dst_ref=dsts[a].at[my], send_sem=send_sems.at[a * N_DEV + d],
                        recv_sem=recv_sems.at[a * N_DEV + my], device_id=dev, device_id_type=MESH).start()
        for s in range(N_DEV):
            for a in range(na):
                @pl.when(my != s)
                def _(a=a, s=s):
                    pltpu.make_async_remote_copy(
                        src_ref=piece(a, srcs[a], s), dst_ref=dsts[a].at[s], send_sem=send_sems.at[a * N_DEV + s],
                        recv_sem=recv_sems.at[a * N_DEV + s], device_id=me, device_id_type=MESH).wait_recv()
        for d in range(N_DEV):
            for a in range(na):
                @pl.when(my != d)
                def _(a=a, d=d):
                    pltpu.make_async_remote_copy(
                        src_ref=piece(a, srcs[a], d), dst_ref=dsts[a].at[d], send_sem=send_sems.at[a * N_DEV + d],
                        recv_sem=recv_sems.at[a * N_DEV + d], device_id=me, device_id_type=MESH).wait_send()

    shapes = [jax.ShapeDtypeStruct((N_DEV, half_in, WIN_W), win.dtype),
              jax.ShapeDtypeStruct((N_DEV, half_out, D_MODEL), wout.dtype)]
    if small is not None:
        shapes.append(jax.ShapeDtypeStruct(small.shape, small.dtype))
    got = _comm_call(name, body, arrays, shapes, na * N_DEV)
    x, y, c = _me()
    chip, my = 2 * x + y, 4 * x + 2 * y + c
    own = [lax.dynamic_slice(win, (half_in * c, WIN_STRIDE * chip), (half_in, WIN_W)),
           lax.dynamic_slice(wout, (2 * half_out * chip + half_out * c, 0), (half_out, D_MODEL))]
    if small is not None:
        own.append(lax.dynamic_index_in_dim(small, my, 0, keepdims=False))
    return [_put_own(g, t, my) for g, t in zip(got, own)]


def _share_reduced(pair_arrays, small, name):
    arrays = list(pair_arrays) + ([small] if small is not None else [])
    na, npair = len(arrays), len(pair_arrays)
    nm = len(ALL_MASKS)

    def body(*refs):
        srcs, dsts = refs[:na], refs[na:2 * na]
        send_sems, recv_sems = refs[2 * na:]
        me = _me()
        sib = _flip(me, (0, 0, 1))

        def copy(a, k, sender, to):
            slot = sender[2] if a < npair else _dev_index(sender)
            return pltpu.make_async_remote_copy(
                src_ref=srcs[a], dst_ref=dsts[a].at[slot], send_sem=send_sems.at[k], recv_sem=recv_sems.at[k],
                device_id=to, device_id_type=MESH)

        sends = [copy(a, a, me, sib) for a in range(npair)]
        if small is not None:
            sends += [copy(npair, npair + mi, me, _flip(me, mask)) for mi, mask in enumerate(ALL_MASKS)]
        for cp in sends:
            cp.start()
        for a in range(npair):
            copy(a, a, sib, me).wait_recv()
        if small is not None:
            for mi, mask in enumerate(ALL_MASKS):
                copy(npair, npair + mi, _flip(me, mask), me).wait_recv()
        for cp in sends:
            cp.wait_send()

    shapes = [jax.ShapeDtypeStruct((2,) + t.shape, t.dtype) for t in pair_arrays]
    if small is not None:
        shapes.append(jax.ShapeDtypeStruct((N_DEV,) + small.shape, small.dtype))
    got = _comm_call(name, body, arrays, shapes, npair + (nm if small is not None else 0))
    x, y, c = _me()
    slots = [c] * npair + [4 * x + 2 * y + c]
    return [_put_own(g, t, slot) for g, t, slot in zip(got, arrays, slots)]


def _sum_parts(parts, name):
    _, r, c = parts.shape
    tr = r
    while tr * c * 4 * N_DEV > (8 << 20) and tr % 32 == 0:
        tr //= 2

    def body(p_ref, o_ref):
        total = p_ref[0].astype(f32)
        for d in range(1, N_DEV):
            total = total + p_ref[d].astype(f32)
        o_ref[...] = total

    return pl.pallas_call(
        body, name=name, grid=(r // tr,), in_specs=[pl.BlockSpec((N_DEV, tr, c), lambda i: (0, i, 0))],
        out_specs=pl.BlockSpec((tr, c), lambda i: (i, 0)), out_shape=jax.ShapeDtypeStruct((r, c), f32),
        compiler_params=_params(("parallel",)),
    )(parts)


def _adamw(w, g, m, v, name):
    shape = w.shape
    cols = shape[-1]
    rows = w.size // cols
    tr = rows
    while tr * cols * 4 > (1 << 20) and tr % 16 == 0:
        tr //= 2
    c1 = 1.0 / (1.0 - ADAM_B1 ** ADAM_STEP)
    c2 = 1.0 / (1.0 - ADAM_B2 ** ADAM_STEP)

    def body(w_ref, g_ref, m_ref, v_ref, d_ref, nm_ref, nv_ref):
        gv = g_ref[...]
        mv = ADAM_B1 * m_ref[...] + (1.0 - ADAM_B1) * gv
        vv = ADAM_B2 * v_ref[...] + (1.0 - ADAM_B2) * (gv * gv)
        nm_ref[...] = mv
        nv_ref[...] = vv
        d_ref[...] = -ADAM_LR * ((mv * c1) / (jnp.sqrt(vv * c2) + ADAM_EPS) + ADAM_WD * w_ref[...])

    spec = pl.BlockSpec((tr, cols), lambda i: (i, 0))
    outs = pl.pallas_call(
        body, name=name, grid=(rows // tr,), in_specs=[spec] * 4, out_specs=[spec] * 3,
        out_shape=[jax.ShapeDtypeStruct((rows, cols), f32)] * 3, compiler_params=_params(("parallel",)),
    )(*[t.reshape(rows, cols) for t in (w, g, m, v)])
    return tuple(t.reshape(shape) for t in outs)


def _pad_lanes(t, n=HD):
    return jnp.pad(t, [(0, 0)] * (t.ndim - 1) + [(0, n - t.shape[-1])])


def _rows128(t, rows=None):
    t = t.reshape(-1, HD)
    rows = rows or -(-t.shape[0] // 8) * 8
    return jnp.pad(t, ((0, rows - t.shape[0]), (0, 0)))


def kernel(x, norm_w, w_in, lru_conv_w, lru_conv_b, lru_wa, lru_ba, lru_wx, lru_bx, lru_lambda, lru_norm_w, dn_conv_w, dn_A_log, dn_dt_bias, dn_norm_w, w_out, final_norm_w, loss_target, m_norm_w, m_w_in, m_lru_conv_w, m_lru_conv_b, m_lru_wa, m_lru_ba, m_lru_wx, m_lru_bx, m_lru_lambda, m_lru_norm_w, m_dn_conv_w, m_dn_A_log, m_dn_dt_bias, m_dn_norm_w, m_w_out, m_final_norm_w, v_norm_w, v_w_in, v_lru_conv_w, v_lru_conv_b, v_lru_wa, v_lru_ba, v_lru_wx, v_lru_bx, v_lru_lambda, v_lru_norm_w, v_dn_conv_w, v_dn_A_log, v_dn_dt_bias, v_dn_norm_w, v_w_out, v_final_norm_w):
    depth = norm_w.shape[0]
    xs = x[0]
    target = loss_target[0]
    chip = 2 * lax.axis_index("x") + lax.axis_index("y")

    win_b, wout_b = w_in.astype(bf16), w_out.astype(bf16)
    wp, wo = [], []
    for l in range(depth):
        small_shards = [lru_conv_w, dn_conv_w] if l == 0 else []
        got = _gather_halves([win_b[l], wout_b[l]], small_shards, f"gather_weights{l}")
        wp.append(_pad_lanes(jnp.concatenate([got[0][j] for j in range(4)], axis=1), D_INP))
        wo.append(got[1].reshape(D_MIX, D_MODEL))
        if l == 0:
            lcw_full = got[2].transpose(1, 2, 0, 3).reshape(depth, 4, D_MODEL)
            dcw_full = got[3].transpose(1, 2, 0, 3).reshape(depth, 4, 3 * D_MODEL)
    vecs, dnvs = [], []
    zrow = jnp.zeros((D_MODEL,), f32)
    for l in range(depth):
        vecs.append(jnp.stack([lru_conv_b[l], lru_ba[l], lru_bx[l], lru_lambda[l], lru_norm_w[l], zrow, zrow, zrow]))
        dnvs.append(jnp.concatenate([_pad_lanes(dn_A_log[l][None]), _pad_lanes(dn_dt_bias[l][None]),
                                     dn_norm_w[l][None], jnp.zeros((5, HD), f32)], axis=0))

    loss_part, grad_x, d_fnw, g_nw, g_vec, g_wa, g_wx, g_lcw_, g_dcw_, g_dsc, g_win, g_wout = _local_step(
        xs, target, norm_w, final_norm_w, wp, wo, lcw_full, dcw_full, vecs, dnvs, lru_wa, lru_wx)
    loss = lax.psum(loss_part[0, 0], ("x", "y", "c"))
    grads = _reduce_grads(chip, d_fnw, g_nw, g_vec, g_wa, g_wx, g_lcw_, g_dcw_, g_dsc, g_win, g_wout)

    names = ["norm_w", "w_in", "lru_conv_w", "lru_conv_b", "lru_wa", "lru_ba", "lru_wx", "lru_bx", "lru_lambda",
             "lru_norm_w", "dn_conv_w", "dn_A_log", "dn_dt_bias", "dn_norm_w", "w_out", "final_norm_w"]
    ws = dict(zip(names, [norm_w, w_in, lru_conv_w, lru_conv_b, lru_wa, lru_ba, lru_wx, lru_bx, lru_lambda, lru_norm_w,
                          dn_conv_w, dn_A_log, dn_dt_bias, dn_norm_w, w_out, final_norm_w]))
    ms = dict(zip(names, [m_norm_w, m_w_in, m_lru_conv_w, m_lru_conv_b, m_lru_wa, m_lru_ba, m_lru_wx, m_lru_bx,
                          m_lru_lambda, m_lru_norm_w, m_dn_conv_w, m_dn_A_log, m_dn_dt_bias, m_dn_norm_w, m_w_out,
                          m_final_norm_w]))
    vs = dict(zip(names, [v_norm_w, v_w_in, v_lru_conv_w, v_lru_conv_b, v_lru_wa, v_lru_ba, v_lru_wx, v_lru_bx,
                          v_lru_lambda, v_lru_norm_w, v_dn_conv_w, v_dn_A_log, v_dn_dt_bias, v_dn_norm_w, v_w_out,
                          v_final_norm_w]))
    deltas, new_m, new_v = [], [], []
    for n in names:
        d, nm_, nv_ = _adamw(ws[n], grads[n], ms[n], vs[n], f"adamw_{n}")
        deltas.append(d)
        new_m.append(nm_)
        new_v.append(nv_)
    return (loss, grad_x[None], *[grads[n] for n in names], *deltas, *new_m, *new_v)


def _local_step(xs, target, norm_w, final_norm_w, wp, wo, lcw_full, dcw_full, vecs, dnvs, lru_wa, lru_wx):
    depth = norm_w.shape[0]
    saved = []
    h = xs
    for l in range(depth):
        hn = _rms_fwd(h, norm_w[l][None], f"rms_fwd{l}")
        proj = _matmul(hn, wp[l], tm=2048, tn=896, tk=D_MODEL, name=f"in_proj{l}")
        y, hl = _lru_fwd(proj, lcw_full[l], vecs[l], lru_wa[l], lru_wx[l], f"lru_fwd{l}")
        y, o, states = _dn_fwd(proj, dcw_full[l], dnvs[l], y, f"dn_fwd{l}")
        out = _matmul(y, wo[l], tm=512, tn=D_MODEL, tk=D_MIX, res=h, name=f"out_proj{l}")
        saved.append((h, hn, proj, hl, o, states, y))
        h = out

    dh, d_fnw, loss_part = _final_loss(h, final_norm_w[None], target, "final_loss")

    g_nw, g_vec, g_wa, g_wx, g_lcw_, g_dcw_, g_dsc, g_win, g_wout = ([None] * depth for _ in range(9))
    for l in reversed(range(depth)):
        hin, hn, proj, hl, o, states, y = saved[l]
        dy = _matmul(dh, wo[l], tb=True, tm=512, tn=1024, tk=D_MODEL, name=f"d_y{l}")
        g_wout[l] = _matmul(_transpose(y, f"y_t{l}"), dh, tm=1024, tn=512, tk=1024, out_dtype=bf16, name=f"d_wout{l}")
        dlx, dlz, g_lcw_[l], g_vec[l], g_wa[l], g_wx[l] = _lru_bwd(dy, proj, hl, lcw_full[l], vecs[l], lru_wa[l],
                                                                 lru_wx[l], f"lru_bwd{l}")
        dq, dk, dv, dz, dba, dcw8, g_dsc[l] = _dn_bwd(dy, proj, o, states, dcw_full[l], dnvs[l], f"dn_bwd{l}")
        g_dcw_[l] = dcw8.reshape(HEADS, 4, 3, HD).transpose(1, 2, 0, 3).reshape(4, 3 * D_MODEL)
        dxs = [dlx, dlz, dq, dk, dv, dz]
        dhn = _d_hn(dxs, dba, wp[l], f"d_hn{l}")
        g_win[l] = _d_win(_transpose(hn, f"hn_t{l}"), dxs, dba, f"d_win{l}")
        dh, g_nw[l] = _rms_bwd(dhn, hin, norm_w[l][None], dh, f"rms_bwd{l}")
    return loss_part, dh, d_fnw, g_nw, g_vec, g_wa, g_wx, g_lcw_, g_dcw_, g_dsc, g_win, g_wout


def _reduce_grads(chip, d_fnw, g_nw, g_vec, g_wa, g_wx, g_lcw_, g_dcw_, g_dsc, g_win, g_wout):
    depth = len(g_nw)
    both = lambda f: jnp.concatenate([f(l) for l in range(depth)])
    small = [
        both(lambda l: _rows128(g_nw[l])),
        both(lambda l: _rows128(g_vec[l][0])), both(lambda l: _rows128(g_vec[l][1])),
        both(lambda l: _rows128(g_vec[l][2])), both(lambda l: _rows128(g_vec[l][3])),
        both(lambda l: _rows128(g_vec[l][4])),
        both(lambda l: _rows128(g_wa[l])), both(lambda l: _rows128(g_wx[l])),
        both(lambda l: _rows128(g_dsc[l][0:1])), both(lambda l: _rows128(g_dsc[l][1:2])),
        both(lambda l: _rows128(g_dsc[l][2:3])),
        _rows128(d_fnw),
        both(lambda l: _rows128(g_lcw_[l])), both(lambda l: _rows128(g_dcw_[l])),
    ]
    counts = [t.shape[0] for t in small]
    total = sum(counts)
    per = -(-total // (8 * N_DEV)) * 8
    small = jnp.concatenate(small + [jnp.zeros((per * N_DEV - total, HD), f32)]).reshape(N_DEV, per, HD)

    pairs = [None] * depth
    a_small = None
    for l in reversed(range(depth)):
        got = _scatter_layer(g_win[l], g_wout[l], small if l == 0 else None, f"scatter_grads{l}")
        sums = [_sum_parts(got[0], f"sum_win{l}"), _sum_parts(got[1], f"sum_wout{l}")]
        if l == 0:
            shared = _share_reduced(sums, _sum_parts(got[2], "sum_small"), f"share_grads{l}")
            a_small = shared[2].reshape(N_DEV * per, HD)
        else:
            shared = _share_reduced(sums, None, f"share_grads{l}")
        pairs[l] = shared
    grad_w_in = jnp.stack([lax.dynamic_slice_in_dim(pairs[l][0].reshape(D_MODEL, WIN_W), 4 * chip, SHARD_IN, 1)
                           for l in range(depth)])
    grad_w_out = jnp.stack([pairs[l][1].reshape(D_MIX // 4, D_MODEL) for l in range(depth)])

    offs = [0]
    for c in counts:
        offs.append(offs[-1] + c)

    def piece(k, rows_per_layer=None):
        t = a_small[offs[k]:offs[k + 1]]
        if rows_per_layer is None:
            return t
        return t.reshape(depth, -1, HD)[:, :rows_per_layer]

    vec = lambda k: piece(k).reshape(depth, D_MODEL)
    return {
        "norm_w": vec(0), "lru_conv_b": vec(1), "lru_ba": vec(2), "lru_bx": vec(3), "lru_lambda": vec(4),
        "lru_norm_w": vec(5),
        "lru_wa": piece(6).reshape(depth, HEADS, HD, HD), "lru_wx": piece(7).reshape(depth, HEADS, HD, HD),
        "dn_A_log": piece(8, 1)[:, 0, :HEADS], "dn_dt_bias": piece(9, 1)[:, 0, :HEADS], "dn_norm_w": piece(10, 1)[:, 0],
        "final_norm_w": piece(11).reshape(D_MODEL),
        "lru_conv_w": lax.dynamic_slice_in_dim(piece(12).reshape(depth, 4, D_MODEL), chip * (D_MODEL // 4), D_MODEL // 4, 2),
        "dn_conv_w": lax.dynamic_slice_in_dim(piece(13).reshape(depth, 4, 3 * D_MODEL), chip * (3 * D_MODEL // 4),
                                              3 * D_MODEL // 4, 2),
        "w_in": grad_w_in, "w_out": grad_w_out,
    }
```

```python
import jax
import jax.numpy as jnp
from jax import lax
from jax.experimental import pallas as pl
from jax.experimental.pallas import tpu as pltpu

f32 = jnp.float32
bf16 = jnp.bfloat16
HI = lax.Precision.HIGHEST
MESH = pl.DeviceIdType.MESH

D_MODEL = 1024
HEADS = 8
HD = 128
CHUNK = 64
LRU_T = 128
SEQ_BLOCK = 1024
DN_SEQ_BLOCK = 512
DN_HP = 4
LRU_C = 8.0
D_IN = 6160
D_INP = 6272
D_MIX = 2048
N_GROUPS = 6
BLK_BA = 48
SHARD_IN = D_IN // 4
WIN_W = 1664
WIN_STRIDE = 1536
EPS = 1e-6
QSCALE = HD ** -0.5
N_DEV = 8
VMEM_LIMIT = 56 * 1024 * 1024

ADAM_LR, ADAM_B1, ADAM_B2, ADAM_EPS, ADAM_WD, ADAM_STEP = 0.001, 0.9, 0.999, 1e-08, 0.01, 10


def _sig(x):
    return 1.0 / (1.0 + jnp.exp(-x))


def _log1p(u):
    w = 1.0 + u
    d = w - 1.0
    return jnp.where(d == 0.0, u, jnp.log(w) * (u / jnp.where(d == 0.0, 1.0, d)))


def _softplus(x):
    return jnp.maximum(x, 0.0) + _log1p(jnp.exp(-jnp.abs(x)))


def _expm1(x):
    t = x * (1.0 + x * (0.5 + x * (1.0 / 6 + x * (1.0 / 24 + x * (1.0 / 120 + x * (1.0 / 720))))))
    return jnp.where(jnp.abs(x) < 0.2, t, jnp.exp(x) - 1.0)


def _dot(a, b, prec=None):
    return jnp.dot(a, b, precision=prec, preferred_element_type=f32)


def _dot_nt(a, b, prec=None):
    return lax.dot_general(a, b, (((1,), (1,)), ((), ())), precision=prec, preferred_element_type=f32)


def _dot_tn(a, b, prec=None):
    return lax.dot_general(a, b, (((0,), (0,)), ((), ())), precision=prec, preferred_element_type=f32)


def _b(x):
    return x.astype(bf16)


def _sum0(x):
    return jnp.sum(x, axis=0, keepdims=True)


def _sum1(x):
    return jnp.sum(x, axis=1, keepdims=True)


def _rowmean(x):
    return jnp.mean(x, axis=-1, keepdims=True)


def _dsilu(z, sg):
    return sg * (1.0 + z * (1.0 - sg))


def _conv_taps(x, halo):
    xx = jnp.concatenate([halo, x], axis=0)
    return [pltpu.roll(xx, 3, axis=0)[8:], pltpu.roll(xx, 2, axis=0)[8:], pltpu.roll(xx, 1, axis=0)[8:], x]


def _conv(xs, cw):
    return cw[0:1] * xs[0] + cw[1:2] * xs[1] + cw[2:3] * xs[2] + cw[3:4] * xs[3]


def _rows_before(ref, halo_ref, lanes, t0, c, sblk):
    tprev = pl.multiple_of(jnp.maximum(t0 - 8, 0), 8)
    return jnp.where(c > 0, ref[pl.ds(tprev, 8), lanes], jnp.where(sblk > 0, halo_ref[:, lanes], 0.0))


def _conv_back(dxc, halo_after, cw):
    rows = dxc.shape[0]
    dd = jnp.concatenate([dxc, halo_after], axis=0)
    out = cw[3:4] * dxc
    for s in (1, 2, 3):
        out = out + cw[3 - s:4 - s] * pltpu.roll(dd, rows + 8 - s, axis=0)[:rows]
    return out


def _params(sem=None):
    return pltpu.CompilerParams(dimension_semantics=sem, vmem_limit_bytes=VMEM_LIMIT)


def _seq_specs(sb, width, rowblk, colblk):
    main = pl.BlockSpec((sb, width), lambda a, b: (rowblk(a, b), colblk(a, b)))
    halo = pl.BlockSpec((8, width), lambda a, b: (jnp.maximum(rowblk(a, b) * (sb // 8) - 1, 0), colblk(a, b)))
    return main, halo


def _matmul(a, b, *, ta=False, tb=False, tm, tn, tk, res=None, out_dtype=f32, name):
    if ta:
        kdim, m = a.shape
    else:
        m, kdim = a.shape
    n = b.shape[0] if tb else b.shape[1]
    tm, tn, tk = min(tm, m), min(tn, n), min(tk, kdim)
    assert m % tm == 0 and n % tn == 0 and kdim % tk == 0, (name, m, n, kdim, tm, tn, tk)
    nk = kdim // tk
    dims = (((0 if ta else 1,), (1 if tb else 0,)), ((), ()))
    has_res = res is not None

    def body(*refs):
        a_ref, b_ref = refs[:2]
        r_ref = refs[2] if has_res else None
        o_ref = refs[3] if has_res else refs[2]
        acc_ref = refs[-1] if nk > 1 else None
        part = lax.dot_general(_b(a_ref[...]), _b(b_ref[...]), dims, preferred_element_type=f32)

        def finish(total):
            if has_res:
                total = total + r_ref[...]
            o_ref[...] = total.astype(out_dtype)

        if nk == 1:
            finish(part)
        else:
            k = pl.program_id(2)

            @pl.when(k == 0)
            def _():
                acc_ref[...] = part

            @pl.when(k > 0)
            def _():
                acc_ref[...] += part

            @pl.when(k == nk - 1)
            def _():
                finish(acc_ref[...])

    a_spec = pl.BlockSpec((tk, tm), lambda i, j, k: (k, i)) if ta else pl.BlockSpec((tm, tk), lambda i, j, k: (i, k))
    b_spec = pl.BlockSpec((tn, tk), lambda i, j, k: (j, k)) if tb else pl.BlockSpec((tk, tn), lambda i, j, k: (k, j))
    o_spec = pl.BlockSpec((tm, tn), lambda i, j, k: (i, j))
    in_specs, args = [a_spec, b_spec], [a, b]
    if has_res:
        in_specs.append(o_spec)
        args.append(res)
    return pl.pallas_call(
        body, name=name, grid=(m // tm, n // tn, nk), in_specs=in_specs, out_specs=o_spec,
        out_shape=jax.ShapeDtypeStruct((m, n), out_dtype),
        scratch_shapes=[pltpu.VMEM((tm, tn), f32)] if nk > 1 else [],
        compiler_params=_params(("parallel", "parallel", "arbitrary")),
    )(*args)


def _transpose(x, name):
    r, c = x.shape
    tr, tc = min(512, r), min(512, c)

    def body(x_ref, o_ref):
        o_ref[...] = x_ref[...].T

    return pl.pallas_call(
        body, name=name, grid=(r // tr, c // tc), in_specs=[pl.BlockSpec((tr, tc), lambda i, j: (i, j))],
        out_specs=pl.BlockSpec((tc, tr), lambda i, j: (j, i)), out_shape=jax.ShapeDtypeStruct((c, r), x.dtype),
        compiler_params=_params(("parallel", "parallel")),
    )(x)


def _d_hn(dxs, dba, wp, name):
    s = dxs[0].shape[0]
    tm = min(1024, s)

    def body(*refs):
        dx_refs = refs[:N_GROUPS]
        dba_ref, w_ref, wba_ref, o_ref, acc_ref = refs[N_GROUPS:]
        k = pl.program_id(1)
        for g in range(N_GROUPS):
            @pl.when(k == g)
            def _(g=g):
                part = _dot_nt(_b(dx_refs[g][...]), w_ref[...])
                if g == 0:
                    acc_ref[...] = part + _dot_nt(_b(dba_ref[...]), wba_ref[...])
                else:
                    acc_ref[...] += part

        @pl.when(k == N_GROUPS - 1)
        def _():
            o_ref[...] = acc_ref[...]

    row = lambda w: pl.BlockSpec((tm, w), lambda i, k: (i, 0))
    return pl.pallas_call(
        body, name=name, grid=(s // tm, N_GROUPS),
        in_specs=[row(D_MODEL)] * N_GROUPS + [row(HD), pl.BlockSpec((D_MODEL, D_MODEL), lambda i, k: (0, k)),
                                              pl.BlockSpec((D_MODEL, HD), lambda i, k: (0, BLK_BA))],
        out_specs=row(D_MODEL), out_shape=jax.ShapeDtypeStruct((s, D_MODEL), f32),
        scratch_shapes=[pltpu.VMEM((tm, D_MODEL), f32)],
        compiler_params=_params(("parallel", "arbitrary")),
    )(*dxs, dba, wp, wp)


def _d_win(hnt, dxs, dba, name):
    s = hnt.shape[1]
    tn = 256
    per = D_MODEL // tn

    def body(*refs):
        hnt_ref = refs[0]
        dx_refs = refs[1:1 + N_GROUPS]
        o_ref = refs[1 + N_GROUPS]
        j = pl.program_id(0)
        for g in range(N_GROUPS):
            @pl.when(j // per == g)
            def _(g=g):
                o_ref[...] = _dot(hnt_ref[...], _b(dx_refs[g][...])).astype(bf16)

    def dx_spec(g):
        on = lambda j: (j // per == g).astype(jnp.int32)
        return pl.BlockSpec((s, tn), lambda j: (0, (j % per) * on(j)))

    main = pl.pallas_call(
        body, name=name, grid=(N_GROUPS * per,),
        in_specs=[pl.BlockSpec((D_MODEL, s), lambda j: (0, 0))] + [dx_spec(g) for g in range(N_GROUPS)],
        out_specs=pl.BlockSpec((D_MODEL, tn), lambda j: (0, j)),
        out_shape=jax.ShapeDtypeStruct((D_MODEL, D_INP), bf16),
        compiler_params=_params(("parallel",)),
    )(hnt, *dxs)

    def tail(hnt_ref, dba_ref, full_ref, o_ref):
        del full_ref
        o_ref[...] = _dot(hnt_ref[...], _b(dba_ref[...])).astype(bf16)

    return pl.pallas_call(
        tail, name=name + "_ba", grid=(1,),
        in_specs=[pl.BlockSpec((D_MODEL, s), lambda k: (0, 0)), pl.BlockSpec((s, HD), lambda k: (0, 0)),
                  pl.BlockSpec(memory_space=pl.ANY)],
        out_specs=pl.BlockSpec((D_MODEL, HD), lambda k: (0, BLK_BA)),
        out_shape=jax.ShapeDtypeStruct((D_MODEL, D_INP), bf16),
        input_output_aliases={2: 0},
        compiler_params=_params(("arbitrary",)),
    )(hnt, dba, main)


def _rms_fwd(x, w, name):
    s = x.shape[0]
    tr = min(512, s)

    def body(x_ref, w_ref, h_ref):
        xv = x_ref[...]
        r = lax.rsqrt(_rowmean(xv * xv) + EPS)
        h_ref[...] = (xv * r * w_ref[...]).astype(bf16)

    return pl.pallas_call(
        body, name=name, grid=(s // tr,),
        in_specs=[pl.BlockSpec((tr, D_MODEL), lambda i: (i, 0)), pl.BlockSpec((1, D_MODEL), lambda i: (0, 0))],
        out_specs=pl.BlockSpec((tr, D_MODEL), lambda i: (i, 0)),
        out_shape=jax.ShapeDtypeStruct((s, D_MODEL), bf16), compiler_params=_params(("parallel",)),
    )(x, w)


def _rms_bwd(dh, x, w, dres, name):
    s = x.shape[0]
    tr = min(512, s)

    def body(dh_ref, x_ref, w_ref, dres_ref, dx_ref, dw_ref):
        xv = x_ref[...]
        r = lax.rsqrt(_rowmean(xv * xv) + EPS)
        xn = xv * r
        d = dh_ref[...]
        dxn = d * w_ref[...]
        dx_ref[...] = dres_ref[...] + r * (dxn - xn * _rowmean(dxn * xn))
        part = _sum0(d * xn)

        @pl.when(pl.program_id(0) == 0)
        def _():
            dw_ref[...] = part

        @pl.when(pl.program_id(0) > 0)
        def _():
            dw_ref[...] += part

    row = pl.BlockSpec((tr, D_MODEL), lambda i: (i, 0))
    vec = pl.BlockSpec((1, D_MODEL), lambda i: (0, 0))
    return pl.pallas_call(
        body, name=name, grid=(s // tr,), in_specs=[row, row, vec, row], out_specs=[row, vec],
        out_shape=[jax.ShapeDtypeStruct((s, D_MODEL), f32), jax.ShapeDtypeStruct((1, D_MODEL), f32)],
        compiler_params=_params(("arbitrary",)),
    )(dh, x, w, dres)


def _final_loss(x, w, target, name):
    s = x.shape[0]
    tr = min(512, s)

    def body(x_ref, w_ref, t_ref, dx_ref, dw_ref, loss_ref):
        xv = x_ref[...]
        wv = w_ref[...]
        r = lax.rsqrt(_rowmean(xv * xv) + EPS)
        xn = xv * r
        e = xn * wv - t_ref[...]
        lb = jnp.broadcast_to(0.5 * _sum0(_rowmean(e * e)), (1, HD))
        dy = e * (1.0 / D_MODEL)
        dxn = dy * wv
        dx_ref[...] = r * (dxn - xn * _rowmean(dxn * xn))
        part = _sum0(dy * xn)

        @pl.when(pl.program_id(0) == 0)
        def _():
            dw_ref[...] = part
            loss_ref[...] = lb

        @pl.when(pl.program_id(0) > 0)
        def _():
            dw_ref[...] += part
            loss_ref[...] += lb

    row = pl.BlockSpec((tr, D_MODEL), lambda i: (i, 0))
    vec = pl.BlockSpec((1, D_MODEL), lambda i: (0, 0))
    lsp = pl.BlockSpec((1, HD), lambda i: (0, 0))
    return pl.pallas_call(
        body, name=name, grid=(s // tr,), in_specs=[row, vec, row], out_specs=[row, vec, lsp],
        out_shape=[jax.ShapeDtypeStruct((s, D_MODEL), f32), jax.ShapeDtypeStruct((1, D_MODEL), f32),
                   jax.ShapeDtypeStruct((1, HD), f32)],
        compiler_params=_params(("arbitrary",)),
    )(x, w, target)


ALL = slice(None)


def _lru_gates(xc, vec, wa, wx):
    sp = _softplus(-vec[3:4])
    xcb = _b(xc)
    r = _sig(_dot(xcb, wa) + vec[1:2])
    i = _sig(_dot(xcb, wx) + vec[2:3])
    la = (-LRU_C) * r * sp
    a = jnp.exp(la)
    mult = jnp.sqrt(-_expm1(2.0 * la))
    return r, i, a, mult, sp, xcb


def _lru_fwd(proj, cw, vec, wa, wx, name):
    s = proj.shape[0]
    sb = min(SEQ_BLOCK, s)
    nsb = s // sb
    t = min(LRU_T, sb)
    nc = sb // t

    def body(x_ref, xh_ref, z_ref, cw_ref, vec_ref, wa_ref, wx_ref, y_ref, hl_ref, hc_ref):
        sblk = pl.program_id(1)
        cwv = cw_ref[...]
        vec_v = vec_ref[...]
        wa_v = _b(wa_ref[...])
        wx_v = _b(wx_ref[...])
        row = lax.broadcasted_iota(jnp.int32, (t, HD), 0)

        @pl.when(sblk == 0)
        def _():
            hc_ref[...] = jnp.zeros((8, HD), f32)

        def chunk(c, hcarry):
            t0 = pl.multiple_of(c * t, t)
            xs = _conv_taps(x_ref[pl.ds(t0, t), :], _rows_before(x_ref, xh_ref, ALL, t0, c, sblk))
            xc = vec_v[0:1] + _conv(xs, cwv)
            r, i, a, mult, _, _ = _lru_gates(xc, vec_v, wa_v, wx_v)
            bb = mult * (i * xc)
            d = 1
            while d < t:
                m = row >= d
                bb = jnp.where(m, a * pltpu.roll(bb, d, axis=0) + bb, bb)
                a = jnp.where(m, a * pltpu.roll(a, d, axis=0), a)
                d *= 2
            hl = bb + a * hcarry
            hl_ref[pl.ds(t0, t), :] = hl
            z = z_ref[pl.ds(t0, t), :]
            nrm = lax.rsqrt(_rowmean(hl * hl) + EPS)
            y_ref[pl.ds(t0, t), :] = (hl * nrm * vec_v[4:5] * (z * _sig(z))).astype(bf16)
            return hl[t - 1:t, :]

        hlast = lax.fori_loop(0, nc, chunk, hc_ref[0:1, :])
        hc_ref[...] = jnp.broadcast_to(hlast, (8, HD))

    xsp, xhalo = _seq_specs(sb, HD, lambda h, i: i, lambda h, i: h)
    zsp, _ = _seq_specs(sb, HD, lambda h, i: i, lambda h, i: HEADS + h)
    hcol = lambda h, i: (0, h)
    wsp = pl.BlockSpec((None, HD, HD), lambda h, i: (h, 0, 0))
    osp = pl.BlockSpec((sb, HD), lambda h, i: (i, h))
    return pl.pallas_call(
        body, name=name, grid=(HEADS, nsb),
        in_specs=[xsp, xhalo, zsp, pl.BlockSpec((4, HD), hcol), pl.BlockSpec((8, HD), hcol), wsp, wsp],
        out_specs=[osp, osp],
        out_shape=[jax.ShapeDtypeStruct((s, D_MIX), bf16), jax.ShapeDtypeStruct((s, D_MODEL), f32)],
        scratch_shapes=[pltpu.VMEM((8, HD), f32)],
        compiler_params=_params(("parallel", "arbitrary")),
    )(proj, proj, proj, cw, vec, wa, wx)


def _lru_bwd(dy, proj, hl, cw, vec, wa, wx, name):
    s = proj.shape[0]
    sb = min(SEQ_BLOCK, s)
    nsb = s // sb
    t = min(LRU_T, sb)
    nc = sb // t

    def body(dy_ref, x_ref, xh_ref, z_ref, hl_ref, hlh_ref, cw_ref, vec_ref, wa_ref, wx_ref,
             dx_ref, dz_ref, dcw_ref, dvec_ref, dwa_ref, dwx_ref, acc_ref, dxh_ref):
        step = pl.program_id(1)
        sblk = nsb - 1 - step
        cwv = cw_ref[...]
        vec_v = vec_ref[...]
        wa_v = _b(wa_ref[...])
        wx_v = _b(wx_ref[...])
        lnw = vec_v[4:5]
        row = lax.broadcasted_iota(jnp.int32, (t, HD), 0)

        @pl.when(step == 0)
        def _():
            acc_ref[...] = jnp.zeros((16, HD), f32)
            dxh_ref[...] = jnp.zeros((8, HD), f32)
            dwa_ref[...] = jnp.zeros((HD, HD), f32)
            dwx_ref[...] = jnp.zeros((HD, HD), f32)

        def chunk(ci, carry):
            mu, dxc_halo, dcw0, dcw1, dcw2, dcw3, dcb, dba, dbx, dsp, dlnw = carry
            c = nc - 1 - ci
            t0 = pl.multiple_of(c * t, t)
            xs = _conv_taps(x_ref[pl.ds(t0, t), :], _rows_before(x_ref, xh_ref, ALL, t0, c, sblk))
            xc = vec_v[0:1] + _conv(xs, cwv)
            r, i, a, mult, sp, xcb = _lru_gates(xc, vec_v, wa_v, wx_v)
            hv = hl_ref[pl.ds(t0, t), :]
            hhalo = _rows_before(hl_ref, hlh_ref, ALL, t0, c, sblk)
            hprev = pltpu.roll(jnp.concatenate([hhalo, hv], axis=0), 1, axis=0)[8:]
            z = z_ref[pl.ds(t0, t), :]
            sgz = _sig(z)
            sz = z * sgz
            dyv = dy_ref[pl.ds(t0, t), :]
            nrm = lax.rsqrt(_rowmean(hv * hv) + EPS)
            hn = hv * nrm
            dlnw = dlnw + _sum0(dyv * hn * sz)
            dz_ref[pl.ds(t0, t), :] = _b(dyv * hn * lnw * _dsilu(z, sgz))
            dhn = dyv * lnw * sz
            lam = nrm * (dhn - hn * _rowmean(dhn * hn))
            cf = jnp.where(row == t - 1, 1.0, pltpu.roll(a, t - 1, axis=0))
            d = 1
            while d < t:
                m = row < t - d
                lam = jnp.where(m, lam + cf * pltpu.roll(lam, t - d, axis=0), lam)
                cf = jnp.where(m, cf * pltpu.roll(cf, t - d, axis=0), cf)
                d *= 2
            lam = lam + cf * mu
            mu = a[0:1] * lam[0:1]
            da = lam * hprev
            dmult = lam * (i * xc)
            di = lam * mult * xc
            dxc = lam * mult * i
            dla = da * a - dmult * (a * a) / mult
            dr = dla * (-LRU_C * sp)
            dsp = dsp + _sum0(dla * (-LRU_C * r))
            dra = dr * r * (1.0 - r)
            dia = di * i * (1.0 - i)
            dba = dba + _sum0(dra)
            dbx = dbx + _sum0(dia)
            drab, diab = _b(dra), _b(dia)
            dwa_ref[...] += _dot_tn(xcb, drab)
            dwx_ref[...] += _dot_tn(xcb, diab)
            dxc = dxc + _dot_nt(drab, wa_v) + _dot_nt(diab, wx_v)
            dcb = dcb + _sum0(dxc)
            dcw0 = dcw0 + _sum0(dxc * xs[0])
            dcw1 = dcw1 + _sum0(dxc * xs[1])
            dcw2 = dcw2 + _sum0(dxc * xs[2])
            dcw3 = dcw3 + _sum0(dxc * xs[3])
            dx_ref[pl.ds(t0, t), :] = _b(_conv_back(dxc, dxc_halo, cwv))
            return (mu, dxc[0:8], dcw0, dcw1, dcw2, dcw3, dcb, dba, dbx, dsp, dlnw)

        acc = acc_ref[...]
        init = (acc[9:10], dxh_ref[...]) + tuple(acc[k:k + 1] for k in range(9))
        mu, dxh, dcw0, dcw1, dcw2, dcw3, dcb, dba, dbx, dsp, dlnw = lax.fori_loop(0, nc, chunk, init)
        zero = jnp.zeros((1, HD), f32)
        acc_ref[...] = jnp.concatenate([dcw0, dcw1, dcw2, dcw3, dcb, dba, dbx, dsp, dlnw, mu] + [zero] * 6, axis=0)
        dxh_ref[...] = dxh

        @pl.when(step == nsb - 1)
        def _():
            dcw_ref[...] = jnp.concatenate([dcw0, dcw1, dcw2, dcw3], axis=0)
            dlam = -dsp * _sig(-vec_v[3:4])
            dvec_ref[...] = jnp.concatenate([dcb, dba, dbx, dlam, dlnw, zero, zero, zero], axis=0)

    rblk = lambda h, i: nsb - 1 - i
    xsp, xhalo = _seq_specs(sb, HD, rblk, lambda h, i: h)
    zsp, _ = _seq_specs(sb, HD, rblk, lambda h, i: HEADS + h)
    hcol = lambda h, i: (0, h)
    wsp = pl.BlockSpec((None, HD, HD), lambda h, i: (h, 0, 0))
    return pl.pallas_call(
        body, name=name, grid=(HEADS, nsb),
        in_specs=[xsp, xsp, xhalo, zsp, xsp, xhalo, pl.BlockSpec((4, HD), hcol), pl.BlockSpec((8, HD), hcol), wsp, wsp],
        out_specs=[xsp, xsp, pl.BlockSpec((4, HD), hcol), pl.BlockSpec((8, HD), hcol), wsp, wsp],
        out_shape=[jax.ShapeDtypeStruct((s, D_MODEL), bf16), jax.ShapeDtypeStruct((s, D_MODEL), bf16),
                   jax.ShapeDtypeStruct((4, D_MODEL), f32), jax.ShapeDtypeStruct((8, D_MODEL), f32),
                   jax.ShapeDtypeStruct((HEADS, HD, HD), f32), jax.ShapeDtypeStruct((HEADS, HD, HD), f32)],
        scratch_shapes=[pltpu.VMEM((16, HD), f32), pltpu.VMEM((8, HD), f32)],
        compiler_params=_params(("parallel", "arbitrary")),
    )(dy, proj, proj, proj, hl, hl, cw, vec, wa, wx)


def _lane_pick(x, lane, idx):
    return _sum1(jnp.where(lane == idx, x, 0.0))


def _round_robin(gens):
    results = [None] * len(gens)
    live = list(range(len(gens)))
    while live:
        for i in list(live):
            try:
                next(gens[i])
            except StopIteration as done:
                results[i] = done.value
                live.remove(i)
    return results


def _sdot(a, b):
    return _dot(_b(a), _b(b))


def _sdot_tn(a, b):
    return _dot_tn(_b(a), _b(b))


def _sdot_nt(a, b):
    return _dot_nt(_b(a), _b(b))


def _inv_unit_lower(a):
    n = -a
    p = _sdot(a, a)
    yield
    for k in range(5):
        n = n + p + _sdot(n, p)
        if k < 4:
            p = _sdot(p, p)
        yield
    return n


def _dn_chunk(x3, halo3, bav, cw3, dnv, h, masks):
    lane, ri, ci, eye, ltri = masks
    xs = _conv_taps(x3, halo3)
    cpre = _conv(xs, cw3)
    sg = _sig(cpre)
    act = cpre * sg
    q_raw, k_raw, v = act[:, 0:HD], act[:, HD:2 * HD], act[:, 2 * HD:3 * HD]
    rq = lax.rsqrt(_sum1(q_raw * q_raw) + EPS)
    rk = lax.rsqrt(_sum1(k_raw * k_raw) + EPS)
    qn = q_raw * rq
    k = k_raw * rk
    q = qn * QSCALE
    b_in = jnp.broadcast_to(_lane_pick(bav, lane, h), (CHUNK, HD))
    a_in = jnp.broadcast_to(_lane_pick(bav, lane, HEADS + h), (CHUNK, HD))
    lane1 = lane[0:1]
    ea = jnp.exp(_lane_pick(dnv[0:1], lane1, h))
    dt = _lane_pick(dnv[1:2], lane1, h)
    beta = _sig(b_in)
    sp = _softplus(a_in + dt)
    g = -ea * sp
    gc = _dot(ltri, g, HI)
    yield
    gc64 = gc[:, 0:CHUNK]
    grow = _sum0(gc64 * eye)
    dm = jnp.exp(jnp.where(ri >= ci, gc64 - grow, -1e30))
    ds_ = jnp.where(ri > ci, dm, 0.0)
    eg = jnp.exp(gc)
    glast = gc[CHUNK - 1:CHUNK, :]
    ek = jnp.exp(glast - gc)
    kb = k * beta
    kbf = _b(k)
    amat = _dot_nt(_b(kb), kbf) * ds_
    pmat = _dot_nt(_b(q), kbf) * dm
    yield
    tinv = yield from _inv_unit_lower(amat)
    return dict(xs=xs, cpre=cpre, sg=sg, rq=rq, rk=rk, qn=qn, q=q, k=k, v=v, kbf=kbf, b_in=b_in, a_in=a_in, ea=ea,
                dt=dt, beta=beta, g=g, gc=gc, dm=dm, ds=ds_, eg=eg, glast=glast, ek=ek, kb=kb, amat=amat,
                tinv=tinv, pmat=pmat)


def _dn_masks():
    lane = lax.broadcasted_iota(jnp.int32, (CHUNK, HD), 1)
    ri = lax.broadcasted_iota(jnp.int32, (CHUNK, CHUNK), 0)
    ci = lax.broadcasted_iota(jnp.int32, (CHUNK, CHUNK), 1)
    eye = (ri == ci).astype(f32)
    ltri = (ri >= ci).astype(f32)
    return lane, ri, ci, eye, ltri


def _dn_load_qkv(q_ref, qh_ref, k_ref, kh_ref, v_ref, vh_ref, hl, rows, t0, c, sblk):
    x3 = jnp.concatenate([q_ref[rows, hl], k_ref[rows, hl], v_ref[rows, hl]], axis=1)
    halo3 = jnp.concatenate([_rows_before(q_ref, qh_ref, hl, t0, c, sblk), _rows_before(k_ref, kh_ref, hl, t0, c, sblk),
                             _rows_before(v_ref, vh_ref, hl, t0, c, sblk)], axis=1)
    return x3, halo3


def _dn_cw3(cwq_ref, cwk_ref, cwv_ref, j):
    return jnp.concatenate([r[:, j * HD:(j + 1) * HD] for r in (cwq_ref, cwk_ref, cwv_ref)], axis=1)


def _dn_fwd(proj, cw, dnv, y, name):
    s = proj.shape[0]
    sb = min(DN_SEQ_BLOCK, s)
    nsb = s // sb
    nc = sb // CHUNK
    hp = DN_HP

    def body(q_ref, qh_ref, k_ref, kh_ref, v_ref, vh_ref, z_ref, ba_ref, cwq_ref, cwk_ref, cwv_ref, dnv_ref, yin_ref,
             y_ref, o_ref, st_ref, s_ref):
        del yin_ref
        grp = pl.program_id(0)
        sblk = pl.program_id(1)
        masks = _dn_masks()
        dnv_v = dnv_ref[...]
        cw3 = [_dn_cw3(cwq_ref, cwk_ref, cwv_ref, j) for j in range(hp)]

        @pl.when(sblk == 0)
        def _():
            s_ref[...] = jnp.zeros((hp, HD, HD), f32)

        def one_head(j, x3, halo3, bav, z, st):
            f = yield from _dn_chunk(x3, halo3, bav, cw3[j], dnv_v, grp * hp + j, masks)
            sbf = _b(st)
            qs = _dot(_b(f["q"] * f["eg"]), sbf)
            rhs = f["beta"] * (f["v"] - _dot(_b(f["k"] * f["eg"]), sbf))
            yield
            vn = rhs + _sdot(f["tinv"], rhs)
            yield
            vnb = _b(vn)
            o = qs + _dot(_b(f["pmat"]), vnb)
            st_new = st * jnp.exp(f["glast"]) + _dot_tn(_b(f["k"] * f["ek"]), vnb)
            yield
            nrm = lax.rsqrt(_rowmean(o * o) + EPS)
            yv = (o * nrm * dnv_v[2:3] * (z * _sig(z))).astype(bf16)
            return o, yv, st_new

        def chunk(c, states):
            t0 = pl.multiple_of(c * CHUNK, CHUNK)
            rows = pl.ds(t0, CHUNK)
            bav = ba_ref[rows, :]
            ins = []
            for j in range(hp):
                hl = slice(j * HD, (j + 1) * HD)
                ins.append(_dn_load_qkv(q_ref, qh_ref, k_ref, kh_ref, v_ref, vh_ref, hl, rows, t0, c, sblk)
                           + (bav, z_ref[rows, hl]))
            outs = _round_robin([one_head(j, *ins[j], states[j]) for j in range(hp)])
            for j in range(hp):
                hl = slice(j * HD, (j + 1) * HD)
                st_ref[j, c] = states[j]
                o_ref[rows, hl] = outs[j][0]
                y_ref[rows, hl] = outs[j][1]
            return tuple(out[2] for out in outs)

        final = lax.fori_loop(0, nc, chunk, tuple(s_ref[j] for j in range(hp)))
        for j in range(hp):
            s_ref[j] = final[j]

    wide = hp * HD
    grp_specs = lambda off: _seq_specs(sb, wide, lambda g, i: i, lambda g, i: off // hp + g)
    (qsp, qhalo), (ksp, khalo), (vsp, vhalo) = grp_specs(2 * HEADS), grp_specs(3 * HEADS), grp_specs(4 * HEADS)
    zsp, _ = grp_specs(5 * HEADS)
    basp = pl.BlockSpec((sb, HD), lambda g, i: (i, BLK_BA))
    cws = lambda off: pl.BlockSpec((4, wide), lambda g, i: (0, off // hp + g))
    osp = lambda off: pl.BlockSpec((sb, wide), lambda g, i: (i, off // hp + g))
    return pl.pallas_call(
        body, name=name, grid=(HEADS // hp, nsb),
        in_specs=[qsp, qhalo, ksp, khalo, vsp, vhalo, zsp, basp, cws(0), cws(HEADS), cws(2 * HEADS),
                  pl.BlockSpec((8, HD), lambda g, i: (0, 0)), pl.BlockSpec(memory_space=pl.ANY)],
        out_specs=[osp(HEADS), osp(0), pl.BlockSpec((hp, nc, HD, HD), lambda g, i: (g, i, 0, 0))],
        out_shape=[jax.ShapeDtypeStruct((s, D_MIX), bf16), jax.ShapeDtypeStruct((s, D_MODEL), f32),
                   jax.ShapeDtypeStruct((HEADS, s // CHUNK, HD, HD), f32)],
        scratch_shapes=[pltpu.VMEM((hp, HD, HD), f32)],
        input_output_aliases={12: 0},
        compiler_params=_params(("parallel", "arbitrary")),
    )(proj, proj, proj, proj, proj, proj, proj, proj, cw, cw, cw, dnv, y)


def _dn_bwd(dy, proj, o, states, cw, dnv, name):
    s = proj.shape[0]
    sb = min(DN_SEQ_BLOCK, s)
    nsb = s // sb
    nc = sb // CHUNK
    hp = DN_HP
    ngrp = HEADS // hp

    def body(dy_ref, q_ref, qh_ref, k_ref, kh_ref, v_ref, vh_ref, z_ref, ba_ref, o_ref, st_ref,
             cwq_ref, cwk_ref, cwv_ref, dnv_ref,
             dq_ref, dk_ref, dv_ref, dz_ref, dba_ref, dcw_ref, dsc_ref,
             ds_ref, dch_ref, cwacc_ref, sacc_ref, nacc_ref):
        step = pl.program_id(0)
        grp = pl.program_id(1)
        sblk = nsb - 1 - step
        h0 = grp * hp
        masks = _dn_masks()
        lane, ri, ci, eye, ltri = masks
        utri = (ri <= ci).astype(f32)
        cw3s = [_dn_cw3(cwq_ref, cwk_ref, cwv_ref, j) for j in range(hp)]
        dnv_v = dnv_ref[...]
        dnw = dnv_v[2:3]
        row64 = lax.broadcasted_iota(jnp.int32, (CHUNK, HD), 0)

        @pl.when(step == 0)
        def _():
            for j in range(hp):
                ds_ref[h0 + j] = jnp.zeros((HD, HD), f32)
                dch_ref[h0 + j] = jnp.zeros((8, 3 * HD), f32)
                cwacc_ref[h0 + j] = jnp.zeros((8, 3 * HD), f32)
                sacc_ref[h0 + j] = jnp.zeros((8, HD), f32)

        @pl.when((step == 0) & (grp == 0))
        def _():
            nacc_ref[...] = jnp.zeros((8, HD), f32)

        def one_head(j, x3, halo3, bav, z, st, ov, dyv, carry):
            dst, dch, cwacc, sa0, sa1 = carry
            hd = h0 + j
            cw3 = cw3s[j]
            f = yield from _dn_chunk(x3, halo3, bav, cw3, dnv_v, hd, masks)
            q, k, v, beta, eg, ek, kbf = f["q"], f["k"], f["v"], f["beta"], f["eg"], f["ek"], f["kbf"]
            sbf = _b(st)
            dstb = _b(dst)
            qe = q * eg
            keg = k * eg
            kd = k * ek
            kegb, qeb, kdb = _b(keg), _b(qe), _b(kd)
            e = v - _dot(kegb, sbf)
            yield
            rhs = beta * e
            vn = rhs + _sdot(f["tinv"], rhs)
            yield
            vnb = _b(vn)
            pb = _b(f["pmat"])
            sgz = _sig(z)
            sz = z * sgz
            nrm = lax.rsqrt(_rowmean(ov * ov) + EPS)
            on = ov * nrm
            ddnw = _sum0(dyv * on * sz)
            dpz = dyv * on * dnw * _dsilu(z, sgz)
            don = dyv * dnw * sz
            do = nrm * (don - on * _rowmean(don * on))
            dob = _b(do)
            dvn = _dot_tn(pb, dob) + _dot(kdb, dstb)
            dpm = jnp.where(ri >= ci, _dot_nt(dob, vnb), 0.0)
            dqe = _dot_nt(dob, sbf)
            dkd = _dot_nt(vnb, dstb)
            qtdo = _dot_tn(qeb, dob)
            yield
            drhs = dvn + _sdot_tn(f["tinv"], dvn)
            dqk = _b(dpm * f["dm"])
            dq = _dot(dqk, kbf) + dqe * eg
            dk_p = _dot_tn(dqk, _b(q))
            yield
            dam = jnp.where(ri > ci, -_sdot_nt(drhs, vn), 0.0)
            de = drhs * beta
            deb = _b(de)
            dbeta = _sum1(drhs * e)
            dkeg = -_dot_nt(deb, sbf)
            dst_new = qtdo + dst * jnp.exp(f["glast"]) - _dot_tn(kegb, deb)
            yield
            dkk = _b(dam * f["ds"])
            dkb = _dot(dkk, kbf)
            dk = dk_p + _dot_tn(dkk, _b(f["kb"])) + dkb * beta + dkeg * eg + dkd * ek
            yield
            dbeta = dbeta + _sum1(dkb * k)
            mm = dpm * f["pmat"] + dam * f["amat"]
            dglast = _sum1(_sum0(dst * st)) * jnp.exp(f["glast"]) + _sum1(_sum0(dkd * kd))
            dgc = (_sum1(mm) - _sum1(eye * _sum0(mm)) + _sum1(dqe * qe) + _sum1(dkeg * keg) - _sum1(dkd * kd))
            dgc = jnp.broadcast_to(dgc, (CHUNK, HD)) + jnp.where(row64 == CHUNK - 1, dglast, 0.0)
            dg = _dot(utri, dgc, HI)
            yield
            dbin = jnp.broadcast_to(dbeta, (CHUNK, HD)) * beta * (1.0 - beta)
            dain = dg * (-f["ea"]) * _sig(f["a_in"] + f["dt"])
            dba = jnp.where(lane == hd, dbin, 0.0) + jnp.where(lane == HEADS + hd, dain, 0.0)
            sa0 = sa0 + _sum0(dg * f["g"])
            sa1 = sa1 + _sum0(dain)
            dq_raw = (QSCALE * f["rq"]) * (dq - f["qn"] * _sum1(f["qn"] * dq))
            dk_raw = f["rk"] * (dk - k * _sum1(k * dk))
            dact = jnp.concatenate([dq_raw, dk_raw, de], axis=1)
            dc = dact * _dsilu(f["cpre"], f["sg"])
            xs = f["xs"]
            cwacc = cwacc + jnp.concatenate([_sum0(dc * xs[0]), _sum0(dc * xs[1]), _sum0(dc * xs[2]),
                                             _sum0(dc * xs[3])], axis=0)
            dqkv = _conv_back(dc, dch, cw3)
            return dpz, dqkv, dba, ddnw, (dst_new, dc[0:8], cwacc, sa0, sa1)

        def chunk(cidx, carry):
            heads, nacc = carry
            c = nc - 1 - cidx
            t0 = pl.multiple_of(c * CHUNK, CHUNK)
            rows = pl.ds(t0, CHUNK)
            bav = ba_ref[rows, :]
            ins = []
            for j in range(hp):
                hl = slice(j * HD, (j + 1) * HD)
                ins.append(_dn_load_qkv(q_ref, qh_ref, k_ref, kh_ref, v_ref, vh_ref, hl, rows, t0, c, sblk)
                           + (bav, z_ref[rows, hl], st_ref[j, c], o_ref[rows, hl], dy_ref[rows, hl]))
            outs = _round_robin([one_head(j, *ins[j], heads[j]) for j in range(hp)])
            dba = outs[0][2]
            for j in range(hp):
                hl = slice(j * HD, (j + 1) * HD)
                dpz, dqkv, dba_j, ddnw, _ = outs[j]
                dq_ref[rows, hl] = _b(dqkv[:, 0:HD])
                dk_ref[rows, hl] = _b(dqkv[:, HD:2 * HD])
                dv_ref[rows, hl] = _b(dqkv[:, 2 * HD:3 * HD])
                dz_ref[rows, hl] = _b(dpz)
                nacc = nacc + ddnw
                if j > 0:
                    dba = dba + dba_j

            @pl.when(grp == 0)
            def _():
                dba_ref[rows, :] = dba

            @pl.when(grp > 0)
            def _():
                dba_ref[rows, :] += dba

            return tuple(out[4] for out in outs), nacc

        init = tuple((ds_ref[h0 + j], dch_ref[h0 + j], cwacc_ref[h0 + j][0:4], sacc_ref[h0 + j][0:1],
                      sacc_ref[h0 + j][1:2]) for j in range(hp))
        heads, nacc = lax.fori_loop(0, nc, chunk, (init, nacc_ref[0:1, :]))
        nacc_ref[0:1, :] = nacc
        zpad = jnp.zeros((4, 3 * HD), f32)
        zrow = jnp.zeros((6, HD), f32)
        for j in range(hp):
            dst, dch, cwacc, sa0, sa1 = heads[j]
            ds_ref[h0 + j] = dst
            dch_ref[h0 + j] = dch
            cwacc_ref[h0 + j] = jnp.concatenate([cwacc, zpad], axis=0)
            sacc_ref[h0 + j] = jnp.concatenate([sa0, sa1, zrow], axis=0)

        @pl.when(step == nsb - 1)
        def _():
            lane8 = lax.broadcasted_iota(jnp.int32, (8, HD), 1)
            row8 = lax.broadcasted_iota(jnp.int32, (8, HD), 0)
            mine = jnp.zeros((8, HD), f32)
            for j in range(hp):
                dcw_ref[h0 + j] = heads[j][2]
                sa = jnp.concatenate([heads[j][3], heads[j][4], zrow], axis=0)
                mine = mine + jnp.where((lane8 == h0 + j) & (row8 < 2), sa, 0.0)

            @pl.when(grp == 0)
            def _():
                dsc_ref[...] = mine

            @pl.when(grp > 0)
            def _():
                dsc_ref[...] += mine

            @pl.when(grp == ngrp - 1)
            def _():
                dsc_ref[2:3, :] = nacc

    wide = hp * HD
    rblk = lambda i, g: nsb - 1 - i
    grp_specs = lambda off: _seq_specs(sb, wide, rblk, lambda i, g: off // hp + g)
    (qsp, qhalo), (ksp, khalo), (vsp, vhalo) = grp_specs(2 * HEADS), grp_specs(3 * HEADS), grp_specs(4 * HEADS)
    zsp, _ = grp_specs(5 * HEADS)
    hsp = lambda off: pl.BlockSpec((sb, wide), lambda i, g: (rblk(i, g), off // hp + g))
    basp = lambda col: pl.BlockSpec((sb, HD), lambda i, g: (rblk(i, g), col))
    cws = lambda off: pl.BlockSpec((4, wide), lambda i, g: (0, off // hp + g))
    whole = lambda shape: pl.BlockSpec(shape, lambda i, g: (0,) * len(shape))
    return pl.pallas_call(
        body, name=name, grid=(nsb, ngrp),
        in_specs=[hsp(HEADS), qsp, qhalo, ksp, khalo, vsp, vhalo, zsp, basp(BLK_BA), hsp(0),
                  pl.BlockSpec((hp, nc, HD, HD), lambda i, g: (g, rblk(i, g), 0, 0)),
                  cws(0), cws(HEADS), cws(2 * HEADS), whole((8, HD))],
        out_specs=[hsp(0), hsp(0), hsp(0), hsp(0), basp(0), whole((HEADS, 4, 3 * HD)), whole((8, HD))],
        out_shape=[jax.ShapeDtypeStruct((s, D_MODEL), bf16)] * 4
        + [jax.ShapeDtypeStruct((s, HD), f32), jax.ShapeDtypeStruct((HEADS, 4, 3 * HD), f32),
           jax.ShapeDtypeStruct((8, HD), f32)],
        scratch_shapes=[pltpu.VMEM((HEADS, HD, HD), f32), pltpu.VMEM((HEADS, 8, 3 * HD), f32),
                        pltpu.VMEM((HEADS, 8, 3 * HD), f32), pltpu.VMEM((HEADS, 8, HD), f32), pltpu.VMEM((8, HD), f32)],
        compiler_params=_params(("arbitrary", "arbitrary")),
    )(dy, proj, proj, proj, proj, proj, proj, proj, proj, o, states, cw, cw, cw, dnv)


ANY = pl.BlockSpec(memory_space=pl.ANY)
CHIP_MASKS = ((1, 0, 0), (0, 1, 0), (1, 1, 0))
ALL_MASKS = tuple((m >> 2 & 1, m >> 1 & 1, m & 1) for m in range(1, N_DEV))


def _me():
    return lax.axis_index("x"), lax.axis_index("y"), lax.axis_index("c")


def _flip(me, mask):
    return tuple((1 - v) if m else v for v, m in zip(me, mask))


def _dev_index(dev):
    return 4 * dev[0] + 2 * dev[1] + dev[2]


def _chip(dev):
    return 2 * dev[0] + dev[1]


def _comm_call(name, body, arrays, out_shapes, nsem, nloc=0):
    scratch = [pltpu.SemaphoreType.DMA((nsem,)), pltpu.SemaphoreType.DMA((nsem,))]
    if nloc:
        scratch.append(pltpu.SemaphoreType.DMA((nloc,)))
    return pl.pallas_call(
        body, name=name, in_specs=[ANY] * len(arrays), out_specs=[ANY] * len(out_shapes), out_shape=out_shapes,
        scratch_shapes=scratch, compiler_params=pltpu.CompilerParams(has_side_effects=True),
    )(*arrays)


def _put_own(gathered, own, slot):
    return lax.dynamic_update_index_in_dim(gathered, own, slot, 0)


def _gather_halves(arrays, whole_arrays, name):
    na, nw = len(arrays), len(whole_arrays)
    nm = len(CHIP_MASKS)

    def body(*refs):
        srcs, dsts = refs[:na + nw], refs[na + nw:2 * (na + nw)]
        send_sems, recv_sems = refs[2 * (na + nw):]
        me = _me()
        sib = _flip(me, (0, 0, 1))

        def half(a, ref, core):
            rows = arrays[a].shape[0] // 2
            return ref.at[pl.ds(pl.multiple_of(core * rows, rows), rows)]

        def over_ici(a, mi, sender, to):
            if a < na:
                src, dst = half(a, srcs[a], sender[2]), half(a, dsts[a].at[_chip(sender)], sender[2])
            else:
                src, dst = srcs[a], dsts[a].at[_chip(sender)]
            return pltpu.make_async_remote_copy(src_ref=src, dst_ref=dst, send_sem=send_sems.at[a * nm + mi],
                                                recv_sem=recv_sems.at[a * nm + mi], device_id=to, device_id_type=MESH)

        def over_d2d(a, mi, origin, sender, to):
            k = (na + nw) * nm + a * nm + mi
            blk = half(a, dsts[a].at[_chip(origin)], sender[2])
            return pltpu.make_async_remote_copy(src_ref=blk, dst_ref=blk, send_sem=send_sems.at[k],
                                                recv_sem=recv_sems.at[k], device_id=to, device_id_type=MESH)

        sends = []
        for a in range(na + nw):
            for mi, mask in enumerate(CHIP_MASKS):
                cp = over_ici(a, mi, me, _flip(me, mask))
                cp.start()
                sends.append(cp)
        for a in range(na + nw):
            for mi, mask in enumerate(CHIP_MASKS):
                peer = _flip(me, mask)
                over_ici(a, mi, peer, me).wait_recv()
                if a < na:
                    cp = over_d2d(a, mi, peer, me, sib)
                    cp.start()
                    sends.append(cp)
        for a in range(na):
            for mi, mask in enumerate(CHIP_MASKS):
                over_d2d(a, mi, _flip(sib, mask), sib, me).wait_recv()
        for cp in sends:
            cp.wait_send()

    every = list(arrays) + list(whole_arrays)
    got = _comm_call(name, body, every, [jax.ShapeDtypeStruct((4,) + t.shape, t.dtype) for t in every],
                     (2 * na + nw) * nm)
    chip = 2 * lax.axis_index("x") + lax.axis_index("y")
    return [_put_own(g, t, chip) for g, t in zip(got, every)]


HBM = pl.BlockSpec(memory_space=pltpu.HBM)
SEM = pl.BlockSpec(memory_space=pltpu.SEMAPHORE)
DATAFLOW = pltpu.SideEffectType.DATAFLOW_SIDE_EFFECTING


def _split_start(name, srcs, land_shapes, nsem, issue):
    ns, nl = len(srcs), len(land_shapes)

    def body(*refs):
        issue(refs[:ns], refs[ns:ns + nl], refs[2 * (ns + nl)], refs[2 * (ns + nl) + 1])
        token = refs[-1]
        token[...] = jnp.zeros_like(token)

    lands = [lax.empty(t.shape, t.dtype) for t in land_shapes]
    thru = [pltpu.HBM(t.shape, t.dtype) for t in list(srcs) + list(land_shapes)]
    hbm = lambda t: pltpu.with_memory_space_constraint(t, pltpu.HBM)
    return pl.pallas_call(
        body, name=name, in_specs=[HBM] * (ns + nl),
        out_shape=(*thru, pltpu.SemaphoreType.DMA((nsem,)), pltpu.SemaphoreType.DMA((nsem,)),
                   jax.ShapeDtypeStruct((8, HD), f32)),
        out_specs=(*[HBM] * (ns + nl), SEM, SEM, pl.BlockSpec(memory_space=pltpu.VMEM)),
        input_output_aliases={i: i for i in range(ns + nl)},
        compiler_params=pltpu.CompilerParams(has_side_effects=DATAFLOW),
    )(*[hbm(t) for t in srcs], *[hbm(t) for t in lands])


def _split_wait(name, started, ns, after, finish):
    thru, send_sems, recv_sems = started[:-3], started[-3], started[-2]
    nt = len(thru)

    def body(*refs):
        finish(refs[:ns], refs[ns:nt], refs[nt], refs[nt + 1])

    outs = pl.pallas_call(
        body, name=name, in_specs=[HBM] * nt + [SEM, SEM, ANY],
        out_shape=tuple(pltpu.HBM(t.shape, t.dtype) for t in thru), out_specs=tuple([HBM] * nt),
        input_output_aliases={i: i for i in range(nt)},
        compiler_params=pltpu.CompilerParams(has_side_effects=DATAFLOW),
    )(*thru, send_sems, recv_sems, after)
    return list(outs[ns:])


def _gather_start(arrays, name):
    nm = len(CHIP_MASKS)

    def issue(srcs, lands, send_sems, recv_sems):
        me = _me()
        for a in range(len(arrays)):
            for mi, mask in enumerate(CHIP_MASKS):
                pltpu.make_async_remote_copy(
                    src_ref=srcs[a], dst_ref=lands[a].at[_chip(me)], send_sem=send_sems.at[a * nm + mi],
                    recv_sem=recv_sems.at[a * nm + mi], device_id=_flip(me, mask), device_id_type=MESH).start()

    shapes = [jax.ShapeDtypeStruct((4,) + t.shape, t.dtype) for t in arrays]
    return _split_start(name, arrays, shapes, len(arrays) * nm, issue)


def _gather_finish(started, arrays, after, name):
    nm = len(CHIP_MASKS)

    def finish(srcs, lands, send_sems, recv_sems):
        me = _me()
        for a in range(len(arrays)):
            for mi, mask in enumerate(CHIP_MASKS):
                peer = _flip(me, mask)
                cp = pltpu.make_async_remote_copy(
                    src_ref=srcs[a], dst_ref=lands[a].at[_chip(peer)], send_sem=send_sems.at[a * nm + mi],
                    recv_sem=recv_sems.at[a * nm + mi], device_id=peer, device_id_type=MESH)
                cp.wait_send()
                cp.wait_recv()

    got = _split_wait(name, started, len(arrays), after, finish)
    chip = 2 * lax.axis_index("x") + lax.axis_index("y")
    return [_put_own(g, t, chip) for g, t in zip(got, arrays)]


HALF_IN, HALF_OUT = D_MODEL // 2, D_MIX // 8


def _scatter_piece(kind, ref, d):
    j, r = d >> 1, d & 1
    if kind == "win":
        return ref.at[pl.ds(HALF_IN * r, HALF_IN), pl.ds(WIN_STRIDE * j, WIN_W)]
    if kind == "wout":
        return ref.at[pl.ds(2 * HALF_OUT * j + HALF_OUT * r, HALF_OUT)]
    return ref.at[d]


def _scatter_start(arrays, kinds, name):
    na = len(arrays)

    def issue(srcs, lands, send_sems, recv_sems):
        me = _me()
        my = _dev_index(me)
        for d in range(N_DEV):
            dev = (d >> 2, (d >> 1) & 1, d & 1)
            for a in range(na):
                @pl.when(my != d)
                def _(a=a, d=d, dev=dev):
                    pltpu.make_async_remote_copy(
                        src_ref=_scatter_piece(kinds[a], srcs[a], d), dst_ref=lands[a].at[my],
                        send_sem=send_sems.at[a * N_DEV + d], recv_sem=recv_sems.at[a * N_DEV + my],
                        device_id=dev, device_id_type=MESH).start()

    shape_of = {"win": (N_DEV, HALF_IN, WIN_W), "wout": (N_DEV, HALF_OUT, D_MODEL)}
    shapes = [jax.ShapeDtypeStruct(shape_of.get(k, t.shape), t.dtype) for t, k in zip(arrays, kinds)]
    return _split_start(name, arrays, shapes, na * N_DEV, issue)


def _scatter_finish(started, arrays, kinds, after, name):
    na = len(arrays)

    def finish(srcs, lands, send_sems, recv_sems):
        me = _me()
        my = _dev_index(me)
        for s in range(N_DEV):
            for a in range(na):
                @pl.when(my != s)
                def _(a=a, s=s):
                    cp = pltpu.make_async_remote_copy(
                        src_ref=_scatter_piece(kinds[a], srcs[a], s), dst_ref=lands[a].at[s],
                        send_sem=send_sems.at[a * N_DEV + s], recv_sem=recv_sems.at[a * N_DEV + s],
                        device_id=me, device_id_type=MESH)
                    cp.wait_recv()
                    cp.wait_send()

    got = _split_wait(name, started, na, after, finish)
    x, y, c = _me()
    chip, my = 2 * x + y, 4 * x + 2 * y + c
    own = {"win": lambda t: lax.dynamic_slice(t, (HALF_IN * c, WIN_STRIDE * chip), (HALF_IN, WIN_W)),
           "wout": lambda t: lax.dynamic_slice(t, (2 * HALF_OUT * chip + HALF_OUT * c, 0), (HALF_OUT, D_MODEL)),
           "small": lambda t: lax.dynamic_index_in_dim(t, my, 0, keepdims=False)}
    return [_put_own(g, own[k](t), my) for g, t, k in zip(got, arrays, kinds)]


def _share_reduced(pair_arrays, small, name):
    arrays = list(pair_arrays) + ([small] if small is not None else [])
    na, npair = len(arrays), len(pair_arrays)
    nm = len(ALL_MASKS)

    def body(*refs):
        srcs, dsts = refs[:na], refs[na:2 * na]
        send_sems, recv_sems = refs[2 * na:]
        me = _me()
        sib = _flip(me, (0, 0, 1))

        def copy(a, k, sender, to):
            slot = sender[2] if a < npair else _dev_index(sender)
            return pltpu.make_async_remote_copy(
                src_ref=srcs[a], dst_ref=dsts[a].at[slot], send_sem=send_sems.at[k], recv_sem=recv_sems.at[k],
                device_id=to, device_id_type=MESH)

        sends = [copy(a, a, me, sib) for a in range(npair)]
        if small is not None:
            sends += [copy(npair, npair + mi, me, _flip(me, mask)) for mi, mask in enumerate(ALL_MASKS)]
        for cp in sends:
            cp.start()
        for a in range(npair):
            copy(a, a, sib, me).wait_recv()
        if small is not None:
            for mi, mask in enumerate(ALL_MASKS):
                copy(npair, npair + mi, _flip(me, mask), me).wait_recv()
        for cp in sends:
            cp.wait_send()

    shapes = [jax.ShapeDtypeStruct((2,) + t.shape, t.dtype) for t in pair_arrays]
    if small is not None:
        shapes.append(jax.ShapeDtypeStruct((N_DEV,) + small.shape, small.dtype))
    got = _comm_call(name, body, arrays, shapes, npair + (nm if small is not None else 0))
    x, y, c = _me()
    slots = [c] * npair + [4 * x + 2 * y + c]
    return [_put_own(g, t, slot) for g, t, slot in zip(got, arrays, slots)]


def _sum_parts(parts, name):
    _, r, c = parts.shape
    tr = r
    while tr * c * 4 * N_DEV > (8 << 20) and tr % 32 == 0:
        tr //= 2

    def body(p_ref, o_ref):
        total = p_ref[0].astype(f32)
        for d in range(1, N_DEV):
            total = total + p_ref[d].astype(f32)
        o_ref[...] = total

    return pl.pallas_call(
        body, name=name, grid=(r // tr,), in_specs=[pl.BlockSpec((N_DEV, tr, c), lambda i: (0, i, 0))],
        out_specs=pl.BlockSpec((tr, c), lambda i: (i, 0)), out_shape=jax.ShapeDtypeStruct((r, c), f32),
        compiler_params=_params(("parallel",)),
    )(parts)


def _adamw(w, g, m, v, name):
    shape = w.shape
    cols = shape[-1]
    rows = w.size // cols
    tr = rows
    while tr * cols * 4 > (1 << 20) and tr % 16 == 0:
        tr //= 2
    c1 = 1.0 / (1.0 - ADAM_B1 ** ADAM_STEP)
    c2 = 1.0 / (1.0 - ADAM_B2 ** ADAM_STEP)

    def body(w_ref, g_ref, m_ref, v_ref, d_ref, nm_ref, nv_ref):
        gv = g_ref[...]
        mv = ADAM_B1 * m_ref[...] + (1.0 - ADAM_B1) * gv
        vv = ADAM_B2 * v_ref[...] + (1.0 - ADAM_B2) * (gv * gv)
        nm_ref[...] = mv
        nv_ref[...] = vv
        d_ref[...] = -ADAM_LR * ((mv * c1) / (jnp.sqrt(vv * c2) + ADAM_EPS) + ADAM_WD * w_ref[...])

    spec = pl.BlockSpec((tr, cols), lambda i: (i, 0))
    outs = pl.pallas_call(
        body, name=name, grid=(rows // tr,), in_specs=[spec] * 4, out_specs=[spec] * 3,
        out_shape=[jax.ShapeDtypeStruct((rows, cols), f32)] * 3, compiler_params=_params(("parallel",)),
    )(*[t.reshape(rows, cols) for t in (w, g, m, v)])
    return tuple(t.reshape(shape) for t in outs)


def _pad_lanes(t, n=HD):
    return jnp.pad(t, [(0, 0)] * (t.ndim - 1) + [(0, n - t.shape[-1])])


def _rows128(t, rows=None):
    t = t.reshape(-1, HD)
    rows = rows or -(-t.shape[0] // 8) * 8
    return jnp.pad(t, ((0, rows - t.shape[0]), (0, 0)))


def kernel(x, norm_w, w_in, lru_conv_w, lru_conv_b, lru_wa, lru_ba, lru_wx, lru_bx, lru_lambda, lru_norm_w, dn_conv_w, dn_A_log, dn_dt_bias, dn_norm_w, w_out, final_norm_w, loss_target, m_norm_w, m_w_in, m_lru_conv_w, m_lru_conv_b, m_lru_wa, m_lru_ba, m_lru_wx, m_lru_bx, m_lru_lambda, m_lru_norm_w, m_dn_conv_w, m_dn_A_log, m_dn_dt_bias, m_dn_norm_w, m_w_out, m_final_norm_w, v_norm_w, v_w_in, v_lru_conv_w, v_lru_conv_b, v_lru_wa, v_lru_ba, v_lru_wx, v_lru_bx, v_lru_lambda, v_lru_norm_w, v_dn_conv_w, v_dn_A_log, v_dn_dt_bias, v_dn_norm_w, v_w_out, v_final_norm_w):
    depth = norm_w.shape[0]
    xs = x[0]
    target = loss_target[0]
    chip = 2 * lax.axis_index("x") + lax.axis_index("y")

    win_b, wout_b = w_in.astype(bf16), w_out.astype(bf16)
    got0 = _gather_halves([win_b[0], wout_b[0]], [lru_conv_w, dn_conv_w], "gather_weights0")
    lcw_full = got0[2].transpose(1, 2, 0, 3).reshape(depth, 4, D_MODEL)
    dcw_full = got0[3].transpose(1, 2, 0, 3).reshape(depth, 4, 3 * D_MODEL)
    later = {l: _gather_start([win_b[l], wout_b[l]], f"gather_weights{l}_start") for l in range(1, depth)}

    def weights_of(l, after):
        if l == 0:
            got, token = got0, later[1][-1] if depth > 1 else None
        else:
            got, token = _gather_finish(later[l], [win_b[l], wout_b[l]], after, f"gather_weights{l}_wait"), None
        wp_l = _pad_lanes(jnp.concatenate([got[0][j] for j in range(4)], axis=1), D_INP)
        return wp_l, got[1].reshape(D_MIX, D_MODEL), token

    in_flight = {}

    def on_grad(kind, l, g):
        if l == depth - 1 and depth > 1:
            if kind == "wout":
                in_flight["held"] = g
                return None
            srcs, kinds, key = [g, in_flight.pop("held")], ["win", "wout"], ("both", l)
        else:
            srcs, kinds, key = [g], [kind], (kind, l)
        started = _scatter_start(srcs, kinds, f"scatter_{key[0]}{key[1]}_start")
        in_flight[key] = (started, srcs, kinds)
        return started[-1]

    vecs, dnvs = [], []
    zrow = jnp.zeros((D_MODEL,), f32)
    for l in range(depth):
        vecs.append(jnp.stack([lru_conv_b[l], lru_ba[l], lru_bx[l], lru_lambda[l], lru_norm_w[l], zrow, zrow, zrow]))
        dnvs.append(jnp.concatenate([_pad_lanes(dn_A_log[l][None]), _pad_lanes(dn_dt_bias[l][None]),
                                     dn_norm_w[l][None], jnp.zeros((5, HD), f32)], axis=0))

    loss_part, grad_x, d_fnw, g_nw, g_vec, g_wa, g_wx, g_lcw_, g_dcw_, g_dsc = _local_step(
        xs, target, norm_w, final_norm_w, weights_of, on_grad, lcw_full, dcw_full, vecs, dnvs, lru_wa, lru_wx)
    loss = lax.psum(loss_part[0, 0], ("x", "y", "c"))
    grads = _reduce_grads(chip, grad_x, in_flight, d_fnw, g_nw, g_vec, g_wa, g_wx, g_lcw_, g_dcw_, g_dsc)

    names = ["norm_w", "w_in", "lru_conv_w", "lru_conv_b", "lru_wa", "lru_ba", "lru_wx", "lru_bx", "lru_lambda",
             "lru_norm_w", "dn_conv_w", "dn_A_log", "dn_dt_bias", "dn_norm_w", "w_out", "final_norm_w"]
    ws = dict(zip(names, [norm_w, w_in, lru_conv_w, lru_conv_b, lru_wa, lru_ba, lru_wx, lru_bx, lru_lambda, lru_norm_w,
                          dn_conv_w, dn_A_log, dn_dt_bias, dn_norm_w, w_out, final_norm_w]))
    ms = dict(zip(names, [m_norm_w, m_w_in, m_lru_conv_w, m_lru_conv_b, m_lru_wa, m_lru_ba, m_lru_wx, m_lru_bx,
                          m_lru_lambda, m_lru_norm_w, m_dn_conv_w, m_dn_A_log, m_dn_dt_bias, m_dn_norm_w, m_w_out,
                          m_final_norm_w]))
    vs = dict(zip(names, [v_norm_w, v_w_in, v_lru_conv_w, v_lru_conv_b, v_lru_wa, v_lru_ba, v_lru_wx, v_lru_bx,
                          v_lru_lambda, v_lru_norm_w, v_dn_conv_w, v_dn_A_log, v_dn_dt_bias, v_dn_norm_w, v_w_out,
                          v_final_norm_w]))
    deltas, new_m, new_v = [], [], []
    for n in names:
        d, nm_, nv_ = _adamw(ws[n], grads[n], ms[n], vs[n], f"adamw_{n}")
        deltas.append(d)
        new_m.append(nm_)
        new_v.append(nv_)
    return (loss, grad_x[None], *[grads[n] for n in names], *deltas, *new_m, *new_v)


def _behind(t, token):
    return t if token is None else t + token[0:1, 0:1]


def _local_step(xs, target, norm_w, final_norm_w, weights_of, on_grad, lcw_full, dcw_full, vecs, dnvs, lru_wa, lru_wx):
    depth = norm_w.shape[0]
    saved = []
    h = xs
    for l in range(depth):
        wp_l, wo_l, token = weights_of(l, h)
        hn = _rms_fwd(h, _behind(norm_w[l][None], token), f"rms_fwd{l}")
        proj = _matmul(hn, wp_l, tm=2048, tn=896, tk=D_MODEL, name=f"in_proj{l}")
        y, hl = _lru_fwd(proj, lcw_full[l], vecs[l], lru_wa[l], lru_wx[l], f"lru_fwd{l}")
        y, o, states = _dn_fwd(proj, dcw_full[l], dnvs[l], y, f"dn_fwd{l}")
        out = _matmul(y, wo_l, tm=512, tn=D_MODEL, tk=D_MIX, res=h, name=f"out_proj{l}")
        saved.append((h, hn, proj, hl, o, states, y, wp_l, wo_l))
        h = out

    dh, d_fnw, loss_part = _final_loss(h, final_norm_w[None], target, "final_loss")

    g_nw, g_vec, g_wa, g_wx, g_lcw_, g_dcw_, g_dsc = ([None] * depth for _ in range(7))
    for l in reversed(range(depth)):
        hin, hn, proj, hl, o, states, y, wp_l, wo_l = saved[l]
        dy = _matmul(dh, wo_l, tb=True, tm=512, tn=1024, tk=D_MODEL, name=f"d_y{l}")
        g_wout = _matmul(y, dh, ta=True, tm=512, tn=D_MODEL, tk=512, out_dtype=bf16, name=f"d_wout{l}")
        token = on_grad("wout", l, g_wout)
        dlx, dlz, g_lcw_[l], g_vec[l], g_wa[l], g_wx[l] = _lru_bwd(dy, proj, hl, lcw_full[l], _behind(vecs[l], token),
                                                                 lru_wa[l], lru_wx[l], f"lru_bwd{l}")
        dq, dk, dv, dz, dba, dcw8, g_dsc[l] = _dn_bwd(dy, proj, o, states, dcw_full[l], dnvs[l], f"dn_bwd{l}")
        g_dcw_[l] = dcw8.reshape(HEADS, 4, 3, HD).transpose(1, 2, 0, 3).reshape(4, 3 * D_MODEL)
        dxs = [dlx, dlz, dq, dk, dv, dz]
        g_win = _d_win(_transpose(hn, f"hn_t{l}"), dxs, dba, f"d_win{l}")
        token = on_grad("win", l, g_win)
        dhn = _d_hn(dxs, _behind(dba, token), wp_l, f"d_hn{l}")
        dh, g_nw[l] = _rms_bwd(dhn, hin, norm_w[l][None], dh, f"rms_bwd{l}")
    return loss_part, dh, d_fnw, g_nw, g_vec, g_wa, g_wx, g_lcw_, g_dcw_, g_dsc


def _reduce_grads(chip, after, in_flight, d_fnw, g_nw, g_vec, g_wa, g_wx, g_lcw_, g_dcw_, g_dsc):
    depth = len(g_nw)
    both = lambda f: jnp.concatenate([f(l) for l in range(depth)])
    small = [
        both(lambda l: _rows128(g_nw[l])),
        both(lambda l: _rows128(g_vec[l][0])), both(lambda l: _rows128(g_vec[l][1])),
        both(lambda l: _rows128(g_vec[l][2])), both(lambda l: _rows128(g_vec[l][3])),
        both(lambda l: _rows128(g_vec[l][4])),
        both(lambda l: _rows128(g_wa[l])), both(lambda l: _rows128(g_wx[l])),
        both(lambda l: _rows128(g_dsc[l][0:1])), both(lambda l: _rows128(g_dsc[l][1:2])),
        both(lambda l: _rows128(g_dsc[l][2:3])),
        _rows128(d_fnw),
        both(lambda l: _rows128(g_lcw_[l])), both(lambda l: _rows128(g_dcw_[l])),
    ]
    counts = [t.shape[0] for t in small]
    total = sum(counts)
    per = -(-total // (8 * N_DEV)) * 8
    small = jnp.concatenate(small + [jnp.zeros((per * N_DEV - total, HD), f32)]).reshape(N_DEV, per, HD)

    small_started = _scatter_start([small], ["small"], "scatter_small_start")
    reduced = {}
    for key in sorted(in_flight, key=lambda k: -k[1]):
        started, srcs, kinds = in_flight[key]
        got = _scatter_finish(started, srcs, kinds, after, f"scatter_{key[0]}{key[1]}_wait")
        for kind, part in zip(kinds, got):
            reduced[(kind, key[1])] = _sum_parts(part, f"sum_{kind}{key[1]}")
    got = _scatter_finish(small_started, [small], ["small"], after, "scatter_small_wait")
    pairs = [None] * depth
    a_small = None
    for l in reversed(range(depth)):
        sums = [reduced[("win", l)], reduced[("wout", l)]]
        if l == 0:
            shared = _share_reduced(sums, _sum_parts(got[0], "sum_small"), f"share_grads{l}")
            a_small = shared[2].reshape(N_DEV * per, HD)
        else:
            shared = _share_reduced(sums, None, f"share_grads{l}")
        pairs[l] = shared
    grad_w_in = jnp.stack([lax.dynamic_slice_in_dim(pairs[l][0].reshape(D_MODEL, WIN_W), 4 * chip, SHARD_IN, 1)
                           for l in range(depth)])
    grad_w_out = jnp.stack([pairs[l][1].reshape(D_MIX // 4, D_MODEL) for l in range(depth)])

    offs = [0]
    for c in counts:
        offs.append(offs[-1] + c)

    def piece(k, rows_per_layer=None):
        t = a_small[offs[k]:offs[k + 1]]
        if rows_per_layer is None:
            return t
        return t.reshape(depth, -1, HD)[:, :rows_per_layer]

    vec = lambda k: piece(k).reshape(depth, D_MODEL)
    return {
        "norm_w": vec(0), "lru_conv_b": vec(1), "lru_ba": vec(2), "lru_bx": vec(3), "lru_lambda": vec(4),
        "lru_norm_w": vec(5),
        "lru_wa": piece(6).reshape(depth, HEADS, HD, HD), "lru_wx": piece(7).reshape(depth, HEADS, HD, HD),
        "dn_A_log": piece(8, 1)[:, 0, :HEADS], "dn_dt_bias": piece(9, 1)[:, 0, :HEADS], "dn_norm_w": piece(10, 1)[:, 0],
        "final_norm_w": piece(11).reshape(D_MODEL),
        "lru_conv_w": lax.dynamic_slice_in_dim(piece(12).reshape(depth, 4, D_MODEL), chip * (D_MODEL // 4), D_MODEL // 4, 2),
        "dn_conv_w": lax.dynamic_slice_in_dim(piece(13).reshape(depth, 4, 3 * D_MODEL), chip * (3 * D_MODEL // 4),
                                              3 * D_MODEL // 4, 2),
        "w_in": grad_w_in, "w_out": grad_w_out,
    }
```

```python
import jax
import jax.numpy as jnp
from jax import lax
from jax.experimental import pallas as pl
from jax.experimental.pallas import tpu as pltpu

f32 = jnp.float32
bf16 = jnp.bfloat16
HI = lax.Precision.HIGHEST
MESH = pl.DeviceIdType.MESH

D_MODEL = 1024
HEADS = 8
HD = 128
CHUNK = 64
LRU_T = 128
SEQ_BLOCK = 1024
DN_SEQ_BLOCK = 256
DN_HP = 8
LRU_C = 8.0
D_IN = 6160
D_INP = 6272
D_MIX = 2048
N_GROUPS = 6
BLK_BA = 48
SHARD_IN = D_IN // 4
WIN_W = 1664
WIN_STRIDE = 1536
EPS = 1e-6
QSCALE = HD ** -0.5
N_DEV = 8
VMEM_LIMIT = 56 * 1024 * 1024

ADAM_LR, ADAM_B1, ADAM_B2, ADAM_EPS, ADAM_WD, ADAM_STEP = 0.001, 0.9, 0.999, 1e-08, 0.01, 10


def _sig(x):
    return 1.0 / (1.0 + jnp.exp(-x))


def _log1p(u):
    w = 1.0 + u
    d = w - 1.0
    return jnp.where(d == 0.0, u, jnp.log(w) * (u / jnp.where(d == 0.0, 1.0, d)))


def _softplus(x):
    return jnp.maximum(x, 0.0) + _log1p(jnp.exp(-jnp.abs(x)))


def _expm1(x):
    t = x * (1.0 + x * (0.5 + x * (1.0 / 6 + x * (1.0 / 24 + x * (1.0 / 120 + x * (1.0 / 720))))))
    return jnp.where(jnp.abs(x) < 0.2, t, jnp.exp(x) - 1.0)


def _dot(a, b, prec=None):
    return jnp.dot(a, b, precision=prec, preferred_element_type=f32)


def _dot_nt(a, b, prec=None):
    return lax.dot_general(a, b, (((1,), (1,)), ((), ())), precision=prec, preferred_element_type=f32)


def _dot_tn(a, b, prec=None):
    return lax.dot_general(a, b, (((0,), (0,)), ((), ())), precision=prec, preferred_element_type=f32)


def _b(x):
    return x.astype(bf16)


def _sum0(x):
    return jnp.sum(x, axis=0, keepdims=True)


def _sum1(x):
    return jnp.sum(x, axis=1, keepdims=True)


def _rowmean(x):
    return jnp.mean(x, axis=-1, keepdims=True)


def _dsilu(z, sg):
    return sg * (1.0 + z * (1.0 - sg))


def _conv_taps(x, halo):
    xx = jnp.concatenate([halo, x], axis=0)
    return [pltpu.roll(xx, 3, axis=0)[8:], pltpu.roll(xx, 2, axis=0)[8:], pltpu.roll(xx, 1, axis=0)[8:], x]


def _conv(xs, cw):
    return cw[0:1] * xs[0] + cw[1:2] * xs[1] + cw[2:3] * xs[2] + cw[3:4] * xs[3]


def _rows_before(ref, halo_ref, lanes, t0, c, sblk):
    tprev = pl.multiple_of(jnp.maximum(t0 - 8, 0), 8)
    return jnp.where(c > 0, ref[pl.ds(tprev, 8), lanes], jnp.where(sblk > 0, halo_ref[:, lanes], 0.0))


def _conv_back(dxc, halo_after, cw):
    rows = dxc.shape[0]
    dd = jnp.concatenate([dxc, halo_after], axis=0)
    out = cw[3:4] * dxc
    for s in (1, 2, 3):
        out = out + cw[3 - s:4 - s] * pltpu.roll(dd, rows + 8 - s, axis=0)[:rows]
    return out


def _params(sem=None):
    return pltpu.CompilerParams(dimension_semantics=sem, vmem_limit_bytes=VMEM_LIMIT)


def _seq_specs(sb, width, rowblk, colblk):
    main = pl.BlockSpec((sb, width), lambda a, b: (rowblk(a, b), colblk(a, b)))
    halo = pl.BlockSpec((8, width), lambda a, b: (jnp.maximum(rowblk(a, b) * (sb // 8) - 1, 0), colblk(a, b)))
    return main, halo


def _matmul(a, b, *, ta=False, tb=False, tm, tn, tk, res=None, out_dtype=f32, name):
    if ta:
        kdim, m = a.shape
    else:
        m, kdim = a.shape
    n = b.shape[0] if tb else b.shape[1]
    tm, tn, tk = min(tm, m), min(tn, n), min(tk, kdim)
    assert m % tm == 0 and n % tn == 0 and kdim % tk == 0, (name, m, n, kdim, tm, tn, tk)
    nk = kdim // tk
    dims = (((0 if ta else 1,), (1 if tb else 0,)), ((), ()))
    has_res = res is not None

    def body(*refs):
        a_ref, b_ref = refs[:2]
        r_ref = refs[2] if has_res else None
        o_ref = refs[3] if has_res else refs[2]
        acc_ref = refs[-1] if nk > 1 else None
        part = lax.dot_general(_b(a_ref[...]), _b(b_ref[...]), dims, preferred_element_type=f32)

        def finish(total):
            if has_res:
                total = total + r_ref[...]
            o_ref[...] = total.astype(out_dtype)

        if nk == 1:
            finish(part)
        else:
            k = pl.program_id(2)

            @pl.when(k == 0)
            def _():
                acc_ref[...] = part

            @pl.when(k > 0)
            def _():
                acc_ref[...] += part

            @pl.when(k == nk - 1)
            def _():
                finish(acc_ref[...])

    a_spec = pl.BlockSpec((tk, tm), lambda i, j, k: (k, i)) if ta else pl.BlockSpec((tm, tk), lambda i, j, k: (i, k))
    b_spec = pl.BlockSpec((tn, tk), lambda i, j, k: (j, k)) if tb else pl.BlockSpec((tk, tn), lambda i, j, k: (k, j))
    o_spec = pl.BlockSpec((tm, tn), lambda i, j, k: (i, j))
    in_specs, args = [a_spec, b_spec], [a, b]
    if has_res:
        in_specs.append(o_spec)
        args.append(res)
    return pl.pallas_call(
        body, name=name, grid=(m // tm, n // tn, nk), in_specs=in_specs, out_specs=o_spec,
        out_shape=jax.ShapeDtypeStruct((m, n), out_dtype),
        scratch_shapes=[pltpu.VMEM((tm, tn), f32)] if nk > 1 else [],
        compiler_params=_params(("parallel", "parallel", "arbitrary")),
    )(*args)


def _transpose(x, name):
    r, c = x.shape
    tr, tc = min(512, r), min(512, c)

    def body(x_ref, o_ref):
        o_ref[...] = x_ref[...].T

    return pl.pallas_call(
        body, name=name, grid=(r // tr, c // tc), in_specs=[pl.BlockSpec((tr, tc), lambda i, j: (i, j))],
        out_specs=pl.BlockSpec((tc, tr), lambda i, j: (j, i)), out_shape=jax.ShapeDtypeStruct((c, r), x.dtype),
        compiler_params=_params(("parallel", "parallel")),
    )(x)


def _d_hn(dxs, dba, wp, name):
    s = dxs[0].shape[0]
    tm = min(1024, s)

    def body(*refs):
        dx_refs = refs[:N_GROUPS]
        dba_ref, w_ref, wba_ref, o_ref, acc_ref = refs[N_GROUPS:]
        k = pl.program_id(1)
        for g in range(N_GROUPS):
            @pl.when(k == g)
            def _(g=g):
                part = _dot_nt(_b(dx_refs[g][...]), w_ref[...])
                if g == 0:
                    acc_ref[...] = part + _dot_nt(_b(dba_ref[...]), wba_ref[...])
                else:
                    acc_ref[...] += part

        @pl.when(k == N_GROUPS - 1)
        def _():
            o_ref[...] = acc_ref[...]

    row = lambda w: pl.BlockSpec((tm, w), lambda i, k: (i, 0))
    return pl.pallas_call(
        body, name=name, grid=(s // tm, N_GROUPS),
        in_specs=[row(D_MODEL)] * N_GROUPS + [row(HD), pl.BlockSpec((D_MODEL, D_MODEL), lambda i, k: (0, k)),
                                              pl.BlockSpec((D_MODEL, HD), lambda i, k: (0, BLK_BA))],
        out_specs=row(D_MODEL), out_shape=jax.ShapeDtypeStruct((s, D_MODEL), f32),
        scratch_shapes=[pltpu.VMEM((tm, D_MODEL), f32)],
        compiler_params=_params(("parallel", "arbitrary")),
    )(*dxs, dba, wp, wp)


def _d_win(hnt, dxs, dba, name):
    s = hnt.shape[1]
    tn = 256
    per = D_MODEL // tn

    def body(*refs):
        hnt_ref = refs[0]
        dx_refs = refs[1:1 + N_GROUPS]
        o_ref = refs[1 + N_GROUPS]
        j = pl.program_id(0)
        for g in range(N_GROUPS):
            @pl.when(j // per == g)
            def _(g=g):
                o_ref[...] = _dot(hnt_ref[...], _b(dx_refs[g][...])).astype(bf16)

    def dx_spec(g):
        on = lambda j: (j // per == g).astype(jnp.int32)
        return pl.BlockSpec((s, tn), lambda j: (0, (j % per) * on(j)))

    main = pl.pallas_call(
        body, name=name, grid=(N_GROUPS * per,),
        in_specs=[pl.BlockSpec((D_MODEL, s), lambda j: (0, 0))] + [dx_spec(g) for g in range(N_GROUPS)],
        out_specs=pl.BlockSpec((D_MODEL, tn), lambda j: (0, j)),
        out_shape=jax.ShapeDtypeStruct((D_MODEL, D_INP), bf16),
        compiler_params=_params(("parallel",)),
    )(hnt, *dxs)

    def tail(hnt_ref, dba_ref, full_ref, o_ref):
        del full_ref
        o_ref[...] = _dot(hnt_ref[...], _b(dba_ref[...])).astype(bf16)

    return pl.pallas_call(
        tail, name=name + "_ba", grid=(1,),
        in_specs=[pl.BlockSpec((D_MODEL, s), lambda k: (0, 0)), pl.BlockSpec((s, HD), lambda k: (0, 0)),
                  pl.BlockSpec(memory_space=pl.ANY)],
        out_specs=pl.BlockSpec((D_MODEL, HD), lambda k: (0, BLK_BA)),
        out_shape=jax.ShapeDtypeStruct((D_MODEL, D_INP), bf16),
        input_output_aliases={2: 0},
        compiler_params=_params(("arbitrary",)),
    )(hnt, dba, main)


def _rms_fwd(x, w, name):
    s = x.shape[0]
    tr = min(512, s)

    def body(x_ref, w_ref, h_ref):
        xv = x_ref[...]
        r = lax.rsqrt(_rowmean(xv * xv) + EPS)
        h_ref[...] = (xv * r * w_ref[...]).astype(bf16)

    return pl.pallas_call(
        body, name=name, grid=(s // tr,),
        in_specs=[pl.BlockSpec((tr, D_MODEL), lambda i: (i, 0)), pl.BlockSpec((1, D_MODEL), lambda i: (0, 0))],
        out_specs=pl.BlockSpec((tr, D_MODEL), lambda i: (i, 0)),
        out_shape=jax.ShapeDtypeStruct((s, D_MODEL), bf16), compiler_params=_params(("parallel",)),
    )(x, w)


def _rms_bwd(dh, x, w, dres, name):
    s = x.shape[0]
    tr = min(512, s)

    def body(dh_ref, x_ref, w_ref, dres_ref, dx_ref, dw_ref):
        xv = x_ref[...]
        r = lax.rsqrt(_rowmean(xv * xv) + EPS)
        xn = xv * r
        d = dh_ref[...]
        dxn = d * w_ref[...]
        dx_ref[...] = dres_ref[...] + r * (dxn - xn * _rowmean(dxn * xn))
        part = _sum0(d * xn)

        @pl.when(pl.program_id(0) == 0)
        def _():
            dw_ref[...] = part

        @pl.when(pl.program_id(0) > 0)
        def _():
            dw_ref[...] += part

    row = pl.BlockSpec((tr, D_MODEL), lambda i: (i, 0))
    vec = pl.BlockSpec((1, D_MODEL), lambda i: (0, 0))
    return pl.pallas_call(
        body, name=name, grid=(s // tr,), in_specs=[row, row, vec, row], out_specs=[row, vec],
        out_shape=[jax.ShapeDtypeStruct((s, D_MODEL), f32), jax.ShapeDtypeStruct((1, D_MODEL), f32)],
        compiler_params=_params(("arbitrary",)),
    )(dh, x, w, dres)


def _final_loss(x, w, target, name):
    s = x.shape[0]
    tr = min(512, s)

    def body(x_ref, w_ref, t_ref, dx_ref, dw_ref, loss_ref):
        xv = x_ref[...]
        wv = w_ref[...]
        r = lax.rsqrt(_rowmean(xv * xv) + EPS)
        xn = xv * r
        e = xn * wv - t_ref[...]
        lb = jnp.broadcast_to(0.5 * _sum0(_rowmean(e * e)), (1, HD))
        dy = e * (1.0 / D_MODEL)
        dxn = dy * wv
        dx_ref[...] = r * (dxn - xn * _rowmean(dxn * xn))
        part = _sum0(dy * xn)

        @pl.when(pl.program_id(0) == 0)
        def _():
            dw_ref[...] = part
            loss_ref[...] = lb

        @pl.when(pl.program_id(0) > 0)
        def _():
            dw_ref[...] += part
            loss_ref[...] += lb

    row = pl.BlockSpec((tr, D_MODEL), lambda i: (i, 0))
    vec = pl.BlockSpec((1, D_MODEL), lambda i: (0, 0))
    lsp = pl.BlockSpec((1, HD), lambda i: (0, 0))
    return pl.pallas_call(
        body, name=name, grid=(s // tr,), in_specs=[row, vec, row], out_specs=[row, vec, lsp],
        out_shape=[jax.ShapeDtypeStruct((s, D_MODEL), f32), jax.ShapeDtypeStruct((1, D_MODEL), f32),
                   jax.ShapeDtypeStruct((1, HD), f32)],
        compiler_params=_params(("arbitrary",)),
    )(x, w, target)


ALL = slice(None)


def _lru_gates(xc, vec, wa, wx):
    sp = _softplus(-vec[3:4])
    xcb = _b(xc)
    r = _sig(_dot(xcb, wa) + vec[1:2])
    i = _sig(_dot(xcb, wx) + vec[2:3])
    la = (-LRU_C) * r * sp
    a = jnp.exp(la)
    mult = jnp.sqrt(-_expm1(2.0 * la))
    return r, i, a, mult, sp, xcb


def _lru_fwd(proj, cw, vec, wa, wx, name):
    s = proj.shape[0]
    sb = min(SEQ_BLOCK, s)
    nsb = s // sb
    t = min(LRU_T, sb)
    nc = sb // t

    def body(x_ref, xh_ref, z_ref, cw_ref, vec_ref, wa_ref, wx_ref, y_ref, hl_ref, hc_ref):
        sblk = pl.program_id(1)
        cwv = cw_ref[...]
        vec_v = vec_ref[...]
        wa_v = _b(wa_ref[...])
        wx_v = _b(wx_ref[...])
        row = lax.broadcasted_iota(jnp.int32, (t, HD), 0)

        @pl.when(sblk == 0)
        def _():
            hc_ref[...] = jnp.zeros((8, HD), f32)

        def chunk(c, hcarry):
            t0 = pl.multiple_of(c * t, t)
            xs = _conv_taps(x_ref[pl.ds(t0, t), :], _rows_before(x_ref, xh_ref, ALL, t0, c, sblk))
            xc = vec_v[0:1] + _conv(xs, cwv)
            r, i, a, mult, _, _ = _lru_gates(xc, vec_v, wa_v, wx_v)
            bb = mult * (i * xc)
            d = 1
            while d < t:
                m = row >= d
                bb = jnp.where(m, a * pltpu.roll(bb, d, axis=0) + bb, bb)
                a = jnp.where(m, a * pltpu.roll(a, d, axis=0), a)
                d *= 2
            hl = bb + a * hcarry
            hl_ref[pl.ds(t0, t), :] = hl
            z = z_ref[pl.ds(t0, t), :]
            nrm = lax.rsqrt(_rowmean(hl * hl) + EPS)
            y_ref[pl.ds(t0, t), :] = (hl * nrm * vec_v[4:5] * (z * _sig(z))).astype(bf16)
            return hl[t - 1:t, :]

        hlast = lax.fori_loop(0, nc, chunk, hc_ref[0:1, :])
        hc_ref[...] = jnp.broadcast_to(hlast, (8, HD))

    xsp, xhalo = _seq_specs(sb, HD, lambda h, i: i, lambda h, i: h)
    zsp, _ = _seq_specs(sb, HD, lambda h, i: i, lambda h, i: HEADS + h)
    hcol = lambda h, i: (0, h)
    wsp = pl.BlockSpec((None, HD, HD), lambda h, i: (h, 0, 0))
    osp = pl.BlockSpec((sb, HD), lambda h, i: (i, h))
    return pl.pallas_call(
        body, name=name, grid=(HEADS, nsb),
        in_specs=[xsp, xhalo, zsp, pl.BlockSpec((4, HD), hcol), pl.BlockSpec((8, HD), hcol), wsp, wsp],
        out_specs=[osp, osp],
        out_shape=[jax.ShapeDtypeStruct((s, D_MIX), bf16), jax.ShapeDtypeStruct((s, D_MODEL), f32)],
        scratch_shapes=[pltpu.VMEM((8, HD), f32)],
        compiler_params=_params(("parallel", "arbitrary")),
    )(proj, proj, proj, cw, vec, wa, wx)


def _lru_bwd(dy, proj, hl, cw, vec, wa, wx, name):
    s = proj.shape[0]
    sb = min(SEQ_BLOCK, s)
    nsb = s // sb
    t = min(LRU_T, sb)
    nc = sb // t

    def body(dy_ref, x_ref, xh_ref, z_ref, hl_ref, hlh_ref, cw_ref, vec_ref, wa_ref, wx_ref,
             dx_ref, dz_ref, dcw_ref, dvec_ref, dwa_ref, dwx_ref, acc_ref, dxh_ref):
        step = pl.program_id(1)
        sblk = nsb - 1 - step
        cwv = cw_ref[...]
        vec_v = vec_ref[...]
        wa_v = _b(wa_ref[...])
        wx_v = _b(wx_ref[...])
        lnw = vec_v[4:5]
        row = lax.broadcasted_iota(jnp.int32, (t, HD), 0)

        @pl.when(step == 0)
        def _():
            acc_ref[...] = jnp.zeros((16, HD), f32)
            dxh_ref[...] = jnp.zeros((8, HD), f32)
            dwa_ref[...] = jnp.zeros((HD, HD), f32)
            dwx_ref[...] = jnp.zeros((HD, HD), f32)

        def chunk(ci, carry):
            mu, dxc_halo, dcw0, dcw1, dcw2, dcw3, dcb, dba, dbx, dsp, dlnw = carry
            c = nc - 1 - ci
            t0 = pl.multiple_of(c * t, t)
            xs = _conv_taps(x_ref[pl.ds(t0, t), :], _rows_before(x_ref, xh_ref, ALL, t0, c, sblk))
            xc = vec_v[0:1] + _conv(xs, cwv)
            r, i, a, mult, sp, xcb = _lru_gates(xc, vec_v, wa_v, wx_v)
            hv = hl_ref[pl.ds(t0, t), :]
            hhalo = _rows_before(hl_ref, hlh_ref, ALL, t0, c, sblk)
            hprev = pltpu.roll(jnp.concatenate([hhalo, hv], axis=0), 1, axis=0)[8:]
            z = z_ref[pl.ds(t0, t), :]
            sgz = _sig(z)
            sz = z * sgz
            dyv = dy_ref[pl.ds(t0, t), :]
            nrm = lax.rsqrt(_rowmean(hv * hv) + EPS)
            hn = hv * nrm
            dlnw = dlnw + _sum0(dyv * hn * sz)
            dz_ref[pl.ds(t0, t), :] = _b(dyv * hn * lnw * _dsilu(z, sgz))
            dhn = dyv * lnw * sz
            lam = nrm * (dhn - hn * _rowmean(dhn * hn))
            cf = jnp.where(row == t - 1, 1.0, pltpu.roll(a, t - 1, axis=0))
            d = 1
            while d < t:
                m = row < t - d
                lam = jnp.where(m, lam + cf * pltpu.roll(lam, t - d, axis=0), lam)
                cf = jnp.where(m, cf * pltpu.roll(cf, t - d, axis=0), cf)
                d *= 2
            lam = lam + cf * mu
            mu = a[0:1] * lam[0:1]
            da = lam * hprev
            dmult = lam * (i * xc)
            di = lam * mult * xc
            dxc = lam * mult * i
            dla = da * a - dmult * (a * a) / mult
            dr = dla * (-LRU_C * sp)
            dsp = dsp + _sum0(dla * (-LRU_C * r))
            dra = dr * r * (1.0 - r)
            dia = di * i * (1.0 - i)
            dba = dba + _sum0(dra)
            dbx = dbx + _sum0(dia)
            drab, diab = _b(dra), _b(dia)
            dwa_ref[...] += _dot_tn(xcb, drab)
            dwx_ref[...] += _dot_tn(xcb, diab)
            dxc = dxc + _dot_nt(drab, wa_v) + _dot_nt(diab, wx_v)
            dcb = dcb + _sum0(dxc)
            dcw0 = dcw0 + _sum0(dxc * xs[0])
            dcw1 = dcw1 + _sum0(dxc * xs[1])
            dcw2 = dcw2 + _sum0(dxc * xs[2])
            dcw3 = dcw3 + _sum0(dxc * xs[3])
            dx_ref[pl.ds(t0, t), :] = _b(_conv_back(dxc, dxc_halo, cwv))
            return (mu, dxc[0:8], dcw0, dcw1, dcw2, dcw3, dcb, dba, dbx, dsp, dlnw)

        acc = acc_ref[...]
        init = (acc[9:10], dxh_ref[...]) + tuple(acc[k:k + 1] for k in range(9))
        mu, dxh, dcw0, dcw1, dcw2, dcw3, dcb, dba, dbx, dsp, dlnw = lax.fori_loop(0, nc, chunk, init)
        zero = jnp.zeros((1, HD), f32)
        acc_ref[...] = jnp.concatenate([dcw0, dcw1, dcw2, dcw3, dcb, dba, dbx, dsp, dlnw, mu] + [zero] * 6, axis=0)
        dxh_ref[...] = dxh

        @pl.when(step == nsb - 1)
        def _():
            dcw_ref[...] = jnp.concatenate([dcw0, dcw1, dcw2, dcw3], axis=0)
            dlam = -dsp * _sig(-vec_v[3:4])
            dvec_ref[...] = jnp.concatenate([dcb, dba, dbx, dlam, dlnw, zero, zero, zero], axis=0)

    rblk = lambda h, i: nsb - 1 - i
    xsp, xhalo = _seq_specs(sb, HD, rblk, lambda h, i: h)
    zsp, _ = _seq_specs(sb, HD, rblk, lambda h, i: HEADS + h)
    hcol = lambda h, i: (0, h)
    wsp = pl.BlockSpec((None, HD, HD), lambda h, i: (h, 0, 0))
    return pl.pallas_call(
        body, name=name, grid=(HEADS, nsb),
        in_specs=[xsp, xsp, xhalo, zsp, xsp, xhalo, pl.BlockSpec((4, HD), hcol), pl.BlockSpec((8, HD), hcol), wsp, wsp],
        out_specs=[xsp, xsp, pl.BlockSpec((4, HD), hcol), pl.BlockSpec((8, HD), hcol), wsp, wsp],
        out_shape=[jax.ShapeDtypeStruct((s, D_MODEL), bf16), jax.ShapeDtypeStruct((s, D_MODEL), bf16),
                   jax.ShapeDtypeStruct((4, D_MODEL), f32), jax.ShapeDtypeStruct((8, D_MODEL), f32),
                   jax.ShapeDtypeStruct((HEADS, HD, HD), f32), jax.ShapeDtypeStruct((HEADS, HD, HD), f32)],
        scratch_shapes=[pltpu.VMEM((16, HD), f32), pltpu.VMEM((8, HD), f32)],
        compiler_params=_params(("parallel", "arbitrary")),
    )(dy, proj, proj, proj, hl, hl, cw, vec, wa, wx)


def _lane_pick(x, lane, idx):
    return _sum1(jnp.where(lane == idx, x, 0.0))


def _round_robin(gens):
    results = [None] * len(gens)
    live = list(range(len(gens)))
    while live:
        for i in list(live):
            try:
                next(gens[i])
            except StopIteration as done:
                results[i] = done.value
                live.remove(i)
    return results


def _sdot(a, b):
    return _dot(_b(a), _b(b))


def _sdot_tn(a, b):
    return _dot_tn(_b(a), _b(b))


def _sdot_nt(a, b):
    return _dot_nt(_b(a), _b(b))


def _tri_dot(tri, x):
    trib = tri.astype(bf16)
    x1 = x.astype(bf16)
    r1 = x - x1.astype(f32)
    x2 = r1.astype(bf16)
    x3 = (r1 - x2.astype(f32)).astype(bf16)
    return _dot(trib, x1) + _dot(trib, x2) + _dot(trib, x3)


def _inv_unit_lower(a):
    n = -a
    p = _sdot(a, a)
    yield
    for k in range(5):
        n = n + p + _sdot(n, p)
        if k < 4:
            p = _sdot(p, p)
        yield
    return n


def _dn_chunk(x3, halo3, bav, cw3, dnv, h, masks, tinv=None):
    lane, ri, ci, eye, ltri = masks
    xs = _conv_taps(x3, halo3)
    cpre = _conv(xs, cw3)
    sg = _sig(cpre)
    act = cpre * sg
    q_raw, k_raw, v = act[:, 0:HD], act[:, HD:2 * HD], act[:, 2 * HD:3 * HD]
    rq = lax.rsqrt(_sum1(q_raw * q_raw) + EPS)
    rk = lax.rsqrt(_sum1(k_raw * k_raw) + EPS)
    qn = q_raw * rq
    k = k_raw * rk
    q = qn * QSCALE
    b_in = jnp.broadcast_to(_lane_pick(bav, lane, h), (CHUNK, HD))
    a_in = jnp.broadcast_to(_lane_pick(bav, lane, HEADS + h), (CHUNK, HD))
    lane1 = lane[0:1]
    ea = jnp.exp(_lane_pick(dnv[0:1], lane1, h))
    dt = _lane_pick(dnv[1:2], lane1, h)
    beta = _sig(b_in)
    sp = _softplus(a_in + dt)
    g = -ea * sp
    gc = _tri_dot(ltri, g)
    yield
    gc64 = gc[:, 0:CHUNK]
    grow = _sum0(gc64 * eye)
    dm = jnp.exp(jnp.where(ri >= ci, gc64 - grow, -1e30))
    ds_ = jnp.where(ri > ci, dm, 0.0)
    eg = jnp.exp(gc)
    glast = gc[CHUNK - 1:CHUNK, :]
    ek = jnp.exp(glast - gc)
    kb = k * beta
    kbf = _b(k)
    amat = _dot_nt(_b(kb), kbf) * ds_
    pmat = _dot_nt(_b(q), kbf) * dm
    yield
    if tinv is None:
        tinv = yield from _inv_unit_lower(amat)
    return dict(xs=xs, cpre=cpre, sg=sg, rq=rq, rk=rk, qn=qn, q=q, k=k, v=v, kbf=kbf, b_in=b_in, a_in=a_in, ea=ea,
                dt=dt, beta=beta, g=g, gc=gc, dm=dm, ds=ds_, eg=eg, glast=glast, ek=ek, kb=kb, amat=amat,
                tinv=tinv, pmat=pmat)


def _dn_masks():
    lane = lax.broadcasted_iota(jnp.int32, (CHUNK, HD), 1)
    ri = lax.broadcasted_iota(jnp.int32, (CHUNK, CHUNK), 0)
    ci = lax.broadcasted_iota(jnp.int32, (CHUNK, CHUNK), 1)
    eye = (ri == ci).astype(f32)
    ltri = (ri >= ci).astype(f32)
    return lane, ri, ci, eye, ltri


def _dn_load_qkv(q_ref, qh_ref, k_ref, kh_ref, v_ref, vh_ref, hl, rows, t0, c, sblk):
    x3 = jnp.concatenate([q_ref[rows, hl], k_ref[rows, hl], v_ref[rows, hl]], axis=1)
    halo3 = jnp.concatenate([_rows_before(q_ref, qh_ref, hl, t0, c, sblk), _rows_before(k_ref, kh_ref, hl, t0, c, sblk),
                             _rows_before(v_ref, vh_ref, hl, t0, c, sblk)], axis=1)
    return x3, halo3


def _dn_cw3(cwq_ref, cwk_ref, cwv_ref, j):
    return jnp.concatenate([r[:, j * HD:(j + 1) * HD] for r in (cwq_ref, cwk_ref, cwv_ref)], axis=1)


def _dn_fwd(proj, cw, dnv, y, name):
    s = proj.shape[0]
    sb = min(DN_SEQ_BLOCK, s)
    nsb = s // sb
    nc = sb // CHUNK
    hp = DN_HP

    def body(q_ref, qh_ref, k_ref, kh_ref, v_ref, vh_ref, z_ref, ba_ref, cwq_ref, cwk_ref, cwv_ref, dnv_ref, yin_ref,
             y_ref, o_ref, st_ref, ti_ref, s_ref):
        del yin_ref
        grp = pl.program_id(0)
        sblk = pl.program_id(1)
        masks = _dn_masks()
        dnv_v = dnv_ref[...]
        cw3 = [_dn_cw3(cwq_ref, cwk_ref, cwv_ref, j) for j in range(hp)]

        @pl.when(sblk == 0)
        def _():
            s_ref[...] = jnp.zeros((hp, HD, HD), f32)

        def one_head(j, x3, halo3, bav, z, st):
            f = yield from _dn_chunk(x3, halo3, bav, cw3[j], dnv_v, grp * hp + j, masks)
            sbf = _b(st)
            qs = _dot(_b(f["q"] * f["eg"]), sbf)
            rhs = f["beta"] * (f["v"] - _dot(_b(f["k"] * f["eg"]), sbf))
            yield
            vn = rhs + _sdot(f["tinv"], rhs)
            yield
            vnb = _b(vn)
            o = qs + _dot(_b(f["pmat"]), vnb)
            st_new = st * jnp.exp(f["glast"]) + _dot_tn(_b(f["k"] * f["ek"]), vnb)
            yield
            nrm = lax.rsqrt(_rowmean(o * o) + EPS)
            yv = (o * nrm * dnv_v[2:3] * (z * _sig(z))).astype(bf16)
            return o, yv, st_new, f["tinv"]

        def chunk(c, states):
            t0 = pl.multiple_of(c * CHUNK, CHUNK)
            rows = pl.ds(t0, CHUNK)
            bav = ba_ref[rows, :]
            ins = []
            for j in range(hp):
                hl = slice(j * HD, (j + 1) * HD)
                ins.append(_dn_load_qkv(q_ref, qh_ref, k_ref, kh_ref, v_ref, vh_ref, hl, rows, t0, c, sblk)
                           + (bav, z_ref[rows, hl]))
            outs = _round_robin([one_head(j, *ins[j], states[j]) for j in range(hp)])
            for j in range(hp):
                hl = slice(j * HD, (j + 1) * HD)
                st_ref[j, c] = states[j]
                o_ref[rows, hl] = outs[j][0]
                y_ref[rows, hl] = outs[j][1]
                ti_ref[j, c] = outs[j][3]
            return tuple(out[2] for out in outs)

        final = lax.fori_loop(0, nc, chunk, tuple(s_ref[j] for j in range(hp)))
        for j in range(hp):
            s_ref[j] = final[j]

    wide = hp * HD
    grp_specs = lambda off: _seq_specs(sb, wide, lambda g, i: i, lambda g, i: off // hp + g)
    (qsp, qhalo), (ksp, khalo), (vsp, vhalo) = grp_specs(2 * HEADS), grp_specs(3 * HEADS), grp_specs(4 * HEADS)
    zsp, _ = grp_specs(5 * HEADS)
    basp = pl.BlockSpec((sb, HD), lambda g, i: (i, BLK_BA))
    cws = lambda off: pl.BlockSpec((4, wide), lambda g, i: (0, off // hp + g))
    osp = lambda off: pl.BlockSpec((sb, wide), lambda g, i: (i, off // hp + g))
    return pl.pallas_call(
        body, name=name, grid=(HEADS // hp, nsb),
        in_specs=[qsp, qhalo, ksp, khalo, vsp, vhalo, zsp, basp, cws(0), cws(HEADS), cws(2 * HEADS),
                  pl.BlockSpec((8, HD), lambda g, i: (0, 0)), pl.BlockSpec(memory_space=pl.ANY)],
        out_specs=[osp(HEADS), osp(0), pl.BlockSpec((hp, nc, HD, HD), lambda g, i: (g, i, 0, 0)),
                   pl.BlockSpec((hp, nc, CHUNK, CHUNK), lambda g, i: (g, i, 0, 0))],
        out_shape=[jax.ShapeDtypeStruct((s, D_MIX), bf16), jax.ShapeDtypeStruct((s, D_MODEL), f32),
                   jax.ShapeDtypeStruct((HEADS, s // CHUNK, HD, HD), f32),
                   jax.ShapeDtypeStruct((HEADS, s // CHUNK, CHUNK, CHUNK), f32)],
        scratch_shapes=[pltpu.VMEM((hp, HD, HD), f32)],
        input_output_aliases={12: 0},
        compiler_params=_params(("parallel", "arbitrary")),
    )(proj, proj, proj, proj, proj, proj, proj, proj, cw, cw, cw, dnv, y)


def _dn_bwd(dy, proj, o, states, tinvs, cw, dnv, name):
    s = proj.shape[0]
    sb = min(DN_SEQ_BLOCK, s)
    nsb = s // sb
    nc = sb // CHUNK
    hp = DN_HP
    ngrp = HEADS // hp

    def body(dy_ref, q_ref, qh_ref, k_ref, kh_ref, v_ref, vh_ref, z_ref, ba_ref, o_ref, st_ref, ti_ref,
             cwq_ref, cwk_ref, cwv_ref, dnv_ref,
             dq_ref, dk_ref, dv_ref, dz_ref, dba_ref, dcw_ref, dsc_ref,
             ds_ref, dch_ref, cwacc_ref, sacc_ref, nacc_ref):
        step = pl.program_id(0)
        grp = pl.program_id(1)
        sblk = nsb - 1 - step
        h0 = grp * hp
        masks = _dn_masks()
        lane, ri, ci, eye, ltri = masks
        utri = (ri <= ci).astype(f32)
        cw3s = [_dn_cw3(cwq_ref, cwk_ref, cwv_ref, j) for j in range(hp)]
        dnv_v = dnv_ref[...]
        dnw = dnv_v[2:3]
        row64 = lax.broadcasted_iota(jnp.int32, (CHUNK, HD), 0)

        @pl.when(step == 0)
        def _():
            for j in range(hp):
                ds_ref[h0 + j] = jnp.zeros((HD, HD), f32)
                dch_ref[h0 + j] = jnp.zeros((8, 3 * HD), f32)
                cwacc_ref[h0 + j] = jnp.zeros((8, 3 * HD), f32)
                sacc_ref[h0 + j] = jnp.zeros((8, HD), f32)

        @pl.when((step == 0) & (grp == 0))
        def _():
            nacc_ref[...] = jnp.zeros((8, HD), f32)

        def one_head(j, x3, halo3, bav, z, st, ti, ov, dyv, carry):
            dst, dch, cwacc, sa0, sa1 = carry
            hd = h0 + j
            cw3 = cw3s[j]
            f = yield from _dn_chunk(x3, halo3, bav, cw3, dnv_v, hd, masks, ti)
            q, k, v, beta, eg, ek, kbf = f["q"], f["k"], f["v"], f["beta"], f["eg"], f["ek"], f["kbf"]
            sbf = _b(st)
            dstb = _b(dst)
            qe = q * eg
            keg = k * eg
            kd = k * ek
            kegb, qeb, kdb = _b(keg), _b(qe), _b(kd)
            e = v - _dot(kegb, sbf)
            yield
            rhs = beta * e
            vn = rhs + _sdot(f["tinv"], rhs)
            yield
            vnb = _b(vn)
            pb = _b(f["pmat"])
            sgz = _sig(z)
            sz = z * sgz
            nrm = lax.rsqrt(_rowmean(ov * ov) + EPS)
            on = ov * nrm
            ddnw = _sum0(dyv * on * sz)
            dpz = dyv * on * dnw * _dsilu(z, sgz)
            don = dyv * dnw * sz
            do = nrm * (don - on * _rowmean(don * on))
            dob = _b(do)
            dvn = _dot_tn(pb, dob) + _dot(kdb, dstb)
            dpm = jnp.where(ri >= ci, _dot_nt(dob, vnb), 0.0)
            dqe = _dot_nt(dob, sbf)
            dkd = _dot_nt(vnb, dstb)
            qtdo = _dot_tn(qeb, dob)
            yield
            drhs = dvn + _sdot_tn(f["tinv"], dvn)
            dqk = _b(dpm * f["dm"])
            dq = _dot(dqk, kbf) + dqe * eg
            dk_p = _dot_tn(dqk, _b(q))
            yield
            dam = jnp.where(ri > ci, -_sdot_nt(drhs, vn), 0.0)
            de = drhs * beta
            deb = _b(de)
            dbeta = _sum1(drhs * e)
            dkeg = -_dot_nt(deb, sbf)
            dst_new = qtdo + dst * jnp.exp(f["glast"]) - _dot_tn(kegb, deb)
            yield
            dkk = _b(dam * f["ds"])
            dkb = _dot(dkk, kbf)
            dk = dk_p + _dot_tn(dkk, _b(f["kb"])) + dkb * beta + dkeg * eg + dkd * ek
            yield
            dbeta = dbeta + _sum1(dkb * k)
            mm = dpm * f["pmat"] + dam * f["amat"]
            dglast = _sum1(_sum0(dst * st)) * jnp.exp(f["glast"]) + _sum1(_sum0(dkd * kd))
            dgc = (_sum1(mm) - _sum1(eye * _sum0(mm)) + _sum1(dqe * qe) + _sum1(dkeg * keg) - _sum1(dkd * kd))
            dgc = jnp.broadcast_to(dgc, (CHUNK, HD)) + jnp.where(row64 == CHUNK - 1, dglast, 0.0)
            dg = _tri_dot(utri, dgc)
            yield
            dbin = jnp.broadcast_to(dbeta, (CHUNK, HD)) * beta * (1.0 - beta)
            dain = dg * (-f["ea"]) * _sig(f["a_in"] + f["dt"])
            dba = jnp.where(lane == hd, dbin, 0.0) + jnp.where(lane == HEADS + hd, dain, 0.0)
            sa0 = sa0 + _sum0(dg * f["g"])
            sa1 = sa1 + _sum0(dain)
            dq_raw = (QSCALE * f["rq"]) * (dq - f["qn"] * _sum1(f["qn"] * dq))
            dk_raw = f["rk"] * (dk - k * _sum1(k * dk))
            dact = jnp.concatenate([dq_raw, dk_raw, de], axis=1)
            dc = dact * _dsilu(f["cpre"], f["sg"])
            xs = f["xs"]
            cwacc = cwacc + jnp.concatenate([_sum0(dc * xs[0]), _sum0(dc * xs[1]), _sum0(dc * xs[2]),
                                             _sum0(dc * xs[3])], axis=0)
            dqkv = _conv_back(dc, dch, cw3)
            return dpz, dqkv, dba, ddnw, (dst_new, dc[0:8], cwacc, sa0, sa1)

        def chunk(cidx, carry):
            heads, nacc = carry
            c = nc - 1 - cidx
            t0 = pl.multiple_of(c * CHUNK, CHUNK)
            rows = pl.ds(t0, CHUNK)
            bav = ba_ref[rows, :]
            ins = []
            for j in range(hp):
                hl = slice(j * HD, (j + 1) * HD)
                ins.append(_dn_load_qkv(q_ref, qh_ref, k_ref, kh_ref, v_ref, vh_ref, hl, rows, t0, c, sblk)
                           + (bav, z_ref[rows, hl], st_ref[j, c], ti_ref[j, c], o_ref[rows, hl], dy_ref[rows, hl]))
            outs = _round_robin([one_head(j, *ins[j], heads[j]) for j in range(hp)])
            dba = outs[0][2]
            for j in range(hp):
                hl = slice(j * HD, (j + 1) * HD)
                dpz, dqkv, dba_j, ddnw, _ = outs[j]
                dq_ref[rows, hl] = _b(dqkv[:, 0:HD])
                dk_ref[rows, hl] = _b(dqkv[:, HD:2 * HD])
                dv_ref[rows, hl] = _b(dqkv[:, 2 * HD:3 * HD])
                dz_ref[rows, hl] = _b(dpz)
                nacc = nacc + ddnw
                if j > 0:
                    dba = dba + dba_j

            @pl.when(grp == 0)
            def _():
                dba_ref[rows, :] = dba

            @pl.when(grp > 0)
            def _():
                dba_ref[rows, :] += dba

            return tuple(out[4] for out in outs), nacc

        init = tuple((ds_ref[h0 + j], dch_ref[h0 + j], cwacc_ref[h0 + j][0:4], sacc_ref[h0 + j][0:1],
                      sacc_ref[h0 + j][1:2]) for j in range(hp))
        heads, nacc = lax.fori_loop(0, nc, chunk, (init, nacc_ref[0:1, :]))
        nacc_ref[0:1, :] = nacc
        zpad = jnp.zeros((4, 3 * HD), f32)
        zrow = jnp.zeros((6, HD), f32)
        for j in range(hp):
            dst, dch, cwacc, sa0, sa1 = heads[j]
            ds_ref[h0 + j] = dst
            dch_ref[h0 + j] = dch
            cwacc_ref[h0 + j] = jnp.concatenate([cwacc, zpad], axis=0)
            sacc_ref[h0 + j] = jnp.concatenate([sa0, sa1, zrow], axis=0)

        @pl.when(step == nsb - 1)
        def _():
            lane8 = lax.broadcasted_iota(jnp.int32, (8, HD), 1)
            row8 = lax.broadcasted_iota(jnp.int32, (8, HD), 0)
            mine = jnp.zeros((8, HD), f32)
            for j in range(hp):
                dcw_ref[h0 + j] = heads[j][2]
                sa = jnp.concatenate([heads[j][3], heads[j][4], zrow], axis=0)
                mine = mine + jnp.where((lane8 == h0 + j) & (row8 < 2), sa, 0.0)

            @pl.when(grp == 0)
            def _():
                dsc_ref[...] = mine

            @pl.when(grp > 0)
            def _():
                dsc_ref[...] += mine

            @pl.when(grp == ngrp - 1)
            def _():
                dsc_ref[2:3, :] = nacc

    wide = hp * HD
    rblk = lambda i, g: nsb - 1 - i
    grp_specs = lambda off: _seq_specs(sb, wide, rblk, lambda i, g: off // hp + g)
    (qsp, qhalo), (ksp, khalo), (vsp, vhalo) = grp_specs(2 * HEADS), grp_specs(3 * HEADS), grp_specs(4 * HEADS)
    zsp, _ = grp_specs(5 * HEADS)
    hsp = lambda off: pl.BlockSpec((sb, wide), lambda i, g: (rblk(i, g), off // hp + g))
    basp = lambda col: pl.BlockSpec((sb, HD), lambda i, g: (rblk(i, g), col))
    cws = lambda off: pl.BlockSpec((4, wide), lambda i, g: (0, off // hp + g))
    whole = lambda shape: pl.BlockSpec(shape, lambda i, g: (0,) * len(shape))
    return pl.pallas_call(
        body, name=name, grid=(nsb, ngrp),
        in_specs=[hsp(HEADS), qsp, qhalo, ksp, khalo, vsp, vhalo, zsp, basp(BLK_BA), hsp(0),
                  pl.BlockSpec((hp, nc, HD, HD), lambda i, g: (g, rblk(i, g), 0, 0)),
                  pl.BlockSpec((hp, nc, CHUNK, CHUNK), lambda i, g: (g, rblk(i, g), 0, 0)),
                  cws(0), cws(HEADS), cws(2 * HEADS), whole((8, HD))],
        out_specs=[hsp(0), hsp(0), hsp(0), hsp(0), basp(0), whole((HEADS, 4, 3 * HD)), whole((8, HD))],
        out_shape=[jax.ShapeDtypeStruct((s, D_MODEL), bf16)] * 4
        + [jax.ShapeDtypeStruct((s, HD), f32), jax.ShapeDtypeStruct((HEADS, 4, 3 * HD), f32),
           jax.ShapeDtypeStruct((8, HD), f32)],
        scratch_shapes=[pltpu.VMEM((HEADS, HD, HD), f32), pltpu.VMEM((HEADS, 8, 3 * HD), f32),
                        pltpu.VMEM((HEADS, 8, 3 * HD), f32), pltpu.VMEM((HEADS, 8, HD), f32), pltpu.VMEM((8, HD), f32)],
        compiler_params=_params(("arbitrary", "arbitrary")),
    )(dy, proj, proj, proj, proj, proj, proj, proj, proj, o, states, tinvs, cw, cw, cw, dnv)


ANY = pl.BlockSpec(memory_space=pl.ANY)
CHIP_MASKS = ((1, 0, 0), (0, 1, 0), (1, 1, 0))
ALL_MASKS = tuple((m >> 2 & 1, m >> 1 & 1, m & 1) for m in range(1, N_DEV))


def _me():
    return lax.axis_index("x"), lax.axis_index("y"), lax.axis_index("c")


def _flip(me, mask):
    return tuple((1 - v) if m else v for v, m in zip(me, mask))


def _dev_index(dev):
    return 4 * dev[0] + 2 * dev[1] + dev[2]


def _chip(dev):
    return 2 * dev[0] + dev[1]


def _comm_call(name, body, arrays, out_shapes, nsem, nloc=0):
    scratch = [pltpu.SemaphoreType.DMA((nsem,)), pltpu.SemaphoreType.DMA((nsem,))]
    if nloc:
        scratch.append(pltpu.SemaphoreType.DMA((nloc,)))
    return pl.pallas_call(
        body, name=name, in_specs=[ANY] * len(arrays), out_specs=[ANY] * len(out_shapes), out_shape=out_shapes,
        scratch_shapes=scratch, compiler_params=pltpu.CompilerParams(has_side_effects=True),
    )(*arrays)


def _put_own(gathered, own, slot):
    return lax.dynamic_update_index_in_dim(gathered, own, slot, 0)


def _gather_halves(arrays, whole_arrays, name):
    na, nw = len(arrays), len(whole_arrays)
    nm = len(CHIP_MASKS)

    def body(*refs):
        srcs, dsts = refs[:na + nw], refs[na + nw:2 * (na + nw)]
        send_sems, recv_sems = refs[2 * (na + nw):]
        me = _me()
        sib = _flip(me, (0, 0, 1))

        def half(a, ref, core):
            rows = arrays[a].shape[0] // 2
            return ref.at[pl.ds(pl.multiple_of(core * rows, rows), rows)]

        def over_ici(a, mi, sender, to):
            if a < na:
                src, dst = half(a, srcs[a], sender[2]), half(a, dsts[a].at[_chip(sender)], sender[2])
            else:
                src, dst = srcs[a], dsts[a].at[_chip(sender)]
            return pltpu.make_async_remote_copy(src_ref=src, dst_ref=dst, send_sem=send_sems.at[a * nm + mi],
                                                recv_sem=recv_sems.at[a * nm + mi], device_id=to, device_id_type=MESH)

        def over_d2d(a, mi, origin, sender, to):
            k = (na + nw) * nm + a * nm + mi
            blk = half(a, dsts[a].at[_chip(origin)], sender[2])
            return pltpu.make_async_remote_copy(src_ref=blk, dst_ref=blk, send_sem=send_sems.at[k],
                                                recv_sem=recv_sems.at[k], device_id=to, device_id_type=MESH)

        sends = []
        for a in range(na + nw):
            for mi, mask in enumerate(CHIP_MASKS):
                cp = over_ici(a, mi, me, _flip(me, mask))
                cp.start()
                sends.append(cp)
        for a in range(na + nw):
            for mi, mask in enumerate(CHIP_MASKS):
                peer = _flip(me, mask)
                over_ici(a, mi, peer, me).wait_recv()
                if a < na:
                    cp = over_d2d(a, mi, peer, me, sib)
                    cp.start()
                    sends.append(cp)
        for a in range(na):
            for mi, mask in enumerate(CHIP_MASKS):
                over_d2d(a, mi, _flip(sib, mask), sib, me).wait_recv()
        for cp in sends:
            cp.wait_send()

    every = list(arrays) + list(whole_arrays)
    got = _comm_call(name, body, every, [jax.ShapeDtypeStruct((4,) + t.shape, t.dtype) for t in every],
                     (2 * na + nw) * nm)
    chip = 2 * lax.axis_index("x") + lax.axis_index("y")
    return [_put_own(g, t, chip) for g, t in zip(got, every)]


HBM = pl.BlockSpec(memory_space=pltpu.HBM)
SEM = pl.BlockSpec(memory_space=pltpu.SEMAPHORE)
DATAFLOW = pltpu.SideEffectType.DATAFLOW_SIDE_EFFECTING


def _split_start(name, srcs, land_shapes, nsem, issue):
    ns, nl = len(srcs), len(land_shapes)

    def body(*refs):
        issue(refs[:ns], refs[ns:ns + nl], refs[2 * (ns + nl)], refs[2 * (ns + nl) + 1])
        token = refs[-1]
        token[...] = jnp.zeros_like(token)

    lands = [lax.empty(t.shape, t.dtype) for t in land_shapes]
    thru = [pltpu.HBM(t.shape, t.dtype) for t in list(srcs) + list(land_shapes)]
    hbm = lambda t: pltpu.with_memory_space_constraint(t, pltpu.HBM)
    return pl.pallas_call(
        body, name=name, in_specs=[HBM] * (ns + nl),
        out_shape=(*thru, pltpu.SemaphoreType.DMA((nsem,)), pltpu.SemaphoreType.DMA((nsem,)),
                   jax.ShapeDtypeStruct((8, HD), f32)),
        out_specs=(*[HBM] * (ns + nl), SEM, SEM, pl.BlockSpec(memory_space=pltpu.VMEM)),
        input_output_aliases={i: i for i in range(ns + nl)},
        compiler_params=pltpu.CompilerParams(has_side_effects=DATAFLOW),
    )(*[hbm(t) for t in srcs], *[hbm(t) for t in lands])


def _split_wait(name, started, ns, after, finish):
    thru, send_sems, recv_sems = started[:-3], started[-3], started[-2]
    nt = len(thru)

    def body(*refs):
        finish(refs[:ns], refs[ns:nt], refs[nt], refs[nt + 1])

    outs = pl.pallas_call(
        body, name=name, in_specs=[HBM] * nt + [SEM, SEM, ANY],
        out_shape=tuple(pltpu.HBM(t.shape, t.dtype) for t in thru), out_specs=tuple([HBM] * nt),
        input_output_aliases={i: i for i in range(nt)},
        compiler_params=pltpu.CompilerParams(has_side_effects=DATAFLOW),
    )(*thru, send_sems, recv_sems, after)
    return list(outs[ns:])


def _gather_start(arrays, name):
    nm = len(CHIP_MASKS)

    def issue(srcs, lands, send_sems, recv_sems):
        me = _me()
        for a in range(len(arrays)):
            for mi, mask in enumerate(CHIP_MASKS):
                pltpu.make_async_remote_copy(
                    src_ref=srcs[a], dst_ref=lands[a].at[_chip(me)], send_sem=send_sems.at[a * nm + mi],
                    recv_sem=recv_sems.at[a * nm + mi], device_id=_flip(me, mask), device_id_type=MESH).start()

    shapes = [jax.ShapeDtypeStruct((4,) + t.shape, t.dtype) for t in arrays]
    return _split_start(name, arrays, shapes, len(arrays) * nm, issue)


def _gather_finish(started, arrays, after, name):
    nm = len(CHIP_MASKS)

    def finish(srcs, lands, send_sems, recv_sems):
        me = _me()
        for a in range(len(arrays)):
            for mi, mask in enumerate(CHIP_MASKS):
                peer = _flip(me, mask)
                cp = pltpu.make_async_remote_copy(
                    src_ref=srcs[a], dst_ref=lands[a].at[_chip(peer)], send_sem=send_sems.at[a * nm + mi],
                    recv_sem=recv_sems.at[a * nm + mi], device_id=peer, device_id_type=MESH)
                cp.wait_send()
                cp.wait_recv()

    got = _split_wait(name, started, len(arrays), after, finish)
    chip = 2 * lax.axis_index("x") + lax.axis_index("y")
    return [_put_own(g, t, chip) for g, t in zip(got, arrays)]


HALF_IN, HALF_OUT = D_MODEL // 2, D_MIX // 8


def _scatter_piece(kind, ref, d):
    j, r = d >> 1, d & 1
    if kind == "win":
        return ref.at[pl.ds(HALF_IN * r, HALF_IN), pl.ds(WIN_STRIDE * j, WIN_W)]
    if kind == "wout":
        return ref.at[pl.ds(2 * HALF_OUT * j + HALF_OUT * r, HALF_OUT)]
    return ref.at[d]


def _scatter_start(arrays, kinds, name):
    na = len(arrays)

    def issue(srcs, lands, send_sems, recv_sems):
        me = _me()
        my = _dev_index(me)
        for d in range(N_DEV):
            dev = (d >> 2, (d >> 1) & 1, d & 1)
            for a in range(na):
                @pl.when(my != d)
                def _(a=a, d=d, dev=dev):
                    pltpu.make_async_remote_copy(
                        src_ref=_scatter_piece(kinds[a], srcs[a], d), dst_ref=lands[a].at[my],
                        send_sem=send_sems.at[a * N_DEV + d], recv_sem=recv_sems.at[a * N_DEV + my],
                        device_id=dev, device_id_type=MESH).start()

    shape_of = {"win": (N_DEV, HALF_IN, WIN_W), "wout": (N_DEV, HALF_OUT, D_MODEL)}
    shapes = [jax.ShapeDtypeStruct(shape_of.get(k, t.shape), t.dtype) for t, k in zip(arrays, kinds)]
    return _split_start(name, arrays, shapes, na * N_DEV, issue)


def _scatter_finish(started, arrays, kinds, after, name):
    na = len(arrays)

    def finish(srcs, lands, send_sems, recv_sems):
        me = _me()
        my = _dev_index(me)
        for s in range(N_DEV):
            for a in range(na):
                @pl.when(my != s)
                def _(a=a, s=s):
                    cp = pltpu.make_async_remote_copy(
                        src_ref=_scatter_piece(kinds[a], srcs[a], s), dst_ref=lands[a].at[s],
                        send_sem=send_sems.at[a * N_DEV + s], recv_sem=recv_sems.at[a * N_DEV + s],
                        device_id=me, device_id_type=MESH)
                    cp.wait_recv()
                    cp.wait_send()

    got = _split_wait(name, started, na, after, finish)
    x, y, c = _me()
    chip, my = 2 * x + y, 4 * x + 2 * y + c
    own = {"win": lambda t: lax.dynamic_slice(t, (HALF_IN * c, WIN_STRIDE * chip), (HALF_IN, WIN_W)),
           "wout": lambda t: lax.dynamic_slice(t, (2 * HALF_OUT * chip + HALF_OUT * c, 0), (HALF_OUT, D_MODEL)),
           "small": lambda t: lax.dynamic_index_in_dim(t, my, 0, keepdims=False)}
    return [_put_own(g, own[k](t), my) for g, t, k in zip(got, arrays, kinds)]


def _share_reduced(pair_arrays, small, name):
    arrays = list(pair_arrays) + ([small] if small is not None else [])
    na, npair = len(arrays), len(pair_arrays)
    nm = len(ALL_MASKS)

    def body(*refs):
        srcs, dsts = refs[:na], refs[na:2 * na]
        send_sems, recv_sems = refs[2 * na:]
        me = _me()
        sib = _flip(me, (0, 0, 1))

        def copy(a, k, sender, to):
            slot = sender[2] if a < npair else _dev_index(sender)
            return pltpu.make_async_remote_copy(
                src_ref=srcs[a], dst_ref=dsts[a].at[slot], send_sem=send_sems.at[k], recv_sem=recv_sems.at[k],
                device_id=to, device_id_type=MESH)

        sends = [copy(a, a, me, sib) for a in range(npair)]
        if small is not None:
            sends += [copy(npair, npair + mi, me, _flip(me, mask)) for mi, mask in enumerate(ALL_MASKS)]
        for cp in sends:
            cp.start()
        for a in range(npair):
            copy(a, a, sib, me).wait_recv()
        if small is not None:
            for mi, mask in enumerate(ALL_MASKS):
                copy(npair, npair + mi, _flip(me, mask), me).wait_recv()
        for cp in sends:
            cp.wait_send()

    shapes = [jax.ShapeDtypeStruct((2,) + t.shape, t.dtype) for t in pair_arrays]
    if small is not None:
        shapes.append(jax.ShapeDtypeStruct((N_DEV,) + small.shape, small.dtype))
    got = _comm_call(name, body, arrays, shapes, npair + (nm if small is not None else 0))
    x, y, c = _me()
    slots = [c] * npair + [4 * x + 2 * y + c]
    return [_put_own(g, t, slot) for g, t, slot in zip(got, arrays, slots)]


def _sum_parts(parts, name):
    _, r, c = parts.shape
    tr = r
    while tr * c * 4 * N_DEV > (8 << 20) and tr % 32 == 0:
        tr //= 2

    def body(p_ref, o_ref):
        total = p_ref[0].astype(f32)
        for d in range(1, N_DEV):
            total = total + p_ref[d].astype(f32)
        o_ref[...] = total

    return pl.pallas_call(
        body, name=name, grid=(r // tr,), in_specs=[pl.BlockSpec((N_DEV, tr, c), lambda i: (0, i, 0))],
        out_specs=pl.BlockSpec((tr, c), lambda i: (i, 0)), out_shape=jax.ShapeDtypeStruct((r, c), f32),
        compiler_params=_params(("parallel",)),
    )(parts)


def _adamw(w, g, m, v, name):
    shape = w.shape
    cols = shape[-1]
    rows = w.size // cols
    tr = rows
    while tr * cols * 4 > (1 << 20) and tr % 16 == 0:
        tr //= 2
    c1 = 1.0 / (1.0 - ADAM_B1 ** ADAM_STEP)
    c2 = 1.0 / (1.0 - ADAM_B2 ** ADAM_STEP)

    def body(w_ref, g_ref, m_ref, v_ref, d_ref, nm_ref, nv_ref):
        gv = g_ref[...]
        mv = ADAM_B1 * m_ref[...] + (1.0 - ADAM_B1) * gv
        vv = ADAM_B2 * v_ref[...] + (1.0 - ADAM_B2) * (gv * gv)
        nm_ref[...] = mv
        nv_ref[...] = vv
        d_ref[...] = -ADAM_LR * ((mv * c1) / (jnp.sqrt(vv * c2) + ADAM_EPS) + ADAM_WD * w_ref[...])

    spec = pl.BlockSpec((tr, cols), lambda i: (i, 0))
    outs = pl.pallas_call(
        body, name=name, grid=(rows // tr,), in_specs=[spec] * 4, out_specs=[spec] * 3,
        out_shape=[jax.ShapeDtypeStruct((rows, cols), f32)] * 3, compiler_params=_params(("parallel",)),
    )(*[t.reshape(rows, cols) for t in (w, g, m, v)])
    return tuple(t.reshape(shape) for t in outs)


def _pad_lanes(t, n=HD):
    return jnp.pad(t, [(0, 0)] * (t.ndim - 1) + [(0, n - t.shape[-1])])


def _rows128(t, rows=None):
    t = t.reshape(-1, HD)
    rows = rows or -(-t.shape[0] // 8) * 8
    return jnp.pad(t, ((0, rows - t.shape[0]), (0, 0)))


def kernel(x, norm_w, w_in, lru_conv_w, lru_conv_b, lru_wa, lru_ba, lru_wx, lru_bx, lru_lambda, lru_norm_w, dn_conv_w, dn_A_log, dn_dt_bias, dn_norm_w, w_out, final_norm_w, loss_target, m_norm_w, m_w_in, m_lru_conv_w, m_lru_conv_b, m_lru_wa, m_lru_ba, m_lru_wx, m_lru_bx, m_lru_lambda, m_lru_norm_w, m_dn_conv_w, m_dn_A_log, m_dn_dt_bias, m_dn_norm_w, m_w_out, m_final_norm_w, v_norm_w, v_w_in, v_lru_conv_w, v_lru_conv_b, v_lru_wa, v_lru_ba, v_lru_wx, v_lru_bx, v_lru_lambda, v_lru_norm_w, v_dn_conv_w, v_dn_A_log, v_dn_dt_bias, v_dn_norm_w, v_w_out, v_final_norm_w):
    depth = norm_w.shape[0]
    xs = x[0]
    target = loss_target[0]
    chip = 2 * lax.axis_index("x") + lax.axis_index("y")

    win_b, wout_b = w_in.astype(bf16), w_out.astype(bf16)
    got0 = _gather_halves([win_b[0], wout_b[0]], [lru_conv_w, dn_conv_w], "gather_weights0")
    lcw_full = got0[2].transpose(1, 2, 0, 3).reshape(depth, 4, D_MODEL)
    dcw_full = got0[3].transpose(1, 2, 0, 3).reshape(depth, 4, 3 * D_MODEL)
    later = {l: _gather_start([win_b[l], wout_b[l]], f"gather_weights{l}_start") for l in range(1, depth)}

    def weights_of(l, after):
        if l == 0:
            got, token = got0, later[1][-1] if depth > 1 else None
        else:
            got, token = _gather_finish(later[l], [win_b[l], wout_b[l]], after, f"gather_weights{l}_wait"), None
        wp_l = _pad_lanes(jnp.concatenate([got[0][j] for j in range(4)], axis=1), D_INP)
        return wp_l, got[1].reshape(D_MIX, D_MODEL), token

    in_flight = {}

    def on_grad(kind, l, g):
        if l == depth - 1 and depth > 1:
            if kind == "wout":
                in_flight["held"] = g
                return None
            srcs, kinds, key = [g, in_flight.pop("held")], ["win", "wout"], ("both", l)
        else:
            srcs, kinds, key = [g], [kind], (kind, l)
        started = _scatter_start(srcs, kinds, f"scatter_{key[0]}{key[1]}_start")
        in_flight[key] = (started, srcs, kinds)
        return started[-1]

    vecs, dnvs = [], []
    zrow = jnp.zeros((D_MODEL,), f32)
    for l in range(depth):
        vecs.append(jnp.stack([lru_conv_b[l], lru_ba[l], lru_bx[l], lru_lambda[l], lru_norm_w[l], zrow, zrow, zrow]))
        dnvs.append(jnp.concatenate([_pad_lanes(dn_A_log[l][None]), _pad_lanes(dn_dt_bias[l][None]),
                                     dn_norm_w[l][None], jnp.zeros((5, HD), f32)], axis=0))

    loss_part, grad_x, d_fnw, g_nw, g_vec, g_wa, g_wx, g_lcw_, g_dcw_, g_dsc = _local_step(
        xs, target, norm_w, final_norm_w, weights_of, on_grad, lcw_full, dcw_full, vecs, dnvs, lru_wa, lru_wx)
    loss = lax.psum(loss_part[0, 0], ("x", "y", "c"))
    grads = _reduce_grads(chip, grad_x, in_flight, d_fnw, g_nw, g_vec, g_wa, g_wx, g_lcw_, g_dcw_, g_dsc)

    names = ["norm_w", "w_in", "lru_conv_w", "lru_conv_b", "lru_wa", "lru_ba", "lru_wx", "lru_bx", "lru_lambda",
             "lru_norm_w", "dn_conv_w", "dn_A_log", "dn_dt_bias", "dn_norm_w", "w_out", "final_norm_w"]
    ws = dict(zip(names, [norm_w, w_in, lru_conv_w, lru_conv_b, lru_wa, lru_ba, lru_wx, lru_bx, lru_lambda, lru_norm_w,
                          dn_conv_w, dn_A_log, dn_dt_bias, dn_norm_w, w_out, final_norm_w]))
    ms = dict(zip(names, [m_norm_w, m_w_in, m_lru_conv_w, m_lru_conv_b, m_lru_wa, m_lru_ba, m_lru_wx, m_lru_bx,
                          m_lru_lambda, m_lru_norm_w, m_dn_conv_w, m_dn_A_log, m_dn_dt_bias, m_dn_norm_w, m_w_out,
                          m_final_norm_w]))
    vs = dict(zip(names, [v_norm_w, v_w_in, v_lru_conv_w, v_lru_conv_b, v_lru_wa, v_lru_ba, v_lru_wx, v_lru_bx,
                          v_lru_lambda, v_lru_norm_w, v_dn_conv_w, v_dn_A_log, v_dn_dt_bias, v_dn_norm_w, v_w_out,
                          v_final_norm_w]))
    deltas, new_m, new_v = [], [], []
    for n in names:
        d, nm_, nv_ = _adamw(ws[n], grads[n], ms[n], vs[n], f"adamw_{n}")
        deltas.append(d)
        new_m.append(nm_)
        new_v.append(nv_)
    return (loss, grad_x[None], *[grads[n] for n in names], *deltas, *new_m, *new_v)


def _behind(t, token):
    return t if token is None else t + token[0:1, 0:1]


def _local_step(xs, target, norm_w, final_norm_w, weights_of, on_grad, lcw_full, dcw_full, vecs, dnvs, lru_wa, lru_wx):
    depth = norm_w.shape[0]
    saved = []
    h = xs
    for l in range(depth):
        wp_l, wo_l, token = weights_of(l, h)
        hn = _rms_fwd(h, _behind(norm_w[l][None], token), f"rms_fwd{l}")
        proj = _matmul(hn, wp_l, tm=2048, tn=896, tk=D_MODEL, name=f"in_proj{l}")
        y, hl = _lru_fwd(proj, lcw_full[l], vecs[l], lru_wa[l], lru_wx[l], f"lru_fwd{l}")
        y, o, states, tinvs = _dn_fwd(proj, dcw_full[l], dnvs[l], y, f"dn_fwd{l}")
        out = _matmul(y, wo_l, tm=512, tn=D_MODEL, tk=D_MIX, res=h, name=f"out_proj{l}")
        saved.append((h, hn, proj, hl, o, states, tinvs, y, wp_l, wo_l))
        h = out

    dh, d_fnw, loss_part = _final_loss(h, final_norm_w[None], target, "final_loss")

    g_nw, g_vec, g_wa, g_wx, g_lcw_, g_dcw_, g_dsc = ([None] * depth for _ in range(7))
    for l in reversed(range(depth)):
        hin, hn, proj, hl, o, states, tinvs, y, wp_l, wo_l = saved[l]
        dy = _matmul(dh, wo_l, tb=True, tm=512, tn=1024, tk=D_MODEL, name=f"d_y{l}")
        g_wout = _matmul(y, dh, ta=True, tm=512, tn=D_MODEL, tk=512, out_dtype=bf16, name=f"d_wout{l}")
        token = on_grad("wout", l, g_wout)
        dlx, dlz, g_lcw_[l], g_vec[l], g_wa[l], g_wx[l] = _lru_bwd(dy, proj, hl, lcw_full[l], _behind(vecs[l], token),
                                                                 lru_wa[l], lru_wx[l], f"lru_bwd{l}")
        dq, dk, dv, dz, dba, dcw8, g_dsc[l] = _dn_bwd(dy, proj, o, states, tinvs, dcw_full[l], dnvs[l], f"dn_bwd{l}")
        g_dcw_[l] = dcw8.reshape(HEADS, 4, 3, HD).transpose(1, 2, 0, 3).reshape(4, 3 * D_MODEL)
        dxs = [dlx, dlz, dq, dk, dv, dz]
        g_win = _d_win(_transpose(hn, f"hn_t{l}"), dxs, dba, f"d_win{l}")
        token = on_grad("win", l, g_win)
        dhn = _d_hn(dxs, _behind(dba, token), wp_l, f"d_hn{l}")
        dh, g_nw[l] = _rms_bwd(dhn, hin, norm_w[l][None], dh, f"rms_bwd{l}")
    return loss_part, dh, d_fnw, g_nw, g_vec, g_wa, g_wx, g_lcw_, g_dcw_, g_dsc


def _reduce_grads(chip, after, in_flight, d_fnw, g_nw, g_vec, g_wa, g_wx, g_lcw_, g_dcw_, g_dsc):
    depth = len(g_nw)
    both = lambda f: jnp.concatenate([f(l) for l in range(depth)])
    small = [
        both(lambda l: _rows128(g_nw[l])),
        both(lambda l: _rows128(g_vec[l][0])), both(lambda l: _rows128(g_vec[l][1])),
        both(lambda l: _rows128(g_vec[l][2])), both(lambda l: _rows128(g_vec[l][3])),
        both(lambda l: _rows128(g_vec[l][4])),
        both(lambda l: _rows128(g_wa[l])), both(lambda l: _rows128(g_wx[l])),
        both(lambda l: _rows128(g_dsc[l][0:1])), both(lambda l: _rows128(g_dsc[l][1:2])),
        both(lambda l: _rows128(g_dsc[l][2:3])),
        _rows128(d_fnw),
        both(lambda l: _rows128(g_lcw_[l])), both(lambda l: _rows128(g_dcw_[l])),
    ]
    counts = [t.shape[0] for t in small]
    total = sum(counts)
    per = -(-total // (8 * N_DEV)) * 8
    small = jnp.concatenate(small + [jnp.zeros((per * N_DEV - total, HD), f32)]).reshape(N_DEV, per, HD)

    small_started = _scatter_start([small], ["small"], "scatter_small_start")
    reduced = {}
    for key in sorted(in_flight, key=lambda k: -k[1]):
        started, srcs, kinds = in_flight[key]
        got = _scatter_finish(started, srcs, kinds, after, f"scatter_{key[0]}{key[1]}_wait")
        for kind, part in zip(kinds, got):
            reduced[(kind, key[1])] = _sum_parts(part, f"sum_{kind}{key[1]}")
    got = _scatter_finish(small_started, [small], ["small"], after, "scatter_small_wait")
    pairs = [None] * depth
    a_small = None
    for l in reversed(range(depth)):
        sums = [reduced[("win", l)], reduced[("wout", l)]]
        if l == 0:
            shared = _share_reduced(sums, _sum_parts(got[0], "sum_small"), f"share_grads{l}")
            a_small = shared[2].reshape(N_DEV * per, HD)
        else:
            shared = _share_reduced(sums, None, f"share_grads{l}")
        pairs[l] = shared
    grad_w_in = jnp.stack([lax.dynamic_slice_in_dim(pairs[l][0].reshape(D_MODEL, WIN_W), 4 * chip, SHARD_IN, 1)
                           for l in range(depth)])
    grad_w_out = jnp.stack([pairs[l][1].reshape(D_MIX // 4, D_MODEL) for l in range(depth)])

    offs = [0]
    for c in counts:
        offs.append(offs[-1] + c)

    def piece(k, rows_per_layer=None):
        t = a_small[offs[k]:offs[k + 1]]
        if rows_per_layer is None:
            return t
        return t.reshape(depth, -1, HD)[:, :rows_per_layer]

    vec = lambda k: piece(k).reshape(depth, D_MODEL)
    return {
        "norm_w": vec(0), "lru_conv_b": vec(1), "lru_ba": vec(2), "lru_bx": vec(3), "lru_lambda": vec(4),
        "lru_norm_w": vec(5),
        "lru_wa": piece(6).reshape(depth, HEADS, HD, HD), "lru_wx": piece(7).reshape(depth, HEADS, HD, HD),
        "dn_A_log": piece(8, 1)[:, 0, :HEADS], "dn_dt_bias": piece(9, 1)[:, 0, :HEADS], "dn_norm_w": piece(10, 1)[:, 0],
        "final_norm_w": piece(11).reshape(D_MODEL),
        "lru_conv_w": lax.dynamic_slice_in_dim(piece(12).reshape(depth, 4, D_MODEL), chip * (D_MODEL // 4), D_MODEL // 4, 2),
        "dn_conv_w": lax.dynamic_slice_in_dim(piece(13).reshape(depth, 4, 3 * D_MODEL), chip * (3 * D_MODEL // 4),
                                              3 * D_MODEL // 4, 2),
        "w_in": grad_w_in, "w_out": grad_w_out,
    }
```

```python
import jax
import jax.numpy as jnp
from jax import lax
from jax.experimental import pallas as pl
from jax.experimental.pallas import tpu as pltpu

f32 = jnp.float32
bf16 = jnp.bfloat16
HI = lax.Precision.HIGHEST
MESH = pl.DeviceIdType.MESH

D_MODEL = 1024
HEADS = 8
HD = 128
CHUNK = 64
LRU_T = 128
SEQ_BLOCK = 1024
LRU_HP = 4
DN_SEQ_BLOCK = 256
DN_HP = 8
LRU_C = 8.0
D_IN = 6160
D_INP = 6272
D_MIX = 2048
N_GROUPS = 6
BLK_BA = 48
SHARD_IN = D_IN // 4
WIN_W = 1664
WIN_STRIDE = 1536
EPS = 1e-6
QSCALE = HD ** -0.5
N_DEV = 8
VMEM_LIMIT = 56 * 1024 * 1024

ADAM_LR, ADAM_B1, ADAM_B2, ADAM_EPS, ADAM_WD, ADAM_STEP = 0.001, 0.9, 0.999, 1e-08, 0.01, 10


def _sig(x):
    return 1.0 / (1.0 + jnp.exp(-x))


def _log1p(u):
    w = 1.0 + u
    d = w - 1.0
    return jnp.where(d == 0.0, u, jnp.log(w) * (u / jnp.where(d == 0.0, 1.0, d)))


def _softplus(x):
    return jnp.maximum(x, 0.0) + _log1p(jnp.exp(-jnp.abs(x)))


def _expm1(x):
    t = x * (1.0 + x * (0.5 + x * (1.0 / 6 + x * (1.0 / 24 + x * (1.0 / 120 + x * (1.0 / 720))))))
    return jnp.where(jnp.abs(x) < 0.2, t, jnp.exp(x) - 1.0)


def _dot(a, b, prec=None):
    return jnp.dot(a, b, precision=prec, preferred_element_type=f32)


def _dot_nt(a, b, prec=None):
    return lax.dot_general(a, b, (((1,), (1,)), ((), ())), precision=prec, preferred_element_type=f32)


def _dot_tn(a, b, prec=None):
    return lax.dot_general(a, b, (((0,), (0,)), ((), ())), precision=prec, preferred_element_type=f32)


def _b(x):
    return x.astype(bf16)


def _sum0(x):
    return jnp.sum(x, axis=0, keepdims=True)


def _sum1(x):
    return jnp.sum(x, axis=1, keepdims=True)


def _rowmean(x):
    return jnp.mean(x, axis=-1, keepdims=True)


def _dsilu(z, sg):
    return sg * (1.0 + z * (1.0 - sg))


def _conv_taps(x, halo):
    xx = jnp.concatenate([halo, x], axis=0)
    return [pltpu.roll(xx, 3, axis=0)[8:], pltpu.roll(xx, 2, axis=0)[8:], pltpu.roll(xx, 1, axis=0)[8:], x]


def _conv(xs, cw):
    return cw[0:1] * xs[0] + cw[1:2] * xs[1] + cw[2:3] * xs[2] + cw[3:4] * xs[3]


def _rows_before(ref, halo_ref, lanes, t0, c, sblk):
    tprev = pl.multiple_of(jnp.maximum(t0 - 8, 0), 8)
    return jnp.where(c > 0, ref[pl.ds(tprev, 8), lanes], jnp.where(sblk > 0, halo_ref[:, lanes], 0.0))


def _conv_back(dxc, halo_after, cw):
    rows = dxc.shape[0]
    dd = jnp.concatenate([dxc, halo_after], axis=0)
    out = cw[3:4] * dxc
    for s in (1, 2, 3):
        out = out + cw[3 - s:4 - s] * pltpu.roll(dd, rows + 8 - s, axis=0)[:rows]
    return out


def _params(sem=None):
    return pltpu.CompilerParams(dimension_semantics=sem, vmem_limit_bytes=VMEM_LIMIT)


def _seq_specs(sb, width, rowblk, colblk):
    main = pl.BlockSpec((sb, width), lambda a, b: (rowblk(a, b), colblk(a, b)))
    halo = pl.BlockSpec((8, width), lambda a, b: (jnp.maximum(rowblk(a, b) * (sb // 8) - 1, 0), colblk(a, b)))
    return main, halo


def _matmul(a, b, *, ta=False, tb=False, tm, tn, tk, res=None, out_dtype=f32, name):
    if ta:
        kdim, m = a.shape
    else:
        m, kdim = a.shape
    n = b.shape[0] if tb else b.shape[1]
    tm, tn, tk = min(tm, m), min(tn, n), min(tk, kdim)
    assert m % tm == 0 and n % tn == 0 and kdim % tk == 0, (name, m, n, kdim, tm, tn, tk)
    nk = kdim // tk
    dims = (((0 if ta else 1,), (1 if tb else 0,)), ((), ()))
    has_res = res is not None

    def body(*refs):
        a_ref, b_ref = refs[:2]
        r_ref = refs[2] if has_res else None
        o_ref = refs[3] if has_res else refs[2]
        acc_ref = refs[-1] if nk > 1 else None
        part = lax.dot_general(_b(a_ref[...]), _b(b_ref[...]), dims, preferred_element_type=f32)

        def finish(total):
            if has_res:
                total = total + r_ref[...]
            o_ref[...] = total.astype(out_dtype)

        if nk == 1:
            finish(part)
        else:
            k = pl.program_id(2)

            @pl.when(k == 0)
            def _():
                acc_ref[...] = part

            @pl.when(k > 0)
            def _():
                acc_ref[...] += part

            @pl.when(k == nk - 1)
            def _():
                finish(acc_ref[...])

    a_spec = pl.BlockSpec((tk, tm), lambda i, j, k: (k, i)) if ta else pl.BlockSpec((tm, tk), lambda i, j, k: (i, k))
    b_spec = pl.BlockSpec((tn, tk), lambda i, j, k: (j, k)) if tb else pl.BlockSpec((tk, tn), lambda i, j, k: (k, j))
    o_spec = pl.BlockSpec((tm, tn), lambda i, j, k: (i, j))
    in_specs, args = [a_spec, b_spec], [a, b]
    if has_res:
        in_specs.append(o_spec)
        args.append(res)
    return pl.pallas_call(
        body, name=name, grid=(m // tm, n // tn, nk), in_specs=in_specs, out_specs=o_spec,
        out_shape=jax.ShapeDtypeStruct((m, n), out_dtype),
        scratch_shapes=[pltpu.VMEM((tm, tn), f32)] if nk > 1 else [],
        compiler_params=_params(("parallel", "parallel", "arbitrary")),
    )(*args)


def _transpose(x, name):
    r, c = x.shape
    tr, tc = min(512, r), min(512, c)

    def body(x_ref, o_ref):
        o_ref[...] = x_ref[...].T

    return pl.pallas_call(
        body, name=name, grid=(r // tr, c // tc), in_specs=[pl.BlockSpec((tr, tc), lambda i, j: (i, j))],
        out_specs=pl.BlockSpec((tc, tr), lambda i, j: (j, i)), out_shape=jax.ShapeDtypeStruct((c, r), x.dtype),
        compiler_params=_params(("parallel", "parallel")),
    )(x)


def _d_hn(dxs, dba, wp, name):
    s = dxs[0].shape[0]
    tm = min(1024, s)

    def body(*refs):
        dx_refs = refs[:N_GROUPS]
        dba_ref, w_ref, wba_ref, o_ref, acc_ref = refs[N_GROUPS:]
        k = pl.program_id(1)
        for g in range(N_GROUPS):
            @pl.when(k == g)
            def _(g=g):
                part = _dot_nt(_b(dx_refs[g][...]), w_ref[...])
                if g == 0:
                    acc_ref[...] = part + _dot_nt(_b(dba_ref[...]), wba_ref[...])
                else:
                    acc_ref[...] += part

        @pl.when(k == N_GROUPS - 1)
        def _():
            o_ref[...] = acc_ref[...]

    row = lambda w: pl.BlockSpec((tm, w), lambda i, k: (i, 0))
    return pl.pallas_call(
        body, name=name, grid=(s // tm, N_GROUPS),
        in_specs=[row(D_MODEL)] * N_GROUPS + [row(HD), pl.BlockSpec((D_MODEL, D_MODEL), lambda i, k: (0, k)),
                                              pl.BlockSpec((D_MODEL, HD), lambda i, k: (0, BLK_BA))],
        out_specs=row(D_MODEL), out_shape=jax.ShapeDtypeStruct((s, D_MODEL), f32),
        scratch_shapes=[pltpu.VMEM((tm, D_MODEL), f32)],
        compiler_params=_params(("parallel", "arbitrary")),
    )(*dxs, dba, wp, wp)


def _d_win(hnt, dxs, dba, name):
    s = hnt.shape[1]
    tn = 256
    per = D_MODEL // tn

    def body(*refs):
        hnt_ref = refs[0]
        dx_refs = refs[1:1 + N_GROUPS]
        o_ref = refs[1 + N_GROUPS]
        j = pl.program_id(0)
        for g in range(N_GROUPS):
            @pl.when(j // per == g)
            def _(g=g):
                o_ref[...] = _dot(hnt_ref[...], _b(dx_refs[g][...])).astype(bf16)

    def dx_spec(g):
        on = lambda j: (j // per == g).astype(jnp.int32)
        return pl.BlockSpec((s, tn), lambda j: (0, (j % per) * on(j)))

    main = pl.pallas_call(
        body, name=name, grid=(N_GROUPS * per,),
        in_specs=[pl.BlockSpec((D_MODEL, s), lambda j: (0, 0))] + [dx_spec(g) for g in range(N_GROUPS)],
        out_specs=pl.BlockSpec((D_MODEL, tn), lambda j: (0, j)),
        out_shape=jax.ShapeDtypeStruct((D_MODEL, D_INP), bf16),
        compiler_params=_params(("parallel",)),
    )(hnt, *dxs)

    def tail(hnt_ref, dba_ref, full_ref, o_ref):
        del full_ref
        o_ref[...] = _dot(hnt_ref[...], _b(dba_ref[...])).astype(bf16)

    return pl.pallas_call(
        tail, name=name + "_ba", grid=(1,),
        in_specs=[pl.BlockSpec((D_MODEL, s), lambda k: (0, 0)), pl.BlockSpec((s, HD), lambda k: (0, 0)),
                  pl.BlockSpec(memory_space=pl.ANY)],
        out_specs=pl.BlockSpec((D_MODEL, HD), lambda k: (0, BLK_BA)),
        out_shape=jax.ShapeDtypeStruct((D_MODEL, D_INP), bf16),
        input_output_aliases={2: 0},
        compiler_params=_params(("arbitrary",)),
    )(hnt, dba, main)


def _rms_fwd(x, w, name):
    s = x.shape[0]
    tr = min(512, s)

    def body(x_ref, w_ref, h_ref):
        xv = x_ref[...]
        r = lax.rsqrt(_rowmean(xv * xv) + EPS)
        h_ref[...] = (xv * r * w_ref[...]).astype(bf16)

    return pl.pallas_call(
        body, name=name, grid=(s // tr,),
        in_specs=[pl.BlockSpec((tr, D_MODEL), lambda i: (i, 0)), pl.BlockSpec((1, D_MODEL), lambda i: (0, 0))],
        out_specs=pl.BlockSpec((tr, D_MODEL), lambda i: (i, 0)),
        out_shape=jax.ShapeDtypeStruct((s, D_MODEL), bf16), compiler_params=_params(("parallel",)),
    )(x, w)


def _rms_bwd(dh, x, w, dres, name):
    s = x.shape[0]
    tr = min(512, s)

    def body(dh_ref, x_ref, w_ref, dres_ref, dx_ref, dw_ref):
        xv = x_ref[...]
        r = lax.rsqrt(_rowmean(xv * xv) + EPS)
        xn = xv * r
        d = dh_ref[...]
        dxn = d * w_ref[...]
        dx_ref[...] = dres_ref[...] + r * (dxn - xn * _rowmean(dxn * xn))
        part = _sum0(d * xn)

        @pl.when(pl.program_id(0) == 0)
        def _():
            dw_ref[...] = part

        @pl.when(pl.program_id(0) > 0)
        def _():
            dw_ref[...] += part

    row = pl.BlockSpec((tr, D_MODEL), lambda i: (i, 0))
    vec = pl.BlockSpec((1, D_MODEL), lambda i: (0, 0))
    return pl.pallas_call(
        body, name=name, grid=(s // tr,), in_specs=[row, row, vec, row], out_specs=[row, vec],
        out_shape=[jax.ShapeDtypeStruct((s, D_MODEL), f32), jax.ShapeDtypeStruct((1, D_MODEL), f32)],
        compiler_params=_params(("arbitrary",)),
    )(dh, x, w, dres)


def _final_loss(x, w, target, name):
    s = x.shape[0]
    tr = min(512, s)

    def body(x_ref, w_ref, t_ref, dx_ref, dw_ref, loss_ref):
        xv = x_ref[...]
        wv = w_ref[...]
        r = lax.rsqrt(_rowmean(xv * xv) + EPS)
        xn = xv * r
        e = xn * wv - t_ref[...]
        lb = jnp.broadcast_to(0.5 * _sum0(_rowmean(e * e)), (1, HD))
        dy = e * (1.0 / D_MODEL)
        dxn = dy * wv
        dx_ref[...] = r * (dxn - xn * _rowmean(dxn * xn))
        part = _sum0(dy * xn)

        @pl.when(pl.program_id(0) == 0)
        def _():
            dw_ref[...] = part
            loss_ref[...] = lb

        @pl.when(pl.program_id(0) > 0)
        def _():
            dw_ref[...] += part
            loss_ref[...] += lb

    row = pl.BlockSpec((tr, D_MODEL), lambda i: (i, 0))
    vec = pl.BlockSpec((1, D_MODEL), lambda i: (0, 0))
    lsp = pl.BlockSpec((1, HD), lambda i: (0, 0))
    return pl.pallas_call(
        body, name=name, grid=(s // tr,), in_specs=[row, vec, row], out_specs=[row, vec, lsp],
        out_shape=[jax.ShapeDtypeStruct((s, D_MODEL), f32), jax.ShapeDtypeStruct((1, D_MODEL), f32),
                   jax.ShapeDtypeStruct((1, HD), f32)],
        compiler_params=_params(("arbitrary",)),
    )(x, w, target)


ALL = slice(None)


def _lru_gates(xc, vec, wa, wx):
    sp = _softplus(-vec[3:4])
    xcb = _b(xc)
    r = _sig(_dot(xcb, wa) + vec[1:2])
    i = _sig(_dot(xcb, wx) + vec[2:3])
    la = (-LRU_C) * r * sp
    a = jnp.exp(la)
    mult = jnp.sqrt(-_expm1(2.0 * la))
    return r, i, a, mult, sp, xcb


def _lru_fwd(proj, cw, vec, wa, wx, name):
    s = proj.shape[0]
    sb = min(SEQ_BLOCK, s)
    nsb = s // sb
    t = min(LRU_T, sb)
    nc = sb // t
    hp = LRU_HP
    wide = hp * HD
    lanes = [slice(j * HD, (j + 1) * HD) for j in range(hp)]

    def body(x_ref, xh_ref, z_ref, cw_ref, vec_ref, wa_ref, wx_ref, y_ref, hl_ref, hc_ref):
        sblk = pl.program_id(1)
        consts = [(cw_ref[:, hl], vec_ref[:, hl], _b(wa_ref[j]), _b(wx_ref[j])) for j, hl in enumerate(lanes)]
        row = lax.broadcasted_iota(jnp.int32, (t, HD), 0)

        @pl.when(sblk == 0)
        def _():
            hc_ref[...] = jnp.zeros((8, wide), f32)

        def one_head(j, x, halo, z, hcarry):
            cwv, vec_v, wa_v, wx_v = consts[j]
            xs = _conv_taps(x, halo)
            xc = vec_v[0:1] + _conv(xs, cwv)
            r, i, a, mult, _, _ = _lru_gates(xc, vec_v, wa_v, wx_v)
            yield
            bb = mult * (i * xc)
            d = 1
            while d < t:
                m = row >= d
                bb = jnp.where(m, a * pltpu.roll(bb, d, axis=0) + bb, bb)
                a = jnp.where(m, a * pltpu.roll(a, d, axis=0), a)
                d *= 2
                yield
            hl = bb + a * hcarry
            nrm = lax.rsqrt(_rowmean(hl * hl) + EPS)
            return hl, (hl * nrm * vec_v[4:5] * (z * _sig(z))).astype(bf16)

        def chunk(c, carries):
            t0 = pl.multiple_of(c * t, t)
            rows = pl.ds(t0, t)
            ins = [(x_ref[rows, hl], _rows_before(x_ref, xh_ref, hl, t0, c, sblk), z_ref[rows, hl]) for hl in lanes]
            outs = _round_robin([one_head(j, *ins[j], carries[j]) for j in range(hp)])
            for j, hl in enumerate(lanes):
                hl_ref[rows, hl] = outs[j][0]
                y_ref[rows, hl] = outs[j][1]
            return tuple(out[0][t - 1:t, :] for out in outs)

        final = lax.fori_loop(0, nc, chunk, tuple(hc_ref[0:1, hl] for hl in lanes))
        hc_ref[...] = jnp.concatenate([jnp.broadcast_to(f, (8, HD)) for f in final], axis=1)

    xsp, xhalo = _seq_specs(sb, wide, lambda g, i: i, lambda g, i: g)
    zsp, _ = _seq_specs(sb, wide, lambda g, i: i, lambda g, i: HEADS // hp + g)
    gcol = lambda g, i: (0, g)
    wsp = pl.BlockSpec((hp, HD, HD), lambda g, i: (g, 0, 0))
    osp = pl.BlockSpec((sb, wide), lambda g, i: (i, g))
    return pl.pallas_call(
        body, name=name, grid=(HEADS // hp, nsb),
        in_specs=[xsp, xhalo, zsp, pl.BlockSpec((4, wide), gcol), pl.BlockSpec((8, wide), gcol), wsp, wsp],
        out_specs=[osp, osp],
        out_shape=[jax.ShapeDtypeStruct((s, D_MIX), bf16), jax.ShapeDtypeStruct((s, D_MODEL), f32)],
        scratch_shapes=[pltpu.VMEM((8, wide), f32)],
        compiler_params=_params(("parallel", "arbitrary")),
    )(proj, proj, proj, cw, vec, wa, wx)


def _lru_bwd(dy, proj, hl, cw, vec, wa, wx, name):
    s = proj.shape[0]
    sb = min(SEQ_BLOCK, s)
    nsb = s // sb
    t = min(LRU_T, sb)
    nc = sb // t
    hp = LRU_HP
    wide = hp * HD
    lanes = [slice(j * HD, (j + 1) * HD) for j in range(hp)]
    n_acc = 10

    def body(dy_ref, x_ref, xh_ref, z_ref, hl_ref, hlh_ref, cw_ref, vec_ref, wa_ref, wx_ref,
             dx_ref, dz_ref, dcw_ref, dvec_ref, dwa_ref, dwx_ref, acc_ref, dxh_ref):
        step = pl.program_id(1)
        sblk = nsb - 1 - step
        consts = [(cw_ref[:, hl], vec_ref[:, hl], _b(wa_ref[j]), _b(wx_ref[j])) for j, hl in enumerate(lanes)]
        row = lax.broadcasted_iota(jnp.int32, (t, HD), 0)

        @pl.when(step == 0)
        def _():
            acc_ref[...] = jnp.zeros((16, wide), f32)
            dxh_ref[...] = jnp.zeros((8, wide), f32)
            dwa_ref[...] = jnp.zeros((hp, HD, HD), f32)
            dwx_ref[...] = jnp.zeros((hp, HD, HD), f32)

        def one_head(j, x, halo, z, hv, hhalo, dyv, carry):
            dcw0, dcw1, dcw2, dcw3, dcb, dba, dbx, dsp, dlnw, mu, dxc_halo = carry
            cwv, vec_v, wa_v, wx_v = consts[j]
            lnw = vec_v[4:5]
            xs = _conv_taps(x, halo)
            xc = vec_v[0:1] + _conv(xs, cwv)
            r, i, a, mult, sp, xcb = _lru_gates(xc, vec_v, wa_v, wx_v)
            yield
            hprev = pltpu.roll(jnp.concatenate([hhalo, hv], axis=0), 1, axis=0)[8:]
            sgz = _sig(z)
            sz = z * sgz
            nrm = lax.rsqrt(_rowmean(hv * hv) + EPS)
            hn = hv * nrm
            dlnw = dlnw + _sum0(dyv * hn * sz)
            dzv = _b(dyv * hn * lnw * _dsilu(z, sgz))
            dhn = dyv * lnw * sz
            lam = nrm * (dhn - hn * _rowmean(dhn * hn))
            cf = jnp.where(row == t - 1, 1.0, pltpu.roll(a, t - 1, axis=0))
            d = 1
            while d < t:
                m = row < t - d
                lam = jnp.where(m, lam + cf * pltpu.roll(lam, t - d, axis=0), lam)
                cf = jnp.where(m, cf * pltpu.roll(cf, t - d, axis=0), cf)
                d *= 2
                yield
            lam = lam + cf * mu
            mu = a[0:1] * lam[0:1]
            da = lam * hprev
            dmult = lam * (i * xc)
            di = lam * mult * xc
            dxc = lam * mult * i
            dla = da * a - dmult * (a * a) / mult
            dr = dla * (-LRU_C * sp)
            dsp = dsp + _sum0(dla * (-LRU_C * r))
            dra = dr * r * (1.0 - r)
            dia = di * i * (1.0 - i)
            dba = dba + _sum0(dra)
            dbx = dbx + _sum0(dia)
            drab, diab = _b(dra), _b(dia)
            dwa = _dot_tn(xcb, drab)
            dwx = _dot_tn(xcb, diab)
            dxc = dxc + _dot_nt(drab, wa_v) + _dot_nt(diab, wx_v)
            yield
            dcb = dcb + _sum0(dxc)
            dcw0 = dcw0 + _sum0(dxc * xs[0])
            dcw1 = dcw1 + _sum0(dxc * xs[1])
            dcw2 = dcw2 + _sum0(dxc * xs[2])
            dcw3 = dcw3 + _sum0(dxc * xs[3])
            dxv = _b(_conv_back(dxc, dxc_halo, cwv))
            return dxv, dzv, dwa, dwx, (dcw0, dcw1, dcw2, dcw3, dcb, dba, dbx, dsp, dlnw, mu, dxc[0:8])

        def chunk(ci, carries):
            c = nc - 1 - ci
            t0 = pl.multiple_of(c * t, t)
            rows = pl.ds(t0, t)
            ins = [(x_ref[rows, hl], _rows_before(x_ref, xh_ref, hl, t0, c, sblk), z_ref[rows, hl], hl_ref[rows, hl],
                    _rows_before(hl_ref, hlh_ref, hl, t0, c, sblk), dy_ref[rows, hl]) for hl in lanes]
            outs = _round_robin([one_head(j, *ins[j], carries[j]) for j in range(hp)])
            for j, hl in enumerate(lanes):
                dx_ref[rows, hl] = outs[j][0]
                dz_ref[rows, hl] = outs[j][1]
                dwa_ref[j] += outs[j][2]
                dwx_ref[j] += outs[j][3]
            return tuple(out[4] for out in outs)

        acc = acc_ref[...]
        dxh = dxh_ref[...]
        init = tuple(tuple(acc[k:k + 1, hl] for k in range(n_acc)) + (dxh[:, hl],) for hl in lanes)
        final = lax.fori_loop(0, nc, chunk, init)
        rows_out = [jnp.concatenate([final[j][k] for j in range(hp)], axis=1) for k in range(n_acc)]
        zero = jnp.zeros((1, wide), f32)
        acc_ref[...] = jnp.concatenate(rows_out + [zero] * (16 - n_acc), axis=0)
        dxh_ref[...] = jnp.concatenate([final[j][n_acc] for j in range(hp)], axis=1)

        @pl.when(step == nsb - 1)
        def _():
            dcw_ref[...] = jnp.concatenate(rows_out[0:4], axis=0)
            dlam = -rows_out[7] * _sig(-vec_ref[3:4, :])
            dvec_ref[...] = jnp.concatenate([rows_out[4], rows_out[5], rows_out[6], dlam, rows_out[8], zero, zero, zero],
                                            axis=0)

    rblk = lambda g, i: nsb - 1 - i
    xsp, xhalo = _seq_specs(sb, wide, rblk, lambda g, i: g)
    zsp, _ = _seq_specs(sb, wide, rblk, lambda g, i: HEADS // hp + g)
    gcol = lambda g, i: (0, g)
    wsp = pl.BlockSpec((hp, HD, HD), lambda g, i: (g, 0, 0))
    return pl.pallas_call(
        body, name=name, grid=(HEADS // hp, nsb),
        in_specs=[xsp, xsp, xhalo, zsp, xsp, xhalo, pl.BlockSpec((4, wide), gcol), pl.BlockSpec((8, wide), gcol), wsp, wsp],
        out_specs=[xsp, xsp, pl.BlockSpec((4, wide), gcol), pl.BlockSpec((8, wide), gcol), wsp, wsp],
        out_shape=[jax.ShapeDtypeStruct((s, D_MODEL), bf16), jax.ShapeDtypeStruct((s, D_MODEL), bf16),
                   jax.ShapeDtypeStruct((4, D_MODEL), f32), jax.ShapeDtypeStruct((8, D_MODEL), f32),
                   jax.ShapeDtypeStruct((HEADS, HD, HD), f32), jax.ShapeDtypeStruct((HEADS, HD, HD), f32)],
        scratch_shapes=[pltpu.VMEM((16, wide), f32), pltpu.VMEM((8, wide), f32)],
        compiler_params=_params(("parallel", "arbitrary")),
    )(dy, proj, proj, proj, hl, hl, cw, vec, wa, wx)


def _lane_pick(x, lane, idx):
    return _sum1(jnp.where(lane == idx, x, 0.0))


def _round_robin(gens):
    results = [None] * len(gens)
    live = list(range(len(gens)))
    while live:
        for i in list(live):
            try:
                next(gens[i])
            except StopIteration as done:
                results[i] = done.value
                live.remove(i)
    return results


def _sdot(a, b):
    return _dot(_b(a), _b(b))


def _sdot_tn(a, b):
    return _dot_tn(_b(a), _b(b))


def _sdot_nt(a, b):
    return _dot_nt(_b(a), _b(b))


def _tri_dot(tri, x):
    trib = tri.astype(bf16)
    x1 = x.astype(bf16)
    r1 = x - x1.astype(f32)
    x2 = r1.astype(bf16)
    x3 = (r1 - x2.astype(f32)).astype(bf16)
    return _dot(trib, x1) + _dot(trib, x2) + _dot(trib, x3)


def _inv_unit_lower(a):
    n = -a
    p = _sdot(a, a)
    yield
    for k in range(5):
        n = n + p + _sdot(n, p)
        if k < 4:
            p = _sdot(p, p)
        yield
    return n


def _dn_chunk(x3, halo3, bav, cw3, dnv, h, masks, tinv=None):
    lane, ri, ci, eye, ltri = masks
    xs = _conv_taps(x3, halo3)
    cpre = _conv(xs, cw3)
    sg = _sig(cpre)
    act = cpre * sg
    q_raw, k_raw, v = act[:, 0:HD], act[:, HD:2 * HD], act[:, 2 * HD:3 * HD]
    rq = lax.rsqrt(_sum1(q_raw * q_raw) + EPS)
    rk = lax.rsqrt(_sum1(k_raw * k_raw) + EPS)
    qn = q_raw * rq
    k = k_raw * rk
    q = qn * QSCALE
    b_in = jnp.broadcast_to(_lane_pick(bav, lane, h), (CHUNK, HD))
    a_in = jnp.broadcast_to(_lane_pick(bav, lane, HEADS + h), (CHUNK, HD))
    lane1 = lane[0:1]
    ea = jnp.exp(_lane_pick(dnv[0:1], lane1, h))
    dt = _lane_pick(dnv[1:2], lane1, h)
    beta = _sig(b_in)
    sp = _softplus(a_in + dt)
    g = -ea * sp
    gc = _tri_dot(ltri, g)
    yield
    gc64 = gc[:, 0:CHUNK]
    grow = _sum0(gc64 * eye)
    dm = jnp.exp(jnp.where(ri >= ci, gc64 - grow, -1e30))
    ds_ = jnp.where(ri > ci, dm, 0.0)
    eg = jnp.exp(gc)
    glast = gc[CHUNK - 1:CHUNK, :]
    ek = jnp.exp(glast - gc)
    kb = k * beta
    kbf = _b(k)
    amat = _dot_nt(_b(kb), kbf) * ds_
    pmat = _dot_nt(_b(q), kbf) * dm
    yield
    if tinv is None:
        tinv = yield from _inv_unit_lower(amat)
    return dict(xs=xs, cpre=cpre, sg=sg, rq=rq, rk=rk, qn=qn, q=q, k=k, v=v, kbf=kbf, b_in=b_in, a_in=a_in, ea=ea,
                dt=dt, beta=beta, g=g, gc=gc, dm=dm, ds=ds_, eg=eg, glast=glast, ek=ek, kb=kb, amat=amat,
                tinv=tinv, pmat=pmat)


def _dn_masks():
    lane = lax.broadcasted_iota(jnp.int32, (CHUNK, HD), 1)
    ri = lax.broadcasted_iota(jnp.int32, (CHUNK, CHUNK), 0)
    ci = lax.broadcasted_iota(jnp.int32, (CHUNK, CHUNK), 1)
    eye = (ri == ci).astype(f32)
    ltri = (ri >= ci).astype(f32)
    return lane, ri, ci, eye, ltri


def _dn_load_qkv(q_ref, qh_ref, k_ref, kh_ref, v_ref, vh_ref, hl, rows, t0, c, sblk):
    x3 = jnp.concatenate([q_ref[rows, hl], k_ref[rows, hl], v_ref[rows, hl]], axis=1)
    halo3 = jnp.concatenate([_rows_before(q_ref, qh_ref, hl, t0, c, sblk), _rows_before(k_ref, kh_ref, hl, t0, c, sblk),
                             _rows_before(v_ref, vh_ref, hl, t0, c, sblk)], axis=1)
    return x3, halo3


def _dn_cw3(cwq_ref, cwk_ref, cwv_ref, j):
    return jnp.concatenate([r[:, j * HD:(j + 1) * HD] for r in (cwq_ref, cwk_ref, cwv_ref)], axis=1)


def _dn_fwd(proj, cw, dnv, y, name):
    s = proj.shape[0]
    sb = min(DN_SEQ_BLOCK, s)
    nsb = s // sb
    nc = sb // CHUNK
    hp = DN_HP

    def body(q_ref, qh_ref, k_ref, kh_ref, v_ref, vh_ref, z_ref, ba_ref, cwq_ref, cwk_ref, cwv_ref, dnv_ref, yin_ref,
             y_ref, o_ref, st_ref, ti_ref, s_ref):
        del yin_ref
        grp = pl.program_id(0)
        sblk = pl.program_id(1)
        masks = _dn_masks()
        dnv_v = dnv_ref[...]
        cw3 = [_dn_cw3(cwq_ref, cwk_ref, cwv_ref, j) for j in range(hp)]

        @pl.when(sblk == 0)
        def _():
            s_ref[...] = jnp.zeros((hp, HD, HD), f32)

        def one_head(j, x3, halo3, bav, z, st):
            f = yield from _dn_chunk(x3, halo3, bav, cw3[j], dnv_v, grp * hp + j, masks)
            sbf = _b(st)
            qs = _dot(_b(f["q"] * f["eg"]), sbf)
            rhs = f["beta"] * (f["v"] - _dot(_b(f["k"] * f["eg"]), sbf))
            yield
            vn = rhs + _sdot(f["tinv"], rhs)
            yield
            vnb = _b(vn)
            o = qs + _dot(_b(f["pmat"]), vnb)
            st_new = st * jnp.exp(f["glast"]) + _dot_tn(_b(f["k"] * f["ek"]), vnb)
            yield
            nrm = lax.rsqrt(_rowmean(o * o) + EPS)
            yv = (o * nrm * dnv_v[2:3] * (z * _sig(z))).astype(bf16)
            return o, yv, st_new, f["tinv"]

        def chunk(c, states):
            t0 = pl.multiple_of(c * CHUNK, CHUNK)
            rows = pl.ds(t0, CHUNK)
            bav = ba_ref[rows, :]
            ins = []
            for j in range(hp):
                hl = slice(j * HD, (j + 1) * HD)
                ins.append(_dn_load_qkv(q_ref, qh_ref, k_ref, kh_ref, v_ref, vh_ref, hl, rows, t0, c, sblk)
                           + (bav, z_ref[rows, hl]))
            outs = _round_robin([one_head(j, *ins[j], states[j]) for j in range(hp)])
            for j in range(hp):
                hl = slice(j * HD, (j + 1) * HD)
                st_ref[j, c] = states[j]
                o_ref[rows, hl] = outs[j][0]
                y_ref[rows, hl] = outs[j][1]
                ti_ref[j, c] = outs[j][3]
            return tuple(out[2] for out in outs)

        final = lax.fori_loop(0, nc, chunk, tuple(s_ref[j] for j in range(hp)))
        for j in range(hp):
            s_ref[j] = final[j]

    wide = hp * HD
    grp_specs = lambda off: _seq_specs(sb, wide, lambda g, i: i, lambda g, i: off // hp + g)
    (qsp, qhalo), (ksp, khalo), (vsp, vhalo) = grp_specs(2 * HEADS), grp_specs(3 * HEADS), grp_specs(4 * HEADS)
    zsp, _ = grp_specs(5 * HEADS)
    basp = pl.BlockSpec((sb, HD), lambda g, i: (i, BLK_BA))
    cws = lambda off: pl.BlockSpec((4, wide), lambda g, i: (0, off // hp + g))
    osp = lambda off: pl.BlockSpec((sb, wide), lambda g, i: (i, off // hp + g))
    return pl.pallas_call(
        body, name=name, grid=(HEADS // hp, nsb),
        in_specs=[qsp, qhalo, ksp, khalo, vsp, vhalo, zsp, basp, cws(0), cws(HEADS), cws(2 * HEADS),
                  pl.BlockSpec((8, HD), lambda g, i: (0, 0)), pl.BlockSpec(memory_space=pl.ANY)],
        out_specs=[osp(HEADS), osp(0), pl.BlockSpec((hp, nc, HD, HD), lambda g, i: (g, i, 0, 0)),
                   pl.BlockSpec((hp, nc, CHUNK, CHUNK), lambda g, i: (g, i, 0, 0))],
        out_shape=[jax.ShapeDtypeStruct((s, D_MIX), bf16), jax.ShapeDtypeStruct((s, D_MODEL), f32),
                   jax.ShapeDtypeStruct((HEADS, s // CHUNK, HD, HD), f32),
                   jax.ShapeDtypeStruct((HEADS, s // CHUNK, CHUNK, CHUNK), f32)],
        scratch_shapes=[pltpu.VMEM((hp, HD, HD), f32)],
        input_output_aliases={12: 0},
        compiler_params=_params(("parallel", "arbitrary")),
    )(proj, proj, proj, proj, proj, proj, proj, proj, cw, cw, cw, dnv, y)


def _dn_bwd(dy, proj, o, states, tinvs, cw, dnv, name):
    s = proj.shape[0]
    sb = min(DN_SEQ_BLOCK, s)
    nsb = s // sb
    nc = sb // CHUNK
    hp = DN_HP
    ngrp = HEADS // hp

    def body(dy_ref, q_ref, qh_ref, k_ref, kh_ref, v_ref, vh_ref, z_ref, ba_ref, o_ref, st_ref, ti_ref,
             cwq_ref, cwk_ref, cwv_ref, dnv_ref,
             dq_ref, dk_ref, dv_ref, dz_ref, dba_ref, dcw_ref, dsc_ref,
             ds_ref, dch_ref, cwacc_ref, sacc_ref, nacc_ref):
        step = pl.program_id(0)
        grp = pl.program_id(1)
        sblk = nsb - 1 - step
        h0 = grp * hp
        masks = _dn_masks()
        lane, ri, ci, eye, ltri = masks
        utri = (ri <= ci).astype(f32)
        cw3s = [_dn_cw3(cwq_ref, cwk_ref, cwv_ref, j) for j in range(hp)]
        dnv_v = dnv_ref[...]
        dnw = dnv_v[2:3]
        row64 = lax.broadcasted_iota(jnp.int32, (CHUNK, HD), 0)

        @pl.when(step == 0)
        def _():
            for j in range(hp):
                ds_ref[h0 + j] = jnp.zeros((HD, HD), f32)
                dch_ref[h0 + j] = jnp.zeros((8, 3 * HD), f32)
                cwacc_ref[h0 + j] = jnp.zeros((8, 3 * HD), f32)
                sacc_ref[h0 + j] = jnp.zeros((8, HD), f32)

        @pl.when((step == 0) & (grp == 0))
        def _():
            nacc_ref[...] = jnp.zeros((8, HD), f32)

        def one_head(j, x3, halo3, bav, z, st, ti, ov, dyv, carry):
            dst, dch, cwacc, sa0, sa1 = carry
            hd = h0 + j
            cw3 = cw3s[j]
            f = yield from _dn_chunk(x3, halo3, bav, cw3, dnv_v, hd, masks, ti)
            q, k, v, beta, eg, ek, kbf = f["q"], f["k"], f["v"], f["beta"], f["eg"], f["ek"], f["kbf"]
            sbf = _b(st)
            dstb = _b(dst)
            qe = q * eg
            keg = k * eg
            kd = k * ek
            kegb, qeb, kdb = _b(keg), _b(qe), _b(kd)
            e = v - _dot(kegb, sbf)
            yield
            rhs = beta * e
            vn = rhs + _sdot(f["tinv"], rhs)
            yield
            vnb = _b(vn)
            pb = _b(f["pmat"])
            sgz = _sig(z)
            sz = z * sgz
            nrm = lax.rsqrt(_rowmean(ov * ov) + EPS)
            on = ov * nrm
            ddnw = _sum0(dyv * on * sz)
            dpz = dyv * on * dnw * _dsilu(z, sgz)
            don = dyv * dnw * sz
            do = nrm * (don - on * _rowmean(don * on))
            dob = _b(do)
            dvn = _dot_tn(pb, dob) + _dot(kdb, dstb)
            dpm = jnp.where(ri >= ci, _dot_nt(dob, vnb), 0.0)
            dqe = _dot_nt(dob, sbf)
            dkd = _dot_nt(vnb, dstb)
            qtdo = _dot_tn(qeb, dob)
            yield
            drhs = dvn + _sdot_tn(f["tinv"], dvn)
            dqk = _b(dpm * f["dm"])
            dq = _dot(dqk, kbf) + dqe * eg
            dk_p = _dot_tn(dqk, _b(q))
            yield
            dam = jnp.where(ri > ci, -_sdot_nt(drhs, vn), 0.0)
            de = drhs * beta
            deb = _b(de)
            dbeta = _sum1(drhs * e)
            dkeg = -_dot_nt(deb, sbf)
            dst_new = qtdo + dst * jnp.exp(f["glast"]) - _dot_tn(kegb, deb)
            yield
            dkk = _b(dam * f["ds"])
            dkb = _dot(dkk, kbf)
            dk = dk_p + _dot_tn(dkk, _b(f["kb"])) + dkb * beta + dkeg * eg + dkd * ek
            yield
            dbeta = dbeta + _sum1(dkb * k)
            mm = dpm * f["pmat"] + dam * f["amat"]
            dglast = _sum1(_sum0(dst * st)) * jnp.exp(f["glast"]) + _sum1(_sum0(dkd * kd))
            dgc = (_sum1(mm) - _sum1(eye * _sum0(mm)) + _sum1(dqe * qe) + _sum1(dkeg * keg) - _sum1(dkd * kd))
            dgc = jnp.broadcast_to(dgc, (CHUNK, HD)) + jnp.where(row64 == CHUNK - 1, dglast, 0.0)
            dg = _tri_dot(utri, dgc)
            yield
            dbin = jnp.broadcast_to(dbeta, (CHUNK, HD)) * beta * (1.0 - beta)
            dain = dg * (-f["ea"]) * _sig(f["a_in"] + f["dt"])
            dba = jnp.where(lane == hd, dbin, 0.0) + jnp.where(lane == HEADS + hd, dain, 0.0)
            sa0 = sa0 + _sum0(dg * f["g"])
            sa1 = sa1 + _sum0(dain)
            dq_raw = (QSCALE * f["rq"]) * (dq - f["qn"] * _sum1(f["qn"] * dq))
            dk_raw = f["rk"] * (dk - k * _sum1(k * dk))
            dact = jnp.concatenate([dq_raw, dk_raw, de], axis=1)
            dc = dact * _dsilu(f["cpre"], f["sg"])
            xs = f["xs"]
            cwacc = cwacc + jnp.concatenate([_sum0(dc * xs[0]), _sum0(dc * xs[1]), _sum0(dc * xs[2]),
                                             _sum0(dc * xs[3])], axis=0)
            dqkv = _conv_back(dc, dch, cw3)
            return dpz, dqkv, dba, ddnw, (dst_new, dc[0:8], cwacc, sa0, sa1)

        def chunk(cidx, carry):
            heads, nacc = carry
            c = nc - 1 - cidx
            t0 = pl.multiple_of(c * CHUNK, CHUNK)
            rows = pl.ds(t0, CHUNK)
            bav = ba_ref[rows, :]
            ins = []
            for j in range(hp):
                hl = slice(j * HD, (j + 1) * HD)
                ins.append(_dn_load_qkv(q_ref, qh_ref, k_ref, kh_ref, v_ref, vh_ref, hl, rows, t0, c, sblk)
                           + (bav, z_ref[rows, hl], st_ref[j, c], ti_ref[j, c], o_ref[rows, hl], dy_ref[rows, hl]))
            outs = _round_robin([one_head(j, *ins[j], heads[j]) for j in range(hp)])
            dba = outs[0][2]
            for j in range(hp):
                hl = slice(j * HD, (j + 1) * HD)
                dpz, dqkv, dba_j, ddnw, _ = outs[j]
                dq_ref[rows, hl] = _b(dqkv[:, 0:HD])
                dk_ref[rows, hl] = _b(dqkv[:, HD:2 * HD])
                dv_ref[rows, hl] = _b(dqkv[:, 2 * HD:3 * HD])
                dz_ref[rows, hl] = _b(dpz)
                nacc = nacc + ddnw
                if j > 0:
                    dba = dba + dba_j

            @pl.when(grp == 0)
            def _():
                dba_ref[rows, :] = dba

            @pl.when(grp > 0)
            def _():
                dba_ref[rows, :] += dba

            return tuple(out[4] for out in outs), nacc

        init = tuple((ds_ref[h0 + j], dch_ref[h0 + j], cwacc_ref[h0 + j][0:4], sacc_ref[h0 + j][0:1],
                      sacc_ref[h0 + j][1:2]) for j in range(hp))
        heads, nacc = lax.fori_loop(0, nc, chunk, (init, nacc_ref[0:1, :]))
        nacc_ref[0:1, :] = nacc
        zpad = jnp.zeros((4, 3 * HD), f32)
        zrow = jnp.zeros((6, HD), f32)
        for j in range(hp):
            dst, dch, cwacc, sa0, sa1 = heads[j]
            ds_ref[h0 + j] = dst
            dch_ref[h0 + j] = dch
            cwacc_ref[h0 + j] = jnp.concatenate([cwacc, zpad], axis=0)
            sacc_ref[h0 + j] = jnp.concatenate([sa0, sa1, zrow], axis=0)

        @pl.when(step == nsb - 1)
        def _():
            lane8 = lax.broadcasted_iota(jnp.int32, (8, HD), 1)
            row8 = lax.broadcasted_iota(jnp.int32, (8, HD), 0)
            mine = jnp.zeros((8, HD), f32)
            for j in range(hp):
                dcw_ref[h0 + j] = heads[j][2]
                sa = jnp.concatenate([heads[j][3], heads[j][4], zrow], axis=0)
                mine = mine + jnp.where((lane8 == h0 + j) & (row8 < 2), sa, 0.0)

            @pl.when(grp == 0)
            def _():
                dsc_ref[...] = mine

            @pl.when(grp > 0)
            def _():
                dsc_ref[...] += mine

            @pl.when(grp == ngrp - 1)
            def _():
                dsc_ref[2:3, :] = nacc

    wide = hp * HD
    rblk = lambda i, g: nsb - 1 - i
    grp_specs = lambda off: _seq_specs(sb, wide, rblk, lambda i, g: off // hp + g)
    (qsp, qhalo), (ksp, khalo), (vsp, vhalo) = grp_specs(2 * HEADS), grp_specs(3 * HEADS), grp_specs(4 * HEADS)
    zsp, _ = grp_specs(5 * HEADS)
    hsp = lambda off: pl.BlockSpec((sb, wide), lambda i, g: (rblk(i, g), off // hp + g))
    basp = lambda col: pl.BlockSpec((sb, HD), lambda i, g: (rblk(i, g), col))
    cws = lambda off: pl.BlockSpec((4, wide), lambda i, g: (0, off // hp + g))
    whole = lambda shape: pl.BlockSpec(shape, lambda i, g: (0,) * len(shape))
    return pl.pallas_call(
        body, name=name, grid=(nsb, ngrp),
        in_specs=[hsp(HEADS), qsp, qhalo, ksp, khalo, vsp, vhalo, zsp, basp(BLK_BA), hsp(0),
                  pl.BlockSpec((hp, nc, HD, HD), lambda i, g: (g, rblk(i, g), 0, 0)),
                  pl.BlockSpec((hp, nc, CHUNK, CHUNK), lambda i, g: (g, rblk(i, g), 0, 0)),
                  cws(0), cws(HEADS), cws(2 * HEADS), whole((8, HD))],
        out_specs=[hsp(0), hsp(0), hsp(0), hsp(0), basp(0), whole((HEADS, 4, 3 * HD)), whole((8, HD))],
        out_shape=[jax.ShapeDtypeStruct((s, D_MODEL), bf16)] * 4
        + [jax.ShapeDtypeStruct((s, HD), f32), jax.ShapeDtypeStruct((HEADS, 4, 3 * HD), f32),
           jax.ShapeDtypeStruct((8, HD), f32)],
        scratch_shapes=[pltpu.VMEM((HEADS, HD, HD), f32), pltpu.VMEM((HEADS, 8, 3 * HD), f32),
                        pltpu.VMEM((HEADS, 8, 3 * HD), f32), pltpu.VMEM((HEADS, 8, HD), f32), pltpu.VMEM((8, HD), f32)],
        compiler_params=_params(("arbitrary", "arbitrary")),
    )(dy, proj, proj, proj, proj, proj, proj, proj, proj, o, states, tinvs, cw, cw, cw, dnv)


ANY = pl.BlockSpec(memory_space=pl.ANY)
CHIP_MASKS = ((1, 0, 0), (0, 1, 0), (1, 1, 0))
ALL_MASKS = tuple((m >> 2 & 1, m >> 1 & 1, m & 1) for m in range(1, N_DEV))


def _me():
    return lax.axis_index("x"), lax.axis_index("y"), lax.axis_index("c")


def _flip(me, mask):
    return tuple((1 - v) if m else v for v, m in zip(me, mask))


def _dev_index(dev):
    return 4 * dev[0] + 2 * dev[1] + dev[2]


def _chip(dev):
    return 2 * dev[0] + dev[1]


def _comm_call(name, body, arrays, out_shapes, nsem, nloc=0):
    scratch = [pltpu.SemaphoreType.DMA((nsem,)), pltpu.SemaphoreType.DMA((nsem,))]
    if nloc:
        scratch.append(pltpu.SemaphoreType.DMA((nloc,)))
    return pl.pallas_call(
        body, name=name, in_specs=[ANY] * len(arrays), out_specs=[ANY] * len(out_shapes), out_shape=out_shapes,
        scratch_shapes=scratch, compiler_params=pltpu.CompilerParams(has_side_effects=True),
    )(*arrays)


def _put_own(gathered, own, slot):
    return lax.dynamic_update_index_in_dim(gathered, own, slot, 0)


def _gather_halves(arrays, whole_arrays, name):
    na, nw = len(arrays), len(whole_arrays)
    nm = len(CHIP_MASKS)

    def body(*refs):
        srcs, dsts = refs[:na + nw], refs[na + nw:2 * (na + nw)]
        send_sems, recv_sems = refs[2 * (na + nw):]
        me = _me()
        sib = _flip(me, (0, 0, 1))

        def half(a, ref, core):
            rows = arrays[a].shape[0] // 2
            return ref.at[pl.ds(pl.multiple_of(core * rows, rows), rows)]

        def over_ici(a, mi, sender, to):
            if a < na:
                src, dst = half(a, srcs[a], sender[2]), half(a, dsts[a].at[_chip(sender)], sender[2])
            else:
                src, dst = srcs[a], dsts[a].at[_chip(sender)]
            return pltpu.make_async_remote_copy(src_ref=src, dst_ref=dst, send_sem=send_sems.at[a * nm + mi],
                                                recv_sem=recv_sems.at[a * nm + mi], device_id=to, device_id_type=MESH)

        def over_d2d(a, mi, origin, sender, to):
            k = (na + nw) * nm + a * nm + mi
            blk = half(a, dsts[a].at[_chip(origin)], sender[2])
            return pltpu.make_async_remote_copy(src_ref=blk, dst_ref=blk, send_sem=send_sems.at[k],
                                                recv_sem=recv_sems.at[k], device_id=to, device_id_type=MESH)

        sends = []
        for a in range(na + nw):
            for mi, mask in enumerate(CHIP_MASKS):
                cp = over_ici(a, mi, me, _flip(me, mask))
                cp.start()
                sends.append(cp)
        for a in range(na + nw):
            for mi, mask in enumerate(CHIP_MASKS):
                peer = _flip(me, mask)
                over_ici(a, mi, peer, me).wait_recv()
                if a < na:
                    cp = over_d2d(a, mi, peer, me, sib)
                    cp.start()
                    sends.append(cp)
        for a in range(na):
            for mi, mask in enumerate(CHIP_MASKS):
                over_d2d(a, mi, _flip(sib, mask), sib, me).wait_recv()
        for cp in sends:
            cp.wait_send()

    every = list(arrays) + list(whole_arrays)
    got = _comm_call(name, body, every, [jax.ShapeDtypeStruct((4,) + t.shape, t.dtype) for t in every],
                     (2 * na + nw) * nm)
    chip = 2 * lax.axis_index("x") + lax.axis_index("y")
    return [_put_own(g, t, chip) for g, t in zip(got, every)]


HBM = pl.BlockSpec(memory_space=pltpu.HBM)
SEM = pl.BlockSpec(memory_space=pltpu.SEMAPHORE)
DATAFLOW = pltpu.SideEffectType.DATAFLOW_SIDE_EFFECTING


def _split_start(name, srcs, land_shapes, nsem, issue):
    ns, nl = len(srcs), len(land_shapes)

    def body(*refs):
        issue(refs[:ns], refs[ns:ns + nl], refs[2 * (ns + nl)], refs[2 * (ns + nl) + 1])
        token = refs[-1]
        token[...] = jnp.zeros_like(token)

    lands = [lax.empty(t.shape, t.dtype) for t in land_shapes]
    thru = [pltpu.HBM(t.shape, t.dtype) for t in list(srcs) + list(land_shapes)]
    hbm = lambda t: pltpu.with_memory_space_constraint(t, pltpu.HBM)
    return pl.pallas_call(
        body, name=name, in_specs=[HBM] * (ns + nl),
        out_shape=(*thru, pltpu.SemaphoreType.DMA((nsem,)), pltpu.SemaphoreType.DMA((nsem,)),
                   jax.ShapeDtypeStruct((8, HD), f32)),
        out_specs=(*[HBM] * (ns + nl), SEM, SEM, pl.BlockSpec(memory_space=pltpu.VMEM)),
        input_output_aliases={i: i for i in range(ns + nl)},
        compiler_params=pltpu.CompilerParams(has_side_effects=DATAFLOW),
    )(*[hbm(t) for t in srcs], *[hbm(t) for t in lands])


def _split_wait(name, started, ns, after, finish):
    thru, send_sems, recv_sems = started[:-3], started[-3], started[-2]
    nt = len(thru)

    def body(*refs):
        finish(refs[:ns], refs[ns:nt], refs[nt], refs[nt + 1])

    outs = pl.pallas_call(
        body, name=name, in_specs=[HBM] * nt + [SEM, SEM, ANY],
        out_shape=tuple(pltpu.HBM(t.shape, t.dtype) for t in thru), out_specs=tuple([HBM] * nt),
        input_output_aliases={i: i for i in range(nt)},
        compiler_params=pltpu.CompilerParams(has_side_effects=DATAFLOW),
    )(*thru, send_sems, recv_sems, after)
    return list(outs[ns:])


def _gather_start(arrays, name):
    nm = len(CHIP_MASKS)

    def issue(srcs, lands, send_sems, recv_sems):
        me = _me()
        for a in range(len(arrays)):
            for mi, mask in enumerate(CHIP_MASKS):
                pltpu.make_async_remote_copy(
                    src_ref=srcs[a], dst_ref=lands[a].at[_chip(me)], send_sem=send_sems.at[a * nm + mi],
                    recv_sem=recv_sems.at[a * nm + mi], device_id=_flip(me, mask), device_id_type=MESH).start()

    shapes = [jax.ShapeDtypeStruct((4,) + t.shape, t.dtype) for t in arrays]
    return _split_start(name, arrays, shapes, len(arrays) * nm, issue)


def _gather_finish(started, arrays, after, name):
    nm = len(CHIP_MASKS)

    def finish(srcs, lands, send_sems, recv_sems):
        me = _me()
        for a in range(len(arrays)):
            for mi, mask in enumerate(CHIP_MASKS):
                peer = _flip(me, mask)
                cp = pltpu.make_async_remote_copy(
                    src_ref=srcs[a], dst_ref=lands[a].at[_chip(peer)], send_sem=send_sems.at[a * nm + mi],
                    recv_sem=recv_sems.at[a * nm + mi], device_id=peer, device_id_type=MESH)
                cp.wait_send()
                cp.wait_recv()

    got = _split_wait(name, started, len(arrays), after, finish)
    chip = 2 * lax.axis_index("x") + lax.axis_index("y")
    return [_put_own(g, t, chip) for g, t in zip(got, arrays)]


HALF_IN, HALF_OUT = D_MODEL // 2, D_MIX // 8


def _scatter_piece(kind, ref, d):
    j, r = d >> 1, d & 1
    if kind == "win":
        return ref.at[pl.ds(HALF_IN * r, HALF_IN), pl.ds(WIN_STRIDE * j, WIN_W)]
    if kind == "wout":
        return ref.at[pl.ds(2 * HALF_OUT * j + HALF_OUT * r, HALF_OUT)]
    return ref.at[d]


def _scatter_start(arrays, kinds, name):
    na = len(arrays)

    def issue(srcs, lands, send_sems, recv_sems):
        me = _me()
        my = _dev_index(me)
        for d in range(N_DEV):
            dev = (d >> 2, (d >> 1) & 1, d & 1)
            for a in range(na):
                @pl.when(my != d)
                def _(a=a, d=d, dev=dev):
                    pltpu.make_async_remote_copy(
                        src_ref=_scatter_piece(kinds[a], srcs[a], d), dst_ref=lands[a].at[my],
                        send_sem=send_sems.at[a * N_DEV + d], recv_sem=recv_sems.at[a * N_DEV + my],
                        device_id=dev, device_id_type=MESH).start()

    shape_of = {"win": (N_DEV, HALF_IN, WIN_W), "wout": (N_DEV, HALF_OUT, D_MODEL)}
    shapes = [jax.ShapeDtypeStruct(shape_of.get(k, t.shape), t.dtype) for t, k in zip(arrays, kinds)]
    return _split_start(name, arrays, shapes, na * N_DEV, issue)


def _scatter_finish(started, arrays, kinds, after, name):
    na = len(arrays)

    def finish(srcs, lands, send_sems, recv_sems):
        me = _me()
        my = _dev_index(me)
        for s in range(N_DEV):
            for a in range(na):
                @pl.when(my != s)
                def _(a=a, s=s):
                    cp = pltpu.make_async_remote_copy(
                        src_ref=_scatter_piece(kinds[a], srcs[a], s), dst_ref=lands[a].at[s],
                        send_sem=send_sems.at[a * N_DEV + s], recv_sem=recv_sems.at[a * N_DEV + s],
                        device_id=me, device_id_type=MESH)
                    cp.wait_recv()
                    cp.wait_send()

    got = _split_wait(name, started, na, after, finish)
    x, y, c = _me()
    chip, my = 2 * x + y, 4 * x + 2 * y + c
    own = {"win": lambda t: lax.dynamic_slice(t, (HALF_IN * c, WIN_STRIDE * chip), (HALF_IN, WIN_W)),
           "wout": lambda t: lax.dynamic_slice(t, (2 * HALF_OUT * chip + HALF_OUT * c, 0), (HALF_OUT, D_MODEL)),
           "small": lambda t: lax.dynamic_index_in_dim(t, my, 0, keepdims=False)}
    return [_put_own(g, own[k](t), my) for g, t, k in zip(got, arrays, kinds)]


def _share_reduced(pair_arrays, small, name):
    arrays = list(pair_arrays) + ([small] if small is not None else [])
    na, npair = len(arrays), len(pair_arrays)
    nm = len(ALL_MASKS)

    def body(*refs):
        srcs, dsts = refs[:na], refs[na:2 * na]
        send_sems, recv_sems = refs[2 * na:]
        me = _me()
        sib = _flip(me, (0, 0, 1))

        def copy(a, k, sender, to):
            slot = sender[2] if a < npair else _dev_index(sender)
            return pltpu.make_async_remote_copy(
                src_ref=srcs[a], dst_ref=dsts[a].at[slot], send_sem=send_sems.at[k], recv_sem=recv_sems.at[k],
                device_id=to, device_id_type=MESH)

        sends = [copy(a, a, me, sib) for a in range(npair)]
        if small is not None:
            sends += [copy(npair, npair + mi, me, _flip(me, mask)) for mi, mask in enumerate(ALL_MASKS)]
        for cp in sends:
            cp.start()
        for a in range(npair):
            copy(a, a, sib, me).wait_recv()
        if small is not None:
            for mi, mask in enumerate(ALL_MASKS):
                copy(npair, npair + mi, _flip(me, mask), me).wait_recv()
        for cp in sends:
            cp.wait_send()

    shapes = [jax.ShapeDtypeStruct((2,) + t.shape, t.dtype) for t in pair_arrays]
    if small is not None:
        shapes.append(jax.ShapeDtypeStruct((N_DEV,) + small.shape, small.dtype))
    got = _comm_call(name, body, arrays, shapes, npair + (nm if small is not None else 0))
    x, y, c = _me()
    slots = [c] * npair + [4 * x + 2 * y + c]
    return [_put_own(g, t, slot) for g, t, slot in zip(got, arrays, slots)]


def _sum_parts(parts, name):
    _, r, c = parts.shape
    tr = r
    while tr * c * 4 * N_DEV > (8 << 20) and tr % 32 == 0:
        tr //= 2

    def body(p_ref, o_ref):
        total = p_ref[0].astype(f32)
        for d in range(1, N_DEV):
            total = total + p_ref[d].astype(f32)
        o_ref[...] = total

    return pl.pallas_call(
        body, name=name, grid=(r // tr,), in_specs=[pl.BlockSpec((N_DEV, tr, c), lambda i: (0, i, 0))],
        out_specs=pl.BlockSpec((tr, c), lambda i: (i, 0)), out_shape=jax.ShapeDtypeStruct((r, c), f32),
        compiler_params=_params(("parallel",)),
    )(parts)


def _adamw(w, g, m, v, name):
    shape = w.shape
    cols = shape[-1]
    rows = w.size // cols
    tr = rows
    while tr * cols * 4 > (1 << 20) and tr % 16 == 0:
        tr //= 2
    c1 = 1.0 / (1.0 - ADAM_B1 ** ADAM_STEP)
    c2 = 1.0 / (1.0 - ADAM_B2 ** ADAM_STEP)

    def body(w_ref, g_ref, m_ref, v_ref, d_ref, nm_ref, nv_ref):
        gv = g_ref[...]
        mv = ADAM_B1 * m_ref[...] + (1.0 - ADAM_B1) * gv
        vv = ADAM_B2 * v_ref[...] + (1.0 - ADAM_B2) * (gv * gv)
        nm_ref[...] = mv
        nv_ref[...] = vv
        d_ref[...] = -ADAM_LR * ((mv * c1) / (jnp.sqrt(vv * c2) + ADAM_EPS) + ADAM_WD * w_ref[...])

    spec = pl.BlockSpec((tr, cols), lambda i: (i, 0))
    outs = pl.pallas_call(
        body, name=name, grid=(rows // tr,), in_specs=[spec] * 4, out_specs=[spec] * 3,
        out_shape=[jax.ShapeDtypeStruct((rows, cols), f32)] * 3, compiler_params=_params(("parallel",)),
    )(*[t.reshape(rows, cols) for t in (w, g, m, v)])
    return tuple(t.reshape(shape) for t in outs)


def _pad_lanes(t, n=HD):
    return jnp.pad(t, [(0, 0)] * (t.ndim - 1) + [(0, n - t.shape[-1])])


def _rows128(t, rows=None):
    t = t.reshape(-1, HD)
    rows = rows or -(-t.shape[0] // 8) * 8
    return jnp.pad(t, ((0, rows - t.shape[0]), (0, 0)))


def kernel(x, norm_w, w_in, lru_conv_w, lru_conv_b, lru_wa, lru_ba, lru_wx, lru_bx, lru_lambda, lru_norm_w, dn_conv_w, dn_A_log, dn_dt_bias, dn_norm_w, w_out, final_norm_w, loss_target, m_norm_w, m_w_in, m_lru_conv_w, m_lru_conv_b, m_lru_wa, m_lru_ba, m_lru_wx, m_lru_bx, m_lru_lambda, m_lru_norm_w, m_dn_conv_w, m_dn_A_log, m_dn_dt_bias, m_dn_norm_w, m_w_out, m_final_norm_w, v_norm_w, v_w_in, v_lru_conv_w, v_lru_conv_b, v_lru_wa, v_lru_ba, v_lru_wx, v_lru_bx, v_lru_lambda, v_lru_norm_w, v_dn_conv_w, v_dn_A_log, v_dn_dt_bias, v_dn_norm_w, v_w_out, v_final_norm_w):
    depth = norm_w.shape[0]
    xs = x[0]
    target = loss_target[0]
    chip = 2 * lax.axis_index("x") + lax.axis_index("y")

    win_b, wout_b = w_in.astype(bf16), w_out.astype(bf16)
    got0 = _gather_halves([win_b[0], wout_b[0]], [lru_conv_w, dn_conv_w], "gather_weights0")
    lcw_full = got0[2].transpose(1, 2, 0, 3).reshape(depth, 4, D_MODEL)
    dcw_full = got0[3].transpose(1, 2, 0, 3).reshape(depth, 4, 3 * D_MODEL)
    later = {l: _gather_start([win_b[l], wout_b[l]], f"gather_weights{l}_start") for l in range(1, depth)}

    def weights_of(l, after):
        if l == 0:
            got, token = got0, later[1][-1] if depth > 1 else None
        else:
            got, token = _gather_finish(later[l], [win_b[l], wout_b[l]], after, f"gather_weights{l}_wait"), None
        wp_l = _pad_lanes(jnp.concatenate([got[0][j] for j in range(4)], axis=1), D_INP)
        return wp_l, got[1].reshape(D_MIX, D_MODEL), token

    in_flight = {}

    def on_grad(kind, l, g):
        if l == depth - 1 and depth > 1:
            if kind == "wout":
                in_flight["held"] = g
                return None
            srcs, kinds, key = [g, in_flight.pop("held")], ["win", "wout"], ("both", l)
        else:
            srcs, kinds, key = [g], [kind], (kind, l)
        started = _scatter_start(srcs, kinds, f"scatter_{key[0]}{key[1]}_start")
        in_flight[key] = (started, srcs, kinds)
        return started[-1]

    vecs, dnvs = [], []
    zrow = jnp.zeros((D_MODEL,), f32)
    for l in range(depth):
        vecs.append(jnp.stack([lru_conv_b[l], lru_ba[l], lru_bx[l], lru_lambda[l], lru_norm_w[l], zrow, zrow, zrow]))
        dnvs.append(jnp.concatenate([_pad_lanes(dn_A_log[l][None]), _pad_lanes(dn_dt_bias[l][None]),
                                     dn_norm_w[l][None], jnp.zeros((5, HD), f32)], axis=0))

    loss_part, grad_x, d_fnw, g_nw, g_vec, g_wa, g_wx, g_lcw_, g_dcw_, g_dsc = _local_step(
        xs, target, norm_w, final_norm_w, weights_of, on_grad, lcw_full, dcw_full, vecs, dnvs, lru_wa, lru_wx)
    loss = lax.psum(loss_part[0, 0], ("x", "y", "c"))
    grads = _reduce_grads(chip, grad_x, in_flight, d_fnw, g_nw, g_vec, g_wa, g_wx, g_lcw_, g_dcw_, g_dsc)

    names = ["norm_w", "w_in", "lru_conv_w", "lru_conv_b", "lru_wa", "lru_ba", "lru_wx", "lru_bx", "lru_lambda",
             "lru_norm_w", "dn_conv_w", "dn_A_log", "dn_dt_bias", "dn_norm_w", "w_out", "final_norm_w"]
    ws = dict(zip(names, [norm_w, w_in, lru_conv_w, lru_conv_b, lru_wa, lru_ba, lru_wx, lru_bx, lru_lambda, lru_norm_w,
                          dn_conv_w, dn_A_log, dn_dt_bias, dn_norm_w, w_out, final_norm_w]))
    ms = dict(zip(names, [m_norm_w, m_w_in, m_lru_conv_w, m_lru_conv_b, m_lru_wa, m_lru_ba, m_lru_wx, m_lru_bx,
                          m_lru_lambda, m_lru_norm_w, m_dn_conv_w, m_dn_A_log, m_dn_dt_bias, m_dn_norm_w, m_w_out,
                          m_final_norm_w]))
    vs = dict(zip(names, [v_norm_w, v_w_in, v_lru_conv_w, v_lru_conv_b, v_lru_wa, v_lru_ba, v_lru_wx, v_lru_bx,
                          v_lru_lambda, v_lru_norm_w, v_dn_conv_w, v_dn_A_log, v_dn_dt_bias, v_dn_norm_w, v_w_out,
                          v_final_norm_w]))
    deltas, new_m, new_v = [], [], []
    for n in names:
        d, nm_, nv_ = _adamw(ws[n], grads[n], ms[n], vs[n], f"adamw_{n}")
        deltas.append(d)
        new_m.append(nm_)
        new_v.append(nv_)
    return (loss, grad_x[None], *[grads[n] for n in names], *deltas, *new_m, *new_v)


def _behind(t, token):
    return t if token is None else t + token[0:1, 0:1]


def _local_step(xs, target, norm_w, final_norm_w, weights_of, on_grad, lcw_full, dcw_full, vecs, dnvs, lru_wa, lru_wx):
    depth = norm_w.shape[0]
    saved = []
    h = xs
    for l in range(depth):
        wp_l, wo_l, token = weights_of(l, h)
        hn = _rms_fwd(h, _behind(norm_w[l][None], token), f"rms_fwd{l}")
        proj = _matmul(hn, wp_l, tm=2048, tn=896, tk=D_MODEL, name=f"in_proj{l}")
        y, hl = _lru_fwd(proj, lcw_full[l], vecs[l], lru_wa[l], lru_wx[l], f"lru_fwd{l}")
        y, o, states, tinvs = _dn_fwd(proj, dcw_full[l], dnvs[l], y, f"dn_fwd{l}")
        out = _matmul(y, wo_l, tm=512, tn=D_MODEL, tk=D_MIX, res=h, name=f"out_proj{l}")
        saved.append((h, hn, proj, hl, o, states, tinvs, y, wp_l, wo_l))
        h = out

    dh, d_fnw, loss_part = _final_loss(h, final_norm_w[None], target, "final_loss")

    g_nw, g_vec, g_wa, g_wx, g_lcw_, g_dcw_, g_dsc = ([None] * depth for _ in range(7))
    for l in reversed(range(depth)):
        hin, hn, proj, hl, o, states, tinvs, y, wp_l, wo_l = saved[l]
        dy = _matmul(dh, wo_l, tb=True, tm=512, tn=1024, tk=D_MODEL, name=f"d_y{l}")
        g_wout = _matmul(y, dh, ta=True, tm=512, tn=D_MODEL, tk=512, out_dtype=bf16, name=f"d_wout{l}")
        token = on_grad("wout", l, g_wout)
        dlx, dlz, g_lcw_[l], g_vec[l], g_wa[l], g_wx[l] = _lru_bwd(dy, proj, hl, lcw_full[l], _behind(vecs[l], token),
                                                                 lru_wa[l], lru_wx[l], f"lru_bwd{l}")
        dq, dk, dv, dz, dba, dcw8, g_dsc[l] = _dn_bwd(dy, proj, o, states, tinvs, dcw_full[l], dnvs[l], f"dn_bwd{l}")
        g_dcw_[l] = dcw8.reshape(HEADS, 4, 3, HD).transpose(1, 2, 0, 3).reshape(4, 3 * D_MODEL)
        dxs = [dlx, dlz, dq, dk, dv, dz]
        g_win = _d_win(_transpose(hn, f"hn_t{l}"), dxs, dba, f"d_win{l}")
        token = on_grad("win", l, g_win)
        dhn = _d_hn(dxs, _behind(dba, token), wp_l, f"d_hn{l}")
        dh, g_nw[l] = _rms_bwd(dhn, hin, norm_w[l][None], dh, f"rms_bwd{l}")
    return loss_part, dh, d_fnw, g_nw, g_vec, g_wa, g_wx, g_lcw_, g_dcw_, g_dsc


def _reduce_grads(chip, after, in_flight, d_fnw, g_nw, g_vec, g_wa, g_wx, g_lcw_, g_dcw_, g_dsc):
    depth = len(g_nw)
    both = lambda f: jnp.concatenate([f(l) for l in range(depth)])
    small = [
        both(lambda l: _rows128(g_nw[l])),
        both(lambda l: _rows128(g_vec[l][0])), both(lambda l: _rows128(g_vec[l][1])),
        both(lambda l: _rows128(g_vec[l][2])), both(lambda l: _rows128(g_vec[l][3])),
        both(lambda l: _rows128(g_vec[l][4])),
        both(lambda l: _rows128(g_wa[l])), both(lambda l: _rows128(g_wx[l])),
        both(lambda l: _rows128(g_dsc[l][0:1])), both(lambda l: _rows128(g_dsc[l][1:2])),
        both(lambda l: _rows128(g_dsc[l][2:3])),
        _rows128(d_fnw),
        both(lambda l: _rows128(g_lcw_[l])), both(lambda l: _rows128(g_dcw_[l])),
    ]
    counts = [t.shape[0] for t in small]
    total = sum(counts)
    per = -(-total // (8 * N_DEV)) * 8
    small = jnp.concatenate(small + [jnp.zeros((per * N_DEV - total, HD), f32)]).reshape(N_DEV, per, HD)

    small_started = _scatter_start([small], ["small"], "scatter_small_start")
    reduced = {}
    for key in sorted(in_flight, key=lambda k: -k[1]):
        started, srcs, kinds = in_flight[key]
        got = _scatter_finish(started, srcs, kinds, after, f"scatter_{key[0]}{key[1]}_wait")
        for kind, part in zip(kinds, got):
            reduced[(kind, key[1])] = _sum_parts(part, f"sum_{kind}{key[1]}")
    got = _scatter_finish(small_started, [small], ["small"], after, "scatter_small_wait")
    pairs = [None] * depth
    a_small = None
    for l in reversed(range(depth)):
        sums = [reduced[("win", l)], reduced[("wout", l)]]
        if l == 0:
            shared = _share_reduced(sums, _sum_parts(got[0], "sum_small"), f"share_grads{l}")
            a_small = shared[2].reshape(N_DEV * per, HD)
        else:
            shared = _share_reduced(sums, None, f"share_grads{l}")
        pairs[l] = shared
    grad_w_in = jnp.stack([lax.dynamic_slice_in_dim(pairs[l][0].reshape(D_MODEL, WIN_W), 4 * chip, SHARD_IN, 1)
                           for l in range(depth)])
    grad_w_out = jnp.stack([pairs[l][1].reshape(D_MIX // 4, D_MODEL) for l in range(depth)])

    offs = [0]
    for c in counts:
        offs.append(offs[-1] + c)

    def piece(k, rows_per_layer=None):
        t = a_small[offs[k]:offs[k + 1]]
        if rows_per_layer is None:
            return t
        return t.reshape(depth, -1, HD)[:, :rows_per_layer]

    vec = lambda k: piece(k).reshape(depth, D_MODEL)
    return {
        "norm_w": vec(0), "lru_conv_b": vec(1), "lru_ba": vec(2), "lru_bx": vec(3), "lru_lambda": vec(4),
        "lru_norm_w": vec(5),
        "lru_wa": piece(6).reshape(depth, HEADS, HD, HD), "lru_wx": piece(7).reshape(depth, HEADS, HD, HD),
        "dn_A_log": piece(8, 1)[:, 0, :HEADS], "dn_dt_bias": piece(9, 1)[:, 0, :HEADS], "dn_norm_w": piece(10, 1)[:, 0],
        "final_norm_w": piece(11).reshape(D_MODEL),
        "lru_conv_w": lax.dynamic_slice_in_dim(piece(12).reshape(depth, 4, D_MODEL), chip * (D_MODEL // 4), D_MODEL // 4, 2),
        "dn_conv_w": lax.dynamic_slice_in_dim(piece(13).reshape(depth, 4, 3 * D_MODEL), chip * (3 * D_MODEL // 4),
                                              3 * D_MODEL // 4, 2),
        "w_in": grad_w_in, "w_out": grad_w_out,
    }
```

```python
import jax
import jax.numpy as jnp
from jax import lax
from jax.experimental import pallas as pl
from jax.experimental.pallas import tpu as pltpu

f32 = jnp.float32
bf16 = jnp.bfloat16
HI = lax.Precision.HIGHEST
MESH = pl.DeviceIdType.MESH

D_MODEL = 1024
HEADS = 8
HD = 128
CHUNK = 64
LRU_T = 128
SEQ_BLOCK = 1024
LRU_HP = 4
DN_SEQ_BLOCK = 256
DN_HP = 8
LRU_C = 8.0
D_IN = 6160
D_INP = 6272
D_MIX = 2048
N_GROUPS = 6
BLK_BA = 48
SHARD_IN = D_IN // 4
WIN_W = 1664
WIN_STRIDE = 1536
EPS = 1e-6
QSCALE = HD ** -0.5
N_DEV = 8
VMEM_LIMIT = 56 * 1024 * 1024

ADAM_LR, ADAM_B1, ADAM_B2, ADAM_EPS, ADAM_WD, ADAM_STEP = 0.001, 0.9, 0.999, 1e-08, 0.01, 10


def _sig(x):
    return 1.0 / (1.0 + jnp.exp(-x))


def _log1p(u):
    w = 1.0 + u
    d = w - 1.0
    return jnp.where(d == 0.0, u, jnp.log(w) * (u / jnp.where(d == 0.0, 1.0, d)))


def _softplus(x):
    return jnp.maximum(x, 0.0) + _log1p(jnp.exp(-jnp.abs(x)))


def _expm1(x):
    t = x * (1.0 + x * (0.5 + x * (1.0 / 6 + x * (1.0 / 24 + x * (1.0 / 120 + x * (1.0 / 720))))))
    return jnp.where(jnp.abs(x) < 0.2, t, jnp.exp(x) - 1.0)


def _dot(a, b, prec=None):
    return jnp.dot(a, b, precision=prec, preferred_element_type=f32)


def _dot_nt(a, b, prec=None):
    return lax.dot_general(a, b, (((1,), (1,)), ((), ())), precision=prec, preferred_element_type=f32)


def _dot_tn(a, b, prec=None):
    return lax.dot_general(a, b, (((0,), (0,)), ((), ())), precision=prec, preferred_element_type=f32)


def _b(x):
    return x.astype(bf16)


def _sum0(x):
    return jnp.sum(x, axis=0, keepdims=True)


def _sum1(x):
    return jnp.sum(x, axis=1, keepdims=True)


def _rowmean(x):
    return jnp.mean(x, axis=-1, keepdims=True)


def _dsilu(z, sg):
    return sg * (1.0 + z * (1.0 - sg))


def _conv_taps(x, halo):
    xx = jnp.concatenate([halo, x], axis=0)
    return [pltpu.roll(xx, 3, axis=0)[8:], pltpu.roll(xx, 2, axis=0)[8:], pltpu.roll(xx, 1, axis=0)[8:], x]


def _conv(xs, cw):
    return cw[0:1] * xs[0] + cw[1:2] * xs[1] + cw[2:3] * xs[2] + cw[3:4] * xs[3]


def _rows_before(ref, halo_ref, lanes, t0, c, sblk):
    tprev = pl.multiple_of(jnp.maximum(t0 - 8, 0), 8)
    return jnp.where(c > 0, ref[pl.ds(tprev, 8), lanes], jnp.where(sblk > 0, halo_ref[:, lanes], 0.0))


def _conv_back(dxc, halo_after, cw):
    rows = dxc.shape[0]
    dd = jnp.concatenate([dxc, halo_after], axis=0)
    out = cw[3:4] * dxc
    for s in (1, 2, 3):
        out = out + cw[3 - s:4 - s] * pltpu.roll(dd, rows + 8 - s, axis=0)[:rows]
    return out


def _params(sem=None):
    return pltpu.CompilerParams(dimension_semantics=sem, vmem_limit_bytes=VMEM_LIMIT)


def _seq_specs(sb, width, rowblk, colblk):
    main = pl.BlockSpec((sb, width), lambda a, b: (rowblk(a, b), colblk(a, b)))
    halo = pl.BlockSpec((8, width), lambda a, b: (jnp.maximum(rowblk(a, b) * (sb // 8) - 1, 0), colblk(a, b)))
    return main, halo


def _matmul(a, b, *, ta=False, tb=False, tm, tn, tk, res=None, out_dtype=f32, name):
    if ta:
        kdim, m = a.shape
    else:
        m, kdim = a.shape
    n = b.shape[0] if tb else b.shape[1]
    tm, tn, tk = min(tm, m), min(tn, n), min(tk, kdim)
    assert m % tm == 0 and n % tn == 0 and kdim % tk == 0, (name, m, n, kdim, tm, tn, tk)
    nk = kdim // tk
    dims = (((0 if ta else 1,), (1 if tb else 0,)), ((), ()))
    has_res = res is not None

    def body(*refs):
        a_ref, b_ref = refs[:2]
        r_ref = refs[2] if has_res else None
        o_ref = refs[3] if has_res else refs[2]
        acc_ref = refs[-1] if nk > 1 else None
        part = lax.dot_general(_b(a_ref[...]), _b(b_ref[...]), dims, preferred_element_type=f32)

        def finish(total):
            if has_res:
                total = total + r_ref[...]
            o_ref[...] = total.astype(out_dtype)

        if nk == 1:
            finish(part)
        else:
            k = pl.program_id(2)

            @pl.when(k == 0)
            def _():
                acc_ref[...] = part

            @pl.when(k > 0)
            def _():
                acc_ref[...] += part

            @pl.when(k == nk - 1)
            def _():
                finish(acc_ref[...])

    a_spec = pl.BlockSpec((tk, tm), lambda i, j, k: (k, i)) if ta else pl.BlockSpec((tm, tk), lambda i, j, k: (i, k))
    b_spec = pl.BlockSpec((tn, tk), lambda i, j, k: (j, k)) if tb else pl.BlockSpec((tk, tn), lambda i, j, k: (k, j))
    o_spec = pl.BlockSpec((tm, tn), lambda i, j, k: (i, j))
    in_specs, args = [a_spec, b_spec], [a, b]
    if has_res:
        in_specs.append(o_spec)
        args.append(res)
    return pl.pallas_call(
        body, name=name, grid=(m // tm, n // tn, nk), in_specs=in_specs, out_specs=o_spec,
        out_shape=jax.ShapeDtypeStruct((m, n), out_dtype),
        scratch_shapes=[pltpu.VMEM((tm, tn), f32)] if nk > 1 else [],
        compiler_params=_params(("parallel", "parallel", "arbitrary")),
    )(*args)


def _transpose(x, name):
    r, c = x.shape
    tr, tc = min(512, r), min(512, c)

    def body(x_ref, o_ref):
        o_ref[...] = x_ref[...].T

    return pl.pallas_call(
        body, name=name, grid=(r // tr, c // tc), in_specs=[pl.BlockSpec((tr, tc), lambda i, j: (i, j))],
        out_specs=pl.BlockSpec((tc, tr), lambda i, j: (j, i)), out_shape=jax.ShapeDtypeStruct((c, r), x.dtype),
        compiler_params=_params(("parallel", "parallel")),
    )(x)


def _d_hn(dxs, dba, wp, name):
    s = dxs[0].shape[0]
    tm = min(1024, s)

    def body(*refs):
        dx_refs = refs[:N_GROUPS]
        dba_ref, w_ref, wba_ref, o_ref, acc_ref = refs[N_GROUPS:]
        k = pl.program_id(1)
        for g in range(N_GROUPS):
            @pl.when(k == g)
            def _(g=g):
                part = _dot_nt(_b(dx_refs[g][...]), w_ref[...])
                if g == 0:
                    acc_ref[...] = part + _dot_nt(_b(dba_ref[...]), wba_ref[...])
                else:
                    acc_ref[...] += part

        @pl.when(k == N_GROUPS - 1)
        def _():
            o_ref[...] = acc_ref[...]

    row = lambda w: pl.BlockSpec((tm, w), lambda i, k: (i, 0))
    return pl.pallas_call(
        body, name=name, grid=(s // tm, N_GROUPS),
        in_specs=[row(D_MODEL)] * N_GROUPS + [row(HD), pl.BlockSpec((D_MODEL, D_MODEL), lambda i, k: (0, k)),
                                              pl.BlockSpec((D_MODEL, HD), lambda i, k: (0, BLK_BA))],
        out_specs=row(D_MODEL), out_shape=jax.ShapeDtypeStruct((s, D_MODEL), f32),
        scratch_shapes=[pltpu.VMEM((tm, D_MODEL), f32)],
        compiler_params=_params(("parallel", "arbitrary")),
    )(*dxs, dba, wp, wp)


def _d_win(hnt, dxs, dba, name):
    s = hnt.shape[1]
    tn = 256
    per = D_MODEL // tn

    def body(*refs):
        hnt_ref = refs[0]
        dx_refs = refs[1:1 + N_GROUPS]
        o_ref = refs[1 + N_GROUPS]
        j = pl.program_id(0)
        for g in range(N_GROUPS):
            @pl.when(j // per == g)
            def _(g=g):
                o_ref[...] = _dot(hnt_ref[...], _b(dx_refs[g][...])).astype(bf16)

    def dx_spec(g):
        on = lambda j: (j // per == g).astype(jnp.int32)
        return pl.BlockSpec((s, tn), lambda j: (0, (j % per) * on(j)))

    main = pl.pallas_call(
        body, name=name, grid=(N_GROUPS * per,),
        in_specs=[pl.BlockSpec((D_MODEL, s), lambda j: (0, 0))] + [dx_spec(g) for g in range(N_GROUPS)],
        out_specs=pl.BlockSpec((D_MODEL, tn), lambda j: (0, j)),
        out_shape=jax.ShapeDtypeStruct((D_MODEL, D_INP), bf16),
        compiler_params=_params(("parallel",)),
    )(hnt, *dxs)

    def tail(hnt_ref, dba_ref, full_ref, o_ref):
        del full_ref
        o_ref[...] = _dot(hnt_ref[...], _b(dba_ref[...])).astype(bf16)

    return pl.pallas_call(
        tail, name=name + "_ba", grid=(1,),
        in_specs=[pl.BlockSpec((D_MODEL, s), lambda k: (0, 0)), pl.BlockSpec((s, HD), lambda k: (0, 0)),
                  pl.BlockSpec(memory_space=pl.ANY)],
        out_specs=pl.BlockSpec((D_MODEL, HD), lambda k: (0, BLK_BA)),
        out_shape=jax.ShapeDtypeStruct((D_MODEL, D_INP), bf16),
        input_output_aliases={2: 0},
        compiler_params=_params(("arbitrary",)),
    )(hnt, dba, main)


def _rms_fwd(x, w, name):
    s = x.shape[0]
    tr = min(512, s)

    def body(x_ref, w_ref, h_ref):
        xv = x_ref[...]
        r = lax.rsqrt(_rowmean(xv * xv) + EPS)
        h_ref[...] = (xv * r * w_ref[...]).astype(bf16)

    return pl.pallas_call(
        body, name=name, grid=(s // tr,),
        in_specs=[pl.BlockSpec((tr, D_MODEL), lambda i: (i, 0)), pl.BlockSpec((1, D_MODEL), lambda i: (0, 0))],
        out_specs=pl.BlockSpec((tr, D_MODEL), lambda i: (i, 0)),
        out_shape=jax.ShapeDtypeStruct((s, D_MODEL), bf16), compiler_params=_params(("parallel",)),
    )(x, w)


def _rms_bwd(dh, x, w, dres, name):
    s = x.shape[0]
    tr = min(512, s)

    def body(dh_ref, x_ref, w_ref, dres_ref, dx_ref, dw_ref):
        xv = x_ref[...]
        r = lax.rsqrt(_rowmean(xv * xv) + EPS)
        xn = xv * r
        d = dh_ref[...]
        dxn = d * w_ref[...]
        dx_ref[...] = dres_ref[...] + r * (dxn - xn * _rowmean(dxn * xn))
        part = _sum0(d * xn)

        @pl.when(pl.program_id(0) == 0)
        def _():
            dw_ref[...] = part

        @pl.when(pl.program_id(0) > 0)
        def _():
            dw_ref[...] += part

    row = pl.BlockSpec((tr, D_MODEL), lambda i: (i, 0))
    vec = pl.BlockSpec((1, D_MODEL), lambda i: (0, 0))
    return pl.pallas_call(
        body, name=name, grid=(s // tr,), in_specs=[row, row, vec, row], out_specs=[row, vec],
        out_shape=[jax.ShapeDtypeStruct((s, D_MODEL), f32), jax.ShapeDtypeStruct((1, D_MODEL), f32)],
        compiler_params=_params(("arbitrary",)),
    )(dh, x, w, dres)


def _final_loss(x, w, target, name):
    s = x.shape[0]
    tr = min(512, s)

    def body(x_ref, w_ref, t_ref, dx_ref, dw_ref, loss_ref):
        xv = x_ref[...]
        wv = w_ref[...]
        r = lax.rsqrt(_rowmean(xv * xv) + EPS)
        xn = xv * r
        e = xn * wv - t_ref[...]
        lb = jnp.broadcast_to(0.5 * _sum0(_rowmean(e * e)), (1, HD))
        dy = e * (1.0 / D_MODEL)
        dxn = dy * wv
        dx_ref[...] = r * (dxn - xn * _rowmean(dxn * xn))
        part = _sum0(dy * xn)

        @pl.when(pl.program_id(0) == 0)
        def _():
            dw_ref[...] = part
            loss_ref[...] = lb

        @pl.when(pl.program_id(0) > 0)
        def _():
            dw_ref[...] += part
            loss_ref[...] += lb

    row = pl.BlockSpec((tr, D_MODEL), lambda i: (i, 0))
    vec = pl.BlockSpec((1, D_MODEL), lambda i: (0, 0))
    lsp = pl.BlockSpec((1, HD), lambda i: (0, 0))
    return pl.pallas_call(
        body, name=name, grid=(s // tr,), in_specs=[row, vec, row], out_specs=[row, vec, lsp],
        out_shape=[jax.ShapeDtypeStruct((s, D_MODEL), f32), jax.ShapeDtypeStruct((1, D_MODEL), f32),
                   jax.ShapeDtypeStruct((1, HD), f32)],
        compiler_params=_params(("arbitrary",)),
    )(x, w, target)


ALL = slice(None)


def _lru_gates(xc, vec, wa, wx):
    sp = _softplus(-vec[3:4])
    xcb = _b(xc)
    r = _sig(_dot(xcb, wa) + vec[1:2])
    i = _sig(_dot(xcb, wx) + vec[2:3])
    la = (-LRU_C) * r * sp
    a = jnp.exp(la)
    mult = jnp.sqrt(-_expm1(2.0 * la))
    return r, i, a, mult, sp, xcb


def _lru_fwd(proj, cw, vec, wa, wx, name):
    s = proj.shape[0]
    sb = min(SEQ_BLOCK, s)
    nsb = s // sb
    t = min(LRU_T, sb)
    nc = sb // t
    hp = LRU_HP
    wide = hp * HD
    lanes = [slice(j * HD, (j + 1) * HD) for j in range(hp)]

    def body(x_ref, xh_ref, z_ref, cw_ref, vec_ref, wa_ref, wx_ref, y_ref, hl_ref, hc_ref):
        sblk = pl.program_id(1)
        consts = [(cw_ref[:, hl], vec_ref[:, hl], _b(wa_ref[j]), _b(wx_ref[j])) for j, hl in enumerate(lanes)]
        row = lax.broadcasted_iota(jnp.int32, (t, HD), 0)

        @pl.when(sblk == 0)
        def _():
            hc_ref[...] = jnp.zeros((8, wide), f32)

        def one_head(j, x, halo, z, hcarry):
            cwv, vec_v, wa_v, wx_v = consts[j]
            xs = _conv_taps(x, halo)
            xc = vec_v[0:1] + _conv(xs, cwv)
            r, i, a, mult, _, _ = _lru_gates(xc, vec_v, wa_v, wx_v)
            yield
            bb = mult * (i * xc)
            d = 1
            while d < t:
                m = row >= d
                bb = jnp.where(m, a * pltpu.roll(bb, d, axis=0) + bb, bb)
                a = jnp.where(m, a * pltpu.roll(a, d, axis=0), a)
                d *= 2
                yield
            hl = bb + a * hcarry
            nrm = lax.rsqrt(_rowmean(hl * hl) + EPS)
            return hl, (hl * nrm * vec_v[4:5] * (z * _sig(z))).astype(bf16)

        def chunk(c, carries):
            t0 = pl.multiple_of(c * t, t)
            rows = pl.ds(t0, t)
            ins = [(x_ref[rows, hl], _rows_before(x_ref, xh_ref, hl, t0, c, sblk), z_ref[rows, hl]) for hl in lanes]
            outs = _round_robin([one_head(j, *ins[j], carries[j]) for j in range(hp)])
            for j, hl in enumerate(lanes):
                hl_ref[rows, hl] = outs[j][0]
                y_ref[rows, hl] = outs[j][1]
            return tuple(out[0][t - 1:t, :] for out in outs)

        final = lax.fori_loop(0, nc, chunk, tuple(hc_ref[0:1, hl] for hl in lanes))
        hc_ref[...] = jnp.concatenate([jnp.broadcast_to(f, (8, HD)) for f in final], axis=1)

    xsp, xhalo = _seq_specs(sb, wide, lambda g, i: i, lambda g, i: g)
    zsp, _ = _seq_specs(sb, wide, lambda g, i: i, lambda g, i: HEADS // hp + g)
    gcol = lambda g, i: (0, g)
    wsp = pl.BlockSpec((hp, HD, HD), lambda g, i: (g, 0, 0))
    osp = pl.BlockSpec((sb, wide), lambda g, i: (i, g))
    return pl.pallas_call(
        body, name=name, grid=(HEADS // hp, nsb),
        in_specs=[xsp, xhalo, zsp, pl.BlockSpec((4, wide), gcol), pl.BlockSpec((8, wide), gcol), wsp, wsp],
        out_specs=[osp, osp],
        out_shape=[jax.ShapeDtypeStruct((s, D_MIX), bf16), jax.ShapeDtypeStruct((s, D_MODEL), f32)],
        scratch_shapes=[pltpu.VMEM((8, wide), f32)],
        compiler_params=_params(("parallel", "arbitrary")),
    )(proj, proj, proj, cw, vec, wa, wx)


def _lru_bwd(dy, proj, hl, cw, vec, wa, wx, name):
    s = proj.shape[0]
    sb = min(SEQ_BLOCK, s)
    nsb = s // sb
    t = min(LRU_T, sb)
    nc = sb // t
    hp = LRU_HP
    wide = hp * HD
    lanes = [slice(j * HD, (j + 1) * HD) for j in range(hp)]
    n_acc = 10

    def body(dy_ref, x_ref, xh_ref, z_ref, hl_ref, hlh_ref, cw_ref, vec_ref, wa_ref, wx_ref,
             dx_ref, dz_ref, dcw_ref, dvec_ref, dwa_ref, dwx_ref, acc_ref, dxh_ref):
        step = pl.program_id(1)
        sblk = nsb - 1 - step
        consts = [(cw_ref[:, hl], vec_ref[:, hl], _b(wa_ref[j]), _b(wx_ref[j])) for j, hl in enumerate(lanes)]
        row = lax.broadcasted_iota(jnp.int32, (t, HD), 0)

        @pl.when(step == 0)
        def _():
            acc_ref[...] = jnp.zeros((16, wide), f32)
            dxh_ref[...] = jnp.zeros((8, wide), f32)
            dwa_ref[...] = jnp.zeros((hp, HD, HD), f32)
            dwx_ref[...] = jnp.zeros((hp, HD, HD), f32)

        def one_head(j, x, halo, z, hv, hhalo, dyv, carry):
            dcw0, dcw1, dcw2, dcw3, dcb, dba, dbx, dsp, dlnw, mu, dxc_halo = carry
            cwv, vec_v, wa_v, wx_v = consts[j]
            lnw = vec_v[4:5]
            xs = _conv_taps(x, halo)
            xc = vec_v[0:1] + _conv(xs, cwv)
            r, i, a, mult, sp, xcb = _lru_gates(xc, vec_v, wa_v, wx_v)
            yield
            hprev = pltpu.roll(jnp.concatenate([hhalo, hv], axis=0), 1, axis=0)[8:]
            sgz = _sig(z)
            sz = z * sgz
            nrm = lax.rsqrt(_rowmean(hv * hv) + EPS)
            hn = hv * nrm
            dlnw = dlnw + _sum0(dyv * hn * sz)
            dzv = _b(dyv * hn * lnw * _dsilu(z, sgz))
            dhn = dyv * lnw * sz
            lam = nrm * (dhn - hn * _rowmean(dhn * hn))
            cf = jnp.where(row == t - 1, 1.0, pltpu.roll(a, t - 1, axis=0))
            d = 1
            while d < t:
                m = row < t - d
                lam = jnp.where(m, lam + cf * pltpu.roll(lam, t - d, axis=0), lam)
                cf = jnp.where(m, cf * pltpu.roll(cf, t - d, axis=0), cf)
                d *= 2
                yield
            lam = lam + cf * mu
            mu = a[0:1] * lam[0:1]
            da = lam * hprev
            dmult = lam * (i * xc)
            di = lam * mult * xc
            dxc = lam * mult * i
            dla = da * a - dmult * (a * a) / mult
            dr = dla * (-LRU_C * sp)
            dsp = dsp + _sum0(dla * (-LRU_C * r))
            dra = dr * r * (1.0 - r)
            dia = di * i * (1.0 - i)
            dba = dba + _sum0(dra)
            dbx = dbx + _sum0(dia)
            drab, diab = _b(dra), _b(dia)
            dwa = _dot_tn(xcb, drab)
            dwx = _dot_tn(xcb, diab)
            dxc = dxc + _dot_nt(drab, wa_v) + _dot_nt(diab, wx_v)
            yield
            dcb = dcb + _sum0(dxc)
            dcw0 = dcw0 + _sum0(dxc * xs[0])
            dcw1 = dcw1 + _sum0(dxc * xs[1])
            dcw2 = dcw2 + _sum0(dxc * xs[2])
            dcw3 = dcw3 + _sum0(dxc * xs[3])
            dxv = _b(_conv_back(dxc, dxc_halo, cwv))
            return dxv, dzv, dwa, dwx, (dcw0, dcw1, dcw2, dcw3, dcb, dba, dbx, dsp, dlnw, mu, dxc[0:8])

        def chunk(ci, carries):
            c = nc - 1 - ci
            t0 = pl.multiple_of(c * t, t)
            rows = pl.ds(t0, t)
            ins = [(x_ref[rows, hl], _rows_before(x_ref, xh_ref, hl, t0, c, sblk), z_ref[rows, hl], hl_ref[rows, hl],
                    _rows_before(hl_ref, hlh_ref, hl, t0, c, sblk), dy_ref[rows, hl]) for hl in lanes]
            outs = _round_robin([one_head(j, *ins[j], carries[j]) for j in range(hp)])
            for j, hl in enumerate(lanes):
                dx_ref[rows, hl] = outs[j][0]
                dz_ref[rows, hl] = outs[j][1]
                dwa_ref[j] += outs[j][2]
                dwx_ref[j] += outs[j][3]
            return tuple(out[4] for out in outs)

        acc = acc_ref[...]
        dxh = dxh_ref[...]
        init = tuple(tuple(acc[k:k + 1, hl] for k in range(n_acc)) + (dxh[:, hl],) for hl in lanes)
        final = lax.fori_loop(0, nc, chunk, init)
        rows_out = [jnp.concatenate([final[j][k] for j in range(hp)], axis=1) for k in range(n_acc)]
        zero = jnp.zeros((1, wide), f32)
        acc_ref[...] = jnp.concatenate(rows_out + [zero] * (16 - n_acc), axis=0)
        dxh_ref[...] = jnp.concatenate([final[j][n_acc] for j in range(hp)], axis=1)

        @pl.when(step == nsb - 1)
        def _():
            dcw_ref[...] = jnp.concatenate(rows_out[0:4], axis=0)
            dlam = -rows_out[7] * _sig(-vec_ref[3:4, :])
            dvec_ref[...] = jnp.concatenate([rows_out[4], rows_out[5], rows_out[6], dlam, rows_out[8], zero, zero, zero],
                                            axis=0)

    rblk = lambda g, i: nsb - 1 - i
    xsp, xhalo = _seq_specs(sb, wide, rblk, lambda g, i: g)
    zsp, _ = _seq_specs(sb, wide, rblk, lambda g, i: HEADS // hp + g)
    gcol = lambda g, i: (0, g)
    wsp = pl.BlockSpec((hp, HD, HD), lambda g, i: (g, 0, 0))
    return pl.pallas_call(
        body, name=name, grid=(HEADS // hp, nsb),
        in_specs=[xsp, xsp, xhalo, zsp, xsp, xhalo, pl.BlockSpec((4, wide), gcol), pl.BlockSpec((8, wide), gcol), wsp, wsp],
        out_specs=[xsp, xsp, pl.BlockSpec((4, wide), gcol), pl.BlockSpec((8, wide), gcol), wsp, wsp],
        out_shape=[jax.ShapeDtypeStruct((s, D_MODEL), bf16), jax.ShapeDtypeStruct((s, D_MODEL), bf16),
                   jax.ShapeDtypeStruct((4, D_MODEL), f32), jax.ShapeDtypeStruct((8, D_MODEL), f32),
                   jax.ShapeDtypeStruct((HEADS, HD, HD), f32), jax.ShapeDtypeStruct((HEADS, HD, HD), f32)],
        scratch_shapes=[pltpu.VMEM((16, wide), f32), pltpu.VMEM((8, wide), f32)],
        compiler_params=_params(("parallel", "arbitrary")),
    )(dy, proj, proj, proj, hl, hl, cw, vec, wa, wx)


def _lane_pick(x, lane, idx):
    return _sum1(jnp.where(lane == idx, x, 0.0))


def _round_robin(gens):
    results = [None] * len(gens)
    live = list(range(len(gens)))
    while live:
        for i in list(live):
            try:
                next(gens[i])
            except StopIteration as done:
                results[i] = done.value
                live.remove(i)
    return results


def _sdot(a, b):
    return _dot(_b(a), _b(b))


def _sdot_tn(a, b):
    return _dot_tn(_b(a), _b(b))


def _sdot_nt(a, b):
    return _dot_nt(_b(a), _b(b))


def _tri_dot(tri, x):
    trib = tri.astype(bf16)
    x1 = x.astype(bf16)
    r1 = x - x1.astype(f32)
    x2 = r1.astype(bf16)
    x3 = (r1 - x2.astype(f32)).astype(bf16)
    return _dot(trib, x1) + _dot(trib, x2) + _dot(trib, x3)


def _inv_unit_lower(a):
    n = -a
    p = _sdot(a, a)
    yield
    for k in range(5):
        n = n + p + _sdot(n, p)
        if k < 4:
            p = _sdot(p, p)
        yield
    return n


def _dn_chunk(x3, halo3, bav, cw3, dnv, h, masks, tinv=None):
    lane, ri, ci, eye, ltri = masks
    xs = _conv_taps(x3, halo3)
    cpre = _conv(xs, cw3)
    sg = _sig(cpre)
    act = cpre * sg
    q_raw, k_raw, v = act[:, 0:HD], act[:, HD:2 * HD], act[:, 2 * HD:3 * HD]
    rq = lax.rsqrt(_sum1(q_raw * q_raw) + EPS)
    rk = lax.rsqrt(_sum1(k_raw * k_raw) + EPS)
    qn = q_raw * rq
    k = k_raw * rk
    q = qn * QSCALE
    b_in = jnp.broadcast_to(_lane_pick(bav, lane, h), (CHUNK, HD))
    a_in = jnp.broadcast_to(_lane_pick(bav, lane, HEADS + h), (CHUNK, HD))
    lane1 = lane[0:1]
    ea = jnp.exp(_lane_pick(dnv[0:1], lane1, h))
    dt = _lane_pick(dnv[1:2], lane1, h)
    beta = _sig(b_in)
    sp = _softplus(a_in + dt)
    g = -ea * sp
    gc = _tri_dot(ltri, g)
    yield
    gc64 = gc[:, 0:CHUNK]
    grow = _sum0(gc64 * eye)
    dm = jnp.exp(jnp.where(ri >= ci, gc64 - grow, -1e30))
    ds_ = jnp.where(ri > ci, dm, 0.0)
    eg = jnp.exp(gc)
    glast = gc[CHUNK - 1:CHUNK, :]
    ek = jnp.exp(glast - gc)
    kb = k * beta
    kbf = _b(k)
    amat = _dot_nt(_b(kb), kbf) * ds_
    pmat = _dot_nt(_b(q), kbf) * dm
    yield
    if tinv is None:
        tinv = yield from _inv_unit_lower(amat)
    return dict(xs=xs, cpre=cpre, sg=sg, rq=rq, rk=rk, qn=qn, q=q, k=k, v=v, kbf=kbf, b_in=b_in, a_in=a_in, ea=ea,
                dt=dt, beta=beta, g=g, gc=gc, dm=dm, ds=ds_, eg=eg, glast=glast, ek=ek, kb=kb, amat=amat,
                tinv=tinv, pmat=pmat)


def _dn_masks():
    lane = lax.broadcasted_iota(jnp.int32, (CHUNK, HD), 1)
    ri = lax.broadcasted_iota(jnp.int32, (CHUNK, CHUNK), 0)
    ci = lax.broadcasted_iota(jnp.int32, (CHUNK, CHUNK), 1)
    eye = (ri == ci).astype(f32)
    ltri = (ri >= ci).astype(f32)
    return lane, ri, ci, eye, ltri


def _dn_load_qkv(q_ref, qh_ref, k_ref, kh_ref, v_ref, vh_ref, hl, rows, t0, c, sblk):
    x3 = jnp.concatenate([q_ref[rows, hl], k_ref[rows, hl], v_ref[rows, hl]], axis=1)
    halo3 = jnp.concatenate([_rows_before(q_ref, qh_ref, hl, t0, c, sblk), _rows_before(k_ref, kh_ref, hl, t0, c, sblk),
                             _rows_before(v_ref, vh_ref, hl, t0, c, sblk)], axis=1)
    return x3, halo3


def _dn_cw3(cwq_ref, cwk_ref, cwv_ref, j):
    return jnp.concatenate([r[:, j * HD:(j + 1) * HD] for r in (cwq_ref, cwk_ref, cwv_ref)], axis=1)


def _dn_fwd(proj, cw, dnv, y, name):
    s = proj.shape[0]
    sb = min(DN_SEQ_BLOCK, s)
    nsb = s // sb
    nc = sb // CHUNK
    hp = DN_HP

    def body(q_ref, qh_ref, k_ref, kh_ref, v_ref, vh_ref, z_ref, ba_ref, cwq_ref, cwk_ref, cwv_ref, dnv_ref, yin_ref,
             y_ref, o_ref, st_ref, ti_ref, s_ref):
        del yin_ref
        grp = pl.program_id(0)
        sblk = pl.program_id(1)
        masks = _dn_masks()
        dnv_v = dnv_ref[...]
        cw3 = [_dn_cw3(cwq_ref, cwk_ref, cwv_ref, j) for j in range(hp)]

        @pl.when(sblk == 0)
        def _():
            s_ref[...] = jnp.zeros((hp, HD, HD), f32)

        def one_head(j, x3, halo3, bav, z, st):
            f = yield from _dn_chunk(x3, halo3, bav, cw3[j], dnv_v, grp * hp + j, masks)
            sbf = _b(st)
            qs = _dot(_b(f["q"] * f["eg"]), sbf)
            rhs = f["beta"] * (f["v"] - _dot(_b(f["k"] * f["eg"]), sbf))
            yield
            vn = rhs + _sdot(f["tinv"], rhs)
            yield
            vnb = _b(vn)
            o = qs + _dot(_b(f["pmat"]), vnb)
            st_new = st * jnp.exp(f["glast"]) + _dot_tn(_b(f["k"] * f["ek"]), vnb)
            yield
            nrm = lax.rsqrt(_rowmean(o * o) + EPS)
            yv = (o * nrm * dnv_v[2:3] * (z * _sig(z))).astype(bf16)
            return o, yv, st_new, f["tinv"]

        def chunk(c, states):
            t0 = pl.multiple_of(c * CHUNK, CHUNK)
            rows = pl.ds(t0, CHUNK)
            bav = ba_ref[rows, :]
            ins = []
            for j in range(hp):
                hl = slice(j * HD, (j + 1) * HD)
                ins.append(_dn_load_qkv(q_ref, qh_ref, k_ref, kh_ref, v_ref, vh_ref, hl, rows, t0, c, sblk)
                           + (bav, z_ref[rows, hl]))
            outs = _round_robin([one_head(j, *ins[j], states[j]) for j in range(hp)])
            for j in range(hp):
                hl = slice(j * HD, (j + 1) * HD)
                st_ref[j, c] = states[j]
                o_ref[rows, hl] = outs[j][0]
                y_ref[rows, hl] = outs[j][1]
                ti_ref[j, c] = outs[j][3]
            return tuple(out[2] for out in outs)

        final = lax.fori_loop(0, nc, chunk, tuple(s_ref[j] for j in range(hp)))
        for j in range(hp):
            s_ref[j] = final[j]

    wide = hp * HD
    grp_specs = lambda off: _seq_specs(sb, wide, lambda g, i: i, lambda g, i: off // hp + g)
    (qsp, qhalo), (ksp, khalo), (vsp, vhalo) = grp_specs(2 * HEADS), grp_specs(3 * HEADS), grp_specs(4 * HEADS)
    zsp, _ = grp_specs(5 * HEADS)
    basp = pl.BlockSpec((sb, HD), lambda g, i: (i, BLK_BA))
    cws = lambda off: pl.BlockSpec((4, wide), lambda g, i: (0, off // hp + g))
    osp = lambda off: pl.BlockSpec((sb, wide), lambda g, i: (i, off // hp + g))
    return pl.pallas_call(
        body, name=name, grid=(HEADS // hp, nsb),
        in_specs=[qsp, qhalo, ksp, khalo, vsp, vhalo, zsp, basp, cws(0), cws(HEADS), cws(2 * HEADS),
                  pl.BlockSpec((8, HD), lambda g, i: (0, 0)), pl.BlockSpec(memory_space=pl.ANY)],
        out_specs=[osp(HEADS), osp(0), pl.BlockSpec((hp, nc, HD, HD), lambda g, i: (g, i, 0, 0)),
                   pl.BlockSpec((hp, nc, CHUNK, CHUNK), lambda g, i: (g, i, 0, 0))],
        out_shape=[jax.ShapeDtypeStruct((s, D_MIX), bf16), jax.ShapeDtypeStruct((s, D_MODEL), f32),
                   jax.ShapeDtypeStruct((HEADS, s // CHUNK, HD, HD), f32),
                   jax.ShapeDtypeStruct((HEADS, s // CHUNK, CHUNK, CHUNK), f32)],
        scratch_shapes=[pltpu.VMEM((hp, HD, HD), f32)],
        input_output_aliases={12: 0},
        compiler_params=_params(("parallel", "arbitrary")),
    )(proj, proj, proj, proj, proj, proj, proj, proj, cw, cw, cw, dnv, y)


def _dn_bwd(dy, proj, o, states, tinvs, cw, dnv, name):
    s = proj.shape[0]
    sb = min(DN_SEQ_BLOCK, s)
    nsb = s // sb
    nc = sb // CHUNK
    hp = DN_HP
    ngrp = HEADS // hp

    def body(dy_ref, q_ref, qh_ref, k_ref, kh_ref, v_ref, vh_ref, z_ref, ba_ref, o_ref, st_ref, ti_ref,
             cwq_ref, cwk_ref, cwv_ref, dnv_ref,
             dq_ref, dk_ref, dv_ref, dz_ref, dba_ref, dcw_ref, dsc_ref,
             ds_ref, dch_ref, cwacc_ref, sacc_ref, nacc_ref):
        step = pl.program_id(0)
        grp = pl.program_id(1)
        sblk = nsb - 1 - step
        h0 = grp * hp
        masks = _dn_masks()
        lane, ri, ci, eye, ltri = masks
        utri = (ri <= ci).astype(f32)
        cw3s = [_dn_cw3(cwq_ref, cwk_ref, cwv_ref, j) for j in range(hp)]
        dnv_v = dnv_ref[...]
        dnw = dnv_v[2:3]
        row64 = lax.broadcasted_iota(jnp.int32, (CHUNK, HD), 0)

        @pl.when(step == 0)
        def _():
            for j in range(hp):
                ds_ref[h0 + j] = jnp.zeros((HD, HD), f32)
                dch_ref[h0 + j] = jnp.zeros((8, 3 * HD), f32)
                cwacc_ref[h0 + j] = jnp.zeros((8, 3 * HD), f32)
                sacc_ref[h0 + j] = jnp.zeros((8, HD), f32)

        @pl.when((step == 0) & (grp == 0))
        def _():
            nacc_ref[...] = jnp.zeros((8, HD), f32)

        def one_head(j, x3, halo3, bav, z, st, ti, ov, dyv, carry):
            dst, dch, cwacc, sa0, sa1 = carry
            hd = h0 + j
            cw3 = cw3s[j]
            f = yield from _dn_chunk(x3, halo3, bav, cw3, dnv_v, hd, masks, ti)
            q, k, v, beta, eg, ek, kbf = f["q"], f["k"], f["v"], f["beta"], f["eg"], f["ek"], f["kbf"]
            sbf = _b(st)
            dstb = _b(dst)
            qe = q * eg
            keg = k * eg
            kd = k * ek
            kegb, qeb, kdb = _b(keg), _b(qe), _b(kd)
            e = v - _dot(kegb, sbf)
            yield
            rhs = beta * e
            vn = rhs + _sdot(f["tinv"], rhs)
            yield
            vnb = _b(vn)
            pb = _b(f["pmat"])
            sgz = _sig(z)
            sz = z * sgz
            nrm = lax.rsqrt(_rowmean(ov * ov) + EPS)
            on = ov * nrm
            ddnw = _sum0(dyv * on * sz)
            dpz = dyv * on * dnw * _dsilu(z, sgz)
            don = dyv * dnw * sz
            do = nrm * (don - on * _rowmean(don * on))
            dob = _b(do)
            dvn = _dot_tn(pb, dob) + _dot(kdb, dstb)
            dpm = jnp.where(ri >= ci, _dot_nt(dob, vnb), 0.0)
            dqe = _dot_nt(dob, sbf)
            dkd = _dot_nt(vnb, dstb)
            qtdo = _dot_tn(qeb, dob)
            yield
            drhs = dvn + _sdot_tn(f["tinv"], dvn)
            dqk = _b(dpm * f["dm"])
            dq = _dot(dqk, kbf) + dqe * eg
            dk_p = _dot_tn(dqk, _b(q))
            yield
            dam = jnp.where(ri > ci, -_sdot_nt(drhs, vn), 0.0)
            de = drhs * beta
            deb = _b(de)
            dbeta = _sum1(drhs * e)
            dkeg = -_dot_nt(deb, sbf)
            dst_new = qtdo + dst * jnp.exp(f["glast"]) - _dot_tn(kegb, deb)
            yield
            dkk = _b(dam * f["ds"])
            dkb = _dot(dkk, kbf)
            dk = dk_p + _dot_tn(dkk, _b(f["kb"])) + dkb * beta + dkeg * eg + dkd * ek
            yield
            dbeta = dbeta + _sum1(dkb * k)
            mm = dpm * f["pmat"] + dam * f["amat"]
            dglast = _sum1(_sum0(dst * st)) * jnp.exp(f["glast"]) + _sum1(_sum0(dkd * kd))
            dgc = (_sum1(mm) - _sum1(eye * _sum0(mm)) + _sum1(dqe * qe) + _sum1(dkeg * keg) - _sum1(dkd * kd))
            dgc = jnp.broadcast_to(dgc, (CHUNK, HD)) + jnp.where(row64 == CHUNK - 1, dglast, 0.0)
            dg = _tri_dot(utri, dgc)
            yield
            dbin = jnp.broadcast_to(dbeta, (CHUNK, HD)) * beta * (1.0 - beta)
            dain = dg * (-f["ea"]) * _sig(f["a_in"] + f["dt"])
            dba = jnp.where(lane == hd, dbin, 0.0) + jnp.where(lane == HEADS + hd, dain, 0.0)
            sa0 = sa0 + _sum0(dg * f["g"])
            sa1 = sa1 + _sum0(dain)
            dq_raw = (QSCALE * f["rq"]) * (dq - f["qn"] * _sum1(f["qn"] * dq))
            dk_raw = f["rk"] * (dk - k * _sum1(k * dk))
            dact = jnp.concatenate([dq_raw, dk_raw, de], axis=1)
            dc = dact * _dsilu(f["cpre"], f["sg"])
            xs = f["xs"]
            cwacc = cwacc + jnp.concatenate([_sum0(dc * xs[0]), _sum0(dc * xs[1]), _sum0(dc * xs[2]),
                                             _sum0(dc * xs[3])], axis=0)
            dqkv = _conv_back(dc, dch, cw3)
            return dpz, dqkv, dba, ddnw, (dst_new, dc[0:8], cwacc, sa0, sa1)

        def chunk(cidx, carry):
            heads, nacc = carry
            c = nc - 1 - cidx
            t0 = pl.multiple_of(c * CHUNK, CHUNK)
            rows = pl.ds(t0, CHUNK)
            bav = ba_ref[rows, :]
            ins = []
            for j in range(hp):
                hl = slice(j * HD, (j + 1) * HD)
                ins.append(_dn_load_qkv(q_ref, qh_ref, k_ref, kh_ref, v_ref, vh_ref, hl, rows, t0, c, sblk)
                           + (bav, z_ref[rows, hl], st_ref[j, c], ti_ref[j, c], o_ref[rows, hl], dy_ref[rows, hl]))
            outs = _round_robin([one_head(j, *ins[j], heads[j]) for j in range(hp)])
            dba = outs[0][2]
            for j in range(hp):
                hl = slice(j * HD, (j + 1) * HD)
                dpz, dqkv, dba_j, ddnw, _ = outs[j]
                dq_ref[rows, hl] = _b(dqkv[:, 0:HD])
                dk_ref[rows, hl] = _b(dqkv[:, HD:2 * HD])
                dv_ref[rows, hl] = _b(dqkv[:, 2 * HD:3 * HD])
                dz_ref[rows, hl] = _b(dpz)
                nacc = nacc + ddnw
                if j > 0:
                    dba = dba + dba_j

            @pl.when(grp == 0)
            def _():
                dba_ref[rows, :] = dba

            @pl.when(grp > 0)
            def _():
                dba_ref[rows, :] += dba

            return tuple(out[4] for out in outs), nacc

        init = tuple((ds_ref[h0 + j], dch_ref[h0 + j], cwacc_ref[h0 + j][0:4], sacc_ref[h0 + j][0:1],
                      sacc_ref[h0 + j][1:2]) for j in range(hp))
        heads, nacc = lax.fori_loop(0, nc, chunk, (init, nacc_ref[0:1, :]))
        nacc_ref[0:1, :] = nacc
        zpad = jnp.zeros((4, 3 * HD), f32)
        zrow = jnp.zeros((6, HD), f32)
        for j in range(hp):
            dst, dch, cwacc, sa0, sa1 = heads[j]
            ds_ref[h0 + j] = dst
            dch_ref[h0 + j] = dch
            cwacc_ref[h0 + j] = jnp.concatenate([cwacc, zpad], axis=0)
            sacc_ref[h0 + j] = jnp.concatenate([sa0, sa1, zrow], axis=0)

        @pl.when(step == nsb - 1)
        def _():
            lane8 = lax.broadcasted_iota(jnp.int32, (8, HD), 1)
            row8 = lax.broadcasted_iota(jnp.int32, (8, HD), 0)
            mine = jnp.zeros((8, HD), f32)
            for j in range(hp):
                dcw_ref[h0 + j] = heads[j][2]
                sa = jnp.concatenate([heads[j][3], heads[j][4], zrow], axis=0)
                mine = mine + jnp.where((lane8 == h0 + j) & (row8 < 2), sa, 0.0)

            @pl.when(grp == 0)
            def _():
                dsc_ref[...] = mine

            @pl.when(grp > 0)
            def _():
                dsc_ref[...] += mine

            @pl.when(grp == ngrp - 1)
            def _():
                dsc_ref[2:3, :] = nacc

    wide = hp * HD
    rblk = lambda i, g: nsb - 1 - i
    grp_specs = lambda off: _seq_specs(sb, wide, rblk, lambda i, g: off // hp + g)
    (qsp, qhalo), (ksp, khalo), (vsp, vhalo) = grp_specs(2 * HEADS), grp_specs(3 * HEADS), grp_specs(4 * HEADS)
    zsp, _ = grp_specs(5 * HEADS)
    hsp = lambda off: pl.BlockSpec((sb, wide), lambda i, g: (rblk(i, g), off // hp + g))
    basp = lambda col: pl.BlockSpec((sb, HD), lambda i, g: (rblk(i, g), col))
    cws = lambda off: pl.BlockSpec((4, wide), lambda i, g: (0, off // hp + g))
    whole = lambda shape: pl.BlockSpec(shape, lambda i, g: (0,) * len(shape))
    return pl.pallas_call(
        body, name=name, grid=(nsb, ngrp),
        in_specs=[hsp(HEADS), qsp, qhalo, ksp, khalo, vsp, vhalo, zsp, basp(BLK_BA), hsp(0),
                  pl.BlockSpec((hp, nc, HD, HD), lambda i, g: (g, rblk(i, g), 0, 0)),
                  pl.BlockSpec((hp, nc, CHUNK, CHUNK), lambda i, g: (g, rblk(i, g), 0, 0)),
                  cws(0), cws(HEADS), cws(2 * HEADS), whole((8, HD))],
        out_specs=[hsp(0), hsp(0), hsp(0), hsp(0), basp(0), whole((HEADS, 4, 3 * HD)), whole((8, HD))],
        out_shape=[jax.ShapeDtypeStruct((s, D_MODEL), bf16)] * 4
        + [jax.ShapeDtypeStruct((s, HD), f32), jax.ShapeDtypeStruct((HEADS, 4, 3 * HD), f32),
           jax.ShapeDtypeStruct((8, HD), f32)],
        scratch_shapes=[pltpu.VMEM((HEADS, HD, HD), f32), pltpu.VMEM((HEADS, 8, 3 * HD), f32),
                        pltpu.VMEM((HEADS, 8, 3 * HD), f32), pltpu.VMEM((HEADS, 8, HD), f32), pltpu.VMEM((8, HD), f32)],
        compiler_params=_params(("arbitrary", "arbitrary")),
    )(dy, proj, proj, proj, proj, proj, proj, proj, proj, o, states, tinvs, cw, cw, cw, dnv)


ANY = pl.BlockSpec(memory_space=pl.ANY)
CHIP_MASKS = ((1, 0, 0), (0, 1, 0), (1, 1, 0))
ALL_MASKS = tuple((m >> 2 & 1, m >> 1 & 1, m & 1) for m in range(1, N_DEV))


def _me():
    return lax.axis_index("x"), lax.axis_index("y"), lax.axis_index("c")


def _flip(me, mask):
    return tuple((1 - v) if m else v for v, m in zip(me, mask))


def _dev_index(dev):
    return 4 * dev[0] + 2 * dev[1] + dev[2]


def _chip(dev):
    return 2 * dev[0] + dev[1]


def _comm_call(name, body, arrays, out_shapes, nsem, nloc=0):
    scratch = [pltpu.SemaphoreType.DMA((nsem,)), pltpu.SemaphoreType.DMA((nsem,))]
    if nloc:
        scratch.append(pltpu.SemaphoreType.DMA((nloc,)))
    return pl.pallas_call(
        body, name=name, in_specs=[ANY] * len(arrays), out_specs=[ANY] * len(out_shapes), out_shape=out_shapes,
        scratch_shapes=scratch, compiler_params=pltpu.CompilerParams(has_side_effects=True),
    )(*arrays)


def _put_own(gathered, own, slot):
    return lax.dynamic_update_index_in_dim(gathered, own, slot, 0)


def _gather_halves(arrays, whole_arrays, name):
    na, nw = len(arrays), len(whole_arrays)
    nm = len(CHIP_MASKS)

    def body(*refs):
        srcs, dsts = refs[:na + nw], refs[na + nw:2 * (na + nw)]
        send_sems, recv_sems = refs[2 * (na + nw):]
        me = _me()
        sib = _flip(me, (0, 0, 1))

        def half(a, ref, core):
            rows = arrays[a].shape[0] // 2
            return ref.at[pl.ds(pl.multiple_of(core * rows, rows), rows)]

        def over_ici(a, mi, sender, to):
            if a < na:
                src, dst = half(a, srcs[a], sender[2]), half(a, dsts[a].at[_chip(sender)], sender[2])
            else:
                src, dst = srcs[a], dsts[a].at[_chip(sender)]
            return pltpu.make_async_remote_copy(src_ref=src, dst_ref=dst, send_sem=send_sems.at[a * nm + mi],
                                                recv_sem=recv_sems.at[a * nm + mi], device_id=to, device_id_type=MESH)

        def over_d2d(a, mi, origin, sender, to):
            k = (na + nw) * nm + a * nm + mi
            blk = half(a, dsts[a].at[_chip(origin)], sender[2])
            return pltpu.make_async_remote_copy(src_ref=blk, dst_ref=blk, send_sem=send_sems.at[k],
                                                recv_sem=recv_sems.at[k], device_id=to, device_id_type=MESH)

        sends = []
        for a in range(na + nw):
            for mi, mask in enumerate(CHIP_MASKS):
                cp = over_ici(a, mi, me, _flip(me, mask))
                cp.start()
                sends.append(cp)
        for a in range(na + nw):
            for mi, mask in enumerate(CHIP_MASKS):
                peer = _flip(me, mask)
                over_ici(a, mi, peer, me).wait_recv()
                if a < na:
                    cp = over_d2d(a, mi, peer, me, sib)
                    cp.start()
                    sends.append(cp)
        for a in range(na):
            for mi, mask in enumerate(CHIP_MASKS):
                over_d2d(a, mi, _flip(sib, mask), sib, me).wait_recv()
        for cp in sends:
            cp.wait_send()

    every = list(arrays) + list(whole_arrays)
    got = _comm_call(name, body, every, [jax.ShapeDtypeStruct((4,) + t.shape, t.dtype) for t in every],
                     (2 * na + nw) * nm)
    chip = 2 * lax.axis_index("x") + lax.axis_index("y")
    return [_put_own(g, t, chip) for g, t in zip(got, every)]


HBM = pl.BlockSpec(memory_space=pltpu.HBM)
SEM = pl.BlockSpec(memory_space=pltpu.SEMAPHORE)
DATAFLOW = pltpu.SideEffectType.DATAFLOW_SIDE_EFFECTING


def _split_start(name, srcs, land_shapes, nsem, issue):
    ns, nl = len(srcs), len(land_shapes)

    def body(*refs):
        issue(refs[:ns], refs[ns:ns + nl], refs[2 * (ns + nl)], refs[2 * (ns + nl) + 1])
        token = refs[-1]
        token[...] = jnp.zeros_like(token)

    lands = [lax.empty(t.shape, t.dtype) for t in land_shapes]
    thru = [pltpu.HBM(t.shape, t.dtype) for t in list(srcs) + list(land_shapes)]
    hbm = lambda t: pltpu.with_memory_space_constraint(t, pltpu.HBM)
    return pl.pallas_call(
        body, name=name, in_specs=[HBM] * (ns + nl),
        out_shape=(*thru, pltpu.SemaphoreType.DMA((nsem,)), pltpu.SemaphoreType.DMA((nsem,)),
                   jax.ShapeDtypeStruct((8, HD), f32)),
        out_specs=(*[HBM] * (ns + nl), SEM, SEM, pl.BlockSpec(memory_space=pltpu.VMEM)),
        input_output_aliases={i: i for i in range(ns + nl)},
        compiler_params=pltpu.CompilerParams(has_side_effects=DATAFLOW),
    )(*[hbm(t) for t in srcs], *[hbm(t) for t in lands])


def _split_wait(name, started, ns, after, finish):
    thru, send_sems, recv_sems = started[:-3], started[-3], started[-2]
    nt = len(thru)

    def body(*refs):
        finish(refs[:ns], refs[ns:nt], refs[nt], refs[nt + 1])

    outs = pl.pallas_call(
        body, name=name, in_specs=[HBM] * nt + [SEM, SEM, ANY],
        out_shape=tuple(pltpu.HBM(t.shape, t.dtype) for t in thru), out_specs=tuple([HBM] * nt),
        input_output_aliases={i: i for i in range(nt)},
        compiler_params=pltpu.CompilerParams(has_side_effects=DATAFLOW),
    )(*thru, send_sems, recv_sems, after)
    return list(outs[ns:])


def _gather_start(arrays, name):
    nm = len(CHIP_MASKS)

    def issue(srcs, lands, send_sems, recv_sems):
        me = _me()
        for a in range(len(arrays)):
            for mi, mask in enumerate(CHIP_MASKS):
                pltpu.make_async_remote_copy(
                    src_ref=srcs[a], dst_ref=lands[a].at[_chip(me)], send_sem=send_sems.at[a * nm + mi],
                    recv_sem=recv_sems.at[a * nm + mi], device_id=_flip(me, mask), device_id_type=MESH).start()

    shapes = [jax.ShapeDtypeStruct((4,) + t.shape, t.dtype) for t in arrays]
    return _split_start(name, arrays, shapes, len(arrays) * nm, issue)


def _gather_finish(started, arrays, after, name):
    nm = len(CHIP_MASKS)

    def finish(srcs, lands, send_sems, recv_sems):
        me = _me()
        for a in range(len(arrays)):
            for mi, mask in enumerate(CHIP_MASKS):
                peer = _flip(me, mask)
                cp = pltpu.make_async_remote_copy(
                    src_ref=srcs[a], dst_ref=lands[a].at[_chip(peer)], send_sem=send_sems.at[a * nm + mi],
                    recv_sem=recv_sems.at[a * nm + mi], device_id=peer, device_id_type=MESH)
                cp.wait_send()
                cp.wait_recv()

    got = _split_wait(name, started, len(arrays), after, finish)
    chip = 2 * lax.axis_index("x") + lax.axis_index("y")
    return [_put_own(g, t, chip) for g, t in zip(got, arrays)]


HALF_IN, HALF_OUT = D_MODEL // 2, D_MIX // 8


def _scatter_piece(kind, ref, d):
    j, r = d >> 1, d & 1
    if kind == "win":
        return ref.at[pl.ds(HALF_IN * r, HALF_IN), pl.ds(WIN_STRIDE * j, WIN_W)]
    if kind == "wout":
        return ref.at[pl.ds(2 * HALF_OUT * j + HALF_OUT * r, HALF_OUT)]
    return ref.at[d]


def _scatter_start(arrays, kinds, name):
    na = len(arrays)

    def issue(srcs, lands, send_sems, recv_sems):
        me = _me()
        my = _dev_index(me)
        for d in range(N_DEV):
            dev = (d >> 2, (d >> 1) & 1, d & 1)
            for a in range(na):
                @pl.when(my != d)
                def _(a=a, d=d, dev=dev):
                    pltpu.make_async_remote_copy(
                        src_ref=_scatter_piece(kinds[a], srcs[a], d), dst_ref=lands[a].at[my],
                        send_sem=send_sems.at[a * N_DEV + d], recv_sem=recv_sems.at[a * N_DEV + my],
                        device_id=dev, device_id_type=MESH).start()

    shape_of = {"win": (N_DEV, HALF_IN, WIN_W), "wout": (N_DEV, HALF_OUT, D_MODEL)}
    shapes = [jax.ShapeDtypeStruct(shape_of.get(k, t.shape), t.dtype) for t, k in zip(arrays, kinds)]
    return _split_start(name, arrays, shapes, na * N_DEV, issue)


def _scatter_finish(started, arrays, kinds, after, name):
    na = len(arrays)

    def finish(srcs, lands, send_sems, recv_sems):
        me = _me()
        my = _dev_index(me)
        for s in range(N_DEV):
            for a in range(na):
                @pl.when(my != s)
                def _(a=a, s=s):
                    cp = pltpu.make_async_remote_copy(
                        src_ref=_scatter_piece(kinds[a], srcs[a], s), dst_ref=lands[a].at[s],
                        send_sem=send_sems.at[a * N_DEV + s], recv_sem=recv_sems.at[a * N_DEV + s],
                        device_id=me, device_id_type=MESH)
                    cp.wait_recv()
                    cp.wait_send()

    got = _split_wait(name, started, na, after, finish)
    x, y, c = _me()
    chip, my = 2 * x + y, 4 * x + 2 * y + c
    own = {"win": lambda t: lax.dynamic_slice(t, (HALF_IN * c, WIN_STRIDE * chip), (HALF_IN, WIN_W)),
           "wout": lambda t: lax.dynamic_slice(t, (2 * HALF_OUT * chip + HALF_OUT * c, 0), (HALF_OUT, D_MODEL)),
           "small": lambda t: lax.dynamic_index_in_dim(t, my, 0, keepdims=False)}
    return [_put_own(g, own[k](t), my) for g, t, k in zip(got, arrays, kinds)]


def _share_reduced(pair_arrays, small, name):
    arrays = list(pair_arrays) + ([small] if small is not None else [])
    na, npair = len(arrays), len(pair_arrays)
    nm = len(ALL_MASKS)

    def body(*refs):
        srcs, dsts = refs[:na], refs[na:2 * na]
        send_sems, recv_sems = refs[2 * na:]
        me = _me()
        sib = _flip(me, (0, 0, 1))

        def copy(a, k, sender, to):
            slot = sender[2] if a < npair else _dev_index(sender)
            return pltpu.make_async_remote_copy(
                src_ref=srcs[a], dst_ref=dsts[a].at[slot], send_sem=send_sems.at[k], recv_sem=recv_sems.at[k],
                device_id=to, device_id_type=MESH)

        sends = [copy(a, a, me, sib) for a in range(npair)]
        if small is not None:
            sends += [copy(npair, npair + mi, me, _flip(me, mask)) for mi, mask in enumerate(ALL_MASKS)]
        for cp in sends:
            cp.start()
        for a in range(npair):
            copy(a, a, sib, me).wait_recv()
        if small is not None:
            for mi, mask in enumerate(ALL_MASKS):
                copy(npair, npair + mi, _flip(me, mask), me).wait_recv()
        for cp in sends:
            cp.wait_send()

    shapes = [jax.ShapeDtypeStruct((2,) + t.shape, t.dtype) for t in pair_arrays]
    if small is not None:
        shapes.append(jax.ShapeDtypeStruct((N_DEV,) + small.shape, small.dtype))
    got = _comm_call(name, body, arrays, shapes, npair + (nm if small is not None else 0))
    x, y, c = _me()
    slots = [c] * npair + [4 * x + 2 * y + c]
    return [_put_own(g, t, slot) for g, t, slot in zip(got, arrays, slots)]


def _sum_parts(parts, name):
    _, r, c = parts.shape
    tr = r
    while tr * c * 4 * N_DEV > (8 << 20) and tr % 32 == 0:
        tr //= 2

    def body(p_ref, o_ref):
        total = p_ref[0].astype(f32)
        for d in range(1, N_DEV):
            total = total + p_ref[d].astype(f32)
        o_ref[...] = total

    return pl.pallas_call(
        body, name=name, grid=(r // tr,), in_specs=[pl.BlockSpec((N_DEV, tr, c), lambda i: (0, i, 0))],
        out_specs=pl.BlockSpec((tr, c), lambda i: (i, 0)), out_shape=jax.ShapeDtypeStruct((r, c), f32),
        compiler_params=_params(("parallel",)),
    )(parts)


def _adamw(w, g, m, v, name):
    shape = w.shape
    cols = shape[-1]
    lead = shape[0] if w.ndim == 3 else 1
    rows = w.size // (cols * lead)
    tr = rows
    while tr * cols * 4 > (1 << 20) and tr % 16 == 0:
        tr //= 2
    c1 = 1.0 / (1.0 - ADAM_B1 ** ADAM_STEP)
    c2 = 1.0 / (1.0 - ADAM_B2 ** ADAM_STEP)

    def body(w_ref, g_ref, m_ref, v_ref, d_ref, nm_ref, nv_ref):
        gv = g_ref[...]
        mv = ADAM_B1 * m_ref[...] + (1.0 - ADAM_B1) * gv
        vv = ADAM_B2 * v_ref[...] + (1.0 - ADAM_B2) * (gv * gv)
        nm_ref[...] = mv
        nv_ref[...] = vv
        d_ref[...] = -ADAM_LR * ((mv * c1) / (jnp.sqrt(vv * c2) + ADAM_EPS) + ADAM_WD * w_ref[...])

    if w.ndim == 3:
        spec = pl.BlockSpec((None, tr, cols), lambda l, i: (l, i, 0))
        return tuple(pl.pallas_call(
            body, name=name, grid=(lead, rows // tr), in_specs=[spec] * 4, out_specs=[spec] * 3,
            out_shape=[jax.ShapeDtypeStruct(shape, f32)] * 3, compiler_params=_params(("parallel", "parallel")),
        )(w, g, m, v))
    spec = pl.BlockSpec((tr, cols), lambda i: (i, 0))
    outs = pl.pallas_call(
        body, name=name, grid=(rows // tr,), in_specs=[spec] * 4, out_specs=[spec] * 3,
        out_shape=[jax.ShapeDtypeStruct((rows, cols), f32)] * 3, compiler_params=_params(("parallel",)),
    )(*[t.reshape(rows, cols) for t in (w, g, m, v)])
    return tuple(t.reshape(shape) for t in outs)


def _pad_lanes(t, n=HD):
    return jnp.pad(t, [(0, 0)] * (t.ndim - 1) + [(0, n - t.shape[-1])])


def _rows128(t, rows=None):
    t = t.reshape(-1, HD)
    rows = rows or -(-t.shape[0] // 8) * 8
    return jnp.pad(t, ((0, rows - t.shape[0]), (0, 0)))


def kernel(x, norm_w, w_in, lru_conv_w, lru_conv_b, lru_wa, lru_ba, lru_wx, lru_bx, lru_lambda, lru_norm_w, dn_conv_w, dn_A_log, dn_dt_bias, dn_norm_w, w_out, final_norm_w, loss_target, m_norm_w, m_w_in, m_lru_conv_w, m_lru_conv_b, m_lru_wa, m_lru_ba, m_lru_wx, m_lru_bx, m_lru_lambda, m_lru_norm_w, m_dn_conv_w, m_dn_A_log, m_dn_dt_bias, m_dn_norm_w, m_w_out, m_final_norm_w, v_norm_w, v_w_in, v_lru_conv_w, v_lru_conv_b, v_lru_wa, v_lru_ba, v_lru_wx, v_lru_bx, v_lru_lambda, v_lru_norm_w, v_dn_conv_w, v_dn_A_log, v_dn_dt_bias, v_dn_norm_w, v_w_out, v_final_norm_w):
    depth = norm_w.shape[0]
    xs = x[0]
    target = loss_target[0]
    chip = 2 * lax.axis_index("x") + lax.axis_index("y")

    win_b, wout_b = w_in.astype(bf16), w_out.astype(bf16)
    got_in0 = _gather_halves([win_b[0]], [], "gather_weights0")[0]
    rest0 = [wout_b[0], lru_conv_w, dn_conv_w]
    rest0_started = _gather_start(rest0, "gather_rest0_start")
    later = {l: _gather_start([win_b[l], wout_b[l]], f"gather_weights{l}_start") for l in range(1, depth)}
    gathered = {}
    pad_cols = jnp.zeros((D_MODEL, D_INP - D_IN), bf16)

    def weights_of(l, after):
        if l == 0:
            token = rest0_started[-1]
            for started in later.values():
                token = token + started[-1]
            got = got_in0
        else:
            gathered[l] = _gather_finish(later[l], [win_b[l], wout_b[l]], after, f"gather_weights{l}_wait")
            got, token = gathered[l][0], None
        return jnp.concatenate([got[j] for j in range(4)] + [pad_cols], axis=1), token

    def rest_of(l, after):
        if "rest0" not in gathered:
            gathered["rest0"] = _gather_finish(rest0_started, rest0, after, "gather_rest0_wait")
        g_out, g_lcw, g_dcw = gathered["rest0"]
        if l > 0:
            g_out = gathered[l][1]
        lcw_full = g_lcw.transpose(1, 2, 0, 3).reshape(depth, 4, D_MODEL)
        dcw_full = g_dcw.transpose(1, 2, 0, 3).reshape(depth, 4, 3 * D_MODEL)
        return g_out.reshape(D_MIX, D_MODEL), lcw_full[l], dcw_full[l]

    in_flight = {}

    def on_grad(kind, l, g):
        if l == depth - 1 and depth > 1:
            if kind == "wout":
                in_flight["held"] = g
                return None
            srcs, kinds, key = [g, in_flight.pop("held")], ["win", "wout"], ("both", l)
        else:
            srcs, kinds, key = [g], [kind], (kind, l)
        started = _scatter_start(srcs, kinds, f"scatter_{key[0]}{key[1]}_start")
        in_flight[key] = (started, srcs, kinds)
        return started[-1]

    vecs, dnvs = [], []
    zrow = jnp.zeros((D_MODEL,), f32)
    for l in range(depth):
        vecs.append(jnp.stack([lru_conv_b[l], lru_ba[l], lru_bx[l], lru_lambda[l], lru_norm_w[l], zrow, zrow, zrow]))
        dnvs.append(jnp.concatenate([_pad_lanes(dn_A_log[l][None]), _pad_lanes(dn_dt_bias[l][None]),
                                     dn_norm_w[l][None], jnp.zeros((5, HD), f32)], axis=0))

    loss_part, grad_x, d_fnw, g_nw, g_vec, g_wa, g_wx, g_lcw_, g_dcw_, g_dsc = _local_step(
        xs, target, norm_w, final_norm_w, weights_of, rest_of, on_grad, vecs, dnvs, lru_wa, lru_wx)
    loss = lax.psum(loss_part[0, 0], ("x", "y", "c"))
    grads = _reduce_grads(chip, grad_x, in_flight, d_fnw, g_nw, g_vec, g_wa, g_wx, g_lcw_, g_dcw_, g_dsc)

    names = ["norm_w", "w_in", "lru_conv_w", "lru_conv_b", "lru_wa", "lru_ba", "lru_wx", "lru_bx", "lru_lambda",
             "lru_norm_w", "dn_conv_w", "dn_A_log", "dn_dt_bias", "dn_norm_w", "w_out", "final_norm_w"]
    ws = dict(zip(names, [norm_w, w_in, lru_conv_w, lru_conv_b, lru_wa, lru_ba, lru_wx, lru_bx, lru_lambda, lru_norm_w,
                          dn_conv_w, dn_A_log, dn_dt_bias, dn_norm_w, w_out, final_norm_w]))
    ms = dict(zip(names, [m_norm_w, m_w_in, m_lru_conv_w, m_lru_conv_b, m_lru_wa, m_lru_ba, m_lru_wx, m_lru_bx,
                          m_lru_lambda, m_lru_norm_w, m_dn_conv_w, m_dn_A_log, m_dn_dt_bias, m_dn_norm_w, m_w_out,
                          m_final_norm_w]))
    vs = dict(zip(names, [v_norm_w, v_w_in, v_lru_conv_w, v_lru_conv_b, v_lru_wa, v_lru_ba, v_lru_wx, v_lru_bx,
                          v_lru_lambda, v_lru_norm_w, v_dn_conv_w, v_dn_A_log, v_dn_dt_bias, v_dn_norm_w, v_w_out,
                          v_final_norm_w]))
    deltas, new_m, new_v = [], [], []
    for n in names:
        d, nm_, nv_ = _adamw(ws[n], grads[n], ms[n], vs[n], f"adamw_{n}")
        deltas.append(d)
        new_m.append(nm_)
        new_v.append(nv_)
    return (loss, grad_x[None], *[grads[n] for n in names], *deltas, *new_m, *new_v)


def _behind(t, token):
    return t if token is None else t + token[0:1, 0:1]


def _local_step(xs, target, norm_w, final_norm_w, weights_of, rest_of, on_grad, vecs, dnvs, lru_wa, lru_wx):
    depth = norm_w.shape[0]
    saved = []
    h = xs
    for l in range(depth):
        wp_l, token = weights_of(l, h)
        hn = _rms_fwd(h, _behind(norm_w[l][None], token), f"rms_fwd{l}")
        proj = _matmul(hn, wp_l, tm=2048, tn=896, tk=D_MODEL, name=f"in_proj{l}")
        wo_l, lcw_l, dcw_l = rest_of(l, proj)
        y, hl = _lru_fwd(proj, lcw_l, vecs[l], lru_wa[l], lru_wx[l], f"lru_fwd{l}")
        y, o, states, tinvs = _dn_fwd(proj, dcw_l, dnvs[l], y, f"dn_fwd{l}")
        out = _matmul(y, wo_l, tm=512, tn=D_MODEL, tk=D_MIX, res=h, name=f"out_proj{l}")
        saved.append((h, hn, proj, hl, o, states, tinvs, y, wp_l, wo_l, lcw_l, dcw_l))
        h = out

    dh, d_fnw, loss_part = _final_loss(h, final_norm_w[None], target, "final_loss")

    g_nw, g_vec, g_wa, g_wx, g_lcw_, g_dcw_, g_dsc = ([None] * depth for _ in range(7))
    for l in reversed(range(depth)):
        hin, hn, proj, hl, o, states, tinvs, y, wp_l, wo_l, lcw_l, dcw_l = saved[l]
        dy = _matmul(dh, wo_l, tb=True, tm=512, tn=1024, tk=D_MODEL, name=f"d_y{l}")
        g_wout = _matmul(y, dh, ta=True, tm=512, tn=D_MODEL, tk=512, out_dtype=bf16, name=f"d_wout{l}")
        token = on_grad("wout", l, g_wout)
        dlx, dlz, g_lcw_[l], g_vec[l], g_wa[l], g_wx[l] = _lru_bwd(dy, proj, hl, lcw_l, _behind(vecs[l], token),
                                                                 lru_wa[l], lru_wx[l], f"lru_bwd{l}")
        dq, dk, dv, dz, dba, dcw8, g_dsc[l] = _dn_bwd(dy, proj, o, states, tinvs, dcw_l, dnvs[l], f"dn_bwd{l}")
        g_dcw_[l] = dcw8.reshape(HEADS, 4, 3, HD).transpose(1, 2, 0, 3).reshape(4, 3 * D_MODEL)
        dxs = [dlx, dlz, dq, dk, dv, dz]
        g_win = _d_win(_transpose(hn, f"hn_t{l}"), dxs, dba, f"d_win{l}")
        token = on_grad("win", l, g_win)
        dhn = _d_hn(dxs, _behind(dba, token), wp_l, f"d_hn{l}")
        dh, g_nw[l] = _rms_bwd(dhn, hin, norm_w[l][None], dh, f"rms_bwd{l}")
    return loss_part, dh, d_fnw, g_nw, g_vec, g_wa, g_wx, g_lcw_, g_dcw_, g_dsc


def _reduce_grads(chip, after, in_flight, d_fnw, g_nw, g_vec, g_wa, g_wx, g_lcw_, g_dcw_, g_dsc):
    depth = len(g_nw)
    both = lambda f: jnp.concatenate([f(l) for l in range(depth)])
    small = [
        both(lambda l: _rows128(g_nw[l])),
        both(lambda l: _rows128(g_vec[l][0])), both(lambda l: _rows128(g_vec[l][1])),
        both(lambda l: _rows128(g_vec[l][2])), both(lambda l: _rows128(g_vec[l][3])),
        both(lambda l: _rows128(g_vec[l][4])),
        both(lambda l: _rows128(g_wa[l])), both(lambda l: _rows128(g_wx[l])),
        both(lambda l: _rows128(g_dsc[l][0:1])), both(lambda l: _rows128(g_dsc[l][1:2])),
        both(lambda l: _rows128(g_dsc[l][2:3])),
        _rows128(d_fnw),
        both(lambda l: _rows128(g_lcw_[l])), both(lambda l: _rows128(g_dcw_[l])),
    ]
    counts = [t.shape[0] for t in small]
    total = sum(counts)
    per = -(-total // (8 * N_DEV)) * 8
    small = jnp.concatenate(small + [jnp.zeros((per * N_DEV - total, HD), f32)]).reshape(N_DEV, per, HD)

    small_started = _scatter_start([small], ["small"], "scatter_small_start")
    reduced = {}
    for key in sorted(in_flight, key=lambda k: -k[1]):
        started, srcs, kinds = in_flight[key]
        got = _scatter_finish(started, srcs, kinds, after, f"scatter_{key[0]}{key[1]}_wait")
        for kind, part in zip(kinds, got):
            reduced[(kind, key[1])] = _sum_parts(part, f"sum_{kind}{key[1]}")
    got = _scatter_finish(small_started, [small], ["small"], after, "scatter_small_wait")
    pairs = [None] * depth
    a_small = None
    for l in reversed(range(depth)):
        sums = [reduced[("win", l)], reduced[("wout", l)]]
        if l == 0:
            shared = _share_reduced(sums, _sum_parts(got[0], "sum_small"), f"share_grads{l}")
            a_small = shared[2].reshape(N_DEV * per, HD)
        else:
            shared = _share_reduced(sums, None, f"share_grads{l}")
        pairs[l] = shared
    grad_w_in = jnp.stack([lax.dynamic_slice_in_dim(pairs[l][0].reshape(D_MODEL, WIN_W), 4 * chip, SHARD_IN, 1)
                           for l in range(depth)])
    grad_w_out = jnp.stack([pairs[l][1].reshape(D_MIX // 4, D_MODEL) for l in range(depth)])

    offs = [0]
    for c in counts:
        offs.append(offs[-1] + c)

    def piece(k, rows_per_layer=None):
        t = a_small[offs[k]:offs[k + 1]]
        if rows_per_layer is None:
            return t
        return t.reshape(depth, -1, HD)[:, :rows_per_layer]

    vec = lambda k: piece(k).reshape(depth, D_MODEL)
    return {
        "norm_w": vec(0), "lru_conv_b": vec(1), "lru_ba": vec(2), "lru_bx": vec(3), "lru_lambda": vec(4),
        "lru_norm_w": vec(5),
        "lru_wa": piece(6).reshape(depth, HEADS, HD, HD), "lru_wx": piece(7).reshape(depth, HEADS, HD, HD),
        "dn_A_log": piece(8, 1)[:, 0, :HEADS], "dn_dt_bias": piece(9, 1)[:, 0, :HEADS], "dn_norm_w": piece(10, 1)[:, 0],
        "final_norm_w": piece(11).reshape(D_MODEL),
        "lru_conv_w": lax.dynamic_slice_in_dim(piece(12).reshape(depth, 4, D_MODEL), chip * (D_MODEL // 4), D_MODEL // 4, 2),
        "dn_conv_w": lax.dynamic_slice_in_dim(piece(13).reshape(depth, 4, 3 * D_MODEL), chip * (3 * D_MODEL // 4),
                                              3 * D_MODEL // 4, 2),
        "w_in": grad_w_in, "w_out": grad_w_out,
    }
```

```python
import jax
import jax.numpy as jnp
from jax import lax
from jax.experimental import pallas as pl
from jax.experimental.pallas import tpu as pltpu

f32 = jnp.float32
bf16 = jnp.bfloat16
HI = lax.Precision.HIGHEST
MESH = pl.DeviceIdType.MESH

D_MODEL = 1024
HEADS = 8
HD = 128
CHUNK = 64
LRU_T = 128
SEQ_BLOCK = 1024
LRU_HP = 4
DN_SEQ_BLOCK = 256
DN_HP = 8
LRU_C = 8.0
D_IN = 6160
D_INP = 6272
D_MIX = 2048
N_GROUPS = 6
BLK_BA = 48
SHARD_IN = D_IN // 4
WIN_W = 1664
WIN_STRIDE = 1536
EPS = 1e-6
QSCALE = HD ** -0.5
N_DEV = 8
VMEM_LIMIT = 56 * 1024 * 1024

ADAM_LR, ADAM_B1, ADAM_B2, ADAM_EPS, ADAM_WD, ADAM_STEP = 0.001, 0.9, 0.999, 1e-08, 0.01, 10


def _sig(x):
    return 1.0 / (1.0 + jnp.exp(-x))


def _log1p(u):
    w = 1.0 + u
    d = w - 1.0
    return jnp.where(d == 0.0, u, jnp.log(w) * (u / jnp.where(d == 0.0, 1.0, d)))


def _softplus(x):
    return jnp.maximum(x, 0.0) + _log1p(jnp.exp(-jnp.abs(x)))


def _expm1(x):
    t = x * (1.0 + x * (0.5 + x * (1.0 / 6 + x * (1.0 / 24 + x * (1.0 / 120 + x * (1.0 / 720))))))
    return jnp.where(jnp.abs(x) < 0.2, t, jnp.exp(x) - 1.0)


def _dot(a, b, prec=None):
    return jnp.dot(a, b, precision=prec, preferred_element_type=f32)


def _dot_nt(a, b, prec=None):
    return lax.dot_general(a, b, (((1,), (1,)), ((), ())), precision=prec, preferred_element_type=f32)


def _dot_tn(a, b, prec=None):
    return lax.dot_general(a, b, (((0,), (0,)), ((), ())), precision=prec, preferred_element_type=f32)


def _b(x):
    return x.astype(bf16)


def _sum0(x):
    return jnp.sum(x, axis=0, keepdims=True)


def _sum1(x):
    return jnp.sum(x, axis=1, keepdims=True)


def _rowmean(x):
    return jnp.mean(x, axis=-1, keepdims=True)


def _dsilu(z, sg):
    return sg * (1.0 + z * (1.0 - sg))


def _conv_taps(x, halo):
    xx = jnp.concatenate([halo, x], axis=0)
    return [pltpu.roll(xx, 3, axis=0)[8:], pltpu.roll(xx, 2, axis=0)[8:], pltpu.roll(xx, 1, axis=0)[8:], x]


def _conv(xs, cw):
    return cw[0:1] * xs[0] + cw[1:2] * xs[1] + cw[2:3] * xs[2] + cw[3:4] * xs[3]


def _rows_before(ref, halo_ref, lanes, t0, c, sblk):
    tprev = pl.multiple_of(jnp.maximum(t0 - 8, 0), 8)
    return jnp.where(c > 0, ref[pl.ds(tprev, 8), lanes], jnp.where(sblk > 0, halo_ref[:, lanes], 0.0))


def _conv_back(dxc, halo_after, cw):
    rows = dxc.shape[0]
    dd = jnp.concatenate([dxc, halo_after], axis=0)
    out = cw[3:4] * dxc
    for s in (1, 2, 3):
        out = out + cw[3 - s:4 - s] * pltpu.roll(dd, rows + 8 - s, axis=0)[:rows]
    return out


def _params(sem=None):
    return pltpu.CompilerParams(dimension_semantics=sem, vmem_limit_bytes=VMEM_LIMIT)


def _seq_specs(sb, width, rowblk, colblk):
    main = pl.BlockSpec((sb, width), lambda a, b: (rowblk(a, b), colblk(a, b)))
    halo = pl.BlockSpec((8, width), lambda a, b: (jnp.maximum(rowblk(a, b) * (sb // 8) - 1, 0), colblk(a, b)))
    return main, halo


def _matmul(a, b, *, ta=False, tb=False, tm, tn, tk, res=None, out_dtype=f32, name):
    if ta:
        kdim, m = a.shape
    else:
        m, kdim = a.shape
    n = b.shape[0] if tb else b.shape[1]
    tm, tn, tk = min(tm, m), min(tn, n), min(tk, kdim)
    assert m % tm == 0 and n % tn == 0 and kdim % tk == 0, (name, m, n, kdim, tm, tn, tk)
    nk = kdim // tk
    dims = (((0 if ta else 1,), (1 if tb else 0,)), ((), ()))
    has_res = res is not None

    def body(*refs):
        a_ref, b_ref = refs[:2]
        r_ref = refs[2] if has_res else None
        o_ref = refs[3] if has_res else refs[2]
        acc_ref = refs[-1] if nk > 1 else None
        part = lax.dot_general(_b(a_ref[...]), _b(b_ref[...]), dims, preferred_element_type=f32)

        def finish(total):
            if has_res:
                total = total + r_ref[...]
            o_ref[...] = total.astype(out_dtype)

        if nk == 1:
            finish(part)
        else:
            k = pl.program_id(2)

            @pl.when(k == 0)
            def _():
                acc_ref[...] = part

            @pl.when(k > 0)
            def _():
                acc_ref[...] += part

            @pl.when(k == nk - 1)
            def _():
                finish(acc_ref[...])

    a_spec = pl.BlockSpec((tk, tm), lambda i, j, k: (k, i)) if ta else pl.BlockSpec((tm, tk), lambda i, j, k: (i, k))
    b_spec = pl.BlockSpec((tn, tk), lambda i, j, k: (j, k)) if tb else pl.BlockSpec((tk, tn), lambda i, j, k: (k, j))
    o_spec = pl.BlockSpec((tm, tn), lambda i, j, k: (i, j))
    in_specs, args = [a_spec, b_spec], [a, b]
    if has_res:
        in_specs.append(o_spec)
        args.append(res)
    return pl.pallas_call(
        body, name=name, grid=(m // tm, n // tn, nk), in_specs=in_specs, out_specs=o_spec,
        out_shape=jax.ShapeDtypeStruct((m, n), out_dtype),
        scratch_shapes=[pltpu.VMEM((tm, tn), f32)] if nk > 1 else [],
        compiler_params=_params(("parallel", "parallel", "arbitrary")),
    )(*args)


def _transpose(x, name):
    r, c = x.shape
    tr, tc = min(512, r), min(512, c)

    def body(x_ref, o_ref):
        o_ref[...] = x_ref[...].T

    return pl.pallas_call(
        body, name=name, grid=(r // tr, c // tc), in_specs=[pl.BlockSpec((tr, tc), lambda i, j: (i, j))],
        out_specs=pl.BlockSpec((tc, tr), lambda i, j: (j, i)), out_shape=jax.ShapeDtypeStruct((c, r), x.dtype),
        compiler_params=_params(("parallel", "parallel")),
    )(x)


def _d_hn(dxs, dba, wp, name):
    s = dxs[0].shape[0]
    tm = min(1024, s)

    def body(*refs):
        dx_refs = refs[:N_GROUPS]
        dba_ref, w_ref, wba_ref, o_ref, acc_ref = refs[N_GROUPS:]
        k = pl.program_id(1)
        for g in range(N_GROUPS):
            @pl.when(k == g)
            def _(g=g):
                part = _dot_nt(_b(dx_refs[g][...]), w_ref[...])
                if g == 0:
                    acc_ref[...] = part + _dot_nt(_b(dba_ref[...]), wba_ref[...])
                else:
                    acc_ref[...] += part

        @pl.when(k == N_GROUPS - 1)
        def _():
            o_ref[...] = acc_ref[...]

    row = lambda w: pl.BlockSpec((tm, w), lambda i, k: (i, 0))
    return pl.pallas_call(
        body, name=name, grid=(s // tm, N_GROUPS),
        in_specs=[row(D_MODEL)] * N_GROUPS + [row(HD), pl.BlockSpec((D_MODEL, D_MODEL), lambda i, k: (0, k)),
                                              pl.BlockSpec((D_MODEL, HD), lambda i, k: (0, BLK_BA))],
        out_specs=row(D_MODEL), out_shape=jax.ShapeDtypeStruct((s, D_MODEL), f32),
        scratch_shapes=[pltpu.VMEM((tm, D_MODEL), f32)],
        compiler_params=_params(("parallel", "arbitrary")),
    )(*dxs, dba, wp, wp)


def _d_win(hnt, dxs, dba, name):
    s = hnt.shape[1]
    tn = 256
    per = D_MODEL // tn

    def body(*refs):
        hnt_ref = refs[0]
        dx_refs = refs[1:1 + N_GROUPS]
        o_ref = refs[1 + N_GROUPS]
        j = pl.program_id(0)
        for g in range(N_GROUPS):
            @pl.when(j // per == g)
            def _(g=g):
                o_ref[...] = _dot(hnt_ref[...], _b(dx_refs[g][...])).astype(bf16)

    def dx_spec(g):
        on = lambda j: (j // per == g).astype(jnp.int32)
        return pl.BlockSpec((s, tn), lambda j: (0, (j % per) * on(j)))

    main = pl.pallas_call(
        body, name=name, grid=(N_GROUPS * per,),
        in_specs=[pl.BlockSpec((D_MODEL, s), lambda j: (0, 0))] + [dx_spec(g) for g in range(N_GROUPS)],
        out_specs=pl.BlockSpec((D_MODEL, tn), lambda j: (0, j)),
        out_shape=jax.ShapeDtypeStruct((D_MODEL, D_INP), bf16),
        compiler_params=_params(("parallel",)),
    )(hnt, *dxs)

    def tail(hnt_ref, dba_ref, full_ref, o_ref):
        del full_ref
        o_ref[...] = _dot(hnt_ref[...], _b(dba_ref[...])).astype(bf16)

    return pl.pallas_call(
        tail, name=name + "_ba", grid=(1,),
        in_specs=[pl.BlockSpec((D_MODEL, s), lambda k: (0, 0)), pl.BlockSpec((s, HD), lambda k: (0, 0)),
                  pl.BlockSpec(memory_space=pl.ANY)],
        out_specs=pl.BlockSpec((D_MODEL, HD), lambda k: (0, BLK_BA)),
        out_shape=jax.ShapeDtypeStruct((D_MODEL, D_INP), bf16),
        input_output_aliases={2: 0},
        compiler_params=_params(("arbitrary",)),
    )(hnt, dba, main)


def _rms_fwd(x, w, name):
    s = x.shape[0]
    tr = min(512, s)

    def body(x_ref, w_ref, h_ref):
        xv = x_ref[...]
        r = lax.rsqrt(_rowmean(xv * xv) + EPS)
        h_ref[...] = (xv * r * w_ref[...]).astype(bf16)

    return pl.pallas_call(
        body, name=name, grid=(s // tr,),
        in_specs=[pl.BlockSpec((tr, D_MODEL), lambda i: (i, 0)), pl.BlockSpec((1, D_MODEL), lambda i: (0, 0))],
        out_specs=pl.BlockSpec((tr, D_MODEL), lambda i: (i, 0)),
        out_shape=jax.ShapeDtypeStruct((s, D_MODEL), bf16), compiler_params=_params(("parallel",)),
    )(x, w)


def _rms_bwd(dh, x, w, dres, name):
    s = x.shape[0]
    tr = min(512, s)

    def body(dh_ref, x_ref, w_ref, dres_ref, dx_ref, dw_ref):
        xv = x_ref[...]
        r = lax.rsqrt(_rowmean(xv * xv) + EPS)
        xn = xv * r
        d = dh_ref[...]
        dxn = d * w_ref[...]
        dx_ref[...] = dres_ref[...] + r * (dxn - xn * _rowmean(dxn * xn))
        part = _sum0(d * xn)

        @pl.when(pl.program_id(0) == 0)
        def _():
            dw_ref[...] = part

        @pl.when(pl.program_id(0) > 0)
        def _():
            dw_ref[...] += part

    row = pl.BlockSpec((tr, D_MODEL), lambda i: (i, 0))
    vec = pl.BlockSpec((1, D_MODEL), lambda i: (0, 0))
    return pl.pallas_call(
        body, name=name, grid=(s // tr,), in_specs=[row, row, vec, row], out_specs=[row, vec],
        out_shape=[jax.ShapeDtypeStruct((s, D_MODEL), f32), jax.ShapeDtypeStruct((1, D_MODEL), f32)],
        compiler_params=_params(("arbitrary",)),
    )(dh, x, w, dres)


def _final_loss(x, w, target, name):
    s = x.shape[0]
    tr = min(512, s)

    def body(x_ref, w_ref, t_ref, dx_ref, dw_ref, loss_ref):
        xv = x_ref[...]
        wv = w_ref[...]
        r = lax.rsqrt(_rowmean(xv * xv) + EPS)
        xn = xv * r
        e = xn * wv - t_ref[...]
        lb = jnp.broadcast_to(0.5 * _sum0(_rowmean(e * e)), (1, HD))
        dy = e * (1.0 / D_MODEL)
        dxn = dy * wv
        dx_ref[...] = r * (dxn - xn * _rowmean(dxn * xn))
        part = _sum0(dy * xn)

        @pl.when(pl.program_id(0) == 0)
        def _():
            dw_ref[...] = part
            loss_ref[...] = lb

        @pl.when(pl.program_id(0) > 0)
        def _():
            dw_ref[...] += part
            loss_ref[...] += lb

    row = pl.BlockSpec((tr, D_MODEL), lambda i: (i, 0))
    vec = pl.BlockSpec((1, D_MODEL), lambda i: (0, 0))
    lsp = pl.BlockSpec((1, HD), lambda i: (0, 0))
    return pl.pallas_call(
        body, name=name, grid=(s // tr,), in_specs=[row, vec, row], out_specs=[row, vec, lsp],
        out_shape=[jax.ShapeDtypeStruct((s, D_MODEL), f32), jax.ShapeDtypeStruct((1, D_MODEL), f32),
                   jax.ShapeDtypeStruct((1, HD), f32)],
        compiler_params=_params(("arbitrary",)),
    )(x, w, target)


ALL = slice(None)


def _lru_gates(xc, vec, wa, wx):
    sp = _softplus(-vec[3:4])
    xcb = _b(xc)
    r = _sig(_dot(xcb, wa) + vec[1:2])
    i = _sig(_dot(xcb, wx) + vec[2:3])
    la = (-LRU_C) * r * sp
    a = jnp.exp(la)
    mult = jnp.sqrt(-_expm1(2.0 * la))
    return r, i, a, mult, sp, xcb


def _lru_fwd(proj, cw, vec, wa, wx, name):
    s = proj.shape[0]
    sb = min(SEQ_BLOCK, s)
    nsb = s // sb
    t = min(LRU_T, sb)
    nc = sb // t
    hp = LRU_HP
    wide = hp * HD
    lanes = [slice(j * HD, (j + 1) * HD) for j in range(hp)]

    def body(x_ref, xh_ref, z_ref, cw_ref, vec_ref, wa_ref, wx_ref, y_ref, hl_ref, hc_ref):
        sblk = pl.program_id(1)
        consts = [(cw_ref[:, hl], vec_ref[:, hl], _b(wa_ref[j]), _b(wx_ref[j])) for j, hl in enumerate(lanes)]
        row = lax.broadcasted_iota(jnp.int32, (t, HD), 0)

        @pl.when(sblk == 0)
        def _():
            hc_ref[...] = jnp.zeros((8, wide), f32)

        def one_head(j, x, halo, z, hcarry):
            cwv, vec_v, wa_v, wx_v = consts[j]
            xs = _conv_taps(x, halo)
            xc = vec_v[0:1] + _conv(xs, cwv)
            r, i, a, mult, _, _ = _lru_gates(xc, vec_v, wa_v, wx_v)
            yield
            bb = mult * (i * xc)
            d = 1
            while d < t:
                m = row >= d
                bb = jnp.where(m, a * pltpu.roll(bb, d, axis=0) + bb, bb)
                a = jnp.where(m, a * pltpu.roll(a, d, axis=0), a)
                d *= 2
                yield
            hl = bb + a * hcarry
            nrm = lax.rsqrt(_rowmean(hl * hl) + EPS)
            return hl, (hl * nrm * vec_v[4:5] * (z * _sig(z))).astype(bf16)

        def chunk(c, carries):
            t0 = pl.multiple_of(c * t, t)
            rows = pl.ds(t0, t)
            ins = [(x_ref[rows, hl], _rows_before(x_ref, xh_ref, hl, t0, c, sblk), z_ref[rows, hl]) for hl in lanes]
            outs = _round_robin([one_head(j, *ins[j], carries[j]) for j in range(hp)])
            for j, hl in enumerate(lanes):
                hl_ref[rows, hl] = outs[j][0]
                y_ref[rows, hl] = outs[j][1]
            return tuple(out[0][t - 1:t, :] for out in outs)

        final = lax.fori_loop(0, nc, chunk, tuple(hc_ref[0:1, hl] for hl in lanes))
        hc_ref[...] = jnp.concatenate([jnp.broadcast_to(f, (8, HD)) for f in final], axis=1)

    xsp, xhalo = _seq_specs(sb, wide, lambda g, i: i, lambda g, i: g)
    zsp, _ = _seq_specs(sb, wide, lambda g, i: i, lambda g, i: HEADS // hp + g)
    gcol = lambda g, i: (0, g)
    wsp = pl.BlockSpec((hp, HD, HD), lambda g, i: (g, 0, 0))
    osp = pl.BlockSpec((sb, wide), lambda g, i: (i, g))
    return pl.pallas_call(
        body, name=name, grid=(HEADS // hp, nsb),
        in_specs=[xsp, xhalo, zsp, pl.BlockSpec((4, wide), gcol), pl.BlockSpec((8, wide), gcol), wsp, wsp],
        out_specs=[osp, osp],
        out_shape=[jax.ShapeDtypeStruct((s, D_MIX), bf16), jax.ShapeDtypeStruct((s, D_MODEL), f32)],
        scratch_shapes=[pltpu.VMEM((8, wide), f32)],
        compiler_params=_params(("parallel", "arbitrary")),
    )(proj, proj, proj, cw, vec, wa, wx)


def _lru_bwd(dy, proj, hl, cw, vec, wa, wx, name):
    s = proj.shape[0]
    sb = min(SEQ_BLOCK, s)
    nsb = s // sb
    t = min(LRU_T, sb)
    nc = sb // t
    hp = LRU_HP
    wide = hp * HD
    lanes = [slice(j * HD, (j + 1) * HD) for j in range(hp)]
    n_acc = 10

    def body(dy_ref, x_ref, xh_ref, z_ref, hl_ref, hlh_ref, cw_ref, vec_ref, wa_ref, wx_ref,
             dx_ref, dz_ref, dcw_ref, dvec_ref, dwa_ref, dwx_ref, acc_ref, dxh_ref):
        step = pl.program_id(1)
        sblk = nsb - 1 - step
        consts = [(cw_ref[:, hl], vec_ref[:, hl], _b(wa_ref[j]), _b(wx_ref[j])) for j, hl in enumerate(lanes)]
        row = lax.broadcasted_iota(jnp.int32, (t, HD), 0)

        @pl.when(step == 0)
        def _():
            acc_ref[...] = jnp.zeros((16, wide), f32)
            dxh_ref[...] = jnp.zeros((8, wide), f32)
            dwa_ref[...] = jnp.zeros((hp, HD, HD), f32)
            dwx_ref[...] = jnp.zeros((hp, HD, HD), f32)

        def one_head(j, x, halo, z, hv, hhalo, dyv, carry):
            dcw0, dcw1, dcw2, dcw3, dcb, dba, dbx, dsp, dlnw, mu, dxc_halo = carry
            cwv, vec_v, wa_v, wx_v = consts[j]
            lnw = vec_v[4:5]
            xs = _conv_taps(x, halo)
            xc = vec_v[0:1] + _conv(xs, cwv)
            r, i, a, mult, sp, xcb = _lru_gates(xc, vec_v, wa_v, wx_v)
            yield
            hprev = pltpu.roll(jnp.concatenate([hhalo, hv], axis=0), 1, axis=0)[8:]
            sgz = _sig(z)
            sz = z * sgz
            nrm = lax.rsqrt(_rowmean(hv * hv) + EPS)
            hn = hv * nrm
            dlnw = dlnw + _sum0(dyv * hn * sz)
            dzv = _b(dyv * hn * lnw * _dsilu(z, sgz))
            dhn = dyv * lnw * sz
            lam = nrm * (dhn - hn * _rowmean(dhn * hn))
            cf = jnp.where(row == t - 1, 1.0, pltpu.roll(a, t - 1, axis=0))
            d = 1
            while d < t:
                m = row < t - d
                lam = jnp.where(m, lam + cf * pltpu.roll(lam, t - d, axis=0), lam)
                cf = jnp.where(m, cf * pltpu.roll(cf, t - d, axis=0), cf)
                d *= 2
                yield
            lam = lam + cf * mu
            mu = a[0:1] * lam[0:1]
            da = lam * hprev
            dmult = lam * (i * xc)
            di = lam * mult * xc
            dxc = lam * mult * i
            dla = da * a - dmult * (a * a) / mult
            dr = dla * (-LRU_C * sp)
            dsp = dsp + _sum0(dla * (-LRU_C * r))
            dra = dr * r * (1.0 - r)
            dia = di * i * (1.0 - i)
            dba = dba + _sum0(dra)
            dbx = dbx + _sum0(dia)
            drab, diab = _b(dra), _b(dia)
            dwa = _dot_tn(xcb, drab)
            dwx = _dot_tn(xcb, diab)
            dxc = dxc + _dot_nt(drab, wa_v) + _dot_nt(diab, wx_v)
            yield
            dcb = dcb + _sum0(dxc)
            dcw0 = dcw0 + _sum0(dxc * xs[0])
            dcw1 = dcw1 + _sum0(dxc * xs[1])
            dcw2 = dcw2 + _sum0(dxc * xs[2])
            dcw3 = dcw3 + _sum0(dxc * xs[3])
            dxv = _b(_conv_back(dxc, dxc_halo, cwv))
            return dxv, dzv, dwa, dwx, (dcw0, dcw1, dcw2, dcw3, dcb, dba, dbx, dsp, dlnw, mu, dxc[0:8])

        def chunk(ci, carries):
            c = nc - 1 - ci
            t0 = pl.multiple_of(c * t, t)
            rows = pl.ds(t0, t)
            ins = [(x_ref[rows, hl], _rows_before(x_ref, xh_ref, hl, t0, c, sblk), z_ref[rows, hl], hl_ref[rows, hl],
                    _rows_before(hl_ref, hlh_ref, hl, t0, c, sblk), dy_ref[rows, hl]) for hl in lanes]
            outs = _round_robin([one_head(j, *ins[j], carries[j]) for j in range(hp)])
            for j, hl in enumerate(lanes):
                dx_ref[rows, hl] = outs[j][0]
                dz_ref[rows, hl] = outs[j][1]
                dwa_ref[j] += outs[j][2]
                dwx_ref[j] += outs[j][3]
            return tuple(out[4] for out in outs)

        acc = acc_ref[...]
        dxh = dxh_ref[...]
        init = tuple(tuple(acc[k:k + 1, hl] for k in range(n_acc)) + (dxh[:, hl],) for hl in lanes)
        final = lax.fori_loop(0, nc, chunk, init)
        rows_out = [jnp.concatenate([final[j][k] for j in range(hp)], axis=1) for k in range(n_acc)]
        zero = jnp.zeros((1, wide), f32)
        acc_ref[...] = jnp.concatenate(rows_out + [zero] * (16 - n_acc), axis=0)
        dxh_ref[...] = jnp.concatenate([final[j][n_acc] for j in range(hp)], axis=1)

        @pl.when(step == nsb - 1)
        def _():
            dcw_ref[...] = jnp.concatenate(rows_out[0:4], axis=0)
            dlam = -rows_out[7] * _sig(-vec_ref[3:4, :])
            dvec_ref[...] = jnp.concatenate([rows_out[4], rows_out[5], rows_out[6], dlam, rows_out[8], zero, zero, zero],
                                            axis=0)

    rblk = lambda g, i: nsb - 1 - i
    xsp, xhalo = _seq_specs(sb, wide, rblk, lambda g, i: g)
    zsp, _ = _seq_specs(sb, wide, rblk, lambda g, i: HEADS // hp + g)
    gcol = lambda g, i: (0, g)
    wsp = pl.BlockSpec((hp, HD, HD), lambda g, i: (g, 0, 0))
    return pl.pallas_call(
        body, name=name, grid=(HEADS // hp, nsb),
        in_specs=[xsp, xsp, xhalo, zsp, xsp, xhalo, pl.BlockSpec((4, wide), gcol), pl.BlockSpec((8, wide), gcol), wsp, wsp],
        out_specs=[xsp, xsp, pl.BlockSpec((4, wide), gcol), pl.BlockSpec((8, wide), gcol), wsp, wsp],
        out_shape=[jax.ShapeDtypeStruct((s, D_MODEL), bf16), jax.ShapeDtypeStruct((s, D_MODEL), bf16),
                   jax.ShapeDtypeStruct((4, D_MODEL), f32), jax.ShapeDtypeStruct((8, D_MODEL), f32),
                   jax.ShapeDtypeStruct((HEADS, HD, HD), f32), jax.ShapeDtypeStruct((HEADS, HD, HD), f32)],
        scratch_shapes=[pltpu.VMEM((16, wide), f32), pltpu.VMEM((8, wide), f32)],
        compiler_params=_params(("parallel", "arbitrary")),
    )(dy, proj, proj, proj, hl, hl, cw, vec, wa, wx)


def _lane_pick(x, lane, idx):
    return _sum1(jnp.where(lane == idx, x, 0.0))


def _round_robin(gens):
    results = [None] * len(gens)
    live = list(range(len(gens)))
    while live:
        for i in list(live):
            try:
                next(gens[i])
            except StopIteration as done:
                results[i] = done.value
                live.remove(i)
    return results


def _sdot(a, b):
    return _dot(_b(a), _b(b))


def _sdot_tn(a, b):
    return _dot_tn(_b(a), _b(b))


def _sdot_nt(a, b):
    return _dot_nt(_b(a), _b(b))


def _tri_dot(tri, x):
    trib = tri.astype(bf16)
    x1 = x.astype(bf16)
    r1 = x - x1.astype(f32)
    x2 = r1.astype(bf16)
    x3 = (r1 - x2.astype(f32)).astype(bf16)
    return _dot(trib, x1) + _dot(trib, x2) + _dot(trib, x3)


def _inv_unit_lower(a):
    n = -a
    p = _sdot(a, a)
    yield
    for k in range(5):
        n = n + p + _sdot(n, p)
        if k < 4:
            p = _sdot(p, p)
        yield
    return n


def _dn_chunk(x3, halo3, bav, cw3, dnv, h, masks, tinv=None):
    lane, ri, ci, eye, ltri = masks
    xs = _conv_taps(x3, halo3)
    cpre = _conv(xs, cw3)
    sg = _sig(cpre)
    act = cpre * sg
    q_raw, k_raw, v = act[:, 0:HD], act[:, HD:2 * HD], act[:, 2 * HD:3 * HD]
    rq = lax.rsqrt(_sum1(q_raw * q_raw) + EPS)
    rk = lax.rsqrt(_sum1(k_raw * k_raw) + EPS)
    qn = q_raw * rq
    k = k_raw * rk
    q = qn * QSCALE
    b_in = jnp.broadcast_to(_lane_pick(bav, lane, h), (CHUNK, HD))
    a_in = jnp.broadcast_to(_lane_pick(bav, lane, HEADS + h), (CHUNK, HD))
    lane1 = lane[0:1]
    ea = jnp.exp(_lane_pick(dnv[0:1], lane1, h))
    dt = _lane_pick(dnv[1:2], lane1, h)
    beta = _sig(b_in)
    sp = _softplus(a_in + dt)
    g = -ea * sp
    gc = _tri_dot(ltri, g)
    yield
    gc64 = gc[:, 0:CHUNK]
    grow = _sum0(gc64 * eye)
    dm = jnp.exp(jnp.where(ri >= ci, gc64 - grow, -1e30))
    ds_ = jnp.where(ri > ci, dm, 0.0)
    eg = jnp.exp(gc)
    glast = gc[CHUNK - 1:CHUNK, :]
    ek = jnp.exp(glast - gc)
    kb = k * beta
    kbf = _b(k)
    amat = _dot_nt(_b(kb), kbf) * ds_
    pmat = _dot_nt(_b(q), kbf) * dm
    yield
    if tinv is None:
        tinv = yield from _inv_unit_lower(amat)
    return dict(xs=xs, cpre=cpre, sg=sg, rq=rq, rk=rk, qn=qn, q=q, k=k, v=v, kbf=kbf, b_in=b_in, a_in=a_in, ea=ea,
                dt=dt, beta=beta, g=g, gc=gc, dm=dm, ds=ds_, eg=eg, glast=glast, ek=ek, kb=kb, amat=amat,
                tinv=tinv, pmat=pmat)


def _dn_masks():
    lane = lax.broadcasted_iota(jnp.int32, (CHUNK, HD), 1)
    ri = lax.broadcasted_iota(jnp.int32, (CHUNK, CHUNK), 0)
    ci = lax.broadcasted_iota(jnp.int32, (CHUNK, CHUNK), 1)
    eye = (ri == ci).astype(f32)
    ltri = (ri >= ci).astype(f32)
    return lane, ri, ci, eye, ltri


def _dn_load_qkv(q_ref, qh_ref, k_ref, kh_ref, v_ref, vh_ref, hl, rows, t0, c, sblk):
    x3 = jnp.concatenate([q_ref[rows, hl], k_ref[rows, hl], v_ref[rows, hl]], axis=1)
    halo3 = jnp.concatenate([_rows_before(q_ref, qh_ref, hl, t0, c, sblk), _rows_before(k_ref, kh_ref, hl, t0, c, sblk),
                             _rows_before(v_ref, vh_ref, hl, t0, c, sblk)], axis=1)
    return x3, halo3


def _dn_cw3(cwq_ref, cwk_ref, cwv_ref, j):
    return jnp.concatenate([r[:, j * HD:(j + 1) * HD] for r in (cwq_ref, cwk_ref, cwv_ref)], axis=1)


def _dn_fwd(proj, cw, dnv, y, name):
    s = proj.shape[0]
    sb = min(DN_SEQ_BLOCK, s)
    nsb = s // sb
    nc = sb // CHUNK
    hp = DN_HP

    def body(q_ref, qh_ref, k_ref, kh_ref, v_ref, vh_ref, z_ref, ba_ref, cwq_ref, cwk_ref, cwv_ref, dnv_ref, yin_ref,
             y_ref, o_ref, st_ref, ti_ref, s_ref):
        del yin_ref
        grp = pl.program_id(0)
        sblk = pl.program_id(1)
        masks = _dn_masks()
        dnv_v = dnv_ref[...]
        cw3 = [_dn_cw3(cwq_ref, cwk_ref, cwv_ref, j) for j in range(hp)]

        @pl.when(sblk == 0)
        def _():
            s_ref[...] = jnp.zeros((hp, HD, HD), f32)

        def one_head(j, x3, halo3, bav, z, st):
            f = yield from _dn_chunk(x3, halo3, bav, cw3[j], dnv_v, grp * hp + j, masks)
            sbf = _b(st)
            qs = _dot(_b(f["q"] * f["eg"]), sbf)
            rhs = f["beta"] * (f["v"] - _dot(_b(f["k"] * f["eg"]), sbf))
            yield
            vn = rhs + _sdot(f["tinv"], rhs)
            yield
            vnb = _b(vn)
            o = qs + _dot(_b(f["pmat"]), vnb)
            st_new = st * jnp.exp(f["glast"]) + _dot_tn(_b(f["k"] * f["ek"]), vnb)
            yield
            nrm = lax.rsqrt(_rowmean(o * o) + EPS)
            yv = (o * nrm * dnv_v[2:3] * (z * _sig(z))).astype(bf16)
            return o, yv, st_new, f["tinv"]

        def chunk(c, states):
            t0 = pl.multiple_of(c * CHUNK, CHUNK)
            rows = pl.ds(t0, CHUNK)
            bav = ba_ref[rows, :]
            ins = []
            for j in range(hp):
                hl = slice(j * HD, (j + 1) * HD)
                ins.append(_dn_load_qkv(q_ref, qh_ref, k_ref, kh_ref, v_ref, vh_ref, hl, rows, t0, c, sblk)
                           + (bav, z_ref[rows, hl]))
            outs = _round_robin([one_head(j, *ins[j], states[j]) for j in range(hp)])
            for j in range(hp):
                hl = slice(j * HD, (j + 1) * HD)
                st_ref[j, c] = states[j]
                o_ref[rows, hl] = outs[j][0]
                y_ref[rows, hl] = outs[j][1]
                ti_ref[j, c] = outs[j][3]
            return tuple(out[2] for out in outs)

        final = lax.fori_loop(0, nc, chunk, tuple(s_ref[j] for j in range(hp)))
        for j in range(hp):
            s_ref[j] = final[j]

    wide = hp * HD
    grp_specs = lambda off: _seq_specs(sb, wide, lambda g, i: i, lambda g, i: off // hp + g)
    (qsp, qhalo), (ksp, khalo), (vsp, vhalo) = grp_specs(2 * HEADS), grp_specs(3 * HEADS), grp_specs(4 * HEADS)
    zsp, _ = grp_specs(5 * HEADS)
    basp = pl.BlockSpec((sb, HD), lambda g, i: (i, BLK_BA))
    cws = lambda off: pl.BlockSpec((4, wide), lambda g, i: (0, off // hp + g))
    osp = lambda off: pl.BlockSpec((sb, wide), lambda g, i: (i, off // hp + g))
    return pl.pallas_call(
        body, name=name, grid=(HEADS // hp, nsb),
        in_specs=[qsp, qhalo, ksp, khalo, vsp, vhalo, zsp, basp, cws(0), cws(HEADS), cws(2 * HEADS),
                  pl.BlockSpec((8, HD), lambda g, i: (0, 0)), pl.BlockSpec(memory_space=pl.ANY)],
        out_specs=[osp(HEADS), osp(0), pl.BlockSpec((hp, nc, HD, HD), lambda g, i: (g, i, 0, 0)),
                   pl.BlockSpec((hp, nc, CHUNK, CHUNK), lambda g, i: (g, i, 0, 0))],
        out_shape=[jax.ShapeDtypeStruct((s, D_MIX), bf16), jax.ShapeDtypeStruct((s, D_MODEL), f32),
                   jax.ShapeDtypeStruct((HEADS, s // CHUNK, HD, HD), f32),
                   jax.ShapeDtypeStruct((HEADS, s // CHUNK, CHUNK, CHUNK), f32)],
        scratch_shapes=[pltpu.VMEM((hp, HD, HD), f32)],
        input_output_aliases={12: 0},
        compiler_params=_params(("parallel", "arbitrary")),
    )(proj, proj, proj, proj, proj, proj, proj, proj, cw, cw, cw, dnv, y)


def _dn_bwd(dy, proj, o, states, tinvs, cw, dnv, name):
    s = proj.shape[0]
    sb = min(DN_SEQ_BLOCK, s)
    nsb = s // sb
    nc = sb // CHUNK
    hp = DN_HP
    ngrp = HEADS // hp

    def body(dy_ref, q_ref, qh_ref, k_ref, kh_ref, v_ref, vh_ref, z_ref, ba_ref, o_ref, st_ref, ti_ref,
             cwq_ref, cwk_ref, cwv_ref, dnv_ref,
             dq_ref, dk_ref, dv_ref, dz_ref, dba_ref, dcw_ref, dsc_ref,
             ds_ref, dch_ref, cwacc_ref, sacc_ref, nacc_ref):
        step = pl.program_id(0)
        grp = pl.program_id(1)
        sblk = nsb - 1 - step
        h0 = grp * hp
        masks = _dn_masks()
        lane, ri, ci, eye, ltri = masks
        utri = (ri <= ci).astype(f32)
        cw3s = [_dn_cw3(cwq_ref, cwk_ref, cwv_ref, j) for j in range(hp)]
        dnv_v = dnv_ref[...]
        dnw = dnv_v[2:3]
        row64 = lax.broadcasted_iota(jnp.int32, (CHUNK, HD), 0)

        @pl.when(step == 0)
        def _():
            for j in range(hp):
                ds_ref[h0 + j] = jnp.zeros((HD, HD), f32)
                dch_ref[h0 + j] = jnp.zeros((8, 3 * HD), f32)
                cwacc_ref[h0 + j] = jnp.zeros((8, 3 * HD), f32)
                sacc_ref[h0 + j] = jnp.zeros((8, HD), f32)

        @pl.when((step == 0) & (grp == 0))
        def _():
            nacc_ref[...] = jnp.zeros((8, HD), f32)

        def one_head(j, x3, halo3, bav, z, st, ti, ov, dyv, carry):
            dst, dch, cwacc, sa0, sa1 = carry
            hd = h0 + j
            cw3 = cw3s[j]
            f = yield from _dn_chunk(x3, halo3, bav, cw3, dnv_v, hd, masks, ti)
            q, k, v, beta, eg, ek, kbf = f["q"], f["k"], f["v"], f["beta"], f["eg"], f["ek"], f["kbf"]
            sbf = _b(st)
            dstb = _b(dst)
            qe = q * eg
            keg = k * eg
            kd = k * ek
            kegb, qeb, kdb = _b(keg), _b(qe), _b(kd)
            e = v - _dot(kegb, sbf)
            yield
            rhs = beta * e
            vn = rhs + _sdot(f["tinv"], rhs)
            yield
            vnb = _b(vn)
            pb = _b(f["pmat"])
            sgz = _sig(z)
            sz = z * sgz
            nrm = lax.rsqrt(_rowmean(ov * ov) + EPS)
            on = ov * nrm
            ddnw = _sum0(dyv * on * sz)
            dpz = dyv * on * dnw * _dsilu(z, sgz)
            don = dyv * dnw * sz
            do = nrm * (don - on * _rowmean(don * on))
            dob = _b(do)
            dvn = _dot_tn(pb, dob) + _dot(kdb, dstb)
            dpm = jnp.where(ri >= ci, _dot_nt(dob, vnb), 0.0)
            dqe = _dot_nt(dob, sbf)
            dkd = _dot_nt(vnb, dstb)
            qtdo = _dot_tn(qeb, dob)
            yield
            drhs = dvn + _sdot_tn(f["tinv"], dvn)
            dqk = _b(dpm * f["dm"])
            dq = _dot(dqk, kbf) + dqe * eg
            dk_p = _dot_tn(dqk, _b(q))
            yield
            dam = jnp.where(ri > ci, -_sdot_nt(drhs, vn), 0.0)
            de = drhs * beta
            deb = _b(de)
            dbeta = _sum1(drhs * e)
            dkeg = -_dot_nt(deb, sbf)
            dst_new = qtdo + dst * jnp.exp(f["glast"]) - _dot_tn(kegb, deb)
            yield
            dkk = _b(dam * f["ds"])
            dkb = _dot(dkk, kbf)
            dk = dk_p + _dot_tn(dkk, _b(f["kb"])) + dkb * beta + dkeg * eg + dkd * ek
            yield
            dbeta = dbeta + _sum1(dkb * k)
            mm = dpm * f["pmat"] + dam * f["amat"]
            dglast = _sum1(_sum0(dst * st)) * jnp.exp(f["glast"]) + _sum1(_sum0(dkd * kd))
            dgc = (_sum1(mm) - _sum1(eye * _sum0(mm)) + _sum1(dqe * qe) + _sum1(dkeg * keg) - _sum1(dkd * kd))
            dgc = jnp.broadcast_to(dgc, (CHUNK, HD)) + jnp.where(row64 == CHUNK - 1, dglast, 0.0)
            dg = _tri_dot(utri, dgc)
            yield
            dbin = jnp.broadcast_to(dbeta, (CHUNK, HD)) * beta * (1.0 - beta)
            dain = dg * (-f["ea"]) * _sig(f["a_in"] + f["dt"])
            dba = jnp.where(lane == hd, dbin, 0.0) + jnp.where(lane == HEADS + hd, dain, 0.0)
            sa0 = sa0 + _sum0(dg * f["g"])
            sa1 = sa1 + _sum0(dain)
            dq_raw = (QSCALE * f["rq"]) * (dq - f["qn"] * _sum1(f["qn"] * dq))
            dk_raw = f["rk"] * (dk - k * _sum1(k * dk))
            dact = jnp.concatenate([dq_raw, dk_raw, de], axis=1)
            dc = dact * _dsilu(f["cpre"], f["sg"])
            xs = f["xs"]
            cwacc = cwacc + jnp.concatenate([_sum0(dc * xs[0]), _sum0(dc * xs[1]), _sum0(dc * xs[2]),
                                             _sum0(dc * xs[3])], axis=0)
            dqkv = _conv_back(dc, dch, cw3)
            return dpz, dqkv, dba, ddnw, (dst_new, dc[0:8], cwacc, sa0, sa1)

        def chunk(cidx, carry):
            heads, nacc = carry
            c = nc - 1 - cidx
            t0 = pl.multiple_of(c * CHUNK, CHUNK)
            rows = pl.ds(t0, CHUNK)
            bav = ba_ref[rows, :]
            ins = []
            for j in range(hp):
                hl = slice(j * HD, (j + 1) * HD)
                ins.append(_dn_load_qkv(q_ref, qh_ref, k_ref, kh_ref, v_ref, vh_ref, hl, rows, t0, c, sblk)
                           + (bav, z_ref[rows, hl], st_ref[j, c], ti_ref[j, c], o_ref[rows, hl], dy_ref[rows, hl]))
            outs = _round_robin([one_head(j, *ins[j], heads[j]) for j in range(hp)])
            dba = outs[0][2]
            for j in range(hp):
                hl = slice(j * HD, (j + 1) * HD)
                dpz, dqkv, dba_j, ddnw, _ = outs[j]
                dq_ref[rows, hl] = _b(dqkv[:, 0:HD])
                dk_ref[rows, hl] = _b(dqkv[:, HD:2 * HD])
                dv_ref[rows, hl] = _b(dqkv[:, 2 * HD:3 * HD])
                dz_ref[rows, hl] = _b(dpz)
                nacc = nacc + ddnw
                if j > 0:
                    dba = dba + dba_j

            @pl.when(grp == 0)
            def _():
                dba_ref[rows, :] = dba

            @pl.when(grp > 0)
            def _():
                dba_ref[rows, :] += dba

            return tuple(out[4] for out in outs), nacc

        init = tuple((ds_ref[h0 + j], dch_ref[h0 + j], cwacc_ref[h0 + j][0:4], sacc_ref[h0 + j][0:1],
                      sacc_ref[h0 + j][1:2]) for j in range(hp))
        heads, nacc = lax.fori_loop(0, nc, chunk, (init, nacc_ref[0:1, :]))
        nacc_ref[0:1, :] = nacc
        zpad = jnp.zeros((4, 3 * HD), f32)
        zrow = jnp.zeros((6, HD), f32)
        for j in range(hp):
            dst, dch, cwacc, sa0, sa1 = heads[j]
            ds_ref[h0 + j] = dst
            dch_ref[h0 + j] = dch
            cwacc_ref[h0 + j] = jnp.concatenate([cwacc, zpad], axis=0)
            sacc_ref[h0 + j] = jnp.concatenate([sa0, sa1, zrow], axis=0)

        @pl.when(step == nsb - 1)
        def _():
            lane8 = lax.broadcasted_iota(jnp.int32, (8, HD), 1)
            row8 = lax.broadcasted_iota(jnp.int32, (8, HD), 0)
            mine = jnp.zeros((8, HD), f32)
            for j in range(hp):
                dcw_ref[h0 + j] = heads[j][2]
                sa = jnp.concatenate([heads[j][3], heads[j][4], zrow], axis=0)
                mine = mine + jnp.where((lane8 == h0 + j) & (row8 < 2), sa, 0.0)

            @pl.when(grp == 0)
            def _():
                dsc_ref[...] = mine

            @pl.when(grp > 0)
            def _():
                dsc_ref[...] += mine

            @pl.when(grp == ngrp - 1)
            def _():
                dsc_ref[2:3, :] = nacc

    wide = hp * HD
    rblk = lambda i, g: nsb - 1 - i
    grp_specs = lambda off: _seq_specs(sb, wide, rblk, lambda i, g: off // hp + g)
    (qsp, qhalo), (ksp, khalo), (vsp, vhalo) = grp_specs(2 * HEADS), grp_specs(3 * HEADS), grp_specs(4 * HEADS)
    zsp, _ = grp_specs(5 * HEADS)
    hsp = lambda off: pl.BlockSpec((sb, wide), lambda i, g: (rblk(i, g), off // hp + g))
    basp = lambda col: pl.BlockSpec((sb, HD), lambda i, g: (rblk(i, g), col))
    cws = lambda off: pl.BlockSpec((4, wide), lambda i, g: (0, off // hp + g))
    whole = lambda shape: pl.BlockSpec(shape, lambda i, g: (0,) * len(shape))
    return pl.pallas_call(
        body, name=name, grid=(nsb, ngrp),
        in_specs=[hsp(HEADS), qsp, qhalo, ksp, khalo, vsp, vhalo, zsp, basp(BLK_BA), hsp(0),
                  pl.BlockSpec((hp, nc, HD, HD), lambda i, g: (g, rblk(i, g), 0, 0)),
                  pl.BlockSpec((hp, nc, CHUNK, CHUNK), lambda i, g: (g, rblk(i, g), 0, 0)),
                  cws(0), cws(HEADS), cws(2 * HEADS), whole((8, HD))],
        out_specs=[hsp(0), hsp(0), hsp(0), hsp(0), basp(0), whole((HEADS, 4, 3 * HD)), whole((8, HD))],
        out_shape=[jax.ShapeDtypeStruct((s, D_MODEL), bf16)] * 4
        + [jax.ShapeDtypeStruct((s, HD), f32), jax.ShapeDtypeStruct((HEADS, 4, 3 * HD), f32),
           jax.ShapeDtypeStruct((8, HD), f32)],
        scratch_shapes=[pltpu.VMEM((HEADS, HD, HD), f32), pltpu.VMEM((HEADS, 8, 3 * HD), f32),
                        pltpu.VMEM((HEADS, 8, 3 * HD), f32), pltpu.VMEM((HEADS, 8, HD), f32), pltpu.VMEM((8, HD), f32)],
        compiler_params=_params(("arbitrary", "arbitrary")),
    )(dy, proj, proj, proj, proj, proj, proj, proj, proj, o, states, tinvs, cw, cw, cw, dnv)


ANY = pl.BlockSpec(memory_space=pl.ANY)
CHIP_MASKS = ((1, 0, 0), (0, 1, 0), (1, 1, 0))
ALL_MASKS = tuple((m >> 2 & 1, m >> 1 & 1, m & 1) for m in range(1, N_DEV))


def _me():
    return lax.axis_index("x"), lax.axis_index("y"), lax.axis_index("c")


def _flip(me, mask):
    return tuple((1 - v) if m else v for v, m in zip(me, mask))


def _dev_index(dev):
    return 4 * dev[0] + 2 * dev[1] + dev[2]


def _chip(dev):
    return 2 * dev[0] + dev[1]


def _comm_call(name, body, arrays, out_shapes, nsem, nloc=0):
    scratch = [pltpu.SemaphoreType.DMA((nsem,)), pltpu.SemaphoreType.DMA((nsem,))]
    if nloc:
        scratch.append(pltpu.SemaphoreType.DMA((nloc,)))
    return pl.pallas_call(
        body, name=name, in_specs=[ANY] * len(arrays), out_specs=[ANY] * len(out_shapes), out_shape=out_shapes,
        scratch_shapes=scratch, compiler_params=pltpu.CompilerParams(has_side_effects=True),
    )(*arrays)


def _put_own(gathered, own, slot):
    return lax.dynamic_update_index_in_dim(gathered, own, slot, 0)


def _gather_halves(arrays, whole_arrays, name):
    na, nw = len(arrays), len(whole_arrays)
    nm = len(CHIP_MASKS)

    def body(*refs):
        srcs, dsts = refs[:na + nw], refs[na + nw:2 * (na + nw)]
        send_sems, recv_sems = refs[2 * (na + nw):]
        me = _me()
        sib = _flip(me, (0, 0, 1))

        def half(a, ref, core):
            rows = arrays[a].shape[0] // 2
            return ref.at[pl.ds(pl.multiple_of(core * rows, rows), rows)]

        def over_ici(a, mi, sender, to):
            if a < na:
                src, dst = half(a, srcs[a], sender[2]), half(a, dsts[a].at[_chip(sender)], sender[2])
            else:
                src, dst = srcs[a], dsts[a].at[_chip(sender)]
            return pltpu.make_async_remote_copy(src_ref=src, dst_ref=dst, send_sem=send_sems.at[a * nm + mi],
                                                recv_sem=recv_sems.at[a * nm + mi], device_id=to, device_id_type=MESH)

        def over_d2d(a, mi, origin, sender, to):
            k = (na + nw) * nm + a * nm + mi
            blk = half(a, dsts[a].at[_chip(origin)], sender[2])
            return pltpu.make_async_remote_copy(src_ref=blk, dst_ref=blk, send_sem=send_sems.at[k],
                                                recv_sem=recv_sems.at[k], device_id=to, device_id_type=MESH)

        sends = []
        for a in range(na + nw):
            for mi, mask in enumerate(CHIP_MASKS):
                cp = over_ici(a, mi, me, _flip(me, mask))
                cp.start()
                sends.append(cp)
        for a in range(na + nw):
            for mi, mask in enumerate(CHIP_MASKS):
                peer = _flip(me, mask)
                over_ici(a, mi, peer, me).wait_recv()
                if a < na:
                    cp = over_d2d(a, mi, peer, me, sib)
                    cp.start()
                    sends.append(cp)
        for a in range(na):
            for mi, mask in enumerate(CHIP_MASKS):
                over_d2d(a, mi, _flip(sib, mask), sib, me).wait_recv()
        for cp in sends:
            cp.wait_send()

    every = list(arrays) + list(whole_arrays)
    got = _comm_call(name, body, every, [jax.ShapeDtypeStruct((4,) + t.shape, t.dtype) for t in every],
                     (2 * na + nw) * nm)
    chip = 2 * lax.axis_index("x") + lax.axis_index("y")
    return [_put_own(g, t, chip) for g, t in zip(got, every)]


HBM = pl.BlockSpec(memory_space=pltpu.HBM)
SEM = pl.BlockSpec(memory_space=pltpu.SEMAPHORE)
DATAFLOW = pltpu.SideEffectType.DATAFLOW_SIDE_EFFECTING


def _split_start(name, srcs, land_shapes, nsem, issue, after=None):
    ns, nl = len(srcs), len(land_shapes)
    n_in = ns + nl + (after is not None)

    def body(*refs):
        issue(refs[:ns], refs[ns:ns + nl], refs[n_in + ns + nl], refs[n_in + ns + nl + 1])
        token = refs[-1]
        token[...] = jnp.zeros_like(token)

    lands = [lax.empty(t.shape, t.dtype) for t in land_shapes]
    thru = [pltpu.HBM(t.shape, t.dtype) for t in list(srcs) + list(land_shapes)]
    hbm = lambda t: pltpu.with_memory_space_constraint(t, pltpu.HBM)
    return pl.pallas_call(
        body, name=name, in_specs=[HBM] * (ns + nl) + [ANY] * (after is not None),
        out_shape=(*thru, pltpu.SemaphoreType.DMA((nsem,)), pltpu.SemaphoreType.DMA((nsem,)),
                   jax.ShapeDtypeStruct((8, HD), f32)),
        out_specs=(*[HBM] * (ns + nl), SEM, SEM, pl.BlockSpec(memory_space=pltpu.VMEM)),
        input_output_aliases={i: i for i in range(ns + nl)},
        compiler_params=pltpu.CompilerParams(has_side_effects=DATAFLOW),
    )(*[hbm(t) for t in srcs], *[hbm(t) for t in lands], *([after] if after is not None else []))


def _split_wait(name, started, ns, after, finish):
    thru, send_sems, recv_sems = started[:-3], started[-3], started[-2]
    nt = len(thru)

    def body(*refs):
        finish(refs[:ns], refs[ns:nt], refs[nt], refs[nt + 1])

    outs = pl.pallas_call(
        body, name=name, in_specs=[HBM] * nt + [SEM, SEM, ANY],
        out_shape=tuple(pltpu.HBM(t.shape, t.dtype) for t in thru), out_specs=tuple([HBM] * nt),
        input_output_aliases={i: i for i in range(nt)},
        compiler_params=pltpu.CompilerParams(has_side_effects=DATAFLOW),
    )(*thru, send_sems, recv_sems, after)
    return list(outs[ns:])


def _gather_start(arrays, name, after=None):
    nm = len(CHIP_MASKS)

    def issue(srcs, lands, send_sems, recv_sems):
        me = _me()
        for a in range(len(arrays)):
            for mi, mask in enumerate(CHIP_MASKS):
                pltpu.make_async_remote_copy(
                    src_ref=srcs[a], dst_ref=lands[a].at[_chip(me)], send_sem=send_sems.at[a * nm + mi],
                    recv_sem=recv_sems.at[a * nm + mi], device_id=_flip(me, mask), device_id_type=MESH).start()

    shapes = [jax.ShapeDtypeStruct((4,) + t.shape, t.dtype) for t in arrays]
    return _split_start(name, arrays, shapes, len(arrays) * nm, issue, after)


def _gather_finish(started, arrays, after, name):
    nm = len(CHIP_MASKS)

    def finish(srcs, lands, send_sems, recv_sems):
        me = _me()
        for a in range(len(arrays)):
            for mi, mask in enumerate(CHIP_MASKS):
                peer = _flip(me, mask)
                cp = pltpu.make_async_remote_copy(
                    src_ref=srcs[a], dst_ref=lands[a].at[_chip(peer)], send_sem=send_sems.at[a * nm + mi],
                    recv_sem=recv_sems.at[a * nm + mi], device_id=peer, device_id_type=MESH)
                cp.wait_send()
                cp.wait_recv()

    got = _split_wait(name, started, len(arrays), after, finish)
    chip = 2 * lax.axis_index("x") + lax.axis_index("y")
    return [_put_own(g, t, chip) for g, t in zip(got, arrays)]


HALF_IN, HALF_OUT = D_MODEL // 2, D_MIX // 8


def _scatter_piece(kind, ref, d):
    j, r = d >> 1, d & 1
    if kind == "win":
        return ref.at[pl.ds(HALF_IN * r, HALF_IN), pl.ds(WIN_STRIDE * j, WIN_W)]
    if kind == "wout":
        return ref.at[pl.ds(2 * HALF_OUT * j + HALF_OUT * r, HALF_OUT)]
    return ref.at[d]


def _scatter_start(arrays, kinds, name):
    na = len(arrays)

    def issue(srcs, lands, send_sems, recv_sems):
        me = _me()
        my = _dev_index(me)
        for d in range(N_DEV):
            dev = (d >> 2, (d >> 1) & 1, d & 1)
            for a in range(na):
                @pl.when(my != d)
                def _(a=a, d=d, dev=dev):
                    pltpu.make_async_remote_copy(
                        src_ref=_scatter_piece(kinds[a], srcs[a], d), dst_ref=lands[a].at[my],
                        send_sem=send_sems.at[a * N_DEV + d], recv_sem=recv_sems.at[a * N_DEV + my],
                        device_id=dev, device_id_type=MESH).start()

    shape_of = {"win": (N_DEV, HALF_IN, WIN_W), "wout": (N_DEV, HALF_OUT, D_MODEL)}
    shapes = [jax.ShapeDtypeStruct(shape_of.get(k, t.shape), t.dtype) for t, k in zip(arrays, kinds)]
    return _split_start(name, arrays, shapes, na * N_DEV, issue)


def _scatter_finish(started, arrays, kinds, after, name):
    na = len(arrays)

    def finish(srcs, lands, send_sems, recv_sems):
        me = _me()
        my = _dev_index(me)
        for s in range(N_DEV):
            for a in range(na):
                @pl.when(my != s)
                def _(a=a, s=s):
                    cp = pltpu.make_async_remote_copy(
                        src_ref=_scatter_piece(kinds[a], srcs[a], s), dst_ref=lands[a].at[s],
                        send_sem=send_sems.at[a * N_DEV + s], recv_sem=recv_sems.at[a * N_DEV + s],
                        device_id=me, device_id_type=MESH)
                    cp.wait_recv()
                    cp.wait_send()

    got = _split_wait(name, started, na, after, finish)
    x, y, c = _me()
    chip, my = 2 * x + y, 4 * x + 2 * y + c
    own = {"win": lambda t: lax.dynamic_slice(t, (HALF_IN * c, WIN_STRIDE * chip), (HALF_IN, WIN_W)),
           "wout": lambda t: lax.dynamic_slice(t, (2 * HALF_OUT * chip + HALF_OUT * c, 0), (HALF_OUT, D_MODEL)),
           "small": lambda t: lax.dynamic_index_in_dim(t, my, 0, keepdims=False)}
    return [_put_own(g, own[k](t), my) for g, t, k in zip(got, arrays, kinds)]


def _share_reduced(pair_arrays, small, name):
    arrays = list(pair_arrays) + ([small] if small is not None else [])
    na, npair = len(arrays), len(pair_arrays)
    nm = len(ALL_MASKS)

    def body(*refs):
        srcs, dsts = refs[:na], refs[na:2 * na]
        send_sems, recv_sems = refs[2 * na:]
        me = _me()
        sib = _flip(me, (0, 0, 1))

        def copy(a, k, sender, to):
            slot = sender[2] if a < npair else _dev_index(sender)
            return pltpu.make_async_remote_copy(
                src_ref=srcs[a], dst_ref=dsts[a].at[slot], send_sem=send_sems.at[k], recv_sem=recv_sems.at[k],
                device_id=to, device_id_type=MESH)

        sends = [copy(a, a, me, sib) for a in range(npair)]
        if small is not None:
            sends += [copy(npair, npair + mi, me, _flip(me, mask)) for mi, mask in enumerate(ALL_MASKS)]
        for cp in sends:
            cp.start()
        for a in range(npair):
            copy(a, a, sib, me).wait_recv()
        if small is not None:
            for mi, mask in enumerate(ALL_MASKS):
                copy(npair, npair + mi, _flip(me, mask), me).wait_recv()
        for cp in sends:
            cp.wait_send()

    shapes = [jax.ShapeDtypeStruct((2,) + t.shape, t.dtype) for t in pair_arrays]
    if small is not None:
        shapes.append(jax.ShapeDtypeStruct((N_DEV,) + small.shape, small.dtype))
    got = _comm_call(name, body, arrays, shapes, npair + (nm if small is not None else 0))
    x, y, c = _me()
    slots = [c] * npair + [4 * x + 2 * y + c]
    return [_put_own(g, t, slot) for g, t, slot in zip(got, arrays, slots)]


def _sum_parts(parts, name):
    _, r, c = parts.shape
    tr = r
    while tr * c * 4 * N_DEV > (8 << 20) and tr % 32 == 0:
        tr //= 2

    def body(p_ref, o_ref):
        total = p_ref[0].astype(f32)
        for d in range(1, N_DEV):
            total = total + p_ref[d].astype(f32)
        o_ref[...] = total

    return pl.pallas_call(
        body, name=name, grid=(r // tr,), in_specs=[pl.BlockSpec((N_DEV, tr, c), lambda i: (0, i, 0))],
        out_specs=pl.BlockSpec((tr, c), lambda i: (i, 0)), out_shape=jax.ShapeDtypeStruct((r, c), f32),
        compiler_params=_params(("parallel",)),
    )(parts)


def _adamw(w, g, m, v, name):
    shape = w.shape
    cols = shape[-1]
    slabs = w.ndim == 3 and shape[1] < 8
    rows = w.size // cols
    tr = rows
    while tr * cols * 4 > (1 << 20) and tr % 16 == 0:
        tr //= 2
    c1 = 1.0 / (1.0 - ADAM_B1 ** ADAM_STEP)
    c2 = 1.0 / (1.0 - ADAM_B2 ** ADAM_STEP)

    def body(w_ref, g_ref, m_ref, v_ref, d_ref, nm_ref, nv_ref):
        gv = g_ref[...]
        mv = ADAM_B1 * m_ref[...] + (1.0 - ADAM_B1) * gv
        vv = ADAM_B2 * v_ref[...] + (1.0 - ADAM_B2) * (gv * gv)
        nm_ref[...] = mv
        nv_ref[...] = vv
        d_ref[...] = -ADAM_LR * ((mv * c1) / (jnp.sqrt(vv * c2) + ADAM_EPS) + ADAM_WD * w_ref[...])

    if slabs:
        tl = max(d for d in range(1, shape[0] + 1) if shape[0] % d == 0 and d * shape[1] * cols * 4 <= (3 << 19))
        spec = pl.BlockSpec((tl, shape[1], cols), lambda i: (i, 0, 0))
        return tuple(pl.pallas_call(
            body, name=name, grid=(shape[0] // tl,), in_specs=[spec] * 4, out_specs=[spec] * 3,
            out_shape=[jax.ShapeDtypeStruct(shape, f32)] * 3, compiler_params=_params(("parallel",)),
        )(w, g, m, v))
    spec = pl.BlockSpec((tr, cols), lambda i: (i, 0))
    outs = pl.pallas_call(
        body, name=name, grid=(rows // tr,), in_specs=[spec] * 4, out_specs=[spec] * 3,
        out_shape=[jax.ShapeDtypeStruct((rows, cols), f32)] * 3, compiler_params=_params(("parallel",)),
    )(*[t.reshape(rows, cols) for t in (w, g, m, v)])
    return tuple(t.reshape(shape) for t in outs)


def _pad_lanes(t, n=HD):
    return jnp.pad(t, [(0, 0)] * (t.ndim - 1) + [(0, n - t.shape[-1])])


def _rows128(t, rows=None):
    t = t.reshape(-1, HD)
    rows = rows or -(-t.shape[0] // 8) * 8
    return jnp.pad(t, ((0, rows - t.shape[0]), (0, 0)))


def kernel(x, norm_w, w_in, lru_conv_w, lru_conv_b, lru_wa, lru_ba, lru_wx, lru_bx, lru_lambda, lru_norm_w, dn_conv_w, dn_A_log, dn_dt_bias, dn_norm_w, w_out, final_norm_w, loss_target, m_norm_w, m_w_in, m_lru_conv_w, m_lru_conv_b, m_lru_wa, m_lru_ba, m_lru_wx, m_lru_bx, m_lru_lambda, m_lru_norm_w, m_dn_conv_w, m_dn_A_log, m_dn_dt_bias, m_dn_norm_w, m_w_out, m_final_norm_w, v_norm_w, v_w_in, v_lru_conv_w, v_lru_conv_b, v_lru_wa, v_lru_ba, v_lru_wx, v_lru_bx, v_lru_lambda, v_lru_norm_w, v_dn_conv_w, v_dn_A_log, v_dn_dt_bias, v_dn_norm_w, v_w_out, v_final_norm_w):
    depth = norm_w.shape[0]
    xs = x[0]
    target = loss_target[0]
    chip = 2 * lax.axis_index("x") + lax.axis_index("y")

    win_b, wout_b = w_in.astype(bf16), w_out.astype(bf16)
    got_in0 = _gather_halves([win_b[0]], [], "gather_weights0")[0]
    rest0 = [wout_b[0], lru_conv_w, dn_conv_w]
    rest0_started = _gather_start(rest0, "gather_rest0_start")
    gathered, later = {}, {}
    pad_cols = jnp.zeros((D_MODEL, D_INP - D_IN), bf16)

    def weights_of(l, after):
        if l == 0:
            got, token = got_in0, rest0_started[-1]
        else:
            gathered[l] = _gather_finish(later[l], [win_b[l], wout_b[l]], after, f"gather_weights{l}_wait")
            got, token = gathered[l][0], None
        return jnp.concatenate([got[j] for j in range(4)] + [pad_cols], axis=1), token

    def rest_of(l, after):
        token = None
        if "rest0" not in gathered:
            gathered["rest0"] = _gather_finish(rest0_started, rest0, after, "gather_rest0_wait")
            for k in range(1, depth):
                later[k] = _gather_start([win_b[k], wout_b[k]], f"gather_weights{k}_start", after=gathered["rest0"][0])
                token = later[k][-1] if token is None else token + later[k][-1]
        g_out, g_lcw, g_dcw = gathered["rest0"]
        if l > 0:
            g_out = gathered[l][1]
        lcw_full = g_lcw.transpose(1, 2, 0, 3).reshape(depth, 4, D_MODEL)
        dcw_full = g_dcw.transpose(1, 2, 0, 3).reshape(depth, 4, 3 * D_MODEL)
        return g_out.reshape(D_MIX, D_MODEL), _behind(lcw_full[l], token), dcw_full[l]

    in_flight = {}

    def on_grad(kind, l, g):
        if l == depth - 1 and depth > 1:
            if kind == "wout":
                in_flight["held"] = g
                return None
            srcs, kinds, key = [g, in_flight.pop("held")], ["win", "wout"], ("both", l)
        else:
            srcs, kinds, key = [g], [kind], (kind, l)
        started = _scatter_start(srcs, kinds, f"scatter_{key[0]}{key[1]}_start")
        in_flight[key] = (started, srcs, kinds)
        return started[-1]

    vecs, dnvs = [], []
    zrow = jnp.zeros((D_MODEL,), f32)
    for l in range(depth):
        vecs.append(jnp.stack([lru_conv_b[l], lru_ba[l], lru_bx[l], lru_lambda[l], lru_norm_w[l], zrow, zrow, zrow]))
        dnvs.append(jnp.concatenate([_pad_lanes(dn_A_log[l][None]), _pad_lanes(dn_dt_bias[l][None]),
                                     dn_norm_w[l][None], jnp.zeros((5, HD), f32)], axis=0))

    loss_part, grad_x, d_fnw, g_nw, g_vec, g_wa, g_wx, g_lcw_, g_dcw_, g_dsc = _local_step(
        xs, target, norm_w, final_norm_w, weights_of, rest_of, on_grad, vecs, dnvs, lru_wa, lru_wx)
    loss = lax.psum(loss_part[0, 0], ("x", "y", "c"))
    grads = _reduce_grads(chip, grad_x, in_flight, d_fnw, g_nw, g_vec, g_wa, g_wx, g_lcw_, g_dcw_, g_dsc)

    names = ["norm_w", "w_in", "lru_conv_w", "lru_conv_b", "lru_wa", "lru_ba", "lru_wx", "lru_bx", "lru_lambda",
             "lru_norm_w", "dn_conv_w", "dn_A_log", "dn_dt_bias", "dn_norm_w", "w_out", "final_norm_w"]
    ws = dict(zip(names, [norm_w, w_in, lru_conv_w, lru_conv_b, lru_wa, lru_ba, lru_wx, lru_bx, lru_lambda, lru_norm_w,
                          dn_conv_w, dn_A_log, dn_dt_bias, dn_norm_w, w_out, final_norm_w]))
    ms = dict(zip(names, [m_norm_w, m_w_in, m_lru_conv_w, m_lru_conv_b, m_lru_wa, m_lru_ba, m_lru_wx, m_lru_bx,
                          m_lru_lambda, m_lru_norm_w, m_dn_conv_w, m_dn_A_log, m_dn_dt_bias, m_dn_norm_w, m_w_out,
                          m_final_norm_w]))
    vs = dict(zip(names, [v_norm_w, v_w_in, v_lru_conv_w, v_lru_conv_b, v_lru_wa, v_lru_ba, v_lru_wx, v_lru_bx,
                          v_lru_lambda, v_lru_norm_w, v_dn_conv_w, v_dn_A_log, v_dn_dt_bias, v_dn_norm_w, v_w_out,
                          v_final_norm_w]))
    to_cols = lambda t: jnp.transpose(t, (2, 0, 1))
    from_cols = lambda t: jnp.transpose(t, (1, 2, 0))
    g_cols = to_cols(grads["w_in"])
    grads["w_in"] = from_cols(g_cols)
    deltas, new_m, new_v = [], [], []
    for n in names:
        if n == "w_in":
            d, nm_, nv_ = (from_cols(t) for t in _adamw(to_cols(ws[n]), g_cols, to_cols(ms[n]), to_cols(vs[n]),
                                                        f"adamw_{n}"))
        else:
            d, nm_, nv_ = _adamw(ws[n], grads[n], ms[n], vs[n], f"adamw_{n}")
        deltas.append(d)
        new_m.append(nm_)
        new_v.append(nv_)
    return (loss, grad_x[None], *[grads[n] for n in names], *deltas, *new_m, *new_v)


def _behind(t, token):
    return t if token is None else t + token[0:1, 0:1]


def _local_step(xs, target, norm_w, final_norm_w, weights_of, rest_of, on_grad, vecs, dnvs, lru_wa, lru_wx):
    depth = norm_w.shape[0]
    saved = []
    h = xs
    for l in range(depth):
        wp_l, token = weights_of(l, h)
        hn = _rms_fwd(h, _behind(norm_w[l][None], token), f"rms_fwd{l}")
        proj = _matmul(hn, wp_l, tm=2048, tn=896, tk=D_MODEL, name=f"in_proj{l}")
        wo_l, lcw_l, dcw_l = rest_of(l, proj)
        y, hl = _lru_fwd(proj, lcw_l, vecs[l], lru_wa[l], lru_wx[l], f"lru_fwd{l}")
        y, o, states, tinvs = _dn_fwd(proj, dcw_l, dnvs[l], y, f"dn_fwd{l}")
        out = _matmul(y, wo_l, tm=512, tn=D_MODEL, tk=D_MIX, res=h, name=f"out_proj{l}")
        saved.append((h, hn, proj, hl, o, states, tinvs, y, wp_l, wo_l, lcw_l, dcw_l))
        h = out

    dh, d_fnw, loss_part = _final_loss(h, final_norm_w[None], target, "final_loss")

    g_nw, g_vec, g_wa, g_wx, g_lcw_, g_dcw_, g_dsc = ([None] * depth for _ in range(7))
    for l in reversed(range(depth)):
        hin, hn, proj, hl, o, states, tinvs, y, wp_l, wo_l, lcw_l, dcw_l = saved[l]
        dy = _matmul(dh, wo_l, tb=True, tm=512, tn=1024, tk=D_MODEL, name=f"d_y{l}")
        g_wout = _matmul(y, dh, ta=True, tm=512, tn=D_MODEL, tk=512, out_dtype=bf16, name=f"d_wout{l}")
        token = on_grad("wout", l, g_wout)
        dlx, dlz, g_lcw_[l], g_vec[l], g_wa[l], g_wx[l] = _lru_bwd(dy, proj, hl, lcw_l, _behind(vecs[l], token),
                                                                 lru_wa[l], lru_wx[l], f"lru_bwd{l}")
        dq, dk, dv, dz, dba, dcw8, g_dsc[l] = _dn_bwd(dy, proj, o, states, tinvs, dcw_l, dnvs[l], f"dn_bwd{l}")
        g_dcw_[l] = dcw8.reshape(HEADS, 4, 3, HD).transpose(1, 2, 0, 3).reshape(4, 3 * D_MODEL)
        dxs = [dlx, dlz, dq, dk, dv, dz]
        g_win = _d_win(_transpose(hn, f"hn_t{l}"), dxs, dba, f"d_win{l}")
        token = on_grad("win", l, g_win)
        dhn = _d_hn(dxs, _behind(dba, token), wp_l, f"d_hn{l}")
        dh, g_nw[l] = _rms_bwd(dhn, hin, norm_w[l][None], dh, f"rms_bwd{l}")
    return loss_part, dh, d_fnw, g_nw, g_vec, g_wa, g_wx, g_lcw_, g_dcw_, g_dsc


def _reduce_grads(chip, after, in_flight, d_fnw, g_nw, g_vec, g_wa, g_wx, g_lcw_, g_dcw_, g_dsc):
    depth = len(g_nw)
    both = lambda f: jnp.concatenate([f(l) for l in range(depth)])
    small = [
        both(lambda l: _rows128(g_nw[l])),
        both(lambda l: _rows128(g_vec[l][0])), both(lambda l: _rows128(g_vec[l][1])),
        both(lambda l: _rows128(g_vec[l][2])), both(lambda l: _rows128(g_vec[l][3])),
        both(lambda l: _rows128(g_vec[l][4])),
        both(lambda l: _rows128(g_wa[l])), both(lambda l: _rows128(g_wx[l])),
        both(lambda l: _rows128(g_dsc[l][0:1])), both(lambda l: _rows128(g_dsc[l][1:2])),
        both(lambda l: _rows128(g_dsc[l][2:3])),
        _rows128(d_fnw),
        both(lambda l: _rows128(g_lcw_[l])), both(lambda l: _rows128(g_dcw_[l])),
    ]
    counts = [t.shape[0] for t in small]
    total = sum(counts)
    per = -(-total // (8 * N_DEV)) * 8
    small = jnp.concatenate(small + [jnp.zeros((per * N_DEV - total, HD), f32)]).reshape(N_DEV, per, HD)

    small_started = _scatter_start([small], ["small"], "scatter_small_start")
    reduced = {}
    for key in sorted(in_flight, key=lambda k: -k[1]):
        started, srcs, kinds = in_flight[key]
        got = _scatter_finish(started, srcs, kinds, after, f"scatter_{key[0]}{key[1]}_wait")
        for kind, part in zip(kinds, got):
            reduced[(kind, key[1])] = _sum_parts(part, f"sum_{kind}{key[1]}")
    got = _scatter_finish(small_started, [small], ["small"], after, "scatter_small_wait")
    pairs = [None] * depth
    a_small = None
    for l in reversed(range(depth)):
        sums = [reduced[("win", l)], reduced[("wout", l)]]
        if l == 0:
            shared = _share_reduced(sums, _sum_parts(got[0], "sum_small"), f"share_grads{l}")
            a_small = shared[2].reshape(N_DEV * per, HD)
        else:
            shared = _share_reduced(sums, None, f"share_grads{l}")
        pairs[l] = shared
    grad_w_in = jnp.stack([lax.dynamic_slice_in_dim(pairs[l][0].reshape(D_MODEL, WIN_W), 4 * chip, SHARD_IN, 1)
                           for l in range(depth)])
    grad_w_out = jnp.stack([pairs[l][1].reshape(D_MIX // 4, D_MODEL) for l in range(depth)])

    offs = [0]
    for c in counts:
        offs.append(offs[-1] + c)

    def piece(k, rows_per_layer=None):
        t = a_small[offs[k]:offs[k + 1]]
        if rows_per_layer is None:
            return t
        return t.reshape(depth, -1, HD)[:, :rows_per_layer]

    vec = lambda k: piece(k).reshape(depth, D_MODEL)
    return {
        "norm_w": vec(0), "lru_conv_b": vec(1), "lru_ba": vec(2), "lru_bx": vec(3), "lru_lambda": vec(4),
        "lru_norm_w": vec(5),
        "lru_wa": piece(6).reshape(depth, HEADS, HD, HD), "lru_wx": piece(7).reshape(depth, HEADS, HD, HD),
        "dn_A_log": piece(8, 1)[:, 0, :HEADS], "dn_dt_bias": piece(9, 1)[:, 0, :HEADS], "dn_norm_w": piece(10, 1)[:, 0],
        "final_norm_w": piece(11).reshape(D_MODEL),
        "lru_conv_w": lax.dynamic_slice_in_dim(piece(12).reshape(depth, 4, D_MODEL), chip * (D_MODEL // 4), D_MODEL // 4, 2),
        "dn_conv_w": lax.dynamic_slice_in_dim(piece(13).reshape(depth, 4, 3 * D_MODEL), chip * (3 * D_MODEL // 4),
                                              3 * D_MODEL // 4, 2),
        "w_in": grad_w_in, "w_out": grad_w_out,
    }
```

```python
import jax
import jax.numpy as jnp
from jax import lax
from jax.experimental import pallas as pl
from jax.experimental.pallas import tpu as pltpu

f32 = jnp.float32
bf16 = jnp.bfloat16
HI = lax.Precision.HIGHEST
MESH = pl.DeviceIdType.MESH

D_MODEL = 1024
HEADS = 8
HD = 128
CHUNK = 64
LRU_T = 128
SEQ_BLOCK = 1024
LRU_HP = 4
DN_SEQ_BLOCK = 256
DN_HP = 8
LRU_C = 8.0
D_IN = 6160
D_INP = 6272
D_MIX = 2048
N_GROUPS = 6
BLK_BA = 48
SHARD_IN = D_IN // 4
WIN_W = 1664
WIN_STRIDE = 1536
EPS = 1e-6
QSCALE = HD ** -0.5
N_DEV = 8
VMEM_LIMIT = 56 * 1024 * 1024

ADAM_LR, ADAM_B1, ADAM_B2, ADAM_EPS, ADAM_WD, ADAM_STEP = 0.001, 0.9, 0.999, 1e-08, 0.01, 10


def _sig(x):
    return 1.0 / (1.0 + jnp.exp(-x))


def _log1p(u):
    w = 1.0 + u
    d = w - 1.0
    return jnp.where(d == 0.0, u, jnp.log(w) * (u / jnp.where(d == 0.0, 1.0, d)))


def _softplus(x):
    return jnp.maximum(x, 0.0) + _log1p(jnp.exp(-jnp.abs(x)))


def _expm1(x):
    t = x * (1.0 + x * (0.5 + x * (1.0 / 6 + x * (1.0 / 24 + x * (1.0 / 120 + x * (1.0 / 720))))))
    return jnp.where(jnp.abs(x) < 0.2, t, jnp.exp(x) - 1.0)


def _dot(a, b, prec=None):
    return jnp.dot(a, b, precision=prec, preferred_element_type=f32)


def _dot_nt(a, b, prec=None):
    return lax.dot_general(a, b, (((1,), (1,)), ((), ())), precision=prec, preferred_element_type=f32)


def _dot_tn(a, b, prec=None):
    return lax.dot_general(a, b, (((0,), (0,)), ((), ())), precision=prec, preferred_element_type=f32)


def _b(x):
    return x.astype(bf16)


def _sum0(x):
    return jnp.sum(x, axis=0, keepdims=True)


def _sum1(x):
    return jnp.sum(x, axis=1, keepdims=True)


def _rowmean(x):
    return jnp.mean(x, axis=-1, keepdims=True)


def _dsilu(z, sg):
    return sg * (1.0 + z * (1.0 - sg))


def _conv_taps(x, halo):
    xx = jnp.concatenate([halo, x], axis=0)
    return [pltpu.roll(xx, 3, axis=0)[8:], pltpu.roll(xx, 2, axis=0)[8:], pltpu.roll(xx, 1, axis=0)[8:], x]


def _conv(xs, cw):
    return cw[0:1] * xs[0] + cw[1:2] * xs[1] + cw[2:3] * xs[2] + cw[3:4] * xs[3]


def _rows_before(ref, halo_ref, lanes, t0, c, sblk):
    tprev = pl.multiple_of(jnp.maximum(t0 - 8, 0), 8)
    return jnp.where(c > 0, ref[pl.ds(tprev, 8), lanes], jnp.where(sblk > 0, halo_ref[:, lanes], 0.0))


def _conv_back(dxc, halo_after, cw, x=None):
    rows = dxc.shape[0]
    dd = jnp.concatenate([dxc, halo_after], axis=0)
    out = cw[3:4] * dxc
    dcw = [None, None, None, None if x is None else _sum0(x * dxc)]
    for s in (1, 2, 3):
        shifted = pltpu.roll(dd, rows + 8 - s, axis=0)[:rows]
        out = out + cw[3 - s:4 - s] * shifted
        if x is not None:
            dcw[3 - s] = _sum0(x * shifted)
    return out if x is None else (out, jnp.concatenate(dcw, axis=0))


def _params(sem=None):
    return pltpu.CompilerParams(dimension_semantics=sem, vmem_limit_bytes=VMEM_LIMIT)


def _seq_specs(sb, width, rowblk, colblk):
    main = pl.BlockSpec((sb, width), lambda a, b: (rowblk(a, b), colblk(a, b)))
    halo = pl.BlockSpec((8, width), lambda a, b: (jnp.maximum(rowblk(a, b) * (sb // 8) - 1, 0), colblk(a, b)))
    return main, halo


def _matmul(a, b, *, ta=False, tb=False, tm, tn, tk, res=None, out_dtype=f32, name):
    if ta:
        kdim, m = a.shape
    else:
        m, kdim = a.shape
    n = b.shape[0] if tb else b.shape[1]
    tm, tn, tk = min(tm, m), min(tn, n), min(tk, kdim)
    assert m % tm == 0 and n % tn == 0 and kdim % tk == 0, (name, m, n, kdim, tm, tn, tk)
    nk = kdim // tk
    dims = (((0 if ta else 1,), (1 if tb else 0,)), ((), ()))
    has_res = res is not None

    def body(*refs):
        a_ref, b_ref = refs[:2]
        r_ref = refs[2] if has_res else None
        o_ref = refs[3] if has_res else refs[2]
        acc_ref = refs[-1] if nk > 1 else None
        part = lax.dot_general(_b(a_ref[...]), _b(b_ref[...]), dims, preferred_element_type=f32)

        def finish(total):
            if has_res:
                total = total + r_ref[...]
            o_ref[...] = total.astype(out_dtype)

        if nk == 1:
            finish(part)
        else:
            k = pl.program_id(2)

            @pl.when(k == 0)
            def _():
                acc_ref[...] = part

            @pl.when(k > 0)
            def _():
                acc_ref[...] += part

            @pl.when(k == nk - 1)
            def _():
                finish(acc_ref[...])

    a_spec = pl.BlockSpec((tk, tm), lambda i, j, k: (k, i)) if ta else pl.BlockSpec((tm, tk), lambda i, j, k: (i, k))
    b_spec = pl.BlockSpec((tn, tk), lambda i, j, k: (j, k)) if tb else pl.BlockSpec((tk, tn), lambda i, j, k: (k, j))
    o_spec = pl.BlockSpec((tm, tn), lambda i, j, k: (i, j))
    in_specs, args = [a_spec, b_spec], [a, b]
    if has_res:
        in_specs.append(o_spec)
        args.append(res)
    return pl.pallas_call(
        body, name=name, grid=(m // tm, n // tn, nk), in_specs=in_specs, out_specs=o_spec,
        out_shape=jax.ShapeDtypeStruct((m, n), out_dtype),
        scratch_shapes=[pltpu.VMEM((tm, tn), f32)] if nk > 1 else [],
        compiler_params=_params(("parallel", "parallel", "arbitrary")),
    )(*args)


def _transpose(x, name):
    r, c = x.shape
    tr, tc = min(512, r), min(512, c)

    def body(x_ref, o_ref):
        o_ref[...] = x_ref[...].T

    return pl.pallas_call(
        body, name=name, grid=(r // tr, c // tc), in_specs=[pl.BlockSpec((tr, tc), lambda i, j: (i, j))],
        out_specs=pl.BlockSpec((tc, tr), lambda i, j: (j, i)), out_shape=jax.ShapeDtypeStruct((c, r), x.dtype),
        compiler_params=_params(("parallel", "parallel")),
    )(x)


def _d_hn(dxs, dba, wp, name):
    s = dxs[0].shape[0]
    tm = min(512, s)

    def body(*refs):
        dx_refs = refs[:N_GROUPS]
        dba_ref, w_ref, o_ref = refs[N_GROUPS:]
        total = _dot_nt(_b(dba_ref[...]), w_ref[:, BLK_BA * HD:])
        for g in range(N_GROUPS):
            total = total + _dot_nt(dx_refs[g][...], w_ref[:, g * D_MODEL:(g + 1) * D_MODEL])
        o_ref[...] = total

    row = lambda w: pl.BlockSpec((tm, w), lambda i: (i, 0))
    return pl.pallas_call(
        body, name=name, grid=(s // tm,),
        in_specs=[row(D_MODEL)] * N_GROUPS + [row(HD), pl.BlockSpec((D_MODEL, D_INP), lambda i: (0, 0))],
        out_specs=row(D_MODEL), out_shape=jax.ShapeDtypeStruct((s, D_MODEL), f32),
        compiler_params=_params(("parallel",)),
    )(*dxs, dba, wp)


def _d_win(hnt, dxs, dba, name):
    s = hnt.shape[1]
    tn = 256
    per = D_MODEL // tn

    def body(*refs):
        hnt_ref = refs[0]
        dx_refs = refs[1:1 + N_GROUPS]
        o_ref = refs[1 + N_GROUPS]
        j = pl.program_id(0)
        for g in range(N_GROUPS):
            @pl.when(j // per == g)
            def _(g=g):
                o_ref[...] = _dot(hnt_ref[...], _b(dx_refs[g][...])).astype(bf16)

    def dx_spec(g):
        on = lambda j: (j // per == g).astype(jnp.int32)
        return pl.BlockSpec((s, tn), lambda j: (0, (j % per) * on(j)))

    main = pl.pallas_call(
        body, name=name, grid=(N_GROUPS * per,),
        in_specs=[pl.BlockSpec((D_MODEL, s), lambda j: (0, 0))] + [dx_spec(g) for g in range(N_GROUPS)],
        out_specs=pl.BlockSpec((D_MODEL, tn), lambda j: (0, j)),
        out_shape=jax.ShapeDtypeStruct((D_MODEL, D_INP), bf16),
        compiler_params=_params(("parallel",)),
    )(hnt, *dxs)

    def tail(hnt_ref, dba_ref, full_ref, o_ref):
        del full_ref
        o_ref[...] = _dot(hnt_ref[...], _b(dba_ref[...])).astype(bf16)

    return pl.pallas_call(
        tail, name=name + "_ba", grid=(1,),
        in_specs=[pl.BlockSpec((D_MODEL, s), lambda k: (0, 0)), pl.BlockSpec((s, HD), lambda k: (0, 0)),
                  pl.BlockSpec(memory_space=pl.ANY)],
        out_specs=pl.BlockSpec((D_MODEL, HD), lambda k: (0, BLK_BA)),
        out_shape=jax.ShapeDtypeStruct((D_MODEL, D_INP), bf16),
        input_output_aliases={2: 0},
        compiler_params=_params(("arbitrary",)),
    )(hnt, dba, main)


def _rms_fwd(x, w, name):
    s = x.shape[0]
    tr = min(512, s)

    def body(x_ref, w_ref, h_ref):
        xv = x_ref[...]
        r = lax.rsqrt(_rowmean(xv * xv) + EPS)
        h_ref[...] = (xv * r * w_ref[...]).astype(bf16)

    return pl.pallas_call(
        body, name=name, grid=(s // tr,),
        in_specs=[pl.BlockSpec((tr, D_MODEL), lambda i: (i, 0)), pl.BlockSpec((1, D_MODEL), lambda i: (0, 0))],
        out_specs=pl.BlockSpec((tr, D_MODEL), lambda i: (i, 0)),
        out_shape=jax.ShapeDtypeStruct((s, D_MODEL), bf16), compiler_params=_params(("parallel",)),
    )(x, w)


def _rms_bwd(dh, x, w, dres, name):
    s = x.shape[0]
    tr = min(512, s)

    def body(dh_ref, x_ref, w_ref, dres_ref, dx_ref, dw_ref):
        xv = x_ref[...]
        r = lax.rsqrt(_rowmean(xv * xv) + EPS)
        xn = xv * r
        d = dh_ref[...]
        dxn = d * w_ref[...]
        dx_ref[...] = dres_ref[...] + r * (dxn - xn * _rowmean(dxn * xn))
        part = _sum0(d * xn)

        @pl.when(pl.program_id(0) == 0)
        def _():
            dw_ref[...] = part

        @pl.when(pl.program_id(0) > 0)
        def _():
            dw_ref[...] += part

    row = pl.BlockSpec((tr, D_MODEL), lambda i: (i, 0))
    vec = pl.BlockSpec((1, D_MODEL), lambda i: (0, 0))
    return pl.pallas_call(
        body, name=name, grid=(s // tr,), in_specs=[row, row, vec, row], out_specs=[row, vec],
        out_shape=[jax.ShapeDtypeStruct((s, D_MODEL), f32), jax.ShapeDtypeStruct((1, D_MODEL), f32)],
        compiler_params=_params(("arbitrary",)),
    )(dh, x, w, dres)


def _final_loss(x, w, target, name):
    s = x.shape[0]
    tr = min(512, s)

    def body(x_ref, w_ref, t_ref, dx_ref, dw_ref, loss_ref):
        xv = x_ref[...]
        wv = w_ref[...]
        r = lax.rsqrt(_rowmean(xv * xv) + EPS)
        xn = xv * r
        e = xn * wv - t_ref[...]
        lb = jnp.broadcast_to(0.5 * _sum0(_rowmean(e * e)), (1, HD))
        dy = e * (1.0 / D_MODEL)
        dxn = dy * wv
        dx_ref[...] = r * (dxn - xn * _rowmean(dxn * xn))
        part = _sum0(dy * xn)

        @pl.when(pl.program_id(0) == 0)
        def _():
            dw_ref[...] = part
            loss_ref[...] = lb

        @pl.when(pl.program_id(0) > 0)
        def _():
            dw_ref[...] += part
            loss_ref[...] += lb

    row = pl.BlockSpec((tr, D_MODEL), lambda i: (i, 0))
    vec = pl.BlockSpec((1, D_MODEL), lambda i: (0, 0))
    lsp = pl.BlockSpec((1, HD), lambda i: (0, 0))
    return pl.pallas_call(
        body, name=name, grid=(s // tr,), in_specs=[row, vec, row], out_specs=[row, vec, lsp],
        out_shape=[jax.ShapeDtypeStruct((s, D_MODEL), f32), jax.ShapeDtypeStruct((1, D_MODEL), f32),
                   jax.ShapeDtypeStruct((1, HD), f32)],
        compiler_params=_params(("arbitrary",)),
    )(x, w, target)


ALL = slice(None)


def _lru_gates(xc, vec, wa, wx):
    sp = _softplus(-vec[3:4])
    xcb = _b(xc)
    r = _sig(_dot(xcb, wa) + vec[1:2])
    i = _sig(_dot(xcb, wx) + vec[2:3])
    la = (-LRU_C) * r * sp
    a = jnp.exp(la)
    mult = jnp.sqrt(-_expm1(2.0 * la))
    return r, i, a, mult, sp, xcb


def _lru_fwd(proj, cw, vec, wa, wx, name):
    s = proj.shape[0]
    sb = min(SEQ_BLOCK, s)
    nsb = s // sb
    t = min(LRU_T, sb)
    nc = sb // t
    hp = LRU_HP
    wide = hp * HD
    lanes = [slice(j * HD, (j + 1) * HD) for j in range(hp)]

    def body(x_ref, xh_ref, z_ref, cw_ref, vec_ref, wa_ref, wx_ref, y_ref, hl_ref, hc_ref):
        sblk = pl.program_id(1)
        consts = [(cw_ref[:, hl], vec_ref[:, hl], _b(wa_ref[j]), _b(wx_ref[j])) for j, hl in enumerate(lanes)]
        row = lax.broadcasted_iota(jnp.int32, (t, HD), 0)

        @pl.when(sblk == 0)
        def _():
            hc_ref[...] = jnp.zeros((8, wide), f32)

        def one_head(j, x, halo, z, hcarry):
            cwv, vec_v, wa_v, wx_v = consts[j]
            xs = _conv_taps(x, halo)
            xc = vec_v[0:1] + _conv(xs, cwv)
            r, i, a, mult, _, _ = _lru_gates(xc, vec_v, wa_v, wx_v)
            yield
            bb = mult * (i * xc)
            d = 1
            while d < t:
                m = row >= d
                bb = jnp.where(m, a * pltpu.roll(bb, d, axis=0) + bb, bb)
                a = jnp.where(m, a * pltpu.roll(a, d, axis=0), a)
                d *= 2
                yield
            hl = bb + a * hcarry
            nrm = lax.rsqrt(_rowmean(hl * hl) + EPS)
            return hl, (hl * nrm * vec_v[4:5] * (z * _sig(z))).astype(bf16)

        def chunk(c, carries):
            t0 = pl.multiple_of(c * t, t)
            rows = pl.ds(t0, t)
            ins = [(x_ref[rows, hl], _rows_before(x_ref, xh_ref, hl, t0, c, sblk), z_ref[rows, hl]) for hl in lanes]
            outs = _round_robin([one_head(j, *ins[j], carries[j]) for j in range(hp)])
            for j, hl in enumerate(lanes):
                hl_ref[rows, hl] = outs[j][0]
                y_ref[rows, hl] = outs[j][1]
            return tuple(out[0][t - 1:t, :] for out in outs)

        final = lax.fori_loop(0, nc, chunk, tuple(hc_ref[0:1, hl] for hl in lanes))
        hc_ref[...] = jnp.concatenate([jnp.broadcast_to(f, (8, HD)) for f in final], axis=1)

    xsp, xhalo = _seq_specs(sb, wide, lambda g, i: i, lambda g, i: g)
    zsp, _ = _seq_specs(sb, wide, lambda g, i: i, lambda g, i: HEADS // hp + g)
    gcol = lambda g, i: (0, g)
    wsp = pl.BlockSpec((hp, HD, HD), lambda g, i: (g, 0, 0))
    osp = pl.BlockSpec((sb, wide), lambda g, i: (i, g))
    return pl.pallas_call(
        body, name=name, grid=(HEADS // hp, nsb),
        in_specs=[xsp, xhalo, zsp, pl.BlockSpec((4, wide), gcol), pl.BlockSpec((8, wide), gcol), wsp, wsp],
        out_specs=[osp, osp],
        out_shape=[jax.ShapeDtypeStruct((s, D_MIX), bf16), jax.ShapeDtypeStruct((s, D_MODEL), f32)],
        scratch_shapes=[pltpu.VMEM((8, wide), f32)],
        compiler_params=_params(("parallel", "arbitrary")),
    )(proj, proj, proj, cw, vec, wa, wx)


def _lru_bwd(dy, proj, hl, cw, vec, wa, wx, name):
    s = proj.shape[0]
    sb = min(SEQ_BLOCK, s)
    nsb = s // sb
    t = min(LRU_T, sb)
    nc = sb // t
    hp = LRU_HP
    wide = hp * HD
    lanes = [slice(j * HD, (j + 1) * HD) for j in range(hp)]
    n_acc = 10

    def body(dy_ref, x_ref, xh_ref, z_ref, hl_ref, hlh_ref, cw_ref, vec_ref, wa_ref, wx_ref,
             dx_ref, dz_ref, dcw_ref, dvec_ref, dwa_ref, dwx_ref, acc_ref, dxh_ref):
        step = pl.program_id(1)
        sblk = nsb - 1 - step
        consts = [(cw_ref[:, hl], vec_ref[:, hl], _b(wa_ref[j]), _b(wx_ref[j])) for j, hl in enumerate(lanes)]
        row = lax.broadcasted_iota(jnp.int32, (t, HD), 0)

        @pl.when(step == 0)
        def _():
            acc_ref[...] = jnp.zeros((16, wide), f32)
            dxh_ref[...] = jnp.zeros((8, wide), f32)
            dwa_ref[...] = jnp.zeros((hp, HD, HD), f32)
            dwx_ref[...] = jnp.zeros((hp, HD, HD), f32)

        def one_head(j, x, halo, z, hv, hhalo, dyv, carry):
            dcw0, dcw1, dcw2, dcw3, dcb, dba, dbx, dsp, dlnw, mu, dxc_halo = carry
            cwv, vec_v, wa_v, wx_v = consts[j]
            lnw = vec_v[4:5]
            xs = _conv_taps(x, halo)
            xc = vec_v[0:1] + _conv(xs, cwv)
            r, i, a, mult, sp, xcb = _lru_gates(xc, vec_v, wa_v, wx_v)
            yield
            hprev = pltpu.roll(jnp.concatenate([hhalo, hv], axis=0), 1, axis=0)[8:]
            sgz = _sig(z)
            sz = z * sgz
            nrm = lax.rsqrt(_rowmean(hv * hv) + EPS)
            hn = hv * nrm
            dlnw = dlnw + _sum0(dyv * hn * sz)
            dzv = _b(dyv * hn * lnw * _dsilu(z, sgz))
            dhn = dyv * lnw * sz
            lam = nrm * (dhn - hn * _rowmean(dhn * hn))
            cf = jnp.where(row == t - 1, 1.0, pltpu.roll(a, t - 1, axis=0))
            d = 1
            while d < t:
                m = row < t - d
                lam = jnp.where(m, lam + cf * pltpu.roll(lam, t - d, axis=0), lam)
                cf = jnp.where(m, cf * pltpu.roll(cf, t - d, axis=0), cf)
                d *= 2
                yield
            lam = lam + cf * mu
            mu = a[0:1] * lam[0:1]
            da = lam * hprev
            dmult = lam * (i * xc)
            di = lam * mult * xc
            dxc = lam * mult * i
            dla = da * a - dmult * (a * a) / mult
            dr = dla * (-LRU_C * sp)
            dsp = dsp + _sum0(dla * (-LRU_C * r))
            dra = dr * r * (1.0 - r)
            dia = di * i * (1.0 - i)
            dba = dba + _sum0(dra)
            dbx = dbx + _sum0(dia)
            drab, diab = _b(dra), _b(dia)
            dwa = _dot_tn(xcb, drab)
            dwx = _dot_tn(xcb, diab)
            dxc = dxc + _dot_nt(drab, wa_v) + _dot_nt(diab, wx_v)
            yield
            dcb = dcb + _sum0(dxc)
            dcw0 = dcw0 + _sum0(dxc * xs[0])
            dcw1 = dcw1 + _sum0(dxc * xs[1])
            dcw2 = dcw2 + _sum0(dxc * xs[2])
            dcw3 = dcw3 + _sum0(dxc * xs[3])
            dxv = _b(_conv_back(dxc, dxc_halo, cwv))
            return dxv, dzv, dwa, dwx, (dcw0, dcw1, dcw2, dcw3, dcb, dba, dbx, dsp, dlnw, mu, dxc[0:8])

        def chunk(ci, carries):
            c = nc - 1 - ci
            t0 = pl.multiple_of(c * t, t)
            rows = pl.ds(t0, t)
            ins = [(x_ref[rows, hl], _rows_before(x_ref, xh_ref, hl, t0, c, sblk), z_ref[rows, hl], hl_ref[rows, hl],
                    _rows_before(hl_ref, hlh_ref, hl, t0, c, sblk), dy_ref[rows, hl]) for hl in lanes]
            outs = _round_robin([one_head(j, *ins[j], carries[j]) for j in range(hp)])
            for j, hl in enumerate(lanes):
                dx_ref[rows, hl] = outs[j][0]
                dz_ref[rows, hl] = outs[j][1]
                dwa_ref[j] += outs[j][2]
                dwx_ref[j] += outs[j][3]
            return tuple(out[4] for out in outs)

        acc = acc_ref[...]
        dxh = dxh_ref[...]
        init = tuple(tuple(acc[k:k + 1, hl] for k in range(n_acc)) + (dxh[:, hl],) for hl in lanes)
        final = lax.fori_loop(0, nc, chunk, init)
        rows_out = [jnp.concatenate([final[j][k] for j in range(hp)], axis=1) for k in range(n_acc)]
        zero = jnp.zeros((1, wide), f32)
        acc_ref[...] = jnp.concatenate(rows_out + [zero] * (16 - n_acc), axis=0)
        dxh_ref[...] = jnp.concatenate([final[j][n_acc] for j in range(hp)], axis=1)

        @pl.when(step == nsb - 1)
        def _():
            dcw_ref[...] = jnp.concatenate(rows_out[0:4], axis=0)
            dlam = -rows_out[7] * _sig(-vec_ref[3:4, :])
            dvec_ref[...] = jnp.concatenate([rows_out[4], rows_out[5], rows_out[6], dlam, rows_out[8], zero, zero, zero],
                                            axis=0)

    rblk = lambda g, i: nsb - 1 - i
    xsp, xhalo = _seq_specs(sb, wide, rblk, lambda g, i: g)
    zsp, _ = _seq_specs(sb, wide, rblk, lambda g, i: HEADS // hp + g)
    gcol = lambda g, i: (0, g)
    wsp = pl.BlockSpec((hp, HD, HD), lambda g, i: (g, 0, 0))
    return pl.pallas_call(
        body, name=name, grid=(HEADS // hp, nsb),
        in_specs=[xsp, xsp, xhalo, zsp, xsp, xhalo, pl.BlockSpec((4, wide), gcol), pl.BlockSpec((8, wide), gcol), wsp, wsp],
        out_specs=[xsp, xsp, pl.BlockSpec((4, wide), gcol), pl.BlockSpec((8, wide), gcol), wsp, wsp],
        out_shape=[jax.ShapeDtypeStruct((s, D_MODEL), bf16), jax.ShapeDtypeStruct((s, D_MODEL), bf16),
                   jax.ShapeDtypeStruct((4, D_MODEL), f32), jax.ShapeDtypeStruct((8, D_MODEL), f32),
                   jax.ShapeDtypeStruct((HEADS, HD, HD), f32), jax.ShapeDtypeStruct((HEADS, HD, HD), f32)],
        scratch_shapes=[pltpu.VMEM((16, wide), f32), pltpu.VMEM((8, wide), f32)],
        compiler_params=_params(("parallel", "arbitrary")),
    )(dy, proj, proj, proj, hl, hl, cw, vec, wa, wx)


def _lane_pick(x, lane, idx):
    return _sum1(jnp.where(lane == idx, x, 0.0))


def _round_robin(gens):
    results = [None] * len(gens)
    live = list(range(len(gens)))
    while live:
        for i in list(live):
            try:
                next(gens[i])
            except StopIteration as done:
                results[i] = done.value
                live.remove(i)
    return results


def _sdot(a, b):
    return _dot(_b(a), _b(b))


def _sdot_tn(a, b):
    return _dot_tn(_b(a), _b(b))


def _sdot_nt(a, b):
    return _dot_nt(_b(a), _b(b))


def _tri_dot(tri, x):
    trib = tri.astype(bf16)
    x1 = x.astype(bf16)
    r1 = x - x1.astype(f32)
    x2 = r1.astype(bf16)
    x3 = (r1 - x2.astype(f32)).astype(bf16)
    return _dot(trib, x1) + _dot(trib, x2) + _dot(trib, x3)


def _inv_unit_lower(a):
    n = -a
    p = _sdot(a, a)
    yield
    for k in range(5):
        n = n + p + _sdot(n, p)
        if k < 4:
            p = _sdot(p, p)
        yield
    return n


def _dn_chunk(x3, halo3, bav, cw3, dnv, h, masks, cpre=None, tinv=None):
    lane, ri, ci, eye, ltri = masks
    if cpre is None:
        cpre = _conv(_conv_taps(x3, halo3), cw3)
    sg = _sig(cpre)
    act = cpre * sg
    q_raw, k_raw, v = act[:, 0:HD], act[:, HD:2 * HD], act[:, 2 * HD:3 * HD]
    rq = lax.rsqrt(_sum1(q_raw * q_raw) + EPS)
    rk = lax.rsqrt(_sum1(k_raw * k_raw) + EPS)
    qn = q_raw * rq
    k = k_raw * rk
    q = qn * QSCALE
    b_in = jnp.broadcast_to(_lane_pick(bav, lane, h), (CHUNK, HD))
    a_in = jnp.broadcast_to(_lane_pick(bav, lane, HEADS + h), (CHUNK, HD))
    lane1 = lane[0:1]
    ea = jnp.exp(_lane_pick(dnv[0:1], lane1, h))
    dt = _lane_pick(dnv[1:2], lane1, h)
    beta = _sig(b_in)
    sp = _softplus(a_in + dt)
    g = -ea * sp
    gc = _tri_dot(ltri, g)
    yield
    gc64 = gc[:, 0:CHUNK]
    grow = _sum0(gc64 * eye)
    dm = jnp.exp(jnp.where(ri >= ci, gc64 - grow, -1e30))
    ds_ = jnp.where(ri > ci, dm, 0.0)
    eg = jnp.exp(gc)
    glast = gc[CHUNK - 1:CHUNK, :]
    ek = jnp.exp(glast - gc)
    kb = k * beta
    kbf = _b(k)
    amat = _dot_nt(_b(kb), kbf) * ds_
    pmat = _dot_nt(_b(q), kbf) * dm
    yield
    if tinv is None:
        tinv = yield from _inv_unit_lower(amat)
    return dict(cpre=cpre, sg=sg, rq=rq, rk=rk, qn=qn, q=q, k=k, v=v, kbf=kbf, b_in=b_in, a_in=a_in, ea=ea,
                dt=dt, beta=beta, g=g, gc=gc, dm=dm, ds=ds_, eg=eg, glast=glast, ek=ek, kb=kb, amat=amat,
                tinv=tinv, pmat=pmat)


def _dn_masks():
    lane = lax.broadcasted_iota(jnp.int32, (CHUNK, HD), 1)
    ri = lax.broadcasted_iota(jnp.int32, (CHUNK, CHUNK), 0)
    ci = lax.broadcasted_iota(jnp.int32, (CHUNK, CHUNK), 1)
    eye = (ri == ci).astype(f32)
    ltri = (ri >= ci).astype(f32)
    return lane, ri, ci, eye, ltri


def _dn_load_qkv(q_ref, qh_ref, k_ref, kh_ref, v_ref, vh_ref, hl, rows, t0, c, sblk):
    x3 = jnp.concatenate([q_ref[rows, hl], k_ref[rows, hl], v_ref[rows, hl]], axis=1)
    halo3 = jnp.concatenate([_rows_before(q_ref, qh_ref, hl, t0, c, sblk), _rows_before(k_ref, kh_ref, hl, t0, c, sblk),
                             _rows_before(v_ref, vh_ref, hl, t0, c, sblk)], axis=1)
    return x3, halo3


def _dn_cw3(cwq_ref, cwk_ref, cwv_ref, j):
    return jnp.concatenate([r[:, j * HD:(j + 1) * HD] for r in (cwq_ref, cwk_ref, cwv_ref)], axis=1)


def _dn_fwd(proj, cw, dnv, y, name):
    s = proj.shape[0]
    sb = min(DN_SEQ_BLOCK, s)
    nsb = s // sb
    nc = sb // CHUNK
    hp = DN_HP

    def body(q_ref, qh_ref, k_ref, kh_ref, v_ref, vh_ref, z_ref, ba_ref, cwq_ref, cwk_ref, cwv_ref, dnv_ref, yin_ref,
             y_ref, o_ref, st_ref, ti_ref, cp_ref, s_ref):
        del yin_ref
        grp = pl.program_id(0)
        sblk = pl.program_id(1)
        masks = _dn_masks()
        dnv_v = dnv_ref[...]
        cw3 = [_dn_cw3(cwq_ref, cwk_ref, cwv_ref, j) for j in range(hp)]

        @pl.when(sblk == 0)
        def _():
            s_ref[...] = jnp.zeros((hp, HD, HD), f32)

        def one_head(j, x3, halo3, bav, z, st):
            f = yield from _dn_chunk(x3, halo3, bav, cw3[j], dnv_v, grp * hp + j, masks)
            sbf = _b(st)
            qs = _dot(_b(f["q"] * f["eg"]), sbf)
            rhs = f["beta"] * (f["v"] - _dot(_b(f["k"] * f["eg"]), sbf))
            yield
            vn = rhs + _sdot(f["tinv"], rhs)
            yield
            vnb = _b(vn)
            o = qs + _dot(_b(f["pmat"]), vnb)
            st_new = st * jnp.exp(f["glast"]) + _dot_tn(_b(f["k"] * f["ek"]), vnb)
            yield
            nrm = lax.rsqrt(_rowmean(o * o) + EPS)
            yv = (o * nrm * dnv_v[2:3] * (z * _sig(z))).astype(bf16)
            return o, yv, st_new, f["tinv"], f["cpre"]

        def chunk(c, states):
            t0 = pl.multiple_of(c * CHUNK, CHUNK)
            rows = pl.ds(t0, CHUNK)
            bav = ba_ref[rows, :]
            ins = []
            for j in range(hp):
                hl = slice(j * HD, (j + 1) * HD)
                ins.append(_dn_load_qkv(q_ref, qh_ref, k_ref, kh_ref, v_ref, vh_ref, hl, rows, t0, c, sblk)
                           + (bav, z_ref[rows, hl]))
            outs = _round_robin([one_head(j, *ins[j], states[j]) for j in range(hp)])
            for j in range(hp):
                hl = slice(j * HD, (j + 1) * HD)
                st_ref[j, c] = states[j]
                o_ref[rows, hl] = outs[j][0]
                y_ref[rows, hl] = outs[j][1]
                ti_ref[j, c] = outs[j][3]
                cp_ref[rows, j * 3 * HD:(j + 1) * 3 * HD] = outs[j][4]
            return tuple(out[2] for out in outs)

        final = lax.fori_loop(0, nc, chunk, tuple(s_ref[j] for j in range(hp)))
        for j in range(hp):
            s_ref[j] = final[j]

    wide = hp * HD
    grp_specs = lambda off: _seq_specs(sb, wide, lambda g, i: i, lambda g, i: off // hp + g)
    (qsp, qhalo), (ksp, khalo), (vsp, vhalo) = grp_specs(2 * HEADS), grp_specs(3 * HEADS), grp_specs(4 * HEADS)
    zsp, _ = grp_specs(5 * HEADS)
    basp = pl.BlockSpec((sb, HD), lambda g, i: (i, BLK_BA))
    cws = lambda off: pl.BlockSpec((4, wide), lambda g, i: (0, off // hp + g))
    osp = lambda off: pl.BlockSpec((sb, wide), lambda g, i: (i, off // hp + g))
    return pl.pallas_call(
        body, name=name, grid=(HEADS // hp, nsb),
        in_specs=[qsp, qhalo, ksp, khalo, vsp, vhalo, zsp, basp, cws(0), cws(HEADS), cws(2 * HEADS),
                  pl.BlockSpec((8, HD), lambda g, i: (0, 0)), pl.BlockSpec(memory_space=pl.ANY)],
        out_specs=[osp(HEADS), osp(0), pl.BlockSpec((hp, nc, HD, HD), lambda g, i: (g, i, 0, 0)),
                   pl.BlockSpec((hp, nc, CHUNK, CHUNK), lambda g, i: (g, i, 0, 0)),
                   pl.BlockSpec((sb, 3 * wide), lambda g, i: (i, g))],
        out_shape=[jax.ShapeDtypeStruct((s, D_MIX), bf16), jax.ShapeDtypeStruct((s, D_MODEL), f32),
                   jax.ShapeDtypeStruct((HEADS, s // CHUNK, HD, HD), f32),
                   jax.ShapeDtypeStruct((HEADS, s // CHUNK, CHUNK, CHUNK), f32),
                   jax.ShapeDtypeStruct((s, 3 * D_MODEL), f32)],
        scratch_shapes=[pltpu.VMEM((hp, HD, HD), f32)],
        input_output_aliases={12: 0},
        compiler_params=_params(("parallel", "arbitrary")),
    )(proj, proj, proj, proj, proj, proj, proj, proj, cw, cw, cw, dnv, y)


def _dn_bwd(dy, proj, o, states, tinvs, cpres, cw, dnv, name):
    s = proj.shape[0]
    sb = min(DN_SEQ_BLOCK, s)
    nsb = s // sb
    nc = sb // CHUNK
    hp = DN_HP
    ngrp = HEADS // hp

    def body(dy_ref, q_ref, k_ref, v_ref, z_ref, ba_ref, o_ref, st_ref, ti_ref, cp_ref,
             cwq_ref, cwk_ref, cwv_ref, dnv_ref,
             dq_ref, dk_ref, dv_ref, dz_ref, dba_ref, dcw_ref, dsc_ref,
             ds_ref, dch_ref, cwacc_ref, sacc_ref, nacc_ref):
        step = pl.program_id(0)
        grp = pl.program_id(1)
        sblk = nsb - 1 - step
        h0 = grp * hp
        masks = _dn_masks()
        lane, ri, ci, eye, ltri = masks
        utri = (ri <= ci).astype(f32)
        cw3s = [_dn_cw3(cwq_ref, cwk_ref, cwv_ref, j) for j in range(hp)]
        dnv_v = dnv_ref[...]
        dnw = dnv_v[2:3]
        row64 = lax.broadcasted_iota(jnp.int32, (CHUNK, HD), 0)

        @pl.when(step == 0)
        def _():
            for j in range(hp):
                ds_ref[h0 + j] = jnp.zeros((HD, HD), f32)
                dch_ref[h0 + j] = jnp.zeros((8, 3 * HD), f32)
                cwacc_ref[h0 + j] = jnp.zeros((8, 3 * HD), f32)
                sacc_ref[h0 + j] = jnp.zeros((8, HD), f32)

        @pl.when((step == 0) & (grp == 0))
        def _():
            nacc_ref[...] = jnp.zeros((8, HD), f32)

        def one_head(j, x3, bav, z, st, ti, cp, ov, dyv, carry):
            dst, dch, cwacc, sa0, sa1 = carry
            hd = h0 + j
            cw3 = cw3s[j]
            f = yield from _dn_chunk(None, None, bav, cw3, dnv_v, hd, masks, cp, ti)
            q, k, v, beta, eg, ek, kbf = f["q"], f["k"], f["v"], f["beta"], f["eg"], f["ek"], f["kbf"]
            sbf = _b(st)
            dstb = _b(dst)
            qe = q * eg
            keg = k * eg
            kd = k * ek
            kegb, qeb, kdb = _b(keg), _b(qe), _b(kd)
            e = v - _dot(kegb, sbf)
            yield
            rhs = beta * e
            vn = rhs + _sdot(f["tinv"], rhs)
            yield
            vnb = _b(vn)
            pb = _b(f["pmat"])
            sgz = _sig(z)
            sz = z * sgz
            nrm = lax.rsqrt(_rowmean(ov * ov) + EPS)
            on = ov * nrm
            ddnw = _sum0(dyv * on * sz)
            dpz = dyv * on * dnw * _dsilu(z, sgz)
            don = dyv * dnw * sz
            do = nrm * (don - on * _rowmean(don * on))
            dob = _b(do)
            dvn = _dot_tn(pb, dob) + _dot(kdb, dstb)
            dpm = jnp.where(ri >= ci, _dot_nt(dob, vnb), 0.0)
            dqe = _dot_nt(dob, sbf)
            dkd = _dot_nt(vnb, dstb)
            qtdo = _dot_tn(qeb, dob)
            yield
            drhs = dvn + _sdot_tn(f["tinv"], dvn)
            dqk = _b(dpm * f["dm"])
            dq = _dot(dqk, kbf) + dqe * eg
            dk_p = _dot_tn(dqk, _b(q))
            yield
            dam = jnp.where(ri > ci, -_sdot_nt(drhs, vn), 0.0)
            de = drhs * beta
            deb = _b(de)
            dbeta = _sum1(drhs * e)
            dkeg = -_dot_nt(deb, sbf)
            dst_new = qtdo + dst * jnp.exp(f["glast"]) - _dot_tn(kegb, deb)
            yield
            dkk = _b(dam * f["ds"])
            dkb = _dot(dkk, kbf)
            dk = dk_p + _dot_tn(dkk, _b(f["kb"])) + dkb * beta + dkeg * eg + dkd * ek
            yield
            dbeta = dbeta + _sum1(dkb * k)
            mm = dpm * f["pmat"] + dam * f["amat"]
            dglast = _sum1(_sum0(dst * st)) * jnp.exp(f["glast"]) + _sum1(_sum0(dkd * kd))
            dgc = (_sum1(mm) - _sum1(eye * _sum0(mm)) + _sum1(dqe * qe) + _sum1(dkeg * keg) - _sum1(dkd * kd))
            dgc = jnp.broadcast_to(dgc, (CHUNK, HD)) + jnp.where(row64 == CHUNK - 1, dglast, 0.0)
            dg = _tri_dot(utri, dgc)
            yield
            dbin = jnp.broadcast_to(dbeta, (CHUNK, HD)) * beta * (1.0 - beta)
            dain = dg * (-f["ea"]) * _sig(f["a_in"] + f["dt"])
            dba = jnp.where(lane == hd, dbin, 0.0) + jnp.where(lane == HEADS + hd, dain, 0.0)
            sa0 = sa0 + _sum0(dg * f["g"])
            sa1 = sa1 + _sum0(dain)
            dq_raw = (QSCALE * f["rq"]) * (dq - f["qn"] * _sum1(f["qn"] * dq))
            dk_raw = f["rk"] * (dk - k * _sum1(k * dk))
            dact = jnp.concatenate([dq_raw, dk_raw, de], axis=1)
            dc = dact * _dsilu(f["cpre"], f["sg"])
            dqkv, dcw_part = _conv_back(dc, dch, cw3, x3)
            cwacc = cwacc + dcw_part
            return dpz, dqkv, dba, ddnw, (dst_new, dc[0:8], cwacc, sa0, sa1)

        def chunk(cidx, carry):
            heads, nacc = carry
            c = nc - 1 - cidx
            t0 = pl.multiple_of(c * CHUNK, CHUNK)
            rows = pl.ds(t0, CHUNK)
            bav = ba_ref[rows, :]
            ins = []
            for j in range(hp):
                hl = slice(j * HD, (j + 1) * HD)
                x3 = jnp.concatenate([q_ref[rows, hl], k_ref[rows, hl], v_ref[rows, hl]], axis=1)
                ins.append((x3, bav, z_ref[rows, hl], st_ref[j, c], ti_ref[j, c],
                            cp_ref[rows, j * 3 * HD:(j + 1) * 3 * HD], o_ref[rows, hl], dy_ref[rows, hl]))
            outs = _round_robin([one_head(j, *ins[j], heads[j]) for j in range(hp)])
            dba = outs[0][2]
            for j in range(hp):
                hl = slice(j * HD, (j + 1) * HD)
                dpz, dqkv, dba_j, ddnw, _ = outs[j]
                dq_ref[rows, hl] = _b(dqkv[:, 0:HD])
                dk_ref[rows, hl] = _b(dqkv[:, HD:2 * HD])
                dv_ref[rows, hl] = _b(dqkv[:, 2 * HD:3 * HD])
                dz_ref[rows, hl] = _b(dpz)
                nacc = nacc + ddnw
                if j > 0:
                    dba = dba + dba_j

            @pl.when(grp == 0)
            def _():
                dba_ref[rows, :] = dba

            @pl.when(grp > 0)
            def _():
                dba_ref[rows, :] += dba

            return tuple(out[4] for out in outs), nacc

        init = tuple((ds_ref[h0 + j], dch_ref[h0 + j], cwacc_ref[h0 + j][0:4], sacc_ref[h0 + j][0:1],
                      sacc_ref[h0 + j][1:2]) for j in range(hp))
        heads, nacc = lax.fori_loop(0, nc, chunk, (init, nacc_ref[0:1, :]))
        nacc_ref[0:1, :] = nacc
        zpad = jnp.zeros((4, 3 * HD), f32)
        zrow = jnp.zeros((6, HD), f32)
        for j in range(hp):
            dst, dch, cwacc, sa0, sa1 = heads[j]
            ds_ref[h0 + j] = dst
            dch_ref[h0 + j] = dch
            cwacc_ref[h0 + j] = jnp.concatenate([cwacc, zpad], axis=0)
            sacc_ref[h0 + j] = jnp.concatenate([sa0, sa1, zrow], axis=0)

        @pl.when(step == nsb - 1)
        def _():
            lane8 = lax.broadcasted_iota(jnp.int32, (8, HD), 1)
            row8 = lax.broadcasted_iota(jnp.int32, (8, HD), 0)
            mine = jnp.zeros((8, HD), f32)
            for j in range(hp):
                dcw_ref[h0 + j] = heads[j][2]
                sa = jnp.concatenate([heads[j][3], heads[j][4], zrow], axis=0)
                mine = mine + jnp.where((lane8 == h0 + j) & (row8 < 2), sa, 0.0)

            @pl.when(grp == 0)
            def _():
                dsc_ref[...] = mine

            @pl.when(grp > 0)
            def _():
                dsc_ref[...] += mine

            @pl.when(grp == ngrp - 1)
            def _():
                dsc_ref[2:3, :] = nacc

    wide = hp * HD
    rblk = lambda i, g: nsb - 1 - i
    grp_specs = lambda off: _seq_specs(sb, wide, rblk, lambda i, g: off // hp + g)
    (qsp, qhalo), (ksp, khalo), (vsp, vhalo) = grp_specs(2 * HEADS), grp_specs(3 * HEADS), grp_specs(4 * HEADS)
    zsp, _ = grp_specs(5 * HEADS)
    hsp = lambda off: pl.BlockSpec((sb, wide), lambda i, g: (rblk(i, g), off // hp + g))
    basp = lambda col: pl.BlockSpec((sb, HD), lambda i, g: (rblk(i, g), col))
    cws = lambda off: pl.BlockSpec((4, wide), lambda i, g: (0, off // hp + g))
    whole = lambda shape: pl.BlockSpec(shape, lambda i, g: (0,) * len(shape))
    return pl.pallas_call(
        body, name=name, grid=(nsb, ngrp),
        in_specs=[hsp(HEADS), qsp, ksp, vsp, zsp, basp(BLK_BA), hsp(0),
                  pl.BlockSpec((hp, nc, HD, HD), lambda i, g: (g, rblk(i, g), 0, 0)),
                  pl.BlockSpec((hp, nc, CHUNK, CHUNK), lambda i, g: (g, rblk(i, g), 0, 0)),
                  pl.BlockSpec((sb, 3 * wide), lambda i, g: (rblk(i, g), g)),
                  cws(0), cws(HEADS), cws(2 * HEADS), whole((8, HD))],
        out_specs=[hsp(0), hsp(0), hsp(0), hsp(0), basp(0), whole((HEADS, 4, 3 * HD)), whole((8, HD))],
        out_shape=[jax.ShapeDtypeStruct((s, D_MODEL), bf16)] * 4
        + [jax.ShapeDtypeStruct((s, HD), f32), jax.ShapeDtypeStruct((HEADS, 4, 3 * HD), f32),
           jax.ShapeDtypeStruct((8, HD), f32)],
        scratch_shapes=[pltpu.VMEM((HEADS, HD, HD), f32), pltpu.VMEM((HEADS, 8, 3 * HD), f32),
                        pltpu.VMEM((HEADS, 8, 3 * HD), f32), pltpu.VMEM((HEADS, 8, HD), f32), pltpu.VMEM((8, HD), f32)],
        compiler_params=_params(("arbitrary", "arbitrary")),
    )(dy, proj, proj, proj, proj, proj, o, states, tinvs, cpres, cw, cw, cw, dnv)


ANY = pl.BlockSpec(memory_space=pl.ANY)
CHIP_MASKS = ((1, 0, 0), (0, 1, 0), (1, 1, 0))
ALL_MASKS = tuple((m >> 2 & 1, m >> 1 & 1, m & 1) for m in range(1, N_DEV))


def _me():
    return lax.axis_index("x"), lax.axis_index("y"), lax.axis_index("c")


def _flip(me, mask):
    return tuple((1 - v) if m else v for v, m in zip(me, mask))


def _dev_index(dev):
    return 4 * dev[0] + 2 * dev[1] + dev[2]


def _chip(dev):
    return 2 * dev[0] + dev[1]


def _comm_call(name, body, arrays, out_shapes, nsem, nloc=0):
    scratch = [pltpu.SemaphoreType.DMA((nsem,)), pltpu.SemaphoreType.DMA((nsem,))]
    if nloc:
        scratch.append(pltpu.SemaphoreType.DMA((nloc,)))
    return pl.pallas_call(
        body, name=name, in_specs=[ANY] * len(arrays), out_specs=[ANY] * len(out_shapes), out_shape=out_shapes,
        scratch_shapes=scratch, compiler_params=pltpu.CompilerParams(has_side_effects=True),
    )(*arrays)


def _put_own(gathered, own, slot):
    return lax.dynamic_update_index_in_dim(gathered, own, slot, 0)


def _gather_halves(arrays, whole_arrays, name):
    na, nw = len(arrays), len(whole_arrays)
    nm = len(CHIP_MASKS)

    def body(*refs):
        srcs, dsts = refs[:na + nw], refs[na + nw:2 * (na + nw)]
        send_sems, recv_sems = refs[2 * (na + nw):]
        me = _me()
        sib = _flip(me, (0, 0, 1))

        def half(a, ref, core):
            rows = arrays[a].shape[0] // 2
            return ref.at[pl.ds(pl.multiple_of(core * rows, rows), rows)]

        def over_ici(a, mi, sender, to):
            if a < na:
                src, dst = half(a, srcs[a], sender[2]), half(a, dsts[a].at[_chip(sender)], sender[2])
            else:
                src, dst = srcs[a], dsts[a].at[_chip(sender)]
            return pltpu.make_async_remote_copy(src_ref=src, dst_ref=dst, send_sem=send_sems.at[a * nm + mi],
                                                recv_sem=recv_sems.at[a * nm + mi], device_id=to, device_id_type=MESH)

        def over_d2d(a, mi, origin, sender, to):
            k = (na + nw) * nm + a * nm + mi
            blk = half(a, dsts[a].at[_chip(origin)], sender[2])
            return pltpu.make_async_remote_copy(src_ref=blk, dst_ref=blk, send_sem=send_sems.at[k],
                                                recv_sem=recv_sems.at[k], device_id=to, device_id_type=MESH)

        sends = []
        for a in range(na + nw):
            for mi, mask in enumerate(CHIP_MASKS):
                cp = over_ici(a, mi, me, _flip(me, mask))
                cp.start()
                sends.append(cp)
        for a in range(na + nw):
            for mi, mask in enumerate(CHIP_MASKS):
                peer = _flip(me, mask)
                over_ici(a, mi, peer, me).wait_recv()
                if a < na:
                    cp = over_d2d(a, mi, peer, me, sib)
                    cp.start()
                    sends.append(cp)
        for a in range(na):
            for mi, mask in enumerate(CHIP_MASKS):
                over_d2d(a, mi, _flip(sib, mask), sib, me).wait_recv()
        for cp in sends:
            cp.wait_send()

    every = list(arrays) + list(whole_arrays)
    got = _comm_call(name, body, every, [jax.ShapeDtypeStruct((4,) + t.shape, t.dtype) for t in every],
                     (2 * na + nw) * nm)
    chip = 2 * lax.axis_index("x") + lax.axis_index("y")
    return [_put_own(g, t, chip) for g, t in zip(got, every)]


HBM = pl.BlockSpec(memory_space=pltpu.HBM)
SEM = pl.BlockSpec(memory_space=pltpu.SEMAPHORE)
DATAFLOW = pltpu.SideEffectType.DATAFLOW_SIDE_EFFECTING


def _split_start(name, srcs, land_shapes, nsem, issue, after=None):
    ns, nl = len(srcs), len(land_shapes)
    n_in = ns + nl + (after is not None)

    def body(*refs):
        issue(refs[:ns], refs[ns:ns + nl], refs[n_in + ns + nl], refs[n_in + ns + nl + 1])
        token = refs[-1]
        token[...] = jnp.zeros_like(token)

    lands = [lax.empty(t.shape, t.dtype) for t in land_shapes]
    thru = [pltpu.HBM(t.shape, t.dtype) for t in list(srcs) + list(land_shapes)]
    hbm = lambda t: pltpu.with_memory_space_constraint(t, pltpu.HBM)
    return pl.pallas_call(
        body, name=name, in_specs=[HBM] * (ns + nl) + [ANY] * (after is not None),
        out_shape=(*thru, pltpu.SemaphoreType.DMA((nsem,)), pltpu.SemaphoreType.DMA((nsem,)),
                   jax.ShapeDtypeStruct((8, HD), f32)),
        out_specs=(*[HBM] * (ns + nl), SEM, SEM, pl.BlockSpec(memory_space=pltpu.VMEM)),
        input_output_aliases={i: i for i in range(ns + nl)},
        compiler_params=pltpu.CompilerParams(has_side_effects=DATAFLOW),
    )(*[hbm(t) for t in srcs], *[hbm(t) for t in lands], *([after] if after is not None else []))


def _split_wait(name, started, ns, after, finish):
    thru, send_sems, recv_sems = started[:-3], started[-3], started[-2]
    nt = len(thru)

    def body(*refs):
        finish(refs[:ns], refs[ns:nt], refs[nt], refs[nt + 1])

    outs = pl.pallas_call(
        body, name=name, in_specs=[HBM] * nt + [SEM, SEM, ANY],
        out_shape=tuple(pltpu.HBM(t.shape, t.dtype) for t in thru), out_specs=tuple([HBM] * nt),
        input_output_aliases={i: i for i in range(nt)},
        compiler_params=pltpu.CompilerParams(has_side_effects=DATAFLOW),
    )(*thru, send_sems, recv_sems, after)
    return list(outs[ns:])


def _gather_start(arrays, name, after=None):
    nm = len(CHIP_MASKS)

    def issue(srcs, lands, send_sems, recv_sems):
        me = _me()
        for a in range(len(arrays)):
            for mi, mask in enumerate(CHIP_MASKS):
                pltpu.make_async_remote_copy(
                    src_ref=srcs[a], dst_ref=lands[a].at[_chip(me)], send_sem=send_sems.at[a * nm + mi],
                    recv_sem=recv_sems.at[a * nm + mi], device_id=_flip(me, mask), device_id_type=MESH).start()

    shapes = [jax.ShapeDtypeStruct((4,) + t.shape, t.dtype) for t in arrays]
    return _split_start(name, arrays, shapes, len(arrays) * nm, issue, after)


def _gather_finish(started, arrays, after, name):
    nm = len(CHIP_MASKS)

    def finish(srcs, lands, send_sems, recv_sems):
        me = _me()
        for a in range(len(arrays)):
            for mi, mask in enumerate(CHIP_MASKS):
                peer = _flip(me, mask)
                cp = pltpu.make_async_remote_copy(
                    src_ref=srcs[a], dst_ref=lands[a].at[_chip(peer)], send_sem=send_sems.at[a * nm + mi],
                    recv_sem=recv_sems.at[a * nm + mi], device_id=peer, device_id_type=MESH)
                cp.wait_send()
                cp.wait_recv()

    got = _split_wait(name, started, len(arrays), after, finish)
    chip = 2 * lax.axis_index("x") + lax.axis_index("y")
    return [_put_own(g, t, chip) for g, t in zip(got, arrays)]


HALF_IN, HALF_OUT = D_MODEL // 2, D_MIX // 8


def _scatter_piece(kind, ref, d):
    j, r = d >> 1, d & 1
    if kind == "win":
        return ref.at[pl.ds(HALF_IN * r, HALF_IN), pl.ds(WIN_STRIDE * j, WIN_W)]
    if kind == "wout":
        return ref.at[pl.ds(2 * HALF_OUT * j + HALF_OUT * r, HALF_OUT)]
    return ref.at[d]


def _scatter_start(arrays, kinds, name):
    na = len(arrays)

    def issue(srcs, lands, send_sems, recv_sems):
        me = _me()
        my = _dev_index(me)
        for d in range(N_DEV):
            dev = (d >> 2, (d >> 1) & 1, d & 1)
            for a in range(na):
                @pl.when(my != d)
                def _(a=a, d=d, dev=dev):
                    pltpu.make_async_remote_copy(
                        src_ref=_scatter_piece(kinds[a], srcs[a], d), dst_ref=lands[a].at[my],
                        send_sem=send_sems.at[a * N_DEV + d], recv_sem=recv_sems.at[a * N_DEV + my],
                        device_id=dev, device_id_type=MESH).start()

    shape_of = {"win": (N_DEV, HALF_IN, WIN_W), "wout": (N_DEV, HALF_OUT, D_MODEL)}
    shapes = [jax.ShapeDtypeStruct(shape_of.get(k, t.shape), t.dtype) for t, k in zip(arrays, kinds)]
    return _split_start(name, arrays, shapes, na * N_DEV, issue)


def _scatter_finish(started, arrays, kinds, after, name):
    na = len(arrays)

    def finish(srcs, lands, send_sems, recv_sems):
        me = _me()
        my = _dev_index(me)
        for s in range(N_DEV):
            for a in range(na):
                @pl.when(my != s)
                def _(a=a, s=s):
                    cp = pltpu.make_async_remote_copy(
                        src_ref=_scatter_piece(kinds[a], srcs[a], s), dst_ref=lands[a].at[s],
                        send_sem=send_sems.at[a * N_DEV + s], recv_sem=recv_sems.at[a * N_DEV + s],
                        device_id=me, device_id_type=MESH)
                    cp.wait_recv()
                    cp.wait_send()

    got = _split_wait(name, started, na, after, finish)
    x, y, c = _me()
    chip, my = 2 * x + y, 4 * x + 2 * y + c
    own = {"win": lambda t: lax.dynamic_slice(t, (HALF_IN * c, WIN_STRIDE * chip), (HALF_IN, WIN_W)),
           "wout": lambda t: lax.dynamic_slice(t, (2 * HALF_OUT * chip + HALF_OUT * c, 0), (HALF_OUT, D_MODEL)),
           "small": lambda t: lax.dynamic_index_in_dim(t, my, 0, keepdims=False)}
    return [_put_own(g, own[k](t), my) for g, t, k in zip(got, arrays, kinds)]


def _share_reduced(pair_arrays, small, name):
    arrays = list(pair_arrays) + ([small] if small is not None else [])
    na, npair = len(arrays), len(pair_arrays)
    nm = len(ALL_MASKS)

    def body(*refs):
        srcs, dsts = refs[:na], refs[na:2 * na]
        send_sems, recv_sems = refs[2 * na:]
        me = _me()
        sib = _flip(me, (0, 0, 1))

        def copy(a, k, sender, to):
            slot = sender[2] if a < npair else _dev_index(sender)
            return pltpu.make_async_remote_copy(
                src_ref=srcs[a], dst_ref=dsts[a].at[slot], send_sem=send_sems.at[k], recv_sem=recv_sems.at[k],
                device_id=to, device_id_type=MESH)

        sends = [copy(a, a, me, sib) for a in range(npair)]
        if small is not None:
            sends += [copy(npair, npair + mi, me, _flip(me, mask)) for mi, mask in enumerate(ALL_MASKS)]
        for cp in sends:
            cp.start()
        for a in range(npair):
            copy(a, a, sib, me).wait_recv()
        if small is not None:
            for mi, mask in enumerate(ALL_MASKS):
                copy(npair, npair + mi, _flip(me, mask), me).wait_recv()
        for cp in sends:
            cp.wait_send()

    shapes = [jax.ShapeDtypeStruct((2,) + t.shape, t.dtype) for t in pair_arrays]
    if small is not None:
        shapes.append(jax.ShapeDtypeStruct((N_DEV,) + small.shape, small.dtype))
    got = _comm_call(name, body, arrays, shapes, npair + (nm if small is not None else 0))
    x, y, c = _me()
    slots = [c] * npair + [4 * x + 2 * y + c]
    return [_put_own(g, t, slot) for g, t, slot in zip(got, arrays, slots)]


def _sum_parts(parts, name):
    _, r, c = parts.shape
    tr = r
    while tr * c * 4 * N_DEV > (8 << 20) and tr % 32 == 0:
        tr //= 2

    def body(p_ref, o_ref):
        total = p_ref[0].astype(f32)
        for d in range(1, N_DEV):
            total = total + p_ref[d].astype(f32)
        o_ref[...] = total

    return pl.pallas_call(
        body, name=name, grid=(r // tr,), in_specs=[pl.BlockSpec((N_DEV, tr, c), lambda i: (0, i, 0))],
        out_specs=pl.BlockSpec((tr, c), lambda i: (i, 0)), out_shape=jax.ShapeDtypeStruct((r, c), f32),
        compiler_params=_params(("parallel",)),
    )(parts)


def _adamw(w, g, m, v, name):
    shape = w.shape
    cols = shape[-1]
    slabs = w.ndim == 3 and shape[1] < 8
    rows = w.size // cols
    tr = rows
    while tr * cols * 4 > (1 << 20) and tr % 16 == 0:
        tr //= 2
    c1 = 1.0 / (1.0 - ADAM_B1 ** ADAM_STEP)
    c2 = 1.0 / (1.0 - ADAM_B2 ** ADAM_STEP)

    def body(w_ref, g_ref, m_ref, v_ref, d_ref, nm_ref, nv_ref):
        gv = g_ref[...]
        mv = ADAM_B1 * m_ref[...] + (1.0 - ADAM_B1) * gv
        vv = ADAM_B2 * v_ref[...] + (1.0 - ADAM_B2) * (gv * gv)
        nm_ref[...] = mv
        nv_ref[...] = vv
        d_ref[...] = -ADAM_LR * ((mv * c1) / (jnp.sqrt(vv * c2) + ADAM_EPS) + ADAM_WD * w_ref[...])

    if slabs:
        tl = max(d for d in range(1, shape[0] + 1) if shape[0] % d == 0 and d * shape[1] * cols * 4 <= (3 << 19))
        spec = pl.BlockSpec((tl, shape[1], cols), lambda i: (i, 0, 0))
        return tuple(pl.pallas_call(
            body, name=name, grid=(shape[0] // tl,), in_specs=[spec] * 4, out_specs=[spec] * 3,
            out_shape=[jax.ShapeDtypeStruct(shape, f32)] * 3, compiler_params=_params(("parallel",)),
        )(w, g, m, v))
    spec = pl.BlockSpec((tr, cols), lambda i: (i, 0))
    outs = pl.pallas_call(
        body, name=name, grid=(rows // tr,), in_specs=[spec] * 4, out_specs=[spec] * 3,
        out_shape=[jax.ShapeDtypeStruct((rows, cols), f32)] * 3, compiler_params=_params(("parallel",)),
    )(*[t.reshape(rows, cols) for t in (w, g, m, v)])
    return tuple(t.reshape(shape) for t in outs)


def _pad_lanes(t, n=HD):
    return jnp.pad(t, [(0, 0)] * (t.ndim - 1) + [(0, n - t.shape[-1])])


def _rows128(t, rows=None):
    t = t.reshape(-1, HD)
    rows = rows or -(-t.shape[0] // 8) * 8
    return jnp.pad(t, ((0, rows - t.shape[0]), (0, 0)))


def kernel(x, norm_w, w_in, lru_conv_w, lru_conv_b, lru_wa, lru_ba, lru_wx, lru_bx, lru_lambda, lru_norm_w, dn_conv_w, dn_A_log, dn_dt_bias, dn_norm_w, w_out, final_norm_w, loss_target, m_norm_w, m_w_in, m_lru_conv_w, m_lru_conv_b, m_lru_wa, m_lru_ba, m_lru_wx, m_lru_bx, m_lru_lambda, m_lru_norm_w, m_dn_conv_w, m_dn_A_log, m_dn_dt_bias, m_dn_norm_w, m_w_out, m_final_norm_w, v_norm_w, v_w_in, v_lru_conv_w, v_lru_conv_b, v_lru_wa, v_lru_ba, v_lru_wx, v_lru_bx, v_lru_lambda, v_lru_norm_w, v_dn_conv_w, v_dn_A_log, v_dn_dt_bias, v_dn_norm_w, v_w_out, v_final_norm_w):
    depth = norm_w.shape[0]
    xs = x[0]
    target = loss_target[0]
    chip = 2 * lax.axis_index("x") + lax.axis_index("y")

    win_b, wout_b = w_in.astype(bf16), w_out.astype(bf16)
    got_in0 = _gather_halves([win_b[0]], [], "gather_weights0")[0]
    rest0 = [wout_b[0], lru_conv_w, dn_conv_w]
    rest0_started = _gather_start(rest0, "gather_rest0_start")
    gathered, later = {}, {}
    pad_cols = jnp.zeros((D_MODEL, D_INP - D_IN), bf16)

    def weights_of(l, after):
        if l == 0:
            got, token = got_in0, rest0_started[-1]
        else:
            gathered[l] = _gather_finish(later[l], [win_b[l], wout_b[l]], after, f"gather_weights{l}_wait")
            got, token = gathered[l][0], None
        return jnp.concatenate([got[j] for j in range(4)] + [pad_cols], axis=1), token

    def rest_of(l, after):
        token = None
        if "rest0" not in gathered:
            gathered["rest0"] = _gather_finish(rest0_started, rest0, after, "gather_rest0_wait")
            for k in range(1, depth):
                later[k] = _gather_start([win_b[k], wout_b[k]], f"gather_weights{k}_start", after=gathered["rest0"][0])
                token = later[k][-1] if token is None else token + later[k][-1]
        g_out, g_lcw, g_dcw = gathered["rest0"]
        if l > 0:
            g_out = gathered[l][1]
        lcw_full = g_lcw.transpose(1, 2, 0, 3).reshape(depth, 4, D_MODEL)
        dcw_full = g_dcw.transpose(1, 2, 0, 3).reshape(depth, 4, 3 * D_MODEL)
        return g_out.reshape(D_MIX, D_MODEL), _behind(lcw_full[l], token), dcw_full[l]

    in_flight = {}

    def on_grad(kind, l, g):
        if l == depth - 1 and depth > 1:
            if kind == "wout":
                in_flight["held"] = g
                return None
            srcs, kinds, key = [g, in_flight.pop("held")], ["win", "wout"], ("both", l)
        else:
            srcs, kinds, key = [g], [kind], (kind, l)
        started = _scatter_start(srcs, kinds, f"scatter_{key[0]}{key[1]}_start")
        in_flight[key] = (started, srcs, kinds)
        return started[-1]

    vecs, dnvs = [], []
    zrow = jnp.zeros((D_MODEL,), f32)
    for l in range(depth):
        vecs.append(jnp.stack([lru_conv_b[l], lru_ba[l], lru_bx[l], lru_lambda[l], lru_norm_w[l], zrow, zrow, zrow]))
        dnvs.append(jnp.concatenate([_pad_lanes(dn_A_log[l][None]), _pad_lanes(dn_dt_bias[l][None]),
                                     dn_norm_w[l][None], jnp.zeros((5, HD), f32)], axis=0))

    loss_part, grad_x, d_fnw, g_nw, g_vec, g_wa, g_wx, g_lcw_, g_dcw_, g_dsc = _local_step(
        xs, target, norm_w, final_norm_w, weights_of, rest_of, on_grad, vecs, dnvs, lru_wa, lru_wx)
    loss = lax.psum(loss_part[0, 0], ("x", "y", "c"))
    grads = _reduce_grads(chip, grad_x, in_flight, d_fnw, g_nw, g_vec, g_wa, g_wx, g_lcw_, g_dcw_, g_dsc)

    names = ["norm_w", "w_in", "lru_conv_w", "lru_conv_b", "lru_wa", "lru_ba", "lru_wx", "lru_bx", "lru_lambda",
             "lru_norm_w", "dn_conv_w", "dn_A_log", "dn_dt_bias", "dn_norm_w", "w_out", "final_norm_w"]
    ws = dict(zip(names, [norm_w, w_in, lru_conv_w, lru_conv_b, lru_wa, lru_ba, lru_wx, lru_bx, lru_lambda, lru_norm_w,
                          dn_conv_w, dn_A_log, dn_dt_bias, dn_norm_w, w_out, final_norm_w]))
    ms = dict(zip(names, [m_norm_w, m_w_in, m_lru_conv_w, m_lru_conv_b, m_lru_wa, m_lru_ba, m_lru_wx, m_lru_bx,
                          m_lru_lambda, m_lru_norm_w, m_dn_conv_w, m_dn_A_log, m_dn_dt_bias, m_dn_norm_w, m_w_out,
                          m_final_norm_w]))
    vs = dict(zip(names, [v_norm_w, v_w_in, v_lru_conv_w, v_lru_conv_b, v_lru_wa, v_lru_ba, v_lru_wx, v_lru_bx,
                          v_lru_lambda, v_lru_norm_w, v_dn_conv_w, v_dn_A_log, v_dn_dt_bias, v_dn_norm_w, v_w_out,
                          v_final_norm_w]))
    to_cols = lambda t: jnp.transpose(t, (2, 0, 1))
    from_cols = lambda t: jnp.transpose(t, (1, 2, 0))
    g_cols = to_cols(grads["w_in"])
    grads["w_in"] = from_cols(g_cols)
    deltas, new_m, new_v = [], [], []
    for n in names:
        if n == "w_in":
            d, nm_, nv_ = (from_cols(t) for t in _adamw(to_cols(ws[n]), g_cols, to_cols(ms[n]), to_cols(vs[n]),
                                                        f"adamw_{n}"))
        else:
            d, nm_, nv_ = _adamw(ws[n], grads[n], ms[n], vs[n], f"adamw_{n}")
        deltas.append(d)
        new_m.append(nm_)
        new_v.append(nv_)
    return (loss, grad_x[None], *[grads[n] for n in names], *deltas, *new_m, *new_v)


def _behind(t, token):
    return t if token is None else t + token[0:1, 0:1]


def _local_step(xs, target, norm_w, final_norm_w, weights_of, rest_of, on_grad, vecs, dnvs, lru_wa, lru_wx):
    depth = norm_w.shape[0]
    saved = []
    h = xs
    for l in range(depth):
        wp_l, token = weights_of(l, h)
        hn = _rms_fwd(h, _behind(norm_w[l][None], token), f"rms_fwd{l}")
        proj = _matmul(hn, wp_l, tm=2048, tn=896, tk=D_MODEL, name=f"in_proj{l}")
        wo_l, lcw_l, dcw_l = rest_of(l, proj)
        y, hl = _lru_fwd(proj, lcw_l, vecs[l], lru_wa[l], lru_wx[l], f"lru_fwd{l}")
        y, o, states, tinvs, cpres = _dn_fwd(proj, dcw_l, dnvs[l], y, f"dn_fwd{l}")
        out = _matmul(y, wo_l, tm=1024, tn=D_MODEL, tk=D_MIX, res=h, name=f"out_proj{l}")
        saved.append((h, hn, proj, hl, o, states, tinvs, cpres, y, wp_l, wo_l, lcw_l, dcw_l))
        h = out

    dh, d_fnw, loss_part = _final_loss(h, final_norm_w[None], target, "final_loss")

    g_nw, g_vec, g_wa, g_wx, g_lcw_, g_dcw_, g_dsc = ([None] * depth for _ in range(7))
    for l in reversed(range(depth)):
        hin, hn, proj, hl, o, states, tinvs, cpres, y, wp_l, wo_l, lcw_l, dcw_l = saved[l]
        dy = _matmul(dh, wo_l, tb=True, tm=2048, tn=1024, tk=D_MODEL, name=f"d_y{l}")
        g_wout = _matmul(y, dh, ta=True, tm=1024, tn=D_MODEL, tk=512, out_dtype=bf16, name=f"d_wout{l}")
        token = on_grad("wout", l, g_wout)
        dlx, dlz, g_lcw_[l], g_vec[l], g_wa[l], g_wx[l] = _lru_bwd(dy, proj, hl, lcw_l, _behind(vecs[l], token),
                                                                 lru_wa[l], lru_wx[l], f"lru_bwd{l}")
        dq, dk, dv, dz, dba, dcw8, g_dsc[l] = _dn_bwd(dy, proj, o, states, tinvs, cpres, dcw_l, dnvs[l], f"dn_bwd{l}")
        g_dcw_[l] = dcw8.reshape(HEADS, 4, 3, HD).transpose(1, 2, 0, 3).reshape(4, 3 * D_MODEL)
        dxs = [dlx, dlz, dq, dk, dv, dz]
        g_win = _d_win(_transpose(hn, f"hn_t{l}"), dxs, dba, f"d_win{l}")
        token = on_grad("win", l, g_win)
        dhn = _d_hn(dxs, _behind(dba, token), wp_l, f"d_hn{l}")
        dh, g_nw[l] = _rms_bwd(dhn, hin, norm_w[l][None], dh, f"rms_bwd{l}")
    return loss_part, dh, d_fnw, g_nw, g_vec, g_wa, g_wx, g_lcw_, g_dcw_, g_dsc


def _reduce_grads(chip, after, in_flight, d_fnw, g_nw, g_vec, g_wa, g_wx, g_lcw_, g_dcw_, g_dsc):
    depth = len(g_nw)
    both = lambda f: jnp.concatenate([f(l) for l in range(depth)])
    small = [
        both(lambda l: _rows128(g_nw[l])),
        both(lambda l: _rows128(g_vec[l][0])), both(lambda l: _rows128(g_vec[l][1])),
        both(lambda l: _rows128(g_vec[l][2])), both(lambda l: _rows128(g_vec[l][3])),
        both(lambda l: _rows128(g_vec[l][4])),
        both(lambda l: _rows128(g_wa[l])), both(lambda l: _rows128(g_wx[l])),
        both(lambda l: _rows128(g_dsc[l][0:1])), both(lambda l: _rows128(g_dsc[l][1:2])),
        both(lambda l: _rows128(g_dsc[l][2:3])),
        _rows128(d_fnw),
        both(lambda l: _rows128(g_lcw_[l])), both(lambda l: _rows128(g_dcw_[l])),
    ]
    counts = [t.shape[0] for t in small]
    total = sum(counts)
    per = -(-total // (8 * N_DEV)) * 8
    small = jnp.concatenate(small + [jnp.zeros((per * N_DEV - total, HD), f32)]).reshape(N_DEV, per, HD)

    small_started = _scatter_start([small], ["small"], "scatter_small_start")
    reduced = {}
    for key in sorted(in_flight, key=lambda k: -k[1]):
        started, srcs, kinds = in_flight[key]
        got = _scatter_finish(started, srcs, kinds, after, f"scatter_{key[0]}{key[1]}_wait")
        for kind, part in zip(kinds, got):
            reduced[(kind, key[1])] = _sum_parts(part, f"sum_{kind}{key[1]}")
    got = _scatter_finish(small_started, [small], ["small"], after, "scatter_small_wait")
    pairs = [None] * depth
    a_small = None
    for l in reversed(range(depth)):
        sums = [reduced[("win", l)], reduced[("wout", l)]]
        if l == 0:
            shared = _share_reduced(sums, _sum_parts(got[0], "sum_small"), f"share_grads{l}")
            a_small = shared[2].reshape(N_DEV * per, HD)
        else:
            shared = _share_reduced(sums, None, f"share_grads{l}")
        pairs[l] = shared
    grad_w_in = jnp.stack([lax.dynamic_slice_in_dim(pairs[l][0].reshape(D_MODEL, WIN_W), 4 * chip, SHARD_IN, 1)
                           for l in range(depth)])
    grad_w_out = jnp.stack([pairs[l][1].reshape(D_MIX // 4, D_MODEL) for l in range(depth)])

    offs = [0]
    for c in counts:
        offs.append(offs[-1] + c)

    def piece(k, rows_per_layer=None):
        t = a_small[offs[k]:offs[k + 1]]
        if rows_per_layer is None:
            return t
        return t.reshape(depth, -1, HD)[:, :rows_per_layer]

    vec = lambda k: piece(k).reshape(depth, D_MODEL)
    return {
        "norm_w": vec(0), "lru_conv_b": vec(1), "lru_ba": vec(2), "lru_bx": vec(3), "lru_lambda": vec(4),
        "lru_norm_w": vec(5),
        "lru_wa": piece(6).reshape(depth, HEADS, HD, HD), "lru_wx": piece(7).reshape(depth, HEADS, HD, HD),
        "dn_A_log": piece(8, 1)[:, 0, :HEADS], "dn_dt_bias": piece(9, 1)[:, 0, :HEADS], "dn_norm_w": piece(10, 1)[:, 0],
        "final_norm_w": piece(11).reshape(D_MODEL),
        "lru_conv_w": lax.dynamic_slice_in_dim(piece(12).reshape(depth, 4, D_MODEL), chip * (D_MODEL // 4), D_MODEL // 4, 2),
        "dn_conv_w": lax.dynamic_slice_in_dim(piece(13).reshape(depth, 4, 3 * D_MODEL), chip * (3 * D_MODEL // 4),
                                              3 * D_MODEL // 4, 2),
        "w_in": grad_w_in, "w_out": grad_w_out,
    }
```

```python
import jax
import jax.numpy as jnp
from jax import lax
from jax.experimental import pallas as pl
from jax.experimental.pallas import tpu as pltpu

f32 = jnp.float32
bf16 = jnp.bfloat16
HI = lax.Precision.HIGHEST
MESH = pl.DeviceIdType.MESH

D_MODEL = 1024
HEADS = 8
HD = 128
CHUNK = 64
LRU_T = 128
SEQ_BLOCK = 1024
LRU_HP = 4
DN_SEQ_BLOCK = 256
DN_HP = 8
LRU_C = 8.0
D_IN = 6160
D_INP = 6272
D_MIX = 2048
N_GROUPS = 6
BLK_BA = 48
SHARD_IN = D_IN // 4
WIN_W = 1664
WIN_STRIDE = 1536
EPS = 1e-6
QSCALE = HD ** -0.5
N_DEV = 8
VMEM_LIMIT = 56 * 1024 * 1024

ADAM_LR, ADAM_B1, ADAM_B2, ADAM_EPS, ADAM_WD, ADAM_STEP = 0.001, 0.9, 0.999, 1e-08, 0.01, 10


def _sig(x):
    return 1.0 / (1.0 + jnp.exp(-x))


def _log1p(u):
    w = 1.0 + u
    d = w - 1.0
    return jnp.where(d == 0.0, u, jnp.log(w) * (u / jnp.where(d == 0.0, 1.0, d)))


def _softplus(x):
    return jnp.maximum(x, 0.0) + _log1p(jnp.exp(-jnp.abs(x)))


def _expm1(x):
    t = x * (1.0 + x * (0.5 + x * (1.0 / 6 + x * (1.0 / 24 + x * (1.0 / 120 + x * (1.0 / 720))))))
    return jnp.where(jnp.abs(x) < 0.2, t, jnp.exp(x) - 1.0)


def _dot(a, b, prec=None):
    return jnp.dot(a, b, precision=prec, preferred_element_type=f32)


def _dot_nt(a, b, prec=None):
    return lax.dot_general(a, b, (((1,), (1,)), ((), ())), precision=prec, preferred_element_type=f32)


def _dot_tn(a, b, prec=None):
    return lax.dot_general(a, b, (((0,), (0,)), ((), ())), precision=prec, preferred_element_type=f32)


def _b(x):
    return x.astype(bf16)


def _sum0(x):
    return jnp.sum(x, axis=0, keepdims=True)


def _sum1(x):
    return jnp.sum(x, axis=1, keepdims=True)


def _rowmean(x):
    return jnp.mean(x, axis=-1, keepdims=True)


def _dsilu(z, sg):
    return sg * (1.0 + z * (1.0 - sg))


def _conv_taps(x, halo):
    xx = jnp.concatenate([halo, x], axis=0)
    return [pltpu.roll(xx, 3, axis=0)[8:], pltpu.roll(xx, 2, axis=0)[8:], pltpu.roll(xx, 1, axis=0)[8:], x]


def _conv(xs, cw):
    return cw[0:1] * xs[0] + cw[1:2] * xs[1] + cw[2:3] * xs[2] + cw[3:4] * xs[3]


def _rows_before(ref, halo_ref, lanes, t0, c, sblk):
    tprev = pl.multiple_of(jnp.maximum(t0 - 8, 0), 8)
    return jnp.where(c > 0, ref[pl.ds(tprev, 8), lanes], jnp.where(sblk > 0, halo_ref[:, lanes], 0.0))


def _conv_back(dxc, halo_after, cw, x=None):
    rows = dxc.shape[0]
    dd = jnp.concatenate([dxc, halo_after], axis=0)
    out = cw[3:4] * dxc
    dcw = [None, None, None, None if x is None else _sum0(x * dxc)]
    for s in (1, 2, 3):
        shifted = pltpu.roll(dd, rows + 8 - s, axis=0)[:rows]
        out = out + cw[3 - s:4 - s] * shifted
        if x is not None:
            dcw[3 - s] = _sum0(x * shifted)
    return out if x is None else (out, jnp.concatenate(dcw, axis=0))


def _params(sem=None):
    return pltpu.CompilerParams(dimension_semantics=sem, vmem_limit_bytes=VMEM_LIMIT)


def _seq_specs(sb, width, rowblk, colblk):
    main = pl.BlockSpec((sb, width), lambda a, b: (rowblk(a, b), colblk(a, b)))
    halo = pl.BlockSpec((8, width), lambda a, b: (jnp.maximum(rowblk(a, b) * (sb // 8) - 1, 0), colblk(a, b)))
    return main, halo


def _matmul(a, b, *, ta=False, tb=False, tm, tn, tk, res=None, out_dtype=f32, name):
    if ta:
        kdim, m = a.shape
    else:
        m, kdim = a.shape
    n = b.shape[0] if tb else b.shape[1]
    tm, tn, tk = min(tm, m), min(tn, n), min(tk, kdim)
    assert m % tm == 0 and n % tn == 0 and kdim % tk == 0, (name, m, n, kdim, tm, tn, tk)
    nk = kdim // tk
    dims = (((0 if ta else 1,), (1 if tb else 0,)), ((), ()))
    has_res = res is not None

    def body(*refs):
        a_ref, b_ref = refs[:2]
        r_ref = refs[2] if has_res else None
        o_ref = refs[3] if has_res else refs[2]
        acc_ref = refs[-1] if nk > 1 else None
        part = lax.dot_general(_b(a_ref[...]), _b(b_ref[...]), dims, preferred_element_type=f32)

        def finish(total):
            if has_res:
                total = total + r_ref[...]
            o_ref[...] = total.astype(out_dtype)

        if nk == 1:
            finish(part)
        else:
            k = pl.program_id(2)

            @pl.when(k == 0)
            def _():
                acc_ref[...] = part

            @pl.when(k > 0)
            def _():
                acc_ref[...] += part

            @pl.when(k == nk - 1)
            def _():
                finish(acc_ref[...])

    a_spec = pl.BlockSpec((tk, tm), lambda i, j, k: (k, i)) if ta else pl.BlockSpec((tm, tk), lambda i, j, k: (i, k))
    b_spec = pl.BlockSpec((tn, tk), lambda i, j, k: (j, k)) if tb else pl.BlockSpec((tk, tn), lambda i, j, k: (k, j))
    o_spec = pl.BlockSpec((tm, tn), lambda i, j, k: (i, j))
    in_specs, args = [a_spec, b_spec], [a, b]
    if has_res:
        in_specs.append(o_spec)
        args.append(res)
    return pl.pallas_call(
        body, name=name, grid=(m // tm, n // tn, nk), in_specs=in_specs, out_specs=o_spec,
        out_shape=jax.ShapeDtypeStruct((m, n), out_dtype),
        scratch_shapes=[pltpu.VMEM((tm, tn), f32)] if nk > 1 else [],
        compiler_params=_params(("parallel", "parallel", "arbitrary")),
    )(*args)


def _transpose(x, name):
    r, c = x.shape
    tr, tc = min(512, r), min(512, c)

    def body(x_ref, o_ref):
        o_ref[...] = x_ref[...].T

    return pl.pallas_call(
        body, name=name, grid=(r // tr, c // tc), in_specs=[pl.BlockSpec((tr, tc), lambda i, j: (i, j))],
        out_specs=pl.BlockSpec((tc, tr), lambda i, j: (j, i)), out_shape=jax.ShapeDtypeStruct((c, r), x.dtype),
        compiler_params=_params(("parallel", "parallel")),
    )(x)


def _d_hn(dxs, dba, wp, name):
    s = dxs[0].shape[0]
    tm = min(512, s)

    def body(*refs):
        dx_refs = refs[:N_GROUPS]
        dba_ref, w_ref, o_ref = refs[N_GROUPS:]
        total = _dot_nt(_b(dba_ref[...]), w_ref[:, BLK_BA * HD:])
        for g in range(N_GROUPS):
            total = total + _dot_nt(dx_refs[g][...], w_ref[:, g * D_MODEL:(g + 1) * D_MODEL])
        o_ref[...] = total

    row = lambda w: pl.BlockSpec((tm, w), lambda i: (i, 0))
    return pl.pallas_call(
        body, name=name, grid=(s // tm,),
        in_specs=[row(D_MODEL)] * N_GROUPS + [row(HD), pl.BlockSpec((D_MODEL, D_INP), lambda i: (0, 0))],
        out_specs=row(D_MODEL), out_shape=jax.ShapeDtypeStruct((s, D_MODEL), f32),
        compiler_params=_params(("parallel",)),
    )(*dxs, dba, wp)


def _d_win(hnt, dxs, dba, name):
    s = hnt.shape[1]
    tn = 256
    per = D_MODEL // tn

    def body(*refs):
        hnt_ref = refs[0]
        dx_refs = refs[1:1 + N_GROUPS]
        o_ref = refs[1 + N_GROUPS]
        j = pl.program_id(0)
        for g in range(N_GROUPS):
            @pl.when(j // per == g)
            def _(g=g):
                o_ref[...] = _dot(hnt_ref[...], _b(dx_refs[g][...])).astype(bf16)

    def dx_spec(g):
        on = lambda j: (j // per == g).astype(jnp.int32)
        return pl.BlockSpec((s, tn), lambda j: (0, (j % per) * on(j)))

    main = pl.pallas_call(
        body, name=name, grid=(N_GROUPS * per,),
        in_specs=[pl.BlockSpec((D_MODEL, s), lambda j: (0, 0))] + [dx_spec(g) for g in range(N_GROUPS)],
        out_specs=pl.BlockSpec((D_MODEL, tn), lambda j: (0, j)),
        out_shape=jax.ShapeDtypeStruct((D_MODEL, D_INP), bf16),
        compiler_params=_params(("parallel",)),
    )(hnt, *dxs)

    def tail(hnt_ref, dba_ref, full_ref, o_ref):
        del full_ref
        o_ref[...] = _dot(hnt_ref[...], _b(dba_ref[...])).astype(bf16)

    return pl.pallas_call(
        tail, name=name + "_ba", grid=(1,),
        in_specs=[pl.BlockSpec((D_MODEL, s), lambda k: (0, 0)), pl.BlockSpec((s, HD), lambda k: (0, 0)),
                  pl.BlockSpec(memory_space=pl.ANY)],
        out_specs=pl.BlockSpec((D_MODEL, HD), lambda k: (0, BLK_BA)),
        out_shape=jax.ShapeDtypeStruct((D_MODEL, D_INP), bf16),
        input_output_aliases={2: 0},
        compiler_params=_params(("arbitrary",)),
    )(hnt, dba, main)


def _rms_fwd(x, w, name):
    s = x.shape[0]
    tr = min(512, s)

    def body(x_ref, w_ref, h_ref):
        xv = x_ref[...]
        r = lax.rsqrt(_rowmean(xv * xv) + EPS)
        h_ref[...] = (xv * r * w_ref[...]).astype(bf16)

    return pl.pallas_call(
        body, name=name, grid=(s // tr,),
        in_specs=[pl.BlockSpec((tr, D_MODEL), lambda i: (i, 0)), pl.BlockSpec((1, D_MODEL), lambda i: (0, 0))],
        out_specs=pl.BlockSpec((tr, D_MODEL), lambda i: (i, 0)),
        out_shape=jax.ShapeDtypeStruct((s, D_MODEL), bf16), compiler_params=_params(("parallel",)),
    )(x, w)


def _rms_bwd(dh, x, w, dres, name):
    s = x.shape[0]
    tr = min(512, s)

    def body(dh_ref, x_ref, w_ref, dres_ref, dx_ref, dw_ref):
        xv = x_ref[...]
        r = lax.rsqrt(_rowmean(xv * xv) + EPS)
        xn = xv * r
        d = dh_ref[...]
        dxn = d * w_ref[...]
        dx_ref[...] = dres_ref[...] + r * (dxn - xn * _rowmean(dxn * xn))
        part = _sum0(d * xn)

        @pl.when(pl.program_id(0) == 0)
        def _():
            dw_ref[...] = part

        @pl.when(pl.program_id(0) > 0)
        def _():
            dw_ref[...] += part

    row = pl.BlockSpec((tr, D_MODEL), lambda i: (i, 0))
    vec = pl.BlockSpec((1, D_MODEL), lambda i: (0, 0))
    return pl.pallas_call(
        body, name=name, grid=(s // tr,), in_specs=[row, row, vec, row], out_specs=[row, vec],
        out_shape=[jax.ShapeDtypeStruct((s, D_MODEL), f32), jax.ShapeDtypeStruct((1, D_MODEL), f32)],
        compiler_params=_params(("arbitrary",)),
    )(dh, x, w, dres)


def _final_loss(x, w, target, name):
    s = x.shape[0]
    tr = min(512, s)

    def body(x_ref, w_ref, t_ref, dx_ref, dw_ref, loss_ref):
        xv = x_ref[...]
        wv = w_ref[...]
        r = lax.rsqrt(_rowmean(xv * xv) + EPS)
        xn = xv * r
        e = xn * wv - t_ref[...]
        lb = jnp.broadcast_to(0.5 * _sum0(_rowmean(e * e)), (1, HD))
        dy = e * (1.0 / D_MODEL)
        dxn = dy * wv
        dx_ref[...] = r * (dxn - xn * _rowmean(dxn * xn))
        part = _sum0(dy * xn)

        @pl.when(pl.program_id(0) == 0)
        def _():
            dw_ref[...] = part
            loss_ref[...] = lb

        @pl.when(pl.program_id(0) > 0)
        def _():
            dw_ref[...] += part
            loss_ref[...] += lb

    row = pl.BlockSpec((tr, D_MODEL), lambda i: (i, 0))
    vec = pl.BlockSpec((1, D_MODEL), lambda i: (0, 0))
    lsp = pl.BlockSpec((1, HD), lambda i: (0, 0))
    return pl.pallas_call(
        body, name=name, grid=(s // tr,), in_specs=[row, vec, row], out_specs=[row, vec, lsp],
        out_shape=[jax.ShapeDtypeStruct((s, D_MODEL), f32), jax.ShapeDtypeStruct((1, D_MODEL), f32),
                   jax.ShapeDtypeStruct((1, HD), f32)],
        compiler_params=_params(("arbitrary",)),
    )(x, w, target)


ALL = slice(None)


def _lru_gates(xc, vec, wa, wx):
    sp = _softplus(-vec[3:4])
    xcb = _b(xc)
    r = _sig(_dot(xcb, wa) + vec[1:2])
    i = _sig(_dot(xcb, wx) + vec[2:3])
    la = (-LRU_C) * r * sp
    a = jnp.exp(la)
    mult = jnp.sqrt(-_expm1(2.0 * la))
    return r, i, a, mult, sp, xcb


def _lru_fwd(proj, cw, vec, wa, wx, name):
    s = proj.shape[0]
    sb = min(SEQ_BLOCK, s)
    nsb = s // sb
    t = min(LRU_T, sb)
    nc = sb // t
    hp = LRU_HP
    wide = hp * HD
    lanes = [slice(j * HD, (j + 1) * HD) for j in range(hp)]

    def body(x_ref, xh_ref, z_ref, cw_ref, vec_ref, wa_ref, wx_ref, y_ref, hl_ref, hc_ref):
        sblk = pl.program_id(1)
        consts = [(cw_ref[:, hl], vec_ref[:, hl], _b(wa_ref[j]), _b(wx_ref[j])) for j, hl in enumerate(lanes)]
        row = lax.broadcasted_iota(jnp.int32, (t, HD), 0)

        @pl.when(sblk == 0)
        def _():
            hc_ref[...] = jnp.zeros((8, wide), f32)

        def one_head(j, x, halo, z, hcarry):
            cwv, vec_v, wa_v, wx_v = consts[j]
            xs = _conv_taps(x, halo)
            xc = vec_v[0:1] + _conv(xs, cwv)
            r, i, a, mult, _, _ = _lru_gates(xc, vec_v, wa_v, wx_v)
            yield
            bb = mult * (i * xc)
            d = 1
            while d < t:
                m = row >= d
                bb = jnp.where(m, a * pltpu.roll(bb, d, axis=0) + bb, bb)
                a = jnp.where(m, a * pltpu.roll(a, d, axis=0), a)
                d *= 2
                yield
            hl = bb + a * hcarry
            nrm = lax.rsqrt(_rowmean(hl * hl) + EPS)
            return hl, (hl * nrm * vec_v[4:5] * (z * _sig(z))).astype(bf16)

        def chunk(c, carries):
            t0 = pl.multiple_of(c * t, t)
            rows = pl.ds(t0, t)
            ins = [(x_ref[rows, hl], _rows_before(x_ref, xh_ref, hl, t0, c, sblk), z_ref[rows, hl]) for hl in lanes]
            outs = _round_robin([one_head(j, *ins[j], carries[j]) for j in range(hp)])
            for j, hl in enumerate(lanes):
                hl_ref[rows, hl] = outs[j][0]
                y_ref[rows, hl] = outs[j][1]
            return tuple(out[0][t - 1:t, :] for out in outs)

        final = lax.fori_loop(0, nc, chunk, tuple(hc_ref[0:1, hl] for hl in lanes))
        hc_ref[...] = jnp.concatenate([jnp.broadcast_to(f, (8, HD)) for f in final], axis=1)

    xsp, xhalo = _seq_specs(sb, wide, lambda g, i: i, lambda g, i: g)
    zsp, _ = _seq_specs(sb, wide, lambda g, i: i, lambda g, i: HEADS // hp + g)
    gcol = lambda g, i: (0, g)
    wsp = pl.BlockSpec((hp, HD, HD), lambda g, i: (g, 0, 0))
    osp = pl.BlockSpec((sb, wide), lambda g, i: (i, g))
    return pl.pallas_call(
        body, name=name, grid=(HEADS // hp, nsb),
        in_specs=[xsp, xhalo, zsp, pl.BlockSpec((4, wide), gcol), pl.BlockSpec((8, wide), gcol), wsp, wsp],
        out_specs=[osp, osp],
        out_shape=[jax.ShapeDtypeStruct((s, D_MIX), bf16), jax.ShapeDtypeStruct((s, D_MODEL), f32)],
        scratch_shapes=[pltpu.VMEM((8, wide), f32)],
        compiler_params=_params(("parallel", "arbitrary")),
    )(proj, proj, proj, cw, vec, wa, wx)


def _lru_bwd(dy, proj, hl, cw, vec, wa, wx, name):
    s = proj.shape[0]
    sb = min(SEQ_BLOCK, s)
    nsb = s // sb
    t = min(LRU_T, sb)
    nc = sb // t
    hp = LRU_HP
    wide = hp * HD
    lanes = [slice(j * HD, (j + 1) * HD) for j in range(hp)]
    n_acc = 10

    def body(dy_ref, x_ref, xh_ref, z_ref, hl_ref, hlh_ref, cw_ref, vec_ref, wa_ref, wx_ref,
             dx_ref, dz_ref, dcw_ref, dvec_ref, dwa_ref, dwx_ref, acc_ref, dxh_ref):
        step = pl.program_id(1)
        sblk = nsb - 1 - step
        consts = [(cw_ref[:, hl], vec_ref[:, hl], _b(wa_ref[j]), _b(wx_ref[j])) for j, hl in enumerate(lanes)]
        row = lax.broadcasted_iota(jnp.int32, (t, HD), 0)

        @pl.when(step == 0)
        def _():
            acc_ref[...] = jnp.zeros((16, wide), f32)
            dxh_ref[...] = jnp.zeros((8, wide), f32)
            dwa_ref[...] = jnp.zeros((hp, HD, HD), f32)
            dwx_ref[...] = jnp.zeros((hp, HD, HD), f32)

        def one_head(j, x, halo, z, hv, hhalo, dyv, carry):
            dcw0, dcw1, dcw2, dcw3, dcb, dba, dbx, dsp, dlnw, mu, dxc_halo = carry
            cwv, vec_v, wa_v, wx_v = consts[j]
            lnw = vec_v[4:5]
            xs = _conv_taps(x, halo)
            xc = vec_v[0:1] + _conv(xs, cwv)
            r, i, a, mult, sp, xcb = _lru_gates(xc, vec_v, wa_v, wx_v)
            yield
            hprev = pltpu.roll(jnp.concatenate([hhalo, hv], axis=0), 1, axis=0)[8:]
            sgz = _sig(z)
            sz = z * sgz
            nrm = lax.rsqrt(_rowmean(hv * hv) + EPS)
            hn = hv * nrm
            dlnw = dlnw + _sum0(dyv * hn * sz)
            dzv = _b(dyv * hn * lnw * _dsilu(z, sgz))
            dhn = dyv * lnw * sz
            lam = nrm * (dhn - hn * _rowmean(dhn * hn))
            cf = jnp.where(row == t - 1, 1.0, pltpu.roll(a, t - 1, axis=0))
            d = 1
            while d < t:
                m = row < t - d
                lam = jnp.where(m, lam + cf * pltpu.roll(lam, t - d, axis=0), lam)
                cf = jnp.where(m, cf * pltpu.roll(cf, t - d, axis=0), cf)
                d *= 2
                yield
            lam = lam + cf * mu
            mu = a[0:1] * lam[0:1]
            da = lam * hprev
            dmult = lam * (i * xc)
            di = lam * mult * xc
            dxc = lam * mult * i
            dla = da * a - dmult * (a * a) / mult
            dr = dla * (-LRU_C * sp)
            dsp = dsp + _sum0(dla * (-LRU_C * r))
            dra = dr * r * (1.0 - r)
            dia = di * i * (1.0 - i)
            dba = dba + _sum0(dra)
            dbx = dbx + _sum0(dia)
            drab, diab = _b(dra), _b(dia)
            dwa = _dot_tn(xcb, drab)
            dwx = _dot_tn(xcb, diab)
            dxc = dxc + _dot_nt(drab, wa_v) + _dot_nt(diab, wx_v)
            yield
            dcb = dcb + _sum0(dxc)
            dcw0 = dcw0 + _sum0(dxc * xs[0])
            dcw1 = dcw1 + _sum0(dxc * xs[1])
            dcw2 = dcw2 + _sum0(dxc * xs[2])
            dcw3 = dcw3 + _sum0(dxc * xs[3])
            dxv = _b(_conv_back(dxc, dxc_halo, cwv))
            return dxv, dzv, dwa, dwx, (dcw0, dcw1, dcw2, dcw3, dcb, dba, dbx, dsp, dlnw, mu, dxc[0:8])

        def chunk(ci, carries):
            c = nc - 1 - ci
            t0 = pl.multiple_of(c * t, t)
            rows = pl.ds(t0, t)
            ins = [(x_ref[rows, hl], _rows_before(x_ref, xh_ref, hl, t0, c, sblk), z_ref[rows, hl], hl_ref[rows, hl],
                    _rows_before(hl_ref, hlh_ref, hl, t0, c, sblk), dy_ref[rows, hl]) for hl in lanes]
            outs = _round_robin([one_head(j, *ins[j], carries[j]) for j in range(hp)])
            for j, hl in enumerate(lanes):
                dx_ref[rows, hl] = outs[j][0]
                dz_ref[rows, hl] = outs[j][1]
                dwa_ref[j] += outs[j][2]
                dwx_ref[j] += outs[j][3]
            return tuple(out[4] for out in outs)

        acc = acc_ref[...]
        dxh = dxh_ref[...]
        init = tuple(tuple(acc[k:k + 1, hl] for k in range(n_acc)) + (dxh[:, hl],) for hl in lanes)
        final = lax.fori_loop(0, nc, chunk, init)
        rows_out = [jnp.concatenate([final[j][k] for j in range(hp)], axis=1) for k in range(n_acc)]
        zero = jnp.zeros((1, wide), f32)
        acc_ref[...] = jnp.concatenate(rows_out + [zero] * (16 - n_acc), axis=0)
        dxh_ref[...] = jnp.concatenate([final[j][n_acc] for j in range(hp)], axis=1)

        @pl.when(step == nsb - 1)
        def _():
            dcw_ref[...] = jnp.concatenate(rows_out[0:4], axis=0)
            dlam = -rows_out[7] * _sig(-vec_ref[3:4, :])
            dvec_ref[...] = jnp.concatenate([rows_out[4], rows_out[5], rows_out[6], dlam, rows_out[8], zero, zero, zero],
                                            axis=0)

    rblk = lambda g, i: nsb - 1 - i
    xsp, xhalo = _seq_specs(sb, wide, rblk, lambda g, i: g)
    zsp, _ = _seq_specs(sb, wide, rblk, lambda g, i: HEADS // hp + g)
    gcol = lambda g, i: (0, g)
    wsp = pl.BlockSpec((hp, HD, HD), lambda g, i: (g, 0, 0))
    return pl.pallas_call(
        body, name=name, grid=(HEADS // hp, nsb),
        in_specs=[xsp, xsp, xhalo, zsp, xsp, xhalo, pl.BlockSpec((4, wide), gcol), pl.BlockSpec((8, wide), gcol), wsp, wsp],
        out_specs=[xsp, xsp, pl.BlockSpec((4, wide), gcol), pl.BlockSpec((8, wide), gcol), wsp, wsp],
        out_shape=[jax.ShapeDtypeStruct((s, D_MODEL), bf16), jax.ShapeDtypeStruct((s, D_MODEL), bf16),
                   jax.ShapeDtypeStruct((4, D_MODEL), f32), jax.ShapeDtypeStruct((8, D_MODEL), f32),
                   jax.ShapeDtypeStruct((HEADS, HD, HD), f32), jax.ShapeDtypeStruct((HEADS, HD, HD), f32)],
        scratch_shapes=[pltpu.VMEM((16, wide), f32), pltpu.VMEM((8, wide), f32)],
        compiler_params=_params(("parallel", "arbitrary")),
    )(dy, proj, proj, proj, hl, hl, cw, vec, wa, wx)


def _lane_pick(x, lane, idx):
    return _sum1(jnp.where(lane == idx, x, 0.0))


def _round_robin(gens):
    results = [None] * len(gens)
    live = list(range(len(gens)))
    while live:
        for i in list(live):
            try:
                next(gens[i])
            except StopIteration as done:
                results[i] = done.value
                live.remove(i)
    return results


def _sdot(a, b):
    return _dot(_b(a), _b(b))


def _sdot_tn(a, b):
    return _dot_tn(_b(a), _b(b))


def _sdot_nt(a, b):
    return _dot_nt(_b(a), _b(b))


def _tri_dot(tri, x):
    trib = tri.astype(bf16)
    x1 = x.astype(bf16)
    r1 = x - x1.astype(f32)
    x2 = r1.astype(bf16)
    x3 = (r1 - x2.astype(f32)).astype(bf16)
    return _dot(trib, x1) + _dot(trib, x2) + _dot(trib, x3)


def _inv_unit_lower(a):
    n = -a
    p = _sdot(a, a)
    yield
    for k in range(5):
        n = n + p + _sdot(n, p)
        if k < 4:
            p = _sdot(p, p)
        yield
    return n


def _dn_gates(bav, dnv, masks):
    ltri = masks[4]
    ea = jnp.exp(pltpu.roll(dnv[0:1], HEADS, axis=1))
    dt = pltpu.roll(dnv[1:2], HEADS, axis=1)
    u = bav + dt
    g = -ea * _softplus(u)
    return dict(beta=_sig(bav), g=g, gc=_tri_dot(ltri, g), sga=_sig(u), ea=ea)


def _dn_chunk(x3, halo3, gates, cw3, h, masks, cpre=None, tinv=None):
    lane, ri, ci, eye, ltri = masks
    if cpre is None:
        cpre = _conv(_conv_taps(x3, halo3), cw3)
    sg = _sig(cpre)
    act = cpre * sg
    q_raw, k_raw, v = act[:, 0:HD], act[:, HD:2 * HD], act[:, 2 * HD:3 * HD]
    rq = lax.rsqrt(_sum1(q_raw * q_raw) + EPS)
    rk = lax.rsqrt(_sum1(k_raw * k_raw) + EPS)
    qn = q_raw * rq
    k = k_raw * rk
    q = qn * QSCALE
    beta = jnp.broadcast_to(_lane_pick(gates["beta"], lane, h), (CHUNK, HD))
    gc = jnp.broadcast_to(_lane_pick(gates["gc"], lane, HEADS + h), (CHUNK, HD))
    yield
    gc64 = gc[:, 0:CHUNK]
    grow = _sum0(gc64 * eye)
    dm = jnp.exp(jnp.where(ri >= ci, gc64 - grow, -1e30))
    ds_ = jnp.where(ri > ci, dm, 0.0)
    eg = jnp.exp(gc)
    glast = gc[CHUNK - 1:CHUNK, :]
    ek = jnp.exp(glast - gc)
    kb = k * beta
    kbf = _b(k)
    amat = _dot_nt(_b(kb), kbf) * ds_
    pmat = _dot_nt(_b(q), kbf) * dm
    yield
    if tinv is None:
        tinv = yield from _inv_unit_lower(amat)
    return dict(cpre=cpre, sg=sg, rq=rq, rk=rk, qn=qn, q=q, k=k, v=v, kbf=kbf, beta=beta, gc=gc, dm=dm, ds=ds_, eg=eg,
                glast=glast, ek=ek, kb=kb, amat=amat, tinv=tinv, pmat=pmat)


def _dn_masks():
    lane = lax.broadcasted_iota(jnp.int32, (CHUNK, HD), 1)
    ri = lax.broadcasted_iota(jnp.int32, (CHUNK, CHUNK), 0)
    ci = lax.broadcasted_iota(jnp.int32, (CHUNK, CHUNK), 1)
    eye = (ri == ci).astype(f32)
    ltri = (ri >= ci).astype(f32)
    return lane, ri, ci, eye, ltri


def _dn_load_qkv(q_ref, qh_ref, k_ref, kh_ref, v_ref, vh_ref, hl, rows, t0, c, sblk):
    x3 = jnp.concatenate([q_ref[rows, hl], k_ref[rows, hl], v_ref[rows, hl]], axis=1)
    halo3 = jnp.concatenate([_rows_before(q_ref, qh_ref, hl, t0, c, sblk), _rows_before(k_ref, kh_ref, hl, t0, c, sblk),
                             _rows_before(v_ref, vh_ref, hl, t0, c, sblk)], axis=1)
    return x3, halo3


def _dn_cw3(cwq_ref, cwk_ref, cwv_ref, j):
    return jnp.concatenate([r[:, j * HD:(j + 1) * HD] for r in (cwq_ref, cwk_ref, cwv_ref)], axis=1)


def _dn_fwd(proj, cw, dnv, y, name):
    s = proj.shape[0]
    sb = min(DN_SEQ_BLOCK, s)
    nsb = s // sb
    nc = sb // CHUNK
    hp = DN_HP

    def body(q_ref, qh_ref, k_ref, kh_ref, v_ref, vh_ref, z_ref, ba_ref, cwq_ref, cwk_ref, cwv_ref, dnv_ref, yin_ref,
             y_ref, o_ref, st_ref, ti_ref, cp_ref, s_ref):
        del yin_ref
        grp = pl.program_id(0)
        sblk = pl.program_id(1)
        masks = _dn_masks()
        dnv_v = dnv_ref[...]
        cw3 = [_dn_cw3(cwq_ref, cwk_ref, cwv_ref, j) for j in range(hp)]

        @pl.when(sblk == 0)
        def _():
            s_ref[...] = jnp.zeros((hp, HD, HD), f32)

        def one_head(j, x3, halo3, gates, z, st):
            f = yield from _dn_chunk(x3, halo3, gates, cw3[j], grp * hp + j, masks)
            sbf = _b(st)
            qs = _dot(_b(f["q"] * f["eg"]), sbf)
            rhs = f["beta"] * (f["v"] - _dot(_b(f["k"] * f["eg"]), sbf))
            yield
            vn = rhs + _sdot(f["tinv"], rhs)
            yield
            vnb = _b(vn)
            o = qs + _dot(_b(f["pmat"]), vnb)
            st_new = st * jnp.exp(f["glast"]) + _dot_tn(_b(f["k"] * f["ek"]), vnb)
            yield
            nrm = lax.rsqrt(_rowmean(o * o) + EPS)
            yv = (o * nrm * dnv_v[2:3] * (z * _sig(z))).astype(bf16)
            return o, yv, st_new, f["tinv"], f["cpre"]

        def chunk(c, states):
            t0 = pl.multiple_of(c * CHUNK, CHUNK)
            rows = pl.ds(t0, CHUNK)
            bav = ba_ref[rows, :]
            ins = []
            for j in range(hp):
                hl = slice(j * HD, (j + 1) * HD)
                ins.append(_dn_load_qkv(q_ref, qh_ref, k_ref, kh_ref, v_ref, vh_ref, hl, rows, t0, c, sblk)
                           + (z_ref[rows, hl],))
            gates = _dn_gates(bav, dnv_v, masks)
            outs = _round_robin([one_head(j, ins[j][0], ins[j][1], gates, ins[j][2], states[j]) for j in range(hp)])
            for j in range(hp):
                hl = slice(j * HD, (j + 1) * HD)
                st_ref[j, c] = states[j]
                o_ref[rows, hl] = outs[j][0]
                y_ref[rows, hl] = outs[j][1]
                ti_ref[j, c] = outs[j][3]
                cp_ref[rows, j * 3 * HD:(j + 1) * 3 * HD] = outs[j][4]
            return tuple(out[2] for out in outs)

        final = lax.fori_loop(0, nc, chunk, tuple(s_ref[j] for j in range(hp)))
        for j in range(hp):
            s_ref[j] = final[j]

    wide = hp * HD
    grp_specs = lambda off: _seq_specs(sb, wide, lambda g, i: i, lambda g, i: off // hp + g)
    (qsp, qhalo), (ksp, khalo), (vsp, vhalo) = grp_specs(2 * HEADS), grp_specs(3 * HEADS), grp_specs(4 * HEADS)
    zsp, _ = grp_specs(5 * HEADS)
    basp = pl.BlockSpec((sb, HD), lambda g, i: (i, BLK_BA))
    cws = lambda off: pl.BlockSpec((4, wide), lambda g, i: (0, off // hp + g))
    osp = lambda off: pl.BlockSpec((sb, wide), lambda g, i: (i, off // hp + g))
    return pl.pallas_call(
        body, name=name, grid=(HEADS // hp, nsb),
        in_specs=[qsp, qhalo, ksp, khalo, vsp, vhalo, zsp, basp, cws(0), cws(HEADS), cws(2 * HEADS),
                  pl.BlockSpec((8, HD), lambda g, i: (0, 0)), pl.BlockSpec(memory_space=pl.ANY)],
        out_specs=[osp(HEADS), osp(0), pl.BlockSpec((hp, nc, HD, HD), lambda g, i: (g, i, 0, 0)),
                   pl.BlockSpec((hp, nc, CHUNK, CHUNK), lambda g, i: (g, i, 0, 0)),
                   pl.BlockSpec((sb, 3 * wide), lambda g, i: (i, g))],
        out_shape=[jax.ShapeDtypeStruct((s, D_MIX), bf16), jax.ShapeDtypeStruct((s, D_MODEL), f32),
                   jax.ShapeDtypeStruct((HEADS, s // CHUNK, HD, HD), f32),
                   jax.ShapeDtypeStruct((HEADS, s // CHUNK, CHUNK, CHUNK), f32),
                   jax.ShapeDtypeStruct((s, 3 * D_MODEL), f32)],
        scratch_shapes=[pltpu.VMEM((hp, HD, HD), f32)],
        input_output_aliases={12: 0},
        compiler_params=_params(("parallel", "arbitrary")),
    )(proj, proj, proj, proj, proj, proj, proj, proj, cw, cw, cw, dnv, y)


def _dn_bwd(dy, proj, o, states, tinvs, cpres, cw, dnv, name):
    s = proj.shape[0]
    sb = min(DN_SEQ_BLOCK, s)
    nsb = s // sb
    nc = sb // CHUNK
    hp = DN_HP
    ngrp = HEADS // hp

    def body(dy_ref, q_ref, k_ref, v_ref, z_ref, ba_ref, o_ref, st_ref, ti_ref, cp_ref,
             cwq_ref, cwk_ref, cwv_ref, dnv_ref,
             dq_ref, dk_ref, dv_ref, dz_ref, dba_ref, dcw_ref, dsc_ref,
             ds_ref, dch_ref, cwacc_ref, nacc_ref):
        step = pl.program_id(0)
        grp = pl.program_id(1)
        sblk = nsb - 1 - step
        h0 = grp * hp
        masks = _dn_masks()
        lane, ri, ci, eye, ltri = masks
        utri = (ri <= ci).astype(f32)
        cw3s = [_dn_cw3(cwq_ref, cwk_ref, cwv_ref, j) for j in range(hp)]
        dnv_v = dnv_ref[...]
        dnw = dnv_v[2:3]
        row64 = lax.broadcasted_iota(jnp.int32, (CHUNK, HD), 0)

        @pl.when(step == 0)
        def _():
            for j in range(hp):
                ds_ref[h0 + j] = jnp.zeros((HD, HD), f32)
                dch_ref[h0 + j] = jnp.zeros((8, 3 * HD), f32)
                cwacc_ref[h0 + j] = jnp.zeros((8, 3 * HD), f32)

        @pl.when((step == 0) & (grp == 0))
        def _():
            nacc_ref[...] = jnp.zeros((8, HD), f32)

        def one_head(j, x3, gates, z, st, ti, cp, ov, dyv, carry):
            dst, dch, cwacc = carry
            cw3 = cw3s[j]
            f = yield from _dn_chunk(None, None, gates, cw3, h0 + j, masks, cp, ti)
            q, k, v, beta, eg, ek, kbf = f["q"], f["k"], f["v"], f["beta"], f["eg"], f["ek"], f["kbf"]
            sbf = _b(st)
            dstb = _b(dst)
            qe = q * eg
            keg = k * eg
            kd = k * ek
            kegb, qeb, kdb = _b(keg), _b(qe), _b(kd)
            e = v - _dot(kegb, sbf)
            yield
            rhs = beta * e
            vn = rhs + _sdot(f["tinv"], rhs)
            yield
            vnb = _b(vn)
            pb = _b(f["pmat"])
            sgz = _sig(z)
            sz = z * sgz
            nrm = lax.rsqrt(_rowmean(ov * ov) + EPS)
            on = ov * nrm
            ddnw = _sum0(dyv * on * sz)
            dpz = dyv * on * dnw * _dsilu(z, sgz)
            don = dyv * dnw * sz
            do = nrm * (don - on * _rowmean(don * on))
            dob = _b(do)
            dvn = _dot_tn(pb, dob) + _dot(kdb, dstb)
            dpm = jnp.where(ri >= ci, _dot_nt(dob, vnb), 0.0)
            dqe = _dot_nt(dob, sbf)
            dkd = _dot_nt(vnb, dstb)
            qtdo = _dot_tn(qeb, dob)
            yield
            drhs = dvn + _sdot_tn(f["tinv"], dvn)
            dqk = _b(dpm * f["dm"])
            dq = _dot(dqk, kbf) + dqe * eg
            dk_p = _dot_tn(dqk, _b(q))
            yield
            dam = jnp.where(ri > ci, -_sdot_nt(drhs, vn), 0.0)
            de = drhs * beta
            deb = _b(de)
            dbeta = _sum1(drhs * e)
            dkeg = -_dot_nt(deb, sbf)
            dst_new = qtdo + dst * jnp.exp(f["glast"]) - _dot_tn(kegb, deb)
            yield
            dkk = _b(dam * f["ds"])
            dkb = _dot(dkk, kbf)
            dk = dk_p + _dot_tn(dkk, _b(f["kb"])) + dkb * beta + dkeg * eg + dkd * ek
            yield
            dbeta = dbeta + _sum1(dkb * k)
            mm = dpm * f["pmat"] + dam * f["amat"]
            dglast = _sum1(_sum0(dst * st)) * jnp.exp(f["glast"]) + _sum1(_sum0(dkd * kd))
            dgc = (_sum1(mm) - _sum1(eye * _sum0(mm)) + _sum1(dqe * qe) + _sum1(dkeg * keg) - _sum1(dkd * kd))
            dgc = jnp.broadcast_to(dgc, (CHUNK, HD)) + jnp.where(row64 == CHUNK - 1, dglast, 0.0)
            dq_raw = (QSCALE * f["rq"]) * (dq - f["qn"] * _sum1(f["qn"] * dq))
            dk_raw = f["rk"] * (dk - k * _sum1(k * dk))
            dact = jnp.concatenate([dq_raw, dk_raw, de], axis=1)
            dc = dact * _dsilu(f["cpre"], f["sg"])
            dqkv, dcw_part = _conv_back(dc, dch, cw3, x3)
            cwacc = cwacc + dcw_part
            return dpz, dqkv, (dgc, dbeta), ddnw, (dst_new, dc[0:8], cwacc)

        def chunk(cidx, carry):
            heads, nacc, sa0, sa1 = carry
            c = nc - 1 - cidx
            t0 = pl.multiple_of(c * CHUNK, CHUNK)
            rows = pl.ds(t0, CHUNK)
            bav = ba_ref[rows, :]
            ins = []
            for j in range(hp):
                hl = slice(j * HD, (j + 1) * HD)
                x3 = jnp.concatenate([q_ref[rows, hl], k_ref[rows, hl], v_ref[rows, hl]], axis=1)
                ins.append((x3, z_ref[rows, hl], st_ref[j, c], ti_ref[j, c],
                            cp_ref[rows, j * 3 * HD:(j + 1) * 3 * HD], o_ref[rows, hl], dy_ref[rows, hl]))
            gates = _dn_gates(bav, dnv_v, masks)
            outs = _round_robin([one_head(j, ins[j][0], gates, *ins[j][1:], heads[j]) for j in range(hp)])
            dgc_all = jnp.zeros((CHUNK, HD), f32)
            dbeta_all = jnp.zeros((CHUNK, HD), f32)
            for j in range(hp):
                hl = slice(j * HD, (j + 1) * HD)
                dpz, dqkv, (dgc, dbeta), ddnw, _ = outs[j]
                dgc_all = jnp.where(lane == HEADS + h0 + j, dgc, dgc_all)
                dbeta_all = jnp.where(lane == h0 + j, dbeta, dbeta_all)
                dq_ref[rows, hl] = _b(dqkv[:, 0:HD])
                dk_ref[rows, hl] = _b(dqkv[:, HD:2 * HD])
                dv_ref[rows, hl] = _b(dqkv[:, 2 * HD:3 * HD])
                dz_ref[rows, hl] = _b(dpz)
                nacc = nacc + ddnw
            dg_all = _tri_dot(utri, dgc_all)
            dain_all = dg_all * (-gates["ea"]) * gates["sga"]
            dba = dbeta_all * gates["beta"] * (1.0 - gates["beta"]) + dain_all
            sa0 = sa0 + _sum0(dg_all * gates["g"])
            sa1 = sa1 + _sum0(dain_all)

            @pl.when(grp == 0)
            def _():
                dba_ref[rows, :] = dba

            @pl.when(grp > 0)
            def _():
                dba_ref[rows, :] += dba

            return tuple(out[4] for out in outs), nacc, sa0, sa1

        init = tuple((ds_ref[h0 + j], dch_ref[h0 + j], cwacc_ref[h0 + j][0:4]) for j in range(hp))
        heads, nacc, sa0, sa1 = lax.fori_loop(
            0, nc, chunk, (init, nacc_ref[0:1, :], nacc_ref[1:2, :], nacc_ref[2:3, :]))
        nacc_ref[0:3, :] = jnp.concatenate([nacc, sa0, sa1], axis=0)
        zpad = jnp.zeros((4, 3 * HD), f32)
        for j in range(hp):
            dst, dch, cwacc = heads[j]
            ds_ref[h0 + j] = dst
            dch_ref[h0 + j] = dch
            cwacc_ref[h0 + j] = jnp.concatenate([cwacc, zpad], axis=0)

        @pl.when(step == nsb - 1)
        def _():
            for j in range(hp):
                dcw_ref[h0 + j] = heads[j][2]

            @pl.when(grp == ngrp - 1)
            def _():
                dsc_ref[...] = jnp.concatenate([pltpu.roll(sa0, HD - HEADS, axis=1), pltpu.roll(sa1, HD - HEADS, axis=1),
                                                nacc, jnp.zeros((5, HD), f32)], axis=0)

    wide = hp * HD
    rblk = lambda i, g: nsb - 1 - i
    grp_specs = lambda off: _seq_specs(sb, wide, rblk, lambda i, g: off // hp + g)
    (qsp, qhalo), (ksp, khalo), (vsp, vhalo) = grp_specs(2 * HEADS), grp_specs(3 * HEADS), grp_specs(4 * HEADS)
    zsp, _ = grp_specs(5 * HEADS)
    hsp = lambda off: pl.BlockSpec((sb, wide), lambda i, g: (rblk(i, g), off // hp + g))
    basp = lambda col: pl.BlockSpec((sb, HD), lambda i, g: (rblk(i, g), col))
    cws = lambda off: pl.BlockSpec((4, wide), lambda i, g: (0, off // hp + g))
    whole = lambda shape: pl.BlockSpec(shape, lambda i, g: (0,) * len(shape))
    return pl.pallas_call(
        body, name=name, grid=(nsb, ngrp),
        in_specs=[hsp(HEADS), qsp, ksp, vsp, zsp, basp(BLK_BA), hsp(0),
                  pl.BlockSpec((hp, nc, HD, HD), lambda i, g: (g, rblk(i, g), 0, 0)),
                  pl.BlockSpec((hp, nc, CHUNK, CHUNK), lambda i, g: (g, rblk(i, g), 0, 0)),
                  pl.BlockSpec((sb, 3 * wide), lambda i, g: (rblk(i, g), g)),
                  cws(0), cws(HEADS), cws(2 * HEADS), whole((8, HD))],
        out_specs=[hsp(0), hsp(0), hsp(0), hsp(0), basp(0), whole((HEADS, 4, 3 * HD)), whole((8, HD))],
        out_shape=[jax.ShapeDtypeStruct((s, D_MODEL), bf16)] * 4
        + [jax.ShapeDtypeStruct((s, HD), f32), jax.ShapeDtypeStruct((HEADS, 4, 3 * HD), f32),
           jax.ShapeDtypeStruct((8, HD), f32)],
        scratch_shapes=[pltpu.VMEM((HEADS, HD, HD), f32), pltpu.VMEM((HEADS, 8, 3 * HD), f32),
                        pltpu.VMEM((HEADS, 8, 3 * HD), f32), pltpu.VMEM((8, HD), f32)],
        compiler_params=_params(("arbitrary", "arbitrary")),
    )(dy, proj, proj, proj, proj, proj, o, states, tinvs, cpres, cw, cw, cw, dnv)


ANY = pl.BlockSpec(memory_space=pl.ANY)
CHIP_MASKS = ((1, 0, 0), (0, 1, 0), (1, 1, 0))
ALL_MASKS = tuple((m >> 2 & 1, m >> 1 & 1, m & 1) for m in range(1, N_DEV))


def _me():
    return lax.axis_index("x"), lax.axis_index("y"), lax.axis_index("c")


def _flip(me, mask):
    return tuple((1 - v) if m else v for v, m in zip(me, mask))


def _dev_index(dev):
    return 4 * dev[0] + 2 * dev[1] + dev[2]


def _chip(dev):
    return 2 * dev[0] + dev[1]


def _comm_call(name, body, arrays, out_shapes, nsem, nloc=0):
    scratch = [pltpu.SemaphoreType.DMA((nsem,)), pltpu.SemaphoreType.DMA((nsem,))]
    if nloc:
        scratch.append(pltpu.SemaphoreType.DMA((nloc,)))
    return pl.pallas_call(
        body, name=name, in_specs=[ANY] * len(arrays), out_specs=[ANY] * len(out_shapes), out_shape=out_shapes,
        scratch_shapes=scratch, compiler_params=pltpu.CompilerParams(has_side_effects=True),
    )(*arrays)


def _put_own(gathered, own, slot):
    return lax.dynamic_update_index_in_dim(gathered, own, slot, 0)


def _gather_halves(arrays, whole_arrays, name):
    na, nw = len(arrays), len(whole_arrays)
    nm = len(CHIP_MASKS)

    def body(*refs):
        srcs, dsts = refs[:na + nw], refs[na + nw:2 * (na + nw)]
        send_sems, recv_sems = refs[2 * (na + nw):]
        me = _me()
        sib = _flip(me, (0, 0, 1))

        def half(a, ref, core):
            rows = arrays[a].shape[0] // 2
            return ref.at[pl.ds(pl.multiple_of(core * rows, rows), rows)]

        def over_ici(a, mi, sender, to):
            if a < na:
                src, dst = half(a, srcs[a], sender[2]), half(a, dsts[a].at[_chip(sender)], sender[2])
            else:
                src, dst = srcs[a], dsts[a].at[_chip(sender)]
            return pltpu.make_async_remote_copy(src_ref=src, dst_ref=dst, send_sem=send_sems.at[a * nm + mi],
                                                recv_sem=recv_sems.at[a * nm + mi], device_id=to, device_id_type=MESH)

        def over_d2d(a, mi, origin, sender, to):
            k = (na + nw) * nm + a * nm + mi
            blk = half(a, dsts[a].at[_chip(origin)], sender[2])
            return pltpu.make_async_remote_copy(src_ref=blk, dst_ref=blk, send_sem=send_sems.at[k],
                                                recv_sem=recv_sems.at[k], device_id=to, device_id_type=MESH)

        sends = []
        for a in range(na + nw):
            for mi, mask in enumerate(CHIP_MASKS):
                cp = over_ici(a, mi, me, _flip(me, mask))
                cp.start()
                sends.append(cp)
        for a in range(na + nw):
            for mi, mask in enumerate(CHIP_MASKS):
                peer = _flip(me, mask)
                over_ici(a, mi, peer, me).wait_recv()
                if a < na:
                    cp = over_d2d(a, mi, peer, me, sib)
                    cp.start()
                    sends.append(cp)
        for a in range(na):
            for mi, mask in enumerate(CHIP_MASKS):
                over_d2d(a, mi, _flip(sib, mask), sib, me).wait_recv()
        for cp in sends:
            cp.wait_send()

    every = list(arrays) + list(whole_arrays)
    got = _comm_call(name, body, every, [jax.ShapeDtypeStruct((4,) + t.shape, t.dtype) for t in every],
                     (2 * na + nw) * nm)
    chip = 2 * lax.axis_index("x") + lax.axis_index("y")
    return [_put_own(g, t, chip) for g, t in zip(got, every)]


HBM = pl.BlockSpec(memory_space=pltpu.HBM)
SEM = pl.BlockSpec(memory_space=pltpu.SEMAPHORE)
DATAFLOW = pltpu.SideEffectType.DATAFLOW_SIDE_EFFECTING


def _split_start(name, srcs, land_shapes, nsem, issue, after=None):
    ns, nl = len(srcs), len(land_shapes)
    n_in = ns + nl + (after is not None)

    def body(*refs):
        issue(refs[:ns], refs[ns:ns + nl], refs[n_in + ns + nl], refs[n_in + ns + nl + 1])
        token = refs[-1]
        token[...] = jnp.zeros_like(token)

    lands = [lax.empty(t.shape, t.dtype) for t in land_shapes]
    thru = [pltpu.HBM(t.shape, t.dtype) for t in list(srcs) + list(land_shapes)]
    hbm = lambda t: pltpu.with_memory_space_constraint(t, pltpu.HBM)
    return pl.pallas_call(
        body, name=name, in_specs=[HBM] * (ns + nl) + [ANY] * (after is not None),
        out_shape=(*thru, pltpu.SemaphoreType.DMA((nsem,)), pltpu.SemaphoreType.DMA((nsem,)),
                   jax.ShapeDtypeStruct((8, HD), f32)),
        out_specs=(*[HBM] * (ns + nl), SEM, SEM, pl.BlockSpec(memory_space=pltpu.VMEM)),
        input_output_aliases={i: i for i in range(ns + nl)},
        compiler_params=pltpu.CompilerParams(has_side_effects=DATAFLOW),
    )(*[hbm(t) for t in srcs], *[hbm(t) for t in lands], *([after] if after is not None else []))


def _split_wait(name, started, ns, after, finish):
    thru, send_sems, recv_sems = started[:-3], started[-3], started[-2]
    nt = len(thru)

    def body(*refs):
        finish(refs[:ns], refs[ns:nt], refs[nt], refs[nt + 1])

    outs = pl.pallas_call(
        body, name=name, in_specs=[HBM] * nt + [SEM, SEM, ANY],
        out_shape=tuple(pltpu.HBM(t.shape, t.dtype) for t in thru), out_specs=tuple([HBM] * nt),
        input_output_aliases={i: i for i in range(nt)},
        compiler_params=pltpu.CompilerParams(has_side_effects=DATAFLOW),
    )(*thru, send_sems, recv_sems, after)
    return list(outs[ns:])


def _gather_start(arrays, name, after=None):
    nm = len(CHIP_MASKS)

    def issue(srcs, lands, send_sems, recv_sems):
        me = _me()
        for a in range(len(arrays)):
            for mi, mask in enumerate(CHIP_MASKS):
                pltpu.make_async_remote_copy(
                    src_ref=srcs[a], dst_ref=lands[a].at[_chip(me)], send_sem=send_sems.at[a * nm + mi],
                    recv_sem=recv_sems.at[a * nm + mi], device_id=_flip(me, mask), device_id_type=MESH).start()

    shapes = [jax.ShapeDtypeStruct((4,) + t.shape, t.dtype) for t in arrays]
    return _split_start(name, arrays, shapes, len(arrays) * nm, issue, after)


def _gather_finish(started, arrays, after, name):
    nm = len(CHIP_MASKS)

    def finish(srcs, lands, send_sems, recv_sems):
        me = _me()
        for a in range(len(arrays)):
            for mi, mask in enumerate(CHIP_MASKS):
                peer = _flip(me, mask)
                cp = pltpu.make_async_remote_copy(
                    src_ref=srcs[a], dst_ref=lands[a].at[_chip(peer)], send_sem=send_sems.at[a * nm + mi],
                    recv_sem=recv_sems.at[a * nm + mi], device_id=peer, device_id_type=MESH)
                cp.wait_send()
                cp.wait_recv()

    got = _split_wait(name, started, len(arrays), after, finish)
    chip = 2 * lax.axis_index("x") + lax.axis_index("y")
    return [_put_own(g, t, chip) for g, t in zip(got, arrays)]


HALF_IN, HALF_OUT = D_MODEL // 2, D_MIX // 8


def _scatter_piece(kind, ref, d):
    j, r = d >> 1, d & 1
    if kind == "win":
        return ref.at[pl.ds(HALF_IN * r, HALF_IN), pl.ds(WIN_STRIDE * j, WIN_W)]
    if kind == "wout":
        return ref.at[pl.ds(2 * HALF_OUT * j + HALF_OUT * r, HALF_OUT)]
    return ref.at[d]


def _scatter_start(arrays, kinds, name):
    na = len(arrays)

    def issue(srcs, lands, send_sems, recv_sems):
        me = _me()
        my = _dev_index(me)
        for d in range(N_DEV):
            dev = (d >> 2, (d >> 1) & 1, d & 1)
            for a in range(na):
                @pl.when(my != d)
                def _(a=a, d=d, dev=dev):
                    pltpu.make_async_remote_copy(
                        src_ref=_scatter_piece(kinds[a], srcs[a], d), dst_ref=lands[a].at[my],
                        send_sem=send_sems.at[a * N_DEV + d], recv_sem=recv_sems.at[a * N_DEV + my],
                        device_id=dev, device_id_type=MESH).start()

    shape_of = {"win": (N_DEV, HALF_IN, WIN_W), "wout": (N_DEV, HALF_OUT, D_MODEL)}
    shapes = [jax.ShapeDtypeStruct(shape_of.get(k, t.shape), t.dtype) for t, k in zip(arrays, kinds)]
    return _split_start(name, arrays, shapes, na * N_DEV, issue)


def _scatter_finish(started, arrays, kinds, after, name):
    na = len(arrays)

    def finish(srcs, lands, send_sems, recv_sems):
        me = _me()
        my = _dev_index(me)
        for s in range(N_DEV):
            for a in range(na):
                @pl.when(my != s)
                def _(a=a, s=s):
                    cp = pltpu.make_async_remote_copy(
                        src_ref=_scatter_piece(kinds[a], srcs[a], s), dst_ref=lands[a].at[s],
                        send_sem=send_sems.at[a * N_DEV + s], recv_sem=recv_sems.at[a * N_DEV + s],
                        device_id=me, device_id_type=MESH)
                    cp.wait_recv()
                    cp.wait_send()

    got = _split_wait(name, started, na, after, finish)
    x, y, c = _me()
    chip, my = 2 * x + y, 4 * x + 2 * y + c
    own = {"win": lambda t: lax.dynamic_slice(t, (HALF_IN * c, WIN_STRIDE * chip), (HALF_IN, WIN_W)),
           "wout": lambda t: lax.dynamic_slice(t, (2 * HALF_OUT * chip + HALF_OUT * c, 0), (HALF_OUT, D_MODEL)),
           "small": lambda t: lax.dynamic_index_in_dim(t, my, 0, keepdims=False)}
    return [_put_own(g, own[k](t), my) for g, t, k in zip(got, arrays, kinds)]


def _share_reduced(pair_arrays, small, name):
    arrays = list(pair_arrays) + ([small] if small is not None else [])
    na, npair = len(arrays), len(pair_arrays)
    nm = len(ALL_MASKS)

    def body(*refs):
        srcs, dsts = refs[:na], refs[na:2 * na]
        send_sems, recv_sems = refs[2 * na:]
        me = _me()
        sib = _flip(me, (0, 0, 1))

        def copy(a, k, sender, to):
            slot = sender[2] if a < npair else _dev_index(sender)
            return pltpu.make_async_remote_copy(
                src_ref=srcs[a], dst_ref=dsts[a].at[slot], send_sem=send_sems.at[k], recv_sem=recv_sems.at[k],
                device_id=to, device_id_type=MESH)

        sends = [copy(a, a, me, sib) for a in range(npair)]
        if small is not None:
            sends += [copy(npair, npair + mi, me, _flip(me, mask)) for mi, mask in enumerate(ALL_MASKS)]
        for cp in sends:
            cp.start()
        for a in range(npair):
            copy(a, a, sib, me).wait_recv()
        if small is not None:
            for mi, mask in enumerate(ALL_MASKS):
                copy(npair, npair + mi, _flip(me, mask), me).wait_recv()
        for cp in sends:
            cp.wait_send()

    shapes = [jax.ShapeDtypeStruct((2,) + t.shape, t.dtype) for t in pair_arrays]
    if small is not None:
        shapes.append(jax.ShapeDtypeStruct((N_DEV,) + small.shape, small.dtype))
    got = _comm_call(name, body, arrays, shapes, npair + (nm if small is not None else 0))
    x, y, c = _me()
    slots = [c] * npair + [4 * x + 2 * y + c]
    return [_put_own(g, t, slot) for g, t, slot in zip(got, arrays, slots)]


def _sum_parts(parts, name):
    _, r, c = parts.shape
    tr = r
    while tr * c * 4 * N_DEV > (8 << 20) and tr % 32 == 0:
        tr //= 2

    def body(p_ref, o_ref):
        total = p_ref[0].astype(f32)
        for d in range(1, N_DEV):
            total = total + p_ref[d].astype(f32)
        o_ref[...] = total

    return pl.pallas_call(
        body, name=name, grid=(r // tr,), in_specs=[pl.BlockSpec((N_DEV, tr, c), lambda i: (0, i, 0))],
        out_specs=pl.BlockSpec((tr, c), lambda i: (i, 0)), out_shape=jax.ShapeDtypeStruct((r, c), f32),
        compiler_params=_params(("parallel",)),
    )(parts)


def _adamw(w, g, m, v, name):
    shape = w.shape
    cols = shape[-1]
    slabs = w.ndim == 3 and shape[1] < 8
    rows = w.size // cols
    tr = rows
    while tr * cols * 4 > (1 << 20) and tr % 16 == 0:
        tr //= 2
    c1 = 1.0 / (1.0 - ADAM_B1 ** ADAM_STEP)
    c2 = 1.0 / (1.0 - ADAM_B2 ** ADAM_STEP)

    def body(w_ref, g_ref, m_ref, v_ref, d_ref, nm_ref, nv_ref, *g_out):
        gv = g_ref[...]
        mv = ADAM_B1 * m_ref[...] + (1.0 - ADAM_B1) * gv
        vv = ADAM_B2 * v_ref[...] + (1.0 - ADAM_B2) * (gv * gv)
        nm_ref[...] = mv
        nv_ref[...] = vv
        d_ref[...] = -ADAM_LR * ((mv * c1) / (jnp.sqrt(vv * c2) + ADAM_EPS) + ADAM_WD * w_ref[...])
        for ref in g_out:
            ref[...] = gv

    if slabs:
        tl = max(d for d in range(1, shape[0] + 1) if shape[0] % d == 0 and d * shape[1] * cols * 4 <= (3 << 19))
        spec = pl.BlockSpec((tl, shape[1], cols), lambda i: (i, 0, 0))
        return tuple(pl.pallas_call(
            body, name=name, grid=(shape[0] // tl,), in_specs=[spec] * 4, out_specs=[spec] * 4,
            out_shape=[jax.ShapeDtypeStruct(shape, f32)] * 4, compiler_params=_params(("parallel",)),
        )(w, g, m, v))
    spec = pl.BlockSpec((tr, cols), lambda i: (i, 0))
    outs = pl.pallas_call(
        body, name=name, grid=(rows // tr,), in_specs=[spec] * 4, out_specs=[spec] * 3,
        out_shape=[jax.ShapeDtypeStruct((rows, cols), f32)] * 3, compiler_params=_params(("parallel",)),
    )(*[t.reshape(rows, cols) for t in (w, g, m, v)])
    return tuple(t.reshape(shape) for t in outs)


def _pad_lanes(t, n=HD):
    return jnp.pad(t, [(0, 0)] * (t.ndim - 1) + [(0, n - t.shape[-1])])


def _rows128(t, rows=None):
    t = t.reshape(-1, HD)
    rows = rows or -(-t.shape[0] // 8) * 8
    return jnp.pad(t, ((0, rows - t.shape[0]), (0, 0)))


def kernel(x, norm_w, w_in, lru_conv_w, lru_conv_b, lru_wa, lru_ba, lru_wx, lru_bx, lru_lambda, lru_norm_w, dn_conv_w, dn_A_log, dn_dt_bias, dn_norm_w, w_out, final_norm_w, loss_target, m_norm_w, m_w_in, m_lru_conv_w, m_lru_conv_b, m_lru_wa, m_lru_ba, m_lru_wx, m_lru_bx, m_lru_lambda, m_lru_norm_w, m_dn_conv_w, m_dn_A_log, m_dn_dt_bias, m_dn_norm_w, m_w_out, m_final_norm_w, v_norm_w, v_w_in, v_lru_conv_w, v_lru_conv_b, v_lru_wa, v_lru_ba, v_lru_wx, v_lru_bx, v_lru_lambda, v_lru_norm_w, v_dn_conv_w, v_dn_A_log, v_dn_dt_bias, v_dn_norm_w, v_w_out, v_final_norm_w):
    depth = norm_w.shape[0]
    xs = x[0]
    target = loss_target[0]
    chip = 2 * lax.axis_index("x") + lax.axis_index("y")

    win_b, wout_b = w_in.astype(bf16), w_out.astype(bf16)
    got_in0 = _gather_halves([win_b[0]], [], "gather_weights0")[0]
    rest0 = [wout_b[0], lru_conv_w, dn_conv_w]
    rest0_started = _gather_start(rest0, "gather_rest0_start")
    gathered, later = {}, {}
    pad_cols = jnp.zeros((D_MODEL, D_INP - D_IN), bf16)

    def weights_of(l, after):
        if l == 0:
            got, token = got_in0, rest0_started[-1]
        else:
            gathered[l] = _gather_finish(later[l], [win_b[l], wout_b[l]], after, f"gather_weights{l}_wait")
            got, token = gathered[l][0], None
        return jnp.concatenate([got[j] for j in range(4)] + [pad_cols], axis=1), token

    def rest_of(l, after):
        token = None
        if "rest0" not in gathered:
            gathered["rest0"] = _gather_finish(rest0_started, rest0, after, "gather_rest0_wait")
            for k in range(1, depth):
                later[k] = _gather_start([win_b[k], wout_b[k]], f"gather_weights{k}_start", after=gathered["rest0"][0])
                token = later[k][-1] if token is None else token + later[k][-1]
        g_out, g_lcw, g_dcw = gathered["rest0"]
        if l > 0:
            g_out = gathered[l][1]
        lcw_full = g_lcw.transpose(1, 2, 0, 3).reshape(depth, 4, D_MODEL)
        dcw_full = g_dcw.transpose(1, 2, 0, 3).reshape(depth, 4, 3 * D_MODEL)
        return g_out.reshape(D_MIX, D_MODEL), _behind(lcw_full[l], token), dcw_full[l]

    in_flight = {}

    def on_grad(kind, l, g):
        if l == depth - 1 and depth > 1:
            if kind == "wout":
                in_flight["held"] = g
                return None
            srcs, kinds, key = [g, in_flight.pop("held")], ["win", "wout"], ("both", l)
        else:
            srcs, kinds, key = [g], [kind], (kind, l)
        started = _scatter_start(srcs, kinds, f"scatter_{key[0]}{key[1]}_start")
        in_flight[key] = (started, srcs, kinds)
        return started[-1]

    vecs, dnvs = [], []
    zrow = jnp.zeros((D_MODEL,), f32)
    for l in range(depth):
        vecs.append(jnp.stack([lru_conv_b[l], lru_ba[l], lru_bx[l], lru_lambda[l], lru_norm_w[l], zrow, zrow, zrow]))
        dnvs.append(jnp.concatenate([_pad_lanes(dn_A_log[l][None]), _pad_lanes(dn_dt_bias[l][None]),
                                     dn_norm_w[l][None], jnp.zeros((5, HD), f32)], axis=0))

    loss_part, grad_x, d_fnw, g_nw, g_vec, g_wa, g_wx, g_lcw_, g_dcw_, g_dsc = _local_step(
        xs, target, norm_w, final_norm_w, weights_of, rest_of, on_grad, vecs, dnvs, lru_wa, lru_wx)
    loss = lax.psum(loss_part[0, 0], ("x", "y", "c"))
    grads = _reduce_grads(chip, grad_x, in_flight, d_fnw, g_nw, g_vec, g_wa, g_wx, g_lcw_, g_dcw_, g_dsc)

    names = ["norm_w", "w_in", "lru_conv_w", "lru_conv_b", "lru_wa", "lru_ba", "lru_wx", "lru_bx", "lru_lambda",
             "lru_norm_w", "dn_conv_w", "dn_A_log", "dn_dt_bias", "dn_norm_w", "w_out", "final_norm_w"]
    ws = dict(zip(names, [norm_w, w_in, lru_conv_w, lru_conv_b, lru_wa, lru_ba, lru_wx, lru_bx, lru_lambda, lru_norm_w,
                          dn_conv_w, dn_A_log, dn_dt_bias, dn_norm_w, w_out, final_norm_w]))
    ms = dict(zip(names, [m_norm_w, m_w_in, m_lru_conv_w, m_lru_conv_b, m_lru_wa, m_lru_ba, m_lru_wx, m_lru_bx,
                          m_lru_lambda, m_lru_norm_w, m_dn_conv_w, m_dn_A_log, m_dn_dt_bias, m_dn_norm_w, m_w_out,
                          m_final_norm_w]))
    vs = dict(zip(names, [v_norm_w, v_w_in, v_lru_conv_w, v_lru_conv_b, v_lru_wa, v_lru_ba, v_lru_wx, v_lru_bx,
                          v_lru_lambda, v_lru_norm_w, v_dn_conv_w, v_dn_A_log, v_dn_dt_bias, v_dn_norm_w, v_w_out,
                          v_final_norm_w]))
    to_cols = lambda t: jnp.transpose(t, (2, 0, 1))
    from_cols = lambda t: jnp.transpose(t, (1, 2, 0))
    deltas, new_m, new_v = [], [], []
    for n in names:
        if n in ("w_in", "lru_conv_w", "dn_conv_w"):
            view, back = (to_cols, from_cols) if n == "w_in" else (lambda t: t, lambda t: t)
            d, nm_, nv_, grads[n] = (back(t) for t in _adamw(view(ws[n]), view(grads[n]), view(ms[n]), view(vs[n]),
                                                            f"adamw_{n}"))
        else:
            d, nm_, nv_ = _adamw(ws[n], grads[n], ms[n], vs[n], f"adamw_{n}")
        deltas.append(d)
        new_m.append(nm_)
        new_v.append(nv_)
    return (loss, grad_x[None], *[grads[n] for n in names], *deltas, *new_m, *new_v)


def _behind(t, token):
    return t if token is None else t + token[0:1, 0:1]


def _local_step(xs, target, norm_w, final_norm_w, weights_of, rest_of, on_grad, vecs, dnvs, lru_wa, lru_wx):
    depth = norm_w.shape[0]
    saved = []
    h = xs
    for l in range(depth):
        wp_l, token = weights_of(l, h)
        hn = _rms_fwd(h, _behind(norm_w[l][None], token), f"rms_fwd{l}")
        proj = _matmul(hn, wp_l, tm=2048, tn=896, tk=D_MODEL, name=f"in_proj{l}")
        wo_l, lcw_l, dcw_l = rest_of(l, proj)
        y, hl = _lru_fwd(proj, lcw_l, vecs[l], lru_wa[l], lru_wx[l], f"lru_fwd{l}")
        y, o, states, tinvs, cpres = _dn_fwd(proj, dcw_l, dnvs[l], y, f"dn_fwd{l}")
        out = _matmul(y, wo_l, tm=1024, tn=D_MODEL, tk=D_MIX, res=h, name=f"out_proj{l}")
        saved.append((h, hn, proj, hl, o, states, tinvs, cpres, y, wp_l, wo_l, lcw_l, dcw_l))
        h = out

    dh, d_fnw, loss_part = _final_loss(h, final_norm_w[None], target, "final_loss")

    g_nw, g_vec, g_wa, g_wx, g_lcw_, g_dcw_, g_dsc = ([None] * depth for _ in range(7))
    for l in reversed(range(depth)):
        hin, hn, proj, hl, o, states, tinvs, cpres, y, wp_l, wo_l, lcw_l, dcw_l = saved[l]
        dy = _matmul(dh, wo_l, tb=True, tm=2048, tn=1024, tk=D_MODEL, name=f"d_y{l}")
        g_wout = _matmul(y, dh, ta=True, tm=1024, tn=D_MODEL, tk=512, out_dtype=bf16, name=f"d_wout{l}")
        token = on_grad("wout", l, g_wout)
        dlx, dlz, g_lcw_[l], g_vec[l], g_wa[l], g_wx[l] = _lru_bwd(dy, proj, hl, lcw_l, _behind(vecs[l], token),
                                                                 lru_wa[l], lru_wx[l], f"lru_bwd{l}")
        dq, dk, dv, dz, dba, dcw8, g_dsc[l] = _dn_bwd(dy, proj, o, states, tinvs, cpres, dcw_l, dnvs[l], f"dn_bwd{l}")
        g_dcw_[l] = dcw8.reshape(HEADS, 4, 3, HD).transpose(1, 2, 0, 3).reshape(4, 3 * D_MODEL)
        dxs = [dlx, dlz, dq, dk, dv, dz]
        g_win = _d_win(_transpose(hn, f"hn_t{l}"), dxs, dba, f"d_win{l}")
        token = on_grad("win", l, g_win)
        dhn = _d_hn(dxs, _behind(dba, token), wp_l, f"d_hn{l}")
        dh, g_nw[l] = _rms_bwd(dhn, hin, norm_w[l][None], dh, f"rms_bwd{l}")
    return loss_part, dh, d_fnw, g_nw, g_vec, g_wa, g_wx, g_lcw_, g_dcw_, g_dsc


def _reduce_grads(chip, after, in_flight, d_fnw, g_nw, g_vec, g_wa, g_wx, g_lcw_, g_dcw_, g_dsc):
    depth = len(g_nw)
    both = lambda f: jnp.concatenate([f(l) for l in range(depth)])
    small = [
        both(lambda l: _rows128(g_nw[l])),
        both(lambda l: _rows128(g_vec[l][0])), both(lambda l: _rows128(g_vec[l][1])),
        both(lambda l: _rows128(g_vec[l][2])), both(lambda l: _rows128(g_vec[l][3])),
        both(lambda l: _rows128(g_vec[l][4])),
        both(lambda l: _rows128(g_wa[l])), both(lambda l: _rows128(g_wx[l])),
        both(lambda l: _rows128(g_dsc[l][0:1])), both(lambda l: _rows128(g_dsc[l][1:2])),
        both(lambda l: _rows128(g_dsc[l][2:3])),
        _rows128(d_fnw),
        both(lambda l: _rows128(g_lcw_[l])), both(lambda l: _rows128(g_dcw_[l])),
    ]
    counts = [t.shape[0] for t in small]
    total = sum(counts)
    per = -(-total // (8 * N_DEV)) * 8
    small = jnp.concatenate(small + [jnp.zeros((per * N_DEV - total, HD), f32)]).reshape(N_DEV, per, HD)

    small_started = _scatter_start([small], ["small"], "scatter_small_start")
    reduced = {}
    for key in sorted(in_flight, key=lambda k: -k[1]):
        started, srcs, kinds = in_flight[key]
        got = _scatter_finish(started, srcs, kinds, after, f"scatter_{key[0]}{key[1]}_wait")
        for kind, part in zip(kinds, got):
            reduced[(kind, key[1])] = _sum_parts(part, f"sum_{kind}{key[1]}")
    got = _scatter_finish(small_started, [small], ["small"], after, "scatter_small_wait")
    pairs = [None] * depth
    a_small = None
    for l in reversed(range(depth)):
        sums = [reduced[("win", l)], reduced[("wout", l)]]
        if l == 0:
            shared = _share_reduced(sums, _sum_parts(got[0], "sum_small"), f"share_grads{l}")
            a_small = shared[2].reshape(N_DEV * per, HD)
        else:
            shared = _share_reduced(sums, None, f"share_grads{l}")
        pairs[l] = shared
    grad_w_in = jnp.stack([lax.dynamic_slice_in_dim(pairs[l][0].reshape(D_MODEL, WIN_W), 4 * chip, SHARD_IN, 1)
                           for l in range(depth)])
    grad_w_out = jnp.stack([pairs[l][1].reshape(D_MIX // 4, D_MODEL) for l in range(depth)])

    offs = [0]
    for c in counts:
        offs.append(offs[-1] + c)

    def piece(k, rows_per_layer=None):
        t = a_small[offs[k]:offs[k + 1]]
        if rows_per_layer is None:
            return t
        return t.reshape(depth, -1, HD)[:, :rows_per_layer]

    vec = lambda k: piece(k).reshape(depth, D_MODEL)
    return {
        "norm_w": vec(0), "lru_conv_b": vec(1), "lru_ba": vec(2), "lru_bx": vec(3), "lru_lambda": vec(4),
        "lru_norm_w": vec(5),
        "lru_wa": piece(6).reshape(depth, HEADS, HD, HD), "lru_wx": piece(7).reshape(depth, HEADS, HD, HD),
        "dn_A_log": piece(8, 1)[:, 0, :HEADS], "dn_dt_bias": piece(9, 1)[:, 0, :HEADS], "dn_norm_w": piece(10, 1)[:, 0],
        "final_norm_w": piece(11).reshape(D_MODEL),
        "lru_conv_w": lax.dynamic_slice_in_dim(piece(12).reshape(depth, 4, D_MODEL), chip * (D_MODEL // 4), D_MODEL // 4, 2),
        "dn_conv_w": lax.dynamic_slice_in_dim(piece(13).reshape(depth, 4, 3 * D_MODEL), chip * (3 * D_MODEL // 4),
                                              3 * D_MODEL // 4, 2),
        "w_in": grad_w_in, "w_out": grad_w_out,
    }
```

```python
import jax
import jax.numpy as jnp
from jax import lax
from jax.experimental import pallas as pl
from jax.experimental.pallas import tpu as pltpu

f32 = jnp.float32
bf16 = jnp.bfloat16
MESH = pl.DeviceIdType.MESH

D_MODEL = 1024
HEADS = 8
HD = 128
CHUNK = 64
LRU_T = 128
SEQ_BLOCK = 1024
LRU_HP = 4
DN_SEQ_BLOCK = 256
DN_HP = 8
LRU_C = 8.0
D_IN = 6160
D_INP = 6272
D_MIX = 2048
N_GROUPS = 6
BLK_BA = 48
SHARD_IN = D_IN // 4
WIN_W = 1664
WIN_STRIDE = 1536
EPS = 1e-6
QSCALE = HD ** -0.5
N_DEV = 8
VMEM_LIMIT = 56 * 1024 * 1024

ADAM_LR, ADAM_B1, ADAM_B2, ADAM_EPS, ADAM_WD, ADAM_STEP = 0.001, 0.9, 0.999, 1e-08, 0.01, 10


def _sig(x):
    return 1.0 / (1.0 + jnp.exp(-x))


def _log1p(u):
    w = 1.0 + u
    d = w - 1.0
    return jnp.where(d == 0.0, u, jnp.log(w) * (u / jnp.where(d == 0.0, 1.0, d)))


def _softplus(x):
    return jnp.maximum(x, 0.0) + _log1p(jnp.exp(-jnp.abs(x)))


def _expm1(x):
    t = x * (1.0 + x * (0.5 + x * (1.0 / 6 + x * (1.0 / 24 + x * (1.0 / 120 + x * (1.0 / 720))))))
    return jnp.where(jnp.abs(x) < 0.2, t, jnp.exp(x) - 1.0)


def _dot(a, b, prec=None):
    return jnp.dot(a, b, precision=prec, preferred_element_type=f32)


def _dot_nt(a, b, prec=None):
    return lax.dot_general(a, b, (((1,), (1,)), ((), ())), precision=prec, preferred_element_type=f32)


def _dot_tn(a, b, prec=None):
    return lax.dot_general(a, b, (((0,), (0,)), ((), ())), precision=prec, preferred_element_type=f32)


def _b(x):
    return x.astype(bf16)


def _sum0(x):
    return jnp.sum(x, axis=0, keepdims=True)


def _sum1(x):
    return jnp.sum(x, axis=1, keepdims=True)


def _rowmean(x):
    return jnp.mean(x, axis=-1, keepdims=True)


def _dsilu(z, sg):
    return sg * (1.0 + z * (1.0 - sg))


def _conv_taps(x, halo):
    xx = jnp.concatenate([halo, x], axis=0)
    return [pltpu.roll(xx, 3, axis=0)[8:], pltpu.roll(xx, 2, axis=0)[8:], pltpu.roll(xx, 1, axis=0)[8:], x]


def _conv(xs, cw):
    return cw[0:1] * xs[0] + cw[1:2] * xs[1] + cw[2:3] * xs[2] + cw[3:4] * xs[3]


def _rows_before(ref, halo_ref, lanes, t0, c, sblk):
    tprev = pl.multiple_of(jnp.maximum(t0 - 8, 0), 8)
    return jnp.where(c > 0, ref[pl.ds(tprev, 8), lanes], jnp.where(sblk > 0, halo_ref[:, lanes], 0.0))


def _conv_back(dxc, halo_after, cw, x=None):
    rows = dxc.shape[0]
    dd = jnp.concatenate([dxc, halo_after], axis=0)
    out = cw[3:4] * dxc
    dcw = [None, None, None, None if x is None else _sum0(x * dxc)]
    for s in (1, 2, 3):
        shifted = pltpu.roll(dd, rows + 8 - s, axis=0)[:rows]
        out = out + cw[3 - s:4 - s] * shifted
        if x is not None:
            dcw[3 - s] = _sum0(x * shifted)
    return out if x is None else (out, jnp.concatenate(dcw, axis=0))


def _params(sem=None):
    return pltpu.CompilerParams(dimension_semantics=sem, vmem_limit_bytes=VMEM_LIMIT)


def _seq_specs(sb, width, rowblk, colblk):
    main = pl.BlockSpec((sb, width), lambda a, b: (rowblk(a, b), colblk(a, b)))
    halo = pl.BlockSpec((8, width), lambda a, b: (jnp.maximum(rowblk(a, b) * (sb // 8) - 1, 0), colblk(a, b)))
    return main, halo


def _matmul(a, b, *, ta=False, tb=False, tm, tn, tk, res=None, out_dtype=f32, name):
    if ta:
        kdim, m = a.shape
    else:
        m, kdim = a.shape
    n = b.shape[0] if tb else b.shape[1]
    tm, tn, tk = min(tm, m), min(tn, n), min(tk, kdim)
    assert m % tm == 0 and n % tn == 0 and kdim % tk == 0, (name, m, n, kdim, tm, tn, tk)
    nk = kdim // tk
    dims = (((0 if ta else 1,), (1 if tb else 0,)), ((), ()))
    has_res = res is not None

    def body(*refs):
        a_ref, b_ref = refs[:2]
        r_ref = refs[2] if has_res else None
        o_ref = refs[3] if has_res else refs[2]
        acc_ref = refs[-1] if nk > 1 else None
        part = lax.dot_general(_b(a_ref[...]), _b(b_ref[...]), dims, preferred_element_type=f32)

        def finish(total):
            if has_res:
                total = total + r_ref[...]
            o_ref[...] = total.astype(out_dtype)

        if nk == 1:
            finish(part)
        else:
            k = pl.program_id(2)

            @pl.when(k == 0)
            def _():
                acc_ref[...] = part

            @pl.when(k > 0)
            def _():
                acc_ref[...] += part

            @pl.when(k == nk - 1)
            def _():
                finish(acc_ref[...])

    a_spec = pl.BlockSpec((tk, tm), lambda i, j, k: (k, i)) if ta else pl.BlockSpec((tm, tk), lambda i, j, k: (i, k))
    b_spec = pl.BlockSpec((tn, tk), lambda i, j, k: (j, k)) if tb else pl.BlockSpec((tk, tn), lambda i, j, k: (k, j))
    o_spec = pl.BlockSpec((tm, tn), lambda i, j, k: (i, j))
    in_specs, args = [a_spec, b_spec], [a, b]
    if has_res:
        in_specs.append(o_spec)
        args.append(res)
    return pl.pallas_call(
        body, name=name, grid=(m // tm, n // tn, nk), in_specs=in_specs, out_specs=o_spec,
        out_shape=jax.ShapeDtypeStruct((m, n), out_dtype),
        scratch_shapes=[pltpu.VMEM((tm, tn), f32)] if nk > 1 else [],
        compiler_params=_params(("parallel", "parallel", "arbitrary")),
    )(*args)


def _transpose(x, name):
    r, c = x.shape
    tr, tc = min(512, r), min(512, c)

    def body(x_ref, o_ref):
        o_ref[...] = x_ref[...].T

    return pl.pallas_call(
        body, name=name, grid=(r // tr, c // tc), in_specs=[pl.BlockSpec((tr, tc), lambda i, j: (i, j))],
        out_specs=pl.BlockSpec((tc, tr), lambda i, j: (j, i)), out_shape=jax.ShapeDtypeStruct((c, r), x.dtype),
        compiler_params=_params(("parallel", "parallel")),
    )(x)


def _d_hn(dxs, dba, wp, name):
    s = dxs[0].shape[0]
    tm = min(512, s)

    def body(*refs):
        dx_refs = refs[:N_GROUPS]
        dba_ref, w_ref, o_ref = refs[N_GROUPS:]
        total = _dot_nt(_b(dba_ref[...]), w_ref[:, BLK_BA * HD:])
        for g in range(N_GROUPS):
            total = total + _dot_nt(dx_refs[g][...], w_ref[:, g * D_MODEL:(g + 1) * D_MODEL])
        o_ref[...] = total

    row = lambda w: pl.BlockSpec((tm, w), lambda i: (i, 0))
    return pl.pallas_call(
        body, name=name, grid=(s // tm,),
        in_specs=[row(D_MODEL)] * N_GROUPS + [row(HD), pl.BlockSpec((D_MODEL, D_INP), lambda i: (0, 0))],
        out_specs=row(D_MODEL), out_shape=jax.ShapeDtypeStruct((s, D_MODEL), f32),
        compiler_params=_params(("parallel",)),
    )(*dxs, dba, wp)


def _d_win(hnt, dxs, dba, name):
    s = hnt.shape[1]
    tn = 256
    per = D_MODEL // tn

    def body(*refs):
        hnt_ref = refs[0]
        dx_refs = refs[1:1 + N_GROUPS]
        o_ref = refs[1 + N_GROUPS]
        j = pl.program_id(0)
        for g in range(N_GROUPS):
            @pl.when(j // per == g)
            def _(g=g):
                o_ref[...] = _dot(hnt_ref[...], _b(dx_refs[g][...])).astype(bf16)

    def dx_spec(g):
        on = lambda j: (j // per == g).astype(jnp.int32)
        return pl.BlockSpec((s, tn), lambda j: (0, (j % per) * on(j)))

    main = pl.pallas_call(
        body, name=name, grid=(N_GROUPS * per,),
        in_specs=[pl.BlockSpec((D_MODEL, s), lambda j: (0, 0))] + [dx_spec(g) for g in range(N_GROUPS)],
        out_specs=pl.BlockSpec((D_MODEL, tn), lambda j: (0, j)),
        out_shape=jax.ShapeDtypeStruct((D_MODEL, D_INP), bf16),
        compiler_params=_params(("parallel",)),
    )(hnt, *dxs)

    def tail(hnt_ref, dba_ref, full_ref, o_ref):
        del full_ref
        o_ref[...] = _dot(hnt_ref[...], _b(dba_ref[...])).astype(bf16)

    return pl.pallas_call(
        tail, name=name + "_ba", grid=(1,),
        in_specs=[pl.BlockSpec((D_MODEL, s), lambda k: (0, 0)), pl.BlockSpec((s, HD), lambda k: (0, 0)),
                  pl.BlockSpec(memory_space=pl.ANY)],
        out_specs=pl.BlockSpec((D_MODEL, HD), lambda k: (0, BLK_BA)),
        out_shape=jax.ShapeDtypeStruct((D_MODEL, D_INP), bf16),
        input_output_aliases={2: 0},
        compiler_params=_params(("arbitrary",)),
    )(hnt, dba, main)


def _rms_fwd(x, w, name):
    s = x.shape[0]
    tr = min(512, s)

    def body(x_ref, w_ref, h_ref):
        xv = x_ref[...]
        r = lax.rsqrt(_rowmean(xv * xv) + EPS)
        h_ref[...] = (xv * r * w_ref[...]).astype(bf16)

    return pl.pallas_call(
        body, name=name, grid=(s // tr,),
        in_specs=[pl.BlockSpec((tr, D_MODEL), lambda i: (i, 0)), pl.BlockSpec((1, D_MODEL), lambda i: (0, 0))],
        out_specs=pl.BlockSpec((tr, D_MODEL), lambda i: (i, 0)),
        out_shape=jax.ShapeDtypeStruct((s, D_MODEL), bf16), compiler_params=_params(("parallel",)),
    )(x, w)


def _rms_bwd(dh, x, w, dres, name):
    s = x.shape[0]
    tr = min(512, s)

    def body(dh_ref, x_ref, w_ref, dres_ref, dx_ref, dw_ref):
        xv = x_ref[...]
        r = lax.rsqrt(_rowmean(xv * xv) + EPS)
        xn = xv * r
        d = dh_ref[...]
        dxn = d * w_ref[...]
        dx_ref[...] = dres_ref[...] + r * (dxn - xn * _rowmean(dxn * xn))
        part = _sum0(d * xn)

        @pl.when(pl.program_id(0) == 0)
        def _():
            dw_ref[...] = part

        @pl.when(pl.program_id(0) > 0)
        def _():
            dw_ref[...] += part

    row = pl.BlockSpec((tr, D_MODEL), lambda i: (i, 0))
    vec = pl.BlockSpec((1, D_MODEL), lambda i: (0, 0))
    return pl.pallas_call(
        body, name=name, grid=(s // tr,), in_specs=[row, row, vec, row], out_specs=[row, vec],
        out_shape=[jax.ShapeDtypeStruct((s, D_MODEL), f32), jax.ShapeDtypeStruct((1, D_MODEL), f32)],
        compiler_params=_params(("arbitrary",)),
    )(dh, x, w, dres)


def _final_loss(x, w, target, name):
    s = x.shape[0]
    tr = min(512, s)

    def body(x_ref, w_ref, t_ref, dx_ref, dw_ref, loss_ref):
        xv = x_ref[...]
        wv = w_ref[...]
        r = lax.rsqrt(_rowmean(xv * xv) + EPS)
        xn = xv * r
        e = xn * wv - t_ref[...]
        lb = jnp.broadcast_to(0.5 * _sum0(_rowmean(e * e)), (1, HD))
        dy = e * (1.0 / D_MODEL)
        dxn = dy * wv
        dx_ref[...] = r * (dxn - xn * _rowmean(dxn * xn))
        part = _sum0(dy * xn)

        @pl.when(pl.program_id(0) == 0)
        def _():
            dw_ref[...] = part
            loss_ref[...] = lb

        @pl.when(pl.program_id(0) > 0)
        def _():
            dw_ref[...] += part
            loss_ref[...] += lb

    row = pl.BlockSpec((tr, D_MODEL), lambda i: (i, 0))
    vec = pl.BlockSpec((1, D_MODEL), lambda i: (0, 0))
    lsp = pl.BlockSpec((1, HD), lambda i: (0, 0))
    return pl.pallas_call(
        body, name=name, grid=(s // tr,), in_specs=[row, vec, row], out_specs=[row, vec, lsp],
        out_shape=[jax.ShapeDtypeStruct((s, D_MODEL), f32), jax.ShapeDtypeStruct((1, D_MODEL), f32),
                   jax.ShapeDtypeStruct((1, HD), f32)],
        compiler_params=_params(("arbitrary",)),
    )(x, w, target)


def _lru_gates(xc, vec, wa, wx):
    sp = _softplus(-vec[3:4])
    xcb = _b(xc)
    r = _sig(_dot(xcb, wa) + vec[1:2])
    i = _sig(_dot(xcb, wx) + vec[2:3])
    la = (-LRU_C) * r * sp
    a = jnp.exp(la)
    mult = jnp.sqrt(-_expm1(2.0 * la))
    return r, i, a, mult, sp, xcb


def _lru_fwd(proj, cw, vec, wa, wx, name):
    s = proj.shape[0]
    sb = min(SEQ_BLOCK, s)
    nsb = s // sb
    t = min(LRU_T, sb)
    nc = sb // t
    hp = LRU_HP
    wide = hp * HD
    lanes = [slice(j * HD, (j + 1) * HD) for j in range(hp)]

    def body(x_ref, xh_ref, z_ref, cw_ref, vec_ref, wa_ref, wx_ref, y_ref, hl_ref, hc_ref):
        sblk = pl.program_id(1)
        consts = [(cw_ref[:, hl], vec_ref[:, hl], _b(wa_ref[j]), _b(wx_ref[j])) for j, hl in enumerate(lanes)]
        row = lax.broadcasted_iota(jnp.int32, (t, HD), 0)

        @pl.when(sblk == 0)
        def _():
            hc_ref[...] = jnp.zeros((8, wide), f32)

        def one_head(j, x, halo, z, hcarry):
            cwv, vec_v, wa_v, wx_v = consts[j]
            xs = _conv_taps(x, halo)
            xc = vec_v[0:1] + _conv(xs, cwv)
            r, i, a, mult, _, _ = _lru_gates(xc, vec_v, wa_v, wx_v)
            yield
            bb = mult * (i * xc)
            d = 1
            while d < t:
                m = row >= d
                bb = jnp.where(m, a * pltpu.roll(bb, d, axis=0) + bb, bb)
                a = jnp.where(m, a * pltpu.roll(a, d, axis=0), a)
                d *= 2
                yield
            hl = bb + a * hcarry
            nrm = lax.rsqrt(_rowmean(hl * hl) + EPS)
            return hl, (hl * nrm * vec_v[4:5] * (z * _sig(z))).astype(bf16)

        def chunk(c, carries):
            t0 = pl.multiple_of(c * t, t)
            rows = pl.ds(t0, t)
            ins = [(x_ref[rows, hl], _rows_before(x_ref, xh_ref, hl, t0, c, sblk), z_ref[rows, hl]) for hl in lanes]
            outs = _round_robin([one_head(j, *ins[j], carries[j]) for j in range(hp)])
            for j, hl in enumerate(lanes):
                hl_ref[rows, hl] = outs[j][0]
                y_ref[rows, hl] = outs[j][1]
            return tuple(out[0][t - 1:t, :] for out in outs)

        final = lax.fori_loop(0, nc, chunk, tuple(hc_ref[0:1, hl] for hl in lanes))
        hc_ref[...] = jnp.concatenate([jnp.broadcast_to(f, (8, HD)) for f in final], axis=1)

    xsp, xhalo = _seq_specs(sb, wide, lambda g, i: i, lambda g, i: g)
    zsp, _ = _seq_specs(sb, wide, lambda g, i: i, lambda g, i: HEADS // hp + g)
    gcol = lambda g, i: (0, g)
    wsp = pl.BlockSpec((hp, HD, HD), lambda g, i: (g, 0, 0))
    osp = pl.BlockSpec((sb, wide), lambda g, i: (i, g))
    return pl.pallas_call(
        body, name=name, grid=(HEADS // hp, nsb),
        in_specs=[xsp, xhalo, zsp, pl.BlockSpec((4, wide), gcol), pl.BlockSpec((8, wide), gcol), wsp, wsp],
        out_specs=[osp, osp],
        out_shape=[jax.ShapeDtypeStruct((s, D_MIX), bf16), jax.ShapeDtypeStruct((s, D_MODEL), f32)],
        scratch_shapes=[pltpu.VMEM((8, wide), f32)],
        compiler_params=_params(("parallel", "arbitrary")),
    )(proj, proj, proj, cw, vec, wa, wx)


def _lru_bwd(dy, proj, hl, cw, vec, wa, wx, name):
    s = proj.shape[0]
    sb = min(SEQ_BLOCK, s)
    nsb = s // sb
    t = min(LRU_T, sb)
    nc = sb // t
    hp = LRU_HP
    wide = hp * HD
    lanes = [slice(j * HD, (j + 1) * HD) for j in range(hp)]
    n_acc = 10

    def body(dy_ref, x_ref, xh_ref, z_ref, hl_ref, hlh_ref, cw_ref, vec_ref, wa_ref, wx_ref,
             dx_ref, dz_ref, dcw_ref, dvec_ref, dwa_ref, dwx_ref, acc_ref, dxh_ref):
        step = pl.program_id(1)
        sblk = nsb - 1 - step
        consts = [(cw_ref[:, hl], vec_ref[:, hl], _b(wa_ref[j]), _b(wx_ref[j])) for j, hl in enumerate(lanes)]
        row = lax.broadcasted_iota(jnp.int32, (t, HD), 0)

        @pl.when(step == 0)
        def _():
            acc_ref[...] = jnp.zeros((16, wide), f32)
            dxh_ref[...] = jnp.zeros((8, wide), f32)
            dwa_ref[...] = jnp.zeros((hp, HD, HD), f32)
            dwx_ref[...] = jnp.zeros((hp, HD, HD), f32)

        def one_head(j, x, halo, z, hv, hhalo, dyv, carry):
            dcw0, dcw1, dcw2, dcw3, dcb, dba, dbx, dsp, dlnw, mu, dxc_halo = carry
            cwv, vec_v, wa_v, wx_v = consts[j]
            lnw = vec_v[4:5]
            xs = _conv_taps(x, halo)
            xc = vec_v[0:1] + _conv(xs, cwv)
            r, i, a, mult, sp, xcb = _lru_gates(xc, vec_v, wa_v, wx_v)
            yield
            hprev = pltpu.roll(jnp.concatenate([hhalo, hv], axis=0), 1, axis=0)[8:]
            sgz = _sig(z)
            sz = z * sgz
            nrm = lax.rsqrt(_rowmean(hv * hv) + EPS)
            hn = hv * nrm
            dlnw = dlnw + _sum0(dyv * hn * sz)
            dzv = _b(dyv * hn * lnw * _dsilu(z, sgz))
            dhn = dyv * lnw * sz
            lam = nrm * (dhn - hn * _rowmean(dhn * hn))
            cf = jnp.where(row == t - 1, 1.0, pltpu.roll(a, t - 1, axis=0))
            d = 1
            while d < t:
                m = row < t - d
                lam = jnp.where(m, lam + cf * pltpu.roll(lam, t - d, axis=0), lam)
                cf = jnp.where(m, cf * pltpu.roll(cf, t - d, axis=0), cf)
                d *= 2
                yield
            lam = lam + cf * mu
            mu = a[0:1] * lam[0:1]
            da = lam * hprev
            dmult = lam * (i * xc)
            di = lam * mult * xc
            dxc = lam * mult * i
            dla = da * a - dmult * (a * a) / mult
            dr = dla * (-LRU_C * sp)
            dsp = dsp + _sum0(dla * (-LRU_C * r))
            dra = dr * r * (1.0 - r)
            dia = di * i * (1.0 - i)
            dba = dba + _sum0(dra)
            dbx = dbx + _sum0(dia)
            drab, diab = _b(dra), _b(dia)
            dwa = _dot_tn(xcb, drab)
            dwx = _dot_tn(xcb, diab)
            dxc = dxc + _dot_nt(drab, wa_v) + _dot_nt(diab, wx_v)
            yield
            dcb = dcb + _sum0(dxc)
            dcw0 = dcw0 + _sum0(dxc * xs[0])
            dcw1 = dcw1 + _sum0(dxc * xs[1])
            dcw2 = dcw2 + _sum0(dxc * xs[2])
            dcw3 = dcw3 + _sum0(dxc * xs[3])
            dxv = _b(_conv_back(dxc, dxc_halo, cwv))
            return dxv, dzv, dwa, dwx, (dcw0, dcw1, dcw2, dcw3, dcb, dba, dbx, dsp, dlnw, mu, dxc[0:8])

        def chunk(ci, carries):
            c = nc - 1 - ci
            t0 = pl.multiple_of(c * t, t)
            rows = pl.ds(t0, t)
            ins = [(x_ref[rows, hl], _rows_before(x_ref, xh_ref, hl, t0, c, sblk), z_ref[rows, hl], hl_ref[rows, hl],
                    _rows_before(hl_ref, hlh_ref, hl, t0, c, sblk), dy_ref[rows, hl]) for hl in lanes]
            outs = _round_robin([one_head(j, *ins[j], carries[j]) for j in range(hp)])
            for j, hl in enumerate(lanes):
                dx_ref[rows, hl] = outs[j][0]
                dz_ref[rows, hl] = outs[j][1]
                dwa_ref[j] += outs[j][2]
                dwx_ref[j] += outs[j][3]
            return tuple(out[4] for out in outs)

        acc = acc_ref[...]
        dxh = dxh_ref[...]
        init = tuple(tuple(acc[k:k + 1, hl] for k in range(n_acc)) + (dxh[:, hl],) for hl in lanes)
        final = lax.fori_loop(0, nc, chunk, init)
        rows_out = [jnp.concatenate([final[j][k] for j in range(hp)], axis=1) for k in range(n_acc)]
        zero = jnp.zeros((1, wide), f32)
        acc_ref[...] = jnp.concatenate(rows_out + [zero] * (16 - n_acc), axis=0)
        dxh_ref[...] = jnp.concatenate([final[j][n_acc] for j in range(hp)], axis=1)

        @pl.when(step == nsb - 1)
        def _():
            dcw_ref[...] = jnp.concatenate(rows_out[0:4], axis=0)
            dlam = -rows_out[7] * _sig(-vec_ref[3:4, :])
            dvec_ref[...] = jnp.concatenate([rows_out[4], rows_out[5], rows_out[6], dlam, rows_out[8], zero, zero, zero],
                                            axis=0)

    rblk = lambda g, i: nsb - 1 - i
    xsp, xhalo = _seq_specs(sb, wide, rblk, lambda g, i: g)
    zsp, _ = _seq_specs(sb, wide, rblk, lambda g, i: HEADS // hp + g)
    gcol = lambda g, i: (0, g)
    wsp = pl.BlockSpec((hp, HD, HD), lambda g, i: (g, 0, 0))
    return pl.pallas_call(
        body, name=name, grid=(HEADS // hp, nsb),
        in_specs=[xsp, xsp, xhalo, zsp, xsp, xhalo, pl.BlockSpec((4, wide), gcol), pl.BlockSpec((8, wide), gcol), wsp, wsp],
        out_specs=[xsp, xsp, pl.BlockSpec((4, wide), gcol), pl.BlockSpec((8, wide), gcol), wsp, wsp],
        out_shape=[jax.ShapeDtypeStruct((s, D_MODEL), bf16), jax.ShapeDtypeStruct((s, D_MODEL), bf16),
                   jax.ShapeDtypeStruct((4, D_MODEL), f32), jax.ShapeDtypeStruct((8, D_MODEL), f32),
                   jax.ShapeDtypeStruct((HEADS, HD, HD), f32), jax.ShapeDtypeStruct((HEADS, HD, HD), f32)],
        scratch_shapes=[pltpu.VMEM((16, wide), f32), pltpu.VMEM((8, wide), f32)],
        compiler_params=_params(("parallel", "arbitrary")),
    )(dy, proj, proj, proj, hl, hl, cw, vec, wa, wx)


def _lane_pick(x, lane, idx):
    return _sum1(jnp.where(lane == idx, x, 0.0))


def _round_robin(gens):
    results = [None] * len(gens)
    live = list(range(len(gens)))
    while live:
        for i in list(live):
            try:
                next(gens[i])
            except StopIteration as done:
                results[i] = done.value
                live.remove(i)
    return results


def _sdot(a, b):
    return _dot(_b(a), _b(b))


def _sdot_tn(a, b):
    return _dot_tn(_b(a), _b(b))


def _sdot_nt(a, b):
    return _dot_nt(_b(a), _b(b))


def _tri_dot(tri, x):
    trib = tri.astype(bf16)
    x1 = x.astype(bf16)
    r1 = x - x1.astype(f32)
    x2 = r1.astype(bf16)
    x3 = (r1 - x2.astype(f32)).astype(bf16)
    return _dot(trib, x1) + _dot(trib, x2) + _dot(trib, x3)


def _inv_unit_lower(a):
    n = -a
    p = _sdot(a, a)
    yield
    for k in range(5):
        n = n + p + _sdot(n, p)
        if k < 4:
            p = _sdot(p, p)
        yield
    return n


def _dn_gates(bav, dnv, masks):
    ltri = masks[4]
    ea = jnp.exp(pltpu.roll(dnv[0:1], HEADS, axis=1))
    dt = pltpu.roll(dnv[1:2], HEADS, axis=1)
    u = bav + dt
    g = -ea * _softplus(u)
    return dict(beta=_sig(bav), g=g, gc=_tri_dot(ltri, g), sga=_sig(u), ea=ea)


def _dn_chunk(x3, halo3, gates, cw3, h, masks, cpre=None, tinv=None):
    lane, ri, ci, eye, ltri = masks
    if cpre is None:
        cpre = _conv(_conv_taps(x3, halo3), cw3)
    sg = _sig(cpre)
    act = cpre * sg
    q_raw, k_raw, v = act[:, 0:HD], act[:, HD:2 * HD], act[:, 2 * HD:3 * HD]
    rq = lax.rsqrt(_sum1(q_raw * q_raw) + EPS)
    rk = lax.rsqrt(_sum1(k_raw * k_raw) + EPS)
    qn = q_raw * rq
    k = k_raw * rk
    q = qn * QSCALE
    beta = jnp.broadcast_to(_lane_pick(gates["beta"], lane, h), (CHUNK, HD))
    gc = jnp.broadcast_to(_lane_pick(gates["gc"], lane, HEADS + h), (CHUNK, HD))
    yield
    gc64 = gc[:, 0:CHUNK]
    grow = _sum0(gc64 * eye)
    dm = jnp.exp(jnp.where(ri >= ci, gc64 - grow, -1e30))
    ds_ = jnp.where(ri > ci, dm, 0.0)
    eg = jnp.exp(gc)
    glast = gc[CHUNK - 1:CHUNK, :]
    ek = jnp.exp(glast - gc)
    kb = k * beta
    kbf = _b(k)
    amat = _dot_nt(_b(kb), kbf) * ds_
    pmat = _dot_nt(_b(q), kbf) * dm
    yield
    if tinv is None:
        tinv = yield from _inv_unit_lower(amat)
    return dict(cpre=cpre, sg=sg, rq=rq, rk=rk, qn=qn, q=q, k=k, v=v, kbf=kbf, beta=beta, gc=gc, dm=dm, ds=ds_, eg=eg,
                glast=glast, ek=ek, kb=kb, amat=amat, tinv=tinv, pmat=pmat)


def _dn_masks():
    lane = lax.broadcasted_iota(jnp.int32, (CHUNK, HD), 1)
    ri = lax.broadcasted_iota(jnp.int32, (CHUNK, CHUNK), 0)
    ci = lax.broadcasted_iota(jnp.int32, (CHUNK, CHUNK), 1)
    eye = (ri == ci).astype(f32)
    ltri = (ri >= ci).astype(f32)
    return lane, ri, ci, eye, ltri


def _dn_load_qkv(q_ref, qh_ref, k_ref, kh_ref, v_ref, vh_ref, hl, rows, t0, c, sblk):
    x3 = jnp.concatenate([q_ref[rows, hl], k_ref[rows, hl], v_ref[rows, hl]], axis=1)
    halo3 = jnp.concatenate([_rows_before(q_ref, qh_ref, hl, t0, c, sblk), _rows_before(k_ref, kh_ref, hl, t0, c, sblk),
                             _rows_before(v_ref, vh_ref, hl, t0, c, sblk)], axis=1)
    return x3, halo3


def _dn_cw3(cwq_ref, cwk_ref, cwv_ref, j):
    return jnp.concatenate([r[:, j * HD:(j + 1) * HD] for r in (cwq_ref, cwk_ref, cwv_ref)], axis=1)


def _dn_fwd(proj, cw, dnv, y, name):
    s = proj.shape[0]
    sb = min(DN_SEQ_BLOCK, s)
    nsb = s // sb
    nc = sb // CHUNK
    hp = DN_HP

    def body(q_ref, qh_ref, k_ref, kh_ref, v_ref, vh_ref, z_ref, ba_ref, cwq_ref, cwk_ref, cwv_ref, dnv_ref, yin_ref,
             y_ref, o_ref, st_ref, ti_ref, cp_ref, s_ref):
        del yin_ref
        grp = pl.program_id(0)
        sblk = pl.program_id(1)
        masks = _dn_masks()
        dnv_v = dnv_ref[...]
        cw3 = [_dn_cw3(cwq_ref, cwk_ref, cwv_ref, j) for j in range(hp)]

        @pl.when(sblk == 0)
        def _():
            s_ref[...] = jnp.zeros((hp, HD, HD), f32)

        def one_head(j, x3, halo3, gates, z, st):
            f = yield from _dn_chunk(x3, halo3, gates, cw3[j], grp * hp + j, masks)
            sbf = _b(st)
            qs = _dot(_b(f["q"] * f["eg"]), sbf)
            rhs = f["beta"] * (f["v"] - _dot(_b(f["k"] * f["eg"]), sbf))
            yield
            vn = rhs + _sdot(f["tinv"], rhs)
            yield
            vnb = _b(vn)
            o = qs + _dot(_b(f["pmat"]), vnb)
            st_new = st * jnp.exp(f["glast"]) + _dot_tn(_b(f["k"] * f["ek"]), vnb)
            yield
            nrm = lax.rsqrt(_rowmean(o * o) + EPS)
            yv = (o * nrm * dnv_v[2:3] * (z * _sig(z))).astype(bf16)
            return o, yv, st_new, f["tinv"], f["cpre"]

        def chunk(c, states):
            t0 = pl.multiple_of(c * CHUNK, CHUNK)
            rows = pl.ds(t0, CHUNK)
            bav = ba_ref[rows, :]
            ins = []
            for j in range(hp):
                hl = slice(j * HD, (j + 1) * HD)
                ins.append(_dn_load_qkv(q_ref, qh_ref, k_ref, kh_ref, v_ref, vh_ref, hl, rows, t0, c, sblk)
                           + (z_ref[rows, hl],))
            gates = _dn_gates(bav, dnv_v, masks)
            outs = _round_robin([one_head(j, ins[j][0], ins[j][1], gates, ins[j][2], states[j]) for j in range(hp)])
            for j in range(hp):
                hl = slice(j * HD, (j + 1) * HD)
                st_ref[j, c] = states[j]
                o_ref[rows, hl] = outs[j][0]
                y_ref[rows, hl] = outs[j][1]
                ti_ref[j, c] = outs[j][3]
                cp_ref[rows, j * 3 * HD:(j + 1) * 3 * HD] = outs[j][4]
            return tuple(out[2] for out in outs)

        final = lax.fori_loop(0, nc, chunk, tuple(s_ref[j] for j in range(hp)))
        for j in range(hp):
            s_ref[j] = final[j]

    wide = hp * HD
    grp_specs = lambda off: _seq_specs(sb, wide, lambda g, i: i, lambda g, i: off // hp + g)
    (qsp, qhalo), (ksp, khalo), (vsp, vhalo) = grp_specs(2 * HEADS), grp_specs(3 * HEADS), grp_specs(4 * HEADS)
    zsp, _ = grp_specs(5 * HEADS)
    basp = pl.BlockSpec((sb, HD), lambda g, i: (i, BLK_BA))
    cws = lambda off: pl.BlockSpec((4, wide), lambda g, i: (0, off // hp + g))
    osp = lambda off: pl.BlockSpec((sb, wide), lambda g, i: (i, off // hp + g))
    return pl.pallas_call(
        body, name=name, grid=(HEADS // hp, nsb),
        in_specs=[qsp, qhalo, ksp, khalo, vsp, vhalo, zsp, basp, cws(0), cws(HEADS), cws(2 * HEADS),
                  pl.BlockSpec((8, HD), lambda g, i: (0, 0)), pl.BlockSpec(memory_space=pl.ANY)],
        out_specs=[osp(HEADS), osp(0), pl.BlockSpec((hp, nc, HD, HD), lambda g, i: (g, i, 0, 0)),
                   pl.BlockSpec((hp, nc, CHUNK, CHUNK), lambda g, i: (g, i, 0, 0)),
                   pl.BlockSpec((sb, 3 * wide), lambda g, i: (i, g))],
        out_shape=[jax.ShapeDtypeStruct((s, D_MIX), bf16), jax.ShapeDtypeStruct((s, D_MODEL), f32),
                   jax.ShapeDtypeStruct((HEADS, s // CHUNK, HD, HD), f32),
                   jax.ShapeDtypeStruct((HEADS, s // CHUNK, CHUNK, CHUNK), f32),
                   jax.ShapeDtypeStruct((s, 3 * D_MODEL), f32)],
        scratch_shapes=[pltpu.VMEM((hp, HD, HD), f32)],
        input_output_aliases={12: 0},
        compiler_params=_params(("parallel", "arbitrary")),
    )(proj, proj, proj, proj, proj, proj, proj, proj, cw, cw, cw, dnv, y)


def _dn_bwd(dy, proj, o, states, tinvs, cpres, cw, dnv, name):
    s = proj.shape[0]
    sb = min(DN_SEQ_BLOCK, s)
    nsb = s // sb
    nc = sb // CHUNK
    hp = DN_HP
    ngrp = HEADS // hp

    def body(dy_ref, q_ref, k_ref, v_ref, z_ref, ba_ref, o_ref, st_ref, ti_ref, cp_ref,
             cwq_ref, cwk_ref, cwv_ref, dnv_ref,
             dq_ref, dk_ref, dv_ref, dz_ref, dba_ref, dcw_ref, dsc_ref,
             ds_ref, dch_ref, cwacc_ref, nacc_ref):
        step = pl.program_id(0)
        grp = pl.program_id(1)
        sblk = nsb - 1 - step
        h0 = grp * hp
        masks = _dn_masks()
        lane, ri, ci, eye, ltri = masks
        utri = (ri <= ci).astype(f32)
        cw3s = [_dn_cw3(cwq_ref, cwk_ref, cwv_ref, j) for j in range(hp)]
        dnv_v = dnv_ref[...]
        dnw = dnv_v[2:3]
        row64 = lax.broadcasted_iota(jnp.int32, (CHUNK, HD), 0)

        @pl.when(step == 0)
        def _():
            for j in range(hp):
                ds_ref[h0 + j] = jnp.zeros((HD, HD), f32)
                dch_ref[h0 + j] = jnp.zeros((8, 3 * HD), f32)
                cwacc_ref[h0 + j] = jnp.zeros((8, 3 * HD), f32)

        @pl.when((step == 0) & (grp == 0))
        def _():
            nacc_ref[...] = jnp.zeros((8, HD), f32)

        def one_head(j, x3, gates, z, st, ti, cp, ov, dyv, carry):
            dst, dch, cwacc = carry
            cw3 = cw3s[j]
            f = yield from _dn_chunk(None, None, gates, cw3, h0 + j, masks, cp, ti)
            q, k, v, beta, eg, ek, kbf = f["q"], f["k"], f["v"], f["beta"], f["eg"], f["ek"], f["kbf"]
            sbf = _b(st)
            dstb = _b(dst)
            qe = q * eg
            keg = k * eg
            kd = k * ek
            kegb, qeb, kdb = _b(keg), _b(qe), _b(kd)
            e = v - _dot(kegb, sbf)
            yield
            rhs = beta * e
            vn = rhs + _sdot(f["tinv"], rhs)
            yield
            vnb = _b(vn)
            pb = _b(f["pmat"])
            sgz = _sig(z)
            sz = z * sgz
            nrm = lax.rsqrt(_rowmean(ov * ov) + EPS)
            on = ov * nrm
            ddnw = _sum0(dyv * on * sz)
            dpz = dyv * on * dnw * _dsilu(z, sgz)
            don = dyv * dnw * sz
            do = nrm * (don - on * _rowmean(don * on))
            dob = _b(do)
            dvn = _dot_tn(pb, dob) + _dot(kdb, dstb)
            dpm = jnp.where(ri >= ci, _dot_nt(dob, vnb), 0.0)
            dqe = _dot_nt(dob, sbf)
            dkd = _dot_nt(vnb, dstb)
            qtdo = _dot_tn(qeb, dob)
            yield
            drhs = dvn + _sdot_tn(f["tinv"], dvn)
            dqk = _b(dpm * f["dm"])
            dq = _dot(dqk, kbf) + dqe * eg
            dk_p = _dot_tn(dqk, _b(q))
            yield
            dam = jnp.where(ri > ci, -_sdot_nt(drhs, vn), 0.0)
            de = drhs * beta
            deb = _b(de)
            dbeta = _sum1(drhs * e)
            dkeg = -_dot_nt(deb, sbf)
            dst_new = qtdo + dst * jnp.exp(f["glast"]) - _dot_tn(kegb, deb)
            yield
            dkk = _b(dam * f["ds"])
            dkb = _dot(dkk, kbf)
            dk = dk_p + _dot_tn(dkk, _b(f["kb"])) + dkb * beta + dkeg * eg + dkd * ek
            yield
            dbeta = dbeta + _sum1(dkb * k)
            mm = dpm * f["pmat"] + dam * f["amat"]
            dglast = _sum1(_sum0(dst * st)) * jnp.exp(f["glast"]) + _sum1(_sum0(dkd * kd))
            dgc = (_sum1(mm) - _sum1(eye * _sum0(mm)) + _sum1(dqe * qe) + _sum1(dkeg * keg) - _sum1(dkd * kd))
            dgc = jnp.broadcast_to(dgc, (CHUNK, HD)) + jnp.where(row64 == CHUNK - 1, dglast, 0.0)
            dq_raw = (QSCALE * f["rq"]) * (dq - f["qn"] * _sum1(f["qn"] * dq))
            dk_raw = f["rk"] * (dk - k * _sum1(k * dk))
            dact = jnp.concatenate([dq_raw, dk_raw, de], axis=1)
            dc = dact * _dsilu(f["cpre"], f["sg"])
            dqkv, dcw_part = _conv_back(dc, dch, cw3, x3)
            cwacc = cwacc + dcw_part
            return dpz, dqkv, (dgc, dbeta), ddnw, (dst_new, dc[0:8], cwacc)

        def chunk(cidx, carry):
            heads, nacc, sa0, sa1 = carry
            c = nc - 1 - cidx
            t0 = pl.multiple_of(c * CHUNK, CHUNK)
            rows = pl.ds(t0, CHUNK)
            bav = ba_ref[rows, :]
            ins = []
            for j in range(hp):
                hl = slice(j * HD, (j + 1) * HD)
                x3 = jnp.concatenate([q_ref[rows, hl], k_ref[rows, hl], v_ref[rows, hl]], axis=1)
                ins.append((x3, z_ref[rows, hl], st_ref[j, c], ti_ref[j, c],
                            cp_ref[rows, j * 3 * HD:(j + 1) * 3 * HD], o_ref[rows, hl], dy_ref[rows, hl]))
            gates = _dn_gates(bav, dnv_v, masks)
            outs = _round_robin([one_head(j, ins[j][0], gates, *ins[j][1:], heads[j]) for j in range(hp)])
            dgc_all = jnp.zeros((CHUNK, HD), f32)
            dbeta_all = jnp.zeros((CHUNK, HD), f32)
            for j in range(hp):
                hl = slice(j * HD, (j + 1) * HD)
                dpz, dqkv, (dgc, dbeta), ddnw, _ = outs[j]
                dgc_all = jnp.where(lane == HEADS + h0 + j, dgc, dgc_all)
                dbeta_all = jnp.where(lane == h0 + j, dbeta, dbeta_all)
                dq_ref[rows, hl] = _b(dqkv[:, 0:HD])
                dk_ref[rows, hl] = _b(dqkv[:, HD:2 * HD])
                dv_ref[rows, hl] = _b(dqkv[:, 2 * HD:3 * HD])
                dz_ref[rows, hl] = _b(dpz)
                nacc = nacc + ddnw
            dg_all = _tri_dot(utri, dgc_all)
            dain_all = dg_all * (-gates["ea"]) * gates["sga"]
            dba = dbeta_all * gates["beta"] * (1.0 - gates["beta"]) + dain_all
            sa0 = sa0 + _sum0(dg_all * gates["g"])
            sa1 = sa1 + _sum0(dain_all)

            @pl.when(grp == 0)
            def _():
                dba_ref[rows, :] = dba

            @pl.when(grp > 0)
            def _():
                dba_ref[rows, :] += dba

            return tuple(out[4] for out in outs), nacc, sa0, sa1

        init = tuple((ds_ref[h0 + j], dch_ref[h0 + j], cwacc_ref[h0 + j][0:4]) for j in range(hp))
        heads, nacc, sa0, sa1 = lax.fori_loop(
            0, nc, chunk, (init, nacc_ref[0:1, :], nacc_ref[1:2, :], nacc_ref[2:3, :]))
        nacc_ref[0:3, :] = jnp.concatenate([nacc, sa0, sa1], axis=0)
        zpad = jnp.zeros((4, 3 * HD), f32)
        for j in range(hp):
            dst, dch, cwacc = heads[j]
            ds_ref[h0 + j] = dst
            dch_ref[h0 + j] = dch
            cwacc_ref[h0 + j] = jnp.concatenate([cwacc, zpad], axis=0)

        @pl.when(step == nsb - 1)
        def _():
            for j in range(hp):
                dcw_ref[h0 + j] = heads[j][2]

            @pl.when(grp == ngrp - 1)
            def _():
                dsc_ref[...] = jnp.concatenate([pltpu.roll(sa0, HD - HEADS, axis=1), pltpu.roll(sa1, HD - HEADS, axis=1),
                                                nacc, jnp.zeros((5, HD), f32)], axis=0)

    wide = hp * HD
    rblk = lambda i, g: nsb - 1 - i
    grp_specs = lambda off: _seq_specs(sb, wide, rblk, lambda i, g: off // hp + g)
    (qsp, qhalo), (ksp, khalo), (vsp, vhalo) = grp_specs(2 * HEADS), grp_specs(3 * HEADS), grp_specs(4 * HEADS)
    zsp, _ = grp_specs(5 * HEADS)
    hsp = lambda off: pl.BlockSpec((sb, wide), lambda i, g: (rblk(i, g), off // hp + g))
    basp = lambda col: pl.BlockSpec((sb, HD), lambda i, g: (rblk(i, g), col))
    cws = lambda off: pl.BlockSpec((4, wide), lambda i, g: (0, off // hp + g))
    whole = lambda shape: pl.BlockSpec(shape, lambda i, g: (0,) * len(shape))
    return pl.pallas_call(
        body, name=name, grid=(nsb, ngrp),
        in_specs=[hsp(HEADS), qsp, ksp, vsp, zsp, basp(BLK_BA), hsp(0),
                  pl.BlockSpec((hp, nc, HD, HD), lambda i, g: (g, rblk(i, g), 0, 0)),
                  pl.BlockSpec((hp, nc, CHUNK, CHUNK), lambda i, g: (g, rblk(i, g), 0, 0)),
                  pl.BlockSpec((sb, 3 * wide), lambda i, g: (rblk(i, g), g)),
                  cws(0), cws(HEADS), cws(2 * HEADS), whole((8, HD))],
        out_specs=[hsp(0), hsp(0), hsp(0), hsp(0), basp(0), whole((HEADS, 4, 3 * HD)), whole((8, HD))],
        out_shape=[jax.ShapeDtypeStruct((s, D_MODEL), bf16)] * 4
        + [jax.ShapeDtypeStruct((s, HD), f32), jax.ShapeDtypeStruct((HEADS, 4, 3 * HD), f32),
           jax.ShapeDtypeStruct((8, HD), f32)],
        scratch_shapes=[pltpu.VMEM((HEADS, HD, HD), f32), pltpu.VMEM((HEADS, 8, 3 * HD), f32),
                        pltpu.VMEM((HEADS, 8, 3 * HD), f32), pltpu.VMEM((8, HD), f32)],
        compiler_params=_params(("arbitrary", "arbitrary")),
    )(dy, proj, proj, proj, proj, proj, o, states, tinvs, cpres, cw, cw, cw, dnv)


ANY = pl.BlockSpec(memory_space=pl.ANY)
CHIP_MASKS = ((1, 0, 0), (0, 1, 0), (1, 1, 0))
ALL_MASKS = tuple((m >> 2 & 1, m >> 1 & 1, m & 1) for m in range(1, N_DEV))


def _me():
    return lax.axis_index("x"), lax.axis_index("y"), lax.axis_index("c")


def _flip(me, mask):
    return tuple((1 - v) if m else v for v, m in zip(me, mask))


def _dev_index(dev):
    return 4 * dev[0] + 2 * dev[1] + dev[2]


def _chip(dev):
    return 2 * dev[0] + dev[1]


def _comm_call(name, body, arrays, out_shapes, nsem, nloc=0):
    scratch = [pltpu.SemaphoreType.DMA((nsem,)), pltpu.SemaphoreType.DMA((nsem,))]
    if nloc:
        scratch.append(pltpu.SemaphoreType.DMA((nloc,)))
    return pl.pallas_call(
        body, name=name, in_specs=[ANY] * len(arrays), out_specs=[ANY] * len(out_shapes), out_shape=out_shapes,
        scratch_shapes=scratch, compiler_params=pltpu.CompilerParams(has_side_effects=True),
    )(*arrays)


def _put_own(gathered, own, slot):
    return lax.dynamic_update_index_in_dim(gathered, own, slot, 0)


def _gather_halves(arrays, whole_arrays, name):
    na, nw = len(arrays), len(whole_arrays)
    nm = len(CHIP_MASKS)

    def body(*refs):
        srcs, dsts = refs[:na + nw], refs[na + nw:2 * (na + nw)]
        send_sems, recv_sems = refs[2 * (na + nw):]
        me = _me()
        sib = _flip(me, (0, 0, 1))

        def half(a, ref, core):
            rows = arrays[a].shape[0] // 2
            return ref.at[pl.ds(pl.multiple_of(core * rows, rows), rows)]

        def over_ici(a, mi, sender, to):
            if a < na:
                src, dst = half(a, srcs[a], sender[2]), half(a, dsts[a].at[_chip(sender)], sender[2])
            else:
                src, dst = srcs[a], dsts[a].at[_chip(sender)]
            return pltpu.make_async_remote_copy(src_ref=src, dst_ref=dst, send_sem=send_sems.at[a * nm + mi],
                                                recv_sem=recv_sems.at[a * nm + mi], device_id=to, device_id_type=MESH)

        def over_d2d(a, mi, origin, sender, to):
            k = (na + nw) * nm + a * nm + mi
            blk = half(a, dsts[a].at[_chip(origin)], sender[2])
            return pltpu.make_async_remote_copy(src_ref=blk, dst_ref=blk, send_sem=send_sems.at[k],
                                                recv_sem=recv_sems.at[k], device_id=to, device_id_type=MESH)

        sends = []
        for a in range(na + nw):
            for mi, mask in enumerate(CHIP_MASKS):
                cp = over_ici(a, mi, me, _flip(me, mask))
                cp.start()
                sends.append(cp)
        for a in range(na + nw):
            for mi, mask in enumerate(CHIP_MASKS):
                peer = _flip(me, mask)
                over_ici(a, mi, peer, me).wait_recv()
                if a < na:
                    cp = over_d2d(a, mi, peer, me, sib)
                    cp.start()
                    sends.append(cp)
        for a in range(na):
            for mi, mask in enumerate(CHIP_MASKS):
                over_d2d(a, mi, _flip(sib, mask), sib, me).wait_recv()
        for cp in sends:
            cp.wait_send()

    every = list(arrays) + list(whole_arrays)
    got = _comm_call(name, body, every, [jax.ShapeDtypeStruct((4,) + t.shape, t.dtype) for t in every],
                     (2 * na + nw) * nm)
    chip = 2 * lax.axis_index("x") + lax.axis_index("y")
    return [_put_own(g, t, chip) for g, t in zip(got, every)]


HBM = pl.BlockSpec(memory_space=pltpu.HBM)
SEM = pl.BlockSpec(memory_space=pltpu.SEMAPHORE)
DATAFLOW = pltpu.SideEffectType.DATAFLOW_SIDE_EFFECTING


def _split_start(name, srcs, land_shapes, nsem, issue, after=None):
    ns, nl = len(srcs), len(land_shapes)
    n_in = ns + nl + (after is not None)

    def body(*refs):
        issue(refs[:ns], refs[ns:ns + nl], refs[n_in + ns + nl], refs[n_in + ns + nl + 1])
        token = refs[-1]
        token[...] = jnp.zeros_like(token)

    lands = [lax.empty(t.shape, t.dtype) for t in land_shapes]
    thru = [pltpu.HBM(t.shape, t.dtype) for t in list(srcs) + list(land_shapes)]
    hbm = lambda t: pltpu.with_memory_space_constraint(t, pltpu.HBM)
    return pl.pallas_call(
        body, name=name, in_specs=[HBM] * (ns + nl) + [ANY] * (after is not None),
        out_shape=(*thru, pltpu.SemaphoreType.DMA((nsem,)), pltpu.SemaphoreType.DMA((nsem,)),
                   jax.ShapeDtypeStruct((8, HD), f32)),
        out_specs=(*[HBM] * (ns + nl), SEM, SEM, pl.BlockSpec(memory_space=pltpu.VMEM)),
        input_output_aliases={i: i for i in range(ns + nl)},
        compiler_params=pltpu.CompilerParams(has_side_effects=DATAFLOW),
    )(*[hbm(t) for t in srcs], *[hbm(t) for t in lands], *([after] if after is not None else []))


def _split_wait(name, started, ns, after, finish):
    thru, send_sems, recv_sems = started[:-3], started[-3], started[-2]
    nt = len(thru)

    def body(*refs):
        finish(refs[:ns], refs[ns:nt], refs[nt], refs[nt + 1])

    outs = pl.pallas_call(
        body, name=name, in_specs=[HBM] * nt + [SEM, SEM, ANY],
        out_shape=tuple(pltpu.HBM(t.shape, t.dtype) for t in thru), out_specs=tuple([HBM] * nt),
        input_output_aliases={i: i for i in range(nt)},
        compiler_params=pltpu.CompilerParams(has_side_effects=DATAFLOW),
    )(*thru, send_sems, recv_sems, after)
    return list(outs[ns:])


def _gather_start(arrays, name, after=None):
    nm = len(CHIP_MASKS)

    def issue(srcs, lands, send_sems, recv_sems):
        me = _me()
        for a in range(len(arrays)):
            for mi, mask in enumerate(CHIP_MASKS):
                pltpu.make_async_remote_copy(
                    src_ref=srcs[a], dst_ref=lands[a].at[_chip(me)], send_sem=send_sems.at[a * nm + mi],
                    recv_sem=recv_sems.at[a * nm + mi], device_id=_flip(me, mask), device_id_type=MESH).start()

    shapes = [jax.ShapeDtypeStruct((4,) + t.shape, t.dtype) for t in arrays]
    return _split_start(name, arrays, shapes, len(arrays) * nm, issue, after)


def _gather_finish(started, arrays, after, name):
    nm = len(CHIP_MASKS)

    def finish(srcs, lands, send_sems, recv_sems):
        me = _me()
        for a in range(len(arrays)):
            for mi, mask in enumerate(CHIP_MASKS):
                peer = _flip(me, mask)
                cp = pltpu.make_async_remote_copy(
                    src_ref=srcs[a], dst_ref=lands[a].at[_chip(peer)], send_sem=send_sems.at[a * nm + mi],
                    recv_sem=recv_sems.at[a * nm + mi], device_id=peer, device_id_type=MESH)
                cp.wait_send()
                cp.wait_recv()

    got = _split_wait(name, started, len(arrays), after, finish)
    chip = 2 * lax.axis_index("x") + lax.axis_index("y")
    return [_put_own(g, t, chip) for g, t in zip(got, arrays)]


HALF_IN, HALF_OUT = D_MODEL // 2, D_MIX // 8


def _scatter_piece(kind, ref, d):
    j, r = d >> 1, d & 1
    if kind == "win":
        return ref.at[pl.ds(HALF_IN * r, HALF_IN), pl.ds(WIN_STRIDE * j, WIN_W)]
    if kind == "wout":
        return ref.at[pl.ds(2 * HALF_OUT * j + HALF_OUT * r, HALF_OUT)]
    return ref.at[d]


def _scatter_start(arrays, kinds, name):
    na = len(arrays)

    def issue(srcs, lands, send_sems, recv_sems):
        me = _me()
        my = _dev_index(me)
        for d in range(N_DEV):
            dev = (d >> 2, (d >> 1) & 1, d & 1)
            for a in range(na):
                @pl.when(my != d)
                def _(a=a, d=d, dev=dev):
                    pltpu.make_async_remote_copy(
                        src_ref=_scatter_piece(kinds[a], srcs[a], d), dst_ref=lands[a].at[my],
                        send_sem=send_sems.at[a * N_DEV + d], recv_sem=recv_sems.at[a * N_DEV + my],
                        device_id=dev, device_id_type=MESH).start()

    shape_of = {"win": (N_DEV, HALF_IN, WIN_W), "wout": (N_DEV, HALF_OUT, D_MODEL)}
    shapes = [jax.ShapeDtypeStruct(shape_of.get(k, t.shape), t.dtype) for t, k in zip(arrays, kinds)]
    return _split_start(name, arrays, shapes, na * N_DEV, issue)


def _scatter_finish(started, arrays, kinds, after, name):
    na = len(arrays)

    def finish(srcs, lands, send_sems, recv_sems):
        me = _me()
        my = _dev_index(me)
        for s in range(N_DEV):
            for a in range(na):
                @pl.when(my != s)
                def _(a=a, s=s):
                    cp = pltpu.make_async_remote_copy(
                        src_ref=_scatter_piece(kinds[a], srcs[a], s), dst_ref=lands[a].at[s],
                        send_sem=send_sems.at[a * N_DEV + s], recv_sem=recv_sems.at[a * N_DEV + s],
                        device_id=me, device_id_type=MESH)
                    cp.wait_recv()
                    cp.wait_send()

    got = _split_wait(name, started, na, after, finish)
    x, y, c = _me()
    chip, my = 2 * x + y, 4 * x + 2 * y + c
    own = {"win": lambda t: lax.dynamic_slice(t, (HALF_IN * c, WIN_STRIDE * chip), (HALF_IN, WIN_W)),
           "wout": lambda t: lax.dynamic_slice(t, (2 * HALF_OUT * chip + HALF_OUT * c, 0), (HALF_OUT, D_MODEL)),
           "small": lambda t: lax.dynamic_index_in_dim(t, my, 0, keepdims=False)}
    return [_put_own(g, own[k](t), my) for g, t, k in zip(got, arrays, kinds)]


def _share_reduced(pair_arrays, small, name):
    arrays = list(pair_arrays) + ([small] if small is not None else [])
    na, npair = len(arrays), len(pair_arrays)
    nm = len(ALL_MASKS)

    def body(*refs):
        srcs, dsts = refs[:na], refs[na:2 * na]
        send_sems, recv_sems = refs[2 * na:]
        me = _me()
        sib = _flip(me, (0, 0, 1))

        def copy(a, k, sender, to):
            slot = sender[2] if a < npair else _dev_index(sender)
            return pltpu.make_async_remote_copy(
                src_ref=srcs[a], dst_ref=dsts[a].at[slot], send_sem=send_sems.at[k], recv_sem=recv_sems.at[k],
                device_id=to, device_id_type=MESH)

        sends = [copy(a, a, me, sib) for a in range(npair)]
        if small is not None:
            sends += [copy(npair, npair + mi, me, _flip(me, mask)) for mi, mask in enumerate(ALL_MASKS)]
        for cp in sends:
            cp.start()
        for a in range(npair):
            copy(a, a, sib, me).wait_recv()
        if small is not None:
            for mi, mask in enumerate(ALL_MASKS):
                copy(npair, npair + mi, _flip(me, mask), me).wait_recv()
        for cp in sends:
            cp.wait_send()

    shapes = [jax.ShapeDtypeStruct((2,) + t.shape, t.dtype) for t in pair_arrays]
    if small is not None:
        shapes.append(jax.ShapeDtypeStruct((N_DEV,) + small.shape, small.dtype))
    got = _comm_call(name, body, arrays, shapes, npair + (nm if small is not None else 0))
    x, y, c = _me()
    slots = [c] * npair + [4 * x + 2 * y + c]
    return [_put_own(g, t, slot) for g, t, slot in zip(got, arrays, slots)]


def _sum_parts(parts, name):
    _, r, c = parts.shape
    tr = r
    while tr * c * 4 * N_DEV > (8 << 20) and tr % 32 == 0:
        tr //= 2

    def body(p_ref, o_ref):
        total = p_ref[0].astype(f32)
        for d in range(1, N_DEV):
            total = total + p_ref[d].astype(f32)
        o_ref[...] = total

    return pl.pallas_call(
        body, name=name, grid=(r // tr,), in_specs=[pl.BlockSpec((N_DEV, tr, c), lambda i: (0, i, 0))],
        out_specs=pl.BlockSpec((tr, c), lambda i: (i, 0)), out_shape=jax.ShapeDtypeStruct((r, c), f32),
        compiler_params=_params(("parallel",)),
    )(parts)


def _adamw(w, g, m, v, name):
    shape = w.shape
    cols = shape[-1]
    slabs = w.ndim == 3 and shape[1] < 8
    rows = w.size // cols
    tr = rows
    while tr * cols * 4 > (1 << 20) and tr % 16 == 0:
        tr //= 2
    c1 = 1.0 / (1.0 - ADAM_B1 ** ADAM_STEP)
    c2 = 1.0 / (1.0 - ADAM_B2 ** ADAM_STEP)

    def body(w_ref, g_ref, m_ref, v_ref, d_ref, nm_ref, nv_ref, *g_out):
        gv = g_ref[...]
        mv = ADAM_B1 * m_ref[...] + (1.0 - ADAM_B1) * gv
        vv = ADAM_B2 * v_ref[...] + (1.0 - ADAM_B2) * (gv * gv)
        nm_ref[...] = mv
        nv_ref[...] = vv
        d_ref[...] = -ADAM_LR * ((mv * c1) / (jnp.sqrt(vv * c2) + ADAM_EPS) + ADAM_WD * w_ref[...])
        for ref in g_out:
            ref[...] = gv

    if slabs:
        tl = max(d for d in range(1, shape[0] + 1) if shape[0] % d == 0 and d * shape[1] * cols * 4 <= (3 << 19))
        spec = pl.BlockSpec((tl, shape[1], cols), lambda i: (i, 0, 0))
        return tuple(pl.pallas_call(
            body, name=name, grid=(shape[0] // tl,), in_specs=[spec] * 4, out_specs=[spec] * 4,
            out_shape=[jax.ShapeDtypeStruct(shape, f32)] * 4, compiler_params=_params(("parallel",)),
        )(w, g, m, v))
    spec = pl.BlockSpec((tr, cols), lambda i: (i, 0))
    outs = pl.pallas_call(
        body, name=name, grid=(rows // tr,), in_specs=[spec] * 4, out_specs=[spec] * 3,
        out_shape=[jax.ShapeDtypeStruct((rows, cols), f32)] * 3, compiler_params=_params(("parallel",)),
    )(*[t.reshape(rows, cols) for t in (w, g, m, v)])
    return tuple(t.reshape(shape) for t in outs)


def _pad_lanes(t, n=HD):
    return jnp.pad(t, [(0, 0)] * (t.ndim - 1) + [(0, n - t.shape[-1])])


def _rows128(t, rows=None):
    t = t.reshape(-1, HD)
    rows = rows or -(-t.shape[0] // 8) * 8
    return jnp.pad(t, ((0, rows - t.shape[0]), (0, 0)))


def kernel(x, norm_w, w_in, lru_conv_w, lru_conv_b, lru_wa, lru_ba, lru_wx, lru_bx, lru_lambda, lru_norm_w, dn_conv_w, dn_A_log, dn_dt_bias, dn_norm_w, w_out, final_norm_w, loss_target, m_norm_w, m_w_in, m_lru_conv_w, m_lru_conv_b, m_lru_wa, m_lru_ba, m_lru_wx, m_lru_bx, m_lru_lambda, m_lru_norm_w, m_dn_conv_w, m_dn_A_log, m_dn_dt_bias, m_dn_norm_w, m_w_out, m_final_norm_w, v_norm_w, v_w_in, v_lru_conv_w, v_lru_conv_b, v_lru_wa, v_lru_ba, v_lru_wx, v_lru_bx, v_lru_lambda, v_lru_norm_w, v_dn_conv_w, v_dn_A_log, v_dn_dt_bias, v_dn_norm_w, v_w_out, v_final_norm_w):
    depth = norm_w.shape[0]
    xs = x[0]
    target = loss_target[0]
    chip = 2 * lax.axis_index("x") + lax.axis_index("y")

    win_b = [w_in[l].astype(bf16) for l in range(depth)]
    wout_b = [w_out[l].astype(bf16) for l in range(depth)]
    got_in0 = _gather_halves([win_b[0]], [], "gather_weights0")[0]
    rest0 = [wout_b[0], lru_conv_w, dn_conv_w]
    rest0_started = _gather_start(rest0, "gather_rest0_start")
    gathered, later = {}, {}
    pad_cols = jnp.zeros((D_MODEL, D_INP - D_IN), bf16)

    def weights_of(l, after):
        if l == 0:
            got, token = got_in0, rest0_started[-1]
        else:
            gathered[l] = _gather_finish(later[l], [win_b[l], wout_b[l]], after, f"gather_weights{l}_wait")
            got, token = gathered[l][0], None
        return jnp.concatenate([got[j] for j in range(4)] + [pad_cols], axis=1), token

    def rest_of(l, after):
        token = None
        if "rest0" not in gathered:
            gathered["rest0"] = _gather_finish(rest0_started, rest0, after, "gather_rest0_wait")
            for k in range(1, depth):
                later[k] = _gather_start([win_b[k], wout_b[k]], f"gather_weights{k}_start", after=gathered["rest0"][0])
                token = later[k][-1] if token is None else token + later[k][-1]
        g_out, g_lcw, g_dcw = gathered["rest0"]
        if l > 0:
            g_out = gathered[l][1]
        lcw_full = g_lcw.transpose(1, 2, 0, 3).reshape(depth, 4, D_MODEL)
        dcw_full = g_dcw.transpose(1, 2, 0, 3).reshape(depth, 4, 3 * D_MODEL)
        return g_out.reshape(D_MIX, D_MODEL), _behind(lcw_full[l], token), dcw_full[l]

    in_flight = {}

    def on_grad(kind, l, g):
        if l == depth - 1 and depth > 1:
            if kind == "wout":
                in_flight["held"] = g
                return None
            srcs, kinds, key = [g, in_flight.pop("held")], ["win", "wout"], ("both", l)
        else:
            srcs, kinds, key = [g], [kind], (kind, l)
        started = _scatter_start(srcs, kinds, f"scatter_{key[0]}{key[1]}_start")
        in_flight[key] = (started, srcs, kinds)
        return started[-1]

    vecs, dnvs = [], []
    zrow = jnp.zeros((D_MODEL,), f32)
    for l in range(depth):
        vecs.append(jnp.stack([lru_conv_b[l], lru_ba[l], lru_bx[l], lru_lambda[l], lru_norm_w[l], zrow, zrow, zrow]))
        dnvs.append(jnp.concatenate([_pad_lanes(dn_A_log[l][None]), _pad_lanes(dn_dt_bias[l][None]),
                                     dn_norm_w[l][None], jnp.zeros((5, HD), f32)], axis=0))

    loss_part, grad_x, d_fnw, g_nw, g_vec, g_wa, g_wx, g_lcw_, g_dcw_, g_dsc = _local_step(
        xs, target, norm_w, final_norm_w, weights_of, rest_of, on_grad, vecs, dnvs, lru_wa, lru_wx)
    loss = lax.psum(loss_part[0, 0], ("x", "y", "c"))
    grads = _reduce_grads(chip, grad_x, in_flight, d_fnw, g_nw, g_vec, g_wa, g_wx, g_lcw_, g_dcw_, g_dsc)

    names = ["norm_w", "w_in", "lru_conv_w", "lru_conv_b", "lru_wa", "lru_ba", "lru_wx", "lru_bx", "lru_lambda",
             "lru_norm_w", "dn_conv_w", "dn_A_log", "dn_dt_bias", "dn_norm_w", "w_out", "final_norm_w"]
    ws = dict(zip(names, [norm_w, w_in, lru_conv_w, lru_conv_b, lru_wa, lru_ba, lru_wx, lru_bx, lru_lambda, lru_norm_w,
                          dn_conv_w, dn_A_log, dn_dt_bias, dn_norm_w, w_out, final_norm_w]))
    ms = dict(zip(names, [m_norm_w, m_w_in, m_lru_conv_w, m_lru_conv_b, m_lru_wa, m_lru_ba, m_lru_wx, m_lru_bx,
                          m_lru_lambda, m_lru_norm_w, m_dn_conv_w, m_dn_A_log, m_dn_dt_bias, m_dn_norm_w, m_w_out,
                          m_final_norm_w]))
    vs = dict(zip(names, [v_norm_w, v_w_in, v_lru_conv_w, v_lru_conv_b, v_lru_wa, v_lru_ba, v_lru_wx, v_lru_bx,
                          v_lru_lambda, v_lru_norm_w, v_dn_conv_w, v_dn_A_log, v_dn_dt_bias, v_dn_norm_w, v_w_out,
                          v_final_norm_w]))
    to_cols = lambda t: jnp.transpose(t, (2, 0, 1))
    from_cols = lambda t: jnp.transpose(t, (1, 2, 0))
    deltas, new_m, new_v = [], [], []
    for n in names:
        if n in ("w_in", "lru_conv_w", "dn_conv_w"):
            view, back = (to_cols, from_cols) if n == "w_in" else (lambda t: t, lambda t: t)
            d, nm_, nv_, grads[n] = (back(t) for t in _adamw(view(ws[n]), view(grads[n]), view(ms[n]), view(vs[n]),
                                                            f"adamw_{n}"))
        else:
            d, nm_, nv_ = _adamw(ws[n], grads[n], ms[n], vs[n], f"adamw_{n}")
        deltas.append(d)
        new_m.append(nm_)
        new_v.append(nv_)
    return (loss, grad_x[None], *[grads[n] for n in names], *deltas, *new_m, *new_v)


def _behind(t, token):
    return t if token is None else t + token[0:1, 0:1]


def _local_step(xs, target, norm_w, final_norm_w, weights_of, rest_of, on_grad, vecs, dnvs, lru_wa, lru_wx):
    depth = norm_w.shape[0]
    saved = []
    h = xs
    for l in range(depth):
        wp_l, token = weights_of(l, h)
        hn = _rms_fwd(h, _behind(norm_w[l][None], token), f"rms_fwd{l}")
        proj = _matmul(hn, wp_l, tm=2048, tn=896, tk=D_MODEL, name=f"in_proj{l}")
        wo_l, lcw_l, dcw_l = rest_of(l, proj)
        y, hl = _lru_fwd(proj, lcw_l, vecs[l], lru_wa[l], lru_wx[l], f"lru_fwd{l}")
        y, o, states, tinvs, cpres = _dn_fwd(proj, dcw_l, dnvs[l], y, f"dn_fwd{l}")
        out = _matmul(y, wo_l, tm=1024, tn=D_MODEL, tk=D_MIX, res=h, name=f"out_proj{l}")
        saved.append((h, hn, proj, hl, o, states, tinvs, cpres, y, wp_l, wo_l, lcw_l, dcw_l))
        h = out

    dh, d_fnw, loss_part = _final_loss(h, final_norm_w[None], target, "final_loss")

    g_nw, g_vec, g_wa, g_wx, g_lcw_, g_dcw_, g_dsc = ([None] * depth for _ in range(7))
    for l in reversed(range(depth)):
        hin, hn, proj, hl, o, states, tinvs, cpres, y, wp_l, wo_l, lcw_l, dcw_l = saved[l]
        dy = _matmul(dh, wo_l, tb=True, tm=2048, tn=1024, tk=D_MODEL, name=f"d_y{l}")
        g_wout = _matmul(y, dh, ta=True, tm=1024, tn=D_MODEL, tk=512, out_dtype=bf16, name=f"d_wout{l}")
        token = on_grad("wout", l, g_wout)
        dlx, dlz, g_lcw_[l], g_vec[l], g_wa[l], g_wx[l] = _lru_bwd(dy, proj, hl, lcw_l, _behind(vecs[l], token),
                                                                 lru_wa[l], lru_wx[l], f"lru_bwd{l}")
        dq, dk, dv, dz, dba, dcw8, g_dsc[l] = _dn_bwd(dy, proj, o, states, tinvs, cpres, dcw_l, dnvs[l], f"dn_bwd{l}")
        g_dcw_[l] = dcw8.reshape(HEADS, 4, 3, HD).transpose(1, 2, 0, 3).reshape(4, 3 * D_MODEL)
        dxs = [dlx, dlz, dq, dk, dv, dz]
        g_win = _d_win(_transpose(hn, f"hn_t{l}"), dxs, dba, f"d_win{l}")
        token = on_grad("win", l, g_win)
        dhn = _d_hn(dxs, _behind(dba, token), wp_l, f"d_hn{l}")
        dh, g_nw[l] = _rms_bwd(dhn, hin, norm_w[l][None], dh, f"rms_bwd{l}")
    return loss_part, dh, d_fnw, g_nw, g_vec, g_wa, g_wx, g_lcw_, g_dcw_, g_dsc


def _reduce_grads(chip, after, in_flight, d_fnw, g_nw, g_vec, g_wa, g_wx, g_lcw_, g_dcw_, g_dsc):
    depth = len(g_nw)
    both = lambda f: jnp.concatenate([f(l) for l in range(depth)])
    small = [
        both(lambda l: _rows128(g_nw[l])),
        both(lambda l: _rows128(g_vec[l][0])), both(lambda l: _rows128(g_vec[l][1])),
        both(lambda l: _rows128(g_vec[l][2])), both(lambda l: _rows128(g_vec[l][3])),
        both(lambda l: _rows128(g_vec[l][4])),
        both(lambda l: _rows128(g_wa[l])), both(lambda l: _rows128(g_wx[l])),
        both(lambda l: _rows128(g_dsc[l][0:1])), both(lambda l: _rows128(g_dsc[l][1:2])),
        both(lambda l: _rows128(g_dsc[l][2:3])),
        _rows128(d_fnw),
        both(lambda l: _rows128(g_lcw_[l])), both(lambda l: _rows128(g_dcw_[l])),
    ]
    counts = [t.shape[0] for t in small]
    total = sum(counts)
    per = -(-total // (8 * N_DEV)) * 8
    small = jnp.concatenate(small + [jnp.zeros((per * N_DEV - total, HD), f32)]).reshape(N_DEV, per, HD)

    small_started = _scatter_start([small], ["small"], "scatter_small_start")
    reduced, pairs = {}, [None] * depth
    for key in sorted(in_flight, key=lambda k: -k[1]):
        started, srcs, kinds = in_flight[key]
        got = _scatter_finish(started, srcs, kinds, after, f"scatter_{key[0]}{key[1]}_wait")
        for kind, part in zip(kinds, got):
            reduced[(kind, key[1])] = _sum_parts(part, f"sum_{kind}{key[1]}")
        after = reduced[(kinds[-1], key[1])]
        if key[1] > 0 and ("win", key[1]) in reduced and ("wout", key[1]) in reduced:
            pairs[key[1]] = _share_reduced([reduced[("win", key[1])], reduced[("wout", key[1])]], None,
                                           f"share_grads{key[1]}")
            after = pairs[key[1]][0]
    got = _scatter_finish(small_started, [small], ["small"], after, "scatter_small_wait")
    shared = _share_reduced([reduced[("win", 0)], reduced[("wout", 0)]], _sum_parts(got[0], "sum_small"), "share_grads0")
    pairs[0] = shared
    a_small = shared[2].reshape(N_DEV * per, HD)
    grad_w_in = jnp.stack([lax.dynamic_slice_in_dim(pairs[l][0].reshape(D_MODEL, WIN_W), 4 * chip, SHARD_IN, 1)
                           for l in range(depth)])
    grad_w_out = jnp.stack([pairs[l][1].reshape(D_MIX // 4, D_MODEL) for l in range(depth)])

    offs = [0]
    for c in counts:
        offs.append(offs[-1] + c)

    def piece(k, rows_per_layer=None):
        t = a_small[offs[k]:offs[k + 1]]
        if rows_per_layer is None:
            return t
        return t.reshape(depth, -1, HD)[:, :rows_per_layer]

    vec = lambda k: piece(k).reshape(depth, D_MODEL)
    return {
        "norm_w": vec(0), "lru_conv_b": vec(1), "lru_ba": vec(2), "lru_bx": vec(3), "lru_lambda": vec(4),
        "lru_norm_w": vec(5),
        "lru_wa": piece(6).reshape(depth, HEADS, HD, HD), "lru_wx": piece(7).reshape(depth, HEADS, HD, HD),
        "dn_A_log": piece(8, 1)[:, 0, :HEADS], "dn_dt_bias": piece(9, 1)[:, 0, :HEADS], "dn_norm_w": piece(10, 1)[:, 0],
        "final_norm_w": piece(11).reshape(D_MODEL),
        "lru_conv_w": lax.dynamic_slice_in_dim(piece(12).reshape(depth, 4, D_MODEL), chip * (D_MODEL // 4), D_MODEL // 4, 2),
        "dn_conv_w": lax.dynamic_slice_in_dim(piece(13).reshape(depth, 4, 3 * D_MODEL), chip * (3 * D_MODEL // 4),
                                              3 * D_MODEL // 4, 2),
        "w_in": grad_w_in, "w_out": grad_w_out,
    }
```

```python
import jax
import jax.numpy as jnp
from jax import lax
from jax.experimental import pallas as pl
from jax.experimental.pallas import tpu as pltpu

f32 = jnp.float32
bf16 = jnp.bfloat16
MESH = pl.DeviceIdType.MESH

D_MODEL = 1024
HEADS = 8
HD = 128
CHUNK = 64
LRU_T = 128
SEQ_BLOCK = 1024
LRU_HP = 4
DN_SEQ_BLOCK = 256
DN_HP = 8
LRU_C = 8.0
D_IN = 6160
D_INP = 6272
D_MIX = 2048
N_GROUPS = 6
BLK_BA = 48
SHARD_IN = D_IN // 4
WIN_W = 1664
WIN_STRIDE = 1536
EPS = 1e-6
QSCALE = HD ** -0.5
N_DEV = 8
VMEM_LIMIT = 56 * 1024 * 1024

ADAM_LR, ADAM_B1, ADAM_B2, ADAM_EPS, ADAM_WD, ADAM_STEP = 0.001, 0.9, 0.999, 1e-08, 0.01, 10


def _sig(x):
    return 1.0 / (1.0 + jnp.exp(-x))


def _log1p(u):
    w = 1.0 + u
    d = w - 1.0
    return jnp.where(d == 0.0, u, jnp.log(w) * (u / jnp.where(d == 0.0, 1.0, d)))


def _softplus(x):
    return jnp.maximum(x, 0.0) + _log1p(jnp.exp(-jnp.abs(x)))


def _expm1(x):
    t = x * (1.0 + x * (0.5 + x * (1.0 / 6 + x * (1.0 / 24 + x * (1.0 / 120 + x * (1.0 / 720))))))
    return jnp.where(jnp.abs(x) < 0.2, t, jnp.exp(x) - 1.0)


def _dot(a, b, prec=None):
    return jnp.dot(a, b, precision=prec, preferred_element_type=f32)


def _dot_nt(a, b, prec=None):
    return lax.dot_general(a, b, (((1,), (1,)), ((), ())), precision=prec, preferred_element_type=f32)


def _dot_tn(a, b, prec=None):
    return lax.dot_general(a, b, (((0,), (0,)), ((), ())), precision=prec, preferred_element_type=f32)


def _b(x):
    return x.astype(bf16)


def _sum0(x):
    return jnp.sum(x, axis=0, keepdims=True)


def _sum1(x):
    return jnp.sum(x, axis=1, keepdims=True)


def _rowmean(x):
    return jnp.mean(x, axis=-1, keepdims=True)


def _dsilu(z, sg):
    return sg * (1.0 + z * (1.0 - sg))


def _conv_taps(x, halo):
    xx = jnp.concatenate([halo, x], axis=0)
    return [pltpu.roll(xx, 3, axis=0)[8:], pltpu.roll(xx, 2, axis=0)[8:], pltpu.roll(xx, 1, axis=0)[8:], x]


def _conv(xs, cw):
    return cw[0:1] * xs[0] + cw[1:2] * xs[1] + cw[2:3] * xs[2] + cw[3:4] * xs[3]


def _rows_before(ref, halo_ref, lanes, t0, c, sblk):
    tprev = pl.multiple_of(jnp.maximum(t0 - 8, 0), 8)
    return jnp.where(c > 0, ref[pl.ds(tprev, 8), lanes], jnp.where(sblk > 0, halo_ref[:, lanes], 0.0))


def _conv_back(dxc, halo_after, cw, x=None):
    rows = dxc.shape[0]
    dd = jnp.concatenate([dxc, halo_after], axis=0)
    out = cw[3:4] * dxc
    dcw = [None, None, None, None if x is None else _sum0(x * dxc)]
    for s in (1, 2, 3):
        shifted = pltpu.roll(dd, rows + 8 - s, axis=0)[:rows]
        out = out + cw[3 - s:4 - s] * shifted
        if x is not None:
            dcw[3 - s] = _sum0(x * shifted)
    return out if x is None else (out, jnp.concatenate(dcw, axis=0))


def _params(sem=None):
    return pltpu.CompilerParams(dimension_semantics=sem, vmem_limit_bytes=VMEM_LIMIT)


def _seq_specs(sb, width, rowblk, colblk):
    main = pl.BlockSpec((sb, width), lambda a, b: (rowblk(a, b), colblk(a, b)))
    halo = pl.BlockSpec((8, width), lambda a, b: (jnp.maximum(rowblk(a, b) * (sb // 8) - 1, 0), colblk(a, b)))
    return main, halo


def _matmul(a, b, *, ta=False, tb=False, tm, tn, tk, res=None, out_dtype=f32, name):
    if ta:
        kdim, m = a.shape
    else:
        m, kdim = a.shape
    n = b.shape[0] if tb else b.shape[1]
    tm, tn, tk = min(tm, m), min(tn, n), min(tk, kdim)
    assert m % tm == 0 and n % tn == 0 and kdim % tk == 0, (name, m, n, kdim, tm, tn, tk)
    nk = kdim // tk
    dims = (((0 if ta else 1,), (1 if tb else 0,)), ((), ()))
    has_res = res is not None

    def body(*refs):
        a_ref, b_ref = refs[:2]
        r_ref = refs[2] if has_res else None
        o_ref = refs[3] if has_res else refs[2]
        acc_ref = refs[-1] if nk > 1 else None
        part = lax.dot_general(_b(a_ref[...]), _b(b_ref[...]), dims, preferred_element_type=f32)

        def finish(total):
            if has_res:
                total = total + r_ref[...]
            o_ref[...] = total.astype(out_dtype)

        if nk == 1:
            finish(part)
        else:
            k = pl.program_id(2)

            @pl.when(k == 0)
            def _():
                acc_ref[...] = part

            @pl.when(k > 0)
            def _():
                acc_ref[...] += part

            @pl.when(k == nk - 1)
            def _():
                finish(acc_ref[...])

    a_spec = pl.BlockSpec((tk, tm), lambda i, j, k: (k, i)) if ta else pl.BlockSpec((tm, tk), lambda i, j, k: (i, k))
    b_spec = pl.BlockSpec((tn, tk), lambda i, j, k: (j, k)) if tb else pl.BlockSpec((tk, tn), lambda i, j, k: (k, j))
    o_spec = pl.BlockSpec((tm, tn), lambda i, j, k: (i, j))
    in_specs, args = [a_spec, b_spec], [a, b]
    if has_res:
        in_specs.append(o_spec)
        args.append(res)
    return pl.pallas_call(
        body, name=name, grid=(m // tm, n // tn, nk), in_specs=in_specs, out_specs=o_spec,
        out_shape=jax.ShapeDtypeStruct((m, n), out_dtype),
        scratch_shapes=[pltpu.VMEM((tm, tn), f32)] if nk > 1 else [],
        compiler_params=_params(("parallel", "parallel", "arbitrary")),
    )(*args)


def _transpose(x, name):
    r, c = x.shape
    tr, tc = min(512, r), min(512, c)

    def body(x_ref, o_ref):
        o_ref[...] = x_ref[...].T

    return pl.pallas_call(
        body, name=name, grid=(r // tr, c // tc), in_specs=[pl.BlockSpec((tr, tc), lambda i, j: (i, j))],
        out_specs=pl.BlockSpec((tc, tr), lambda i, j: (j, i)), out_shape=jax.ShapeDtypeStruct((c, r), x.dtype),
        compiler_params=_params(("parallel", "parallel")),
    )(x)


def _d_hn(dxs, dba, wp, name):
    s = dxs[0].shape[0]
    tm = min(512, s)

    def body(*refs):
        dx_refs = refs[:N_GROUPS]
        dba_ref, w_ref, o_ref = refs[N_GROUPS:]
        total = _dot_nt(_b(dba_ref[...]), w_ref[:, BLK_BA * HD:])
        for g in range(N_GROUPS):
            total = total + _dot_nt(dx_refs[g][...], w_ref[:, g * D_MODEL:(g + 1) * D_MODEL])
        o_ref[...] = total

    row = lambda w: pl.BlockSpec((tm, w), lambda i: (i, 0))
    return pl.pallas_call(
        body, name=name, grid=(s // tm,),
        in_specs=[row(D_MODEL)] * N_GROUPS + [row(HD), pl.BlockSpec((D_MODEL, D_INP), lambda i: (0, 0))],
        out_specs=row(D_MODEL), out_shape=jax.ShapeDtypeStruct((s, D_MODEL), f32),
        compiler_params=_params(("parallel",)),
    )(*dxs, dba, wp)


def _d_win(hnt, dxs, dba, name):
    s = hnt.shape[1]
    tn = 256
    per = D_MODEL // tn

    def body(*refs):
        hnt_ref = refs[0]
        dx_refs = refs[1:1 + N_GROUPS]
        o_ref = refs[1 + N_GROUPS]
        j = pl.program_id(0)
        for g in range(N_GROUPS):
            @pl.when(j // per == g)
            def _(g=g):
                o_ref[...] = _dot(hnt_ref[...], _b(dx_refs[g][...])).astype(bf16)

    def dx_spec(g):
        on = lambda j: (j // per == g).astype(jnp.int32)
        return pl.BlockSpec((s, tn), lambda j: (0, (j % per) * on(j)))

    main = pl.pallas_call(
        body, name=name, grid=(N_GROUPS * per,),
        in_specs=[pl.BlockSpec((D_MODEL, s), lambda j: (0, 0))] + [dx_spec(g) for g in range(N_GROUPS)],
        out_specs=pl.BlockSpec((D_MODEL, tn), lambda j: (0, j)),
        out_shape=jax.ShapeDtypeStruct((D_MODEL, D_INP), bf16),
        compiler_params=_params(("parallel",)),
    )(hnt, *dxs)

    def tail(hnt_ref, dba_ref, full_ref, o_ref):
        del full_ref
        o_ref[...] = _dot(hnt_ref[...], _b(dba_ref[...])).astype(bf16)

    return pl.pallas_call(
        tail, name=name + "_ba", grid=(1,),
        in_specs=[pl.BlockSpec((D_MODEL, s), lambda k: (0, 0)), pl.BlockSpec((s, HD), lambda k: (0, 0)),
                  pl.BlockSpec(memory_space=pl.ANY)],
        out_specs=pl.BlockSpec((D_MODEL, HD), lambda k: (0, BLK_BA)),
        out_shape=jax.ShapeDtypeStruct((D_MODEL, D_INP), bf16),
        input_output_aliases={2: 0},
        compiler_params=_params(("arbitrary",)),
    )(hnt, dba, main)


def _rms_fwd(x, w, name):
    s = x.shape[0]
    tr = min(512, s)

    def body(x_ref, w_ref, h_ref):
        xv = x_ref[...]
        r = lax.rsqrt(_rowmean(xv * xv) + EPS)
        h_ref[...] = (xv * r * w_ref[...]).astype(bf16)

    return pl.pallas_call(
        body, name=name, grid=(s // tr,),
        in_specs=[pl.BlockSpec((tr, D_MODEL), lambda i: (i, 0)), pl.BlockSpec((1, D_MODEL), lambda i: (0, 0))],
        out_specs=pl.BlockSpec((tr, D_MODEL), lambda i: (i, 0)),
        out_shape=jax.ShapeDtypeStruct((s, D_MODEL), bf16), compiler_params=_params(("parallel",)),
    )(x, w)


def _rms_bwd(dh, x, w, dres, name):
    s = x.shape[0]
    tr = min(512, s)

    def body(dh_ref, x_ref, w_ref, dres_ref, dx_ref, dw_ref):
        xv = x_ref[...]
        r = lax.rsqrt(_rowmean(xv * xv) + EPS)
        xn = xv * r
        d = dh_ref[...]
        dxn = d * w_ref[...]
        dx_ref[...] = dres_ref[...] + r * (dxn - xn * _rowmean(dxn * xn))
        part = _sum0(d * xn)

        @pl.when(pl.program_id(0) == 0)
        def _():
            dw_ref[...] = part

        @pl.when(pl.program_id(0) > 0)
        def _():
            dw_ref[...] += part

    row = pl.BlockSpec((tr, D_MODEL), lambda i: (i, 0))
    vec = pl.BlockSpec((1, D_MODEL), lambda i: (0, 0))
    return pl.pallas_call(
        body, name=name, grid=(s // tr,), in_specs=[row, row, vec, row], out_specs=[row, vec],
        out_shape=[jax.ShapeDtypeStruct((s, D_MODEL), f32), jax.ShapeDtypeStruct((1, D_MODEL), f32)],
        compiler_params=_params(("arbitrary",)),
    )(dh, x, w, dres)


def _final_loss(x, w, target, name):
    s = x.shape[0]
    tr = min(512, s)

    def body(x_ref, w_ref, t_ref, dx_ref, dw_ref, loss_ref):
        xv = x_ref[...]
        wv = w_ref[...]
        r = lax.rsqrt(_rowmean(xv * xv) + EPS)
        xn = xv * r
        e = xn * wv - t_ref[...]
        lb = jnp.broadcast_to(0.5 * _sum0(_rowmean(e * e)), (1, HD))
        dy = e * (1.0 / D_MODEL)
        dxn = dy * wv
        dx_ref[...] = r * (dxn - xn * _rowmean(dxn * xn))
        part = _sum0(dy * xn)

        @pl.when(pl.program_id(0) == 0)
        def _():
            dw_ref[...] = part
            loss_ref[...] = lb

        @pl.when(pl.program_id(0) > 0)
        def _():
            dw_ref[...] += part
            loss_ref[...] += lb

    row = pl.BlockSpec((tr, D_MODEL), lambda i: (i, 0))
    vec = pl.BlockSpec((1, D_MODEL), lambda i: (0, 0))
    lsp = pl.BlockSpec((1, HD), lambda i: (0, 0))
    return pl.pallas_call(
        body, name=name, grid=(s // tr,), in_specs=[row, vec, row], out_specs=[row, vec, lsp],
        out_shape=[jax.ShapeDtypeStruct((s, D_MODEL), f32), jax.ShapeDtypeStruct((1, D_MODEL), f32),
                   jax.ShapeDtypeStruct((1, HD), f32)],
        compiler_params=_params(("arbitrary",)),
    )(x, w, target)


def _lru_gates(xc, vec, sp, wa, wx):
    xcb = _b(xc)
    r = _sig(_dot(xcb, wa) + vec[1:2])
    i = _sig(_dot(xcb, wx) + vec[2:3])
    la = (-LRU_C) * r * sp
    a = jnp.exp(la)
    mult = jnp.sqrt(-_expm1(2.0 * la))
    return r, i, a, mult, sp, xcb


def _lru_fwd(proj, cw, vec, wa, wx, name):
    s = proj.shape[0]
    sb = min(SEQ_BLOCK, s)
    nsb = s // sb
    t = min(LRU_T, sb)
    nc = sb // t
    hp = LRU_HP
    wide = hp * HD
    lanes = [slice(j * HD, (j + 1) * HD) for j in range(hp)]

    def body(x_ref, xh_ref, z_ref, cw_ref, vec_ref, wa_ref, wx_ref, y_ref, hl_ref, hc_ref):
        sblk = pl.program_id(1)
        consts = [(cw_ref[:, hl], vec_ref[:, hl], _softplus(-vec_ref[3:4, hl]), _b(wa_ref[j]), _b(wx_ref[j]))
                  for j, hl in enumerate(lanes)]
        row = lax.broadcasted_iota(jnp.int32, (t, HD), 0)

        @pl.when(sblk == 0)
        def _():
            hc_ref[...] = jnp.zeros((8, wide), f32)

        def one_head(j, x, halo, z, hcarry):
            cwv, vec_v, sp_v, wa_v, wx_v = consts[j]
            xs = _conv_taps(x, halo)
            xc = vec_v[0:1] + _conv(xs, cwv)
            r, i, a, mult, _, _ = _lru_gates(xc, vec_v, sp_v, wa_v, wx_v)
            yield
            bb = mult * (i * xc)
            d = 1
            while d < t:
                m = row >= d
                bb = jnp.where(m, a * pltpu.roll(bb, d, axis=0) + bb, bb)
                a = jnp.where(m, a * pltpu.roll(a, d, axis=0), a)
                d *= 2
                yield
            hl = bb + a * hcarry
            nrm = lax.rsqrt(_rowmean(hl * hl) + EPS)
            return hl, (hl * nrm * vec_v[4:5] * (z * _sig(z))).astype(bf16)

        def chunk(c, carries):
            t0 = pl.multiple_of(c * t, t)
            rows = pl.ds(t0, t)
            ins = [(x_ref[rows, hl], _rows_before(x_ref, xh_ref, hl, t0, c, sblk), z_ref[rows, hl]) for hl in lanes]
            outs = _round_robin([one_head(j, *ins[j], carries[j]) for j in range(hp)])
            for j, hl in enumerate(lanes):
                hl_ref[rows, hl] = outs[j][0]
                y_ref[rows, hl] = outs[j][1]
            return tuple(out[0][t - 1:t, :] for out in outs)

        final = lax.fori_loop(0, nc, chunk, tuple(hc_ref[0:1, hl] for hl in lanes))
        hc_ref[...] = jnp.concatenate([jnp.broadcast_to(f, (8, HD)) for f in final], axis=1)

    xsp, xhalo = _seq_specs(sb, wide, lambda g, i: i, lambda g, i: g)
    zsp, _ = _seq_specs(sb, wide, lambda g, i: i, lambda g, i: HEADS // hp + g)
    gcol = lambda g, i: (0, g)
    wsp = pl.BlockSpec((hp, HD, HD), lambda g, i: (g, 0, 0))
    osp = pl.BlockSpec((sb, wide), lambda g, i: (i, g))
    return pl.pallas_call(
        body, name=name, grid=(HEADS // hp, nsb),
        in_specs=[xsp, xhalo, zsp, pl.BlockSpec((4, wide), gcol), pl.BlockSpec((8, wide), gcol), wsp, wsp],
        out_specs=[osp, osp],
        out_shape=[jax.ShapeDtypeStruct((s, D_MIX), bf16), jax.ShapeDtypeStruct((s, D_MODEL), f32)],
        scratch_shapes=[pltpu.VMEM((8, wide), f32)],
        compiler_params=_params(("parallel", "arbitrary")),
    )(proj, proj, proj, cw, vec, wa, wx)


def _lru_bwd(dy, proj, hl, cw, vec, wa, wx, name):
    s = proj.shape[0]
    sb = min(SEQ_BLOCK, s)
    nsb = s // sb
    t = min(LRU_T, sb)
    nc = sb // t
    hp = LRU_HP
    wide = hp * HD
    lanes = [slice(j * HD, (j + 1) * HD) for j in range(hp)]
    n_acc = 10

    def body(dy_ref, x_ref, xh_ref, z_ref, hl_ref, hlh_ref, cw_ref, vec_ref, wa_ref, wx_ref,
             dx_ref, dz_ref, dcw_ref, dvec_ref, dwa_ref, dwx_ref, acc_ref, dxh_ref):
        step = pl.program_id(1)
        sblk = nsb - 1 - step
        consts = [(cw_ref[:, hl], vec_ref[:, hl], _softplus(-vec_ref[3:4, hl]), _b(wa_ref[j]), _b(wx_ref[j]))
                  for j, hl in enumerate(lanes)]
        row = lax.broadcasted_iota(jnp.int32, (t, HD), 0)

        @pl.when(step == 0)
        def _():
            acc_ref[...] = jnp.zeros((16, wide), f32)
            dxh_ref[...] = jnp.zeros((8, wide), f32)
            dwa_ref[...] = jnp.zeros((hp, HD, HD), f32)
            dwx_ref[...] = jnp.zeros((hp, HD, HD), f32)

        def one_head(j, x, halo, z, hv, hhalo, dyv, carry):
            dcw0, dcw1, dcw2, dcw3, dcb, dba, dbx, dsp, dlnw, mu, dxc_halo = carry
            cwv, vec_v, sp_v, wa_v, wx_v = consts[j]
            lnw = vec_v[4:5]
            xs = _conv_taps(x, halo)
            xc = vec_v[0:1] + _conv(xs, cwv)
            r, i, a, mult, sp, xcb = _lru_gates(xc, vec_v, sp_v, wa_v, wx_v)
            yield
            hprev = pltpu.roll(jnp.concatenate([hhalo, hv], axis=0), 1, axis=0)[8:]
            sgz = _sig(z)
            sz = z * sgz
            nrm = lax.rsqrt(_rowmean(hv * hv) + EPS)
            hn = hv * nrm
            dlnw = dlnw + _sum0(dyv * hn * sz)
            dzv = _b(dyv * hn * lnw * _dsilu(z, sgz))
            dhn = dyv * lnw * sz
            lam = nrm * (dhn - hn * _rowmean(dhn * hn))
            cf = jnp.where(row == t - 1, 1.0, pltpu.roll(a, t - 1, axis=0))
            d = 1
            while d < t:
                m = row < t - d
                lam = jnp.where(m, lam + cf * pltpu.roll(lam, t - d, axis=0), lam)
                cf = jnp.where(m, cf * pltpu.roll(cf, t - d, axis=0), cf)
                d *= 2
                yield
            lam = lam + cf * mu
            mu = a[0:1] * lam[0:1]
            da = lam * hprev
            dmult = lam * (i * xc)
            di = lam * mult * xc
            dxc = lam * mult * i
            dla = da * a - dmult * (a * a) / mult
            dr = dla * (-LRU_C * sp)
            dsp = dsp + _sum0(dla * (-LRU_C * r))
            dra = dr * r * (1.0 - r)
            dia = di * i * (1.0 - i)
            dba = dba + _sum0(dra)
            dbx = dbx + _sum0(dia)
            drab, diab = _b(dra), _b(dia)
            dwa = _dot_tn(xcb, drab)
            dwx = _dot_tn(xcb, diab)
            dxc = dxc + _dot_nt(drab, wa_v) + _dot_nt(diab, wx_v)
            yield
            dcb = dcb + _sum0(dxc)
            dcw0 = dcw0 + _sum0(dxc * xs[0])
            dcw1 = dcw1 + _sum0(dxc * xs[1])
            dcw2 = dcw2 + _sum0(dxc * xs[2])
            dcw3 = dcw3 + _sum0(dxc * xs[3])
            dxv = _b(_conv_back(dxc, dxc_halo, cwv))
            return dxv, dzv, dwa, dwx, (dcw0, dcw1, dcw2, dcw3, dcb, dba, dbx, dsp, dlnw, mu, dxc[0:8])

        def chunk(ci, carries):
            c = nc - 1 - ci
            t0 = pl.multiple_of(c * t, t)
            rows = pl.ds(t0, t)
            ins = [(x_ref[rows, hl], _rows_before(x_ref, xh_ref, hl, t0, c, sblk), z_ref[rows, hl], hl_ref[rows, hl],
                    _rows_before(hl_ref, hlh_ref, hl, t0, c, sblk), dy_ref[rows, hl]) for hl in lanes]
            outs = _round_robin([one_head(j, *ins[j], carries[j]) for j in range(hp)])
            for j, hl in enumerate(lanes):
                dx_ref[rows, hl] = outs[j][0]
                dz_ref[rows, hl] = outs[j][1]
                dwa_ref[j] += outs[j][2]
                dwx_ref[j] += outs[j][3]
            return tuple(out[4] for out in outs)

        acc = acc_ref[...]
        dxh = dxh_ref[...]
        init = tuple(tuple(acc[k:k + 1, hl] for k in range(n_acc)) + (dxh[:, hl],) for hl in lanes)
        final = lax.fori_loop(0, nc, chunk, init)
        rows_out = [jnp.concatenate([final[j][k] for j in range(hp)], axis=1) for k in range(n_acc)]
        zero = jnp.zeros((1, wide), f32)
        acc_ref[...] = jnp.concatenate(rows_out + [zero] * (16 - n_acc), axis=0)
        dxh_ref[...] = jnp.concatenate([final[j][n_acc] for j in range(hp)], axis=1)

        @pl.when(step == nsb - 1)
        def _():
            dcw_ref[...] = jnp.concatenate(rows_out[0:4], axis=0)
            dlam = -rows_out[7] * _sig(-vec_ref[3:4, :])
            dvec_ref[...] = jnp.concatenate([rows_out[4], rows_out[5], rows_out[6], dlam, rows_out[8], zero, zero, zero],
                                            axis=0)

    rblk = lambda g, i: nsb - 1 - i
    xsp, xhalo = _seq_specs(sb, wide, rblk, lambda g, i: g)
    zsp, _ = _seq_specs(sb, wide, rblk, lambda g, i: HEADS // hp + g)
    gcol = lambda g, i: (0, g)
    wsp = pl.BlockSpec((hp, HD, HD), lambda g, i: (g, 0, 0))
    return pl.pallas_call(
        body, name=name, grid=(HEADS // hp, nsb),
        in_specs=[xsp, xsp, xhalo, zsp, xsp, xhalo, pl.BlockSpec((4, wide), gcol), pl.BlockSpec((8, wide), gcol), wsp, wsp],
        out_specs=[xsp, xsp, pl.BlockSpec((4, wide), gcol), pl.BlockSpec((8, wide), gcol), wsp, wsp],
        out_shape=[jax.ShapeDtypeStruct((s, D_MODEL), bf16), jax.ShapeDtypeStruct((s, D_MODEL), bf16),
                   jax.ShapeDtypeStruct((4, D_MODEL), f32), jax.ShapeDtypeStruct((8, D_MODEL), f32),
                   jax.ShapeDtypeStruct((HEADS, HD, HD), f32), jax.ShapeDtypeStruct((HEADS, HD, HD), f32)],
        scratch_shapes=[pltpu.VMEM((16, wide), f32), pltpu.VMEM((8, wide), f32)],
        compiler_params=_params(("parallel", "arbitrary")),
    )(dy, proj, proj, proj, hl, hl, cw, vec, wa, wx)


def _lane_pick(x, lane, idx):
    return _sum1(jnp.where(lane == idx, x, 0.0))


def _round_robin(gens):
    results = [None] * len(gens)
    live = list(range(len(gens)))
    while live:
        for i in list(live):
            try:
                next(gens[i])
            except StopIteration as done:
                results[i] = done.value
                live.remove(i)
    return results


def _sdot(a, b):
    return _dot(_b(a), _b(b))


def _sdot_tn(a, b):
    return _dot_tn(_b(a), _b(b))


def _sdot_nt(a, b):
    return _dot_nt(_b(a), _b(b))


def _tri_dot(tri, x):
    trib = tri.astype(bf16)
    x1 = x.astype(bf16)
    r1 = x - x1.astype(f32)
    x2 = r1.astype(bf16)
    x3 = (r1 - x2.astype(f32)).astype(bf16)
    return _dot(trib, x1) + _dot(trib, x2) + _dot(trib, x3)


def _inv_unit_lower(a):
    n = -a
    p = _sdot(a, a)
    yield
    for k in range(5):
        n = n + p + _sdot(n, p)
        if k < 4:
            p = _sdot(p, p)
        yield
    return n


GATE_KEYS = ("beta", "g", "gc", "sga")


def _dn_gate_consts(dnv):
    return jnp.exp(pltpu.roll(dnv[0:1], HEADS, axis=1)), pltpu.roll(dnv[1:2], HEADS, axis=1)


def _dn_gates(bav, ea, dt, masks):
    ltri = masks[4]
    u = bav + dt
    g = -ea * _softplus(u)
    return _sig(bav), g, _tri_dot(ltri, g), _sig(u)


def _dn_chunk(x3, halo3, gates, cw3, h, masks, cpre=None, tinv=None):
    lane, ri, ci, eye, ltri = masks
    if cpre is None:
        cpre = _conv(_conv_taps(x3, halo3), cw3)
    sg = _sig(cpre)
    act = cpre * sg
    q_raw, k_raw, v = act[:, 0:HD], act[:, HD:2 * HD], act[:, 2 * HD:3 * HD]
    rq = lax.rsqrt(_sum1(q_raw * q_raw) + EPS)
    rk = lax.rsqrt(_sum1(k_raw * k_raw) + EPS)
    qn = q_raw * rq
    k = k_raw * rk
    q = qn * QSCALE
    beta = jnp.broadcast_to(_lane_pick(gates["beta"], lane, h), (CHUNK, HD))
    gc = jnp.broadcast_to(_lane_pick(gates["gc"], lane, HEADS + h), (CHUNK, HD))
    yield
    gc64 = gc[:, 0:CHUNK]
    grow = _sum0(gc64 * eye)
    dm = jnp.exp(jnp.where(ri >= ci, gc64 - grow, -1e30))
    ds_ = jnp.where(ri > ci, dm, 0.0)
    eg = jnp.exp(gc)
    glast = gc[CHUNK - 1:CHUNK, :]
    ek = jnp.exp(glast - gc)
    kb = k * beta
    kbf = _b(k)
    amat = _dot_nt(_b(kb), kbf) * ds_
    pmat = _dot_nt(_b(q), kbf) * dm
    yield
    if tinv is None:
        tinv = yield from _inv_unit_lower(amat)
    return dict(cpre=cpre, sg=sg, rq=rq, rk=rk, qn=qn, q=q, k=k, v=v, kbf=kbf, beta=beta, gc=gc, dm=dm, ds=ds_, eg=eg,
                glast=glast, ek=ek, kb=kb, amat=amat, tinv=tinv, pmat=pmat)


def _dn_masks():
    lane = lax.broadcasted_iota(jnp.int32, (CHUNK, HD), 1)
    ri = lax.broadcasted_iota(jnp.int32, (CHUNK, CHUNK), 0)
    ci = lax.broadcasted_iota(jnp.int32, (CHUNK, CHUNK), 1)
    eye = (ri == ci).astype(f32)
    ltri = (ri >= ci).astype(f32)
    return lane, ri, ci, eye, ltri


def _dn_load_qkv(q_ref, qh_ref, k_ref, kh_ref, v_ref, vh_ref, hl, rows, t0, c, sblk):
    x3 = jnp.concatenate([q_ref[rows, hl], k_ref[rows, hl], v_ref[rows, hl]], axis=1)
    halo3 = jnp.concatenate([_rows_before(q_ref, qh_ref, hl, t0, c, sblk), _rows_before(k_ref, kh_ref, hl, t0, c, sblk),
                             _rows_before(v_ref, vh_ref, hl, t0, c, sblk)], axis=1)
    return x3, halo3


def _dn_cw3(cwq_ref, cwk_ref, cwv_ref, j):
    return jnp.concatenate([r[:, j * HD:(j + 1) * HD] for r in (cwq_ref, cwk_ref, cwv_ref)], axis=1)


def _dn_fwd(proj, cw, dnv, y, name):
    s = proj.shape[0]
    sb = min(DN_SEQ_BLOCK, s)
    nsb = s // sb
    nc = sb // CHUNK
    hp = DN_HP

    def body(q_ref, qh_ref, k_ref, kh_ref, v_ref, vh_ref, z_ref, ba_ref, cwq_ref, cwk_ref, cwv_ref, dnv_ref, yin_ref,
             y_ref, o_ref, st_ref, ti_ref, cp_ref, s_ref):
        del yin_ref
        grp = pl.program_id(0)
        sblk = pl.program_id(1)
        masks = _dn_masks()
        dnv_v = dnv_ref[...]
        cw3 = [_dn_cw3(cwq_ref, cwk_ref, cwv_ref, j) for j in range(hp)]

        @pl.when(sblk == 0)
        def _():
            s_ref[...] = jnp.zeros((hp, HD, HD), f32)

        def one_head(j, x3, halo3, gates, z, st):
            f = yield from _dn_chunk(x3, halo3, gates, cw3[j], grp * hp + j, masks)
            sbf = _b(st)
            qs = _dot(_b(f["q"] * f["eg"]), sbf)
            rhs = f["beta"] * (f["v"] - _dot(_b(f["k"] * f["eg"]), sbf))
            yield
            vn = rhs + _sdot(f["tinv"], rhs)
            yield
            vnb = _b(vn)
            o = qs + _dot(_b(f["pmat"]), vnb)
            st_new = st * jnp.exp(f["glast"]) + _dot_tn(_b(f["k"] * f["ek"]), vnb)
            yield
            nrm = lax.rsqrt(_rowmean(o * o) + EPS)
            yv = (o * nrm * dnv_v[2:3] * (z * _sig(z))).astype(bf16)
            return o, yv, st_new, f["tinv"], f["cpre"]

        ea, dt = _dn_gate_consts(dnv_v)

        def gates_of(c):
            return _dn_gates(ba_ref[pl.ds(pl.multiple_of(c * CHUNK, CHUNK), CHUNK), :], ea, dt, masks)

        def chunk(c, carry):
            states, gate_vals = carry
            gates = dict(zip(GATE_KEYS, gate_vals))
            t0 = pl.multiple_of(c * CHUNK, CHUNK)
            rows = pl.ds(t0, CHUNK)
            ins = []
            for j in range(hp):
                hl = slice(j * HD, (j + 1) * HD)
                ins.append(_dn_load_qkv(q_ref, qh_ref, k_ref, kh_ref, v_ref, vh_ref, hl, rows, t0, c, sblk)
                           + (z_ref[rows, hl],))
            next_gates = gates_of(jnp.minimum(c + 1, nc - 1))
            outs = _round_robin([one_head(j, ins[j][0], ins[j][1], gates, ins[j][2], states[j]) for j in range(hp)])
            for j in range(hp):
                hl = slice(j * HD, (j + 1) * HD)
                st_ref[j, c] = states[j]
                o_ref[rows, hl] = outs[j][0]
                y_ref[rows, hl] = outs[j][1]
                ti_ref[j, c] = outs[j][3]
                cp_ref[rows, j * 3 * HD:(j + 1) * 3 * HD] = outs[j][4]
            return tuple(out[2] for out in outs), next_gates

        final, _ = lax.fori_loop(0, nc, chunk, (tuple(s_ref[j] for j in range(hp)), gates_of(0)))
        for j in range(hp):
            s_ref[j] = final[j]

    wide = hp * HD
    grp_specs = lambda off: _seq_specs(sb, wide, lambda g, i: i, lambda g, i: off // hp + g)
    (qsp, qhalo), (ksp, khalo), (vsp, vhalo) = grp_specs(2 * HEADS), grp_specs(3 * HEADS), grp_specs(4 * HEADS)
    zsp, _ = grp_specs(5 * HEADS)
    basp = pl.BlockSpec((sb, HD), lambda g, i: (i, BLK_BA))
    cws = lambda off: pl.BlockSpec((4, wide), lambda g, i: (0, off // hp + g))
    osp = lambda off: pl.BlockSpec((sb, wide), lambda g, i: (i, off // hp + g))
    return pl.pallas_call(
        body, name=name, grid=(HEADS // hp, nsb),
        in_specs=[qsp, qhalo, ksp, khalo, vsp, vhalo, zsp, basp, cws(0), cws(HEADS), cws(2 * HEADS),
                  pl.BlockSpec((8, HD), lambda g, i: (0, 0)), pl.BlockSpec(memory_space=pl.ANY)],
        out_specs=[osp(HEADS), osp(0), pl.BlockSpec((hp, nc, HD, HD), lambda g, i: (g, i, 0, 0)),
                   pl.BlockSpec((hp, nc, CHUNK, CHUNK), lambda g, i: (g, i, 0, 0)),
                   pl.BlockSpec((sb, 3 * wide), lambda g, i: (i, g))],
        out_shape=[jax.ShapeDtypeStruct((s, D_MIX), bf16), jax.ShapeDtypeStruct((s, D_MODEL), f32),
                   jax.ShapeDtypeStruct((HEADS, s // CHUNK, HD, HD), f32),
                   jax.ShapeDtypeStruct((HEADS, s // CHUNK, CHUNK, CHUNK), f32),
                   jax.ShapeDtypeStruct((s, 3 * D_MODEL), f32)],
        scratch_shapes=[pltpu.VMEM((hp, HD, HD), f32)],
        input_output_aliases={12: 0},
        compiler_params=_params(("parallel", "arbitrary")),
    )(proj, proj, proj, proj, proj, proj, proj, proj, cw, cw, cw, dnv, y)


def _dn_bwd(dy, proj, o, states, tinvs, cpres, cw, dnv, name):
    s = proj.shape[0]
    sb = min(DN_SEQ_BLOCK, s)
    nsb = s // sb
    nc = sb // CHUNK
    hp = DN_HP
    ngrp = HEADS // hp

    def body(dy_ref, q_ref, k_ref, v_ref, z_ref, ba_ref, o_ref, st_ref, ti_ref, cp_ref,
             cwq_ref, cwk_ref, cwv_ref, dnv_ref,
             dq_ref, dk_ref, dv_ref, dz_ref, dba_ref, dcw_ref, dsc_ref,
             ds_ref, dch_ref, cwacc_ref, nacc_ref):
        step = pl.program_id(0)
        grp = pl.program_id(1)
        sblk = nsb - 1 - step
        h0 = grp * hp
        masks = _dn_masks()
        lane, ri, ci, eye, ltri = masks
        utri = (ri <= ci).astype(f32)
        cw3s = [_dn_cw3(cwq_ref, cwk_ref, cwv_ref, j) for j in range(hp)]
        dnv_v = dnv_ref[...]
        dnw = dnv_v[2:3]
        row64 = lax.broadcasted_iota(jnp.int32, (CHUNK, HD), 0)

        @pl.when(step == 0)
        def _():
            for j in range(hp):
                ds_ref[h0 + j] = jnp.zeros((HD, HD), f32)
                dch_ref[h0 + j] = jnp.zeros((8, 3 * HD), f32)
                cwacc_ref[h0 + j] = jnp.zeros((8, 3 * HD), f32)

        @pl.when((step == 0) & (grp == 0))
        def _():
            nacc_ref[...] = jnp.zeros((8, HD), f32)

        def one_head(j, x3, gates, z, st, ti, cp, ov, dyv, carry):
            dst, dch, cwacc = carry
            cw3 = cw3s[j]
            f = yield from _dn_chunk(None, None, gates, cw3, h0 + j, masks, cp, ti)
            q, k, v, beta, eg, ek, kbf = f["q"], f["k"], f["v"], f["beta"], f["eg"], f["ek"], f["kbf"]
            sbf = _b(st)
            dstb = _b(dst)
            qe = q * eg
            keg = k * eg
            kd = k * ek
            kegb, qeb, kdb = _b(keg), _b(qe), _b(kd)
            e = v - _dot(kegb, sbf)
            yield
            rhs = beta * e
            vn = rhs + _sdot(f["tinv"], rhs)
            yield
            vnb = _b(vn)
            pb = _b(f["pmat"])
            sgz = _sig(z)
            sz = z * sgz
            nrm = lax.rsqrt(_rowmean(ov * ov) + EPS)
            on = ov * nrm
            ddnw = _sum0(dyv * on * sz)
            dpz = dyv * on * dnw * _dsilu(z, sgz)
            don = dyv * dnw * sz
            do = nrm * (don - on * _rowmean(don * on))
            dob = _b(do)
            dvn = _dot_tn(pb, dob) + _dot(kdb, dstb)
            dpm = jnp.where(ri >= ci, _dot_nt(dob, vnb), 0.0)
            dqe = _dot_nt(dob, sbf)
            dkd = _dot_nt(vnb, dstb)
            qtdo = _dot_tn(qeb, dob)
            yield
            drhs = dvn + _sdot_tn(f["tinv"], dvn)
            dqk = _b(dpm * f["dm"])
            dq = _dot(dqk, kbf) + dqe * eg
            dk_p = _dot_tn(dqk, _b(q))
            yield
            dam = jnp.where(ri > ci, -_sdot_nt(drhs, vn), 0.0)
            de = drhs * beta
            deb = _b(de)
            dbeta = _sum1(drhs * e)
            dkeg = -_dot_nt(deb, sbf)
            dst_new = qtdo + dst * jnp.exp(f["glast"]) - _dot_tn(kegb, deb)
            yield
            dkk = _b(dam * f["ds"])
            dkb = _dot(dkk, kbf)
            dk = dk_p + _dot_tn(dkk, _b(f["kb"])) + dkb * beta + dkeg * eg + dkd * ek
            yield
            dbeta = dbeta + _sum1(dkb * k)
            mm = dpm * f["pmat"] + dam * f["amat"]
            dglast = _sum1(_sum0(dst * st)) * jnp.exp(f["glast"]) + _sum1(_sum0(dkd * kd))
            dgc = (_sum1(mm) - _sum1(eye * _sum0(mm)) + _sum1(dqe * qe) + _sum1(dkeg * keg) - _sum1(dkd * kd))
            dgc = jnp.broadcast_to(dgc, (CHUNK, HD)) + jnp.where(row64 == CHUNK - 1, dglast, 0.0)
            dq_raw = (QSCALE * f["rq"]) * (dq - f["qn"] * _sum1(f["qn"] * dq))
            dk_raw = f["rk"] * (dk - k * _sum1(k * dk))
            dact = jnp.concatenate([dq_raw, dk_raw, de], axis=1)
            dc = dact * _dsilu(f["cpre"], f["sg"])
            dqkv, dcw_part = _conv_back(dc, dch, cw3, x3)
            cwacc = cwacc + dcw_part
            return dpz, dqkv, (dgc, dbeta), ddnw, (dst_new, dc[0:8], cwacc)

        ea, dt = _dn_gate_consts(dnv_v)

        def gates_of(c):
            return _dn_gates(ba_ref[pl.ds(pl.multiple_of(c * CHUNK, CHUNK), CHUNK), :], ea, dt, masks)

        def gate_grads(pend, c_of):
            dgc_all, dbeta_all, beta, g, sga = pend
            dg_all = _tri_dot(utri, dgc_all)
            dain_all = dg_all * (-ea) * sga
            dba = dbeta_all * beta * (1.0 - beta) + dain_all
            rows_of = pl.ds(c_of * CHUNK if isinstance(c_of, int) else pl.multiple_of(c_of * CHUNK, CHUNK), CHUNK)

            @pl.when(grp == 0)
            def _():
                dba_ref[rows_of, :] = dba

            @pl.when(grp > 0)
            def _():
                dba_ref[rows_of, :] += dba

            return _sum0(dg_all * g), _sum0(dain_all)

        def chunk(cidx, carry):
            heads, nacc, sa0, sa1, gate_vals, pend = carry
            gates = dict(zip(GATE_KEYS, gate_vals))
            c = nc - 1 - cidx
            t0 = pl.multiple_of(c * CHUNK, CHUNK)
            rows = pl.ds(t0, CHUNK)
            ins = []
            for j in range(hp):
                hl = slice(j * HD, (j + 1) * HD)
                x3 = jnp.concatenate([q_ref[rows, hl], k_ref[rows, hl], v_ref[rows, hl]], axis=1)
                ins.append((x3, z_ref[rows, hl], st_ref[j, c], ti_ref[j, c],
                            cp_ref[rows, j * 3 * HD:(j + 1) * 3 * HD], o_ref[rows, hl], dy_ref[rows, hl]))
            next_gates = gates_of(jnp.maximum(c - 1, 0))
            d0, d1 = gate_grads(pend, jnp.minimum(c + 1, nc - 1))
            sa0, sa1 = sa0 + d0, sa1 + d1
            outs = _round_robin([one_head(j, ins[j][0], gates, *ins[j][1:], heads[j]) for j in range(hp)])
            dgc_all = jnp.zeros((CHUNK, HD), f32)
            dbeta_all = jnp.zeros((CHUNK, HD), f32)
            for j in range(hp):
                hl = slice(j * HD, (j + 1) * HD)
                dpz, dqkv, (dgc, dbeta), ddnw, _ = outs[j]
                dgc_all = jnp.where(lane == HEADS + h0 + j, dgc, dgc_all)
                dbeta_all = jnp.where(lane == h0 + j, dbeta, dbeta_all)
                dq_ref[rows, hl] = _b(dqkv[:, 0:HD])
                dk_ref[rows, hl] = _b(dqkv[:, HD:2 * HD])
                dv_ref[rows, hl] = _b(dqkv[:, 2 * HD:3 * HD])
                dz_ref[rows, hl] = _b(dpz)
                nacc = nacc + ddnw
            pend = (dgc_all, dbeta_all, gates["beta"], gates["g"], gates["sga"])
            return tuple(out[4] for out in outs), nacc, sa0, sa1, next_gates, pend

        init = tuple((ds_ref[h0 + j], dch_ref[h0 + j], cwacc_ref[h0 + j][0:4]) for j in range(hp))
        zero = jnp.zeros((CHUNK, HD), f32)
        heads, nacc, sa0, sa1, _, pend = lax.fori_loop(
            0, nc, chunk, (init, nacc_ref[0:1, :], nacc_ref[1:2, :], nacc_ref[2:3, :], gates_of(nc - 1), (zero,) * 5))
        d0, d1 = gate_grads(pend, 0)
        sa0, sa1 = sa0 + d0, sa1 + d1
        nacc_ref[0:3, :] = jnp.concatenate([nacc, sa0, sa1], axis=0)
        zpad = jnp.zeros((4, 3 * HD), f32)
        for j in range(hp):
            dst, dch, cwacc = heads[j]
            ds_ref[h0 + j] = dst
            dch_ref[h0 + j] = dch
            cwacc_ref[h0 + j] = jnp.concatenate([cwacc, zpad], axis=0)

        @pl.when(step == nsb - 1)
        def _():
            for j in range(hp):
                dcw_ref[h0 + j] = heads[j][2]

            @pl.when(grp == ngrp - 1)
            def _():
                dsc_ref[...] = jnp.concatenate([pltpu.roll(sa0, HD - HEADS, axis=1), pltpu.roll(sa1, HD - HEADS, axis=1),
                                                nacc, jnp.zeros((5, HD), f32)], axis=0)

    wide = hp * HD
    rblk = lambda i, g: nsb - 1 - i
    grp_specs = lambda off: _seq_specs(sb, wide, rblk, lambda i, g: off // hp + g)
    (qsp, qhalo), (ksp, khalo), (vsp, vhalo) = grp_specs(2 * HEADS), grp_specs(3 * HEADS), grp_specs(4 * HEADS)
    zsp, _ = grp_specs(5 * HEADS)
    hsp = lambda off: pl.BlockSpec((sb, wide), lambda i, g: (rblk(i, g), off // hp + g))
    basp = lambda col: pl.BlockSpec((sb, HD), lambda i, g: (rblk(i, g), col))
    cws = lambda off: pl.BlockSpec((4, wide), lambda i, g: (0, off // hp + g))
    whole = lambda shape: pl.BlockSpec(shape, lambda i, g: (0,) * len(shape))
    return pl.pallas_call(
        body, name=name, grid=(nsb, ngrp),
        in_specs=[hsp(HEADS), qsp, ksp, vsp, zsp, basp(BLK_BA), hsp(0),
                  pl.BlockSpec((hp, nc, HD, HD), lambda i, g: (g, rblk(i, g), 0, 0)),
                  pl.BlockSpec((hp, nc, CHUNK, CHUNK), lambda i, g: (g, rblk(i, g), 0, 0)),
                  pl.BlockSpec((sb, 3 * wide), lambda i, g: (rblk(i, g), g)),
                  cws(0), cws(HEADS), cws(2 * HEADS), whole((8, HD))],
        out_specs=[hsp(0), hsp(0), hsp(0), hsp(0), basp(0), whole((HEADS, 4, 3 * HD)), whole((8, HD))],
        out_shape=[jax.ShapeDtypeStruct((s, D_MODEL), bf16)] * 4
        + [jax.ShapeDtypeStruct((s, HD), f32), jax.ShapeDtypeStruct((HEADS, 4, 3 * HD), f32),
           jax.ShapeDtypeStruct((8, HD), f32)],
        scratch_shapes=[pltpu.VMEM((HEADS, HD, HD), f32), pltpu.VMEM((HEADS, 8, 3 * HD), f32),
                        pltpu.VMEM((HEADS, 8, 3 * HD), f32), pltpu.VMEM((8, HD), f32)],
        compiler_params=_params(("arbitrary", "arbitrary")),
    )(dy, proj, proj, proj, proj, proj, o, states, tinvs, cpres, cw, cw, cw, dnv)


ANY = pl.BlockSpec(memory_space=pl.ANY)
CHIP_MASKS = ((1, 0, 0), (0, 1, 0), (1, 1, 0))
ALL_MASKS = tuple((m >> 2 & 1, m >> 1 & 1, m & 1) for m in range(1, N_DEV))


def _me():
    return lax.axis_index("x"), lax.axis_index("y"), lax.axis_index("c")


def _flip(me, mask):
    return tuple((1 - v) if m else v for v, m in zip(me, mask))


def _dev_index(dev):
    return 4 * dev[0] + 2 * dev[1] + dev[2]


def _chip(dev):
    return 2 * dev[0] + dev[1]


def _comm_call(name, body, arrays, out_shapes, nsem, nloc=0):
    scratch = [pltpu.SemaphoreType.DMA((nsem,)), pltpu.SemaphoreType.DMA((nsem,))]
    if nloc:
        scratch.append(pltpu.SemaphoreType.DMA((nloc,)))
    return pl.pallas_call(
        body, name=name, in_specs=[ANY] * len(arrays), out_specs=[ANY] * len(out_shapes), out_shape=out_shapes,
        scratch_shapes=scratch, compiler_params=pltpu.CompilerParams(has_side_effects=True),
    )(*arrays)


def _put_own(gathered, own, slot):
    return lax.dynamic_update_index_in_dim(gathered, own, slot, 0)


def _gather_halves(arrays, whole_arrays, name):
    na, nw = len(arrays), len(whole_arrays)
    nm = len(CHIP_MASKS)

    def body(*refs):
        srcs, dsts = refs[:na + nw], refs[na + nw:2 * (na + nw)]
        send_sems, recv_sems = refs[2 * (na + nw):]
        me = _me()
        sib = _flip(me, (0, 0, 1))

        def half(a, ref, core):
            rows = arrays[a].shape[0] // 2
            return ref.at[pl.ds(pl.multiple_of(core * rows, rows), rows)]

        def over_ici(a, mi, sender, to):
            if a < na:
                src, dst = half(a, srcs[a], sender[2]), half(a, dsts[a].at[_chip(sender)], sender[2])
            else:
                src, dst = srcs[a], dsts[a].at[_chip(sender)]
            return pltpu.make_async_remote_copy(src_ref=src, dst_ref=dst, send_sem=send_sems.at[a * nm + mi],
                                                recv_sem=recv_sems.at[a * nm + mi], device_id=to, device_id_type=MESH)

        def over_d2d(a, mi, origin, sender, to):
            k = (na + nw) * nm + a * nm + mi
            blk = half(a, dsts[a].at[_chip(origin)], sender[2])
            return pltpu.make_async_remote_copy(src_ref=blk, dst_ref=blk, send_sem=send_sems.at[k],
                                                recv_sem=recv_sems.at[k], device_id=to, device_id_type=MESH)

        sends = []
        for a in range(na + nw):
            for mi, mask in enumerate(CHIP_MASKS):
                cp = over_ici(a, mi, me, _flip(me, mask))
                cp.start()
                sends.append(cp)
        for a in range(na + nw):
            for mi, mask in enumerate(CHIP_MASKS):
                peer = _flip(me, mask)
                over_ici(a, mi, peer, me).wait_recv()
                if a < na:
                    cp = over_d2d(a, mi, peer, me, sib)
                    cp.start()
                    sends.append(cp)
        for a in range(na):
            for mi, mask in enumerate(CHIP_MASKS):
                over_d2d(a, mi, _flip(sib, mask), sib, me).wait_recv()
        for cp in sends:
            cp.wait_send()

    every = list(arrays) + list(whole_arrays)
    got = _comm_call(name, body, every, [jax.ShapeDtypeStruct((4,) + t.shape, t.dtype) for t in every],
                     (2 * na + nw) * nm)
    chip = 2 * lax.axis_index("x") + lax.axis_index("y")
    return [_put_own(g, t, chip) for g, t in zip(got, every)]


HBM = pl.BlockSpec(memory_space=pltpu.HBM)
SEM = pl.BlockSpec(memory_space=pltpu.SEMAPHORE)
DATAFLOW = pltpu.SideEffectType.DATAFLOW_SIDE_EFFECTING


def _split_start(name, srcs, land_shapes, nsem, issue, after=None):
    ns, nl = len(srcs), len(land_shapes)
    n_in = ns + nl + (after is not None)

    def body(*refs):
        issue(refs[:ns], refs[ns:ns + nl], refs[n_in + ns + nl], refs[n_in + ns + nl + 1])
        token = refs[-1]
        token[...] = jnp.zeros_like(token)

    lands = [lax.empty(t.shape, t.dtype) for t in land_shapes]
    thru = [pltpu.HBM(t.shape, t.dtype) for t in list(srcs) + list(land_shapes)]
    hbm = lambda t: pltpu.with_memory_space_constraint(t, pltpu.HBM)
    return pl.pallas_call(
        body, name=name, in_specs=[HBM] * (ns + nl) + [ANY] * (after is not None),
        out_shape=(*thru, pltpu.SemaphoreType.DMA((nsem,)), pltpu.SemaphoreType.DMA((nsem,)),
                   jax.ShapeDtypeStruct((8, HD), f32)),
        out_specs=(*[HBM] * (ns + nl), SEM, SEM, pl.BlockSpec(memory_space=pltpu.VMEM)),
        input_output_aliases={i: i for i in range(ns + nl)},
        compiler_params=pltpu.CompilerParams(has_side_effects=DATAFLOW),
    )(*[hbm(t) for t in srcs], *[hbm(t) for t in lands], *([after] if after is not None else []))


def _split_wait(name, started, ns, after, finish):
    thru, send_sems, recv_sems = started[:-3], started[-3], started[-2]
    nt = len(thru)

    def body(*refs):
        finish(refs[:ns], refs[ns:nt], refs[nt], refs[nt + 1])

    outs = pl.pallas_call(
        body, name=name, in_specs=[HBM] * nt + [SEM, SEM, ANY],
        out_shape=tuple(pltpu.HBM(t.shape, t.dtype) for t in thru), out_specs=tuple([HBM] * nt),
        input_output_aliases={i: i for i in range(nt)},
        compiler_params=pltpu.CompilerParams(has_side_effects=DATAFLOW),
    )(*thru, send_sems, recv_sems, after)
    return list(outs[ns:])


def _gather_start(arrays, name, after=None):
    nm = len(CHIP_MASKS)

    def issue(srcs, lands, send_sems, recv_sems):
        me = _me()
        for a in range(len(arrays)):
            for mi, mask in enumerate(CHIP_MASKS):
                pltpu.make_async_remote_copy(
                    src_ref=srcs[a], dst_ref=lands[a].at[_chip(me)], send_sem=send_sems.at[a * nm + mi],
                    recv_sem=recv_sems.at[a * nm + mi], device_id=_flip(me, mask), device_id_type=MESH).start()

    shapes = [jax.ShapeDtypeStruct((4,) + t.shape, t.dtype) for t in arrays]
    return _split_start(name, arrays, shapes, len(arrays) * nm, issue, after)


def _gather_finish(started, arrays, after, name):
    nm = len(CHIP_MASKS)

    def finish(srcs, lands, send_sems, recv_sems):
        me = _me()
        for a in range(len(arrays)):
            for mi, mask in enumerate(CHIP_MASKS):
                peer = _flip(me, mask)
                cp = pltpu.make_async_remote_copy(
                    src_ref=srcs[a], dst_ref=lands[a].at[_chip(peer)], send_sem=send_sems.at[a * nm + mi],
                    recv_sem=recv_sems.at[a * nm + mi], device_id=peer, device_id_type=MESH)
                cp.wait_send()
                cp.wait_recv()

    got = _split_wait(name, started, len(arrays), after, finish)
    chip = 2 * lax.axis_index("x") + lax.axis_index("y")
    return [_put_own(g, t, chip) for g, t in zip(got, arrays)]


HALF_IN, HALF_OUT = D_MODEL // 2, D_MIX // 8


def _scatter_piece(kind, ref, d):
    j, r = d >> 1, d & 1
    if kind == "win":
        return ref.at[pl.ds(HALF_IN * r, HALF_IN), pl.ds(WIN_STRIDE * j, WIN_W)]
    if kind == "wout":
        return ref.at[pl.ds(2 * HALF_OUT * j + HALF_OUT * r, HALF_OUT)]
    return ref.at[d]


def _scatter_start(arrays, kinds, name):
    na = len(arrays)

    def issue(srcs, lands, send_sems, recv_sems):
        me = _me()
        my = _dev_index(me)
        for d in range(N_DEV):
            dev = (d >> 2, (d >> 1) & 1, d & 1)
            for a in range(na):
                @pl.when(my != d)
                def _(a=a, d=d, dev=dev):
                    pltpu.make_async_remote_copy(
                        src_ref=_scatter_piece(kinds[a], srcs[a], d), dst_ref=lands[a].at[my],
                        send_sem=send_sems.at[a * N_DEV + d], recv_sem=recv_sems.at[a * N_DEV + my],
                        device_id=dev, device_id_type=MESH).start()

    shape_of = {"win": (N_DEV, HALF_IN, WIN_W), "wout": (N_DEV, HALF_OUT, D_MODEL)}
    shapes = [jax.ShapeDtypeStruct(shape_of.get(k, t.shape), t.dtype) for t, k in zip(arrays, kinds)]
    return _split_start(name, arrays, shapes, na * N_DEV, issue)


def _scatter_finish(started, arrays, kinds, after, name):
    na = len(arrays)

    def finish(srcs, lands, send_sems, recv_sems):
        me = _me()
        my = _dev_index(me)
        for s in range(N_DEV):
            for a in range(na):
                @pl.when(my != s)
                def _(a=a, s=s):
                    cp = pltpu.make_async_remote_copy(
                        src_ref=_scatter_piece(kinds[a], srcs[a], s), dst_ref=lands[a].at[s],
                        send_sem=send_sems.at[a * N_DEV + s], recv_sem=recv_sems.at[a * N_DEV + s],
                        device_id=me, device_id_type=MESH)
                    cp.wait_recv()
                    cp.wait_send()

    got = _split_wait(name, started, na, after, finish)
    x, y, c = _me()
    chip, my = 2 * x + y, 4 * x + 2 * y + c
    own = {"win": lambda t: lax.dynamic_slice(t, (HALF_IN * c, WIN_STRIDE * chip), (HALF_IN, WIN_W)),
           "wout": lambda t: lax.dynamic_slice(t, (2 * HALF_OUT * chip + HALF_OUT * c, 0), (HALF_OUT, D_MODEL)),
           "small": lambda t: lax.dynamic_index_in_dim(t, my, 0, keepdims=False)}
    return [_put_own(g, own[k](t), my) for g, t, k in zip(got, arrays, kinds)]


def _share_reduced(pair_arrays, small, name):
    arrays = list(pair_arrays) + ([small] if small is not None else [])
    na, npair = len(arrays), len(pair_arrays)
    nm = len(ALL_MASKS)

    def body(*refs):
        srcs, dsts = refs[:na], refs[na:2 * na]
        send_sems, recv_sems = refs[2 * na:]
        me = _me()
        sib = _flip(me, (0, 0, 1))

        def copy(a, k, sender, to):
            slot = sender[2] if a < npair else _dev_index(sender)
            return pltpu.make_async_remote_copy(
                src_ref=srcs[a], dst_ref=dsts[a].at[slot], send_sem=send_sems.at[k], recv_sem=recv_sems.at[k],
                device_id=to, device_id_type=MESH)

        sends = [copy(a, a, me, sib) for a in range(npair)]
        if small is not None:
            sends += [copy(npair, npair + mi, me, _flip(me, mask)) for mi, mask in enumerate(ALL_MASKS)]
        for cp in sends:
            cp.start()
        for a in range(npair):
            copy(a, a, sib, me).wait_recv()
        if small is not None:
            for mi, mask in enumerate(ALL_MASKS):
                copy(npair, npair + mi, _flip(me, mask), me).wait_recv()
        for cp in sends:
            cp.wait_send()

    shapes = [jax.ShapeDtypeStruct((2,) + t.shape, t.dtype) for t in pair_arrays]
    if small is not None:
        shapes.append(jax.ShapeDtypeStruct((N_DEV,) + small.shape, small.dtype))
    got = _comm_call(name, body, arrays, shapes, npair + (nm if small is not None else 0))
    x, y, c = _me()
    slots = [c] * npair + [4 * x + 2 * y + c]
    return [_put_own(g, t, slot) for g, t, slot in zip(got, arrays, slots)]


def _sum_parts(parts, name):
    _, r, c = parts.shape
    tr = r
    while tr * c * 4 * N_DEV > (8 << 20) and tr % 32 == 0:
        tr //= 2

    def body(p_ref, o_ref):
        total = p_ref[0].astype(f32)
        for d in range(1, N_DEV):
            total = total + p_ref[d].astype(f32)
        o_ref[...] = total

    return pl.pallas_call(
        body, name=name, grid=(r // tr,), in_specs=[pl.BlockSpec((N_DEV, tr, c), lambda i: (0, i, 0))],
        out_specs=pl.BlockSpec((tr, c), lambda i: (i, 0)), out_shape=jax.ShapeDtypeStruct((r, c), f32),
        compiler_params=_params(("parallel",)),
    )(parts)


def _adamw(w, g, m, v, name):
    shape = w.shape
    cols = shape[-1]
    slabs = w.ndim == 3 and shape[1] < 8
    rows = w.size // cols
    tr = rows
    while tr * cols * 4 > (1 << 20) and tr % 16 == 0:
        tr //= 2
    c1 = 1.0 / (1.0 - ADAM_B1 ** ADAM_STEP)
    c2 = 1.0 / (1.0 - ADAM_B2 ** ADAM_STEP)

    def body(w_ref, g_ref, m_ref, v_ref, d_ref, nm_ref, nv_ref, *g_out):
        gv = g_ref[...]
        mv = ADAM_B1 * m_ref[...] + (1.0 - ADAM_B1) * gv
        vv = ADAM_B2 * v_ref[...] + (1.0 - ADAM_B2) * (gv * gv)
        nm_ref[...] = mv
        nv_ref[...] = vv
        d_ref[...] = -ADAM_LR * ((mv * c1) / (jnp.sqrt(vv * c2) + ADAM_EPS) + ADAM_WD * w_ref[...])
        for ref in g_out:
            ref[...] = gv

    if slabs:
        tl = max(d for d in range(1, shape[0] + 1) if shape[0] % d == 0 and d * shape[1] * cols * 4 <= (3 << 19))
        spec = pl.BlockSpec((tl, shape[1], cols), lambda i: (i, 0, 0))
        return tuple(pl.pallas_call(
            body, name=name, grid=(shape[0] // tl,), in_specs=[spec] * 4, out_specs=[spec] * 4,
            out_shape=[jax.ShapeDtypeStruct(shape, f32)] * 4, compiler_params=_params(("parallel",)),
        )(w, g, m, v))
    spec = pl.BlockSpec((tr, cols), lambda i: (i, 0))
    outs = pl.pallas_call(
        body, name=name, grid=(rows // tr,), in_specs=[spec] * 4, out_specs=[spec] * 3,
        out_shape=[jax.ShapeDtypeStruct((rows, cols), f32)] * 3, compiler_params=_params(("parallel",)),
    )(*[t.reshape(rows, cols) for t in (w, g, m, v)])
    return tuple(t.reshape(shape) for t in outs)


def _pad_lanes(t, n=HD):
    return jnp.pad(t, [(0, 0)] * (t.ndim - 1) + [(0, n - t.shape[-1])])


def _rows128(t, rows=None):
    t = t.reshape(-1, HD)
    rows = rows or -(-t.shape[0] // 8) * 8
    return jnp.pad(t, ((0, rows - t.shape[0]), (0, 0)))


def kernel(x, norm_w, w_in, lru_conv_w, lru_conv_b, lru_wa, lru_ba, lru_wx, lru_bx, lru_lambda, lru_norm_w, dn_conv_w, dn_A_log, dn_dt_bias, dn_norm_w, w_out, final_norm_w, loss_target, m_norm_w, m_w_in, m_lru_conv_w, m_lru_conv_b, m_lru_wa, m_lru_ba, m_lru_wx, m_lru_bx, m_lru_lambda, m_lru_norm_w, m_dn_conv_w, m_dn_A_log, m_dn_dt_bias, m_dn_norm_w, m_w_out, m_final_norm_w, v_norm_w, v_w_in, v_lru_conv_w, v_lru_conv_b, v_lru_wa, v_lru_ba, v_lru_wx, v_lru_bx, v_lru_lambda, v_lru_norm_w, v_dn_conv_w, v_dn_A_log, v_dn_dt_bias, v_dn_norm_w, v_w_out, v_final_norm_w):
    depth = norm_w.shape[0]
    xs = x[0]
    target = loss_target[0]
    chip = 2 * lax.axis_index("x") + lax.axis_index("y")

    win_b = [w_in[l].astype(bf16) for l in range(depth)]
    wout_b = [w_out[l].astype(bf16) for l in range(depth)]
    got_in0 = _gather_halves([win_b[0]], [], "gather_weights0")[0]
    rest0 = [wout_b[0], lru_conv_w, dn_conv_w]
    rest0_started = _gather_start(rest0, "gather_rest0_start")
    gathered, later = {}, {}
    pad_cols = jnp.zeros((D_MODEL, D_INP - D_IN), bf16)

    def weights_of(l, after):
        if l == 0:
            got, token = got_in0, rest0_started[-1]
        else:
            gathered[l] = _gather_finish(later[l], [win_b[l], wout_b[l]], after, f"gather_weights{l}_wait")
            got, token = gathered[l][0], None
        return jnp.concatenate([got[j] for j in range(4)] + [pad_cols], axis=1), token

    def rest_of(l, after):
        token = None
        if "rest0" not in gathered:
            gathered["rest0"] = _gather_finish(rest0_started, rest0, after, "gather_rest0_wait")
            for k in range(1, depth):
                later[k] = _gather_start([win_b[k], wout_b[k]], f"gather_weights{k}_start", after=gathered["rest0"][0])
                token = later[k][-1] if token is None else token + later[k][-1]
        g_out, g_lcw, g_dcw = gathered["rest0"]
        if l > 0:
            g_out = gathered[l][1]
        lcw_full = g_lcw.transpose(1, 2, 0, 3).reshape(depth, 4, D_MODEL)
        dcw_full = g_dcw.transpose(1, 2, 0, 3).reshape(depth, 4, 3 * D_MODEL)
        return g_out.reshape(D_MIX, D_MODEL), _behind(lcw_full[l], token), dcw_full[l]

    in_flight = {}

    def on_grad(kind, l, g):
        if l == depth - 1 and depth > 1:
            if kind == "wout":
                in_flight["held"] = g
                return None
            srcs, kinds, key = [g, in_flight.pop("held")], ["win", "wout"], ("both", l)
        else:
            srcs, kinds, key = [g], [kind], (kind, l)
        started = _scatter_start(srcs, kinds, f"scatter_{key[0]}{key[1]}_start")
        in_flight[key] = (started, srcs, kinds)
        return started[-1]

    vecs, dnvs = [], []
    zrow = jnp.zeros((D_MODEL,), f32)
    for l in range(depth):
        vecs.append(jnp.stack([lru_conv_b[l], lru_ba[l], lru_bx[l], lru_lambda[l], lru_norm_w[l], zrow, zrow, zrow]))
        dnvs.append(jnp.concatenate([_pad_lanes(dn_A_log[l][None]), _pad_lanes(dn_dt_bias[l][None]),
                                     dn_norm_w[l][None], jnp.zeros((5, HD), f32)], axis=0))

    loss_part, grad_x, d_fnw, g_nw, g_vec, g_wa, g_wx, g_lcw_, g_dcw_, g_dsc = _local_step(
        xs, target, norm_w, final_norm_w, weights_of, rest_of, on_grad, vecs, dnvs, lru_wa, lru_wx)
    loss = lax.psum(loss_part[0, 0], ("x", "y", "c"))
    grads = _reduce_grads(chip, grad_x, in_flight, d_fnw, g_nw, g_vec, g_wa, g_wx, g_lcw_, g_dcw_, g_dsc)

    names = ["norm_w", "w_in", "lru_conv_w", "lru_conv_b", "lru_wa", "lru_ba", "lru_wx", "lru_bx", "lru_lambda",
             "lru_norm_w", "dn_conv_w", "dn_A_log", "dn_dt_bias", "dn_norm_w", "w_out", "final_norm_w"]
    ws = dict(zip(names, [norm_w, w_in, lru_conv_w, lru_conv_b, lru_wa, lru_ba, lru_wx, lru_bx, lru_lambda, lru_norm_w,
                          dn_conv_w, dn_A_log, dn_dt_bias, dn_norm_w, w_out, final_norm_w]))
    ms = dict(zip(names, [m_norm_w, m_w_in, m_lru_conv_w, m_lru_conv_b, m_lru_wa, m_lru_ba, m_lru_wx, m_lru_bx,
                          m_lru_lambda, m_lru_norm_w, m_dn_conv_w, m_dn_A_log, m_dn_dt_bias, m_dn_norm_w, m_w_out,
                          m_final_norm_w]))
    vs = dict(zip(names, [v_norm_w, v_w_in, v_lru_conv_w, v_lru_conv_b, v_lru_wa, v_lru_ba, v_lru_wx, v_lru_bx,
                          v_lru_lambda, v_lru_norm_w, v_dn_conv_w, v_dn_A_log, v_dn_dt_bias, v_dn_norm_w, v_w_out,
                          v_final_norm_w]))
    to_cols = lambda t: jnp.transpose(t, (2, 0, 1))
    from_cols = lambda t: jnp.transpose(t, (1, 2, 0))
    deltas, new_m, new_v = [], [], []
    for n in names:
        if n in ("w_in", "lru_conv_w", "dn_conv_w"):
            view, back = (to_cols, from_cols) if n == "w_in" else (lambda t: t, lambda t: t)
            d, nm_, nv_, grads[n] = (back(t) for t in _adamw(view(ws[n]), view(grads[n]), view(ms[n]), view(vs[n]),
                                                            f"adamw_{n}"))
        else:
            d, nm_, nv_ = _adamw(ws[n], grads[n], ms[n], vs[n], f"adamw_{n}")
        deltas.append(d)
        new_m.append(nm_)
        new_v.append(nv_)
    return (loss, grad_x[None], *[grads[n] for n in names], *deltas, *new_m, *new_v)


def _behind(t, token):
    return t if token is None else t + token[0:1, 0:1]


def _local_step(xs, target, norm_w, final_norm_w, weights_of, rest_of, on_grad, vecs, dnvs, lru_wa, lru_wx):
    depth = norm_w.shape[0]
    saved = []
    h = xs
    for l in range(depth):
        wp_l, token = weights_of(l, h)
        hn = _rms_fwd(h, _behind(norm_w[l][None], token), f"rms_fwd{l}")
        proj = _matmul(hn, wp_l, tm=2048, tn=896, tk=D_MODEL, name=f"in_proj{l}")
        wo_l, lcw_l, dcw_l = rest_of(l, proj)
        y, hl = _lru_fwd(proj, lcw_l, vecs[l], lru_wa[l], lru_wx[l], f"lru_fwd{l}")
        y, o, states, tinvs, cpres = _dn_fwd(proj, dcw_l, dnvs[l], y, f"dn_fwd{l}")
        out = _matmul(y, wo_l, tm=1024, tn=D_MODEL, tk=D_MIX, res=h, name=f"out_proj{l}")
        saved.append((h, hn, proj, hl, o, states, tinvs, cpres, y, wp_l, wo_l, lcw_l, dcw_l))
        h = out

    dh, d_fnw, loss_part = _final_loss(h, final_norm_w[None], target, "final_loss")

    g_nw, g_vec, g_wa, g_wx, g_lcw_, g_dcw_, g_dsc = ([None] * depth for _ in range(7))
    for l in reversed(range(depth)):
        hin, hn, proj, hl, o, states, tinvs, cpres, y, wp_l, wo_l, lcw_l, dcw_l = saved[l]
        dy = _matmul(dh, wo_l, tb=True, tm=2048, tn=1024, tk=D_MODEL, name=f"d_y{l}")
        g_wout = _matmul(y, dh, ta=True, tm=1024, tn=D_MODEL, tk=512, out_dtype=bf16, name=f"d_wout{l}")
        token = on_grad("wout", l, g_wout)
        dlx, dlz, g_lcw_[l], g_vec[l], g_wa[l], g_wx[l] = _lru_bwd(dy, proj, hl, lcw_l, _behind(vecs[l], token),
                                                                 lru_wa[l], lru_wx[l], f"lru_bwd{l}")
        dq, dk, dv, dz, dba, dcw8, g_dsc[l] = _dn_bwd(dy, proj, o, states, tinvs, cpres, dcw_l, dnvs[l], f"dn_bwd{l}")
        g_dcw_[l] = dcw8.reshape(HEADS, 4, 3, HD).transpose(1, 2, 0, 3).reshape(4, 3 * D_MODEL)
        dxs = [dlx, dlz, dq, dk, dv, dz]
        g_win = _d_win(_transpose(hn, f"hn_t{l}"), dxs, dba, f"d_win{l}")
        token = on_grad("win", l, g_win)
        dhn = _d_hn(dxs, _behind(dba, token), wp_l, f"d_hn{l}")
        dh, g_nw[l] = _rms_bwd(dhn, hin, norm_w[l][None], dh, f"rms_bwd{l}")
    return loss_part, dh, d_fnw, g_nw, g_vec, g_wa, g_wx, g_lcw_, g_dcw_, g_dsc


def _reduce_grads(chip, after, in_flight, d_fnw, g_nw, g_vec, g_wa, g_wx, g_lcw_, g_dcw_, g_dsc):
    depth = len(g_nw)
    both = lambda f: jnp.concatenate([f(l) for l in range(depth)])
    small = [
        both(lambda l: _rows128(g_nw[l])),
        both(lambda l: _rows128(g_vec[l][0])), both(lambda l: _rows128(g_vec[l][1])),
        both(lambda l: _rows128(g_vec[l][2])), both(lambda l: _rows128(g_vec[l][3])),
        both(lambda l: _rows128(g_vec[l][4])),
        both(lambda l: _rows128(g_wa[l])), both(lambda l: _rows128(g_wx[l])),
        both(lambda l: _rows128(g_dsc[l][0:1])), both(lambda l: _rows128(g_dsc[l][1:2])),
        both(lambda l: _rows128(g_dsc[l][2:3])),
        _rows128(d_fnw),
        both(lambda l: _rows128(g_lcw_[l])), both(lambda l: _rows128(g_dcw_[l])),
    ]
    counts = [t.shape[0] for t in small]
    total = sum(counts)
    per = -(-total // (8 * N_DEV)) * 8
    small = jnp.concatenate(small + [jnp.zeros((per * N_DEV - total, HD), f32)]).reshape(N_DEV, per, HD)

    small_started = _scatter_start([small], ["small"], "scatter_small_start")
    reduced, pairs = {}, [None] * depth
    for key in sorted(in_flight, key=lambda k: -k[1]):
        started, srcs, kinds = in_flight[key]
        got = _scatter_finish(started, srcs, kinds, after, f"scatter_{key[0]}{key[1]}_wait")
        for kind, part in zip(kinds, got):
            reduced[(kind, key[1])] = _sum_parts(part, f"sum_{kind}{key[1]}")
        after = reduced[(kinds[-1], key[1])]
        if key[1] > 0 and ("win", key[1]) in reduced and ("wout", key[1]) in reduced:
            pairs[key[1]] = _share_reduced([reduced[("win", key[1])], reduced[("wout", key[1])]], None,
                                           f"share_grads{key[1]}")
            after = pairs[key[1]][0]
    got = _scatter_finish(small_started, [small], ["small"], after, "scatter_small_wait")
    shared = _share_reduced([reduced[("win", 0)], reduced[("wout", 0)]], _sum_parts(got[0], "sum_small"), "share_grads0")
    pairs[0] = shared
    a_small = shared[2].reshape(N_DEV * per, HD)
    grad_w_in = jnp.stack([lax.dynamic_slice_in_dim(pairs[l][0].reshape(D_MODEL, WIN_W), 4 * chip, SHARD_IN, 1)
                           for l in range(depth)])
    grad_w_out = jnp.stack([pairs[l][1].reshape(D_MIX // 4, D_MODEL) for l in range(depth)])

    offs = [0]
    for c in counts:
        offs.append(offs[-1] + c)

    def piece(k, rows_per_layer=None):
        t = a_small[offs[k]:offs[k + 1]]
        if rows_per_layer is None:
            return t
        return t.reshape(depth, -1, HD)[:, :rows_per_layer]

    vec = lambda k: piece(k).reshape(depth, D_MODEL)
    return {
        "norm_w": vec(0), "lru_conv_b": vec(1), "lru_ba": vec(2), "lru_bx": vec(3), "lru_lambda": vec(4),
        "lru_norm_w": vec(5),
        "lru_wa": piece(6).reshape(depth, HEADS, HD, HD), "lru_wx": piece(7).reshape(depth, HEADS, HD, HD),
        "dn_A_log": piece(8, 1)[:, 0, :HEADS], "dn_dt_bias": piece(9, 1)[:, 0, :HEADS], "dn_norm_w": piece(10, 1)[:, 0],
        "final_norm_w": piece(11).reshape(D_MODEL),
        "lru_conv_w": lax.dynamic_slice_in_dim(piece(12).reshape(depth, 4, D_MODEL), chip * (D_MODEL // 4), D_MODEL // 4, 2),
        "dn_conv_w": lax.dynamic_slice_in_dim(piece(13).reshape(depth, 4, 3 * D_MODEL), chip * (3 * D_MODEL // 4),
                                              3 * D_MODEL // 4, 2),
        "w_in": grad_w_in, "w_out": grad_w_out,
    }
```

```python
import jax
import jax.numpy as jnp
from jax import lax
from jax.experimental import pallas as pl
from jax.experimental.pallas import tpu as pltpu

f32 = jnp.float32
bf16 = jnp.bfloat16
MESH = pl.DeviceIdType.MESH

D_MODEL = 1024
HEADS = 8
HD = 128
CHUNK = 64
LRU_T = 128
SEQ_BLOCK = 1024
LRU_HP = 4
DN_SEQ_BLOCK = 256
DN_HP = 8
LRU_C = 8.0
D_IN = 6160
D_INP = 6272
D_MIX = 2048
N_GROUPS = 6
BLK_BA = 48
SHARD_IN = D_IN // 4
WIN_W = 1664
WIN_STRIDE = 1536
EPS = 1e-6
QSCALE = HD ** -0.5
N_DEV = 8
VMEM_LIMIT = 56 * 1024 * 1024

ADAM_LR, ADAM_B1, ADAM_B2, ADAM_EPS, ADAM_WD, ADAM_STEP = 0.001, 0.9, 0.999, 1e-08, 0.01, 10


def _sig(x):
    return 1.0 / (1.0 + jnp.exp(-x))


def _log1p(u):
    w = 1.0 + u
    d = w - 1.0
    return jnp.where(d == 0.0, u, jnp.log(w) * (u / jnp.where(d == 0.0, 1.0, d)))


def _softplus(x):
    return jnp.maximum(x, 0.0) + _log1p(jnp.exp(-jnp.abs(x)))


def _expm1(x):
    t = x * (1.0 + x * (0.5 + x * (1.0 / 6 + x * (1.0 / 24 + x * (1.0 / 120 + x * (1.0 / 720))))))
    return jnp.where(jnp.abs(x) < 0.2, t, jnp.exp(x) - 1.0)


def _dot(a, b, prec=None):
    return jnp.dot(a, b, precision=prec, preferred_element_type=f32)


def _dot_nt(a, b, prec=None):
    return lax.dot_general(a, b, (((1,), (1,)), ((), ())), precision=prec, preferred_element_type=f32)


def _dot_tn(a, b, prec=None):
    return lax.dot_general(a, b, (((0,), (0,)), ((), ())), precision=prec, preferred_element_type=f32)


def _b(x):
    return x.astype(bf16)


def _sum0(x):
    return jnp.sum(x, axis=0, keepdims=True)


def _sum1(x):
    return jnp.sum(x, axis=1, keepdims=True)


def _rowmean(x):
    return jnp.mean(x, axis=-1, keepdims=True)


def _dsilu(z, sg):
    return sg * (1.0 + z * (1.0 - sg))


def _conv_taps(x, halo):
    xx = jnp.concatenate([halo, x], axis=0)
    return [pltpu.roll(xx, 3, axis=0)[8:], pltpu.roll(xx, 2, axis=0)[8:], pltpu.roll(xx, 1, axis=0)[8:], x]


def _conv(xs, cw):
    return cw[0:1] * xs[0] + cw[1:2] * xs[1] + cw[2:3] * xs[2] + cw[3:4] * xs[3]


def _rows_before(ref, halo_ref, lanes, t0, c, sblk):
    tprev = pl.multiple_of(jnp.maximum(t0 - 8, 0), 8)
    return jnp.where(c > 0, ref[pl.ds(tprev, 8), lanes], jnp.where(sblk > 0, halo_ref[:, lanes], 0.0))


def _conv_back(dxc, halo_after, cw, x=None):
    rows = dxc.shape[0]
    dd = jnp.concatenate([dxc, halo_after], axis=0)
    out = cw[3:4] * dxc
    dcw = [None, None, None, None if x is None else _sum0(x * dxc)]
    for s in (1, 2, 3):
        shifted = pltpu.roll(dd, rows + 8 - s, axis=0)[:rows]
        out = out + cw[3 - s:4 - s] * shifted
        if x is not None:
            dcw[3 - s] = _sum0(x * shifted)
    return out if x is None else (out, jnp.concatenate(dcw, axis=0))


def _params(sem=None):
    return pltpu.CompilerParams(dimension_semantics=sem, vmem_limit_bytes=VMEM_LIMIT)


def _seq_specs(sb, width, rowblk, colblk):
    main = pl.BlockSpec((sb, width), lambda a, b: (rowblk(a, b), colblk(a, b)))
    halo = pl.BlockSpec((8, width), lambda a, b: (jnp.maximum(rowblk(a, b) * (sb // 8) - 1, 0), colblk(a, b)))
    return main, halo


def _matmul(a, b, *, ta=False, tb=False, tm, tn, tk, res=None, out_dtype=f32, name):
    if ta:
        kdim, m = a.shape
    else:
        m, kdim = a.shape
    n = b.shape[0] if tb else b.shape[1]
    tm, tn, tk = min(tm, m), min(tn, n), min(tk, kdim)
    assert m % tm == 0 and n % tn == 0 and kdim % tk == 0, (name, m, n, kdim, tm, tn, tk)
    nk = kdim // tk
    dims = (((0 if ta else 1,), (1 if tb else 0,)), ((), ()))
    has_res = res is not None

    def body(*refs):
        a_ref, b_ref = refs[:2]
        r_ref = refs[2] if has_res else None
        o_ref = refs[3] if has_res else refs[2]
        acc_ref = refs[-1] if nk > 1 else None
        part = lax.dot_general(_b(a_ref[...]), _b(b_ref[...]), dims, preferred_element_type=f32)

        def finish(total):
            if has_res:
                total = total + r_ref[...]
            o_ref[...] = total.astype(out_dtype)

        if nk == 1:
            finish(part)
        else:
            k = pl.program_id(2)

            @pl.when(k == 0)
            def _():
                acc_ref[...] = part

            @pl.when(k > 0)
            def _():
                acc_ref[...] += part

            @pl.when(k == nk - 1)
            def _():
                finish(acc_ref[...])

    a_spec = pl.BlockSpec((tk, tm), lambda i, j, k: (k, i)) if ta else pl.BlockSpec((tm, tk), lambda i, j, k: (i, k))
    b_spec = pl.BlockSpec((tn, tk), lambda i, j, k: (j, k)) if tb else pl.BlockSpec((tk, tn), lambda i, j, k: (k, j))
    o_spec = pl.BlockSpec((tm, tn), lambda i, j, k: (i, j))
    in_specs, args = [a_spec, b_spec], [a, b]
    if has_res:
        in_specs.append(o_spec)
        args.append(res)
    return pl.pallas_call(
        body, name=name, grid=(m // tm, n // tn, nk), in_specs=in_specs, out_specs=o_spec,
        out_shape=jax.ShapeDtypeStruct((m, n), out_dtype),
        scratch_shapes=[pltpu.VMEM((tm, tn), f32)] if nk > 1 else [],
        compiler_params=_params(("parallel", "parallel", "arbitrary")),
    )(*args)


def _transpose(x, name):
    r, c = x.shape
    tr, tc = min(512, r), min(512, c)

    def body(x_ref, o_ref):
        o_ref[...] = x_ref[...].T

    return pl.pallas_call(
        body, name=name, grid=(r // tr, c // tc), in_specs=[pl.BlockSpec((tr, tc), lambda i, j: (i, j))],
        out_specs=pl.BlockSpec((tc, tr), lambda i, j: (j, i)), out_shape=jax.ShapeDtypeStruct((c, r), x.dtype),
        compiler_params=_params(("parallel", "parallel")),
    )(x)


def _d_hn(dxs, dba, wp, x, w, dres, name):
    s = dxs[0].shape[0]
    tm = min(256, s)

    def body(*refs):
        dx_refs = refs[:N_GROUPS]
        dba_ref, w_ref, x_ref, nw_ref, dres_ref, o_ref, dw_ref = refs[N_GROUPS:]
        dhn = _dot_nt(_b(dba_ref[...]), w_ref[:, BLK_BA * HD:])
        for g in range(N_GROUPS):
            dhn = dhn + _dot_nt(dx_refs[g][...], w_ref[:, g * D_MODEL:(g + 1) * D_MODEL])
        xv = x_ref[...]
        r = lax.rsqrt(_rowmean(xv * xv) + EPS)
        xn = xv * r
        dxn = dhn * nw_ref[...]
        o_ref[...] = dres_ref[...] + r * (dxn - xn * _rowmean(dxn * xn))
        part = _sum0(dhn * xn)

        @pl.when(pl.program_id(0) == 0)
        def _():
            dw_ref[...] = part

        @pl.when(pl.program_id(0) > 0)
        def _():
            dw_ref[...] += part

    row = lambda wd: pl.BlockSpec((tm, wd), lambda i: (i, 0))
    vec = pl.BlockSpec((1, D_MODEL), lambda i: (0, 0))
    return pl.pallas_call(
        body, name=name, grid=(s // tm,),
        in_specs=[row(D_MODEL)] * N_GROUPS + [row(HD), pl.BlockSpec((D_MODEL, D_INP), lambda i: (0, 0)),
                                              row(D_MODEL), vec, row(D_MODEL)],
        out_specs=[row(D_MODEL), vec],
        out_shape=[jax.ShapeDtypeStruct((s, D_MODEL), f32), jax.ShapeDtypeStruct((1, D_MODEL), f32)],
        compiler_params=_params(("arbitrary",)),
    )(*dxs, dba, wp, x, w, dres)


def _d_win(hnt, dxs, dba, name):
    s = hnt.shape[1]
    tn = 256
    per = D_MODEL // tn

    def body(*refs):
        hnt_ref = refs[0]
        dx_refs = refs[1:1 + N_GROUPS]
        o_ref = refs[1 + N_GROUPS]
        j = pl.program_id(0)
        for g in range(N_GROUPS):
            @pl.when(j // per == g)
            def _(g=g):
                o_ref[...] = _dot(hnt_ref[...], _b(dx_refs[g][...])).astype(bf16)

    def dx_spec(g):
        on = lambda j: (j // per == g).astype(jnp.int32)
        return pl.BlockSpec((s, tn), lambda j: (0, (j % per) * on(j)))

    main = pl.pallas_call(
        body, name=name, grid=(N_GROUPS * per,),
        in_specs=[pl.BlockSpec((D_MODEL, s), lambda j: (0, 0))] + [dx_spec(g) for g in range(N_GROUPS)],
        out_specs=pl.BlockSpec((D_MODEL, tn), lambda j: (0, j)),
        out_shape=jax.ShapeDtypeStruct((D_MODEL, D_INP), bf16),
        compiler_params=_params(("parallel",)),
    )(hnt, *dxs)

    def tail(hnt_ref, dba_ref, full_ref, o_ref):
        del full_ref
        o_ref[...] = _dot(hnt_ref[...], _b(dba_ref[...])).astype(bf16)

    return pl.pallas_call(
        tail, name=name + "_ba", grid=(1,),
        in_specs=[pl.BlockSpec((D_MODEL, s), lambda k: (0, 0)), pl.BlockSpec((s, HD), lambda k: (0, 0)),
                  pl.BlockSpec(memory_space=pl.ANY)],
        out_specs=pl.BlockSpec((D_MODEL, HD), lambda k: (0, BLK_BA)),
        out_shape=jax.ShapeDtypeStruct((D_MODEL, D_INP), bf16),
        input_output_aliases={2: 0},
        compiler_params=_params(("arbitrary",)),
    )(hnt, dba, main)


def _rms_fwd(x, w, name):
    s = x.shape[0]
    tr = min(512, s)

    def body(x_ref, w_ref, h_ref):
        xv = x_ref[...]
        r = lax.rsqrt(_rowmean(xv * xv) + EPS)
        h_ref[...] = (xv * r * w_ref[...]).astype(bf16)

    return pl.pallas_call(
        body, name=name, grid=(s // tr,),
        in_specs=[pl.BlockSpec((tr, D_MODEL), lambda i: (i, 0)), pl.BlockSpec((1, D_MODEL), lambda i: (0, 0))],
        out_specs=pl.BlockSpec((tr, D_MODEL), lambda i: (i, 0)),
        out_shape=jax.ShapeDtypeStruct((s, D_MODEL), bf16), compiler_params=_params(("parallel",)),
    )(x, w)


def _final_loss(x, w, target, name):
    s = x.shape[0]
    tr = min(512, s)

    def body(x_ref, w_ref, t_ref, dx_ref, dw_ref, loss_ref):
        xv = x_ref[...]
        wv = w_ref[...]
        r = lax.rsqrt(_rowmean(xv * xv) + EPS)
        xn = xv * r
        e = xn * wv - t_ref[...]
        lb = jnp.broadcast_to(0.5 * _sum0(_rowmean(e * e)), (1, HD))
        dy = e * (1.0 / D_MODEL)
        dxn = dy * wv
        dx_ref[...] = r * (dxn - xn * _rowmean(dxn * xn))
        part = _sum0(dy * xn)

        @pl.when(pl.program_id(0) == 0)
        def _():
            dw_ref[...] = part
            loss_ref[...] = lb

        @pl.when(pl.program_id(0) > 0)
        def _():
            dw_ref[...] += part
            loss_ref[...] += lb

    row = pl.BlockSpec((tr, D_MODEL), lambda i: (i, 0))
    vec = pl.BlockSpec((1, D_MODEL), lambda i: (0, 0))
    lsp = pl.BlockSpec((1, HD), lambda i: (0, 0))
    return pl.pallas_call(
        body, name=name, grid=(s // tr,), in_specs=[row, vec, row], out_specs=[row, vec, lsp],
        out_shape=[jax.ShapeDtypeStruct((s, D_MODEL), f32), jax.ShapeDtypeStruct((1, D_MODEL), f32),
                   jax.ShapeDtypeStruct((1, HD), f32)],
        compiler_params=_params(("arbitrary",)),
    )(x, w, target)


def _lru_gates(xc, vec, sp, wa, wx):
    xcb = _b(xc)
    r = _sig(_dot(xcb, wa) + vec[1:2])
    i = _sig(_dot(xcb, wx) + vec[2:3])
    la = (-LRU_C) * r * sp
    a = jnp.exp(la)
    mult = jnp.sqrt(-_expm1(2.0 * la))
    return r, i, a, mult, sp, xcb


def _lru_fwd(proj, cw, vec, wa, wx, name):
    s = proj.shape[0]
    sb = min(SEQ_BLOCK, s)
    nsb = s // sb
    t = min(LRU_T, sb)
    nc = sb // t
    hp = LRU_HP
    wide = hp * HD
    lanes = [slice(j * HD, (j + 1) * HD) for j in range(hp)]

    def body(x_ref, xh_ref, z_ref, cw_ref, vec_ref, wa_ref, wx_ref, y_ref, hl_ref, hc_ref):
        sblk = pl.program_id(1)
        consts = [(cw_ref[:, hl], vec_ref[:, hl], _softplus(-vec_ref[3:4, hl]), _b(wa_ref[j]), _b(wx_ref[j]))
                  for j, hl in enumerate(lanes)]
        row = lax.broadcasted_iota(jnp.int32, (t, HD), 0)

        @pl.when(sblk == 0)
        def _():
            hc_ref[...] = jnp.zeros((8, wide), f32)

        def one_head(j, x, halo, z, hcarry):
            cwv, vec_v, sp_v, wa_v, wx_v = consts[j]
            xs = _conv_taps(x, halo)
            xc = vec_v[0:1] + _conv(xs, cwv)
            r, i, a, mult, _, _ = _lru_gates(xc, vec_v, sp_v, wa_v, wx_v)
            yield
            bb = mult * (i * xc)
            d = 1
            while d < t:
                m = row >= d
                bb = jnp.where(m, a * pltpu.roll(bb, d, axis=0) + bb, bb)
                a = jnp.where(m, a * pltpu.roll(a, d, axis=0), a)
                d *= 2
                yield
            hl = bb + a * hcarry
            nrm = lax.rsqrt(_rowmean(hl * hl) + EPS)
            return hl, (hl * nrm * vec_v[4:5] * (z * _sig(z))).astype(bf16)

        def chunk(c, carries):
            t0 = pl.multiple_of(c * t, t)
            rows = pl.ds(t0, t)
            ins = [(x_ref[rows, hl], _rows_before(x_ref, xh_ref, hl, t0, c, sblk), z_ref[rows, hl]) for hl in lanes]
            outs = _round_robin([one_head(j, *ins[j], carries[j]) for j in range(hp)])
            for j, hl in enumerate(lanes):
                hl_ref[rows, hl] = outs[j][0]
                y_ref[rows, hl] = outs[j][1]
            return tuple(out[0][t - 1:t, :] for out in outs)

        final = lax.fori_loop(0, nc, chunk, tuple(hc_ref[0:1, hl] for hl in lanes))
        hc_ref[...] = jnp.concatenate([jnp.broadcast_to(f, (8, HD)) for f in final], axis=1)

    xsp, xhalo = _seq_specs(sb, wide, lambda g, i: i, lambda g, i: g)
    zsp, _ = _seq_specs(sb, wide, lambda g, i: i, lambda g, i: HEADS // hp + g)
    gcol = lambda g, i: (0, g)
    wsp = pl.BlockSpec((hp, HD, HD), lambda g, i: (g, 0, 0))
    osp = pl.BlockSpec((sb, wide), lambda g, i: (i, g))
    return pl.pallas_call(
        body, name=name, grid=(HEADS // hp, nsb),
        in_specs=[xsp, xhalo, zsp, pl.BlockSpec((4, wide), gcol), pl.BlockSpec((8, wide), gcol), wsp, wsp],
        out_specs=[osp, osp],
        out_shape=[jax.ShapeDtypeStruct((s, D_MIX), bf16), jax.ShapeDtypeStruct((s, D_MODEL), f32)],
        scratch_shapes=[pltpu.VMEM((8, wide), f32)],
        compiler_params=_params(("parallel", "arbitrary")),
    )(proj, proj, proj, cw, vec, wa, wx)


def _lru_bwd(dy, proj, hl, cw, vec, wa, wx, name):
    s = proj.shape[0]
    sb = min(SEQ_BLOCK, s)
    nsb = s // sb
    t = min(LRU_T, sb)
    nc = sb // t
    hp = LRU_HP
    wide = hp * HD
    lanes = [slice(j * HD, (j + 1) * HD) for j in range(hp)]
    n_acc = 10

    def body(dy_ref, x_ref, xh_ref, z_ref, hl_ref, hlh_ref, cw_ref, vec_ref, wa_ref, wx_ref,
             dx_ref, dz_ref, dcw_ref, dvec_ref, dwa_ref, dwx_ref, acc_ref, dxh_ref):
        step = pl.program_id(1)
        sblk = nsb - 1 - step
        consts = [(cw_ref[:, hl], vec_ref[:, hl], _softplus(-vec_ref[3:4, hl]), _b(wa_ref[j]), _b(wx_ref[j]))
                  for j, hl in enumerate(lanes)]
        row = lax.broadcasted_iota(jnp.int32, (t, HD), 0)

        @pl.when(step == 0)
        def _():
            acc_ref[...] = jnp.zeros((16, wide), f32)
            dxh_ref[...] = jnp.zeros((8, wide), f32)
            dwa_ref[...] = jnp.zeros((hp, HD, HD), f32)
            dwx_ref[...] = jnp.zeros((hp, HD, HD), f32)

        def one_head(j, x, halo, z, hv, hhalo, dyv, carry):
            dcw0, dcw1, dcw2, dcw3, dcb, dba, dbx, dsp, dlnw, mu, dxc_halo = carry
            cwv, vec_v, sp_v, wa_v, wx_v = consts[j]
            lnw = vec_v[4:5]
            xs = _conv_taps(x, halo)
            xc = vec_v[0:1] + _conv(xs, cwv)
            r, i, a, mult, sp, xcb = _lru_gates(xc, vec_v, sp_v, wa_v, wx_v)
            yield
            hprev = pltpu.roll(jnp.concatenate([hhalo, hv], axis=0), 1, axis=0)[8:]
            sgz = _sig(z)
            sz = z * sgz
            nrm = lax.rsqrt(_rowmean(hv * hv) + EPS)
            hn = hv * nrm
            dlnw = dlnw + _sum0(dyv * hn * sz)
            dzv = _b(dyv * hn * lnw * _dsilu(z, sgz))
            dhn = dyv * lnw * sz
            lam = nrm * (dhn - hn * _rowmean(dhn * hn))
            cf = jnp.where(row == t - 1, 1.0, pltpu.roll(a, t - 1, axis=0))
            d = 1
            while d < t:
                m = row < t - d
                lam = jnp.where(m, lam + cf * pltpu.roll(lam, t - d, axis=0), lam)
                cf = jnp.where(m, cf * pltpu.roll(cf, t - d, axis=0), cf)
                d *= 2
                yield
            lam = lam + cf * mu
            mu = a[0:1] * lam[0:1]
            da = lam * hprev
            dmult = lam * (i * xc)
            di = lam * mult * xc
            dxc = lam * mult * i
            dla = da * a - dmult * (a * a) / mult
            dr = dla * (-LRU_C * sp)
            dsp = dsp + _sum0(dla * (-LRU_C * r))
            dra = dr * r * (1.0 - r)
            dia = di * i * (1.0 - i)
            dba = dba + _sum0(dra)
            dbx = dbx + _sum0(dia)
            drab, diab = _b(dra), _b(dia)
            dwa = _dot_tn(xcb, drab)
            dwx = _dot_tn(xcb, diab)
            dxc = dxc + _dot_nt(drab, wa_v) + _dot_nt(diab, wx_v)
            yield
            dcb = dcb + _sum0(dxc)
            dcw0 = dcw0 + _sum0(dxc * xs[0])
            dcw1 = dcw1 + _sum0(dxc * xs[1])
            dcw2 = dcw2 + _sum0(dxc * xs[2])
            dcw3 = dcw3 + _sum0(dxc * xs[3])
            dxv = _b(_conv_back(dxc, dxc_halo, cwv))
            return dxv, dzv, dwa, dwx, (dcw0, dcw1, dcw2, dcw3, dcb, dba, dbx, dsp, dlnw, mu, dxc[0:8])

        def chunk(ci, carries):
            c = nc - 1 - ci
            t0 = pl.multiple_of(c * t, t)
            rows = pl.ds(t0, t)
            ins = [(x_ref[rows, hl], _rows_before(x_ref, xh_ref, hl, t0, c, sblk), z_ref[rows, hl], hl_ref[rows, hl],
                    _rows_before(hl_ref, hlh_ref, hl, t0, c, sblk), dy_ref[rows, hl]) for hl in lanes]
            outs = _round_robin([one_head(j, *ins[j], carries[j]) for j in range(hp)])
            for j, hl in enumerate(lanes):
                dx_ref[rows, hl] = outs[j][0]
                dz_ref[rows, hl] = outs[j][1]
                dwa_ref[j] += outs[j][2]
                dwx_ref[j] += outs[j][3]
            return tuple(out[4] for out in outs)

        acc = acc_ref[...]
        dxh = dxh_ref[...]
        init = tuple(tuple(acc[k:k + 1, hl] for k in range(n_acc)) + (dxh[:, hl],) for hl in lanes)
        final = lax.fori_loop(0, nc, chunk, init)
        rows_out = [jnp.concatenate([final[j][k] for j in range(hp)], axis=1) for k in range(n_acc)]
        zero = jnp.zeros((1, wide), f32)
        acc_ref[...] = jnp.concatenate(rows_out + [zero] * (16 - n_acc), axis=0)
        dxh_ref[...] = jnp.concatenate([final[j][n_acc] for j in range(hp)], axis=1)

        @pl.when(step == nsb - 1)
        def _():
            dcw_ref[...] = jnp.concatenate(rows_out[0:4], axis=0)
            dlam = -rows_out[7] * _sig(-vec_ref[3:4, :])
            dvec_ref[...] = jnp.concatenate([rows_out[4], rows_out[5], rows_out[6], dlam, rows_out[8], zero, zero, zero],
                                            axis=0)

    rblk = lambda g, i: nsb - 1 - i
    xsp, xhalo = _seq_specs(sb, wide, rblk, lambda g, i: g)
    zsp, _ = _seq_specs(sb, wide, rblk, lambda g, i: HEADS // hp + g)
    gcol = lambda g, i: (0, g)
    wsp = pl.BlockSpec((hp, HD, HD), lambda g, i: (g, 0, 0))
    return pl.pallas_call(
        body, name=name, grid=(HEADS // hp, nsb),
        in_specs=[xsp, xsp, xhalo, zsp, xsp, xhalo, pl.BlockSpec((4, wide), gcol), pl.BlockSpec((8, wide), gcol), wsp, wsp],
        out_specs=[xsp, xsp, pl.BlockSpec((4, wide), gcol), pl.BlockSpec((8, wide), gcol), wsp, wsp],
        out_shape=[jax.ShapeDtypeStruct((s, D_MODEL), bf16), jax.ShapeDtypeStruct((s, D_MODEL), bf16),
                   jax.ShapeDtypeStruct((4, D_MODEL), f32), jax.ShapeDtypeStruct((8, D_MODEL), f32),
                   jax.ShapeDtypeStruct((HEADS, HD, HD), f32), jax.ShapeDtypeStruct((HEADS, HD, HD), f32)],
        scratch_shapes=[pltpu.VMEM((16, wide), f32), pltpu.VMEM((8, wide), f32)],
        compiler_params=_params(("parallel", "arbitrary")),
    )(dy, proj, proj, proj, hl, hl, cw, vec, wa, wx)


def _lane_pick(x, lane, idx):
    return _sum1(jnp.where(lane == idx, x, 0.0))


def _round_robin(gens):
    results = [None] * len(gens)
    live = list(range(len(gens)))
    while live:
        for i in list(live):
            try:
                next(gens[i])
            except StopIteration as done:
                results[i] = done.value
                live.remove(i)
    return results


def _sdot(a, b):
    return _dot(_b(a), _b(b))


def _sdot_tn(a, b):
    return _dot_tn(_b(a), _b(b))


def _sdot_nt(a, b):
    return _dot_nt(_b(a), _b(b))


def _tri_dot(tri, x):
    trib = tri.astype(bf16)
    x1 = x.astype(bf16)
    r1 = x - x1.astype(f32)
    x2 = r1.astype(bf16)
    x3 = (r1 - x2.astype(f32)).astype(bf16)
    return _dot(trib, x1) + _dot(trib, x2) + _dot(trib, x3)


def _inv_unit_lower(a):
    n = -a
    p = _sdot(a, a)
    yield
    for k in range(5):
        n = n + p + _sdot(n, p)
        if k < 4:
            p = _sdot(p, p)
        yield
    return n


def _dn_gates(bav, dnv, masks):
    ltri = masks[4]
    ea = jnp.exp(pltpu.roll(dnv[0:1], HEADS, axis=1))
    dt = pltpu.roll(dnv[1:2], HEADS, axis=1)
    u = bav + dt
    g = -ea * _softplus(u)
    return dict(beta=_sig(bav), g=g, gc=_tri_dot(ltri, g), sga=_sig(u), ea=ea)


def _dn_chunk(x3, halo3, gates, cw3, h, masks, cpre=None, tinv=None):
    lane, ri, ci, eye, ltri = masks
    if cpre is None:
        cpre = _conv(_conv_taps(x3, halo3), cw3)
    sg = _sig(cpre)
    act = cpre * sg
    q_raw, k_raw, v = act[:, 0:HD], act[:, HD:2 * HD], act[:, 2 * HD:3 * HD]
    rq = lax.rsqrt(_sum1(q_raw * q_raw) + EPS)
    rk = lax.rsqrt(_sum1(k_raw * k_raw) + EPS)
    qn = q_raw * rq
    k = k_raw * rk
    q = qn * QSCALE
    beta = jnp.broadcast_to(_lane_pick(gates["beta"], lane, h), (CHUNK, HD))
    gc = jnp.broadcast_to(_lane_pick(gates["gc"], lane, HEADS + h), (CHUNK, HD))
    yield
    gc64 = gc[:, 0:CHUNK]
    grow = _sum0(gc64 * eye)
    dm = jnp.exp(jnp.where(ri >= ci, gc64 - grow, -1e30))
    ds_ = jnp.where(ri > ci, dm, 0.0)
    eg = jnp.exp(gc)
    glast = gc[CHUNK - 1:CHUNK, :]
    ek = jnp.exp(glast - gc)
    kb = k * beta
    kbf = _b(k)
    amat = _dot_nt(_b(kb), kbf) * ds_
    pmat = _dot_nt(_b(q), kbf) * dm
    yield
    if tinv is None:
        tinv = yield from _inv_unit_lower(amat)
    return dict(cpre=cpre, sg=sg, rq=rq, rk=rk, qn=qn, q=q, k=k, v=v, kbf=kbf, beta=beta, gc=gc, dm=dm, ds=ds_, eg=eg,
                glast=glast, ek=ek, kb=kb, amat=amat, tinv=tinv, pmat=pmat)


def _dn_masks():
    lane = lax.broadcasted_iota(jnp.int32, (CHUNK, HD), 1)
    ri = lax.broadcasted_iota(jnp.int32, (CHUNK, CHUNK), 0)
    ci = lax.broadcasted_iota(jnp.int32, (CHUNK, CHUNK), 1)
    eye = (ri == ci).astype(f32)
    ltri = (ri >= ci).astype(f32)
    return lane, ri, ci, eye, ltri


def _dn_load_qkv(q_ref, qh_ref, k_ref, kh_ref, v_ref, vh_ref, hl, rows, t0, c, sblk):
    x3 = jnp.concatenate([q_ref[rows, hl], k_ref[rows, hl], v_ref[rows, hl]], axis=1)
    halo3 = jnp.concatenate([_rows_before(q_ref, qh_ref, hl, t0, c, sblk), _rows_before(k_ref, kh_ref, hl, t0, c, sblk),
                             _rows_before(v_ref, vh_ref, hl, t0, c, sblk)], axis=1)
    return x3, halo3


def _dn_cw3(cwq_ref, cwk_ref, cwv_ref, j):
    return jnp.concatenate([r[:, j * HD:(j + 1) * HD] for r in (cwq_ref, cwk_ref, cwv_ref)], axis=1)


def _dn_fwd(proj, cw, dnv, y, name):
    s = proj.shape[0]
    sb = min(DN_SEQ_BLOCK, s)
    nsb = s // sb
    nc = sb // CHUNK
    hp = DN_HP

    def body(q_ref, qh_ref, k_ref, kh_ref, v_ref, vh_ref, z_ref, ba_ref, cwq_ref, cwk_ref, cwv_ref, dnv_ref, yin_ref,
             y_ref, o_ref, st_ref, ti_ref, cp_ref, s_ref):
        del yin_ref
        grp = pl.program_id(0)
        sblk = pl.program_id(1)
        masks = _dn_masks()
        dnv_v = dnv_ref[...]
        cw3 = [_dn_cw3(cwq_ref, cwk_ref, cwv_ref, j) for j in range(hp)]

        @pl.when(sblk == 0)
        def _():
            s_ref[...] = jnp.zeros((hp, HD, HD), f32)

        def one_head(j, x3, halo3, gates, z, st):
            f = yield from _dn_chunk(x3, halo3, gates, cw3[j], grp * hp + j, masks)
            sbf = _b(st)
            qs = _dot(_b(f["q"] * f["eg"]), sbf)
            rhs = f["beta"] * (f["v"] - _dot(_b(f["k"] * f["eg"]), sbf))
            yield
            vn = rhs + _sdot(f["tinv"], rhs)
            yield
            vnb = _b(vn)
            o = qs + _dot(_b(f["pmat"]), vnb)
            st_new = st * jnp.exp(f["glast"]) + _dot_tn(_b(f["k"] * f["ek"]), vnb)
            yield
            nrm = lax.rsqrt(_rowmean(o * o) + EPS)
            yv = (o * nrm * dnv_v[2:3] * (z * _sig(z))).astype(bf16)
            return o, yv, st_new, f["tinv"], f["cpre"]

        def chunk(c, states):
            t0 = pl.multiple_of(c * CHUNK, CHUNK)
            rows = pl.ds(t0, CHUNK)
            bav = ba_ref[rows, :]
            ins = []
            for j in range(hp):
                hl = slice(j * HD, (j + 1) * HD)
                ins.append(_dn_load_qkv(q_ref, qh_ref, k_ref, kh_ref, v_ref, vh_ref, hl, rows, t0, c, sblk)
                           + (z_ref[rows, hl],))
            gates = _dn_gates(bav, dnv_v, masks)
            outs = _round_robin([one_head(j, ins[j][0], ins[j][1], gates, ins[j][2], states[j]) for j in range(hp)])
            for j in range(hp):
                hl = slice(j * HD, (j + 1) * HD)
                st_ref[j, c] = states[j]
                o_ref[rows, hl] = outs[j][0]
                y_ref[rows, hl] = outs[j][1]
                ti_ref[j, c] = outs[j][3]
                cp_ref[rows, j * 3 * HD:(j + 1) * 3 * HD] = outs[j][4]
            return tuple(out[2] for out in outs)

        final = lax.fori_loop(0, nc, chunk, tuple(s_ref[j] for j in range(hp)))
        for j in range(hp):
            s_ref[j] = final[j]

    wide = hp * HD
    grp_specs = lambda off: _seq_specs(sb, wide, lambda g, i: i, lambda g, i: off // hp + g)
    (qsp, qhalo), (ksp, khalo), (vsp, vhalo) = grp_specs(2 * HEADS), grp_specs(3 * HEADS), grp_specs(4 * HEADS)
    zsp, _ = grp_specs(5 * HEADS)
    basp = pl.BlockSpec((sb, HD), lambda g, i: (i, BLK_BA))
    cws = lambda off: pl.BlockSpec((4, wide), lambda g, i: (0, off // hp + g))
    osp = lambda off: pl.BlockSpec((sb, wide), lambda g, i: (i, off // hp + g))
    return pl.pallas_call(
        body, name=name, grid=(HEADS // hp, nsb),
        in_specs=[qsp, qhalo, ksp, khalo, vsp, vhalo, zsp, basp, cws(0), cws(HEADS), cws(2 * HEADS),
                  pl.BlockSpec((8, HD), lambda g, i: (0, 0)), pl.BlockSpec(memory_space=pl.ANY)],
        out_specs=[osp(HEADS), osp(0), pl.BlockSpec((hp, nc, HD, HD), lambda g, i: (g, i, 0, 0)),
                   pl.BlockSpec((hp, nc, CHUNK, CHUNK), lambda g, i: (g, i, 0, 0)),
                   pl.BlockSpec((sb, 3 * wide), lambda g, i: (i, g))],
        out_shape=[jax.ShapeDtypeStruct((s, D_MIX), bf16), jax.ShapeDtypeStruct((s, D_MODEL), f32),
                   jax.ShapeDtypeStruct((HEADS, s // CHUNK, HD, HD), f32),
                   jax.ShapeDtypeStruct((HEADS, s // CHUNK, CHUNK, CHUNK), f32),
                   jax.ShapeDtypeStruct((s, 3 * D_MODEL), f32)],
        scratch_shapes=[pltpu.VMEM((hp, HD, HD), f32)],
        input_output_aliases={12: 0},
        compiler_params=_params(("parallel", "arbitrary")),
    )(proj, proj, proj, proj, proj, proj, proj, proj, cw, cw, cw, dnv, y)


def _dn_bwd(dy, proj, o, states, tinvs, cpres, cw, dnv, name):
    s = proj.shape[0]
    sb = min(DN_SEQ_BLOCK, s)
    nsb = s // sb
    nc = sb // CHUNK
    hp = DN_HP
    ngrp = HEADS // hp

    def body(dy_ref, q_ref, k_ref, v_ref, z_ref, ba_ref, o_ref, st_ref, ti_ref, cp_ref,
             cwq_ref, cwk_ref, cwv_ref, dnv_ref,
             dq_ref, dk_ref, dv_ref, dz_ref, dba_ref, dcw_ref, dsc_ref,
             ds_ref, dch_ref, cwacc_ref, nacc_ref):
        step = pl.program_id(0)
        grp = pl.program_id(1)
        sblk = nsb - 1 - step
        h0 = grp * hp
        masks = _dn_masks()
        lane, ri, ci, eye, ltri = masks
        utri = (ri <= ci).astype(f32)
        cw3s = [_dn_cw3(cwq_ref, cwk_ref, cwv_ref, j) for j in range(hp)]
        dnv_v = dnv_ref[...]
        dnw = dnv_v[2:3]
        row64 = lax.broadcasted_iota(jnp.int32, (CHUNK, HD), 0)

        @pl.when(step == 0)
        def _():
            for j in range(hp):
                ds_ref[h0 + j] = jnp.zeros((HD, HD), f32)
                dch_ref[h0 + j] = jnp.zeros((8, 3 * HD), f32)
                cwacc_ref[h0 + j] = jnp.zeros((8, 3 * HD), f32)

        @pl.when((step == 0) & (grp == 0))
        def _():
            nacc_ref[...] = jnp.zeros((8, HD), f32)

        def one_head(j, x3, gates, z, st, ti, cp, ov, dyv, carry):
            dst, dch, cwacc = carry
            cw3 = cw3s[j]
            f = yield from _dn_chunk(None, None, gates, cw3, h0 + j, masks, cp, ti)
            q, k, v, beta, eg, ek, kbf = f["q"], f["k"], f["v"], f["beta"], f["eg"], f["ek"], f["kbf"]
            sbf = _b(st)
            dstb = _b(dst)
            qe = q * eg
            keg = k * eg
            kd = k * ek
            kegb, qeb, kdb = _b(keg), _b(qe), _b(kd)
            e = v - _dot(kegb, sbf)
            yield
            rhs = beta * e
            vn = rhs + _sdot(f["tinv"], rhs)
            yield
            vnb = _b(vn)
            pb = _b(f["pmat"])
            sgz = _sig(z)
            sz = z * sgz
            nrm = lax.rsqrt(_rowmean(ov * ov) + EPS)
            on = ov * nrm
            ddnw = _sum0(dyv * on * sz)
            dpz = dyv * on * dnw * _dsilu(z, sgz)
            don = dyv * dnw * sz
            do = nrm * (don - on * _rowmean(don * on))
            dob = _b(do)
            dvn = _dot_tn(pb, dob) + _dot(kdb, dstb)
            dpm = jnp.where(ri >= ci, _dot_nt(dob, vnb), 0.0)
            dqe = _dot_nt(dob, sbf)
            dkd = _dot_nt(vnb, dstb)
            qtdo = _dot_tn(qeb, dob)
            yield
            drhs = dvn + _sdot_tn(f["tinv"], dvn)
            dqk = _b(dpm * f["dm"])
            dq = _dot(dqk, kbf) + dqe * eg
            dk_p = _dot_tn(dqk, _b(q))
            yield
            dam = jnp.where(ri > ci, -_sdot_nt(drhs, vn), 0.0)
            de = drhs * beta
            deb = _b(de)
            dbeta = _sum1(drhs * e)
            dkeg = -_dot_nt(deb, sbf)
            dst_new = qtdo + dst * jnp.exp(f["glast"]) - _dot_tn(kegb, deb)
            yield
            dkk = _b(dam * f["ds"])
            dkb = _dot(dkk, kbf)
            dk = dk_p + _dot_tn(dkk, _b(f["kb"])) + dkb * beta + dkeg * eg + dkd * ek
            yield
            dbeta = dbeta + _sum1(dkb * k)
            mm = dpm * f["pmat"] + dam * f["amat"]
            dglast = _sum1(_sum0(dst * st)) * jnp.exp(f["glast"]) + _sum1(_sum0(dkd * kd))
            dgc = (_sum1(mm) - _sum1(eye * _sum0(mm)) + _sum1(dqe * qe) + _sum1(dkeg * keg) - _sum1(dkd * kd))
            dgc = jnp.broadcast_to(dgc, (CHUNK, HD)) + jnp.where(row64 == CHUNK - 1, dglast, 0.0)
            dq_raw = (QSCALE * f["rq"]) * (dq - f["qn"] * _sum1(f["qn"] * dq))
            dk_raw = f["rk"] * (dk - k * _sum1(k * dk))
            dact = jnp.concatenate([dq_raw, dk_raw, de], axis=1)
            dc = dact * _dsilu(f["cpre"], f["sg"])
            dqkv, dcw_part = _conv_back(dc, dch, cw3, x3)
            cwacc = cwacc + dcw_part
            return dpz, dqkv, (dgc, dbeta), ddnw, (dst_new, dc[0:8], cwacc)

        def chunk(cidx, carry):
            heads, nacc, sa0, sa1 = carry
            c = nc - 1 - cidx
            t0 = pl.multiple_of(c * CHUNK, CHUNK)
            rows = pl.ds(t0, CHUNK)
            bav = ba_ref[rows, :]
            ins = []
            for j in range(hp):
                hl = slice(j * HD, (j + 1) * HD)
                x3 = jnp.concatenate([q_ref[rows, hl], k_ref[rows, hl], v_ref[rows, hl]], axis=1)
                ins.append((x3, z_ref[rows, hl], st_ref[j, c], ti_ref[j, c],
                            cp_ref[rows, j * 3 * HD:(j + 1) * 3 * HD], o_ref[rows, hl], dy_ref[rows, hl]))
            gates = _dn_gates(bav, dnv_v, masks)
            outs = _round_robin([one_head(j, ins[j][0], gates, *ins[j][1:], heads[j]) for j in range(hp)])
            dgc_all = jnp.zeros((CHUNK, HD), f32)
            dbeta_all = jnp.zeros((CHUNK, HD), f32)
            for j in range(hp):
                hl = slice(j * HD, (j + 1) * HD)
                dpz, dqkv, (dgc, dbeta), ddnw, _ = outs[j]
                dgc_all = jnp.where(lane == HEADS + h0 + j, dgc, dgc_all)
                dbeta_all = jnp.where(lane == h0 + j, dbeta, dbeta_all)
                dq_ref[rows, hl] = _b(dqkv[:, 0:HD])
                dk_ref[rows, hl] = _b(dqkv[:, HD:2 * HD])
                dv_ref[rows, hl] = _b(dqkv[:, 2 * HD:3 * HD])
                dz_ref[rows, hl] = _b(dpz)
                nacc = nacc + ddnw
            dg_all = _tri_dot(utri, dgc_all)
            dain_all = dg_all * (-gates["ea"]) * gates["sga"]
            dba = dbeta_all * gates["beta"] * (1.0 - gates["beta"]) + dain_all
            sa0 = sa0 + _sum0(dg_all * gates["g"])
            sa1 = sa1 + _sum0(dain_all)

            @pl.when(grp == 0)
            def _():
                dba_ref[rows, :] = dba

            @pl.when(grp > 0)
            def _():
                dba_ref[rows, :] += dba

            return tuple(out[4] for out in outs), nacc, sa0, sa1

        init = tuple((ds_ref[h0 + j], dch_ref[h0 + j], cwacc_ref[h0 + j][0:4]) for j in range(hp))
        heads, nacc, sa0, sa1 = lax.fori_loop(
            0, nc, chunk, (init, nacc_ref[0:1, :], nacc_ref[1:2, :], nacc_ref[2:3, :]))
        nacc_ref[0:3, :] = jnp.concatenate([nacc, sa0, sa1], axis=0)
        zpad = jnp.zeros((4, 3 * HD), f32)
        for j in range(hp):
            dst, dch, cwacc = heads[j]
            ds_ref[h0 + j] = dst
            dch_ref[h0 + j] = dch
            cwacc_ref[h0 + j] = jnp.concatenate([cwacc, zpad], axis=0)

        @pl.when(step == nsb - 1)
        def _():
            for j in range(hp):
                dcw_ref[h0 + j] = heads[j][2]

            @pl.when(grp == ngrp - 1)
            def _():
                dsc_ref[...] = jnp.concatenate([pltpu.roll(sa0, HD - HEADS, axis=1), pltpu.roll(sa1, HD - HEADS, axis=1),
                                                nacc, jnp.zeros((5, HD), f32)], axis=0)

    wide = hp * HD
    rblk = lambda i, g: nsb - 1 - i
    grp_specs = lambda off: _seq_specs(sb, wide, rblk, lambda i, g: off // hp + g)
    (qsp, qhalo), (ksp, khalo), (vsp, vhalo) = grp_specs(2 * HEADS), grp_specs(3 * HEADS), grp_specs(4 * HEADS)
    zsp, _ = grp_specs(5 * HEADS)
    hsp = lambda off: pl.BlockSpec((sb, wide), lambda i, g: (rblk(i, g), off // hp + g))
    basp = lambda col: pl.BlockSpec((sb, HD), lambda i, g: (rblk(i, g), col))
    cws = lambda off: pl.BlockSpec((4, wide), lambda i, g: (0, off // hp + g))
    whole = lambda shape: pl.BlockSpec(shape, lambda i, g: (0,) * len(shape))
    return pl.pallas_call(
        body, name=name, grid=(nsb, ngrp),
        in_specs=[hsp(HEADS), qsp, ksp, vsp, zsp, basp(BLK_BA), hsp(0),
                  pl.BlockSpec((hp, nc, HD, HD), lambda i, g: (g, rblk(i, g), 0, 0)),
                  pl.BlockSpec((hp, nc, CHUNK, CHUNK), lambda i, g: (g, rblk(i, g), 0, 0)),
                  pl.BlockSpec((sb, 3 * wide), lambda i, g: (rblk(i, g), g)),
                  cws(0), cws(HEADS), cws(2 * HEADS), whole((8, HD))],
        out_specs=[hsp(0), hsp(0), hsp(0), hsp(0), basp(0), whole((HEADS, 4, 3 * HD)), whole((8, HD))],
        out_shape=[jax.ShapeDtypeStruct((s, D_MODEL), bf16)] * 4
        + [jax.ShapeDtypeStruct((s, HD), f32), jax.ShapeDtypeStruct((HEADS, 4, 3 * HD), f32),
           jax.ShapeDtypeStruct((8, HD), f32)],
        scratch_shapes=[pltpu.VMEM((HEADS, HD, HD), f32), pltpu.VMEM((HEADS, 8, 3 * HD), f32),
                        pltpu.VMEM((HEADS, 8, 3 * HD), f32), pltpu.VMEM((8, HD), f32)],
        compiler_params=_params(("arbitrary", "arbitrary")),
    )(dy, proj, proj, proj, proj, proj, o, states, tinvs, cpres, cw, cw, cw, dnv)


ANY = pl.BlockSpec(memory_space=pl.ANY)
CHIP_MASKS = ((1, 0, 0), (0, 1, 0), (1, 1, 0))
ALL_MASKS = tuple((m >> 2 & 1, m >> 1 & 1, m & 1) for m in range(1, N_DEV))


def _me():
    return lax.axis_index("x"), lax.axis_index("y"), lax.axis_index("c")


def _flip(me, mask):
    return tuple((1 - v) if m else v for v, m in zip(me, mask))


def _dev_index(dev):
    return 4 * dev[0] + 2 * dev[1] + dev[2]


def _chip(dev):
    return 2 * dev[0] + dev[1]


def _comm_call(name, body, arrays, out_shapes, nsem, nloc=0):
    scratch = [pltpu.SemaphoreType.DMA((nsem,)), pltpu.SemaphoreType.DMA((nsem,))]
    if nloc:
        scratch.append(pltpu.SemaphoreType.DMA((nloc,)))
    return pl.pallas_call(
        body, name=name, in_specs=[ANY] * len(arrays), out_specs=[ANY] * len(out_shapes), out_shape=out_shapes,
        scratch_shapes=scratch, compiler_params=pltpu.CompilerParams(has_side_effects=True),
    )(*arrays)


def _put_own(gathered, own, slot):
    return lax.dynamic_update_index_in_dim(gathered, own, slot, 0)


def _gather_halves(arrays, whole_arrays, name):
    na, nw = len(arrays), len(whole_arrays)
    nm = len(CHIP_MASKS)

    def body(*refs):
        srcs, dsts = refs[:na + nw], refs[na + nw:2 * (na + nw)]
        send_sems, recv_sems = refs[2 * (na + nw):]
        me = _me()
        sib = _flip(me, (0, 0, 1))

        def half(a, ref, core):
            rows = arrays[a].shape[0] // 2
            return ref.at[pl.ds(pl.multiple_of(core * rows, rows), rows)]

        def over_ici(a, mi, sender, to):
            if a < na:
                src, dst = half(a, srcs[a], sender[2]), half(a, dsts[a].at[_chip(sender)], sender[2])
            else:
                src, dst = srcs[a], dsts[a].at[_chip(sender)]
            return pltpu.make_async_remote_copy(src_ref=src, dst_ref=dst, send_sem=send_sems.at[a * nm + mi],
                                                recv_sem=recv_sems.at[a * nm + mi], device_id=to, device_id_type=MESH)

        def over_d2d(a, mi, origin, sender, to):
            k = (na + nw) * nm + a * nm + mi
            blk = half(a, dsts[a].at[_chip(origin)], sender[2])
            return pltpu.make_async_remote_copy(src_ref=blk, dst_ref=blk, send_sem=send_sems.at[k],
                                                recv_sem=recv_sems.at[k], device_id=to, device_id_type=MESH)

        sends = []
        for a in range(na + nw):
            for mi, mask in enumerate(CHIP_MASKS):
                cp = over_ici(a, mi, me, _flip(me, mask))
                cp.start()
                sends.append(cp)
        for a in range(na + nw):
            for mi, mask in enumerate(CHIP_MASKS):
                peer = _flip(me, mask)
                over_ici(a, mi, peer, me).wait_recv()
                if a < na:
                    cp = over_d2d(a, mi, peer, me, sib)
                    cp.start()
                    sends.append(cp)
        for a in range(na):
            for mi, mask in enumerate(CHIP_MASKS):
                over_d2d(a, mi, _flip(sib, mask), sib, me).wait_recv()
        for cp in sends:
            cp.wait_send()

    every = list(arrays) + list(whole_arrays)
    got = _comm_call(name, body, every, [jax.ShapeDtypeStruct((4,) + t.shape, t.dtype) for t in every],
                     (2 * na + nw) * nm)
    chip = 2 * lax.axis_index("x") + lax.axis_index("y")
    return [_put_own(g, t, chip) for g, t in zip(got, every)]


HBM = pl.BlockSpec(memory_space=pltpu.HBM)
SEM = pl.BlockSpec(memory_space=pltpu.SEMAPHORE)
DATAFLOW = pltpu.SideEffectType.DATAFLOW_SIDE_EFFECTING


def _split_start(name, srcs, land_shapes, nsem, issue, after=None):
    ns, nl = len(srcs), len(land_shapes)
    n_in = ns + nl + (after is not None)

    def body(*refs):
        issue(refs[:ns], refs[ns:ns + nl], refs[n_in + ns + nl], refs[n_in + ns + nl + 1])
        token = refs[-1]
        token[...] = jnp.zeros_like(token)

    lands = [lax.empty(t.shape, t.dtype) for t in land_shapes]
    thru = [pltpu.HBM(t.shape, t.dtype) for t in list(srcs) + list(land_shapes)]
    hbm = lambda t: pltpu.with_memory_space_constraint(t, pltpu.HBM)
    return pl.pallas_call(
        body, name=name, in_specs=[HBM] * (ns + nl) + [ANY] * (after is not None),
        out_shape=(*thru, pltpu.SemaphoreType.DMA((nsem,)), pltpu.SemaphoreType.DMA((nsem,)),
                   jax.ShapeDtypeStruct((8, HD), f32)),
        out_specs=(*[HBM] * (ns + nl), SEM, SEM, pl.BlockSpec(memory_space=pltpu.VMEM)),
        input_output_aliases={i: i for i in range(ns + nl)},
        compiler_params=pltpu.CompilerParams(has_side_effects=DATAFLOW),
    )(*[hbm(t) for t in srcs], *[hbm(t) for t in lands], *([after] if after is not None else []))


def _split_wait(name, started, ns, after, finish):
    thru, send_sems, recv_sems = started[:-3], started[-3], started[-2]
    nt = len(thru)

    def body(*refs):
        finish(refs[:ns], refs[ns:nt], refs[nt], refs[nt + 1])

    outs = pl.pallas_call(
        body, name=name, in_specs=[HBM] * nt + [SEM, SEM, ANY],
        out_shape=tuple(pltpu.HBM(t.shape, t.dtype) for t in thru), out_specs=tuple([HBM] * nt),
        input_output_aliases={i: i for i in range(nt)},
        compiler_params=pltpu.CompilerParams(has_side_effects=DATAFLOW),
    )(*thru, send_sems, recv_sems, after)
    return list(outs[ns:])


def _gather_start(arrays, name, after=None):
    nm = len(CHIP_MASKS)

    def issue(srcs, lands, send_sems, recv_sems):
        me = _me()
        for a in range(len(arrays)):
            for mi, mask in enumerate(CHIP_MASKS):
                pltpu.make_async_remote_copy(
                    src_ref=srcs[a], dst_ref=lands[a].at[_chip(me)], send_sem=send_sems.at[a * nm + mi],
                    recv_sem=recv_sems.at[a * nm + mi], device_id=_flip(me, mask), device_id_type=MESH).start()

    shapes = [jax.ShapeDtypeStruct((4,) + t.shape, t.dtype) for t in arrays]
    return _split_start(name, arrays, shapes, len(arrays) * nm, issue, after)


def _gather_finish(started, arrays, after, name):
    nm = len(CHIP_MASKS)

    def finish(srcs, lands, send_sems, recv_sems):
        me = _me()
        for a in range(len(arrays)):
            for mi, mask in enumerate(CHIP_MASKS):
                peer = _flip(me, mask)
                cp = pltpu.make_async_remote_copy(
                    src_ref=srcs[a], dst_ref=lands[a].at[_chip(peer)], send_sem=send_sems.at[a * nm + mi],
                    recv_sem=recv_sems.at[a * nm + mi], device_id=peer, device_id_type=MESH)
                cp.wait_send()
                cp.wait_recv()

    got = _split_wait(name, started, len(arrays), after, finish)
    chip = 2 * lax.axis_index("x") + lax.axis_index("y")
    return [_put_own(g, t, chip) for g, t in zip(got, arrays)]


HALF_IN, HALF_OUT = D_MODEL // 2, D_MIX // 8


def _scatter_piece(kind, ref, d):
    j, r = d >> 1, d & 1
    if kind == "win":
        return ref.at[pl.ds(HALF_IN * r, HALF_IN), pl.ds(WIN_STRIDE * j, WIN_W)]
    if kind == "wout":
        return ref.at[pl.ds(2 * HALF_OUT * j + HALF_OUT * r, HALF_OUT)]
    return ref.at[d]


def _scatter_start(arrays, kinds, name):
    na = len(arrays)

    def issue(srcs, lands, send_sems, recv_sems):
        me = _me()
        my = _dev_index(me)
        for d in range(N_DEV):
            dev = (d >> 2, (d >> 1) & 1, d & 1)
            for a in range(na):
                @pl.when(my != d)
                def _(a=a, d=d, dev=dev):
                    pltpu.make_async_remote_copy(
                        src_ref=_scatter_piece(kinds[a], srcs[a], d), dst_ref=lands[a].at[my],
                        send_sem=send_sems.at[a * N_DEV + d], recv_sem=recv_sems.at[a * N_DEV + my],
                        device_id=dev, device_id_type=MESH).start()

    shape_of = {"win": (N_DEV, HALF_IN, WIN_W), "wout": (N_DEV, HALF_OUT, D_MODEL)}
    shapes = [jax.ShapeDtypeStruct(shape_of.get(k, t.shape), t.dtype) for t, k in zip(arrays, kinds)]
    return _split_start(name, arrays, shapes, na * N_DEV, issue)


def _scatter_finish(started, arrays, kinds, after, name):
    na = len(arrays)

    def finish(srcs, lands, send_sems, recv_sems):
        me = _me()
        my = _dev_index(me)
        for s in range(N_DEV):
            for a in range(na):
                @pl.when(my != s)
                def _(a=a, s=s):
                    cp = pltpu.make_async_remote_copy(
                        src_ref=_scatter_piece(kinds[a], srcs[a], s), dst_ref=lands[a].at[s],
                        send_sem=send_sems.at[a * N_DEV + s], recv_sem=recv_sems.at[a * N_DEV + s],
                        device_id=me, device_id_type=MESH)
                    cp.wait_recv()
                    cp.wait_send()

    got = _split_wait(name, started, na, after, finish)
    x, y, c = _me()
    chip, my = 2 * x + y, 4 * x + 2 * y + c
    own = {"win": lambda t: lax.dynamic_slice(t, (HALF_IN * c, WIN_STRIDE * chip), (HALF_IN, WIN_W)),
           "wout": lambda t: lax.dynamic_slice(t, (2 * HALF_OUT * chip + HALF_OUT * c, 0), (HALF_OUT, D_MODEL)),
           "small": lambda t: lax.dynamic_index_in_dim(t, my, 0, keepdims=False)}
    return [_put_own(g, own[k](t), my) for g, t, k in zip(got, arrays, kinds)]


def _share_reduced(pair_arrays, small, name):
    arrays = list(pair_arrays) + ([small] if small is not None else [])
    na, npair = len(arrays), len(pair_arrays)
    nm = len(ALL_MASKS)

    def body(*refs):
        srcs, dsts = refs[:na], refs[na:2 * na]
        send_sems, recv_sems = refs[2 * na:]
        me = _me()
        sib = _flip(me, (0, 0, 1))

        def copy(a, k, sender, to):
            slot = sender[2] if a < npair else _dev_index(sender)
            return pltpu.make_async_remote_copy(
                src_ref=srcs[a], dst_ref=dsts[a].at[slot], send_sem=send_sems.at[k], recv_sem=recv_sems.at[k],
                device_id=to, device_id_type=MESH)

        sends = [copy(a, a, me, sib) for a in range(npair)]
        if small is not None:
            sends += [copy(npair, npair + mi, me, _flip(me, mask)) for mi, mask in enumerate(ALL_MASKS)]
        for cp in sends:
            cp.start()
        for a in range(npair):
            copy(a, a, sib, me).wait_recv()
        if small is not None:
            for mi, mask in enumerate(ALL_MASKS):
                copy(npair, npair + mi, _flip(me, mask), me).wait_recv()
        for cp in sends:
            cp.wait_send()

    shapes = [jax.ShapeDtypeStruct((2,) + t.shape, t.dtype) for t in pair_arrays]
    if small is not None:
        shapes.append(jax.ShapeDtypeStruct((N_DEV,) + small.shape, small.dtype))
    got = _comm_call(name, body, arrays, shapes, npair + (nm if small is not None else 0))
    x, y, c = _me()
    slots = [c] * npair + [4 * x + 2 * y + c]
    return [_put_own(g, t, slot) for g, t, slot in zip(got, arrays, slots)]


def _sum_parts(parts, name):
    _, r, c = parts.shape
    tr = r
    while tr * c * 4 * N_DEV > (8 << 20) and tr % 32 == 0:
        tr //= 2

    def body(p_ref, o_ref):
        total = p_ref[0].astype(f32)
        for d in range(1, N_DEV):
            total = total + p_ref[d].astype(f32)
        o_ref[...] = total

    return pl.pallas_call(
        body, name=name, grid=(r // tr,), in_specs=[pl.BlockSpec((N_DEV, tr, c), lambda i: (0, i, 0))],
        out_specs=pl.BlockSpec((tr, c), lambda i: (i, 0)), out_shape=jax.ShapeDtypeStruct((r, c), f32),
        compiler_params=_params(("parallel",)),
    )(parts)


def _adamw(w, g, m, v, name):
    shape = w.shape
    cols = shape[-1]
    slabs = w.ndim == 3 and shape[1] < 8
    rows = w.size // cols
    tr = rows
    while tr * cols * 4 > (1 << 20) and tr % 16 == 0:
        tr //= 2
    c1 = 1.0 / (1.0 - ADAM_B1 ** ADAM_STEP)
    c2 = 1.0 / (1.0 - ADAM_B2 ** ADAM_STEP)

    def body(w_ref, g_ref, m_ref, v_ref, d_ref, nm_ref, nv_ref, *g_out):
        gv = g_ref[...]
        mv = ADAM_B1 * m_ref[...] + (1.0 - ADAM_B1) * gv
        vv = ADAM_B2 * v_ref[...] + (1.0 - ADAM_B2) * (gv * gv)
        nm_ref[...] = mv
        nv_ref[...] = vv
        d_ref[...] = -ADAM_LR * ((mv * c1) / (jnp.sqrt(vv * c2) + ADAM_EPS) + ADAM_WD * w_ref[...])
        for ref in g_out:
            ref[...] = gv

    if slabs:
        tl = max(d for d in range(1, shape[0] + 1) if shape[0] % d == 0 and d * shape[1] * cols * 4 <= (3 << 19))
        spec = pl.BlockSpec((tl, shape[1], cols), lambda i: (i, 0, 0))
        return tuple(pl.pallas_call(
            body, name=name, grid=(shape[0] // tl,), in_specs=[spec] * 4, out_specs=[spec] * 4,
            out_shape=[jax.ShapeDtypeStruct(shape, f32)] * 4, compiler_params=_params(("parallel",)),
        )(w, g, m, v))
    spec = pl.BlockSpec((tr, cols), lambda i: (i, 0))
    outs = pl.pallas_call(
        body, name=name, grid=(rows // tr,), in_specs=[spec] * 4, out_specs=[spec] * 3,
        out_shape=[jax.ShapeDtypeStruct((rows, cols), f32)] * 3, compiler_params=_params(("parallel",)),
    )(*[t.reshape(rows, cols) for t in (w, g, m, v)])
    return tuple(t.reshape(shape) for t in outs)


def _pad_lanes(t, n=HD):
    return jnp.pad(t, [(0, 0)] * (t.ndim - 1) + [(0, n - t.shape[-1])])


def _rows128(t, rows=None):
    t = t.reshape(-1, HD)
    rows = rows or -(-t.shape[0] // 8) * 8
    return jnp.pad(t, ((0, rows - t.shape[0]), (0, 0)))


def kernel(x, norm_w, w_in, lru_conv_w, lru_conv_b, lru_wa, lru_ba, lru_wx, lru_bx, lru_lambda, lru_norm_w, dn_conv_w, dn_A_log, dn_dt_bias, dn_norm_w, w_out, final_norm_w, loss_target, m_norm_w, m_w_in, m_lru_conv_w, m_lru_conv_b, m_lru_wa, m_lru_ba, m_lru_wx, m_lru_bx, m_lru_lambda, m_lru_norm_w, m_dn_conv_w, m_dn_A_log, m_dn_dt_bias, m_dn_norm_w, m_w_out, m_final_norm_w, v_norm_w, v_w_in, v_lru_conv_w, v_lru_conv_b, v_lru_wa, v_lru_ba, v_lru_wx, v_lru_bx, v_lru_lambda, v_lru_norm_w, v_dn_conv_w, v_dn_A_log, v_dn_dt_bias, v_dn_norm_w, v_w_out, v_final_norm_w):
    depth = norm_w.shape[0]
    xs = x[0]
    target = loss_target[0]
    chip = 2 * lax.axis_index("x") + lax.axis_index("y")

    win_b = [w_in[l].astype(bf16) for l in range(depth)]
    wout_b = [w_out[l].astype(bf16) for l in range(depth)]
    got_in0 = _gather_halves([win_b[0]], [], "gather_weights0")[0]
    rest0 = [wout_b[0], lru_conv_w, dn_conv_w]
    rest0_started = _gather_start(rest0, "gather_rest0_start")
    gathered, later = {}, {}
    pad_cols = jnp.zeros((D_MODEL, D_INP - D_IN), bf16)

    def weights_of(l, after):
        if l == 0:
            got, token = got_in0, rest0_started[-1]
        else:
            gathered[l] = _gather_finish(later[l], [win_b[l], wout_b[l]], after, f"gather_weights{l}_wait")
            got, token = gathered[l][0], None
        return jnp.concatenate([got[j] for j in range(4)] + [pad_cols], axis=1), token

    def rest_of(l, after):
        token = None
        if "rest0" not in gathered:
            gathered["rest0"] = _gather_finish(rest0_started, rest0, after, "gather_rest0_wait")
            for k in range(1, depth):
                later[k] = _gather_start([win_b[k], wout_b[k]], f"gather_weights{k}_start", after=gathered["rest0"][0])
                token = later[k][-1] if token is None else token + later[k][-1]
        g_out, g_lcw, g_dcw = gathered["rest0"]
        if l > 0:
            g_out = gathered[l][1]
        lcw_full = g_lcw.transpose(1, 2, 0, 3).reshape(depth, 4, D_MODEL)
        dcw_full = g_dcw.transpose(1, 2, 0, 3).reshape(depth, 4, 3 * D_MODEL)
        return g_out.reshape(D_MIX, D_MODEL), _behind(lcw_full[l], token), dcw_full[l]

    in_flight = {}

    def on_grad(kind, l, g):
        if l == depth - 1 and depth > 1:
            if kind == "wout":
                in_flight["held"] = g
                return None
            srcs, kinds, key = [g, in_flight.pop("held")], ["win", "wout"], ("both", l)
        else:
            srcs, kinds, key = [g], [kind], (kind, l)
        started = _scatter_start(srcs, kinds, f"scatter_{key[0]}{key[1]}_start")
        in_flight[key] = (started, srcs, kinds)
        return started[-1]

    vecs, dnvs = [], []
    zrow = jnp.zeros((D_MODEL,), f32)
    for l in range(depth):
        vecs.append(jnp.stack([lru_conv_b[l], lru_ba[l], lru_bx[l], lru_lambda[l], lru_norm_w[l], zrow, zrow, zrow]))
        dnvs.append(jnp.concatenate([_pad_lanes(dn_A_log[l][None]), _pad_lanes(dn_dt_bias[l][None]),
                                     dn_norm_w[l][None], jnp.zeros((5, HD), f32)], axis=0))

    loss_part, grad_x, d_fnw, g_nw, g_vec, g_wa, g_wx, g_lcw_, g_dcw_, g_dsc = _local_step(
        xs, target, norm_w, final_norm_w, weights_of, rest_of, on_grad, vecs, dnvs, lru_wa, lru_wx)
    loss = lax.psum(loss_part[0, 0], ("x", "y", "c"))
    grads = _reduce_grads(chip, grad_x, in_flight, d_fnw, g_nw, g_vec, g_wa, g_wx, g_lcw_, g_dcw_, g_dsc)

    names = ["norm_w", "w_in", "lru_conv_w", "lru_conv_b", "lru_wa", "lru_ba", "lru_wx", "lru_bx", "lru_lambda",
             "lru_norm_w", "dn_conv_w", "dn_A_log", "dn_dt_bias", "dn_norm_w", "w_out", "final_norm_w"]
    ws = dict(zip(names, [norm_w, w_in, lru_conv_w, lru_conv_b, lru_wa, lru_ba, lru_wx, lru_bx, lru_lambda, lru_norm_w,
                          dn_conv_w, dn_A_log, dn_dt_bias, dn_norm_w, w_out, final_norm_w]))
    ms = dict(zip(names, [m_norm_w, m_w_in, m_lru_conv_w, m_lru_conv_b, m_lru_wa, m_lru_ba, m_lru_wx, m_lru_bx,
                          m_lru_lambda, m_lru_norm_w, m_dn_conv_w, m_dn_A_log, m_dn_dt_bias, m_dn_norm_w, m_w_out,
                          m_final_norm_w]))
    vs = dict(zip(names, [v_norm_w, v_w_in, v_lru_conv_w, v_lru_conv_b, v_lru_wa, v_lru_ba, v_lru_wx, v_lru_bx,
                          v_lru_lambda, v_lru_norm_w, v_dn_conv_w, v_dn_A_log, v_dn_dt_bias, v_dn_norm_w, v_w_out,
                          v_final_norm_w]))
    to_cols = lambda t: jnp.transpose(t, (2, 0, 1))
    from_cols = lambda t: jnp.transpose(t, (1, 2, 0))
    deltas, new_m, new_v = [], [], []
    for n in names:
        if n in ("w_in", "lru_conv_w", "dn_conv_w"):
            view, back = (to_cols, from_cols) if n == "w_in" else (lambda t: t, lambda t: t)
            d, nm_, nv_, grads[n] = (back(t) for t in _adamw(view(ws[n]), view(grads[n]), view(ms[n]), view(vs[n]),
                                                            f"adamw_{n}"))
        else:
            d, nm_, nv_ = _adamw(ws[n], grads[n], ms[n], vs[n], f"adamw_{n}")
        deltas.append(d)
        new_m.append(nm_)
        new_v.append(nv_)
    return (loss, grad_x[None], *[grads[n] for n in names], *deltas, *new_m, *new_v)


def _behind(t, token):
    return t if token is None else t + token[0:1, 0:1]


def _local_step(xs, target, norm_w, final_norm_w, weights_of, rest_of, on_grad, vecs, dnvs, lru_wa, lru_wx):
    depth = norm_w.shape[0]
    saved = []
    h = xs
    for l in range(depth):
        wp_l, token = weights_of(l, h)
        hn = _rms_fwd(h, _behind(norm_w[l][None], token), f"rms_fwd{l}")
        proj = _matmul(hn, wp_l, tm=2048, tn=896, tk=D_MODEL, name=f"in_proj{l}")
        wo_l, lcw_l, dcw_l = rest_of(l, proj)
        y, hl = _lru_fwd(proj, lcw_l, vecs[l], lru_wa[l], lru_wx[l], f"lru_fwd{l}")
        y, o, states, tinvs, cpres = _dn_fwd(proj, dcw_l, dnvs[l], y, f"dn_fwd{l}")
        out = _matmul(y, wo_l, tm=1024, tn=D_MODEL, tk=D_MIX, res=h, name=f"out_proj{l}")
        saved.append((h, hn, proj, hl, o, states, tinvs, cpres, y, wp_l, wo_l, lcw_l, dcw_l))
        h = out

    dh, d_fnw, loss_part = _final_loss(h, final_norm_w[None], target, "final_loss")

    g_nw, g_vec, g_wa, g_wx, g_lcw_, g_dcw_, g_dsc = ([None] * depth for _ in range(7))
    for l in reversed(range(depth)):
        hin, hn, proj, hl, o, states, tinvs, cpres, y, wp_l, wo_l, lcw_l, dcw_l = saved[l]
        dy = _matmul(dh, wo_l, tb=True, tm=2048, tn=1024, tk=D_MODEL, name=f"d_y{l}")
        g_wout = _matmul(y, dh, ta=True, tm=1024, tn=D_MODEL, tk=512, out_dtype=bf16, name=f"d_wout{l}")
        token = on_grad("wout", l, g_wout)
        dlx, dlz, g_lcw_[l], g_vec[l], g_wa[l], g_wx[l] = _lru_bwd(dy, proj, hl, lcw_l, _behind(vecs[l], token),
                                                                 lru_wa[l], lru_wx[l], f"lru_bwd{l}")
        dq, dk, dv, dz, dba, dcw8, g_dsc[l] = _dn_bwd(dy, proj, o, states, tinvs, cpres, dcw_l, dnvs[l], f"dn_bwd{l}")
        g_dcw_[l] = dcw8.reshape(HEADS, 4, 3, HD).transpose(1, 2, 0, 3).reshape(4, 3 * D_MODEL)
        dxs = [dlx, dlz, dq, dk, dv, dz]
        g_win = _d_win(_transpose(hn, f"hn_t{l}"), dxs, dba, f"d_win{l}")
        token = on_grad("win", l, g_win)
        dh, g_nw[l] = _d_hn(dxs, _behind(dba, token), wp_l, hin, norm_w[l][None], dh, f"d_hn{l}")
    return loss_part, dh, d_fnw, g_nw, g_vec, g_wa, g_wx, g_lcw_, g_dcw_, g_dsc


def _reduce_grads(chip, after, in_flight, d_fnw, g_nw, g_vec, g_wa, g_wx, g_lcw_, g_dcw_, g_dsc):
    depth = len(g_nw)
    both = lambda f: jnp.concatenate([f(l) for l in range(depth)])
    small = [
        both(lambda l: _rows128(g_nw[l])),
        both(lambda l: _rows128(g_vec[l][0])), both(lambda l: _rows128(g_vec[l][1])),
        both(lambda l: _rows128(g_vec[l][2])), both(lambda l: _rows128(g_vec[l][3])),
        both(lambda l: _rows128(g_vec[l][4])),
        both(lambda l: _rows128(g_wa[l])), both(lambda l: _rows128(g_wx[l])),
        both(lambda l: _rows128(g_dsc[l][0:1])), both(lambda l: _rows128(g_dsc[l][1:2])),
        both(lambda l: _rows128(g_dsc[l][2:3])),
        _rows128(d_fnw),
        both(lambda l: _rows128(g_lcw_[l])), both(lambda l: _rows128(g_dcw_[l])),
    ]
    counts = [t.shape[0] for t in small]
    total = sum(counts)
    per = -(-total // (8 * N_DEV)) * 8
    small = jnp.concatenate(small + [jnp.zeros((per * N_DEV - total, HD), f32)]).reshape(N_DEV, per, HD)

    small_started = _scatter_start([small], ["small"], "scatter_small_start")
    reduced, pairs = {}, [None] * depth
    for key in sorted(in_flight, key=lambda k: -k[1]):
        started, srcs, kinds = in_flight[key]
        got = _scatter_finish(started, srcs, kinds, after, f"scatter_{key[0]}{key[1]}_wait")
        for kind, part in zip(kinds, got):
            reduced[(kind, key[1])] = _sum_parts(part, f"sum_{kind}{key[1]}")
        after = reduced[(kinds[-1], key[1])]
        if key[1] > 0 and ("win", key[1]) in reduced and ("wout", key[1]) in reduced:
            pairs[key[1]] = _share_reduced([reduced[("win", key[1])], reduced[("wout", key[1])]], None,
                                           f"share_grads{key[1]}")
            after = pairs[key[1]][0]
    got = _scatter_finish(small_started, [small], ["small"], after, "scatter_small_wait")
    shared = _share_reduced([reduced[("win", 0)], reduced[("wout", 0)]], _sum_parts(got[0], "sum_small"), "share_grads0")
    pairs[0] = shared
    a_small = shared[2].reshape(N_DEV * per, HD)
    grad_w_in = jnp.stack([lax.dynamic_slice_in_dim(pairs[l][0].reshape(D_MODEL, WIN_W), 4 * chip, SHARD_IN, 1)
                           for l in range(depth)])
    grad_w_out = jnp.stack([pairs[l][1].reshape(D_MIX // 4, D_MODEL) for l in range(depth)])

    offs = [0]
    for c in counts:
        offs.append(offs[-1] + c)

    def piece(k, rows_per_layer=None):
        t = a_small[offs[k]:offs[k + 1]]
        if rows_per_layer is None:
            return t
        return t.reshape(depth, -1, HD)[:, :rows_per_layer]

    vec = lambda k: piece(k).reshape(depth, D_MODEL)
    return {
        "norm_w": vec(0), "lru_conv_b": vec(1), "lru_ba": vec(2), "lru_bx": vec(3), "lru_lambda": vec(4),
        "lru_norm_w": vec(5),
        "lru_wa": piece(6).reshape(depth, HEADS, HD, HD), "lru_wx": piece(7).reshape(depth, HEADS, HD, HD),
        "dn_A_log": piece(8, 1)[:, 0, :HEADS], "dn_dt_bias": piece(9, 1)[:, 0, :HEADS], "dn_norm_w": piece(10, 1)[:, 0],
        "final_norm_w": piece(11).reshape(D_MODEL),
        "lru_conv_w": lax.dynamic_slice_in_dim(piece(12).reshape(depth, 4, D_MODEL), chip * (D_MODEL // 4), D_MODEL // 4, 2),
        "dn_conv_w": lax.dynamic_slice_in_dim(piece(13).reshape(depth, 4, 3 * D_MODEL), chip * (3 * D_MODEL // 4),
                                              3 * D_MODEL // 4, 2),
        "w_in": grad_w_in, "w_out": grad_w_out,
    }
```

```python
import jax
import jax.numpy as jnp
from jax import lax
from jax.experimental import pallas as pl
from jax.experimental.pallas import tpu as pltpu

f32 = jnp.float32
bf16 = jnp.bfloat16
MESH = pl.DeviceIdType.MESH

D_MODEL = 1024
HEADS = 8
HD = 128
CHUNK = 64
LRU_T = 128
SEQ_BLOCK = 1024
LRU_HP = 4
DN_SEQ_BLOCK = 256
DN_HP = 8
LRU_C = 8.0
D_IN = 6160
D_INP = 6272
D_MIX = 2048
N_GROUPS = 6
BLK_BA = 48
SHARD_IN = D_IN // 4
WIN_W = 1664
WIN_STRIDE = 1536
EPS = 1e-6
QSCALE = HD ** -0.5
N_DEV = 8
VMEM_LIMIT = 56 * 1024 * 1024

ADAM_LR, ADAM_B1, ADAM_B2, ADAM_EPS, ADAM_WD, ADAM_STEP = 0.001, 0.9, 0.999, 1e-08, 0.01, 10


def _sig(x):
    return 1.0 / (1.0 + jnp.exp(-x))


def _log1p(u):
    w = 1.0 + u
    d = w - 1.0
    return jnp.where(d == 0.0, u, jnp.log(w) * (u / jnp.where(d == 0.0, 1.0, d)))


def _softplus(x):
    return jnp.maximum(x, 0.0) + _log1p(jnp.exp(-jnp.abs(x)))


def _expm1(x):
    t = x * (1.0 + x * (0.5 + x * (1.0 / 6 + x * (1.0 / 24 + x * (1.0 / 120 + x * (1.0 / 720))))))
    return jnp.where(jnp.abs(x) < 0.2, t, jnp.exp(x) - 1.0)


def _dot(a, b, prec=None):
    return jnp.dot(a, b, precision=prec, preferred_element_type=f32)


def _dot_nt(a, b, prec=None):
    return lax.dot_general(a, b, (((1,), (1,)), ((), ())), precision=prec, preferred_element_type=f32)


def _dot_tn(a, b, prec=None):
    return lax.dot_general(a, b, (((0,), (0,)), ((), ())), precision=prec, preferred_element_type=f32)


def _b(x):
    return x.astype(bf16)


def _sum0(x):
    return jnp.sum(x, axis=0, keepdims=True)


def _sum1(x):
    return jnp.sum(x, axis=1, keepdims=True)


def _rowmean(x):
    return jnp.mean(x, axis=-1, keepdims=True)


def _dsilu(z, sg):
    return sg * (1.0 + z * (1.0 - sg))


def _conv_taps(x, halo):
    xx = jnp.concatenate([halo, x], axis=0)
    return [pltpu.roll(xx, 3, axis=0)[8:], pltpu.roll(xx, 2, axis=0)[8:], pltpu.roll(xx, 1, axis=0)[8:], x]


def _conv(xs, cw):
    return cw[0:1] * xs[0] + cw[1:2] * xs[1] + cw[2:3] * xs[2] + cw[3:4] * xs[3]


def _rows_before(ref, halo_ref, lanes, t0, c, sblk):
    tprev = pl.multiple_of(jnp.maximum(t0 - 8, 0), 8)
    return jnp.where(c > 0, ref[pl.ds(tprev, 8), lanes], jnp.where(sblk > 0, halo_ref[:, lanes], 0.0))


def _conv_back(dxc, halo_after, cw, x=None):
    rows = dxc.shape[0]
    dd = jnp.concatenate([dxc, halo_after], axis=0)
    out = cw[3:4] * dxc
    dcw = [None, None, None, None if x is None else _sum0(x * dxc)]
    for s in (1, 2, 3):
        shifted = pltpu.roll(dd, rows + 8 - s, axis=0)[:rows]
        out = out + cw[3 - s:4 - s] * shifted
        if x is not None:
            dcw[3 - s] = _sum0(x * shifted)
    return out if x is None else (out, jnp.concatenate(dcw, axis=0))


def _params(sem=None):
    return pltpu.CompilerParams(dimension_semantics=sem, vmem_limit_bytes=VMEM_LIMIT)


def _seq_specs(sb, width, rowblk, colblk):
    main = pl.BlockSpec((sb, width), lambda a, b: (rowblk(a, b), colblk(a, b)))
    halo = pl.BlockSpec((8, width), lambda a, b: (jnp.maximum(rowblk(a, b) * (sb // 8) - 1, 0), colblk(a, b)))
    return main, halo


def _matmul(a, b, *, ta=False, tb=False, tm, tn, tk, res=None, out_dtype=f32, name):
    if ta:
        kdim, m = a.shape
    else:
        m, kdim = a.shape
    n = b.shape[0] if tb else b.shape[1]
    tm, tn, tk = min(tm, m), min(tn, n), min(tk, kdim)
    assert m % tm == 0 and n % tn == 0 and kdim % tk == 0, (name, m, n, kdim, tm, tn, tk)
    nk = kdim // tk
    dims = (((0 if ta else 1,), (1 if tb else 0,)), ((), ()))
    has_res = res is not None

    def body(*refs):
        a_ref, b_ref = refs[:2]
        r_ref = refs[2] if has_res else None
        o_ref = refs[3] if has_res else refs[2]
        acc_ref = refs[-1] if nk > 1 else None
        part = lax.dot_general(_b(a_ref[...]), _b(b_ref[...]), dims, preferred_element_type=f32)

        def finish(total):
            if has_res:
                total = total + r_ref[...]
            o_ref[...] = total.astype(out_dtype)

        if nk == 1:
            finish(part)
        else:
            k = pl.program_id(2)

            @pl.when(k == 0)
            def _():
                acc_ref[...] = part

            @pl.when(k > 0)
            def _():
                acc_ref[...] += part

            @pl.when(k == nk - 1)
            def _():
                finish(acc_ref[...])

    a_spec = pl.BlockSpec((tk, tm), lambda i, j, k: (k, i)) if ta else pl.BlockSpec((tm, tk), lambda i, j, k: (i, k))
    b_spec = pl.BlockSpec((tn, tk), lambda i, j, k: (j, k)) if tb else pl.BlockSpec((tk, tn), lambda i, j, k: (k, j))
    o_spec = pl.BlockSpec((tm, tn), lambda i, j, k: (i, j))
    in_specs, args = [a_spec, b_spec], [a, b]
    if has_res:
        in_specs.append(o_spec)
        args.append(res)
    return pl.pallas_call(
        body, name=name, grid=(m // tm, n // tn, nk), in_specs=in_specs, out_specs=o_spec,
        out_shape=jax.ShapeDtypeStruct((m, n), out_dtype),
        scratch_shapes=[pltpu.VMEM((tm, tn), f32)] if nk > 1 else [],
        compiler_params=_params(("parallel", "parallel", "arbitrary")),
    )(*args)


def _d_hn(dxs, dba, wp, x, w, dres, name):
    s = dxs[0].shape[0]
    tm = min(256, s)

    def body(*refs):
        dx_refs = refs[:N_GROUPS]
        dba_ref, w_ref, x_ref, nw_ref, dres_ref, o_ref, dw_ref = refs[N_GROUPS:]
        dhn = _dot_nt(_b(dba_ref[...]), w_ref[:, BLK_BA * HD:])
        for g in range(N_GROUPS):
            dhn = dhn + _dot_nt(dx_refs[g][...], w_ref[:, g * D_MODEL:(g + 1) * D_MODEL])
        xv = x_ref[...]
        r = lax.rsqrt(_rowmean(xv * xv) + EPS)
        xn = xv * r
        dxn = dhn * nw_ref[...]
        o_ref[...] = dres_ref[...] + r * (dxn - xn * _rowmean(dxn * xn))
        part = _sum0(dhn * xn)

        @pl.when(pl.program_id(0) == 0)
        def _():
            dw_ref[...] = part

        @pl.when(pl.program_id(0) > 0)
        def _():
            dw_ref[...] += part

    row = lambda wd: pl.BlockSpec((tm, wd), lambda i: (i, 0))
    vec = pl.BlockSpec((1, D_MODEL), lambda i: (0, 0))
    return pl.pallas_call(
        body, name=name, grid=(s // tm,),
        in_specs=[row(D_MODEL)] * N_GROUPS + [row(HD), pl.BlockSpec((D_MODEL, D_INP), lambda i: (0, 0)),
                                              row(D_MODEL), vec, row(D_MODEL)],
        out_specs=[row(D_MODEL), vec],
        out_shape=[jax.ShapeDtypeStruct((s, D_MODEL), f32), jax.ShapeDtypeStruct((1, D_MODEL), f32)],
        compiler_params=_params(("arbitrary",)),
    )(*dxs, dba, wp, x, w, dres)


def _d_win(hnt, dxs, dba, name):
    s = hnt.shape[1]
    tn = 256
    per = D_MODEL // tn

    def body(*refs):
        hnt_ref = refs[0]
        dx_refs = refs[1:1 + N_GROUPS]
        o_ref = refs[1 + N_GROUPS]
        j = pl.program_id(0)
        for g in range(N_GROUPS):
            @pl.when(j // per == g)
            def _(g=g):
                o_ref[...] = _dot(hnt_ref[...], _b(dx_refs[g][...])).astype(bf16)

    def dx_spec(g):
        on = lambda j: (j // per == g).astype(jnp.int32)
        return pl.BlockSpec((s, tn), lambda j: (0, (j % per) * on(j)))

    main = pl.pallas_call(
        body, name=name, grid=(N_GROUPS * per,),
        in_specs=[pl.BlockSpec((D_MODEL, s), lambda j: (0, 0))] + [dx_spec(g) for g in range(N_GROUPS)],
        out_specs=pl.BlockSpec((D_MODEL, tn), lambda j: (0, j)),
        out_shape=jax.ShapeDtypeStruct((D_MODEL, D_INP), bf16),
        compiler_params=_params(("parallel",)),
    )(hnt, *dxs)

    def tail(hnt_ref, dba_ref, full_ref, o_ref):
        del full_ref
        o_ref[...] = _dot(hnt_ref[...], _b(dba_ref[...])).astype(bf16)

    return pl.pallas_call(
        tail, name=name + "_ba", grid=(1,),
        in_specs=[pl.BlockSpec((D_MODEL, s), lambda k: (0, 0)), pl.BlockSpec((s, HD), lambda k: (0, 0)),
                  pl.BlockSpec(memory_space=pl.ANY)],
        out_specs=pl.BlockSpec((D_MODEL, HD), lambda k: (0, BLK_BA)),
        out_shape=jax.ShapeDtypeStruct((D_MODEL, D_INP), bf16),
        input_output_aliases={2: 0},
        compiler_params=_params(("arbitrary",)),
    )(hnt, dba, main)


def _rms_fwd(x, w, name):
    s = x.shape[0]
    tr = min(512, s)

    def body(x_ref, w_ref, h_ref, ht_ref):
        xv = x_ref[...]
        r = lax.rsqrt(_rowmean(xv * xv) + EPS)
        h = (xv * r * w_ref[...]).astype(bf16)
        h_ref[...] = h
        ht_ref[...] = h.T

    return pl.pallas_call(
        body, name=name, grid=(s // tr,),
        in_specs=[pl.BlockSpec((tr, D_MODEL), lambda i: (i, 0)), pl.BlockSpec((1, D_MODEL), lambda i: (0, 0))],
        out_specs=[pl.BlockSpec((tr, D_MODEL), lambda i: (i, 0)), pl.BlockSpec((D_MODEL, tr), lambda i: (0, i))],
        out_shape=[jax.ShapeDtypeStruct((s, D_MODEL), bf16), jax.ShapeDtypeStruct((D_MODEL, s), bf16)],
        compiler_params=_params(("parallel",)),
    )(x, w)


def _final_loss(x, w, target, name):
    s = x.shape[0]
    tr = min(512, s)

    def body(x_ref, w_ref, t_ref, dx_ref, dw_ref, loss_ref):
        xv = x_ref[...]
        wv = w_ref[...]
        r = lax.rsqrt(_rowmean(xv * xv) + EPS)
        xn = xv * r
        e = xn * wv - t_ref[...]
        lb = jnp.broadcast_to(0.5 * _sum0(_rowmean(e * e)), (1, HD))
        dy = e * (1.0 / D_MODEL)
        dxn = dy * wv
        dx_ref[...] = r * (dxn - xn * _rowmean(dxn * xn))
        part = _sum0(dy * xn)

        @pl.when(pl.program_id(0) == 0)
        def _():
            dw_ref[...] = part
            loss_ref[...] = lb

        @pl.when(pl.program_id(0) > 0)
        def _():
            dw_ref[...] += part
            loss_ref[...] += lb

    row = pl.BlockSpec((tr, D_MODEL), lambda i: (i, 0))
    vec = pl.BlockSpec((1, D_MODEL), lambda i: (0, 0))
    lsp = pl.BlockSpec((1, HD), lambda i: (0, 0))
    return pl.pallas_call(
        body, name=name, grid=(s // tr,), in_specs=[row, vec, row], out_specs=[row, vec, lsp],
        out_shape=[jax.ShapeDtypeStruct((s, D_MODEL), f32), jax.ShapeDtypeStruct((1, D_MODEL), f32),
                   jax.ShapeDtypeStruct((1, HD), f32)],
        compiler_params=_params(("arbitrary",)),
    )(x, w, target)


def _lru_gates(xc, vec, sp, wa, wx):
    xcb = _b(xc)
    r = _sig(_dot(xcb, wa) + vec[1:2])
    i = _sig(_dot(xcb, wx) + vec[2:3])
    la = (-LRU_C) * r * sp
    a = jnp.exp(la)
    mult = jnp.sqrt(-_expm1(2.0 * la))
    return r, i, a, mult, sp, xcb


def _lru_fwd(proj, cw, vec, wa, wx, name):
    s = proj.shape[0]
    sb = min(SEQ_BLOCK, s)
    nsb = s // sb
    t = min(LRU_T, sb)
    nc = sb // t
    hp = LRU_HP
    wide = hp * HD
    lanes = [slice(j * HD, (j + 1) * HD) for j in range(hp)]

    def body(x_ref, xh_ref, z_ref, cw_ref, vec_ref, wa_ref, wx_ref, y_ref, hl_ref, hc_ref):
        sblk = pl.program_id(1)
        consts = [(cw_ref[:, hl], vec_ref[:, hl], _softplus(-vec_ref[3:4, hl]), _b(wa_ref[j]), _b(wx_ref[j]))
                  for j, hl in enumerate(lanes)]
        row = lax.broadcasted_iota(jnp.int32, (t, HD), 0)

        @pl.when(sblk == 0)
        def _():
            hc_ref[...] = jnp.zeros((8, wide), f32)

        def one_head(j, x, halo, z, hcarry):
            cwv, vec_v, sp_v, wa_v, wx_v = consts[j]
            xs = _conv_taps(x, halo)
            xc = vec_v[0:1] + _conv(xs, cwv)
            r, i, a, mult, _, _ = _lru_gates(xc, vec_v, sp_v, wa_v, wx_v)
            yield
            bb = mult * (i * xc)
            d = 1
            while d < t:
                m = row >= d
                bb = jnp.where(m, a * pltpu.roll(bb, d, axis=0) + bb, bb)
                a = jnp.where(m, a * pltpu.roll(a, d, axis=0), a)
                d *= 2
                yield
            hl = bb + a * hcarry
            nrm = lax.rsqrt(_rowmean(hl * hl) + EPS)
            return hl, (hl * nrm * vec_v[4:5] * (z * _sig(z))).astype(bf16)

        def chunk(c, carries):
            t0 = pl.multiple_of(c * t, t)
            rows = pl.ds(t0, t)
            ins = [(x_ref[rows, hl], _rows_before(x_ref, xh_ref, hl, t0, c, sblk), z_ref[rows, hl]) for hl in lanes]
            outs = _round_robin([one_head(j, *ins[j], carries[j]) for j in range(hp)])
            for j, hl in enumerate(lanes):
                hl_ref[rows, hl] = outs[j][0]
                y_ref[rows, hl] = outs[j][1]
            return tuple(out[0][t - 1:t, :] for out in outs)

        final = lax.fori_loop(0, nc, chunk, tuple(hc_ref[0:1, hl] for hl in lanes))
        hc_ref[...] = jnp.concatenate([jnp.broadcast_to(f, (8, HD)) for f in final], axis=1)

    xsp, xhalo = _seq_specs(sb, wide, lambda g, i: i, lambda g, i: g)
    zsp, _ = _seq_specs(sb, wide, lambda g, i: i, lambda g, i: HEADS // hp + g)
    gcol = lambda g, i: (0, g)
    wsp = pl.BlockSpec((hp, HD, HD), lambda g, i: (g, 0, 0))
    osp = pl.BlockSpec((sb, wide), lambda g, i: (i, g))
    return pl.pallas_call(
        body, name=name, grid=(HEADS // hp, nsb),
        in_specs=[xsp, xhalo, zsp, pl.BlockSpec((4, wide), gcol), pl.BlockSpec((8, wide), gcol), wsp, wsp],
        out_specs=[osp, osp],
        out_shape=[jax.ShapeDtypeStruct((s, D_MIX), bf16), jax.ShapeDtypeStruct((s, D_MODEL), f32)],
        scratch_shapes=[pltpu.VMEM((8, wide), f32)],
        compiler_params=_params(("parallel", "arbitrary")),
    )(proj, proj, proj, cw, vec, wa, wx)


def _lru_bwd(dy, proj, hl, cw, vec, wa, wx, name):
    s = proj.shape[0]
    sb = min(SEQ_BLOCK, s)
    nsb = s // sb
    t = min(LRU_T, sb)
    nc = sb // t
    hp = LRU_HP
    wide = hp * HD
    lanes = [slice(j * HD, (j + 1) * HD) for j in range(hp)]
    n_acc = 10

    def body(dy_ref, x_ref, xh_ref, z_ref, hl_ref, hlh_ref, cw_ref, vec_ref, wa_ref, wx_ref,
             dx_ref, dz_ref, dcw_ref, dvec_ref, dwa_ref, dwx_ref, acc_ref, dxh_ref):
        step = pl.program_id(1)
        sblk = nsb - 1 - step
        consts = [(cw_ref[:, hl], vec_ref[:, hl], _softplus(-vec_ref[3:4, hl]), _b(wa_ref[j]), _b(wx_ref[j]))
                  for j, hl in enumerate(lanes)]
        row = lax.broadcasted_iota(jnp.int32, (t, HD), 0)

        @pl.when(step == 0)
        def _():
            acc_ref[...] = jnp.zeros((16, wide), f32)
            dxh_ref[...] = jnp.zeros((8, wide), f32)
            dwa_ref[...] = jnp.zeros((hp, HD, HD), f32)
            dwx_ref[...] = jnp.zeros((hp, HD, HD), f32)

        def one_head(j, x, halo, z, hv, hhalo, dyv, carry):
            dcw0, dcw1, dcw2, dcw3, dcb, dba, dbx, dsp, dlnw, mu, dxc_halo = carry
            cwv, vec_v, sp_v, wa_v, wx_v = consts[j]
            lnw = vec_v[4:5]
            xs = _conv_taps(x, halo)
            xc = vec_v[0:1] + _conv(xs, cwv)
            r, i, a, mult, sp, xcb = _lru_gates(xc, vec_v, sp_v, wa_v, wx_v)
            yield
            hprev = pltpu.roll(jnp.concatenate([hhalo, hv], axis=0), 1, axis=0)[8:]
            sgz = _sig(z)
            sz = z * sgz
            nrm = lax.rsqrt(_rowmean(hv * hv) + EPS)
            hn = hv * nrm
            dlnw = dlnw + _sum0(dyv * hn * sz)
            dzv = _b(dyv * hn * lnw * _dsilu(z, sgz))
            dhn = dyv * lnw * sz
            lam = nrm * (dhn - hn * _rowmean(dhn * hn))
            cf = jnp.where(row == t - 1, 1.0, pltpu.roll(a, t - 1, axis=0))
            d = 1
            while d < t:
                m = row < t - d
                lam = jnp.where(m, lam + cf * pltpu.roll(lam, t - d, axis=0), lam)
                cf = jnp.where(m, cf * pltpu.roll(cf, t - d, axis=0), cf)
                d *= 2
                yield
            lam = lam + cf * mu
            mu = a[0:1] * lam[0:1]
            da = lam * hprev
            dmult = lam * (i * xc)
            di = lam * mult * xc
            dxc = lam * mult * i
            dla = da * a - dmult * (a * a) / mult
            dr = dla * (-LRU_C * sp)
            dsp = dsp + _sum0(dla * (-LRU_C * r))
            dra = dr * r * (1.0 - r)
            dia = di * i * (1.0 - i)
            dba = dba + _sum0(dra)
            dbx = dbx + _sum0(dia)
            drab, diab = _b(dra), _b(dia)
            dwa = _dot_tn(xcb, drab)
            dwx = _dot_tn(xcb, diab)
            dxc = dxc + _dot_nt(drab, wa_v) + _dot_nt(diab, wx_v)
            yield
            dcb = dcb + _sum0(dxc)
            dcw0 = dcw0 + _sum0(dxc * xs[0])
            dcw1 = dcw1 + _sum0(dxc * xs[1])
            dcw2 = dcw2 + _sum0(dxc * xs[2])
            dcw3 = dcw3 + _sum0(dxc * xs[3])
            dxv = _b(_conv_back(dxc, dxc_halo, cwv))
            return dxv, dzv, dwa, dwx, (dcw0, dcw1, dcw2, dcw3, dcb, dba, dbx, dsp, dlnw, mu, dxc[0:8])

        def chunk(ci, carries):
            c = nc - 1 - ci
            t0 = pl.multiple_of(c * t, t)
            rows = pl.ds(t0, t)
            ins = [(x_ref[rows, hl], _rows_before(x_ref, xh_ref, hl, t0, c, sblk), z_ref[rows, hl], hl_ref[rows, hl],
                    _rows_before(hl_ref, hlh_ref, hl, t0, c, sblk), dy_ref[rows, hl]) for hl in lanes]
            outs = _round_robin([one_head(j, *ins[j], carries[j]) for j in range(hp)])
            for j, hl in enumerate(lanes):
                dx_ref[rows, hl] = outs[j][0]
                dz_ref[rows, hl] = outs[j][1]
                dwa_ref[j] += outs[j][2]
                dwx_ref[j] += outs[j][3]
            return tuple(out[4] for out in outs)

        acc = acc_ref[...]
        dxh = dxh_ref[...]
        init = tuple(tuple(acc[k:k + 1, hl] for k in range(n_acc)) + (dxh[:, hl],) for hl in lanes)
        final = lax.fori_loop(0, nc, chunk, init)
        rows_out = [jnp.concatenate([final[j][k] for j in range(hp)], axis=1) for k in range(n_acc)]
        zero = jnp.zeros((1, wide), f32)
        acc_ref[...] = jnp.concatenate(rows_out + [zero] * (16 - n_acc), axis=0)
        dxh_ref[...] = jnp.concatenate([final[j][n_acc] for j in range(hp)], axis=1)

        @pl.when(step == nsb - 1)
        def _():
            dcw_ref[...] = jnp.concatenate(rows_out[0:4], axis=0)
            dlam = -rows_out[7] * _sig(-vec_ref[3:4, :])
            dvec_ref[...] = jnp.concatenate([rows_out[4], rows_out[5], rows_out[6], dlam, rows_out[8], zero, zero, zero],
                                            axis=0)

    rblk = lambda g, i: nsb - 1 - i
    xsp, xhalo = _seq_specs(sb, wide, rblk, lambda g, i: g)
    zsp, _ = _seq_specs(sb, wide, rblk, lambda g, i: HEADS // hp + g)
    gcol = lambda g, i: (0, g)
    wsp = pl.BlockSpec((hp, HD, HD), lambda g, i: (g, 0, 0))
    return pl.pallas_call(
        body, name=name, grid=(HEADS // hp, nsb),
        in_specs=[xsp, xsp, xhalo, zsp, xsp, xhalo, pl.BlockSpec((4, wide), gcol), pl.BlockSpec((8, wide), gcol), wsp, wsp],
        out_specs=[xsp, xsp, pl.BlockSpec((4, wide), gcol), pl.BlockSpec((8, wide), gcol), wsp, wsp],
        out_shape=[jax.ShapeDtypeStruct((s, D_MODEL), bf16), jax.ShapeDtypeStruct((s, D_MODEL), bf16),
                   jax.ShapeDtypeStruct((4, D_MODEL), f32), jax.ShapeDtypeStruct((8, D_MODEL), f32),
                   jax.ShapeDtypeStruct((HEADS, HD, HD), f32), jax.ShapeDtypeStruct((HEADS, HD, HD), f32)],
        scratch_shapes=[pltpu.VMEM((16, wide), f32), pltpu.VMEM((8, wide), f32)],
        compiler_params=_params(("parallel", "arbitrary")),
    )(dy, proj, proj, proj, hl, hl, cw, vec, wa, wx)


def _lane_pick(x, lane, idx):
    return _sum1(jnp.where(lane == idx, x, 0.0))


def _round_robin(gens):
    results = [None] * len(gens)
    live = list(range(len(gens)))
    while live:
        for i in list(live):
            try:
                next(gens[i])
            except StopIteration as done:
                results[i] = done.value
                live.remove(i)
    return results


def _sdot(a, b):
    return _dot(_b(a), _b(b))


def _sdot_tn(a, b):
    return _dot_tn(_b(a), _b(b))


def _sdot_nt(a, b):
    return _dot_nt(_b(a), _b(b))


def _tri_dot(tri, x):
    trib = tri.astype(bf16)
    x1 = x.astype(bf16)
    r1 = x - x1.astype(f32)
    x2 = r1.astype(bf16)
    x3 = (r1 - x2.astype(f32)).astype(bf16)
    return _dot(trib, x1) + _dot(trib, x2) + _dot(trib, x3)


def _inv_unit_lower(a):
    n = -a
    p = _sdot(a, a)
    yield
    for k in range(5):
        n = n + p + _sdot(n, p)
        if k < 4:
            p = _sdot(p, p)
        yield
    return n


def _dn_gates(bav, dnv, masks):
    ltri = masks[4]
    ea = jnp.exp(pltpu.roll(dnv[0:1], HEADS, axis=1))
    dt = pltpu.roll(dnv[1:2], HEADS, axis=1)
    u = bav + dt
    g = -ea * _softplus(u)
    return dict(beta=_sig(bav), g=g, gc=_tri_dot(ltri, g), sga=_sig(u), ea=ea)


def _dn_chunk(x3, halo3, gates, cw3, h, masks, cpre=None, tinv=None):
    lane, ri, ci, eye, ltri = masks
    if cpre is None:
        cpre = _conv(_conv_taps(x3, halo3), cw3)
    sg = _sig(cpre)
    act = cpre * sg
    q_raw, k_raw, v = act[:, 0:HD], act[:, HD:2 * HD], act[:, 2 * HD:3 * HD]
    rq = lax.rsqrt(_sum1(q_raw * q_raw) + EPS)
    rk = lax.rsqrt(_sum1(k_raw * k_raw) + EPS)
    qn = q_raw * rq
    k = k_raw * rk
    q = qn * QSCALE
    beta = jnp.broadcast_to(_lane_pick(gates["beta"], lane, h), (CHUNK, HD))
    gc = jnp.broadcast_to(_lane_pick(gates["gc"], lane, HEADS + h), (CHUNK, HD))
    yield
    gc64 = gc[:, 0:CHUNK]
    grow = _sum0(gc64 * eye)
    dm = jnp.exp(jnp.where(ri >= ci, gc64 - grow, -1e30))
    ds_ = jnp.where(ri > ci, dm, 0.0)
    eg = jnp.exp(gc)
    glast = gc[CHUNK - 1:CHUNK, :]
    ek = jnp.exp(glast - gc)
    kb = k * beta
    kbf = _b(k)
    amat = _dot_nt(_b(kb), kbf) * ds_
    pmat = _dot_nt(_b(q), kbf) * dm
    yield
    if tinv is None:
        tinv = yield from _inv_unit_lower(amat)
    return dict(cpre=cpre, sg=sg, rq=rq, rk=rk, qn=qn, q=q, k=k, v=v, kbf=kbf, beta=beta, gc=gc, dm=dm, ds=ds_, eg=eg,
                glast=glast, ek=ek, kb=kb, amat=amat, tinv=tinv, pmat=pmat)


def _dn_masks():
    lane = lax.broadcasted_iota(jnp.int32, (CHUNK, HD), 1)
    ri = lax.broadcasted_iota(jnp.int32, (CHUNK, CHUNK), 0)
    ci = lax.broadcasted_iota(jnp.int32, (CHUNK, CHUNK), 1)
    eye = (ri == ci).astype(f32)
    ltri = (ri >= ci).astype(f32)
    return lane, ri, ci, eye, ltri


def _dn_load_qkv(q_ref, qh_ref, k_ref, kh_ref, v_ref, vh_ref, hl, rows, t0, c, sblk):
    x3 = jnp.concatenate([q_ref[rows, hl], k_ref[rows, hl], v_ref[rows, hl]], axis=1)
    halo3 = jnp.concatenate([_rows_before(q_ref, qh_ref, hl, t0, c, sblk), _rows_before(k_ref, kh_ref, hl, t0, c, sblk),
                             _rows_before(v_ref, vh_ref, hl, t0, c, sblk)], axis=1)
    return x3, halo3


def _dn_cw3(cwq_ref, cwk_ref, cwv_ref, j):
    return jnp.concatenate([r[:, j * HD:(j + 1) * HD] for r in (cwq_ref, cwk_ref, cwv_ref)], axis=1)


def _dn_fwd(proj, cw, dnv, y, name):
    s = proj.shape[0]
    sb = min(DN_SEQ_BLOCK, s)
    nsb = s // sb
    nc = sb // CHUNK
    hp = DN_HP

    def body(q_ref, qh_ref, k_ref, kh_ref, v_ref, vh_ref, z_ref, ba_ref, cwq_ref, cwk_ref, cwv_ref, dnv_ref, yin_ref,
             y_ref, o_ref, st_ref, ti_ref, cp_ref, s_ref):
        del yin_ref
        grp = pl.program_id(0)
        sblk = pl.program_id(1)
        masks = _dn_masks()
        dnv_v = dnv_ref[...]
        cw3 = [_dn_cw3(cwq_ref, cwk_ref, cwv_ref, j) for j in range(hp)]

        @pl.when(sblk == 0)
        def _():
            s_ref[...] = jnp.zeros((hp, HD, HD), f32)

        def one_head(j, x3, halo3, gates, z, st):
            f = yield from _dn_chunk(x3, halo3, gates, cw3[j], grp * hp + j, masks)
            sbf = _b(st)
            qs = _dot(_b(f["q"] * f["eg"]), sbf)
            rhs = f["beta"] * (f["v"] - _dot(_b(f["k"] * f["eg"]), sbf))
            yield
            vn = rhs + _sdot(f["tinv"], rhs)
            yield
            vnb = _b(vn)
            o = qs + _dot(_b(f["pmat"]), vnb)
            st_new = st * jnp.exp(f["glast"]) + _dot_tn(_b(f["k"] * f["ek"]), vnb)
            yield
            nrm = lax.rsqrt(_rowmean(o * o) + EPS)
            yv = (o * nrm * dnv_v[2:3] * (z * _sig(z))).astype(bf16)
            return o, yv, st_new, f["tinv"], f["cpre"]

        def chunk(c, states):
            t0 = pl.multiple_of(c * CHUNK, CHUNK)
            rows = pl.ds(t0, CHUNK)
            bav = ba_ref[rows, :]
            ins = []
            for j in range(hp):
                hl = slice(j * HD, (j + 1) * HD)
                ins.append(_dn_load_qkv(q_ref, qh_ref, k_ref, kh_ref, v_ref, vh_ref, hl, rows, t0, c, sblk)
                           + (z_ref[rows, hl],))
            gates = _dn_gates(bav, dnv_v, masks)
            outs = _round_robin([one_head(j, ins[j][0], ins[j][1], gates, ins[j][2], states[j]) for j in range(hp)])
            for j in range(hp):
                hl = slice(j * HD, (j + 1) * HD)
                st_ref[j, c] = states[j]
                o_ref[rows, hl] = outs[j][0]
                y_ref[rows, hl] = outs[j][1]
                ti_ref[j, c] = outs[j][3]
                cp_ref[rows, j * 3 * HD:(j + 1) * 3 * HD] = outs[j][4]
            return tuple(out[2] for out in outs)

        final = lax.fori_loop(0, nc, chunk, tuple(s_ref[j] for j in range(hp)))
        for j in range(hp):
            s_ref[j] = final[j]

    wide = hp * HD
    grp_specs = lambda off: _seq_specs(sb, wide, lambda g, i: i, lambda g, i: off // hp + g)
    (qsp, qhalo), (ksp, khalo), (vsp, vhalo) = grp_specs(2 * HEADS), grp_specs(3 * HEADS), grp_specs(4 * HEADS)
    zsp, _ = grp_specs(5 * HEADS)
    basp = pl.BlockSpec((sb, HD), lambda g, i: (i, BLK_BA))
    cws = lambda off: pl.BlockSpec((4, wide), lambda g, i: (0, off // hp + g))
    osp = lambda off: pl.BlockSpec((sb, wide), lambda g, i: (i, off // hp + g))
    return pl.pallas_call(
        body, name=name, grid=(HEADS // hp, nsb),
        in_specs=[qsp, qhalo, ksp, khalo, vsp, vhalo, zsp, basp, cws(0), cws(HEADS), cws(2 * HEADS),
                  pl.BlockSpec((8, HD), lambda g, i: (0, 0)), pl.BlockSpec(memory_space=pl.ANY)],
        out_specs=[osp(HEADS), osp(0), pl.BlockSpec((hp, nc, HD, HD), lambda g, i: (g, i, 0, 0)),
                   pl.BlockSpec((hp, nc, CHUNK, CHUNK), lambda g, i: (g, i, 0, 0)),
                   pl.BlockSpec((sb, 3 * wide), lambda g, i: (i, g))],
        out_shape=[jax.ShapeDtypeStruct((s, D_MIX), bf16), jax.ShapeDtypeStruct((s, D_MODEL), f32),
                   jax.ShapeDtypeStruct((HEADS, s // CHUNK, HD, HD), f32),
                   jax.ShapeDtypeStruct((HEADS, s // CHUNK, CHUNK, CHUNK), f32),
                   jax.ShapeDtypeStruct((s, 3 * D_MODEL), f32)],
        scratch_shapes=[pltpu.VMEM((hp, HD, HD), f32)],
        input_output_aliases={12: 0},
        compiler_params=_params(("parallel", "arbitrary")),
    )(proj, proj, proj, proj, proj, proj, proj, proj, cw, cw, cw, dnv, y)


def _dn_bwd(dy, proj, o, states, tinvs, cpres, cw, dnv, name):
    s = proj.shape[0]
    sb = min(DN_SEQ_BLOCK, s)
    nsb = s // sb
    nc = sb // CHUNK
    hp = DN_HP
    ngrp = HEADS // hp

    def body(dy_ref, q_ref, k_ref, v_ref, z_ref, ba_ref, o_ref, st_ref, ti_ref, cp_ref,
             cwq_ref, cwk_ref, cwv_ref, dnv_ref,
             dq_ref, dk_ref, dv_ref, dz_ref, dba_ref, dcw_ref, dsc_ref,
             ds_ref, dch_ref, cwacc_ref, nacc_ref):
        step = pl.program_id(0)
        grp = pl.program_id(1)
        sblk = nsb - 1 - step
        h0 = grp * hp
        masks = _dn_masks()
        lane, ri, ci, eye, ltri = masks
        utri = (ri <= ci).astype(f32)
        cw3s = [_dn_cw3(cwq_ref, cwk_ref, cwv_ref, j) for j in range(hp)]
        dnv_v = dnv_ref[...]
        dnw = dnv_v[2:3]
        row64 = lax.broadcasted_iota(jnp.int32, (CHUNK, HD), 0)

        @pl.when(step == 0)
        def _():
            for j in range(hp):
                ds_ref[h0 + j] = jnp.zeros((HD, HD), f32)
                dch_ref[h0 + j] = jnp.zeros((8, 3 * HD), f32)
                cwacc_ref[h0 + j] = jnp.zeros((8, 3 * HD), f32)

        @pl.when((step == 0) & (grp == 0))
        def _():
            nacc_ref[...] = jnp.zeros((8, HD), f32)

        def one_head(j, x3, gates, z, st, ti, cp, ov, dyv, carry):
            dst, dch, cwacc = carry
            cw3 = cw3s[j]
            f = yield from _dn_chunk(None, None, gates, cw3, h0 + j, masks, cp, ti)
            q, k, v, beta, eg, ek, kbf = f["q"], f["k"], f["v"], f["beta"], f["eg"], f["ek"], f["kbf"]
            sbf = _b(st)
            dstb = _b(dst)
            qe = q * eg
            keg = k * eg
            kd = k * ek
            kegb, qeb, kdb = _b(keg), _b(qe), _b(kd)
            e = v - _dot(kegb, sbf)
            yield
            rhs = beta * e
            vn = rhs + _sdot(f["tinv"], rhs)
            yield
            vnb = _b(vn)
            pb = _b(f["pmat"])
            sgz = _sig(z)
            sz = z * sgz
            nrm = lax.rsqrt(_rowmean(ov * ov) + EPS)
            on = ov * nrm
            ddnw = _sum0(dyv * on * sz)
            dpz = dyv * on * dnw * _dsilu(z, sgz)
            don = dyv * dnw * sz
            do = nrm * (don - on * _rowmean(don * on))
            dob = _b(do)
            dvn = _dot_tn(pb, dob) + _dot(kdb, dstb)
            dpm = jnp.where(ri >= ci, _dot_nt(dob, vnb), 0.0)
            dqe = _dot_nt(dob, sbf)
            dkd = _dot_nt(vnb, dstb)
            qtdo = _dot_tn(qeb, dob)
            yield
            drhs = dvn + _sdot_tn(f["tinv"], dvn)
            dqk = _b(dpm * f["dm"])
            dq = _dot(dqk, kbf) + dqe * eg
            dk_p = _dot_tn(dqk, _b(q))
            yield
            dam = jnp.where(ri > ci, -_sdot_nt(drhs, vn), 0.0)
            de = drhs * beta
            deb = _b(de)
            dbeta = _sum1(drhs * e)
            dkeg = -_dot_nt(deb, sbf)
            dst_new = qtdo + dst * jnp.exp(f["glast"]) - _dot_tn(kegb, deb)
            yield
            dkk = _b(dam * f["ds"])
            dkb = _dot(dkk, kbf)
            dk = dk_p + _dot_tn(dkk, _b(f["kb"])) + dkb * beta + dkeg * eg + dkd * ek
            yield
            dbeta = dbeta + _sum1(dkb * k)
            mm = dpm * f["pmat"] + dam * f["amat"]
            dglast = _sum1(_sum0(dst * st)) * jnp.exp(f["glast"]) + _sum1(_sum0(dkd * kd))
            dgc = (_sum1(mm) - _sum1(eye * _sum0(mm)) + _sum1(dqe * qe) + _sum1(dkeg * keg) - _sum1(dkd * kd))
            dgc = jnp.broadcast_to(dgc, (CHUNK, HD)) + jnp.where(row64 == CHUNK - 1, dglast, 0.0)
            dq_raw = (QSCALE * f["rq"]) * (dq - f["qn"] * _sum1(f["qn"] * dq))
            dk_raw = f["rk"] * (dk - k * _sum1(k * dk))
            dact = jnp.concatenate([dq_raw, dk_raw, de], axis=1)
            dc = dact * _dsilu(f["cpre"], f["sg"])
            dqkv, dcw_part = _conv_back(dc, dch, cw3, x3)
            cwacc = cwacc + dcw_part
            return dpz, dqkv, (dgc, dbeta), ddnw, (dst_new, dc[0:8], cwacc)

        def chunk(cidx, carry):
            heads, nacc, sa0, sa1 = carry
            c = nc - 1 - cidx
            t0 = pl.multiple_of(c * CHUNK, CHUNK)
            rows = pl.ds(t0, CHUNK)
            bav = ba_ref[rows, :]
            ins = []
            for j in range(hp):
                hl = slice(j * HD, (j + 1) * HD)
                x3 = jnp.concatenate([q_ref[rows, hl], k_ref[rows, hl], v_ref[rows, hl]], axis=1)
                ins.append((x3, z_ref[rows, hl], st_ref[j, c], ti_ref[j, c],
                            cp_ref[rows, j * 3 * HD:(j + 1) * 3 * HD], o_ref[rows, hl], dy_ref[rows, hl]))
            gates = _dn_gates(bav, dnv_v, masks)
            outs = _round_robin([one_head(j, ins[j][0], gates, *ins[j][1:], heads[j]) for j in range(hp)])
            dgc_all = jnp.zeros((CHUNK, HD), f32)
            dbeta_all = jnp.zeros((CHUNK, HD), f32)
            for j in range(hp):
                hl = slice(j * HD, (j + 1) * HD)
                dpz, dqkv, (dgc, dbeta), ddnw, _ = outs[j]
                dgc_all = jnp.where(lane == HEADS + h0 + j, dgc, dgc_all)
                dbeta_all = jnp.where(lane == h0 + j, dbeta, dbeta_all)
                dq_ref[rows, hl] = _b(dqkv[:, 0:HD])
                dk_ref[rows, hl] = _b(dqkv[:, HD:2 * HD])
                dv_ref[rows, hl] = _b(dqkv[:, 2 * HD:3 * HD])
                dz_ref[rows, hl] = _b(dpz)
                nacc = nacc + ddnw
            dg_all = _tri_dot(utri, dgc_all)
            dain_all = dg_all * (-gates["ea"]) * gates["sga"]
            dba = dbeta_all * gates["beta"] * (1.0 - gates["beta"]) + dain_all
            sa0 = sa0 + _sum0(dg_all * gates["g"])
            sa1 = sa1 + _sum0(dain_all)

            @pl.when(grp == 0)
            def _():
                dba_ref[rows, :] = dba

            @pl.when(grp > 0)
            def _():
                dba_ref[rows, :] += dba

            return tuple(out[4] for out in outs), nacc, sa0, sa1

        init = tuple((ds_ref[h0 + j], dch_ref[h0 + j], cwacc_ref[h0 + j][0:4]) for j in range(hp))
        heads, nacc, sa0, sa1 = lax.fori_loop(
            0, nc, chunk, (init, nacc_ref[0:1, :], nacc_ref[1:2, :], nacc_ref[2:3, :]))
        nacc_ref[0:3, :] = jnp.concatenate([nacc, sa0, sa1], axis=0)
        zpad = jnp.zeros((4, 3 * HD), f32)
        for j in range(hp):
            dst, dch, cwacc = heads[j]
            ds_ref[h0 + j] = dst
            dch_ref[h0 + j] = dch
            cwacc_ref[h0 + j] = jnp.concatenate([cwacc, zpad], axis=0)

        @pl.when(step == nsb - 1)
        def _():
            for j in range(hp):
                dcw_ref[h0 + j] = heads[j][2]

            @pl.when(grp == ngrp - 1)
            def _():
                dsc_ref[...] = jnp.concatenate([pltpu.roll(sa0, HD - HEADS, axis=1), pltpu.roll(sa1, HD - HEADS, axis=1),
                                                nacc, jnp.zeros((5, HD), f32)], axis=0)

    wide = hp * HD
    rblk = lambda i, g: nsb - 1 - i
    grp_specs = lambda off: _seq_specs(sb, wide, rblk, lambda i, g: off // hp + g)
    (qsp, qhalo), (ksp, khalo), (vsp, vhalo) = grp_specs(2 * HEADS), grp_specs(3 * HEADS), grp_specs(4 * HEADS)
    zsp, _ = grp_specs(5 * HEADS)
    hsp = lambda off: pl.BlockSpec((sb, wide), lambda i, g: (rblk(i, g), off // hp + g))
    basp = lambda col: pl.BlockSpec((sb, HD), lambda i, g: (rblk(i, g), col))
    cws = lambda off: pl.BlockSpec((4, wide), lambda i, g: (0, off // hp + g))
    whole = lambda shape: pl.BlockSpec(shape, lambda i, g: (0,) * len(shape))
    return pl.pallas_call(
        body, name=name, grid=(nsb, ngrp),
        in_specs=[hsp(HEADS), qsp, ksp, vsp, zsp, basp(BLK_BA), hsp(0),
                  pl.BlockSpec((hp, nc, HD, HD), lambda i, g: (g, rblk(i, g), 0, 0)),
                  pl.BlockSpec((hp, nc, CHUNK, CHUNK), lambda i, g: (g, rblk(i, g), 0, 0)),
                  pl.BlockSpec((sb, 3 * wide), lambda i, g: (rblk(i, g), g)),
                  cws(0), cws(HEADS), cws(2 * HEADS), whole((8, HD))],
        out_specs=[hsp(0), hsp(0), hsp(0), hsp(0), basp(0), whole((HEADS, 4, 3 * HD)), whole((8, HD))],
        out_shape=[jax.ShapeDtypeStruct((s, D_MODEL), bf16)] * 4
        + [jax.ShapeDtypeStruct((s, HD), f32), jax.ShapeDtypeStruct((HEADS, 4, 3 * HD), f32),
           jax.ShapeDtypeStruct((8, HD), f32)],
        scratch_shapes=[pltpu.VMEM((HEADS, HD, HD), f32), pltpu.VMEM((HEADS, 8, 3 * HD), f32),
                        pltpu.VMEM((HEADS, 8, 3 * HD), f32), pltpu.VMEM((8, HD), f32)],
        compiler_params=_params(("arbitrary", "arbitrary")),
    )(dy, proj, proj, proj, proj, proj, o, states, tinvs, cpres, cw, cw, cw, dnv)


ANY = pl.BlockSpec(memory_space=pl.ANY)
CHIP_MASKS = ((1, 0, 0), (0, 1, 0), (1, 1, 0))
ALL_MASKS = tuple((m >> 2 & 1, m >> 1 & 1, m & 1) for m in range(1, N_DEV))


def _me():
    return lax.axis_index("x"), lax.axis_index("y"), lax.axis_index("c")


def _flip(me, mask):
    return tuple((1 - v) if m else v for v, m in zip(me, mask))


def _dev_index(dev):
    return 4 * dev[0] + 2 * dev[1] + dev[2]


def _chip(dev):
    return 2 * dev[0] + dev[1]


def _comm_call(name, body, arrays, out_shapes, nsem, nloc=0):
    scratch = [pltpu.SemaphoreType.DMA((nsem,)), pltpu.SemaphoreType.DMA((nsem,))]
    if nloc:
        scratch.append(pltpu.SemaphoreType.DMA((nloc,)))
    return pl.pallas_call(
        body, name=name, in_specs=[ANY] * len(arrays), out_specs=[ANY] * len(out_shapes), out_shape=out_shapes,
        scratch_shapes=scratch, compiler_params=pltpu.CompilerParams(has_side_effects=True),
    )(*arrays)


def _put_own(gathered, own, slot):
    return lax.dynamic_update_index_in_dim(gathered, own, slot, 0)


def _gather_halves(arrays, whole_arrays, name):
    na, nw = len(arrays), len(whole_arrays)
    nm = len(CHIP_MASKS)

    def body(*refs):
        srcs, dsts = refs[:na + nw], refs[na + nw:2 * (na + nw)]
        send_sems, recv_sems = refs[2 * (na + nw):]
        me = _me()
        sib = _flip(me, (0, 0, 1))

        def half(a, ref, core):
            rows = arrays[a].shape[0] // 2
            return ref.at[pl.ds(pl.multiple_of(core * rows, rows), rows)]

        def over_ici(a, mi, sender, to):
            if a < na:
                src, dst = half(a, srcs[a], sender[2]), half(a, dsts[a].at[_chip(sender)], sender[2])
            else:
                src, dst = srcs[a], dsts[a].at[_chip(sender)]
            return pltpu.make_async_remote_copy(src_ref=src, dst_ref=dst, send_sem=send_sems.at[a * nm + mi],
                                                recv_sem=recv_sems.at[a * nm + mi], device_id=to, device_id_type=MESH)

        def over_d2d(a, mi, origin, sender, to):
            k = (na + nw) * nm + a * nm + mi
            blk = half(a, dsts[a].at[_chip(origin)], sender[2])
            return pltpu.make_async_remote_copy(src_ref=blk, dst_ref=blk, send_sem=send_sems.at[k],
                                                recv_sem=recv_sems.at[k], device_id=to, device_id_type=MESH)

        sends = []
        for a in range(na + nw):
            for mi, mask in enumerate(CHIP_MASKS):
                cp = over_ici(a, mi, me, _flip(me, mask))
                cp.start()
                sends.append(cp)
        for a in range(na + nw):
            for mi, mask in enumerate(CHIP_MASKS):
                peer = _flip(me, mask)
                over_ici(a, mi, peer, me).wait_recv()
                if a < na:
                    cp = over_d2d(a, mi, peer, me, sib)
                    cp.start()
                    sends.append(cp)
        for a in range(na):
            for mi, mask in enumerate(CHIP_MASKS):
                over_d2d(a, mi, _flip(sib, mask), sib, me).wait_recv()
        for cp in sends:
            cp.wait_send()

    every = list(arrays) + list(whole_arrays)
    got = _comm_call(name, body, every, [jax.ShapeDtypeStruct((4,) + t.shape, t.dtype) for t in every],
                     (2 * na + nw) * nm)
    chip = 2 * lax.axis_index("x") + lax.axis_index("y")
    return [_put_own(g, t, chip) for g, t in zip(got, every)]


HBM = pl.BlockSpec(memory_space=pltpu.HBM)
SEM = pl.BlockSpec(memory_space=pltpu.SEMAPHORE)
DATAFLOW = pltpu.SideEffectType.DATAFLOW_SIDE_EFFECTING


def _split_start(name, srcs, land_shapes, nsem, issue, after=None):
    ns, nl = len(srcs), len(land_shapes)
    n_in = ns + nl + (after is not None)

    def body(*refs):
        issue(refs[:ns], refs[ns:ns + nl], refs[n_in + ns + nl], refs[n_in + ns + nl + 1])
        token = refs[-1]
        token[...] = jnp.zeros_like(token)

    lands = [lax.empty(t.shape, t.dtype) for t in land_shapes]
    thru = [pltpu.HBM(t.shape, t.dtype) for t in list(srcs) + list(land_shapes)]
    hbm = lambda t: pltpu.with_memory_space_constraint(t, pltpu.HBM)
    return pl.pallas_call(
        body, name=name, in_specs=[HBM] * (ns + nl) + [ANY] * (after is not None),
        out_shape=(*thru, pltpu.SemaphoreType.DMA((nsem,)), pltpu.SemaphoreType.DMA((nsem,)),
                   jax.ShapeDtypeStruct((8, HD), f32)),
        out_specs=(*[HBM] * (ns + nl), SEM, SEM, pl.BlockSpec(memory_space=pltpu.VMEM)),
        input_output_aliases={i: i for i in range(ns + nl)},
        compiler_params=pltpu.CompilerParams(has_side_effects=DATAFLOW),
    )(*[hbm(t) for t in srcs], *[hbm(t) for t in lands], *([after] if after is not None else []))


def _split_wait(name, started, ns, after, finish):
    thru, send_sems, recv_sems = started[:-3], started[-3], started[-2]
    nt = len(thru)

    def body(*refs):
        finish(refs[:ns], refs[ns:nt], refs[nt], refs[nt + 1])

    outs = pl.pallas_call(
        body, name=name, in_specs=[HBM] * nt + [SEM, SEM, ANY],
        out_shape=tuple(pltpu.HBM(t.shape, t.dtype) for t in thru), out_specs=tuple([HBM] * nt),
        input_output_aliases={i: i for i in range(nt)},
        compiler_params=pltpu.CompilerParams(has_side_effects=DATAFLOW),
    )(*thru, send_sems, recv_sems, after)
    return list(outs[ns:])


def _gather_start(arrays, name, after=None):
    nm = len(CHIP_MASKS)

    def issue(srcs, lands, send_sems, recv_sems):
        me = _me()
        for a in range(len(arrays)):
            for mi, mask in enumerate(CHIP_MASKS):
                pltpu.make_async_remote_copy(
                    src_ref=srcs[a], dst_ref=lands[a].at[_chip(me)], send_sem=send_sems.at[a * nm + mi],
                    recv_sem=recv_sems.at[a * nm + mi], device_id=_flip(me, mask), device_id_type=MESH).start()

    shapes = [jax.ShapeDtypeStruct((4,) + t.shape, t.dtype) for t in arrays]
    return _split_start(name, arrays, shapes, len(arrays) * nm, issue, after)


def _gather_finish(started, arrays, after, name):
    nm = len(CHIP_MASKS)

    def finish(srcs, lands, send_sems, recv_sems):
        me = _me()
        for a in range(len(arrays)):
            for mi, mask in enumerate(CHIP_MASKS):
                peer = _flip(me, mask)
                cp = pltpu.make_async_remote_copy(
                    src_ref=srcs[a], dst_ref=lands[a].at[_chip(peer)], send_sem=send_sems.at[a * nm + mi],
                    recv_sem=recv_sems.at[a * nm + mi], device_id=peer, device_id_type=MESH)
                cp.wait_send()
                cp.wait_recv()

    got = _split_wait(name, started, len(arrays), after, finish)
    chip = 2 * lax.axis_index("x") + lax.axis_index("y")
    return [_put_own(g, t, chip) for g, t in zip(got, arrays)]


HALF_IN, HALF_OUT = D_MODEL // 2, D_MIX // 8


def _scatter_piece(kind, ref, d):
    j, r = d >> 1, d & 1
    if kind == "win":
        return ref.at[pl.ds(HALF_IN * r, HALF_IN), pl.ds(WIN_STRIDE * j, WIN_W)]
    if kind == "wout":
        return ref.at[pl.ds(2 * HALF_OUT * j + HALF_OUT * r, HALF_OUT)]
    return ref.at[d]


def _scatter_start(arrays, kinds, name):
    na = len(arrays)

    def issue(srcs, lands, send_sems, recv_sems):
        me = _me()
        my = _dev_index(me)
        for d in range(N_DEV):
            dev = (d >> 2, (d >> 1) & 1, d & 1)
            for a in range(na):
                @pl.when(my != d)
                def _(a=a, d=d, dev=dev):
                    pltpu.make_async_remote_copy(
                        src_ref=_scatter_piece(kinds[a], srcs[a], d), dst_ref=lands[a].at[my],
                        send_sem=send_sems.at[a * N_DEV + d], recv_sem=recv_sems.at[a * N_DEV + my],
                        device_id=dev, device_id_type=MESH).start()

    shape_of = {"win": (N_DEV, HALF_IN, WIN_W), "wout": (N_DEV, HALF_OUT, D_MODEL)}
    shapes = [jax.ShapeDtypeStruct(shape_of.get(k, t.shape), t.dtype) for t, k in zip(arrays, kinds)]
    return _split_start(name, arrays, shapes, na * N_DEV, issue)


def _scatter_finish(started, arrays, kinds, after, name):
    na = len(arrays)

    def finish(srcs, lands, send_sems, recv_sems):
        me = _me()
        my = _dev_index(me)
        for s in range(N_DEV):
            for a in range(na):
                @pl.when(my != s)
                def _(a=a, s=s):
                    cp = pltpu.make_async_remote_copy(
                        src_ref=_scatter_piece(kinds[a], srcs[a], s), dst_ref=lands[a].at[s],
                        send_sem=send_sems.at[a * N_DEV + s], recv_sem=recv_sems.at[a * N_DEV + s],
                        device_id=me, device_id_type=MESH)
                    cp.wait_recv()
                    cp.wait_send()

    got = _split_wait(name, started, na, after, finish)
    x, y, c = _me()
    chip, my = 2 * x + y, 4 * x + 2 * y + c
    own = {"win": lambda t: lax.dynamic_slice(t, (HALF_IN * c, WIN_STRIDE * chip), (HALF_IN, WIN_W)),
           "wout": lambda t: lax.dynamic_slice(t, (2 * HALF_OUT * chip + HALF_OUT * c, 0), (HALF_OUT, D_MODEL)),
           "small": lambda t: lax.dynamic_index_in_dim(t, my, 0, keepdims=False)}
    return [_put_own(g, own[k](t), my) for g, t, k in zip(got, arrays, kinds)]


def _share_reduced(pair_arrays, small, name):
    arrays = list(pair_arrays) + ([small] if small is not None else [])
    na, npair = len(arrays), len(pair_arrays)
    nm = len(ALL_MASKS)

    def body(*refs):
        srcs, dsts = refs[:na], refs[na:2 * na]
        send_sems, recv_sems = refs[2 * na:]
        me = _me()
        sib = _flip(me, (0, 0, 1))

        def copy(a, k, sender, to):
            slot = sender[2] if a < npair else _dev_index(sender)
            return pltpu.make_async_remote_copy(
                src_ref=srcs[a], dst_ref=dsts[a].at[slot], send_sem=send_sems.at[k], recv_sem=recv_sems.at[k],
                device_id=to, device_id_type=MESH)

        sends = [copy(a, a, me, sib) for a in range(npair)]
        if small is not None:
            sends += [copy(npair, npair + mi, me, _flip(me, mask)) for mi, mask in enumerate(ALL_MASKS)]
        for cp in sends:
            cp.start()
        for a in range(npair):
            copy(a, a, sib, me).wait_recv()
        if small is not None:
            for mi, mask in enumerate(ALL_MASKS):
                copy(npair, npair + mi, _flip(me, mask), me).wait_recv()
        for cp in sends:
            cp.wait_send()

    shapes = [jax.ShapeDtypeStruct((2,) + t.shape, t.dtype) for t in pair_arrays]
    if small is not None:
        shapes.append(jax.ShapeDtypeStruct((N_DEV,) + small.shape, small.dtype))
    got = _comm_call(name, body, arrays, shapes, npair + (nm if small is not None else 0))
    x, y, c = _me()
    slots = [c] * npair + [4 * x + 2 * y + c]
    return [_put_own(g, t, slot) for g, t, slot in zip(got, arrays, slots)]


def _sum_parts(parts, name):
    _, r, c = parts.shape
    tr = r
    while tr * c * 4 * N_DEV > (8 << 20) and tr % 32 == 0:
        tr //= 2

    def body(p_ref, o_ref):
        total = p_ref[0].astype(f32)
        for d in range(1, N_DEV):
            total = total + p_ref[d].astype(f32)
        o_ref[...] = total

    return pl.pallas_call(
        body, name=name, grid=(r // tr,), in_specs=[pl.BlockSpec((N_DEV, tr, c), lambda i: (0, i, 0))],
        out_specs=pl.BlockSpec((tr, c), lambda i: (i, 0)), out_shape=jax.ShapeDtypeStruct((r, c), f32),
        compiler_params=_params(("parallel",)),
    )(parts)


def _adamw(w, g, m, v, name):
    shape = w.shape
    cols = shape[-1]
    slabs = w.ndim == 3 and shape[1] < 8
    rows = w.size // cols
    tr = rows
    while tr * cols * 4 > (1 << 20) and tr % 16 == 0:
        tr //= 2
    c1 = 1.0 / (1.0 - ADAM_B1 ** ADAM_STEP)
    c2 = 1.0 / (1.0 - ADAM_B2 ** ADAM_STEP)

    def body(w_ref, g_ref, m_ref, v_ref, d_ref, nm_ref, nv_ref, *g_out):
        gv = g_ref[...]
        mv = ADAM_B1 * m_ref[...] + (1.0 - ADAM_B1) * gv
        vv = ADAM_B2 * v_ref[...] + (1.0 - ADAM_B2) * (gv * gv)
        nm_ref[...] = mv
        nv_ref[...] = vv
        d_ref[...] = -ADAM_LR * ((mv * c1) / (jnp.sqrt(vv * c2) + ADAM_EPS) + ADAM_WD * w_ref[...])
        for ref in g_out:
            ref[...] = gv

    if slabs:
        tl = max(d for d in range(1, shape[0] + 1) if shape[0] % d == 0 and d * shape[1] * cols * 4 <= (3 << 19))
        spec = pl.BlockSpec((tl, shape[1], cols), lambda i: (i, 0, 0))
        return tuple(pl.pallas_call(
            body, name=name, grid=(shape[0] // tl,), in_specs=[spec] * 4, out_specs=[spec] * 4,
            out_shape=[jax.ShapeDtypeStruct(shape, f32)] * 4, compiler_params=_params(("parallel",)),
        )(w, g, m, v))
    spec = pl.BlockSpec((tr, cols), lambda i: (i, 0))
    outs = pl.pallas_call(
        body, name=name, grid=(rows // tr,), in_specs=[spec] * 4, out_specs=[spec] * 3,
        out_shape=[jax.ShapeDtypeStruct((rows, cols), f32)] * 3, compiler_params=_params(("parallel",)),
    )(*[t.reshape(rows, cols) for t in (w, g, m, v)])
    return tuple(t.reshape(shape) for t in outs)


def _pad_lanes(t, n=HD):
    return jnp.pad(t, [(0, 0)] * (t.ndim - 1) + [(0, n - t.shape[-1])])


def _rows128(t, rows=None):
    t = t.reshape(-1, HD)
    rows = rows or -(-t.shape[0] // 8) * 8
    return jnp.pad(t, ((0, rows - t.shape[0]), (0, 0)))


def kernel(x, norm_w, w_in, lru_conv_w, lru_conv_b, lru_wa, lru_ba, lru_wx, lru_bx, lru_lambda, lru_norm_w, dn_conv_w, dn_A_log, dn_dt_bias, dn_norm_w, w_out, final_norm_w, loss_target, m_norm_w, m_w_in, m_lru_conv_w, m_lru_conv_b, m_lru_wa, m_lru_ba, m_lru_wx, m_lru_bx, m_lru_lambda, m_lru_norm_w, m_dn_conv_w, m_dn_A_log, m_dn_dt_bias, m_dn_norm_w, m_w_out, m_final_norm_w, v_norm_w, v_w_in, v_lru_conv_w, v_lru_conv_b, v_lru_wa, v_lru_ba, v_lru_wx, v_lru_bx, v_lru_lambda, v_lru_norm_w, v_dn_conv_w, v_dn_A_log, v_dn_dt_bias, v_dn_norm_w, v_w_out, v_final_norm_w):
    depth = norm_w.shape[0]
    xs = x[0]
    target = loss_target[0]
    chip = 2 * lax.axis_index("x") + lax.axis_index("y")

    win_b = [w_in[l].astype(bf16) for l in range(depth)]
    wout_b = [w_out[l].astype(bf16) for l in range(depth)]
    got_in0 = _gather_halves([win_b[0]], [], "gather_weights0")[0]
    rest0 = [wout_b[0], lru_conv_w, dn_conv_w]
    rest0_started = _gather_start(rest0, "gather_rest0_start")
    gathered, later = {}, {}
    pad_cols = jnp.zeros((D_MODEL, D_INP - D_IN), bf16)

    def weights_of(l, after):
        if l == 0:
            got, token = got_in0, rest0_started[-1]
        else:
            gathered[l] = _gather_finish(later[l], [win_b[l], wout_b[l]], after, f"gather_weights{l}_wait")
            got, token = gathered[l][0], None
        return jnp.concatenate([got[j] for j in range(4)] + [pad_cols], axis=1), token

    def rest_of(l, after):
        token = None
        if "rest0" not in gathered:
            gathered["rest0"] = _gather_finish(rest0_started, rest0, after, "gather_rest0_wait")
            for k in range(1, depth):
                later[k] = _gather_start([win_b[k], wout_b[k]], f"gather_weights{k}_start", after=gathered["rest0"][0])
                token = later[k][-1] if token is None else token + later[k][-1]
        g_out, g_lcw, g_dcw = gathered["rest0"]
        if l > 0:
            g_out = gathered[l][1]
        lcw_full = g_lcw.transpose(1, 2, 0, 3).reshape(depth, 4, D_MODEL)
        dcw_full = g_dcw.transpose(1, 2, 0, 3).reshape(depth, 4, 3 * D_MODEL)
        return g_out.reshape(D_MIX, D_MODEL), _behind(lcw_full[l], token), dcw_full[l]

    in_flight = {}

    def on_grad(kind, l, g):
        if l == depth - 1 and depth > 1:
            if kind == "wout":
                in_flight["held"] = g
                return None
            srcs, kinds, key = [g, in_flight.pop("held")], ["win", "wout"], ("both", l)
        else:
            srcs, kinds, key = [g], [kind], (kind, l)
        started = _scatter_start(srcs, kinds, f"scatter_{key[0]}{key[1]}_start")
        in_flight[key] = (started, srcs, kinds)
        return started[-1]

    vecs, dnvs = [], []
    zrow = jnp.zeros((D_MODEL,), f32)
    for l in range(depth):
        vecs.append(jnp.stack([lru_conv_b[l], lru_ba[l], lru_bx[l], lru_lambda[l], lru_norm_w[l], zrow, zrow, zrow]))
        dnvs.append(jnp.concatenate([_pad_lanes(dn_A_log[l][None]), _pad_lanes(dn_dt_bias[l][None]),
                                     dn_norm_w[l][None], jnp.zeros((5, HD), f32)], axis=0))

    loss_part, grad_x, d_fnw, g_nw, g_vec, g_wa, g_wx, g_lcw_, g_dcw_, g_dsc = _local_step(
        xs, target, norm_w, final_norm_w, weights_of, rest_of, on_grad, vecs, dnvs, lru_wa, lru_wx)
    loss = lax.psum(loss_part[0, 0], ("x", "y", "c"))
    grads = _reduce_grads(chip, grad_x, in_flight, d_fnw, g_nw, g_vec, g_wa, g_wx, g_lcw_, g_dcw_, g_dsc)

    names = ["norm_w", "w_in", "lru_conv_w", "lru_conv_b", "lru_wa", "lru_ba", "lru_wx", "lru_bx", "lru_lambda",
             "lru_norm_w", "dn_conv_w", "dn_A_log", "dn_dt_bias", "dn_norm_w", "w_out", "final_norm_w"]
    ws = dict(zip(names, [norm_w, w_in, lru_conv_w, lru_conv_b, lru_wa, lru_ba, lru_wx, lru_bx, lru_lambda, lru_norm_w,
                          dn_conv_w, dn_A_log, dn_dt_bias, dn_norm_w, w_out, final_norm_w]))
    ms = dict(zip(names, [m_norm_w, m_w_in, m_lru_conv_w, m_lru_conv_b, m_lru_wa, m_lru_ba, m_lru_wx, m_lru_bx,
                          m_lru_lambda, m_lru_norm_w, m_dn_conv_w, m_dn_A_log, m_dn_dt_bias, m_dn_norm_w, m_w_out,
                          m_final_norm_w]))
    vs = dict(zip(names, [v_norm_w, v_w_in, v_lru_conv_w, v_lru_conv_b, v_lru_wa, v_lru_ba, v_lru_wx, v_lru_bx,
                          v_lru_lambda, v_lru_norm_w, v_dn_conv_w, v_dn_A_log, v_dn_dt_bias, v_dn_norm_w, v_w_out,
                          v_final_norm_w]))
    to_cols = lambda t: jnp.transpose(t, (2, 0, 1))
    from_cols = lambda t: jnp.transpose(t, (1, 2, 0))
    deltas, new_m, new_v = [], [], []
    for n in names:
        if n in ("w_in", "lru_conv_w", "dn_conv_w"):
            view, back = (to_cols, from_cols) if n == "w_in" else (lambda t: t, lambda t: t)
            d, nm_, nv_, grads[n] = (back(t) for t in _adamw(view(ws[n]), view(grads[n]), view(ms[n]), view(vs[n]),
                                                            f"adamw_{n}"))
        else:
            d, nm_, nv_ = _adamw(ws[n], grads[n], ms[n], vs[n], f"adamw_{n}")
        deltas.append(d)
        new_m.append(nm_)
        new_v.append(nv_)
    return (loss, grad_x[None], *[grads[n] for n in names], *deltas, *new_m, *new_v)


def _behind(t, token):
    return t if token is None else t + token[0:1, 0:1]


def _local_step(xs, target, norm_w, final_norm_w, weights_of, rest_of, on_grad, vecs, dnvs, lru_wa, lru_wx):
    depth = norm_w.shape[0]
    saved = []
    h = xs
    for l in range(depth):
        wp_l, token = weights_of(l, h)
        hn, hnt = _rms_fwd(h, _behind(norm_w[l][None], token), f"rms_fwd{l}")
        proj = _matmul(hn, wp_l, tm=2048, tn=896, tk=D_MODEL, name=f"in_proj{l}")
        wo_l, lcw_l, dcw_l = rest_of(l, proj)
        y, hl = _lru_fwd(proj, lcw_l, vecs[l], lru_wa[l], lru_wx[l], f"lru_fwd{l}")
        y, o, states, tinvs, cpres = _dn_fwd(proj, dcw_l, dnvs[l], y, f"dn_fwd{l}")
        out = _matmul(y, wo_l, tm=1024, tn=D_MODEL, tk=D_MIX, res=h, name=f"out_proj{l}")
        saved.append((h, hnt, proj, hl, o, states, tinvs, cpres, y, wp_l, wo_l, lcw_l, dcw_l))
        h = out

    dh, d_fnw, loss_part = _final_loss(h, final_norm_w[None], target, "final_loss")

    g_nw, g_vec, g_wa, g_wx, g_lcw_, g_dcw_, g_dsc = ([None] * depth for _ in range(7))
    for l in reversed(range(depth)):
        hin, hnt, proj, hl, o, states, tinvs, cpres, y, wp_l, wo_l, lcw_l, dcw_l = saved[l]
        dy = _matmul(dh, wo_l, tb=True, tm=2048, tn=1024, tk=D_MODEL, name=f"d_y{l}")
        g_wout = _matmul(y, dh, ta=True, tm=1024, tn=D_MODEL, tk=512, out_dtype=bf16, name=f"d_wout{l}")
        token = on_grad("wout", l, g_wout)
        dlx, dlz, g_lcw_[l], g_vec[l], g_wa[l], g_wx[l] = _lru_bwd(dy, proj, hl, lcw_l, _behind(vecs[l], token),
                                                                 lru_wa[l], lru_wx[l], f"lru_bwd{l}")
        dq, dk, dv, dz, dba, dcw8, g_dsc[l] = _dn_bwd(dy, proj, o, states, tinvs, cpres, dcw_l, dnvs[l], f"dn_bwd{l}")
        g_dcw_[l] = dcw8.reshape(HEADS, 4, 3, HD).transpose(1, 2, 0, 3).reshape(4, 3 * D_MODEL)
        dxs = [dlx, dlz, dq, dk, dv, dz]
        g_win = _d_win(hnt, dxs, dba, f"d_win{l}")
        token = on_grad("win", l, g_win)
        dh, g_nw[l] = _d_hn(dxs, _behind(dba, token), wp_l, hin, norm_w[l][None], dh, f"d_hn{l}")
    return loss_part, dh, d_fnw, g_nw, g_vec, g_wa, g_wx, g_lcw_, g_dcw_, g_dsc


def _reduce_grads(chip, after, in_flight, d_fnw, g_nw, g_vec, g_wa, g_wx, g_lcw_, g_dcw_, g_dsc):
    depth = len(g_nw)
    both = lambda f: jnp.concatenate([f(l) for l in range(depth)])
    small = [
        both(lambda l: _rows128(g_nw[l])),
        both(lambda l: _rows128(g_vec[l][0])), both(lambda l: _rows128(g_vec[l][1])),
        both(lambda l: _rows128(g_vec[l][2])), both(lambda l: _rows128(g_vec[l][3])),
        both(lambda l: _rows128(g_vec[l][4])),
        both(lambda l: _rows128(g_wa[l])), both(lambda l: _rows128(g_wx[l])),
        both(lambda l: _rows128(g_dsc[l][0:1])), both(lambda l: _rows128(g_dsc[l][1:2])),
        both(lambda l: _rows128(g_dsc[l][2:3])),
        _rows128(d_fnw),
        both(lambda l: _rows128(g_lcw_[l])), both(lambda l: _rows128(g_dcw_[l])),
    ]
    counts = [t.shape[0] for t in small]
    total = sum(counts)
    per = -(-total // (8 * N_DEV)) * 8
    small = jnp.concatenate(small + [jnp.zeros((per * N_DEV - total, HD), f32)]).reshape(N_DEV, per, HD)

    small_started = _scatter_start([small], ["small"], "scatter_small_start")
    reduced, pairs = {}, [None] * depth
    for key in sorted(in_flight, key=lambda k: -k[1]):
        started, srcs, kinds = in_flight[key]
        got = _scatter_finish(started, srcs, kinds, after, f"scatter_{key[0]}{key[1]}_wait")
        for kind, part in zip(kinds, got):
            reduced[(kind, key[1])] = _sum_parts(part, f"sum_{kind}{key[1]}")
        after = reduced[(kinds[-1], key[1])]
        if key[1] > 0 and ("win", key[1]) in reduced and ("wout", key[1]) in reduced:
            pairs[key[1]] = _share_reduced([reduced[("win", key[1])], reduced[("wout", key[1])]], None,
                                           f"share_grads{key[1]}")
            after = pairs[key[1]][0]
    got = _scatter_finish(small_started, [small], ["small"], after, "scatter_small_wait")
    shared = _share_reduced([reduced[("win", 0)], reduced[("wout", 0)]], _sum_parts(got[0], "sum_small"), "share_grads0")
    pairs[0] = shared
    a_small = shared[2].reshape(N_DEV * per, HD)
    grad_w_in = jnp.stack([lax.dynamic_slice_in_dim(pairs[l][0].reshape(D_MODEL, WIN_W), 4 * chip, SHARD_IN, 1)
                           for l in range(depth)])
    grad_w_out = jnp.stack([pairs[l][1].reshape(D_MIX // 4, D_MODEL) for l in range(depth)])

    offs = [0]
    for c in counts:
        offs.append(offs[-1] + c)

    def piece(k, rows_per_layer=None):
        t = a_small[offs[k]:offs[k + 1]]
        if rows_per_layer is None:
            return t
        return t.reshape(depth, -1, HD)[:, :rows_per_layer]

    vec = lambda k: piece(k).reshape(depth, D_MODEL)
    return {
        "norm_w": vec(0), "lru_conv_b": vec(1), "lru_ba": vec(2), "lru_bx": vec(3), "lru_lambda": vec(4),
        "lru_norm_w": vec(5),
        "lru_wa": piece(6).reshape(depth, HEADS, HD, HD), "lru_wx": piece(7).reshape(depth, HEADS, HD, HD),
        "dn_A_log": piece(8, 1)[:, 0, :HEADS], "dn_dt_bias": piece(9, 1)[:, 0, :HEADS], "dn_norm_w": piece(10, 1)[:, 0],
        "final_norm_w": piece(11).reshape(D_MODEL),
        "lru_conv_w": lax.dynamic_slice_in_dim(piece(12).reshape(depth, 4, D_MODEL), chip * (D_MODEL // 4), D_MODEL // 4, 2),
        "dn_conv_w": lax.dynamic_slice_in_dim(piece(13).reshape(depth, 4, 3 * D_MODEL), chip * (3 * D_MODEL // 4),
                                              3 * D_MODEL // 4, 2),
        "w_in": grad_w_in, "w_out": grad_w_out,
    }
```

```python
import jax
import jax.numpy as jnp
from jax import lax
from jax.experimental import pallas as pl
from jax.experimental.pallas import tpu as pltpu

f32 = jnp.float32
bf16 = jnp.bfloat16
MESH = pl.DeviceIdType.MESH

D_MODEL = 1024
HEADS = 8
HD = 128
CHUNK = 64
LRU_T = 128
SEQ_BLOCK = 1024
LRU_HP = 4
DN_SEQ_BLOCK = 256
DN_HP = 8
LRU_C = 8.0
D_IN = 6160
D_INP = 6272
D_MIX = 2048
N_GROUPS = 6
BLK_BA = 48
SHARD_IN = D_IN // 4
WIN_W = 1664
WIN_STRIDE = 1536
EPS = 1e-6
QSCALE = HD ** -0.5
N_DEV = 8
VMEM_LIMIT = 56 * 1024 * 1024

ADAM_LR, ADAM_B1, ADAM_B2, ADAM_EPS, ADAM_WD, ADAM_STEP = 0.001, 0.9, 0.999, 1e-08, 0.01, 10


def _sig(x):
    return 1.0 / (1.0 + jnp.exp(-x))


def _log1p(u):
    w = 1.0 + u
    d = w - 1.0
    return jnp.where(d == 0.0, u, jnp.log(w) * (u / jnp.where(d == 0.0, 1.0, d)))


def _softplus(x):
    return jnp.maximum(x, 0.0) + _log1p(jnp.exp(-jnp.abs(x)))


def _expm1(x):
    t = x * (1.0 + x * (0.5 + x * (1.0 / 6 + x * (1.0 / 24 + x * (1.0 / 120 + x * (1.0 / 720))))))
    return jnp.where(jnp.abs(x) < 0.2, t, jnp.exp(x) - 1.0)


def _dot(a, b, prec=None):
    return jnp.dot(a, b, precision=prec, preferred_element_type=f32)


def _dot_nt(a, b, prec=None):
    return lax.dot_general(a, b, (((1,), (1,)), ((), ())), precision=prec, preferred_element_type=f32)


def _dot_tn(a, b, prec=None):
    return lax.dot_general(a, b, (((0,), (0,)), ((), ())), precision=prec, preferred_element_type=f32)


def _b(x):
    return x.astype(bf16)


def _sum0(x):
    return jnp.sum(x, axis=0, keepdims=True)


def _sum1(x):
    return jnp.sum(x, axis=1, keepdims=True)


def _rowmean(x):
    return jnp.mean(x, axis=-1, keepdims=True)


def _dsilu(z, sg):
    return sg * (1.0 + z * (1.0 - sg))


def _conv_taps(x, halo):
    xx = jnp.concatenate([halo, x], axis=0)
    return [pltpu.roll(xx, 3, axis=0)[8:], pltpu.roll(xx, 2, axis=0)[8:], pltpu.roll(xx, 1, axis=0)[8:], x]


def _conv(xs, cw):
    return cw[0:1] * xs[0] + cw[1:2] * xs[1] + cw[2:3] * xs[2] + cw[3:4] * xs[3]


def _rows_before(ref, halo_ref, lanes, t0, c, sblk):
    tprev = pl.multiple_of(jnp.maximum(t0 - 8, 0), 8)
    return jnp.where(c > 0, ref[pl.ds(tprev, 8), lanes], jnp.where(sblk > 0, halo_ref[:, lanes], 0.0))


def _conv_back(dxc, halo_after, cw, x=None):
    rows = dxc.shape[0]
    dd = jnp.concatenate([dxc, halo_after], axis=0)
    out = cw[3:4] * dxc
    dcw = [None, None, None, None if x is None else _sum0(x * dxc)]
    for s in (1, 2, 3):
        shifted = pltpu.roll(dd, rows + 8 - s, axis=0)[:rows]
        out = out + cw[3 - s:4 - s] * shifted
        if x is not None:
            dcw[3 - s] = _sum0(x * shifted)
    return out if x is None else (out, jnp.concatenate(dcw, axis=0))


def _params(sem=None):
    return pltpu.CompilerParams(dimension_semantics=sem, vmem_limit_bytes=VMEM_LIMIT)


def _seq_specs(sb, width, rowblk, colblk):
    main = pl.BlockSpec((sb, width), lambda a, b: (rowblk(a, b), colblk(a, b)))
    halo = pl.BlockSpec((8, width), lambda a, b: (jnp.maximum(rowblk(a, b) * (sb // 8) - 1, 0), colblk(a, b)))
    return main, halo


def _matmul(a, b, *, ta=False, tb=False, tm, tn, tk, res=None, out_dtype=f32, name):
    if ta:
        kdim, m = a.shape
    else:
        m, kdim = a.shape
    n = b.shape[0] if tb else b.shape[1]
    tm, tn, tk = min(tm, m), min(tn, n), min(tk, kdim)
    assert m % tm == 0 and n % tn == 0 and kdim % tk == 0, (name, m, n, kdim, tm, tn, tk)
    nk = kdim // tk
    dims = (((0 if ta else 1,), (1 if tb else 0,)), ((), ()))
    has_res = res is not None

    def body(*refs):
        a_ref, b_ref = refs[:2]
        r_ref = refs[2] if has_res else None
        o_ref = refs[3] if has_res else refs[2]
        acc_ref = refs[-1] if nk > 1 else None
        part = lax.dot_general(_b(a_ref[...]), _b(b_ref[...]), dims, preferred_element_type=f32)

        def finish(total):
            if has_res:
                total = total + r_ref[...]
            o_ref[...] = total.astype(out_dtype)

        if nk == 1:
            finish(part)
        else:
            k = pl.program_id(2)

            @pl.when(k == 0)
            def _():
                acc_ref[...] = part

            @pl.when(k > 0)
            def _():
                acc_ref[...] += part

            @pl.when(k == nk - 1)
            def _():
                finish(acc_ref[...])

    a_spec = pl.BlockSpec((tk, tm), lambda i, j, k: (k, i)) if ta else pl.BlockSpec((tm, tk), lambda i, j, k: (i, k))
    b_spec = pl.BlockSpec((tn, tk), lambda i, j, k: (j, k)) if tb else pl.BlockSpec((tk, tn), lambda i, j, k: (k, j))
    o_spec = pl.BlockSpec((tm, tn), lambda i, j, k: (i, j))
    in_specs, args = [a_spec, b_spec], [a, b]
    if has_res:
        in_specs.append(o_spec)
        args.append(res)
    return pl.pallas_call(
        body, name=name, grid=(m // tm, n // tn, nk), in_specs=in_specs, out_specs=o_spec,
        out_shape=jax.ShapeDtypeStruct((m, n), out_dtype),
        scratch_shapes=[pltpu.VMEM((tm, tn), f32)] if nk > 1 else [],
        compiler_params=_params(("parallel", "parallel", "arbitrary")),
    )(*args)


def _d_hn(dxs, dba, wp, x, w, dres, name):
    s = dxs[0].shape[0]
    tm = min(256, s)

    def body(*refs):
        dx_refs = refs[:N_GROUPS]
        dba_ref, w_ref, x_ref, nw_ref, dres_ref, o_ref, dw_ref = refs[N_GROUPS:]
        dhn = _dot_nt(_b(dba_ref[...]), w_ref[:, BLK_BA * HD:])
        for g in range(N_GROUPS):
            dhn = dhn + _dot_nt(dx_refs[g][...], w_ref[:, g * D_MODEL:(g + 1) * D_MODEL])
        xv = x_ref[...]
        r = lax.rsqrt(_rowmean(xv * xv) + EPS)
        xn = xv * r
        dxn = dhn * nw_ref[...]
        o_ref[...] = dres_ref[...] + r * (dxn - xn * _rowmean(dxn * xn))
        part = _sum0(dhn * xn)

        @pl.when(pl.program_id(0) == 0)
        def _():
            dw_ref[...] = part

        @pl.when(pl.program_id(0) > 0)
        def _():
            dw_ref[...] += part

    row = lambda wd: pl.BlockSpec((tm, wd), lambda i: (i, 0))
    vec = pl.BlockSpec((1, D_MODEL), lambda i: (0, 0))
    return pl.pallas_call(
        body, name=name, grid=(s // tm,),
        in_specs=[row(D_MODEL)] * N_GROUPS + [row(HD), pl.BlockSpec((D_MODEL, D_INP), lambda i: (0, 0)),
                                              row(D_MODEL), vec, row(D_MODEL)],
        out_specs=[row(D_MODEL), vec],
        out_shape=[jax.ShapeDtypeStruct((s, D_MODEL), f32), jax.ShapeDtypeStruct((1, D_MODEL), f32)],
        compiler_params=_params(("arbitrary",)),
    )(*dxs, dba, wp, x, w, dres)


def _d_win(hnt, dxs, dba, name):
    s = hnt.shape[1]
    tn = 256
    per = D_MODEL // tn

    def body(*refs):
        hnt_ref = refs[0]
        dx_refs = refs[1:1 + N_GROUPS]
        o_ref = refs[1 + N_GROUPS]
        j = pl.program_id(0)
        for g in range(N_GROUPS):
            @pl.when(j // per == g)
            def _(g=g):
                o_ref[...] = _dot(hnt_ref[...], _b(dx_refs[g][...])).astype(bf16)

    def dx_spec(g):
        on = lambda j: (j // per == g).astype(jnp.int32)
        return pl.BlockSpec((s, tn), lambda j: (0, (j % per) * on(j)))

    main = pl.pallas_call(
        body, name=name, grid=(N_GROUPS * per,),
        in_specs=[pl.BlockSpec((D_MODEL, s), lambda j: (0, 0))] + [dx_spec(g) for g in range(N_GROUPS)],
        out_specs=pl.BlockSpec((D_MODEL, tn), lambda j: (0, j)),
        out_shape=jax.ShapeDtypeStruct((D_MODEL, D_INP), bf16),
        compiler_params=_params(("parallel",)),
    )(hnt, *dxs)

    def tail(hnt_ref, dba_ref, full_ref, o_ref):
        del full_ref
        o_ref[...] = _dot(hnt_ref[...], _b(dba_ref[...])).astype(bf16)

    return pl.pallas_call(
        tail, name=name + "_ba", grid=(1,),
        in_specs=[pl.BlockSpec((D_MODEL, s), lambda k: (0, 0)), pl.BlockSpec((s, HD), lambda k: (0, 0)),
                  pl.BlockSpec(memory_space=pl.ANY)],
        out_specs=pl.BlockSpec((D_MODEL, HD), lambda k: (0, BLK_BA)),
        out_shape=jax.ShapeDtypeStruct((D_MODEL, D_INP), bf16),
        input_output_aliases={2: 0},
        compiler_params=_params(("arbitrary",)),
    )(hnt, dba, main)


def _rms_fwd(x, w, name):
    s = x.shape[0]
    tr = min(512, s)

    def body(x_ref, w_ref, h_ref, ht_ref):
        xv = x_ref[...]
        r = lax.rsqrt(_rowmean(xv * xv) + EPS)
        h = (xv * r * w_ref[...]).astype(bf16)
        h_ref[...] = h
        ht_ref[...] = h.T

    return pl.pallas_call(
        body, name=name, grid=(s // tr,),
        in_specs=[pl.BlockSpec((tr, D_MODEL), lambda i: (i, 0)), pl.BlockSpec((1, D_MODEL), lambda i: (0, 0))],
        out_specs=[pl.BlockSpec((tr, D_MODEL), lambda i: (i, 0)), pl.BlockSpec((D_MODEL, tr), lambda i: (0, i))],
        out_shape=[jax.ShapeDtypeStruct((s, D_MODEL), bf16), jax.ShapeDtypeStruct((D_MODEL, s), bf16)],
        compiler_params=_params(("parallel",)),
    )(x, w)


def _final_loss(x, w, target, name):
    s = x.shape[0]
    tr = min(512, s)

    def body(x_ref, w_ref, t_ref, dx_ref, dw_ref, loss_ref):
        xv = x_ref[...]
        wv = w_ref[...]
        r = lax.rsqrt(_rowmean(xv * xv) + EPS)
        xn = xv * r
        e = xn * wv - t_ref[...]
        lb = jnp.broadcast_to(0.5 * _sum0(_rowmean(e * e)), (1, HD))
        dy = e * (1.0 / D_MODEL)
        dxn = dy * wv
        dx_ref[...] = r * (dxn - xn * _rowmean(dxn * xn))
        part = _sum0(dy * xn)

        @pl.when(pl.program_id(0) == 0)
        def _():
            dw_ref[...] = part
            loss_ref[...] = lb

        @pl.when(pl.program_id(0) > 0)
        def _():
            dw_ref[...] += part
            loss_ref[...] += lb

    row = pl.BlockSpec((tr, D_MODEL), lambda i: (i, 0))
    vec = pl.BlockSpec((1, D_MODEL), lambda i: (0, 0))
    lsp = pl.BlockSpec((1, HD), lambda i: (0, 0))
    return pl.pallas_call(
        body, name=name, grid=(s // tr,), in_specs=[row, vec, row], out_specs=[row, vec, lsp],
        out_shape=[jax.ShapeDtypeStruct((s, D_MODEL), f32), jax.ShapeDtypeStruct((1, D_MODEL), f32),
                   jax.ShapeDtypeStruct((1, HD), f32)],
        compiler_params=_params(("arbitrary",)),
    )(x, w, target)


def _lru_gates(xc, vec, sp, wa, wx):
    xcb = _b(xc)
    r = _sig(_dot(xcb, wa) + vec[1:2])
    i = _sig(_dot(xcb, wx) + vec[2:3])
    la = (-LRU_C) * r * sp
    a = jnp.exp(la)
    mult = jnp.sqrt(-_expm1(2.0 * la))
    return r, i, a, mult, sp, xcb


def _lru_fwd(proj, cw, vec, wa, wx, name):
    s = proj.shape[0]
    sb = min(SEQ_BLOCK, s)
    nsb = s // sb
    t = min(LRU_T, sb)
    nc = sb // t
    hp = LRU_HP
    wide = hp * HD
    lanes = [slice(j * HD, (j + 1) * HD) for j in range(hp)]

    def body(x_ref, xh_ref, z_ref, cw_ref, vec_ref, wa_ref, wx_ref, y_ref, hl_ref, hc_ref):
        sblk = pl.program_id(1)
        consts = [(cw_ref[:, hl], vec_ref[:, hl], _softplus(-vec_ref[3:4, hl]), _b(wa_ref[j]), _b(wx_ref[j]))
                  for j, hl in enumerate(lanes)]
        row = lax.broadcasted_iota(jnp.int32, (t, HD), 0)

        @pl.when(sblk == 0)
        def _():
            hc_ref[...] = jnp.zeros((8, wide), f32)

        def one_head(j, x, halo, z, hcarry):
            cwv, vec_v, sp_v, wa_v, wx_v = consts[j]
            xs = _conv_taps(x, halo)
            xc = vec_v[0:1] + _conv(xs, cwv)
            r, i, a, mult, _, _ = _lru_gates(xc, vec_v, sp_v, wa_v, wx_v)
            yield
            bb = mult * (i * xc)
            d = 1
            while d < t:
                m = row >= d
                bb = jnp.where(m, a * pltpu.roll(bb, d, axis=0) + bb, bb)
                a = jnp.where(m, a * pltpu.roll(a, d, axis=0), a)
                d *= 2
                yield
            hl = bb + a * hcarry
            nrm = lax.rsqrt(_rowmean(hl * hl) + EPS)
            return hl, (hl * nrm * vec_v[4:5] * (z * _sig(z))).astype(bf16)

        def chunk(c, carries):
            t0 = pl.multiple_of(c * t, t)
            rows = pl.ds(t0, t)
            ins = [(x_ref[rows, hl], _rows_before(x_ref, xh_ref, hl, t0, c, sblk), z_ref[rows, hl]) for hl in lanes]
            outs = _round_robin([one_head(j, *ins[j], carries[j]) for j in range(hp)])
            for j, hl in enumerate(lanes):
                hl_ref[rows, hl] = outs[j][0]
                y_ref[rows, hl] = outs[j][1]
            return tuple(out[0][t - 1:t, :] for out in outs)

        final = lax.fori_loop(0, nc, chunk, tuple(hc_ref[0:1, hl] for hl in lanes))
        hc_ref[...] = jnp.concatenate([jnp.broadcast_to(f, (8, HD)) for f in final], axis=1)

    xsp, xhalo = _seq_specs(sb, wide, lambda g, i: i, lambda g, i: g)
    zsp, _ = _seq_specs(sb, wide, lambda g, i: i, lambda g, i: HEADS // hp + g)
    gcol = lambda g, i: (0, g)
    wsp = pl.BlockSpec((hp, HD, HD), lambda g, i: (g, 0, 0))
    osp = pl.BlockSpec((sb, wide), lambda g, i: (i, g))
    return pl.pallas_call(
        body, name=name, grid=(HEADS // hp, nsb),
        in_specs=[xsp, xhalo, zsp, pl.BlockSpec((4, wide), gcol), pl.BlockSpec((8, wide), gcol), wsp, wsp],
        out_specs=[osp, osp],
        out_shape=[jax.ShapeDtypeStruct((s, D_MIX), bf16), jax.ShapeDtypeStruct((s, D_MODEL), f32)],
        scratch_shapes=[pltpu.VMEM((8, wide), f32)],
        compiler_params=_params(("parallel", "arbitrary")),
    )(proj, proj, proj, cw, vec, wa, wx)


def _lru_bwd(dy, proj, hl, cw, vec, wa, wx, name):
    s = proj.shape[0]
    sb = min(SEQ_BLOCK, s)
    nsb = s // sb
    t = min(LRU_T, sb)
    nc = sb // t
    hp = LRU_HP
    wide = hp * HD
    lanes = [slice(j * HD, (j + 1) * HD) for j in range(hp)]
    n_acc = 10

    def body(dy_ref, x_ref, xh_ref, z_ref, hl_ref, hlh_ref, cw_ref, vec_ref, wa_ref, wx_ref,
             dx_ref, dz_ref, dcw_ref, dvec_ref, dwa_ref, dwx_ref, acc_ref, dxh_ref):
        step = pl.program_id(1)
        sblk = nsb - 1 - step
        consts = [(cw_ref[:, hl], vec_ref[:, hl], _softplus(-vec_ref[3:4, hl]), _b(wa_ref[j]), _b(wx_ref[j]))
                  for j, hl in enumerate(lanes)]
        row = lax.broadcasted_iota(jnp.int32, (t, HD), 0)

        @pl.when(step == 0)
        def _():
            acc_ref[...] = jnp.zeros((16, wide), f32)
            dxh_ref[...] = jnp.zeros((8, wide), f32)
            dwa_ref[...] = jnp.zeros((hp, HD, HD), f32)
            dwx_ref[...] = jnp.zeros((hp, HD, HD), f32)

        def one_head(j, x, halo, z, hv, hhalo, dyv, carry):
            dcw0, dcw1, dcw2, dcw3, dcb, dba, dbx, dsp, dlnw, mu, dxc_halo = carry
            cwv, vec_v, sp_v, wa_v, wx_v = consts[j]
            lnw = vec_v[4:5]
            xs = _conv_taps(x, halo)
            xc = vec_v[0:1] + _conv(xs, cwv)
            r, i, a, mult, sp, xcb = _lru_gates(xc, vec_v, sp_v, wa_v, wx_v)
            yield
            hprev = pltpu.roll(jnp.concatenate([hhalo, hv], axis=0), 1, axis=0)[8:]
            sgz = _sig(z)
            sz = z * sgz
            nrm = lax.rsqrt(_rowmean(hv * hv) + EPS)
            hn = hv * nrm
            dlnw = dlnw + _sum0(dyv * hn * sz)
            dzv = _b(dyv * hn * lnw * _dsilu(z, sgz))
            dhn = dyv * lnw * sz
            lam = nrm * (dhn - hn * _rowmean(dhn * hn))
            cf = jnp.where(row == t - 1, 1.0, pltpu.roll(a, t - 1, axis=0))
            d = 1
            while d < t:
                m = row < t - d
                lam = jnp.where(m, lam + cf * pltpu.roll(lam, t - d, axis=0), lam)
                cf = jnp.where(m, cf * pltpu.roll(cf, t - d, axis=0), cf)
                d *= 2
                yield
            lam = lam + cf * mu
            mu = a[0:1] * lam[0:1]
            da = lam * hprev
            dmult = lam * (i * xc)
            di = lam * mult * xc
            dxc = lam * mult * i
            dla = da * a - dmult * (a * a) / mult
            dr = dla * (-LRU_C * sp)
            dsp = dsp + _sum0(dla * (-LRU_C * r))
            dra = dr * r * (1.0 - r)
            dia = di * i * (1.0 - i)
            dba = dba + _sum0(dra)
            dbx = dbx + _sum0(dia)
            drab, diab = _b(dra), _b(dia)
            dwa = _dot_tn(xcb, drab)
            dwx = _dot_tn(xcb, diab)
            dxc = dxc + _dot_nt(drab, wa_v) + _dot_nt(diab, wx_v)
            yield
            dcb = dcb + _sum0(dxc)
            dcw0 = dcw0 + _sum0(dxc * xs[0])
            dcw1 = dcw1 + _sum0(dxc * xs[1])
            dcw2 = dcw2 + _sum0(dxc * xs[2])
            dcw3 = dcw3 + _sum0(dxc * xs[3])
            dxv = _b(_conv_back(dxc, dxc_halo, cwv))
            return dxv, dzv, dwa, dwx, (dcw0, dcw1, dcw2, dcw3, dcb, dba, dbx, dsp, dlnw, mu, dxc[0:8])

        def chunk(ci, carries):
            c = nc - 1 - ci
            t0 = pl.multiple_of(c * t, t)
            rows = pl.ds(t0, t)
            ins = [(x_ref[rows, hl], _rows_before(x_ref, xh_ref, hl, t0, c, sblk), z_ref[rows, hl], hl_ref[rows, hl],
                    _rows_before(hl_ref, hlh_ref, hl, t0, c, sblk), dy_ref[rows, hl]) for hl in lanes]
            outs = _round_robin([one_head(j, *ins[j], carries[j]) for j in range(hp)])
            for j, hl in enumerate(lanes):
                dx_ref[rows, hl] = outs[j][0]
                dz_ref[rows, hl] = outs[j][1]
                dwa_ref[j] += outs[j][2]
                dwx_ref[j] += outs[j][3]
            return tuple(out[4] for out in outs)

        acc = acc_ref[...]
        dxh = dxh_ref[...]
        init = tuple(tuple(acc[k:k + 1, hl] for k in range(n_acc)) + (dxh[:, hl],) for hl in lanes)
        final = lax.fori_loop(0, nc, chunk, init)
        rows_out = [jnp.concatenate([final[j][k] for j in range(hp)], axis=1) for k in range(n_acc)]
        zero = jnp.zeros((1, wide), f32)
        acc_ref[...] = jnp.concatenate(rows_out + [zero] * (16 - n_acc), axis=0)
        dxh_ref[...] = jnp.concatenate([final[j][n_acc] for j in range(hp)], axis=1)

        @pl.when(step == nsb - 1)
        def _():
            dcw_ref[...] = jnp.concatenate(rows_out[0:4], axis=0)
            dlam = -rows_out[7] * _sig(-vec_ref[3:4, :])
            dvec_ref[...] = jnp.concatenate([rows_out[4], rows_out[5], rows_out[6], dlam, rows_out[8], zero, zero, zero],
                                            axis=0)

    rblk = lambda g, i: nsb - 1 - i
    xsp, xhalo = _seq_specs(sb, wide, rblk, lambda g, i: g)
    zsp, _ = _seq_specs(sb, wide, rblk, lambda g, i: HEADS // hp + g)
    gcol = lambda g, i: (0, g)
    wsp = pl.BlockSpec((hp, HD, HD), lambda g, i: (g, 0, 0))
    return pl.pallas_call(
        body, name=name, grid=(HEADS // hp, nsb),
        in_specs=[xsp, xsp, xhalo, zsp, xsp, xhalo, pl.BlockSpec((4, wide), gcol), pl.BlockSpec((8, wide), gcol), wsp, wsp],
        out_specs=[xsp, xsp, pl.BlockSpec((4, wide), gcol), pl.BlockSpec((8, wide), gcol), wsp, wsp],
        out_shape=[jax.ShapeDtypeStruct((s, D_MODEL), bf16), jax.ShapeDtypeStruct((s, D_MODEL), bf16),
                   jax.ShapeDtypeStruct((4, D_MODEL), f32), jax.ShapeDtypeStruct((8, D_MODEL), f32),
                   jax.ShapeDtypeStruct((HEADS, HD, HD), f32), jax.ShapeDtypeStruct((HEADS, HD, HD), f32)],
        scratch_shapes=[pltpu.VMEM((16, wide), f32), pltpu.VMEM((8, wide), f32)],
        compiler_params=_params(("parallel", "arbitrary")),
    )(dy, proj, proj, proj, hl, hl, cw, vec, wa, wx)


def _lane_pick(x, lane, idx):
    return _sum1(jnp.where(lane == idx, x, 0.0))


def _round_robin(gens):
    results = [None] * len(gens)
    live = list(range(len(gens)))
    while live:
        for i in list(live):
            try:
                next(gens[i])
            except StopIteration as done:
                results[i] = done.value
                live.remove(i)
    return results


def _sdot(a, b):
    return _dot(_b(a), _b(b))


def _sdot_tn(a, b):
    return _dot_tn(_b(a), _b(b))


def _sdot_nt(a, b):
    return _dot_nt(_b(a), _b(b))


def _tri_dot(tri, x):
    trib = tri.astype(bf16)
    x1 = x.astype(bf16)
    r1 = x - x1.astype(f32)
    x2 = r1.astype(bf16)
    x3 = (r1 - x2.astype(f32)).astype(bf16)
    return _dot(trib, x1) + _dot(trib, x2) + _dot(trib, x3)


def _inv_unit_lower(a):
    n = -a
    p = _sdot(a, a)
    yield
    for k in range(5):
        n = n + p + _sdot(n, p)
        if k < 4:
            p = _sdot(p, p)
        yield
    return n


def _dn_gates(bav, dnv, masks):
    ltri = masks[4]
    ea = jnp.exp(pltpu.roll(dnv[0:1], HEADS, axis=1))
    dt = pltpu.roll(dnv[1:2], HEADS, axis=1)
    u = bav + dt
    g = -ea * _softplus(u)
    return dict(beta=_sig(bav), g=g, gc=_tri_dot(ltri, g), sga=_sig(u), ea=ea)


def _dn_chunk(x3, halo3, gates, cw3, h, masks, cpre=None, tinv=None):
    lane, ri, ci, eye, ltri = masks
    if cpre is None:
        cpre = _conv(_conv_taps(x3, halo3), cw3)
    sg = _sig(cpre)
    act = cpre * sg
    q_raw, k_raw, v = act[:, 0:HD], act[:, HD:2 * HD], act[:, 2 * HD:3 * HD]
    rq = lax.rsqrt(_sum1(q_raw * q_raw) + EPS)
    rk = lax.rsqrt(_sum1(k_raw * k_raw) + EPS)
    qn = q_raw * rq
    k = k_raw * rk
    q = qn * QSCALE
    beta = jnp.broadcast_to(_lane_pick(gates["beta"], lane, h), (CHUNK, HD))
    gc = jnp.broadcast_to(_lane_pick(gates["gc"], lane, HEADS + h), (CHUNK, HD))
    yield
    gc64 = gc[:, 0:CHUNK]
    grow = _sum0(gc64 * eye)
    dm = jnp.exp(jnp.where(ri >= ci, gc64 - grow, -1e30))
    ds_ = jnp.where(ri > ci, dm, 0.0)
    eg = jnp.exp(gc)
    glast = gc[CHUNK - 1:CHUNK, :]
    ek = jnp.exp(glast - gc)
    kb = k * beta
    kbf = _b(k)
    amat = _dot_nt(_b(kb), kbf) * ds_
    pmat = _dot_nt(_b(q), kbf) * dm
    yield
    if tinv is None:
        tinv = yield from _inv_unit_lower(amat)
    return dict(cpre=cpre, sg=sg, rq=rq, rk=rk, qn=qn, q=q, k=k, v=v, kbf=kbf, beta=beta, gc=gc, dm=dm, ds=ds_, eg=eg,
                glast=glast, ek=ek, kb=kb, amat=amat, tinv=tinv, pmat=pmat)


def _dn_masks():
    lane = lax.broadcasted_iota(jnp.int32, (CHUNK, HD), 1)
    ri = lax.broadcasted_iota(jnp.int32, (CHUNK, CHUNK), 0)
    ci = lax.broadcasted_iota(jnp.int32, (CHUNK, CHUNK), 1)
    eye = (ri == ci).astype(f32)
    ltri = (ri >= ci).astype(f32)
    return lane, ri, ci, eye, ltri


def _dn_load_qkv(q_ref, qh_ref, k_ref, kh_ref, v_ref, vh_ref, hl, rows, t0, c, sblk):
    x3 = jnp.concatenate([q_ref[rows, hl], k_ref[rows, hl], v_ref[rows, hl]], axis=1)
    halo3 = jnp.concatenate([_rows_before(q_ref, qh_ref, hl, t0, c, sblk), _rows_before(k_ref, kh_ref, hl, t0, c, sblk),
                             _rows_before(v_ref, vh_ref, hl, t0, c, sblk)], axis=1)
    return x3, halo3


def _dn_cw3(cwq_ref, cwk_ref, cwv_ref, j):
    return jnp.concatenate([r[:, j * HD:(j + 1) * HD] for r in (cwq_ref, cwk_ref, cwv_ref)], axis=1)


def _dn_fwd(proj, cw, dnv, y, name):
    s = proj.shape[0]
    sb = min(DN_SEQ_BLOCK, s)
    nsb = s // sb
    nc = sb // CHUNK
    hp = DN_HP

    def body(q_ref, qh_ref, k_ref, kh_ref, v_ref, vh_ref, z_ref, ba_ref, cwq_ref, cwk_ref, cwv_ref, dnv_ref, yin_ref,
             y_ref, o_ref, st_ref, ti_ref, cp_ref, s_ref):
        del yin_ref
        grp = pl.program_id(0)
        sblk = pl.program_id(1)
        masks = _dn_masks()
        dnv_v = dnv_ref[...]
        cw3 = [_dn_cw3(cwq_ref, cwk_ref, cwv_ref, j) for j in range(hp)]

        @pl.when(sblk == 0)
        def _():
            s_ref[...] = jnp.zeros((hp, HD, HD), f32)

        def one_head(j, x3, halo3, gates, z, st):
            f = yield from _dn_chunk(x3, halo3, gates, cw3[j], grp * hp + j, masks)
            sbf = _b(st)
            qs = _dot(_b(f["q"] * f["eg"]), sbf)
            rhs = f["beta"] * (f["v"] - _dot(_b(f["k"] * f["eg"]), sbf))
            yield
            vn = rhs + _sdot(f["tinv"], rhs)
            yield
            vnb = _b(vn)
            o = qs + _dot(_b(f["pmat"]), vnb)
            st_new = st * jnp.exp(f["glast"]) + _dot_tn(_b(f["k"] * f["ek"]), vnb)
            yield
            nrm = lax.rsqrt(_rowmean(o * o) + EPS)
            yv = (o * nrm * dnv_v[2:3] * (z * _sig(z))).astype(bf16)
            return o, yv, st_new, f["tinv"], f["cpre"]

        def chunk(c, states):
            t0 = pl.multiple_of(c * CHUNK, CHUNK)
            rows = pl.ds(t0, CHUNK)
            bav = ba_ref[rows, :]
            ins = []
            for j in range(hp):
                hl = slice(j * HD, (j + 1) * HD)
                ins.append(_dn_load_qkv(q_ref, qh_ref, k_ref, kh_ref, v_ref, vh_ref, hl, rows, t0, c, sblk)
                           + (z_ref[rows, hl],))
            gates = _dn_gates(bav, dnv_v, masks)
            outs = _round_robin([one_head(j, ins[j][0], ins[j][1], gates, ins[j][2], states[j]) for j in range(hp)])
            for j in range(hp):
                hl = slice(j * HD, (j + 1) * HD)
                st_ref[j, c] = states[j]
                o_ref[rows, hl] = outs[j][0]
                y_ref[rows, hl] = outs[j][1]
                ti_ref[j, c] = outs[j][3]
                cp_ref[rows, j * 3 * HD:(j + 1) * 3 * HD] = outs[j][4]
            return tuple(out[2] for out in outs)

        final = lax.fori_loop(0, nc, chunk, tuple(s_ref[j] for j in range(hp)))
        for j in range(hp):
            s_ref[j] = final[j]

    wide = hp * HD
    grp_specs = lambda off: _seq_specs(sb, wide, lambda g, i: i, lambda g, i: off // hp + g)
    (qsp, qhalo), (ksp, khalo), (vsp, vhalo) = grp_specs(2 * HEADS), grp_specs(3 * HEADS), grp_specs(4 * HEADS)
    zsp, _ = grp_specs(5 * HEADS)
    basp = pl.BlockSpec((sb, HD), lambda g, i: (i, BLK_BA))
    cws = lambda off: pl.BlockSpec((4, wide), lambda g, i: (0, off // hp + g))
    osp = lambda off: pl.BlockSpec((sb, wide), lambda g, i: (i, off // hp + g))
    return pl.pallas_call(
        body, name=name, grid=(HEADS // hp, nsb),
        in_specs=[qsp, qhalo, ksp, khalo, vsp, vhalo, zsp, basp, cws(0), cws(HEADS), cws(2 * HEADS),
                  pl.BlockSpec((8, HD), lambda g, i: (0, 0)), pl.BlockSpec(memory_space=pl.ANY)],
        out_specs=[osp(HEADS), osp(0), pl.BlockSpec((hp, nc, HD, HD), lambda g, i: (g, i, 0, 0)),
                   pl.BlockSpec((hp, nc, CHUNK, CHUNK), lambda g, i: (g, i, 0, 0)),
                   pl.BlockSpec((sb, 3 * wide), lambda g, i: (i, g))],
        out_shape=[jax.ShapeDtypeStruct((s, D_MIX), bf16), jax.ShapeDtypeStruct((s, D_MODEL), f32),
                   jax.ShapeDtypeStruct((HEADS, s // CHUNK, HD, HD), f32),
                   jax.ShapeDtypeStruct((HEADS, s // CHUNK, CHUNK, CHUNK), f32),
                   jax.ShapeDtypeStruct((s, 3 * D_MODEL), f32)],
        scratch_shapes=[pltpu.VMEM((hp, HD, HD), f32)],
        input_output_aliases={12: 0},
        compiler_params=_params(("parallel", "arbitrary")),
    )(proj, proj, proj, proj, proj, proj, proj, proj, cw, cw, cw, dnv, y)


def _dn_bwd(dy, proj, o, states, tinvs, cpres, cw, dnv, name):
    s = proj.shape[0]
    sb = min(DN_SEQ_BLOCK, s)
    nsb = s // sb
    nc = sb // CHUNK
    hp = DN_HP
    ngrp = HEADS // hp

    def body(dy_ref, q_ref, k_ref, v_ref, z_ref, ba_ref, o_ref, st_ref, ti_ref, cp_ref,
             cwq_ref, cwk_ref, cwv_ref, dnv_ref,
             dq_ref, dk_ref, dv_ref, dz_ref, dba_ref, dcw_ref, dsc_ref,
             ds_ref, dch_ref, cwacc_ref, nacc_ref):
        step = pl.program_id(0)
        grp = pl.program_id(1)
        sblk = nsb - 1 - step
        h0 = grp * hp
        masks = _dn_masks()
        lane, ri, ci, eye, ltri = masks
        utri = (ri <= ci).astype(f32)
        cw3s = [_dn_cw3(cwq_ref, cwk_ref, cwv_ref, j) for j in range(hp)]
        dnv_v = dnv_ref[...]
        dnw = dnv_v[2:3]
        row64 = lax.broadcasted_iota(jnp.int32, (CHUNK, HD), 0)

        @pl.when(step == 0)
        def _():
            for j in range(hp):
                ds_ref[h0 + j] = jnp.zeros((HD, HD), f32)
                dch_ref[h0 + j] = jnp.zeros((8, 3 * HD), f32)
                cwacc_ref[h0 + j] = jnp.zeros((8, 3 * HD), f32)

        @pl.when((step == 0) & (grp == 0))
        def _():
            nacc_ref[...] = jnp.zeros((8, HD), f32)

        def one_head(j, x3, gates, z, st, ti, cp, ov, dyv, carry):
            dst, dch, cwacc = carry
            cw3 = cw3s[j]
            f = yield from _dn_chunk(None, None, gates, cw3, h0 + j, masks, cp, ti)
            q, k, v, beta, eg, ek, kbf = f["q"], f["k"], f["v"], f["beta"], f["eg"], f["ek"], f["kbf"]
            sbf = _b(st)
            dstb = _b(dst)
            qe = q * eg
            keg = k * eg
            kd = k * ek
            kegb, qeb, kdb = _b(keg), _b(qe), _b(kd)
            e = v - _dot(kegb, sbf)
            yield
            rhs = beta * e
            vn = rhs + _sdot(f["tinv"], rhs)
            yield
            vnb = _b(vn)
            pb = _b(f["pmat"])
            sgz = _sig(z)
            sz = z * sgz
            nrm = lax.rsqrt(_rowmean(ov * ov) + EPS)
            on = ov * nrm
            ddnw = _sum0(dyv * on * sz)
            dpz = dyv * on * dnw * _dsilu(z, sgz)
            don = dyv * dnw * sz
            do = nrm * (don - on * _rowmean(don * on))
            dob = _b(do)
            dvn = _dot_tn(pb, dob) + _dot(kdb, dstb)
            dpm = jnp.where(ri >= ci, _dot_nt(dob, vnb), 0.0)
            dqe = _dot_nt(dob, sbf)
            dkd = _dot_nt(vnb, dstb)
            qtdo = _dot_tn(qeb, dob)
            yield
            drhs = dvn + _sdot_tn(f["tinv"], dvn)
            dqk = _b(dpm * f["dm"])
            dq = _dot(dqk, kbf) + dqe * eg
            dk_p = _dot_tn(dqk, _b(q))
            yield
            dam = jnp.where(ri > ci, -_sdot_nt(drhs, vn), 0.0)
            de = drhs * beta
            deb = _b(de)
            dbeta = _sum1(drhs * e)
            dkeg = -_dot_nt(deb, sbf)
            dst_new = qtdo + dst * jnp.exp(f["glast"]) - _dot_tn(kegb, deb)
            yield
            dkk = _b(dam * f["ds"])
            dkb = _dot(dkk, kbf)
            dk = dk_p + _dot_tn(dkk, _b(f["kb"])) + dkb * beta + dkeg * eg + dkd * ek
            yield
            dbeta = dbeta + _sum1(dkb * k)
            mm = dpm * f["pmat"] + dam * f["amat"]
            dglast = _sum1(_sum0(dst * st)) * jnp.exp(f["glast"]) + _sum1(_sum0(dkd * kd))
            dgc = (_sum1(mm) - _sum1(eye * _sum0(mm)) + _sum1(dqe * qe) + _sum1(dkeg * keg) - _sum1(dkd * kd))
            dgc = jnp.broadcast_to(dgc, (CHUNK, HD)) + jnp.where(row64 == CHUNK - 1, dglast, 0.0)
            dq_raw = (QSCALE * f["rq"]) * (dq - f["qn"] * _sum1(f["qn"] * dq))
            dk_raw = f["rk"] * (dk - k * _sum1(k * dk))
            dact = jnp.concatenate([dq_raw, dk_raw, de], axis=1)
            dc = dact * _dsilu(f["cpre"], f["sg"])
            dqkv, dcw_part = _conv_back(dc, dch, cw3, x3)
            cwacc = cwacc + dcw_part
            return dpz, dqkv, (dgc, dbeta), ddnw, (dst_new, dc[0:8], cwacc)

        def chunk(cidx, carry):
            heads, nacc, sa0, sa1 = carry
            c = nc - 1 - cidx
            t0 = pl.multiple_of(c * CHUNK, CHUNK)
            rows = pl.ds(t0, CHUNK)
            bav = ba_ref[rows, :]
            ins = []
            for j in range(hp):
                hl = slice(j * HD, (j + 1) * HD)
                x3 = jnp.concatenate([q_ref[rows, hl], k_ref[rows, hl], v_ref[rows, hl]], axis=1)
                ins.append((x3, z_ref[rows, hl], st_ref[j, c], ti_ref[j, c],
                            cp_ref[rows, j * 3 * HD:(j + 1) * 3 * HD], o_ref[rows, hl], dy_ref[rows, hl]))
            gates = _dn_gates(bav, dnv_v, masks)
            outs = _round_robin([one_head(j, ins[j][0], gates, *ins[j][1:], heads[j]) for j in range(hp)])
            dgc_all = jnp.zeros((CHUNK, HD), f32)
            dbeta_all = jnp.zeros((CHUNK, HD), f32)
            for j in range(hp):
                hl = slice(j * HD, (j + 1) * HD)
                dpz, dqkv, (dgc, dbeta), ddnw, _ = outs[j]
                dgc_all = jnp.where(lane == HEADS + h0 + j, dgc, dgc_all)
                dbeta_all = jnp.where(lane == h0 + j, dbeta, dbeta_all)
                dq_ref[rows, hl] = _b(dqkv[:, 0:HD])
                dk_ref[rows, hl] = _b(dqkv[:, HD:2 * HD])
                dv_ref[rows, hl] = _b(dqkv[:, 2 * HD:3 * HD])
                dz_ref[rows, hl] = _b(dpz)
                nacc = nacc + ddnw
            dg_all = _tri_dot(utri, dgc_all)
            dain_all = dg_all * (-gates["ea"]) * gates["sga"]
            dba = dbeta_all * gates["beta"] * (1.0 - gates["beta"]) + dain_all
            sa0 = sa0 + _sum0(dg_all * gates["g"])
            sa1 = sa1 + _sum0(dain_all)

            @pl.when(grp == 0)
            def _():
                dba_ref[rows, :] = dba

            @pl.when(grp > 0)
            def _():
                dba_ref[rows, :] += dba

            return tuple(out[4] for out in outs), nacc, sa0, sa1

        init = tuple((ds_ref[h0 + j], dch_ref[h0 + j], cwacc_ref[h0 + j][0:4]) for j in range(hp))
        heads, nacc, sa0, sa1 = lax.fori_loop(
            0, nc, chunk, (init, nacc_ref[0:1, :], nacc_ref[1:2, :], nacc_ref[2:3, :]))
        nacc_ref[0:3, :] = jnp.concatenate([nacc, sa0, sa1], axis=0)
        zpad = jnp.zeros((4, 3 * HD), f32)
        for j in range(hp):
            dst, dch, cwacc = heads[j]
            ds_ref[h0 + j] = dst
            dch_ref[h0 + j] = dch
            cwacc_ref[h0 + j] = jnp.concatenate([cwacc, zpad], axis=0)

        @pl.when(step == nsb - 1)
        def _():
            for j in range(hp):
                dcw_ref[h0 + j] = heads[j][2]

            @pl.when(grp == ngrp - 1)
            def _():
                dsc_ref[...] = jnp.concatenate([pltpu.roll(sa0, HD - HEADS, axis=1), pltpu.roll(sa1, HD - HEADS, axis=1),
                                                nacc, jnp.zeros((5, HD), f32)], axis=0)

    wide = hp * HD
    rblk = lambda i, g: nsb - 1 - i
    grp_specs = lambda off: _seq_specs(sb, wide, rblk, lambda i, g: off // hp + g)
    (qsp, qhalo), (ksp, khalo), (vsp, vhalo) = grp_specs(2 * HEADS), grp_specs(3 * HEADS), grp_specs(4 * HEADS)
    zsp, _ = grp_specs(5 * HEADS)
    hsp = lambda off: pl.BlockSpec((sb, wide), lambda i, g: (rblk(i, g), off // hp + g))
    basp = lambda col: pl.BlockSpec((sb, HD), lambda i, g: (rblk(i, g), col))
    cws = lambda off: pl.BlockSpec((4, wide), lambda i, g: (0, off // hp + g))
    whole = lambda shape: pl.BlockSpec(shape, lambda i, g: (0,) * len(shape))
    return pl.pallas_call(
        body, name=name, grid=(nsb, ngrp),
        in_specs=[hsp(HEADS), qsp, ksp, vsp, zsp, basp(BLK_BA), hsp(0),
                  pl.BlockSpec((hp, nc, HD, HD), lambda i, g: (g, rblk(i, g), 0, 0)),
                  pl.BlockSpec((hp, nc, CHUNK, CHUNK), lambda i, g: (g, rblk(i, g), 0, 0)),
                  pl.BlockSpec((sb, 3 * wide), lambda i, g: (rblk(i, g), g)),
                  cws(0), cws(HEADS), cws(2 * HEADS), whole((8, HD))],
        out_specs=[hsp(0), hsp(0), hsp(0), hsp(0), basp(0), whole((HEADS, 4, 3 * HD)), whole((8, HD))],
        out_shape=[jax.ShapeDtypeStruct((s, D_MODEL), bf16)] * 4
        + [jax.ShapeDtypeStruct((s, HD), f32), jax.ShapeDtypeStruct((HEADS, 4, 3 * HD), f32),
           jax.ShapeDtypeStruct((8, HD), f32)],
        scratch_shapes=[pltpu.VMEM((HEADS, HD, HD), f32), pltpu.VMEM((HEADS, 8, 3 * HD), f32),
                        pltpu.VMEM((HEADS, 8, 3 * HD), f32), pltpu.VMEM((8, HD), f32)],
        compiler_params=_params(("arbitrary", "arbitrary")),
    )(dy, proj, proj, proj, proj, proj, o, states, tinvs, cpres, cw, cw, cw, dnv)


ANY = pl.BlockSpec(memory_space=pl.ANY)
CHIP_MASKS = ((1, 0, 0), (0, 1, 0), (1, 1, 0))
ALL_MASKS = tuple((m >> 2 & 1, m >> 1 & 1, m & 1) for m in range(1, N_DEV))


def _me():
    return lax.axis_index("x"), lax.axis_index("y"), lax.axis_index("c")


def _flip(me, mask):
    return tuple((1 - v) if m else v for v, m in zip(me, mask))


def _dev_index(dev):
    return 4 * dev[0] + 2 * dev[1] + dev[2]


def _chip(dev):
    return 2 * dev[0] + dev[1]


def _comm_call(name, body, arrays, out_shapes, nsem, nloc=0):
    scratch = [pltpu.SemaphoreType.DMA((nsem,)), pltpu.SemaphoreType.DMA((nsem,))]
    if nloc:
        scratch.append(pltpu.SemaphoreType.DMA((nloc,)))
    return pl.pallas_call(
        body, name=name, in_specs=[ANY] * len(arrays), out_specs=[ANY] * len(out_shapes), out_shape=out_shapes,
        scratch_shapes=scratch, compiler_params=pltpu.CompilerParams(has_side_effects=True),
    )(*arrays)


def _put_own(gathered, own, slot):
    return lax.dynamic_update_index_in_dim(gathered, own, slot, 0)


def _gather_halves(arrays, whole_arrays, name):
    na, nw = len(arrays), len(whole_arrays)
    nm = len(CHIP_MASKS)

    def body(*refs):
        srcs, dsts = refs[:na + nw], refs[na + nw:2 * (na + nw)]
        send_sems, recv_sems = refs[2 * (na + nw):]
        me = _me()
        sib = _flip(me, (0, 0, 1))

        def half(a, ref, core):
            rows = arrays[a].shape[0] // 2
            return ref.at[pl.ds(pl.multiple_of(core * rows, rows), rows)]

        def over_ici(a, mi, sender, to):
            if a < na:
                src, dst = half(a, srcs[a], sender[2]), half(a, dsts[a].at[_chip(sender)], sender[2])
            else:
                src, dst = srcs[a], dsts[a].at[_chip(sender)]
            return pltpu.make_async_remote_copy(src_ref=src, dst_ref=dst, send_sem=send_sems.at[a * nm + mi],
                                                recv_sem=recv_sems.at[a * nm + mi], device_id=to, device_id_type=MESH)

        def over_d2d(a, mi, origin, sender, to):
            k = (na + nw) * nm + a * nm + mi
            blk = half(a, dsts[a].at[_chip(origin)], sender[2])
            return pltpu.make_async_remote_copy(src_ref=blk, dst_ref=blk, send_sem=send_sems.at[k],
                                                recv_sem=recv_sems.at[k], device_id=to, device_id_type=MESH)

        sends = []
        for a in range(na + nw):
            for mi, mask in enumerate(CHIP_MASKS):
                cp = over_ici(a, mi, me, _flip(me, mask))
                cp.start()
                sends.append(cp)
        for a in range(na + nw):
            for mi, mask in enumerate(CHIP_MASKS):
                peer = _flip(me, mask)
                over_ici(a, mi, peer, me).wait_recv()
                if a < na:
                    cp = over_d2d(a, mi, peer, me, sib)
                    cp.start()
                    sends.append(cp)
        for a in range(na):
            for mi, mask in enumerate(CHIP_MASKS):
                over_d2d(a, mi, _flip(sib, mask), sib, me).wait_recv()
        for cp in sends:
            cp.wait_send()

    every = list(arrays) + list(whole_arrays)
    got = _comm_call(name, body, every, [jax.ShapeDtypeStruct((4,) + t.shape, t.dtype) for t in every],
                     (2 * na + nw) * nm)
    chip = 2 * lax.axis_index("x") + lax.axis_index("y")
    return [_put_own(g, t, chip) for g, t in zip(got, every)]


HBM = pl.BlockSpec(memory_space=pltpu.HBM)
SEM = pl.BlockSpec(memory_space=pltpu.SEMAPHORE)
DATAFLOW = pltpu.SideEffectType.DATAFLOW_SIDE_EFFECTING


def _split_start(name, srcs, land_shapes, nsem, issue, after=None):
    ns, nl = len(srcs), len(land_shapes)
    n_in = ns + nl + (after is not None)

    def body(*refs):
        issue(refs[:ns], refs[ns:ns + nl], refs[n_in + ns + nl], refs[n_in + ns + nl + 1])
        token = refs[-1]
        token[...] = jnp.zeros_like(token)

    lands = [lax.empty(t.shape, t.dtype) for t in land_shapes]
    thru = [pltpu.HBM(t.shape, t.dtype) for t in list(srcs) + list(land_shapes)]
    hbm = lambda t: pltpu.with_memory_space_constraint(t, pltpu.HBM)
    return pl.pallas_call(
        body, name=name, in_specs=[HBM] * (ns + nl) + [ANY] * (after is not None),
        out_shape=(*thru, pltpu.SemaphoreType.DMA((nsem,)), pltpu.SemaphoreType.DMA((nsem,)),
                   jax.ShapeDtypeStruct((8, HD), f32)),
        out_specs=(*[HBM] * (ns + nl), SEM, SEM, pl.BlockSpec(memory_space=pltpu.VMEM)),
        input_output_aliases={i: i for i in range(ns + nl)},
        compiler_params=pltpu.CompilerParams(has_side_effects=DATAFLOW),
    )(*[hbm(t) for t in srcs], *[hbm(t) for t in lands], *([after] if after is not None else []))


def _split_wait(name, started, ns, after, finish):
    thru, send_sems, recv_sems = started[:-3], started[-3], started[-2]
    nt = len(thru)

    def body(*refs):
        finish(refs[:ns], refs[ns:nt], refs[nt], refs[nt + 1])

    outs = pl.pallas_call(
        body, name=name, in_specs=[HBM] * nt + [SEM, SEM, ANY],
        out_shape=tuple(pltpu.HBM(t.shape, t.dtype) for t in thru), out_specs=tuple([HBM] * nt),
        input_output_aliases={i: i for i in range(nt)},
        compiler_params=pltpu.CompilerParams(has_side_effects=DATAFLOW),
    )(*thru, send_sems, recv_sems, after)
    return list(outs[ns:])


def _gather_start(arrays, name, after=None):
    nm = len(CHIP_MASKS)

    def issue(srcs, lands, send_sems, recv_sems):
        me = _me()
        for a in range(len(arrays)):
            for mi, mask in enumerate(CHIP_MASKS):
                pltpu.make_async_remote_copy(
                    src_ref=srcs[a], dst_ref=lands[a].at[_chip(me)], send_sem=send_sems.at[a * nm + mi],
                    recv_sem=recv_sems.at[a * nm + mi], device_id=_flip(me, mask), device_id_type=MESH).start()

    shapes = [jax.ShapeDtypeStruct((4,) + t.shape, t.dtype) for t in arrays]
    return _split_start(name, arrays, shapes, len(arrays) * nm, issue, after)


def _gather_finish(started, arrays, after, name):
    nm = len(CHIP_MASKS)

    def finish(srcs, lands, send_sems, recv_sems):
        me = _me()
        for a in range(len(arrays)):
            for mi, mask in enumerate(CHIP_MASKS):
                peer = _flip(me, mask)
                cp = pltpu.make_async_remote_copy(
                    src_ref=srcs[a], dst_ref=lands[a].at[_chip(peer)], send_sem=send_sems.at[a * nm + mi],
                    recv_sem=recv_sems.at[a * nm + mi], device_id=peer, device_id_type=MESH)
                cp.wait_send()
                cp.wait_recv()

    got = _split_wait(name, started, len(arrays), after, finish)
    chip = 2 * lax.axis_index("x") + lax.axis_index("y")
    return [_put_own(g, t, chip) for g, t in zip(got, arrays)]


HALF_IN, HALF_OUT = D_MODEL // 2, D_MIX // 8


def _scatter_piece(kind, ref, d):
    j, r = d >> 1, d & 1
    if kind == "win":
        return ref.at[pl.ds(HALF_IN * r, HALF_IN), pl.ds(WIN_STRIDE * j, WIN_W)]
    if kind == "wout":
        return ref.at[pl.ds(2 * HALF_OUT * j + HALF_OUT * r, HALF_OUT)]
    return ref.at[d]


def _scatter_start(arrays, kinds, name):
    na = len(arrays)

    def issue(srcs, lands, send_sems, recv_sems):
        me = _me()
        my = _dev_index(me)
        for d in range(N_DEV):
            dev = (d >> 2, (d >> 1) & 1, d & 1)
            for a in range(na):
                @pl.when(my != d)
                def _(a=a, d=d, dev=dev):
                    pltpu.make_async_remote_copy(
                        src_ref=_scatter_piece(kinds[a], srcs[a], d), dst_ref=lands[a].at[my],
                        send_sem=send_sems.at[a * N_DEV + d], recv_sem=recv_sems.at[a * N_DEV + my],
                        device_id=dev, device_id_type=MESH).start()

    shape_of = {"win": (N_DEV, HALF_IN, WIN_W), "wout": (N_DEV, HALF_OUT, D_MODEL)}
    shapes = [jax.ShapeDtypeStruct(shape_of.get(k, t.shape), t.dtype) for t, k in zip(arrays, kinds)]
    return _split_start(name, arrays, shapes, na * N_DEV, issue)


def _scatter_finish(started, arrays, kinds, after, name):
    na = len(arrays)

    def finish(srcs, lands, send_sems, recv_sems):
        me = _me()
        my = _dev_index(me)
        for s in range(N_DEV):
            for a in range(na):
                @pl.when(my != s)
                def _(a=a, s=s):
                    cp = pltpu.make_async_remote_copy(
                        src_ref=_scatter_piece(kinds[a], srcs[a], s), dst_ref=lands[a].at[s],
                        send_sem=send_sems.at[a * N_DEV + s], recv_sem=recv_sems.at[a * N_DEV + s],
                        device_id=me, device_id_type=MESH)
                    cp.wait_recv()
                    cp.wait_send()

    got = _split_wait(name, started, na, after, finish)
    x, y, c = _me()
    chip, my = 2 * x + y, 4 * x + 2 * y + c
    own = {"win": lambda t: lax.dynamic_slice(t, (HALF_IN * c, WIN_STRIDE * chip), (HALF_IN, WIN_W)),
           "wout": lambda t: lax.dynamic_slice(t, (2 * HALF_OUT * chip + HALF_OUT * c, 0), (HALF_OUT, D_MODEL)),
           "small": lambda t: lax.dynamic_index_in_dim(t, my, 0, keepdims=False)}
    return [_put_own(g, own[k](t), my) for g, t, k in zip(got, arrays, kinds)]


def _share_reduced(pair_arrays, small, name):
    arrays = list(pair_arrays) + ([small] if small is not None else [])
    na, npair = len(arrays), len(pair_arrays)
    nm = len(ALL_MASKS)

    def body(*refs):
        srcs, dsts = refs[:na], refs[na:2 * na]
        send_sems, recv_sems = refs[2 * na:]
        me = _me()
        sib = _flip(me, (0, 0, 1))

        def copy(a, k, sender, to):
            slot = sender[2] if a < npair else _dev_index(sender)
            return pltpu.make_async_remote_copy(
                src_ref=srcs[a], dst_ref=dsts[a].at[slot], send_sem=send_sems.at[k], recv_sem=recv_sems.at[k],
                device_id=to, device_id_type=MESH)

        sends = [copy(a, a, me, sib) for a in range(npair)]
        if small is not None:
            sends += [copy(npair, npair + mi, me, _flip(me, mask)) for mi, mask in enumerate(ALL_MASKS)]
        for cp in sends:
            cp.start()
        for a in range(npair):
            copy(a, a, sib, me).wait_recv()
        if small is not None:
            for mi, mask in enumerate(ALL_MASKS):
                copy(npair, npair + mi, _flip(me, mask), me).wait_recv()
        for cp in sends:
            cp.wait_send()

    shapes = [jax.ShapeDtypeStruct((2,) + t.shape, t.dtype) for t in pair_arrays]
    if small is not None:
        shapes.append(jax.ShapeDtypeStruct((N_DEV,) + small.shape, small.dtype))
    got = _comm_call(name, body, arrays, shapes, npair + (nm if small is not None else 0))
    x, y, c = _me()
    slots = [c] * npair + [4 * x + 2 * y + c]
    return [_put_own(g, t, slot) for g, t, slot in zip(got, arrays, slots)]


def _sum_parts(parts, name):
    _, r, c = parts.shape
    tr = r
    while tr * c * 4 * N_DEV > (8 << 20) and tr % 32 == 0:
        tr //= 2

    def body(p_ref, o_ref):
        total = p_ref[0].astype(f32)
        for d in range(1, N_DEV):
            total = total + p_ref[d].astype(f32)
        o_ref[...] = total

    return pl.pallas_call(
        body, name=name, grid=(r // tr,), in_specs=[pl.BlockSpec((N_DEV, tr, c), lambda i: (0, i, 0))],
        out_specs=pl.BlockSpec((tr, c), lambda i: (i, 0)), out_shape=jax.ShapeDtypeStruct((r, c), f32),
        compiler_params=_params(("parallel",)),
    )(parts)


def _adamw(w, g, m, v, name):
    shape = w.shape
    cols = shape[-1]
    slabs = w.ndim == 3 and shape[1] < 8
    rows = w.size // cols
    tr = rows
    while tr * cols * 4 > (1 << 20) and tr % 16 == 0:
        tr //= 2
    c1 = 1.0 / (1.0 - ADAM_B1 ** ADAM_STEP)
    c2 = 1.0 / (1.0 - ADAM_B2 ** ADAM_STEP)

    def body(w_ref, g_ref, m_ref, v_ref, d_ref, nm_ref, nv_ref, *g_out):
        gv = g_ref[...]
        mv = ADAM_B1 * m_ref[...] + (1.0 - ADAM_B1) * gv
        vv = ADAM_B2 * v_ref[...] + (1.0 - ADAM_B2) * (gv * gv)
        nm_ref[...] = mv
        nv_ref[...] = vv
        d_ref[...] = -ADAM_LR * ((mv * c1) / (jnp.sqrt(vv * c2) + ADAM_EPS) + ADAM_WD * w_ref[...])
        for ref in g_out:
            ref[...] = gv

    if slabs:
        tl = max(d for d in range(1, shape[0] + 1) if shape[0] % d == 0 and d * shape[1] * cols * 4 <= (3 << 19))
        spec = pl.BlockSpec((tl, shape[1], cols), lambda i: (i, 0, 0))
        return tuple(pl.pallas_call(
            body, name=name, grid=(shape[0] // tl,), in_specs=[spec] * 4, out_specs=[spec] * 4,
            out_shape=[jax.ShapeDtypeStruct(shape, f32)] * 4, compiler_params=_params(("parallel",)),
        )(w, g, m, v))
    spec = pl.BlockSpec((tr, cols), lambda i: (i, 0))
    outs = pl.pallas_call(
        body, name=name, grid=(rows // tr,), in_specs=[spec] * 4, out_specs=[spec] * 3,
        out_shape=[jax.ShapeDtypeStruct((rows, cols), f32)] * 3, compiler_params=_params(("parallel",)),
    )(*[t.reshape(rows, cols) for t in (w, g, m, v)])
    return tuple(t.reshape(shape) for t in outs)


def _pad_lanes(t, n=HD):
    return jnp.pad(t, [(0, 0)] * (t.ndim - 1) + [(0, n - t.shape[-1])])


def _rows128(t, rows=None):
    t = t.reshape(-1, HD)
    rows = rows or -(-t.shape[0] // 8) * 8
    return jnp.pad(t, ((0, rows - t.shape[0]), (0, 0)))


def kernel(x, norm_w, w_in, lru_conv_w, lru_conv_b, lru_wa, lru_ba, lru_wx, lru_bx, lru_lambda, lru_norm_w, dn_conv_w, dn_A_log, dn_dt_bias, dn_norm_w, w_out, final_norm_w, loss_target, m_norm_w, m_w_in, m_lru_conv_w, m_lru_conv_b, m_lru_wa, m_lru_ba, m_lru_wx, m_lru_bx, m_lru_lambda, m_lru_norm_w, m_dn_conv_w, m_dn_A_log, m_dn_dt_bias, m_dn_norm_w, m_w_out, m_final_norm_w, v_norm_w, v_w_in, v_lru_conv_w, v_lru_conv_b, v_lru_wa, v_lru_ba, v_lru_wx, v_lru_bx, v_lru_lambda, v_lru_norm_w, v_dn_conv_w, v_dn_A_log, v_dn_dt_bias, v_dn_norm_w, v_w_out, v_final_norm_w):
    depth = norm_w.shape[0]
    xs = x[0]
    target = loss_target[0]
    chip = 2 * lax.axis_index("x") + lax.axis_index("y")

    win_b = [w_in[l].astype(bf16) for l in range(depth)]
    wout_b = [w_out[l].astype(bf16) for l in range(depth)]
    got_in0 = _gather_halves([win_b[0]], [], "gather_weights0")[0]
    rest0 = [wout_b[0], lru_conv_w, dn_conv_w]
    rest0_started = _gather_start(rest0, "gather_rest0_start")
    gathered, later = {}, {}
    pad_cols = jnp.zeros((D_MODEL, D_INP - D_IN), bf16)

    def weights_of(l, after):
        if l == 0:
            got, token = got_in0, rest0_started[-1]
        else:
            gathered[l] = _gather_finish(later[l], [win_b[l], wout_b[l]], after, f"gather_weights{l}_wait")
            got, token = gathered[l][0], None
        return jnp.concatenate([got[j] for j in range(4)] + [pad_cols], axis=1), token

    def rest_of(l, after):
        token = None
        if "rest0" not in gathered:
            gathered["rest0"] = _gather_finish(rest0_started, rest0, after, "gather_rest0_wait")
            for k in range(1, depth):
                later[k] = _gather_start([win_b[k], wout_b[k]], f"gather_weights{k}_start", after=gathered["rest0"][0])
                token = later[k][-1] if token is None else token + later[k][-1]
        g_out, g_lcw, g_dcw = gathered["rest0"]
        if l > 0:
            g_out = gathered[l][1]
        lcw_full = g_lcw.transpose(1, 2, 0, 3).reshape(depth, 4, D_MODEL)
        dcw_full = g_dcw.transpose(1, 2, 0, 3).reshape(depth, 4, 3 * D_MODEL)
        return g_out.reshape(D_MIX, D_MODEL), _behind(lcw_full[l], token), dcw_full[l]

    in_flight = {}

    def on_grad(kind, l, g):
        if l == depth - 1 and depth > 1:
            if kind == "wout":
                in_flight["held"] = g
                return None
            srcs, kinds, key = [g, in_flight.pop("held")], ["win", "wout"], ("both", l)
        else:
            srcs, kinds, key = [g], [kind], (kind, l)
        started = _scatter_start(srcs, kinds, f"scatter_{key[0]}{key[1]}_start")
        in_flight[key] = (started, srcs, kinds)
        return started[-1]

    vecs, dnvs = [], []
    zrow = jnp.zeros((D_MODEL,), f32)
    for l in range(depth):
        vecs.append(jnp.stack([lru_conv_b[l], lru_ba[l], lru_bx[l], lru_lambda[l], lru_norm_w[l], zrow, zrow, zrow]))
        dnvs.append(jnp.concatenate([_pad_lanes(dn_A_log[l][None]), _pad_lanes(dn_dt_bias[l][None]),
                                     dn_norm_w[l][None], jnp.zeros((5, HD), f32)], axis=0))

    loss_part, grad_x, d_fnw, g_nw, g_vec, g_wa, g_wx, g_lcw_, g_dcw_, g_dsc = _local_step(
        xs, target, norm_w, final_norm_w, weights_of, rest_of, on_grad, vecs, dnvs, lru_wa, lru_wx)
    loss = lax.psum(loss_part[0, 0], ("x", "y", "c"))
    grads = _reduce_grads(chip, grad_x, in_flight, d_fnw, g_nw, g_vec, g_wa, g_wx, g_lcw_, g_dcw_, g_dsc)

    names = ["norm_w", "w_in", "lru_conv_w", "lru_conv_b", "lru_wa", "lru_ba", "lru_wx", "lru_bx", "lru_lambda",
             "lru_norm_w", "dn_conv_w", "dn_A_log", "dn_dt_bias", "dn_norm_w", "w_out", "final_norm_w"]
    ws = dict(zip(names, [norm_w, w_in, lru_conv_w, lru_conv_b, lru_wa, lru_ba, lru_wx, lru_bx, lru_lambda, lru_norm_w,
                          dn_conv_w, dn_A_log, dn_dt_bias, dn_norm_w, w_out, final_norm_w]))
    ms = dict(zip(names, [m_norm_w, m_w_in, m_lru_conv_w, m_lru_conv_b, m_lru_wa, m_lru_ba, m_lru_wx, m_lru_bx,
                          m_lru_lambda, m_lru_norm_w, m_dn_conv_w, m_dn_A_log, m_dn_dt_bias, m_dn_norm_w, m_w_out,
                          m_final_norm_w]))
    vs = dict(zip(names, [v_norm_w, v_w_in, v_lru_conv_w, v_lru_conv_b, v_lru_wa, v_lru_ba, v_lru_wx, v_lru_bx,
                          v_lru_lambda, v_lru_norm_w, v_dn_conv_w, v_dn_A_log, v_dn_dt_bias, v_dn_norm_w, v_w_out,
                          v_final_norm_w]))
    to_cols = lambda t: jnp.transpose(t, (2, 0, 1))
    from_cols = lambda t: jnp.transpose(t, (1, 2, 0))
    updates = {}
    for n in [k for k in names if k != "w_in"] + ["w_in"]:
        if n == "w_in":
            grads[n] = grads[n](updates["w_out"][0])
        if n in ("w_in", "lru_conv_w", "dn_conv_w"):
            view, back = (to_cols, from_cols) if n == "w_in" else (lambda t: t, lambda t: t)
            d, nm_, nv_, grads[n] = (back(t) for t in _adamw(view(ws[n]), view(grads[n]), view(ms[n]), view(vs[n]),
                                                            f"adamw_{n}"))
        else:
            d, nm_, nv_ = _adamw(ws[n], grads[n], ms[n], vs[n], f"adamw_{n}")
        updates[n] = (d, nm_, nv_)
    return (loss, grad_x[None], *[grads[n] for n in names], *[updates[n][0] for n in names],
            *[updates[n][1] for n in names], *[updates[n][2] for n in names])


def _behind(t, token):
    return t if token is None else t + token[0:1, 0:1]


def _local_step(xs, target, norm_w, final_norm_w, weights_of, rest_of, on_grad, vecs, dnvs, lru_wa, lru_wx):
    depth = norm_w.shape[0]
    saved = []
    h = xs
    for l in range(depth):
        wp_l, token = weights_of(l, h)
        hn, hnt = _rms_fwd(h, _behind(norm_w[l][None], token), f"rms_fwd{l}")
        proj = _matmul(hn, wp_l, tm=2048, tn=896, tk=D_MODEL, name=f"in_proj{l}")
        wo_l, lcw_l, dcw_l = rest_of(l, proj)
        y, hl = _lru_fwd(proj, lcw_l, vecs[l], lru_wa[l], lru_wx[l], f"lru_fwd{l}")
        y, o, states, tinvs, cpres = _dn_fwd(proj, dcw_l, dnvs[l], y, f"dn_fwd{l}")
        out = _matmul(y, wo_l, tm=1024, tn=D_MODEL, tk=D_MIX, res=h, name=f"out_proj{l}")
        saved.append((h, hnt, proj, hl, o, states, tinvs, cpres, y, wp_l, wo_l, lcw_l, dcw_l))
        h = out

    dh, d_fnw, loss_part = _final_loss(h, final_norm_w[None], target, "final_loss")

    g_nw, g_vec, g_wa, g_wx, g_lcw_, g_dcw_, g_dsc = ([None] * depth for _ in range(7))
    for l in reversed(range(depth)):
        hin, hnt, proj, hl, o, states, tinvs, cpres, y, wp_l, wo_l, lcw_l, dcw_l = saved[l]
        dy = _matmul(dh, wo_l, tb=True, tm=2048, tn=1024, tk=D_MODEL, name=f"d_y{l}")
        g_wout = _matmul(y, dh, ta=True, tm=1024, tn=D_MODEL, tk=512, out_dtype=bf16, name=f"d_wout{l}")
        token = on_grad("wout", l, g_wout)
        dlx, dlz, g_lcw_[l], g_vec[l], g_wa[l], g_wx[l] = _lru_bwd(dy, proj, hl, lcw_l, _behind(vecs[l], token),
                                                                 lru_wa[l], lru_wx[l], f"lru_bwd{l}")
        dq, dk, dv, dz, dba, dcw8, g_dsc[l] = _dn_bwd(dy, proj, o, states, tinvs, cpres, dcw_l, dnvs[l], f"dn_bwd{l}")
        g_dcw_[l] = dcw8.reshape(HEADS, 4, 3, HD).transpose(1, 2, 0, 3).reshape(4, 3 * D_MODEL)
        dxs = [dlx, dlz, dq, dk, dv, dz]
        g_win = _d_win(hnt, dxs, dba, f"d_win{l}")
        token = on_grad("win", l, g_win)
        dh, g_nw[l] = _d_hn(dxs, _behind(dba, token), wp_l, hin, norm_w[l][None], dh, f"d_hn{l}")
    return loss_part, dh, d_fnw, g_nw, g_vec, g_wa, g_wx, g_lcw_, g_dcw_, g_dsc


def _reduce_grads(chip, after, in_flight, d_fnw, g_nw, g_vec, g_wa, g_wx, g_lcw_, g_dcw_, g_dsc):
    depth = len(g_nw)
    both = lambda f: jnp.concatenate([f(l) for l in range(depth)])
    small = [
        both(lambda l: _rows128(g_nw[l])),
        both(lambda l: _rows128(g_vec[l][0])), both(lambda l: _rows128(g_vec[l][1])),
        both(lambda l: _rows128(g_vec[l][2])), both(lambda l: _rows128(g_vec[l][3])),
        both(lambda l: _rows128(g_vec[l][4])),
        both(lambda l: _rows128(g_wa[l])), both(lambda l: _rows128(g_wx[l])),
        both(lambda l: _rows128(g_dsc[l][0:1])), both(lambda l: _rows128(g_dsc[l][1:2])),
        both(lambda l: _rows128(g_dsc[l][2:3])),
        _rows128(d_fnw),
        both(lambda l: _rows128(g_lcw_[l])), both(lambda l: _rows128(g_dcw_[l])),
    ]
    counts = [t.shape[0] for t in small]
    total = sum(counts)
    per = -(-total // (8 * N_DEV)) * 8
    small = jnp.concatenate(small + [jnp.zeros((per * N_DEV - total, HD), f32)]).reshape(N_DEV, per, HD)

    small_started = _scatter_start([small], ["small"], "scatter_small_start")
    reduced = {}
    for key in sorted(in_flight, key=lambda k: -k[1]):
        if key == ("win", 0):
            continue
        started, srcs, kinds = in_flight[key]
        got = _scatter_finish(started, srcs, kinds, after, f"scatter_{key[0]}{key[1]}_wait")
        for kind, part in zip(kinds, got):
            reduced[(kind, key[1])] = _sum_parts(part, f"sum_{kind}{key[1]}")
        after = reduced[(kinds[-1], key[1])]
    got = _scatter_finish(small_started, [small], ["small"], after, "scatter_small_wait")
    early = [reduced[("win", l)] for l in range(1, depth)] + [reduced[("wout", l)] for l in range(depth)]
    shared = _share_reduced(early, _sum_parts(got[0], "sum_small"), "share_grads_early")
    win_later, wout_all = shared[:depth - 1], shared[depth - 1:2 * depth - 1]
    a_small = shared[-1].reshape(N_DEV * per, HD)
    grad_w_out = jnp.stack([wout_all[l].reshape(D_MIX // 4, D_MODEL) for l in range(depth)])

    def finish_w_in(behind):
        started, srcs, kinds = in_flight[("win", 0)]
        part = _scatter_finish(started, srcs, kinds, behind, "scatter_win0_wait")[0]
        win0 = _share_reduced([_sum_parts(part, "sum_win0")], None, "share_grads_win0")[0]
        return jnp.stack([lax.dynamic_slice_in_dim(t.reshape(D_MODEL, WIN_W), 4 * chip, SHARD_IN, 1)
                          for t in [win0] + list(win_later)])

    offs = [0]
    for c in counts:
        offs.append(offs[-1] + c)

    def piece(k, rows_per_layer=None):
        t = a_small[offs[k]:offs[k + 1]]
        if rows_per_layer is None:
            return t
        return t.reshape(depth, -1, HD)[:, :rows_per_layer]

    vec = lambda k: piece(k).reshape(depth, D_MODEL)
    return {
        "norm_w": vec(0), "lru_conv_b": vec(1), "lru_ba": vec(2), "lru_bx": vec(3), "lru_lambda": vec(4),
        "lru_norm_w": vec(5),
        "lru_wa": piece(6).reshape(depth, HEADS, HD, HD), "lru_wx": piece(7).reshape(depth, HEADS, HD, HD),
        "dn_A_log": piece(8, 1)[:, 0, :HEADS], "dn_dt_bias": piece(9, 1)[:, 0, :HEADS], "dn_norm_w": piece(10, 1)[:, 0],
        "final_norm_w": piece(11).reshape(D_MODEL),
        "lru_conv_w": lax.dynamic_slice_in_dim(piece(12).reshape(depth, 4, D_MODEL), chip * (D_MODEL // 4), D_MODEL // 4, 2),
        "dn_conv_w": lax.dynamic_slice_in_dim(piece(13).reshape(depth, 4, 3 * D_MODEL), chip * (3 * D_MODEL // 4),
                                              3 * D_MODEL // 4, 2),
        "w_in": finish_w_in, "w_out": grad_w_out,
    }
```

```python
import jax
import jax.numpy as jnp
from jax import lax
from jax.experimental import pallas as pl
from jax.experimental.pallas import tpu as pltpu

f32 = jnp.float32
bf16 = jnp.bfloat16
MESH = pl.DeviceIdType.MESH

D_MODEL = 1024
HEADS = 8
HD = 128
CHUNK = 64
LRU_T = 128
SEQ_BLOCK = 512
LRU_HP = 8
DN_SEQ_BLOCK = 256
DN_HP = 8
LRU_C = 8.0
D_IN = 6160
D_INP = 6272
D_MIX = 2048
N_GROUPS = 6
BLK_BA = 48
SHARD_IN = D_IN // 4
WIN_W = 1664
WIN_STRIDE = 1536
EPS = 1e-6
QSCALE = HD ** -0.5
N_DEV = 8
VMEM_LIMIT = 56 * 1024 * 1024

ADAM_LR, ADAM_B1, ADAM_B2, ADAM_EPS, ADAM_WD, ADAM_STEP = 0.001, 0.9, 0.999, 1e-08, 0.01, 10


def _sig(x):
    return 1.0 / (1.0 + jnp.exp(-x))


def _log1p(u):
    w = 1.0 + u
    d = w - 1.0
    return jnp.where(d == 0.0, u, jnp.log(w) * (u / jnp.where(d == 0.0, 1.0, d)))


def _softplus(x):
    return jnp.maximum(x, 0.0) + _log1p(jnp.exp(-jnp.abs(x)))


def _expm1(x):
    t = x * (1.0 + x * (0.5 + x * (1.0 / 6 + x * (1.0 / 24 + x * (1.0 / 120 + x * (1.0 / 720))))))
    return jnp.where(jnp.abs(x) < 0.2, t, jnp.exp(x) - 1.0)


def _dot(a, b, prec=None):
    return jnp.dot(a, b, precision=prec, preferred_element_type=f32)


def _dot_nt(a, b, prec=None):
    return lax.dot_general(a, b, (((1,), (1,)), ((), ())), precision=prec, preferred_element_type=f32)


def _dot_tn(a, b, prec=None):
    return lax.dot_general(a, b, (((0,), (0,)), ((), ())), precision=prec, preferred_element_type=f32)


def _b(x):
    return x.astype(bf16)


def _sum0(x):
    return jnp.sum(x, axis=0, keepdims=True)


def _sum1(x):
    return jnp.sum(x, axis=1, keepdims=True)


def _rowmean(x):
    return jnp.mean(x, axis=-1, keepdims=True)


def _dsilu(z, sg):
    return sg * (1.0 + z * (1.0 - sg))


def _conv_taps(x, halo):
    xx = jnp.concatenate([halo, x], axis=0)
    return [pltpu.roll(xx, 3, axis=0)[8:], pltpu.roll(xx, 2, axis=0)[8:], pltpu.roll(xx, 1, axis=0)[8:], x]


def _conv(xs, cw):
    return cw[0:1] * xs[0] + cw[1:2] * xs[1] + cw[2:3] * xs[2] + cw[3:4] * xs[3]


def _rows_before(ref, halo_ref, lanes, t0, c, sblk):
    tprev = pl.multiple_of(jnp.maximum(t0 - 8, 0), 8)
    return jnp.where(c > 0, ref[pl.ds(tprev, 8), lanes], jnp.where(sblk > 0, halo_ref[:, lanes], 0.0))


def _conv_back(dxc, halo_after, cw, x=None):
    rows = dxc.shape[0]
    dd = jnp.concatenate([dxc, halo_after], axis=0)
    out = cw[3:4] * dxc
    dcw = [None, None, None, None if x is None else _sum0(x * dxc)]
    for s in (1, 2, 3):
        shifted = pltpu.roll(dd, rows + 8 - s, axis=0)[:rows]
        out = out + cw[3 - s:4 - s] * shifted
        if x is not None:
            dcw[3 - s] = _sum0(x * shifted)
    return out if x is None else (out, jnp.concatenate(dcw, axis=0))


def _params(sem=None):
    return pltpu.CompilerParams(dimension_semantics=sem, vmem_limit_bytes=VMEM_LIMIT)


def _seq_specs(sb, width, rowblk, colblk):
    main = pl.BlockSpec((sb, width), lambda a, b: (rowblk(a, b), colblk(a, b)))
    halo = pl.BlockSpec((8, width), lambda a, b: (jnp.maximum(rowblk(a, b) * (sb // 8) - 1, 0), colblk(a, b)))
    return main, halo


def _matmul(a, b, *, ta=False, tb=False, tm, tn, tk, res=None, out_dtype=f32, name):
    if ta:
        kdim, m = a.shape
    else:
        m, kdim = a.shape
    n = b.shape[0] if tb else b.shape[1]
    tm, tn, tk = min(tm, m), min(tn, n), min(tk, kdim)
    assert m % tm == 0 and n % tn == 0 and kdim % tk == 0, (name, m, n, kdim, tm, tn, tk)
    nk = kdim // tk
    dims = (((0 if ta else 1,), (1 if tb else 0,)), ((), ()))
    has_res = res is not None

    def body(*refs):
        a_ref, b_ref = refs[:2]
        r_ref = refs[2] if has_res else None
        o_ref = refs[3] if has_res else refs[2]
        acc_ref = refs[-1] if nk > 1 else None
        part = lax.dot_general(_b(a_ref[...]), _b(b_ref[...]), dims, preferred_element_type=f32)

        def finish(total):
            if has_res:
                total = total + r_ref[...]
            o_ref[...] = total.astype(out_dtype)

        if nk == 1:
            finish(part)
        else:
            k = pl.program_id(2)

            @pl.when(k == 0)
            def _():
                acc_ref[...] = part

            @pl.when(k > 0)
            def _():
                acc_ref[...] += part

            @pl.when(k == nk - 1)
            def _():
                finish(acc_ref[...])

    a_spec = pl.BlockSpec((tk, tm), lambda i, j, k: (k, i)) if ta else pl.BlockSpec((tm, tk), lambda i, j, k: (i, k))
    b_spec = pl.BlockSpec((tn, tk), lambda i, j, k: (j, k)) if tb else pl.BlockSpec((tk, tn), lambda i, j, k: (k, j))
    o_spec = pl.BlockSpec((tm, tn), lambda i, j, k: (i, j))
    in_specs, args = [a_spec, b_spec], [a, b]
    if has_res:
        in_specs.append(o_spec)
        args.append(res)
    return pl.pallas_call(
        body, name=name, grid=(m // tm, n // tn, nk), in_specs=in_specs, out_specs=o_spec,
        out_shape=jax.ShapeDtypeStruct((m, n), out_dtype),
        scratch_shapes=[pltpu.VMEM((tm, tn), f32)] if nk > 1 else [],
        compiler_params=_params(("parallel", "parallel", "arbitrary")),
    )(*args)


def _d_hn(dxs, dba, wp, x, w, dres, name):
    s = dxs[0].shape[0]
    tm = min(256, s)

    def body(*refs):
        dx_refs = refs[:N_GROUPS]
        dba_ref, w_ref, x_ref, nw_ref, dres_ref, o_ref, dw_ref = refs[N_GROUPS:]
        dhn = _dot_nt(_b(dba_ref[...]), w_ref[:, BLK_BA * HD:])
        for g in range(N_GROUPS):
            dhn = dhn + _dot_nt(dx_refs[g][...], w_ref[:, g * D_MODEL:(g + 1) * D_MODEL])
        xv = x_ref[...]
        r = lax.rsqrt(_rowmean(xv * xv) + EPS)
        xn = xv * r
        dxn = dhn * nw_ref[...]
        o_ref[...] = dres_ref[...] + r * (dxn - xn * _rowmean(dxn * xn))
        part = _sum0(dhn * xn)

        @pl.when(pl.program_id(0) == 0)
        def _():
            dw_ref[...] = part

        @pl.when(pl.program_id(0) > 0)
        def _():
            dw_ref[...] += part

    row = lambda wd: pl.BlockSpec((tm, wd), lambda i: (i, 0))
    vec = pl.BlockSpec((1, D_MODEL), lambda i: (0, 0))
    return pl.pallas_call(
        body, name=name, grid=(s // tm,),
        in_specs=[row(D_MODEL)] * N_GROUPS + [row(HD), pl.BlockSpec((D_MODEL, D_INP), lambda i: (0, 0)),
                                              row(D_MODEL), vec, row(D_MODEL)],
        out_specs=[row(D_MODEL), vec],
        out_shape=[jax.ShapeDtypeStruct((s, D_MODEL), f32), jax.ShapeDtypeStruct((1, D_MODEL), f32)],
        compiler_params=_params(("arbitrary",)),
    )(*dxs, dba, wp, x, w, dres)


def _d_win(hnt, dxs, dba, name):
    s = hnt.shape[1]
    tn = 256
    per = D_MODEL // tn

    def body(*refs):
        hnt_ref = refs[0]
        dx_refs = refs[1:1 + N_GROUPS]
        o_ref = refs[1 + N_GROUPS]
        j = pl.program_id(0)
        for g in range(N_GROUPS):
            @pl.when(j // per == g)
            def _(g=g):
                o_ref[...] = _dot(hnt_ref[...], _b(dx_refs[g][...])).astype(bf16)

    def dx_spec(g):
        on = lambda j: (j // per == g).astype(jnp.int32)
        return pl.BlockSpec((s, tn), lambda j: (0, (j % per) * on(j)))

    main = pl.pallas_call(
        body, name=name, grid=(N_GROUPS * per,),
        in_specs=[pl.BlockSpec((D_MODEL, s), lambda j: (0, 0))] + [dx_spec(g) for g in range(N_GROUPS)],
        out_specs=pl.BlockSpec((D_MODEL, tn), lambda j: (0, j)),
        out_shape=jax.ShapeDtypeStruct((D_MODEL, D_INP), bf16),
        compiler_params=_params(("parallel",)),
    )(hnt, *dxs)

    def tail(hnt_ref, dba_ref, full_ref, o_ref):
        del full_ref
        o_ref[...] = _dot(hnt_ref[...], _b(dba_ref[...])).astype(bf16)

    return pl.pallas_call(
        tail, name=name + "_ba", grid=(1,),
        in_specs=[pl.BlockSpec((D_MODEL, s), lambda k: (0, 0)), pl.BlockSpec((s, HD), lambda k: (0, 0)),
                  pl.BlockSpec(memory_space=pl.ANY)],
        out_specs=pl.BlockSpec((D_MODEL, HD), lambda k: (0, BLK_BA)),
        out_shape=jax.ShapeDtypeStruct((D_MODEL, D_INP), bf16),
        input_output_aliases={2: 0},
        compiler_params=_params(("arbitrary",)),
    )(hnt, dba, main)


def _rms_fwd(x, w, name):
    s = x.shape[0]
    tr = min(512, s)

    def body(x_ref, w_ref, h_ref, ht_ref):
        xv = x_ref[...]
        r = lax.rsqrt(_rowmean(xv * xv) + EPS)
        h = (xv * r * w_ref[...]).astype(bf16)
        h_ref[...] = h
        ht_ref[...] = h.T

    return pl.pallas_call(
        body, name=name, grid=(s // tr,),
        in_specs=[pl.BlockSpec((tr, D_MODEL), lambda i: (i, 0)), pl.BlockSpec((1, D_MODEL), lambda i: (0, 0))],
        out_specs=[pl.BlockSpec((tr, D_MODEL), lambda i: (i, 0)), pl.BlockSpec((D_MODEL, tr), lambda i: (0, i))],
        out_shape=[jax.ShapeDtypeStruct((s, D_MODEL), bf16), jax.ShapeDtypeStruct((D_MODEL, s), bf16)],
        compiler_params=_params(("parallel",)),
    )(x, w)


def _final_loss(x, w, target, name):
    s = x.shape[0]
    tr = min(512, s)

    def body(x_ref, w_ref, t_ref, dx_ref, dw_ref, loss_ref):
        xv = x_ref[...]
        wv = w_ref[...]
        r = lax.rsqrt(_rowmean(xv * xv) + EPS)
        xn = xv * r
        e = xn * wv - t_ref[...]
        lb = jnp.broadcast_to(0.5 * _sum0(_rowmean(e * e)), (1, HD))
        dy = e * (1.0 / D_MODEL)
        dxn = dy * wv
        dx_ref[...] = r * (dxn - xn * _rowmean(dxn * xn))
        part = _sum0(dy * xn)

        @pl.when(pl.program_id(0) == 0)
        def _():
            dw_ref[...] = part
            loss_ref[...] = lb

        @pl.when(pl.program_id(0) > 0)
        def _():
            dw_ref[...] += part
            loss_ref[...] += lb

    row = pl.BlockSpec((tr, D_MODEL), lambda i: (i, 0))
    vec = pl.BlockSpec((1, D_MODEL), lambda i: (0, 0))
    lsp = pl.BlockSpec((1, HD), lambda i: (0, 0))
    return pl.pallas_call(
        body, name=name, grid=(s // tr,), in_specs=[row, vec, row], out_specs=[row, vec, lsp],
        out_shape=[jax.ShapeDtypeStruct((s, D_MODEL), f32), jax.ShapeDtypeStruct((1, D_MODEL), f32),
                   jax.ShapeDtypeStruct((1, HD), f32)],
        compiler_params=_params(("arbitrary",)),
    )(x, w, target)


def _lru_gates(xc, vec, sp, wa, wx):
    xcb = _b(xc)
    r = _sig(_dot(xcb, wa) + vec[1:2])
    i = _sig(_dot(xcb, wx) + vec[2:3])
    la = (-LRU_C) * r * sp
    a = jnp.exp(la)
    mult = jnp.sqrt(-_expm1(2.0 * la))
    return r, i, a, mult, sp, xcb


def _lru_fwd(proj, cw, vec, wa, wx, name):
    s = proj.shape[0]
    sb = min(SEQ_BLOCK, s)
    nsb = s // sb
    t = min(LRU_T, sb)
    nc = sb // t
    hp = LRU_HP
    wide = hp * HD
    lanes = [slice(j * HD, (j + 1) * HD) for j in range(hp)]

    def body(x_ref, xh_ref, z_ref, cw_ref, vec_ref, wa_ref, wx_ref, y_ref, hl_ref, hc_ref):
        sblk = pl.program_id(1)
        consts = [(cw_ref[:, hl], vec_ref[:, hl], _softplus(-vec_ref[3:4, hl]), _b(wa_ref[j]), _b(wx_ref[j]))
                  for j, hl in enumerate(lanes)]
        row = lax.broadcasted_iota(jnp.int32, (t, HD), 0)

        @pl.when(sblk == 0)
        def _():
            hc_ref[...] = jnp.zeros((8, wide), f32)

        def one_head(j, x, halo, z, hcarry):
            cwv, vec_v, sp_v, wa_v, wx_v = consts[j]
            xs = _conv_taps(x, halo)
            xc = vec_v[0:1] + _conv(xs, cwv)
            r, i, a, mult, _, _ = _lru_gates(xc, vec_v, sp_v, wa_v, wx_v)
            yield
            bb = mult * (i * xc)
            d = 1
            while d < t:
                m = row >= d
                bb = jnp.where(m, a * pltpu.roll(bb, d, axis=0) + bb, bb)
                a = jnp.where(m, a * pltpu.roll(a, d, axis=0), a)
                d *= 2
                yield
            hl = bb + a * hcarry
            nrm = lax.rsqrt(_rowmean(hl * hl) + EPS)
            return hl, (hl * nrm * vec_v[4:5] * (z * _sig(z))).astype(bf16)

        def chunk(c, carries):
            t0 = pl.multiple_of(c * t, t)
            rows = pl.ds(t0, t)
            ins = [(x_ref[rows, hl], _rows_before(x_ref, xh_ref, hl, t0, c, sblk), z_ref[rows, hl]) for hl in lanes]
            outs = _round_robin([one_head(j, *ins[j], carries[j]) for j in range(hp)])
            for j, hl in enumerate(lanes):
                hl_ref[rows, hl] = outs[j][0]
                y_ref[rows, hl] = outs[j][1]
            return tuple(out[0][t - 1:t, :] for out in outs)

        final = lax.fori_loop(0, nc, chunk, tuple(hc_ref[0:1, hl] for hl in lanes))
        hc_ref[...] = jnp.concatenate([jnp.broadcast_to(f, (8, HD)) for f in final], axis=1)

    xsp, xhalo = _seq_specs(sb, wide, lambda g, i: i, lambda g, i: g)
    zsp, _ = _seq_specs(sb, wide, lambda g, i: i, lambda g, i: HEADS // hp + g)
    gcol = lambda g, i: (0, g)
    wsp = pl.BlockSpec((hp, HD, HD), lambda g, i: (g, 0, 0))
    osp = pl.BlockSpec((sb, wide), lambda g, i: (i, g))
    return pl.pallas_call(
        body, name=name, grid=(HEADS // hp, nsb),
        in_specs=[xsp, xhalo, zsp, pl.BlockSpec((4, wide), gcol), pl.BlockSpec((8, wide), gcol), wsp, wsp],
        out_specs=[osp, osp],
        out_shape=[jax.ShapeDtypeStruct((s, D_MIX), bf16), jax.ShapeDtypeStruct((s, D_MODEL), f32)],
        scratch_shapes=[pltpu.VMEM((8, wide), f32)],
        compiler_params=_params(("parallel", "arbitrary")),
    )(proj, proj, proj, cw, vec, wa, wx)


def _lru_bwd(dy, proj, hl, cw, vec, wa, wx, name):
    s = proj.shape[0]
    sb = min(SEQ_BLOCK, s)
    nsb = s // sb
    t = min(LRU_T, sb)
    nc = sb // t
    hp = LRU_HP
    wide = hp * HD
    lanes = [slice(j * HD, (j + 1) * HD) for j in range(hp)]
    n_acc = 10

    def body(dy_ref, x_ref, xh_ref, z_ref, hl_ref, hlh_ref, cw_ref, vec_ref, wa_ref, wx_ref,
             dx_ref, dz_ref, dcw_ref, dvec_ref, dwa_ref, dwx_ref, acc_ref, dxh_ref):
        step = pl.program_id(1)
        sblk = nsb - 1 - step
        consts = [(cw_ref[:, hl], vec_ref[:, hl], _softplus(-vec_ref[3:4, hl]), _b(wa_ref[j]), _b(wx_ref[j]))
                  for j, hl in enumerate(lanes)]
        row = lax.broadcasted_iota(jnp.int32, (t, HD), 0)

        @pl.when(step == 0)
        def _():
            acc_ref[...] = jnp.zeros((16, wide), f32)
            dxh_ref[...] = jnp.zeros((8, wide), f32)
            dwa_ref[...] = jnp.zeros((hp, HD, HD), f32)
            dwx_ref[...] = jnp.zeros((hp, HD, HD), f32)

        def one_head(j, x, halo, z, hv, hhalo, dyv, carry):
            dcw0, dcw1, dcw2, dcw3, dcb, dba, dbx, dsp, dlnw, mu, dxc_halo = carry
            cwv, vec_v, sp_v, wa_v, wx_v = consts[j]
            lnw = vec_v[4:5]
            xs = _conv_taps(x, halo)
            xc = vec_v[0:1] + _conv(xs, cwv)
            r, i, a, mult, sp, xcb = _lru_gates(xc, vec_v, sp_v, wa_v, wx_v)
            yield
            hprev = pltpu.roll(jnp.concatenate([hhalo, hv], axis=0), 1, axis=0)[8:]
            sgz = _sig(z)
            sz = z * sgz
            nrm = lax.rsqrt(_rowmean(hv * hv) + EPS)
            hn = hv * nrm
            dlnw = dlnw + _sum0(dyv * hn * sz)
            dzv = _b(dyv * hn * lnw * _dsilu(z, sgz))
            dhn = dyv * lnw * sz
            lam = nrm * (dhn - hn * _rowmean(dhn * hn))
            cf = jnp.where(row == t - 1, 1.0, pltpu.roll(a, t - 1, axis=0))
            d = 1
            while d < t:
                m = row < t - d
                lam = jnp.where(m, lam + cf * pltpu.roll(lam, t - d, axis=0), lam)
                cf = jnp.where(m, cf * pltpu.roll(cf, t - d, axis=0), cf)
                d *= 2
                yield
            lam = lam + cf * mu
            mu = a[0:1] * lam[0:1]
            da = lam * hprev
            dmult = lam * (i * xc)
            di = lam * mult * xc
            dxc = lam * mult * i
            dla = da * a - dmult * (a * a) / mult
            dr = dla * (-LRU_C * sp)
            dsp = dsp + _sum0(dla * (-LRU_C * r))
            dra = dr * r * (1.0 - r)
            dia = di * i * (1.0 - i)
            dba = dba + _sum0(dra)
            dbx = dbx + _sum0(dia)
            drab, diab = _b(dra), _b(dia)
            dwa = _dot_tn(xcb, drab)
            dwx = _dot_tn(xcb, diab)
            dxc = dxc + _dot_nt(drab, wa_v) + _dot_nt(diab, wx_v)
            yield
            dcb = dcb + _sum0(dxc)
            dcw0 = dcw0 + _sum0(dxc * xs[0])
            dcw1 = dcw1 + _sum0(dxc * xs[1])
            dcw2 = dcw2 + _sum0(dxc * xs[2])
            dcw3 = dcw3 + _sum0(dxc * xs[3])
            dxv = _b(_conv_back(dxc, dxc_halo, cwv))
            return dxv, dzv, dwa, dwx, (dcw0, dcw1, dcw2, dcw3, dcb, dba, dbx, dsp, dlnw, mu, dxc[0:8])

        def chunk(ci, carries):
            c = nc - 1 - ci
            t0 = pl.multiple_of(c * t, t)
            rows = pl.ds(t0, t)
            ins = [(x_ref[rows, hl], _rows_before(x_ref, xh_ref, hl, t0, c, sblk), z_ref[rows, hl], hl_ref[rows, hl],
                    _rows_before(hl_ref, hlh_ref, hl, t0, c, sblk), dy_ref[rows, hl]) for hl in lanes]
            outs = _round_robin([one_head(j, *ins[j], carries[j]) for j in range(hp)])
            for j, hl in enumerate(lanes):
                dx_ref[rows, hl] = outs[j][0]
                dz_ref[rows, hl] = outs[j][1]
                dwa_ref[j] += outs[j][2]
                dwx_ref[j] += outs[j][3]
            return tuple(out[4] for out in outs)

        acc = acc_ref[...]
        dxh = dxh_ref[...]
        init = tuple(tuple(acc[k:k + 1, hl] for k in range(n_acc)) + (dxh[:, hl],) for hl in lanes)
        final = lax.fori_loop(0, nc, chunk, init)
        rows_out = [jnp.concatenate([final[j][k] for j in range(hp)], axis=1) for k in range(n_acc)]
        zero = jnp.zeros((1, wide), f32)
        acc_ref[...] = jnp.concatenate(rows_out + [zero] * (16 - n_acc), axis=0)
        dxh_ref[...] = jnp.concatenate([final[j][n_acc] for j in range(hp)], axis=1)

        @pl.when(step == nsb - 1)
        def _():
            dcw_ref[...] = jnp.concatenate(rows_out[0:4], axis=0)
            dlam = -rows_out[7] * _sig(-vec_ref[3:4, :])
            dvec_ref[...] = jnp.concatenate([rows_out[4], rows_out[5], rows_out[6], dlam, rows_out[8], zero, zero, zero],
                                            axis=0)

    rblk = lambda g, i: nsb - 1 - i
    xsp, xhalo = _seq_specs(sb, wide, rblk, lambda g, i: g)
    zsp, _ = _seq_specs(sb, wide, rblk, lambda g, i: HEADS // hp + g)
    gcol = lambda g, i: (0, g)
    wsp = pl.BlockSpec((hp, HD, HD), lambda g, i: (g, 0, 0))
    return pl.pallas_call(
        body, name=name, grid=(HEADS // hp, nsb),
        in_specs=[xsp, xsp, xhalo, zsp, xsp, xhalo, pl.BlockSpec((4, wide), gcol), pl.BlockSpec((8, wide), gcol), wsp, wsp],
        out_specs=[xsp, xsp, pl.BlockSpec((4, wide), gcol), pl.BlockSpec((8, wide), gcol), wsp, wsp],
        out_shape=[jax.ShapeDtypeStruct((s, D_MODEL), bf16), jax.ShapeDtypeStruct((s, D_MODEL), bf16),
                   jax.ShapeDtypeStruct((4, D_MODEL), f32), jax.ShapeDtypeStruct((8, D_MODEL), f32),
                   jax.ShapeDtypeStruct((HEADS, HD, HD), f32), jax.ShapeDtypeStruct((HEADS, HD, HD), f32)],
        scratch_shapes=[pltpu.VMEM((16, wide), f32), pltpu.VMEM((8, wide), f32)],
        compiler_params=_params(("parallel", "arbitrary")),
    )(dy, proj, proj, proj, hl, hl, cw, vec, wa, wx)


def _lane_pick(x, lane, idx):
    return _sum1(jnp.where(lane == idx, x, 0.0))


def _round_robin(gens):
    results = [None] * len(gens)
    live = list(range(len(gens)))
    while live:
        for i in list(live):
            try:
                next(gens[i])
            except StopIteration as done:
                results[i] = done.value
                live.remove(i)
    return results


def _sdot(a, b):
    return _dot(_b(a), _b(b))


def _sdot_tn(a, b):
    return _dot_tn(_b(a), _b(b))


def _sdot_nt(a, b):
    return _dot_nt(_b(a), _b(b))


def _tri_dot(tri, x):
    trib = tri.astype(bf16)
    x1 = x.astype(bf16)
    r1 = x - x1.astype(f32)
    x2 = r1.astype(bf16)
    x3 = (r1 - x2.astype(f32)).astype(bf16)
    return _dot(trib, x1) + _dot(trib, x2) + _dot(trib, x3)


def _inv_unit_lower(a):
    n = -a
    p = _sdot(a, a)
    yield
    for k in range(5):
        n = n + p + _sdot(n, p)
        if k < 4:
            p = _sdot(p, p)
        yield
    return n


def _dn_gates(bav, dnv, masks):
    ltri = masks[4]
    ea = jnp.exp(pltpu.roll(dnv[0:1], HEADS, axis=1))
    dt = pltpu.roll(dnv[1:2], HEADS, axis=1)
    u = bav + dt
    g = -ea * _softplus(u)
    return dict(beta=_sig(bav), g=g, gc=_tri_dot(ltri, g), sga=_sig(u), ea=ea)


def _dn_chunk(x3, halo3, gates, cw3, h, masks, cpre=None, tinv=None):
    lane, ri, ci, eye, ltri = masks
    if cpre is None:
        cpre = _conv(_conv_taps(x3, halo3), cw3)
    sg = _sig(cpre)
    act = cpre * sg
    q_raw, k_raw, v = act[:, 0:HD], act[:, HD:2 * HD], act[:, 2 * HD:3 * HD]
    rq = lax.rsqrt(_sum1(q_raw * q_raw) + EPS)
    rk = lax.rsqrt(_sum1(k_raw * k_raw) + EPS)
    qn = q_raw * rq
    k = k_raw * rk
    q = qn * QSCALE
    beta = jnp.broadcast_to(_lane_pick(gates["beta"], lane, h), (CHUNK, HD))
    gc = jnp.broadcast_to(_lane_pick(gates["gc"], lane, HEADS + h), (CHUNK, HD))
    yield
    gc64 = gc[:, 0:CHUNK]
    grow = _sum0(gc64 * eye)
    dm = jnp.exp(jnp.where(ri >= ci, gc64 - grow, -1e30))
    ds_ = jnp.where(ri > ci, dm, 0.0)
    eg = jnp.exp(gc)
    glast = gc[CHUNK - 1:CHUNK, :]
    ek = jnp.exp(glast - gc)
    kb = k * beta
    kbf = _b(k)
    amat = _dot_nt(_b(kb), kbf) * ds_
    pmat = _dot_nt(_b(q), kbf) * dm
    yield
    if tinv is None:
        tinv = yield from _inv_unit_lower(amat)
    return dict(cpre=cpre, sg=sg, rq=rq, rk=rk, qn=qn, q=q, k=k, v=v, kbf=kbf, beta=beta, gc=gc, dm=dm, ds=ds_, eg=eg,
                glast=glast, ek=ek, kb=kb, amat=amat, tinv=tinv, pmat=pmat)


def _dn_masks():
    lane = lax.broadcasted_iota(jnp.int32, (CHUNK, HD), 1)
    ri = lax.broadcasted_iota(jnp.int32, (CHUNK, CHUNK), 0)
    ci = lax.broadcasted_iota(jnp.int32, (CHUNK, CHUNK), 1)
    eye = (ri == ci).astype(f32)
    ltri = (ri >= ci).astype(f32)
    return lane, ri, ci, eye, ltri


def _dn_load_qkv(q_ref, qh_ref, k_ref, kh_ref, v_ref, vh_ref, hl, rows, t0, c, sblk):
    x3 = jnp.concatenate([q_ref[rows, hl], k_ref[rows, hl], v_ref[rows, hl]], axis=1)
    halo3 = jnp.concatenate([_rows_before(q_ref, qh_ref, hl, t0, c, sblk), _rows_before(k_ref, kh_ref, hl, t0, c, sblk),
                             _rows_before(v_ref, vh_ref, hl, t0, c, sblk)], axis=1)
    return x3, halo3


def _dn_cw3(cwq_ref, cwk_ref, cwv_ref, j):
    return jnp.concatenate([r[:, j * HD:(j + 1) * HD] for r in (cwq_ref, cwk_ref, cwv_ref)], axis=1)


def _dn_fwd(proj, cw, dnv, y, name):
    s = proj.shape[0]
    sb = min(DN_SEQ_BLOCK, s)
    nsb = s // sb
    nc = sb // CHUNK
    hp = DN_HP

    def body(q_ref, qh_ref, k_ref, kh_ref, v_ref, vh_ref, z_ref, ba_ref, cwq_ref, cwk_ref, cwv_ref, dnv_ref, yin_ref,
             y_ref, o_ref, st_ref, ti_ref, cp_ref, s_ref):
        del yin_ref
        grp = pl.program_id(0)
        sblk = pl.program_id(1)
        masks = _dn_masks()
        dnv_v = dnv_ref[...]
        cw3 = [_dn_cw3(cwq_ref, cwk_ref, cwv_ref, j) for j in range(hp)]

        @pl.when(sblk == 0)
        def _():
            s_ref[...] = jnp.zeros((hp, HD, HD), f32)

        def one_head(j, x3, halo3, gates, z, st):
            f = yield from _dn_chunk(x3, halo3, gates, cw3[j], grp * hp + j, masks)
            sbf = _b(st)
            qs = _dot(_b(f["q"] * f["eg"]), sbf)
            rhs = f["beta"] * (f["v"] - _dot(_b(f["k"] * f["eg"]), sbf))
            yield
            vn = rhs + _sdot(f["tinv"], rhs)
            yield
            vnb = _b(vn)
            o = qs + _dot(_b(f["pmat"]), vnb)
            st_new = st * jnp.exp(f["glast"]) + _dot_tn(_b(f["k"] * f["ek"]), vnb)
            yield
            nrm = lax.rsqrt(_rowmean(o * o) + EPS)
            yv = (o * nrm * dnv_v[2:3] * (z * _sig(z))).astype(bf16)
            return o, yv, st_new, f["tinv"], f["cpre"]

        def chunk(c, states):
            t0 = pl.multiple_of(c * CHUNK, CHUNK)
            rows = pl.ds(t0, CHUNK)
            bav = ba_ref[rows, :]
            ins = []
            for j in range(hp):
                hl = slice(j * HD, (j + 1) * HD)
                ins.append(_dn_load_qkv(q_ref, qh_ref, k_ref, kh_ref, v_ref, vh_ref, hl, rows, t0, c, sblk)
                           + (z_ref[rows, hl],))
            gates = _dn_gates(bav, dnv_v, masks)
            outs = _round_robin([one_head(j, ins[j][0], ins[j][1], gates, ins[j][2], states[j]) for j in range(hp)])
            for j in range(hp):
                hl = slice(j * HD, (j + 1) * HD)
                st_ref[j, c] = states[j]
                o_ref[rows, hl] = outs[j][0]
                y_ref[rows, hl] = outs[j][1]
                ti_ref[j, c] = outs[j][3]
                cp_ref[rows, j * 3 * HD:(j + 1) * 3 * HD] = outs[j][4]
            return tuple(out[2] for out in outs)

        final = lax.fori_loop(0, nc, chunk, tuple(s_ref[j] for j in range(hp)))
        for j in range(hp):
            s_ref[j] = final[j]

    wide = hp * HD
    grp_specs = lambda off: _seq_specs(sb, wide, lambda g, i: i, lambda g, i: off // hp + g)
    (qsp, qhalo), (ksp, khalo), (vsp, vhalo) = grp_specs(2 * HEADS), grp_specs(3 * HEADS), grp_specs(4 * HEADS)
    zsp, _ = grp_specs(5 * HEADS)
    basp = pl.BlockSpec((sb, HD), lambda g, i: (i, BLK_BA))
    cws = lambda off: pl.BlockSpec((4, wide), lambda g, i: (0, off // hp + g))
    osp = lambda off: pl.BlockSpec((sb, wide), lambda g, i: (i, off // hp + g))
    return pl.pallas_call(
        body, name=name, grid=(HEADS // hp, nsb),
        in_specs=[qsp, qhalo, ksp, khalo, vsp, vhalo, zsp, basp, cws(0), cws(HEADS), cws(2 * HEADS),
                  pl.BlockSpec((8, HD), lambda g, i: (0, 0)), pl.BlockSpec(memory_space=pl.ANY)],
        out_specs=[osp(HEADS), osp(0), pl.BlockSpec((hp, nc, HD, HD), lambda g, i: (g, i, 0, 0)),
                   pl.BlockSpec((hp, nc, CHUNK, CHUNK), lambda g, i: (g, i, 0, 0)),
                   pl.BlockSpec((sb, 3 * wide), lambda g, i: (i, g))],
        out_shape=[jax.ShapeDtypeStruct((s, D_MIX), bf16), jax.ShapeDtypeStruct((s, D_MODEL), f32),
                   jax.ShapeDtypeStruct((HEADS, s // CHUNK, HD, HD), f32),
                   jax.ShapeDtypeStruct((HEADS, s // CHUNK, CHUNK, CHUNK), f32),
                   jax.ShapeDtypeStruct((s, 3 * D_MODEL), f32)],
        scratch_shapes=[pltpu.VMEM((hp, HD, HD), f32)],
        input_output_aliases={12: 0},
        compiler_params=_params(("parallel", "arbitrary")),
    )(proj, proj, proj, proj, proj, proj, proj, proj, cw, cw, cw, dnv, y)


def _dn_bwd(dy, proj, o, states, tinvs, cpres, cw, dnv, name):
    s = proj.shape[0]
    sb = min(DN_SEQ_BLOCK, s)
    nsb = s // sb
    nc = sb // CHUNK
    hp = DN_HP
    ngrp = HEADS // hp

    def body(dy_ref, q_ref, k_ref, v_ref, z_ref, ba_ref, o_ref, st_ref, ti_ref, cp_ref,
             cwq_ref, cwk_ref, cwv_ref, dnv_ref,
             dq_ref, dk_ref, dv_ref, dz_ref, dba_ref, dcw_ref, dsc_ref,
             ds_ref, dch_ref, cwacc_ref, nacc_ref):
        step = pl.program_id(0)
        grp = pl.program_id(1)
        sblk = nsb - 1 - step
        h0 = grp * hp
        masks = _dn_masks()
        lane, ri, ci, eye, ltri = masks
        utri = (ri <= ci).astype(f32)
        cw3s = [_dn_cw3(cwq_ref, cwk_ref, cwv_ref, j) for j in range(hp)]
        dnv_v = dnv_ref[...]
        dnw = dnv_v[2:3]
        row64 = lax.broadcasted_iota(jnp.int32, (CHUNK, HD), 0)

        @pl.when(step == 0)
        def _():
            for j in range(hp):
                ds_ref[h0 + j] = jnp.zeros((HD, HD), f32)
                dch_ref[h0 + j] = jnp.zeros((8, 3 * HD), f32)
                cwacc_ref[h0 + j] = jnp.zeros((8, 3 * HD), f32)

        @pl.when((step == 0) & (grp == 0))
        def _():
            nacc_ref[...] = jnp.zeros((8, HD), f32)

        def one_head(j, x3, gates, z, st, ti, cp, ov, dyv, carry):
            dst, dch, cwacc = carry
            cw3 = cw3s[j]
            f = yield from _dn_chunk(None, None, gates, cw3, h0 + j, masks, cp, ti)
            q, k, v, beta, eg, ek, kbf = f["q"], f["k"], f["v"], f["beta"], f["eg"], f["ek"], f["kbf"]
            sbf = _b(st)
            dstb = _b(dst)
            qe = q * eg
            keg = k * eg
            kd = k * ek
            kegb, qeb, kdb = _b(keg), _b(qe), _b(kd)
            e = v - _dot(kegb, sbf)
            yield
            rhs = beta * e
            vn = rhs + _sdot(f["tinv"], rhs)
            yield
            vnb = _b(vn)
            pb = _b(f["pmat"])
            sgz = _sig(z)
            sz = z * sgz
            nrm = lax.rsqrt(_rowmean(ov * ov) + EPS)
            on = ov * nrm
            ddnw = _sum0(dyv * on * sz)
            dpz = dyv * on * dnw * _dsilu(z, sgz)
            don = dyv * dnw * sz
            do = nrm * (don - on * _rowmean(don * on))
            dob = _b(do)
            dvn = _dot_tn(pb, dob) + _dot(kdb, dstb)
            dpm = jnp.where(ri >= ci, _dot_nt(dob, vnb), 0.0)
            dqe = _dot_nt(dob, sbf)
            dkd = _dot_nt(vnb, dstb)
            qtdo = _dot_tn(qeb, dob)
            yield
            drhs = dvn + _sdot_tn(f["tinv"], dvn)
            dqk = _b(dpm * f["dm"])
            dq = _dot(dqk, kbf) + dqe * eg
            dk_p = _dot_tn(dqk, _b(q))
            yield
            dam = jnp.where(ri > ci, -_sdot_nt(drhs, vn), 0.0)
            de = drhs * beta
            deb = _b(de)
            dbeta = _sum1(drhs * e)
            dkeg = -_dot_nt(deb, sbf)
            dst_new = qtdo + dst * jnp.exp(f["glast"]) - _dot_tn(kegb, deb)
            yield
            dkk = _b(dam * f["ds"])
            dkb = _dot(dkk, kbf)
            dk = dk_p + _dot_tn(dkk, _b(f["kb"])) + dkb * beta + dkeg * eg + dkd * ek
            yield
            dbeta = dbeta + _sum1(dkb * k)
            mm = dpm * f["pmat"] + dam * f["amat"]
            dglast = _sum1(_sum0(dst * st)) * jnp.exp(f["glast"]) + _sum1(_sum0(dkd * kd))
            dgc = (_sum1(mm) - _sum1(eye * _sum0(mm)) + _sum1(dqe * qe) + _sum1(dkeg * keg) - _sum1(dkd * kd))
            dgc = jnp.broadcast_to(dgc, (CHUNK, HD)) + jnp.where(row64 == CHUNK - 1, dglast, 0.0)
            dq_raw = (QSCALE * f["rq"]) * (dq - f["qn"] * _sum1(f["qn"] * dq))
            dk_raw = f["rk"] * (dk - k * _sum1(k * dk))
            dact = jnp.concatenate([dq_raw, dk_raw, de], axis=1)
            dc = dact * _dsilu(f["cpre"], f["sg"])
            dqkv, dcw_part = _conv_back(dc, dch, cw3, x3)
            cwacc = cwacc + dcw_part
            return dpz, dqkv, (dgc, dbeta), ddnw, (dst_new, dc[0:8], cwacc)

        def chunk(cidx, carry):
            heads, nacc, sa0, sa1 = carry
            c = nc - 1 - cidx
            t0 = pl.multiple_of(c * CHUNK, CHUNK)
            rows = pl.ds(t0, CHUNK)
            bav = ba_ref[rows, :]
            ins = []
            for j in range(hp):
                hl = slice(j * HD, (j + 1) * HD)
                x3 = jnp.concatenate([q_ref[rows, hl], k_ref[rows, hl], v_ref[rows, hl]], axis=1)
                ins.append((x3, z_ref[rows, hl], st_ref[j, c], ti_ref[j, c],
                            cp_ref[rows, j * 3 * HD:(j + 1) * 3 * HD], o_ref[rows, hl], dy_ref[rows, hl]))
            gates = _dn_gates(bav, dnv_v, masks)
            outs = _round_robin([one_head(j, ins[j][0], gates, *ins[j][1:], heads[j]) for j in range(hp)])
            dgc_all = jnp.zeros((CHUNK, HD), f32)
            dbeta_all = jnp.zeros((CHUNK, HD), f32)
            for j in range(hp):
                hl = slice(j * HD, (j + 1) * HD)
                dpz, dqkv, (dgc, dbeta), ddnw, _ = outs[j]
                dgc_all = jnp.where(lane == HEADS + h0 + j, dgc, dgc_all)
                dbeta_all = jnp.where(lane == h0 + j, dbeta, dbeta_all)
                dq_ref[rows, hl] = _b(dqkv[:, 0:HD])
                dk_ref[rows, hl] = _b(dqkv[:, HD:2 * HD])
                dv_ref[rows, hl] = _b(dqkv[:, 2 * HD:3 * HD])
                dz_ref[rows, hl] = _b(dpz)
                nacc = nacc + ddnw
            dg_all = _tri_dot(utri, dgc_all)
            dain_all = dg_all * (-gates["ea"]) * gates["sga"]
            dba = dbeta_all * gates["beta"] * (1.0 - gates["beta"]) + dain_all
            sa0 = sa0 + _sum0(dg_all * gates["g"])
            sa1 = sa1 + _sum0(dain_all)

            @pl.when(grp == 0)
            def _():
                dba_ref[rows, :] = dba

            @pl.when(grp > 0)
            def _():
                dba_ref[rows, :] += dba

            return tuple(out[4] for out in outs), nacc, sa0, sa1

        init = tuple((ds_ref[h0 + j], dch_ref[h0 + j], cwacc_ref[h0 + j][0:4]) for j in range(hp))
        heads, nacc, sa0, sa1 = lax.fori_loop(
            0, nc, chunk, (init, nacc_ref[0:1, :], nacc_ref[1:2, :], nacc_ref[2:3, :]))
        nacc_ref[0:3, :] = jnp.concatenate([nacc, sa0, sa1], axis=0)
        zpad = jnp.zeros((4, 3 * HD), f32)
        for j in range(hp):
            dst, dch, cwacc = heads[j]
            ds_ref[h0 + j] = dst
            dch_ref[h0 + j] = dch
            cwacc_ref[h0 + j] = jnp.concatenate([cwacc, zpad], axis=0)

        @pl.when(step == nsb - 1)
        def _():
            for j in range(hp):
                dcw_ref[h0 + j] = heads[j][2]

            @pl.when(grp == ngrp - 1)
            def _():
                dsc_ref[...] = jnp.concatenate([pltpu.roll(sa0, HD - HEADS, axis=1), pltpu.roll(sa1, HD - HEADS, axis=1),
                                                nacc, jnp.zeros((5, HD), f32)], axis=0)

    wide = hp * HD
    rblk = lambda i, g: nsb - 1 - i
    grp_specs = lambda off: _seq_specs(sb, wide, rblk, lambda i, g: off // hp + g)
    (qsp, qhalo), (ksp, khalo), (vsp, vhalo) = grp_specs(2 * HEADS), grp_specs(3 * HEADS), grp_specs(4 * HEADS)
    zsp, _ = grp_specs(5 * HEADS)
    hsp = lambda off: pl.BlockSpec((sb, wide), lambda i, g: (rblk(i, g), off // hp + g))
    basp = lambda col: pl.BlockSpec((sb, HD), lambda i, g: (rblk(i, g), col))
    cws = lambda off: pl.BlockSpec((4, wide), lambda i, g: (0, off // hp + g))
    whole = lambda shape: pl.BlockSpec(shape, lambda i, g: (0,) * len(shape))
    return pl.pallas_call(
        body, name=name, grid=(nsb, ngrp),
        in_specs=[hsp(HEADS), qsp, ksp, vsp, zsp, basp(BLK_BA), hsp(0),
                  pl.BlockSpec((hp, nc, HD, HD), lambda i, g: (g, rblk(i, g), 0, 0)),
                  pl.BlockSpec((hp, nc, CHUNK, CHUNK), lambda i, g: (g, rblk(i, g), 0, 0)),
                  pl.BlockSpec((sb, 3 * wide), lambda i, g: (rblk(i, g), g)),
                  cws(0), cws(HEADS), cws(2 * HEADS), whole((8, HD))],
        out_specs=[hsp(0), hsp(0), hsp(0), hsp(0), basp(0), whole((HEADS, 4, 3 * HD)), whole((8, HD))],
        out_shape=[jax.ShapeDtypeStruct((s, D_MODEL), bf16)] * 4
        + [jax.ShapeDtypeStruct((s, HD), f32), jax.ShapeDtypeStruct((HEADS, 4, 3 * HD), f32),
           jax.ShapeDtypeStruct((8, HD), f32)],
        scratch_shapes=[pltpu.VMEM((HEADS, HD, HD), f32), pltpu.VMEM((HEADS, 8, 3 * HD), f32),
                        pltpu.VMEM((HEADS, 8, 3 * HD), f32), pltpu.VMEM((8, HD), f32)],
        compiler_params=_params(("arbitrary", "arbitrary")),
    )(dy, proj, proj, proj, proj, proj, o, states, tinvs, cpres, cw, cw, cw, dnv)


ANY = pl.BlockSpec(memory_space=pl.ANY)
CHIP_MASKS = ((1, 0, 0), (0, 1, 0), (1, 1, 0))
ALL_MASKS = tuple((m >> 2 & 1, m >> 1 & 1, m & 1) for m in range(1, N_DEV))


def _me():
    return lax.axis_index("x"), lax.axis_index("y"), lax.axis_index("c")


def _flip(me, mask):
    return tuple((1 - v) if m else v for v, m in zip(me, mask))


def _dev_index(dev):
    return 4 * dev[0] + 2 * dev[1] + dev[2]


def _chip(dev):
    return 2 * dev[0] + dev[1]


def _comm_call(name, body, arrays, out_shapes, nsem, nloc=0):
    scratch = [pltpu.SemaphoreType.DMA((nsem,)), pltpu.SemaphoreType.DMA((nsem,))]
    if nloc:
        scratch.append(pltpu.SemaphoreType.DMA((nloc,)))
    return pl.pallas_call(
        body, name=name, in_specs=[ANY] * len(arrays), out_specs=[ANY] * len(out_shapes), out_shape=out_shapes,
        scratch_shapes=scratch, compiler_params=pltpu.CompilerParams(has_side_effects=True),
    )(*arrays)


def _put_own(gathered, own, slot):
    return lax.dynamic_update_index_in_dim(gathered, own, slot, 0)


def _gather_halves(arrays, whole_arrays, name):
    na, nw = len(arrays), len(whole_arrays)
    nm = len(CHIP_MASKS)

    def body(*refs):
        srcs, dsts = refs[:na + nw], refs[na + nw:2 * (na + nw)]
        send_sems, recv_sems = refs[2 * (na + nw):]
        me = _me()
        sib = _flip(me, (0, 0, 1))

        def half(a, ref, core):
            rows = arrays[a].shape[0] // 2
            return ref.at[pl.ds(pl.multiple_of(core * rows, rows), rows)]

        def over_ici(a, mi, sender, to):
            if a < na:
                src, dst = half(a, srcs[a], sender[2]), half(a, dsts[a].at[_chip(sender)], sender[2])
            else:
                src, dst = srcs[a], dsts[a].at[_chip(sender)]
            return pltpu.make_async_remote_copy(src_ref=src, dst_ref=dst, send_sem=send_sems.at[a * nm + mi],
                                                recv_sem=recv_sems.at[a * nm + mi], device_id=to, device_id_type=MESH)

        def over_d2d(a, mi, origin, sender, to):
            k = (na + nw) * nm + a * nm + mi
            blk = half(a, dsts[a].at[_chip(origin)], sender[2])
            return pltpu.make_async_remote_copy(src_ref=blk, dst_ref=blk, send_sem=send_sems.at[k],
                                                recv_sem=recv_sems.at[k], device_id=to, device_id_type=MESH)

        sends = []
        for a in range(na + nw):
            for mi, mask in enumerate(CHIP_MASKS):
                cp = over_ici(a, mi, me, _flip(me, mask))
                cp.start()
                sends.append(cp)
        for a in range(na + nw):
            for mi, mask in enumerate(CHIP_MASKS):
                peer = _flip(me, mask)
                over_ici(a, mi, peer, me).wait_recv()
                if a < na:
                    cp = over_d2d(a, mi, peer, me, sib)
                    cp.start()
                    sends.append(cp)
        for a in range(na):
            for mi, mask in enumerate(CHIP_MASKS):
                over_d2d(a, mi, _flip(sib, mask), sib, me).wait_recv()
        for cp in sends:
            cp.wait_send()

    every = list(arrays) + list(whole_arrays)
    got = _comm_call(name, body, every, [jax.ShapeDtypeStruct((4,) + t.shape, t.dtype) for t in every],
                     (2 * na + nw) * nm)
    chip = 2 * lax.axis_index("x") + lax.axis_index("y")
    return [_put_own(g, t, chip) for g, t in zip(got, every)]


HBM = pl.BlockSpec(memory_space=pltpu.HBM)
SEM = pl.BlockSpec(memory_space=pltpu.SEMAPHORE)
DATAFLOW = pltpu.SideEffectType.DATAFLOW_SIDE_EFFECTING


def _split_start(name, srcs, land_shapes, nsem, issue, after=None):
    ns, nl = len(srcs), len(land_shapes)
    n_in = ns + nl + (after is not None)

    def body(*refs):
        issue(refs[:ns], refs[ns:ns + nl], refs[n_in + ns + nl], refs[n_in + ns + nl + 1])
        token = refs[-1]
        token[...] = jnp.zeros_like(token)

    lands = [lax.empty(t.shape, t.dtype) for t in land_shapes]
    thru = [pltpu.HBM(t.shape, t.dtype) for t in list(srcs) + list(land_shapes)]
    hbm = lambda t: pltpu.with_memory_space_constraint(t, pltpu.HBM)
    return pl.pallas_call(
        body, name=name, in_specs=[HBM] * (ns + nl) + [ANY] * (after is not None),
        out_shape=(*thru, pltpu.SemaphoreType.DMA((nsem,)), pltpu.SemaphoreType.DMA((nsem,)),
                   jax.ShapeDtypeStruct((8, HD), f32)),
        out_specs=(*[HBM] * (ns + nl), SEM, SEM, pl.BlockSpec(memory_space=pltpu.VMEM)),
        input_output_aliases={i: i for i in range(ns + nl)},
        compiler_params=pltpu.CompilerParams(has_side_effects=DATAFLOW),
    )(*[hbm(t) for t in srcs], *[hbm(t) for t in lands], *([after] if after is not None else []))


def _split_wait(name, started, ns, after, finish):
    thru, send_sems, recv_sems = started[:-3], started[-3], started[-2]
    nt = len(thru)

    def body(*refs):
        finish(refs[:ns], refs[ns:nt], refs[nt], refs[nt + 1])

    outs = pl.pallas_call(
        body, name=name, in_specs=[HBM] * nt + [SEM, SEM, ANY],
        out_shape=tuple(pltpu.HBM(t.shape, t.dtype) for t in thru), out_specs=tuple([HBM] * nt),
        input_output_aliases={i: i for i in range(nt)},
        compiler_params=pltpu.CompilerParams(has_side_effects=DATAFLOW),
    )(*thru, send_sems, recv_sems, after)
    return list(outs[ns:])


def _gather_start(arrays, name, after=None):
    nm = len(CHIP_MASKS)

    def issue(srcs, lands, send_sems, recv_sems):
        me = _me()
        for a in range(len(arrays)):
            for mi, mask in enumerate(CHIP_MASKS):
                pltpu.make_async_remote_copy(
                    src_ref=srcs[a], dst_ref=lands[a].at[_chip(me)], send_sem=send_sems.at[a * nm + mi],
                    recv_sem=recv_sems.at[a * nm + mi], device_id=_flip(me, mask), device_id_type=MESH).start()

    shapes = [jax.ShapeDtypeStruct((4,) + t.shape, t.dtype) for t in arrays]
    return _split_start(name, arrays, shapes, len(arrays) * nm, issue, after)


def _gather_finish(started, arrays, after, name):
    nm = len(CHIP_MASKS)

    def finish(srcs, lands, send_sems, recv_sems):
        me = _me()
        for a in range(len(arrays)):
            for mi, mask in enumerate(CHIP_MASKS):
                peer = _flip(me, mask)
                cp = pltpu.make_async_remote_copy(
                    src_ref=srcs[a], dst_ref=lands[a].at[_chip(peer)], send_sem=send_sems.at[a * nm + mi],
                    recv_sem=recv_sems.at[a * nm + mi], device_id=peer, device_id_type=MESH)
                cp.wait_send()
                cp.wait_recv()

    got = _split_wait(name, started, len(arrays), after, finish)
    chip = 2 * lax.axis_index("x") + lax.axis_index("y")
    return [_put_own(g, t, chip) for g, t in zip(got, arrays)]


HALF_IN, HALF_OUT = D_MODEL // 2, D_MIX // 8


def _scatter_piece(kind, ref, d):
    j, r = d >> 1, d & 1
    if kind == "win":
        return ref.at[pl.ds(HALF_IN * r, HALF_IN), pl.ds(WIN_STRIDE * j, WIN_W)]
    if kind == "wout":
        return ref.at[pl.ds(2 * HALF_OUT * j + HALF_OUT * r, HALF_OUT)]
    return ref.at[d]


def _scatter_start(arrays, kinds, name):
    na = len(arrays)

    def issue(srcs, lands, send_sems, recv_sems):
        me = _me()
        my = _dev_index(me)
        for d in range(N_DEV):
            dev = (d >> 2, (d >> 1) & 1, d & 1)
            for a in range(na):
                @pl.when(my != d)
                def _(a=a, d=d, dev=dev):
                    pltpu.make_async_remote_copy(
                        src_ref=_scatter_piece(kinds[a], srcs[a], d), dst_ref=lands[a].at[my],
                        send_sem=send_sems.at[a * N_DEV + d], recv_sem=recv_sems.at[a * N_DEV + my],
                        device_id=dev, device_id_type=MESH).start()

    shape_of = {"win": (N_DEV, HALF_IN, WIN_W), "wout": (N_DEV, HALF_OUT, D_MODEL)}
    shapes = [jax.ShapeDtypeStruct(shape_of.get(k, t.shape), t.dtype) for t, k in zip(arrays, kinds)]
    return _split_start(name, arrays, shapes, na * N_DEV, issue)


def _scatter_finish(started, arrays, kinds, after, name):
    na = len(arrays)

    def finish(srcs, lands, send_sems, recv_sems):
        me = _me()
        my = _dev_index(me)
        for s in range(N_DEV):
            for a in range(na):
                @pl.when(my != s)
                def _(a=a, s=s):
                    cp = pltpu.make_async_remote_copy(
                        src_ref=_scatter_piece(kinds[a], srcs[a], s), dst_ref=lands[a].at[s],
                        send_sem=send_sems.at[a * N_DEV + s], recv_sem=recv_sems.at[a * N_DEV + s],
                        device_id=me, device_id_type=MESH)
                    cp.wait_recv()
                    cp.wait_send()

    got = _split_wait(name, started, na, after, finish)
    x, y, c = _me()
    chip, my = 2 * x + y, 4 * x + 2 * y + c
    own = {"win": lambda t: lax.dynamic_slice(t, (HALF_IN * c, WIN_STRIDE * chip), (HALF_IN, WIN_W)),
           "wout": lambda t: lax.dynamic_slice(t, (2 * HALF_OUT * chip + HALF_OUT * c, 0), (HALF_OUT, D_MODEL)),
           "small": lambda t: lax.dynamic_index_in_dim(t, my, 0, keepdims=False)}
    return [_put_own(g, own[k](t), my) for g, t, k in zip(got, arrays, kinds)]


def _share_reduced(pair_arrays, small, name):
    arrays = list(pair_arrays) + ([small] if small is not None else [])
    na, npair = len(arrays), len(pair_arrays)
    nm = len(ALL_MASKS)

    def body(*refs):
        srcs, dsts = refs[:na], refs[na:2 * na]
        send_sems, recv_sems = refs[2 * na:]
        me = _me()
        sib = _flip(me, (0, 0, 1))

        def copy(a, k, sender, to):
            slot = sender[2] if a < npair else _dev_index(sender)
            return pltpu.make_async_remote_copy(
                src_ref=srcs[a], dst_ref=dsts[a].at[slot], send_sem=send_sems.at[k], recv_sem=recv_sems.at[k],
                device_id=to, device_id_type=MESH)

        sends = [copy(a, a, me, sib) for a in range(npair)]
        if small is not None:
            sends += [copy(npair, npair + mi, me, _flip(me, mask)) for mi, mask in enumerate(ALL_MASKS)]
        for cp in sends:
            cp.start()
        for a in range(npair):
            copy(a, a, sib, me).wait_recv()
        if small is not None:
            for mi, mask in enumerate(ALL_MASKS):
                copy(npair, npair + mi, _flip(me, mask), me).wait_recv()
        for cp in sends:
            cp.wait_send()

    shapes = [jax.ShapeDtypeStruct((2,) + t.shape, t.dtype) for t in pair_arrays]
    if small is not None:
        shapes.append(jax.ShapeDtypeStruct((N_DEV,) + small.shape, small.dtype))
    got = _comm_call(name, body, arrays, shapes, npair + (nm if small is not None else 0))
    x, y, c = _me()
    slots = [c] * npair + [4 * x + 2 * y + c]
    return [_put_own(g, t, slot) for g, t, slot in zip(got, arrays, slots)]


def _sum_parts(parts, name):
    _, r, c = parts.shape
    tr = r
    while tr * c * 4 * N_DEV > (8 << 20) and tr % 32 == 0:
        tr //= 2

    def body(p_ref, o_ref):
        total = p_ref[0].astype(f32)
        for d in range(1, N_DEV):
            total = total + p_ref[d].astype(f32)
        o_ref[...] = total

    return pl.pallas_call(
        body, name=name, grid=(r // tr,), in_specs=[pl.BlockSpec((N_DEV, tr, c), lambda i: (0, i, 0))],
        out_specs=pl.BlockSpec((tr, c), lambda i: (i, 0)), out_shape=jax.ShapeDtypeStruct((r, c), f32),
        compiler_params=_params(("parallel",)),
    )(parts)


def _adamw(w, g, m, v, name):
    shape = w.shape
    cols = shape[-1]
    slabs = w.ndim == 3 and shape[1] < 8
    rows = w.size // cols
    tr = rows
    while tr * cols * 4 > (1 << 20) and tr % 16 == 0:
        tr //= 2
    c1 = 1.0 / (1.0 - ADAM_B1 ** ADAM_STEP)
    c2 = 1.0 / (1.0 - ADAM_B2 ** ADAM_STEP)

    def body(w_ref, g_ref, m_ref, v_ref, d_ref, nm_ref, nv_ref, *g_out):
        gv = g_ref[...]
        mv = ADAM_B1 * m_ref[...] + (1.0 - ADAM_B1) * gv
        vv = ADAM_B2 * v_ref[...] + (1.0 - ADAM_B2) * (gv * gv)
        nm_ref[...] = mv
        nv_ref[...] = vv
        d_ref[...] = -ADAM_LR * ((mv * c1) / (jnp.sqrt(vv * c2) + ADAM_EPS) + ADAM_WD * w_ref[...])
        for ref in g_out:
            ref[...] = gv

    if slabs:
        tl = max(d for d in range(1, shape[0] + 1) if shape[0] % d == 0 and d * shape[1] * cols * 4 <= (3 << 19))
        spec = pl.BlockSpec((tl, shape[1], cols), lambda i: (i, 0, 0))
        return tuple(pl.pallas_call(
            body, name=name, grid=(shape[0] // tl,), in_specs=[spec] * 4, out_specs=[spec] * 4,
            out_shape=[jax.ShapeDtypeStruct(shape, f32)] * 4, compiler_params=_params(("parallel",)),
        )(w, g, m, v))
    spec = pl.BlockSpec((tr, cols), lambda i: (i, 0))
    outs = pl.pallas_call(
        body, name=name, grid=(rows // tr,), in_specs=[spec] * 4, out_specs=[spec] * 3,
        out_shape=[jax.ShapeDtypeStruct((rows, cols), f32)] * 3, compiler_params=_params(("parallel",)),
    )(*[t.reshape(rows, cols) for t in (w, g, m, v)])
    return tuple(t.reshape(shape) for t in outs)


def _pad_lanes(t, n=HD):
    return jnp.pad(t, [(0, 0)] * (t.ndim - 1) + [(0, n - t.shape[-1])])


def _rows128(t, rows=None):
    t = t.reshape(-1, HD)
    rows = rows or -(-t.shape[0] // 8) * 8
    return jnp.pad(t, ((0, rows - t.shape[0]), (0, 0)))


def kernel(x, norm_w, w_in, lru_conv_w, lru_conv_b, lru_wa, lru_ba, lru_wx, lru_bx, lru_lambda, lru_norm_w, dn_conv_w, dn_A_log, dn_dt_bias, dn_norm_w, w_out, final_norm_w, loss_target, m_norm_w, m_w_in, m_lru_conv_w, m_lru_conv_b, m_lru_wa, m_lru_ba, m_lru_wx, m_lru_bx, m_lru_lambda, m_lru_norm_w, m_dn_conv_w, m_dn_A_log, m_dn_dt_bias, m_dn_norm_w, m_w_out, m_final_norm_w, v_norm_w, v_w_in, v_lru_conv_w, v_lru_conv_b, v_lru_wa, v_lru_ba, v_lru_wx, v_lru_bx, v_lru_lambda, v_lru_norm_w, v_dn_conv_w, v_dn_A_log, v_dn_dt_bias, v_dn_norm_w, v_w_out, v_final_norm_w):
    depth = norm_w.shape[0]
    xs = x[0]
    target = loss_target[0]
    chip = 2 * lax.axis_index("x") + lax.axis_index("y")

    win_b = [w_in[l].astype(bf16) for l in range(depth)]
    wout_b = [w_out[l].astype(bf16) for l in range(depth)]
    got_in0 = _gather_halves([win_b[0]], [], "gather_weights0")[0]
    rest0 = [wout_b[0], lru_conv_w, dn_conv_w]
    rest0_started = _gather_start(rest0, "gather_rest0_start")
    gathered, later = {}, {}
    pad_cols = jnp.zeros((D_MODEL, D_INP - D_IN), bf16)

    def weights_of(l, after):
        if l == 0:
            got, token = got_in0, rest0_started[-1]
        else:
            gathered[l] = _gather_finish(later[l], [win_b[l], wout_b[l]], after, f"gather_weights{l}_wait")
            got, token = gathered[l][0], None
        return jnp.concatenate([got[j] for j in range(4)] + [pad_cols], axis=1), token

    def rest_of(l, after):
        token = None
        if "rest0" not in gathered:
            gathered["rest0"] = _gather_finish(rest0_started, rest0, after, "gather_rest0_wait")
            for k in range(1, depth):
                later[k] = _gather_start([win_b[k], wout_b[k]], f"gather_weights{k}_start", after=gathered["rest0"][0])
                token = later[k][-1] if token is None else token + later[k][-1]
        g_out, g_lcw, g_dcw = gathered["rest0"]
        if l > 0:
            g_out = gathered[l][1]
        lcw_full = g_lcw.transpose(1, 2, 0, 3).reshape(depth, 4, D_MODEL)
        dcw_full = g_dcw.transpose(1, 2, 0, 3).reshape(depth, 4, 3 * D_MODEL)
        return g_out.reshape(D_MIX, D_MODEL), _behind(lcw_full[l], token), dcw_full[l]

    in_flight = {}

    def on_grad(kind, l, g):
        if l == depth - 1 and depth > 1:
            if kind == "wout":
                in_flight["held"] = g
                return None
            srcs, kinds, key = [g, in_flight.pop("held")], ["win", "wout"], ("both", l)
        else:
            srcs, kinds, key = [g], [kind], (kind, l)
        started = _scatter_start(srcs, kinds, f"scatter_{key[0]}{key[1]}_start")
        in_flight[key] = (started, srcs, kinds)
        return started[-1]

    vecs, dnvs = [], []
    zrow = jnp.zeros((D_MODEL,), f32)
    for l in range(depth):
        vecs.append(jnp.stack([lru_conv_b[l], lru_ba[l], lru_bx[l], lru_lambda[l], lru_norm_w[l], zrow, zrow, zrow]))
        dnvs.append(jnp.concatenate([_pad_lanes(dn_A_log[l][None]), _pad_lanes(dn_dt_bias[l][None]),
                                     dn_norm_w[l][None], jnp.zeros((5, HD), f32)], axis=0))

    loss_part, grad_x, d_fnw, g_nw, g_vec, g_wa, g_wx, g_lcw_, g_dcw_, g_dsc = _local_step(
        xs, target, norm_w, final_norm_w, weights_of, rest_of, on_grad, vecs, dnvs, lru_wa, lru_wx)
    loss = lax.psum(loss_part[0, 0], ("x", "y", "c"))
    grads = _reduce_grads(chip, grad_x, in_flight, d_fnw, g_nw, g_vec, g_wa, g_wx, g_lcw_, g_dcw_, g_dsc)

    names = ["norm_w", "w_in", "lru_conv_w", "lru_conv_b", "lru_wa", "lru_ba", "lru_wx", "lru_bx", "lru_lambda",
             "lru_norm_w", "dn_conv_w", "dn_A_log", "dn_dt_bias", "dn_norm_w", "w_out", "final_norm_w"]
    ws = dict(zip(names, [norm_w, w_in, lru_conv_w, lru_conv_b, lru_wa, lru_ba, lru_wx, lru_bx, lru_lambda, lru_norm_w,
                          dn_conv_w, dn_A_log, dn_dt_bias, dn_norm_w, w_out, final_norm_w]))
    ms = dict(zip(names, [m_norm_w, m_w_in, m_lru_conv_w, m_lru_conv_b, m_lru_wa, m_lru_ba, m_lru_wx, m_lru_bx,
                          m_lru_lambda, m_lru_norm_w, m_dn_conv_w, m_dn_A_log, m_dn_dt_bias, m_dn_norm_w, m_w_out,
                          m_final_norm_w]))
    vs = dict(zip(names, [v_norm_w, v_w_in, v_lru_conv_w, v_lru_conv_b, v_lru_wa, v_lru_ba, v_lru_wx, v_lru_bx,
                          v_lru_lambda, v_lru_norm_w, v_dn_conv_w, v_dn_A_log, v_dn_dt_bias, v_dn_norm_w, v_w_out,
                          v_final_norm_w]))
    to_cols = lambda t: jnp.transpose(t, (2, 0, 1))
    from_cols = lambda t: jnp.transpose(t, (1, 2, 0))
    deltas, new_m, new_v = [], [], []
    for n in names:
        if n in ("w_in", "lru_conv_w", "dn_conv_w"):
            view, back = (to_cols, from_cols) if n == "w_in" else (lambda t: t, lambda t: t)
            d, nm_, nv_, grads[n] = (back(t) for t in _adamw(view(ws[n]), view(grads[n]), view(ms[n]), view(vs[n]),
                                                            f"adamw_{n}"))
        else:
            d, nm_, nv_ = _adamw(ws[n], grads[n], ms[n], vs[n], f"adamw_{n}")
        deltas.append(d)
        new_m.append(nm_)
        new_v.append(nv_)
    return (loss, grad_x[None], *[grads[n] for n in names], *deltas, *new_m, *new_v)


def _behind(t, token):
    return t if token is None else t + token[0:1, 0:1]


def _local_step(xs, target, norm_w, final_norm_w, weights_of, rest_of, on_grad, vecs, dnvs, lru_wa, lru_wx):
    depth = norm_w.shape[0]
    saved = []
    h = xs
    for l in range(depth):
        wp_l, token = weights_of(l, h)
        hn, hnt = _rms_fwd(h, _behind(norm_w[l][None], token), f"rms_fwd{l}")
        proj = _matmul(hn, wp_l, tm=2048, tn=896, tk=D_MODEL, name=f"in_proj{l}")
        wo_l, lcw_l, dcw_l = rest_of(l, proj)
        y, hl = _lru_fwd(proj, lcw_l, vecs[l], lru_wa[l], lru_wx[l], f"lru_fwd{l}")
        y, o, states, tinvs, cpres = _dn_fwd(proj, dcw_l, dnvs[l], y, f"dn_fwd{l}")
        out = _matmul(y, wo_l, tm=1024, tn=D_MODEL, tk=D_MIX, res=h, name=f"out_proj{l}")
        saved.append((h, hnt, proj, hl, o, states, tinvs, cpres, y, wp_l, wo_l, lcw_l, dcw_l))
        h = out

    dh, d_fnw, loss_part = _final_loss(h, final_norm_w[None], target, "final_loss")

    g_nw, g_vec, g_wa, g_wx, g_lcw_, g_dcw_, g_dsc = ([None] * depth for _ in range(7))
    for l in reversed(range(depth)):
        hin, hnt, proj, hl, o, states, tinvs, cpres, y, wp_l, wo_l, lcw_l, dcw_l = saved[l]
        dy = _matmul(dh, wo_l, tb=True, tm=2048, tn=1024, tk=D_MODEL, name=f"d_y{l}")
        g_wout = _matmul(y, dh, ta=True, tm=1024, tn=D_MODEL, tk=512, out_dtype=bf16, name=f"d_wout{l}")
        token = on_grad("wout", l, g_wout)
        dlx, dlz, g_lcw_[l], g_vec[l], g_wa[l], g_wx[l] = _lru_bwd(dy, proj, hl, lcw_l, _behind(vecs[l], token),
                                                                 lru_wa[l], lru_wx[l], f"lru_bwd{l}")
        dq, dk, dv, dz, dba, dcw8, g_dsc[l] = _dn_bwd(dy, proj, o, states, tinvs, cpres, dcw_l, dnvs[l], f"dn_bwd{l}")
        g_dcw_[l] = dcw8.reshape(HEADS, 4, 3, HD).transpose(1, 2, 0, 3).reshape(4, 3 * D_MODEL)
        dxs = [dlx, dlz, dq, dk, dv, dz]
        g_win = _d_win(hnt, dxs, dba, f"d_win{l}")
        token = on_grad("win", l, g_win)
        dh, g_nw[l] = _d_hn(dxs, _behind(dba, token), wp_l, hin, norm_w[l][None], dh, f"d_hn{l}")
    return loss_part, dh, d_fnw, g_nw, g_vec, g_wa, g_wx, g_lcw_, g_dcw_, g_dsc


def _reduce_grads(chip, after, in_flight, d_fnw, g_nw, g_vec, g_wa, g_wx, g_lcw_, g_dcw_, g_dsc):
    depth = len(g_nw)
    both = lambda f: jnp.concatenate([f(l) for l in range(depth)])
    small = [
        both(lambda l: _rows128(g_nw[l])),
        both(lambda l: _rows128(g_vec[l][0])), both(lambda l: _rows128(g_vec[l][1])),
        both(lambda l: _rows128(g_vec[l][2])), both(lambda l: _rows128(g_vec[l][3])),
        both(lambda l: _rows128(g_vec[l][4])),
        both(lambda l: _rows128(g_wa[l])), both(lambda l: _rows128(g_wx[l])),
        both(lambda l: _rows128(g_dsc[l][0:1])), both(lambda l: _rows128(g_dsc[l][1:2])),
        both(lambda l: _rows128(g_dsc[l][2:3])),
        _rows128(d_fnw),
        both(lambda l: _rows128(g_lcw_[l])), both(lambda l: _rows128(g_dcw_[l])),
    ]
    counts = [t.shape[0] for t in small]
    total = sum(counts)
    per = -(-total // (8 * N_DEV)) * 8
    small = jnp.concatenate(small + [jnp.zeros((per * N_DEV - total, HD), f32)]).reshape(N_DEV, per, HD)

    small_started = _scatter_start([small], ["small"], "scatter_small_start")
    reduced, pairs = {}, [None] * depth
    for key in sorted(in_flight, key=lambda k: -k[1]):
        started, srcs, kinds = in_flight[key]
        got = _scatter_finish(started, srcs, kinds, after, f"scatter_{key[0]}{key[1]}_wait")
        for kind, part in zip(kinds, got):
            reduced[(kind, key[1])] = _sum_parts(part, f"sum_{kind}{key[1]}")
        after = reduced[(kinds[-1], key[1])]
        if key[1] > 0 and ("win", key[1]) in reduced and ("wout", key[1]) in reduced:
            pairs[key[1]] = _share_reduced([reduced[("win", key[1])], reduced[("wout", key[1])]], None,
                                           f"share_grads{key[1]}")
            after = pairs[key[1]][0]
    got = _scatter_finish(small_started, [small], ["small"], after, "scatter_small_wait")
    shared = _share_reduced([reduced[("win", 0)], reduced[("wout", 0)]], _sum_parts(got[0], "sum_small"), "share_grads0")
    pairs[0] = shared
    a_small = shared[2].reshape(N_DEV * per, HD)
    grad_w_in = jnp.stack([lax.dynamic_slice_in_dim(pairs[l][0].reshape(D_MODEL, WIN_W), 4 * chip, SHARD_IN, 1)
                           for l in range(depth)])
    grad_w_out = jnp.stack([pairs[l][1].reshape(D_MIX // 4, D_MODEL) for l in range(depth)])

    offs = [0]
    for c in counts:
        offs.append(offs[-1] + c)

    def piece(k, rows_per_layer=None):
        t = a_small[offs[k]:offs[k + 1]]
        if rows_per_layer is None:
            return t
        return t.reshape(depth, -1, HD)[:, :rows_per_layer]

    vec = lambda k: piece(k).reshape(depth, D_MODEL)
    return {
        "norm_w": vec(0), "lru_conv_b": vec(1), "lru_ba": vec(2), "lru_bx": vec(3), "lru_lambda": vec(4),
        "lru_norm_w": vec(5),
        "lru_wa": piece(6).reshape(depth, HEADS, HD, HD), "lru_wx": piece(7).reshape(depth, HEADS, HD, HD),
        "dn_A_log": piece(8, 1)[:, 0, :HEADS], "dn_dt_bias": piece(9, 1)[:, 0, :HEADS], "dn_norm_w": piece(10, 1)[:, 0],
        "final_norm_w": piece(11).reshape(D_MODEL),
        "lru_conv_w": lax.dynamic_slice_in_dim(piece(12).reshape(depth, 4, D_MODEL), chip * (D_MODEL // 4), D_MODEL // 4, 2),
        "dn_conv_w": lax.dynamic_slice_in_dim(piece(13).reshape(depth, 4, 3 * D_MODEL), chip * (3 * D_MODEL // 4),
                                              3 * D_MODEL // 4, 2),
        "w_in": grad_w_in, "w_out": grad_w_out,
    }
```

```python
import jax
import jax.numpy as jnp
from jax import lax
from jax.experimental import pallas as pl
from jax.experimental.pallas import tpu as pltpu

f32 = jnp.float32
bf16 = jnp.bfloat16
MESH = pl.DeviceIdType.MESH

D_MODEL = 1024
HEADS = 8
HD = 128
CHUNK = 64
LRU_T = 128
SEQ_BLOCK = 512
LRU_HP = 8
DN_SEQ_BLOCK = 256
DN_HP = 8
LRU_C = 8.0
D_IN = 6160
D_INP = 6272
D_MIX = 2048
N_GROUPS = 6
BLK_BA = 48
SHARD_IN = D_IN // 4
WIN_W = 1664
WIN_STRIDE = 1536
EPS = 1e-6
QSCALE = HD ** -0.5
N_DEV = 8
VMEM_LIMIT = 56 * 1024 * 1024

ADAM_LR, ADAM_B1, ADAM_B2, ADAM_EPS, ADAM_WD, ADAM_STEP = 0.001, 0.9, 0.999, 1e-08, 0.01, 10


def _sig(x):
    return 1.0 / (1.0 + jnp.exp(-x))


def _log1p(u):
    w = 1.0 + u
    d = w - 1.0
    return jnp.where(d == 0.0, u, jnp.log(w) * (u / jnp.where(d == 0.0, 1.0, d)))


def _softplus(x):
    return jnp.maximum(x, 0.0) + _log1p(jnp.exp(-jnp.abs(x)))


def _expm1(x):
    t = x * (1.0 + x * (0.5 + x * (1.0 / 6 + x * (1.0 / 24 + x * (1.0 / 120 + x * (1.0 / 720))))))
    return jnp.where(jnp.abs(x) < 0.2, t, jnp.exp(x) - 1.0)


def _dot(a, b, prec=None):
    return jnp.dot(a, b, precision=prec, preferred_element_type=f32)


def _dot_nt(a, b, prec=None):
    return lax.dot_general(a, b, (((1,), (1,)), ((), ())), precision=prec, preferred_element_type=f32)


def _dot_tn(a, b, prec=None):
    return lax.dot_general(a, b, (((0,), (0,)), ((), ())), precision=prec, preferred_element_type=f32)


def _b(x):
    return x.astype(bf16)


def _sum0(x):
    return jnp.sum(x, axis=0, keepdims=True)


def _sum1(x):
    return jnp.sum(x, axis=1, keepdims=True)


def _rowmean(x):
    return jnp.mean(x, axis=-1, keepdims=True)


def _dsilu(z, sg):
    return sg * (1.0 + z * (1.0 - sg))


def _conv_taps(x, halo):
    xx = jnp.concatenate([halo, x], axis=0)
    return [pltpu.roll(xx, 3, axis=0)[8:], pltpu.roll(xx, 2, axis=0)[8:], pltpu.roll(xx, 1, axis=0)[8:], x]


def _conv(xs, cw):
    return cw[0:1] * xs[0] + cw[1:2] * xs[1] + cw[2:3] * xs[2] + cw[3:4] * xs[3]


def _rows_before(ref, halo_ref, lanes, t0, c, sblk):
    tprev = pl.multiple_of(jnp.maximum(t0 - 8, 0), 8)
    return jnp.where(c > 0, ref[pl.ds(tprev, 8), lanes], jnp.where(sblk > 0, halo_ref[:, lanes], 0.0))


def _conv_back(dxc, halo_after, cw, x=None):
    rows = dxc.shape[0]
    dd = jnp.concatenate([dxc, halo_after], axis=0)
    out = cw[3:4] * dxc
    dcw = [None, None, None, None if x is None else _sum0(x * dxc)]
    for s in (1, 2, 3):
        shifted = pltpu.roll(dd, rows + 8 - s, axis=0)[:rows]
        out = out + cw[3 - s:4 - s] * shifted
        if x is not None:
            dcw[3 - s] = _sum0(x * shifted)
    return out if x is None else (out, jnp.concatenate(dcw, axis=0))


def _params(sem=None):
    return pltpu.CompilerParams(dimension_semantics=sem, vmem_limit_bytes=VMEM_LIMIT)


def _seq_specs(sb, width, rowblk, colblk):
    main = pl.BlockSpec((sb, width), lambda a, b: (rowblk(a, b), colblk(a, b)))
    halo = pl.BlockSpec((8, width), lambda a, b: (jnp.maximum(rowblk(a, b) * (sb // 8) - 1, 0), colblk(a, b)))
    return main, halo


def _matmul(a, b, *, ta=False, tb=False, tm, tn, tk, res=None, out_dtype=f32, name):
    if ta:
        kdim, m = a.shape
    else:
        m, kdim = a.shape
    n = b.shape[0] if tb else b.shape[1]
    tm, tn, tk = min(tm, m), min(tn, n), min(tk, kdim)
    assert m % tm == 0 and n % tn == 0 and kdim % tk == 0, (name, m, n, kdim, tm, tn, tk)
    nk = kdim // tk
    dims = (((0 if ta else 1,), (1 if tb else 0,)), ((), ()))
    has_res = res is not None

    def body(*refs):
        a_ref, b_ref = refs[:2]
        r_ref = refs[2] if has_res else None
        o_ref = refs[3] if has_res else refs[2]
        acc_ref = refs[-1] if nk > 1 else None
        part = lax.dot_general(_b(a_ref[...]), _b(b_ref[...]), dims, preferred_element_type=f32)

        def finish(total):
            if has_res:
                total = total + r_ref[...]
            o_ref[...] = total.astype(out_dtype)

        if nk == 1:
            finish(part)
        else:
            k = pl.program_id(2)

            @pl.when(k == 0)
            def _():
                acc_ref[...] = part

            @pl.when(k > 0)
            def _():
                acc_ref[...] += part

            @pl.when(k == nk - 1)
            def _():
                finish(acc_ref[...])

    a_spec = pl.BlockSpec((tk, tm), lambda i, j, k: (k, i)) if ta else pl.BlockSpec((tm, tk), lambda i, j, k: (i, k))
    b_spec = pl.BlockSpec((tn, tk), lambda i, j, k: (j, k)) if tb else pl.BlockSpec((tk, tn), lambda i, j, k: (k, j))
    o_spec = pl.BlockSpec((tm, tn), lambda i, j, k: (i, j))
    in_specs, args = [a_spec, b_spec], [a, b]
    if has_res:
        in_specs.append(o_spec)
        args.append(res)
    return pl.pallas_call(
        body, name=name, grid=(m // tm, n // tn, nk), in_specs=in_specs, out_specs=o_spec,
        out_shape=jax.ShapeDtypeStruct((m, n), out_dtype),
        scratch_shapes=[pltpu.VMEM((tm, tn), f32)] if nk > 1 else [],
        compiler_params=_params(("parallel", "parallel", "arbitrary")),
    )(*args)


def _d_hn(dxs, dba, wp, x, w, dres, name):
    s = dxs[0].shape[0]
    tm = min(256, s)

    def body(*refs):
        dx_refs = refs[:N_GROUPS]
        dba_ref, w_ref, x_ref, nw_ref, dres_ref, o_ref, dw_ref = refs[N_GROUPS:]
        dhn = _dot_nt(_b(dba_ref[...]), w_ref[:, BLK_BA * HD:])
        for g in range(N_GROUPS):
            dhn = dhn + _dot_nt(dx_refs[g][...], w_ref[:, g * D_MODEL:(g + 1) * D_MODEL])
        xv = x_ref[...]
        r = lax.rsqrt(_rowmean(xv * xv) + EPS)
        xn = xv * r
        dxn = dhn * nw_ref[...]
        o_ref[...] = dres_ref[...] + r * (dxn - xn * _rowmean(dxn * xn))
        part = _sum0(dhn * xn)

        @pl.when(pl.program_id(0) == 0)
        def _():
            dw_ref[...] = part

        @pl.when(pl.program_id(0) > 0)
        def _():
            dw_ref[...] += part

    row = lambda wd: pl.BlockSpec((tm, wd), lambda i: (i, 0))
    vec = pl.BlockSpec((1, D_MODEL), lambda i: (0, 0))
    return pl.pallas_call(
        body, name=name, grid=(s // tm,),
        in_specs=[row(D_MODEL)] * N_GROUPS + [row(HD), pl.BlockSpec((D_MODEL, D_INP), lambda i: (0, 0)),
                                              row(D_MODEL), vec, row(D_MODEL)],
        out_specs=[row(D_MODEL), vec],
        out_shape=[jax.ShapeDtypeStruct((s, D_MODEL), f32), jax.ShapeDtypeStruct((1, D_MODEL), f32)],
        compiler_params=_params(("arbitrary",)),
    )(*dxs, dba, wp, x, w, dres)


def _d_win(hnt, dxs, dba, name):
    s = hnt.shape[1]
    tn = 256
    per = D_MODEL // tn

    def body(*refs):
        hnt_ref = refs[0]
        dx_refs = refs[1:1 + N_GROUPS]
        o_ref = refs[1 + N_GROUPS]
        j = pl.program_id(0)
        for g in range(N_GROUPS):
            @pl.when(j // per == g)
            def _(g=g):
                o_ref[...] = _dot(hnt_ref[...], _b(dx_refs[g][...])).astype(bf16)

    def dx_spec(g):
        on = lambda j: (j // per == g).astype(jnp.int32)
        return pl.BlockSpec((s, tn), lambda j: (0, (j % per) * on(j)))

    main = pl.pallas_call(
        body, name=name, grid=(N_GROUPS * per,),
        in_specs=[pl.BlockSpec((D_MODEL, s), lambda j: (0, 0))] + [dx_spec(g) for g in range(N_GROUPS)],
        out_specs=pl.BlockSpec((D_MODEL, tn), lambda j: (0, j)),
        out_shape=jax.ShapeDtypeStruct((D_MODEL, D_INP), bf16),
        compiler_params=_params(("parallel",)),
    )(hnt, *dxs)

    def tail(hnt_ref, dba_ref, full_ref, o_ref):
        del full_ref
        o_ref[...] = _dot(hnt_ref[...], _b(dba_ref[...])).astype(bf16)

    return pl.pallas_call(
        tail, name=name + "_ba", grid=(1,),
        in_specs=[pl.BlockSpec((D_MODEL, s), lambda k: (0, 0)), pl.BlockSpec((s, HD), lambda k: (0, 0)),
                  pl.BlockSpec(memory_space=pl.ANY)],
        out_specs=pl.BlockSpec((D_MODEL, HD), lambda k: (0, BLK_BA)),
        out_shape=jax.ShapeDtypeStruct((D_MODEL, D_INP), bf16),
        input_output_aliases={2: 0},
        compiler_params=_params(("arbitrary",)),
    )(hnt, dba, main)


def _rms_fwd(x, w, name):
    s = x.shape[0]
    tr = min(512, s)

    def body(x_ref, w_ref, h_ref, ht_ref):
        xv = x_ref[...]
        r = lax.rsqrt(_rowmean(xv * xv) + EPS)
        h = (xv * r * w_ref[...]).astype(bf16)
        h_ref[...] = h
        ht_ref[...] = h.T

    return pl.pallas_call(
        body, name=name, grid=(s // tr,),
        in_specs=[pl.BlockSpec((tr, D_MODEL), lambda i: (i, 0)), pl.BlockSpec((1, D_MODEL), lambda i: (0, 0))],
        out_specs=[pl.BlockSpec((tr, D_MODEL), lambda i: (i, 0)), pl.BlockSpec((D_MODEL, tr), lambda i: (0, i))],
        out_shape=[jax.ShapeDtypeStruct((s, D_MODEL), bf16), jax.ShapeDtypeStruct((D_MODEL, s), bf16)],
        compiler_params=_params(("parallel",)),
    )(x, w)


def _final_loss(x, w, target, name):
    s = x.shape[0]
    tr = min(512, s)

    def body(x_ref, w_ref, t_ref, dx_ref, dw_ref, loss_ref):
        xv = x_ref[...]
        wv = w_ref[...]
        r = lax.rsqrt(_rowmean(xv * xv) + EPS)
        xn = xv * r
        e = xn * wv - t_ref[...]
        lb = jnp.broadcast_to(0.5 * _sum0(_rowmean(e * e)), (1, HD))
        dy = e * (1.0 / D_MODEL)
        dxn = dy * wv
        dx_ref[...] = r * (dxn - xn * _rowmean(dxn * xn))
        part = _sum0(dy * xn)

        @pl.when(pl.program_id(0) == 0)
        def _():
            dw_ref[...] = part
            loss_ref[...] = lb

        @pl.when(pl.program_id(0) > 0)
        def _():
            dw_ref[...] += part
            loss_ref[...] += lb

    row = pl.BlockSpec((tr, D_MODEL), lambda i: (i, 0))
    vec = pl.BlockSpec((1, D_MODEL), lambda i: (0, 0))
    lsp = pl.BlockSpec((1, HD), lambda i: (0, 0))
    return pl.pallas_call(
        body, name=name, grid=(s // tr,), in_specs=[row, vec, row], out_specs=[row, vec, lsp],
        out_shape=[jax.ShapeDtypeStruct((s, D_MODEL), f32), jax.ShapeDtypeStruct((1, D_MODEL), f32),
                   jax.ShapeDtypeStruct((1, HD), f32)],
        compiler_params=_params(("arbitrary",)),
    )(x, w, target)


def _lru_gates(xc, vec, sp, wa, wx):
    xcb = _b(xc)
    r = _sig(_dot(xcb, wa) + vec[1:2])
    i = _sig(_dot(xcb, wx) + vec[2:3])
    la = (-LRU_C) * r * sp
    a = jnp.exp(la)
    mult = jnp.sqrt(-_expm1(2.0 * la))
    return r, i, a, mult, sp, xcb


def _lru_fwd(proj, cw, vec, wa, wx, name):
    s = proj.shape[0]
    sb = min(SEQ_BLOCK, s)
    nsb = s // sb
    t = min(LRU_T, sb)
    nc = sb // t
    hp = LRU_HP
    wide = hp * HD
    lanes = [slice(j * HD, (j + 1) * HD) for j in range(hp)]

    def body(x_ref, xh_ref, z_ref, cw_ref, vec_ref, wa_ref, wx_ref, y_ref, hl_ref, hc_ref):
        sblk = pl.program_id(1)
        consts = [(cw_ref[:, hl], vec_ref[:, hl], _softplus(-vec_ref[3:4, hl]), _b(wa_ref[j]), _b(wx_ref[j]))
                  for j, hl in enumerate(lanes)]
        row = lax.broadcasted_iota(jnp.int32, (t, HD), 0)

        @pl.when(sblk == 0)
        def _():
            hc_ref[...] = jnp.zeros((8, wide), f32)

        def one_head(j, x, halo, z, hcarry):
            cwv, vec_v, sp_v, wa_v, wx_v = consts[j]
            xs = _conv_taps(x, halo)
            xc = vec_v[0:1] + _conv(xs, cwv)
            r, i, a, mult, _, _ = _lru_gates(xc, vec_v, sp_v, wa_v, wx_v)
            yield
            bb = mult * (i * xc)
            d = 1
            while d < t:
                m = row >= d
                bb = jnp.where(m, a * pltpu.roll(bb, d, axis=0) + bb, bb)
                a = jnp.where(m, a * pltpu.roll(a, d, axis=0), a)
                d *= 2
                yield
            hl = bb + a * hcarry
            nrm = lax.rsqrt(_rowmean(hl * hl) + EPS)
            return hl, (hl * nrm * vec_v[4:5] * (z * _sig(z))).astype(bf16)

        def chunk(c, carries):
            t0 = pl.multiple_of(c * t, t)
            rows = pl.ds(t0, t)
            ins = [(x_ref[rows, hl], _rows_before(x_ref, xh_ref, hl, t0, c, sblk), z_ref[rows, hl]) for hl in lanes]
            outs = _round_robin([one_head(j, *ins[j], carries[j]) for j in range(hp)])
            for j, hl in enumerate(lanes):
                hl_ref[rows, hl] = outs[j][0]
                y_ref[rows, hl] = outs[j][1]
            return tuple(out[0][t - 1:t, :] for out in outs)

        final = lax.fori_loop(0, nc, chunk, tuple(hc_ref[0:1, hl] for hl in lanes))
        hc_ref[...] = jnp.concatenate([jnp.broadcast_to(f, (8, HD)) for f in final], axis=1)

    xsp, xhalo = _seq_specs(sb, wide, lambda g, i: i, lambda g, i: g)
    zsp, _ = _seq_specs(sb, wide, lambda g, i: i, lambda g, i: HEADS // hp + g)
    gcol = lambda g, i: (0, g)
    wsp = pl.BlockSpec((hp, HD, HD), lambda g, i: (g, 0, 0))
    osp = pl.BlockSpec((sb, wide), lambda g, i: (i, g))
    return pl.pallas_call(
        body, name=name, grid=(HEADS // hp, nsb),
        in_specs=[xsp, xhalo, zsp, pl.BlockSpec((4, wide), gcol), pl.BlockSpec((8, wide), gcol), wsp, wsp],
        out_specs=[osp, osp],
        out_shape=[jax.ShapeDtypeStruct((s, D_MIX), bf16), jax.ShapeDtypeStruct((s, D_MODEL), f32)],
        scratch_shapes=[pltpu.VMEM((8, wide), f32)],
        compiler_params=_params(("parallel", "arbitrary")),
    )(proj, proj, proj, cw, vec, wa, wx)


def _lru_bwd(dy, proj, hl, cw, vec, wa, wx, name):
    s = proj.shape[0]
    sb = min(SEQ_BLOCK, s)
    nsb = s // sb
    t = min(LRU_T, sb)
    nc = sb // t
    hp = LRU_HP
    wide = hp * HD
    lanes = [slice(j * HD, (j + 1) * HD) for j in range(hp)]
    n_acc = 10

    def body(dy_ref, x_ref, xh_ref, z_ref, hl_ref, hlh_ref, cw_ref, vec_ref, wa_ref, wx_ref,
             dx_ref, dz_ref, dcw_ref, dvec_ref, dwa_ref, dwx_ref, acc_ref, dxh_ref):
        step = pl.program_id(1)
        sblk = nsb - 1 - step
        consts = [(cw_ref[:, hl], vec_ref[:, hl], _softplus(-vec_ref[3:4, hl]), _b(wa_ref[j]), _b(wx_ref[j]))
                  for j, hl in enumerate(lanes)]
        row = lax.broadcasted_iota(jnp.int32, (t, HD), 0)

        @pl.when(step == 0)
        def _():
            acc_ref[...] = jnp.zeros((16, wide), f32)
            dxh_ref[...] = jnp.zeros((8, wide), f32)
            dwa_ref[...] = jnp.zeros((hp, HD, HD), f32)
            dwx_ref[...] = jnp.zeros((hp, HD, HD), f32)

        def one_head(j, x, halo, z, hv, hhalo, dyv, carry):
            dcw0, dcw1, dcw2, dcw3, dcb, dba, dbx, dsp, dlnw, mu, dxc_halo = carry
            cwv, vec_v, sp_v, wa_v, wx_v = consts[j]
            lnw = vec_v[4:5]
            xs = _conv_taps(x, halo)
            xc = vec_v[0:1] + _conv(xs, cwv)
            r, i, a, mult, sp, xcb = _lru_gates(xc, vec_v, sp_v, wa_v, wx_v)
            yield
            hprev = pltpu.roll(jnp.concatenate([hhalo, hv], axis=0), 1, axis=0)[8:]
            sgz = _sig(z)
            sz = z * sgz
            nrm = lax.rsqrt(_rowmean(hv * hv) + EPS)
            hn = hv * nrm
            dlnw = dlnw + _sum0(dyv * hn * sz)
            dzv = _b(dyv * hn * lnw * _dsilu(z, sgz))
            dhn = dyv * lnw * sz
            lam = nrm * (dhn - hn * _rowmean(dhn * hn))
            cf = jnp.where(row == t - 1, 1.0, pltpu.roll(a, t - 1, axis=0))
            d = 1
            while d < t:
                m = row < t - d
                lam = jnp.where(m, lam + cf * pltpu.roll(lam, t - d, axis=0), lam)
                cf = jnp.where(m, cf * pltpu.roll(cf, t - d, axis=0), cf)
                d *= 2
                yield
            lam = lam + cf * mu
            mu = a[0:1] * lam[0:1]
            da = lam * hprev
            dmult = lam * (i * xc)
            di = lam * mult * xc
            dxc = lam * mult * i
            dla = da * a - dmult * (a * a) / mult
            dr = dla * (-LRU_C * sp)
            dsp = dsp + _sum0(dla * (-LRU_C * r))
            dra = dr * r * (1.0 - r)
            dia = di * i * (1.0 - i)
            dba = dba + _sum0(dra)
            dbx = dbx + _sum0(dia)
            drab, diab = _b(dra), _b(dia)
            dwa = _dot_tn(xcb, drab)
            dwx = _dot_tn(xcb, diab)
            dxc = dxc + _dot_nt(drab, wa_v) + _dot_nt(diab, wx_v)
            yield
            dcb = dcb + _sum0(dxc)
            dcw0 = dcw0 + _sum0(dxc * xs[0])
            dcw1 = dcw1 + _sum0(dxc * xs[1])
            dcw2 = dcw2 + _sum0(dxc * xs[2])
            dcw3 = dcw3 + _sum0(dxc * xs[3])
            dxv = _b(_conv_back(dxc, dxc_halo, cwv))
            return dxv, dzv, dwa, dwx, (dcw0, dcw1, dcw2, dcw3, dcb, dba, dbx, dsp, dlnw, mu, dxc[0:8])

        def chunk(ci, carries):
            c = nc - 1 - ci
            t0 = pl.multiple_of(c * t, t)
            rows = pl.ds(t0, t)
            ins = [(x_ref[rows, hl], _rows_before(x_ref, xh_ref, hl, t0, c, sblk), z_ref[rows, hl], hl_ref[rows, hl],
                    _rows_before(hl_ref, hlh_ref, hl, t0, c, sblk), dy_ref[rows, hl]) for hl in lanes]
            outs = _round_robin([one_head(j, *ins[j], carries[j]) for j in range(hp)])
            for j, hl in enumerate(lanes):
                dx_ref[rows, hl] = outs[j][0]
                dz_ref[rows, hl] = outs[j][1]
                dwa_ref[j] += outs[j][2]
                dwx_ref[j] += outs[j][3]
            return tuple(out[4] for out in outs)

        acc = acc_ref[...]
        dxh = dxh_ref[...]
        init = tuple(tuple(acc[k:k + 1, hl] for k in range(n_acc)) + (dxh[:, hl],) for hl in lanes)
        final = lax.fori_loop(0, nc, chunk, init)
        rows_out = [jnp.concatenate([final[j][k] for j in range(hp)], axis=1) for k in range(n_acc)]
        zero = jnp.zeros((1, wide), f32)
        acc_ref[...] = jnp.concatenate(rows_out + [zero] * (16 - n_acc), axis=0)
        dxh_ref[...] = jnp.concatenate([final[j][n_acc] for j in range(hp)], axis=1)

        @pl.when(step == nsb - 1)
        def _():
            dcw_ref[...] = jnp.concatenate(rows_out[0:4], axis=0)
            dlam = -rows_out[7] * _sig(-vec_ref[3:4, :])
            dvec_ref[...] = jnp.concatenate([rows_out[4], rows_out[5], rows_out[6], dlam, rows_out[8], zero, zero, zero],
                                            axis=0)

    rblk = lambda g, i: nsb - 1 - i
    xsp, xhalo = _seq_specs(sb, wide, rblk, lambda g, i: g)
    zsp, _ = _seq_specs(sb, wide, rblk, lambda g, i: HEADS // hp + g)
    gcol = lambda g, i: (0, g)
    wsp = pl.BlockSpec((hp, HD, HD), lambda g, i: (g, 0, 0))
    return pl.pallas_call(
        body, name=name, grid=(HEADS // hp, nsb),
        in_specs=[xsp, xsp, xhalo, zsp, xsp, xhalo, pl.BlockSpec((4, wide), gcol), pl.BlockSpec((8, wide), gcol), wsp, wsp],
        out_specs=[xsp, xsp, pl.BlockSpec((4, wide), gcol), pl.BlockSpec((8, wide), gcol), wsp, wsp],
        out_shape=[jax.ShapeDtypeStruct((s, D_MODEL), bf16), jax.ShapeDtypeStruct((s, D_MODEL), bf16),
                   jax.ShapeDtypeStruct((4, D_MODEL), f32), jax.ShapeDtypeStruct((8, D_MODEL), f32),
                   jax.ShapeDtypeStruct((HEADS, HD, HD), f32), jax.ShapeDtypeStruct((HEADS, HD, HD), f32)],
        scratch_shapes=[pltpu.VMEM((16, wide), f32), pltpu.VMEM((8, wide), f32)],
        compiler_params=_params(("parallel", "arbitrary")),
    )(dy, proj, proj, proj, hl, hl, cw, vec, wa, wx)


def _lane_pick(x, lane, idx):
    return _sum1(jnp.where(lane == idx, x, 0.0))


def _round_robin(gens):
    results = [None] * len(gens)
    live = list(range(len(gens)))
    while live:
        for i in list(live):
            try:
                next(gens[i])
            except StopIteration as done:
                results[i] = done.value
                live.remove(i)
    return results


def _sdot(a, b):
    return _dot(_b(a), _b(b))


def _sdot_tn(a, b):
    return _dot_tn(_b(a), _b(b))


def _sdot_nt(a, b):
    return _dot_nt(_b(a), _b(b))


def _tri_dot(tri, x):
    trib = tri.astype(bf16)
    x1 = x.astype(bf16)
    r1 = x - x1.astype(f32)
    x2 = r1.astype(bf16)
    x3 = (r1 - x2.astype(f32)).astype(bf16)
    return _dot(trib, x1) + _dot(trib, x2) + _dot(trib, x3)


def _inv_unit_lower(a):
    n = -a
    p = _sdot(a, a)
    yield
    for k in range(5):
        n = n + p + _sdot(n, p)
        if k < 4:
            p = _sdot(p, p)
        yield
    return n


def _dn_gates(bav, dnv, masks):
    ltri = masks[4]
    ea = jnp.exp(pltpu.roll(dnv[0:1], HEADS, axis=1))
    dt = pltpu.roll(dnv[1:2], HEADS, axis=1)
    u = bav + dt
    g = -ea * _softplus(u)
    return dict(beta=_sig(bav), g=g, gc=_tri_dot(ltri, g), sga=_sig(u), ea=ea)


def _dn_chunk(x3, halo3, gates, cw3, h, masks, cpre=None, tinv=None):
    lane, ri, ci, eye, ltri = masks
    if cpre is None:
        cpre = _conv(_conv_taps(x3, halo3), cw3)
    sg = _sig(cpre)
    act = cpre * sg
    q_raw, k_raw, v = act[:, 0:HD], act[:, HD:2 * HD], act[:, 2 * HD:3 * HD]
    rq = lax.rsqrt(_sum1(q_raw * q_raw) + EPS)
    rk = lax.rsqrt(_sum1(k_raw * k_raw) + EPS)
    qn = q_raw * rq
    k = k_raw * rk
    q = qn * QSCALE
    beta = jnp.broadcast_to(_lane_pick(gates["beta"], lane, h), (CHUNK, HD))
    gc = jnp.broadcast_to(_lane_pick(gates["gc"], lane, HEADS + h), (CHUNK, HD))
    yield
    gc64 = gc[:, 0:CHUNK]
    grow = _sum0(gc64 * eye)
    dm = jnp.exp(jnp.where(ri >= ci, gc64 - grow, -1e30))
    ds_ = jnp.where(ri > ci, dm, 0.0)
    eg = jnp.exp(gc)
    glast = gc[CHUNK - 1:CHUNK, :]
    ek = jnp.exp(glast - gc)
    kb = k * beta
    kbf = _b(k)
    amat = _dot_nt(_b(kb), kbf) * ds_
    pmat = _dot_nt(_b(q), kbf) * dm
    yield
    if tinv is None:
        tinv = yield from _inv_unit_lower(amat)
    return dict(cpre=cpre, sg=sg, rq=rq, rk=rk, qn=qn, q=q, k=k, v=v, kbf=kbf, beta=beta, gc=gc, dm=dm, ds=ds_, eg=eg,
                glast=glast, ek=ek, kb=kb, amat=amat, tinv=tinv, pmat=pmat)


def _dn_masks():
    lane = lax.broadcasted_iota(jnp.int32, (CHUNK, HD), 1)
    ri = lax.broadcasted_iota(jnp.int32, (CHUNK, CHUNK), 0)
    ci = lax.broadcasted_iota(jnp.int32, (CHUNK, CHUNK), 1)
    eye = (ri == ci).astype(f32)
    ltri = (ri >= ci).astype(f32)
    return lane, ri, ci, eye, ltri


def _dn_load_qkv(q_ref, qh_ref, k_ref, kh_ref, v_ref, vh_ref, hl, rows, t0, c, sblk):
    x3 = jnp.concatenate([q_ref[rows, hl], k_ref[rows, hl], v_ref[rows, hl]], axis=1)
    halo3 = jnp.concatenate([_rows_before(q_ref, qh_ref, hl, t0, c, sblk), _rows_before(k_ref, kh_ref, hl, t0, c, sblk),
                             _rows_before(v_ref, vh_ref, hl, t0, c, sblk)], axis=1)
    return x3, halo3


def _dn_cw3(cwq_ref, cwk_ref, cwv_ref, j):
    return jnp.concatenate([r[:, j * HD:(j + 1) * HD] for r in (cwq_ref, cwk_ref, cwv_ref)], axis=1)


def _dn_fwd(proj, cw, dnv, y, name):
    s = proj.shape[0]
    sb = min(DN_SEQ_BLOCK, s)
    nsb = s // sb
    nc = sb // CHUNK
    hp = DN_HP

    def body(q_ref, qh_ref, k_ref, kh_ref, v_ref, vh_ref, z_ref, ba_ref, cwq_ref, cwk_ref, cwv_ref, dnv_ref, yin_ref,
             y_ref, o_ref, st_ref, ti_ref, cp_ref, s_ref):
        del yin_ref
        grp = pl.program_id(0)
        sblk = pl.program_id(1)
        masks = _dn_masks()
        dnv_v = dnv_ref[...]
        cw3 = [_dn_cw3(cwq_ref, cwk_ref, cwv_ref, j) for j in range(hp)]

        @pl.when(sblk == 0)
        def _():
            s_ref[...] = jnp.zeros((hp, HD, HD), f32)

        def one_head(j, x3, halo3, gates, z, st):
            f = yield from _dn_chunk(x3, halo3, gates, cw3[j], grp * hp + j, masks)
            sbf = _b(st)
            qs = _dot(_b(f["q"] * f["eg"]), sbf)
            rhs = f["beta"] * (f["v"] - _dot(_b(f["k"] * f["eg"]), sbf))
            yield
            vn = rhs + _sdot(f["tinv"], rhs)
            yield
            vnb = _b(vn)
            o = qs + _dot(_b(f["pmat"]), vnb)
            st_new = st * jnp.exp(f["glast"]) + _dot_tn(_b(f["k"] * f["ek"]), vnb)
            yield
            nrm = lax.rsqrt(_rowmean(o * o) + EPS)
            yv = (o * nrm * dnv_v[2:3] * (z * _sig(z))).astype(bf16)
            return o, yv, st_new, f["tinv"], f["cpre"]

        def chunk(c, states):
            t0 = pl.multiple_of(c * CHUNK, CHUNK)
            rows = pl.ds(t0, CHUNK)
            bav = ba_ref[rows, :]
            ins = []
            for j in range(hp):
                hl = slice(j * HD, (j + 1) * HD)
                ins.append(_dn_load_qkv(q_ref, qh_ref, k_ref, kh_ref, v_ref, vh_ref, hl, rows, t0, c, sblk)
                           + (z_ref[rows, hl],))
            gates = _dn_gates(bav, dnv_v, masks)
            outs = _round_robin([one_head(j, ins[j][0], ins[j][1], gates, ins[j][2], states[j]) for j in range(hp)])
            for j in range(hp):
                hl = slice(j * HD, (j + 1) * HD)
                st_ref[j, c] = states[j]
                o_ref[rows, hl] = outs[j][0]
                y_ref[rows, hl] = outs[j][1]
                ti_ref[j, c] = outs[j][3]
                cp_ref[rows, j * 3 * HD:(j + 1) * 3 * HD] = outs[j][4]
            return tuple(out[2] for out in outs)

        final = lax.fori_loop(0, nc, chunk, tuple(s_ref[j] for j in range(hp)))
        for j in range(hp):
            s_ref[j] = final[j]

    wide = hp * HD
    grp_specs = lambda off: _seq_specs(sb, wide, lambda g, i: i, lambda g, i: off // hp + g)
    (qsp, qhalo), (ksp, khalo), (vsp, vhalo) = grp_specs(2 * HEADS), grp_specs(3 * HEADS), grp_specs(4 * HEADS)
    zsp, _ = grp_specs(5 * HEADS)
    basp = pl.BlockSpec((sb, HD), lambda g, i: (i, BLK_BA))
    cws = lambda off: pl.BlockSpec((4, wide), lambda g, i: (0, off // hp + g))
    osp = lambda off: pl.BlockSpec((sb, wide), lambda g, i: (i, off // hp + g))
    return pl.pallas_call(
        body, name=name, grid=(HEADS // hp, nsb),
        in_specs=[qsp, qhalo, ksp, khalo, vsp, vhalo, zsp, basp, cws(0), cws(HEADS), cws(2 * HEADS),
                  pl.BlockSpec((8, HD), lambda g, i: (0, 0)), pl.BlockSpec(memory_space=pl.ANY)],
        out_specs=[osp(HEADS), osp(0), pl.BlockSpec((hp, nc, HD, HD), lambda g, i: (g, i, 0, 0)),
                   pl.BlockSpec((hp, nc, CHUNK, CHUNK), lambda g, i: (g, i, 0, 0)),
                   pl.BlockSpec((sb, 3 * wide), lambda g, i: (i, g))],
        out_shape=[jax.ShapeDtypeStruct((s, D_MIX), bf16), jax.ShapeDtypeStruct((s, D_MODEL), f32),
                   jax.ShapeDtypeStruct((HEADS, s // CHUNK, HD, HD), f32),
                   jax.ShapeDtypeStruct((HEADS, s // CHUNK, CHUNK, CHUNK), f32),
                   jax.ShapeDtypeStruct((s, 3 * D_MODEL), f32)],
        scratch_shapes=[pltpu.VMEM((hp, HD, HD), f32)],
        input_output_aliases={12: 0},
        compiler_params=_params(("parallel", "arbitrary")),
    )(proj, proj, proj, proj, proj, proj, proj, proj, cw, cw, cw, dnv, y)


def _dn_bwd(dy, proj, o, states, tinvs, cpres, cw, dnv, name):
    s = proj.shape[0]
    sb = min(DN_SEQ_BLOCK, s)
    nsb = s // sb
    nc = sb // CHUNK
    hp = DN_HP
    ngrp = HEADS // hp

    def body(dy_ref, q_ref, k_ref, v_ref, z_ref, ba_ref, o_ref, st_ref, ti_ref, cp_ref,
             cwq_ref, cwk_ref, cwv_ref, dnv_ref,
             dq_ref, dk_ref, dv_ref, dz_ref, dba_ref, dcw_ref, dsc_ref,
             ds_ref, dch_ref, cwacc_ref, nacc_ref):
        step = pl.program_id(0)
        grp = pl.program_id(1)
        sblk = nsb - 1 - step
        h0 = grp * hp
        masks = _dn_masks()
        lane, ri, ci, eye, ltri = masks
        utri = (ri <= ci).astype(f32)
        cw3s = [_dn_cw3(cwq_ref, cwk_ref, cwv_ref, j) for j in range(hp)]
        dnv_v = dnv_ref[...]
        dnw = dnv_v[2:3]
        row64 = lax.broadcasted_iota(jnp.int32, (CHUNK, HD), 0)

        @pl.when(step == 0)
        def _():
            for j in range(hp):
                ds_ref[h0 + j] = jnp.zeros((HD, HD), f32)
                dch_ref[h0 + j] = jnp.zeros((8, 3 * HD), f32)
                cwacc_ref[h0 + j] = jnp.zeros((8, 3 * HD), f32)

        @pl.when((step == 0) & (grp == 0))
        def _():
            nacc_ref[...] = jnp.zeros((8, HD), f32)

        def one_head(j, x3, gates, z, st, ti, cp, ov, dyv, carry):
            dst, dch, cwacc = carry
            cw3 = cw3s[j]
            f = yield from _dn_chunk(None, None, gates, cw3, h0 + j, masks, cp, ti)
            q, k, v, beta, eg, ek, kbf = f["q"], f["k"], f["v"], f["beta"], f["eg"], f["ek"], f["kbf"]
            sbf = _b(st)
            dstb = _b(dst)
            qe = q * eg
            keg = k * eg
            kd = k * ek
            kegb, qeb, kdb = _b(keg), _b(qe), _b(kd)
            e = v - _dot(kegb, sbf)
            yield
            rhs = beta * e
            vn = rhs + _sdot(f["tinv"], rhs)
            yield
            vnb = _b(vn)
            pb = _b(f["pmat"])
            sgz = _sig(z)
            sz = z * sgz
            nrm = lax.rsqrt(_rowmean(ov * ov) + EPS)
            on = ov * nrm
            ddnw = _sum0(dyv * on * sz)
            dpz = dyv * on * dnw * _dsilu(z, sgz)
            don = dyv * dnw * sz
            do = nrm * (don - on * _rowmean(don * on))
            dob = _b(do)
            dvn = _dot_tn(pb, dob) + _dot(kdb, dstb)
            dpm = jnp.where(ri >= ci, _dot_nt(dob, vnb), 0.0)
            dqe = _dot_nt(dob, sbf)
            dkd = _dot_nt(vnb, dstb)
            qtdo = _dot_tn(qeb, dob)
            yield
            drhs = dvn + _sdot_tn(f["tinv"], dvn)
            dqk = _b(dpm * f["dm"])
            dq = _dot(dqk, kbf) + dqe * eg
            dk_p = _dot_tn(dqk, _b(q))
            yield
            dam = jnp.where(ri > ci, -_sdot_nt(drhs, vn), 0.0)
            de = drhs * beta
            deb = _b(de)
            dbeta = _sum1(drhs * e)
            dkeg = -_dot_nt(deb, sbf)
            dst_new = qtdo + dst * jnp.exp(f["glast"]) - _dot_tn(kegb, deb)
            yield
            dkk = _b(dam * f["ds"])
            dkb = _dot(dkk, kbf)
            dk = dk_p + _dot_tn(dkk, _b(f["kb"])) + dkb * beta + dkeg * eg + dkd * ek
            yield
            dbeta = dbeta + _sum1(dkb * k)
            mm = dpm * f["pmat"] + dam * f["amat"]
            dglast = _sum1(_sum0(dst * st)) * jnp.exp(f["glast"]) + _sum1(_sum0(dkd * kd))
            dgc = (_sum1(mm) - _sum1(eye * _sum0(mm)) + _sum1(dqe * qe) + _sum1(dkeg * keg) - _sum1(dkd * kd))
            dgc = jnp.broadcast_to(dgc, (CHUNK, HD)) + jnp.where(row64 == CHUNK - 1, dglast, 0.0)
            dq_raw = (QSCALE * f["rq"]) * (dq - f["qn"] * _sum1(f["qn"] * dq))
            dk_raw = f["rk"] * (dk - k * _sum1(k * dk))
            pieces = []
            for p, dpart in enumerate((dq_raw, dk_raw, de)):
                ls = slice(p * HD, (p + 1) * HD)
                dc_p = dpart * _dsilu(f["cpre"][:, ls], f["sg"][:, ls])
                pieces.append(_conv_back(dc_p, dch[:, ls], cw3[:, ls], x3[:, ls]) + (dc_p[0:8],))
            dqkv, dcw_part, dc_head = (jnp.concatenate([pc[i] for pc in pieces], axis=1) for i in range(3))
            cwacc = cwacc + dcw_part
            return dpz, dqkv, (dgc, dbeta), ddnw, (dst_new, dc_head, cwacc)

        def chunk(cidx, carry):
            heads, nacc, sa0, sa1 = carry
            c = nc - 1 - cidx
            t0 = pl.multiple_of(c * CHUNK, CHUNK)
            rows = pl.ds(t0, CHUNK)
            bav = ba_ref[rows, :]
            ins = []
            for j in range(hp):
                hl = slice(j * HD, (j + 1) * HD)
                x3 = jnp.concatenate([q_ref[rows, hl], k_ref[rows, hl], v_ref[rows, hl]], axis=1)
                ins.append((x3, z_ref[rows, hl], st_ref[j, c], ti_ref[j, c],
                            cp_ref[rows, j * 3 * HD:(j + 1) * 3 * HD], o_ref[rows, hl], dy_ref[rows, hl]))
            gates = _dn_gates(bav, dnv_v, masks)
            outs = _round_robin([one_head(j, ins[j][0], gates, *ins[j][1:], heads[j]) for j in range(hp)])
            dgc_all = jnp.zeros((CHUNK, HD), f32)
            dbeta_all = jnp.zeros((CHUNK, HD), f32)
            for j in range(hp):
                hl = slice(j * HD, (j + 1) * HD)
                dpz, dqkv, (dgc, dbeta), ddnw, _ = outs[j]
                dgc_all = jnp.where(lane == HEADS + h0 + j, dgc, dgc_all)
                dbeta_all = jnp.where(lane == h0 + j, dbeta, dbeta_all)
                dq_ref[rows, hl] = _b(dqkv[:, 0:HD])
                dk_ref[rows, hl] = _b(dqkv[:, HD:2 * HD])
                dv_ref[rows, hl] = _b(dqkv[:, 2 * HD:3 * HD])
                dz_ref[rows, hl] = _b(dpz)
                nacc = nacc + ddnw
            dg_all = _tri_dot(utri, dgc_all)
            dain_all = dg_all * (-gates["ea"]) * gates["sga"]
            dba = dbeta_all * gates["beta"] * (1.0 - gates["beta"]) + dain_all
            sa0 = sa0 + _sum0(dg_all * gates["g"])
            sa1 = sa1 + _sum0(dain_all)

            @pl.when(grp == 0)
            def _():
                dba_ref[rows, :] = dba

            @pl.when(grp > 0)
            def _():
                dba_ref[rows, :] += dba

            return tuple(out[4] for out in outs), nacc, sa0, sa1

        init = tuple((ds_ref[h0 + j], dch_ref[h0 + j], cwacc_ref[h0 + j][0:4]) for j in range(hp))
        heads, nacc, sa0, sa1 = lax.fori_loop(
            0, nc, chunk, (init, nacc_ref[0:1, :], nacc_ref[1:2, :], nacc_ref[2:3, :]))
        nacc_ref[0:3, :] = jnp.concatenate([nacc, sa0, sa1], axis=0)
        zpad = jnp.zeros((4, 3 * HD), f32)
        for j in range(hp):
            dst, dch, cwacc = heads[j]
            ds_ref[h0 + j] = dst
            dch_ref[h0 + j] = dch
            cwacc_ref[h0 + j] = jnp.concatenate([cwacc, zpad], axis=0)

        @pl.when(step == nsb - 1)
        def _():
            for j in range(hp):
                dcw_ref[h0 + j] = heads[j][2]

            @pl.when(grp == ngrp - 1)
            def _():
                dsc_ref[...] = jnp.concatenate([pltpu.roll(sa0, HD - HEADS, axis=1), pltpu.roll(sa1, HD - HEADS, axis=1),
                                                nacc, jnp.zeros((5, HD), f32)], axis=0)

    wide = hp * HD
    rblk = lambda i, g: nsb - 1 - i
    grp_specs = lambda off: _seq_specs(sb, wide, rblk, lambda i, g: off // hp + g)
    (qsp, qhalo), (ksp, khalo), (vsp, vhalo) = grp_specs(2 * HEADS), grp_specs(3 * HEADS), grp_specs(4 * HEADS)
    zsp, _ = grp_specs(5 * HEADS)
    hsp = lambda off: pl.BlockSpec((sb, wide), lambda i, g: (rblk(i, g), off // hp + g))
    basp = lambda col: pl.BlockSpec((sb, HD), lambda i, g: (rblk(i, g), col))
    cws = lambda off: pl.BlockSpec((4, wide), lambda i, g: (0, off // hp + g))
    whole = lambda shape: pl.BlockSpec(shape, lambda i, g: (0,) * len(shape))
    return pl.pallas_call(
        body, name=name, grid=(nsb, ngrp),
        in_specs=[hsp(HEADS), qsp, ksp, vsp, zsp, basp(BLK_BA), hsp(0),
                  pl.BlockSpec((hp, nc, HD, HD), lambda i, g: (g, rblk(i, g), 0, 0)),
                  pl.BlockSpec((hp, nc, CHUNK, CHUNK), lambda i, g: (g, rblk(i, g), 0, 0)),
                  pl.BlockSpec((sb, 3 * wide), lambda i, g: (rblk(i, g), g)),
                  cws(0), cws(HEADS), cws(2 * HEADS), whole((8, HD))],
        out_specs=[hsp(0), hsp(0), hsp(0), hsp(0), basp(0), whole((HEADS, 4, 3 * HD)), whole((8, HD))],
        out_shape=[jax.ShapeDtypeStruct((s, D_MODEL), bf16)] * 4
        + [jax.ShapeDtypeStruct((s, HD), f32), jax.ShapeDtypeStruct((HEADS, 4, 3 * HD), f32),
           jax.ShapeDtypeStruct((8, HD), f32)],
        scratch_shapes=[pltpu.VMEM((HEADS, HD, HD), f32), pltpu.VMEM((HEADS, 8, 3 * HD), f32),
                        pltpu.VMEM((HEADS, 8, 3 * HD), f32), pltpu.VMEM((8, HD), f32)],
        compiler_params=_params(("arbitrary", "arbitrary")),
    )(dy, proj, proj, proj, proj, proj, o, states, tinvs, cpres, cw, cw, cw, dnv)


ANY = pl.BlockSpec(memory_space=pl.ANY)
CHIP_MASKS = ((1, 0, 0), (0, 1, 0), (1, 1, 0))
ALL_MASKS = tuple((m >> 2 & 1, m >> 1 & 1, m & 1) for m in range(1, N_DEV))


def _me():
    return lax.axis_index("x"), lax.axis_index("y"), lax.axis_index("c")


def _flip(me, mask):
    return tuple((1 - v) if m else v for v, m in zip(me, mask))


def _dev_index(dev):
    return 4 * dev[0] + 2 * dev[1] + dev[2]


def _chip(dev):
    return 2 * dev[0] + dev[1]


def _comm_call(name, body, arrays, out_shapes, nsem, nloc=0):
    scratch = [pltpu.SemaphoreType.DMA((nsem,)), pltpu.SemaphoreType.DMA((nsem,))]
    if nloc:
        scratch.append(pltpu.SemaphoreType.DMA((nloc,)))
    return pl.pallas_call(
        body, name=name, in_specs=[ANY] * len(arrays), out_specs=[ANY] * len(out_shapes), out_shape=out_shapes,
        scratch_shapes=scratch, compiler_params=pltpu.CompilerParams(has_side_effects=True),
    )(*arrays)


def _put_own(gathered, own, slot):
    return lax.dynamic_update_index_in_dim(gathered, own, slot, 0)


def _gather_halves(arrays, whole_arrays, name):
    na, nw = len(arrays), len(whole_arrays)
    nm = len(CHIP_MASKS)

    def body(*refs):
        srcs, dsts = refs[:na + nw], refs[na + nw:2 * (na + nw)]
        send_sems, recv_sems = refs[2 * (na + nw):]
        me = _me()
        sib = _flip(me, (0, 0, 1))

        def half(a, ref, core):
            rows = arrays[a].shape[0] // 2
            return ref.at[pl.ds(pl.multiple_of(core * rows, rows), rows)]

        def over_ici(a, mi, sender, to):
            if a < na:
                src, dst = half(a, srcs[a], sender[2]), half(a, dsts[a].at[_chip(sender)], sender[2])
            else:
                src, dst = srcs[a], dsts[a].at[_chip(sender)]
            return pltpu.make_async_remote_copy(src_ref=src, dst_ref=dst, send_sem=send_sems.at[a * nm + mi],
                                                recv_sem=recv_sems.at[a * nm + mi], device_id=to, device_id_type=MESH)

        def over_d2d(a, mi, origin, sender, to):
            k = (na + nw) * nm + a * nm + mi
            blk = half(a, dsts[a].at[_chip(origin)], sender[2])
            return pltpu.make_async_remote_copy(src_ref=blk, dst_ref=blk, send_sem=send_sems.at[k],
                                                recv_sem=recv_sems.at[k], device_id=to, device_id_type=MESH)

        sends = []
        for a in range(na + nw):
            for mi, mask in enumerate(CHIP_MASKS):
                cp = over_ici(a, mi, me, _flip(me, mask))
                cp.start()
                sends.append(cp)
        for a in range(na + nw):
            for mi, mask in enumerate(CHIP_MASKS):
                peer = _flip(me, mask)
                over_ici(a, mi, peer, me).wait_recv()
                if a < na:
                    cp = over_d2d(a, mi, peer, me, sib)
                    cp.start()
                    sends.append(cp)
        for a in range(na):
            for mi, mask in enumerate(CHIP_MASKS):
                over_d2d(a, mi, _flip(sib, mask), sib, me).wait_recv()
        for cp in sends:
            cp.wait_send()

    every = list(arrays) + list(whole_arrays)
    got = _comm_call(name, body, every, [jax.ShapeDtypeStruct((4,) + t.shape, t.dtype) for t in every],
                     (2 * na + nw) * nm)
    chip = 2 * lax.axis_index("x") + lax.axis_index("y")
    return [_put_own(g, t, chip) for g, t in zip(got, every)]


HBM = pl.BlockSpec(memory_space=pltpu.HBM)
SEM = pl.BlockSpec(memory_space=pltpu.SEMAPHORE)
DATAFLOW = pltpu.SideEffectType.DATAFLOW_SIDE_EFFECTING


def _split_start(name, srcs, land_shapes, nsem, issue, after=None):
    ns, nl = len(srcs), len(land_shapes)
    n_in = ns + nl + (after is not None)

    def body(*refs):
        issue(refs[:ns], refs[ns:ns + nl], refs[n_in + ns + nl], refs[n_in + ns + nl + 1])
        token = refs[-1]
        token[...] = jnp.zeros_like(token)

    lands = [lax.empty(t.shape, t.dtype) for t in land_shapes]
    thru = [pltpu.HBM(t.shape, t.dtype) for t in list(srcs) + list(land_shapes)]
    hbm = lambda t: pltpu.with_memory_space_constraint(t, pltpu.HBM)
    return pl.pallas_call(
        body, name=name, in_specs=[HBM] * (ns + nl) + [ANY] * (after is not None),
        out_shape=(*thru, pltpu.SemaphoreType.DMA((nsem,)), pltpu.SemaphoreType.DMA((nsem,)),
                   jax.ShapeDtypeStruct((8, HD), f32)),
        out_specs=(*[HBM] * (ns + nl), SEM, SEM, pl.BlockSpec(memory_space=pltpu.VMEM)),
        input_output_aliases={i: i for i in range(ns + nl)},
        compiler_params=pltpu.CompilerParams(has_side_effects=DATAFLOW),
    )(*[hbm(t) for t in srcs], *[hbm(t) for t in lands], *([after] if after is not None else []))


def _split_wait(name, started, ns, after, finish):
    thru, send_sems, recv_sems = started[:-3], started[-3], started[-2]
    nt = len(thru)

    def body(*refs):
        finish(refs[:ns], refs[ns:nt], refs[nt], refs[nt + 1])

    outs = pl.pallas_call(
        body, name=name, in_specs=[HBM] * nt + [SEM, SEM, ANY],
        out_shape=tuple(pltpu.HBM(t.shape, t.dtype) for t in thru), out_specs=tuple([HBM] * nt),
        input_output_aliases={i: i for i in range(nt)},
        compiler_params=pltpu.CompilerParams(has_side_effects=DATAFLOW),
    )(*thru, send_sems, recv_sems, after)
    return list(outs[ns:])


def _gather_start(arrays, name, after=None):
    nm = len(CHIP_MASKS)

    def issue(srcs, lands, send_sems, recv_sems):
        me = _me()
        for a in range(len(arrays)):
            for mi, mask in enumerate(CHIP_MASKS):
                pltpu.make_async_remote_copy(
                    src_ref=srcs[a], dst_ref=lands[a].at[_chip(me)], send_sem=send_sems.at[a * nm + mi],
                    recv_sem=recv_sems.at[a * nm + mi], device_id=_flip(me, mask), device_id_type=MESH).start()

    shapes = [jax.ShapeDtypeStruct((4,) + t.shape, t.dtype) for t in arrays]
    return _split_start(name, arrays, shapes, len(arrays) * nm, issue, after)


def _gather_finish(started, arrays, after, name):
    nm = len(CHIP_MASKS)

    def finish(srcs, lands, send_sems, recv_sems):
        me = _me()
        for a in range(len(arrays)):
            for mi, mask in enumerate(CHIP_MASKS):
                peer = _flip(me, mask)
                cp = pltpu.make_async_remote_copy(
                    src_ref=srcs[a], dst_ref=lands[a].at[_chip(peer)], send_sem=send_sems.at[a * nm + mi],
                    recv_sem=recv_sems.at[a * nm + mi], device_id=peer, device_id_type=MESH)
                cp.wait_send()
                cp.wait_recv()

    got = _split_wait(name, started, len(arrays), after, finish)
    chip = 2 * lax.axis_index("x") + lax.axis_index("y")
    return [_put_own(g, t, chip) for g, t in zip(got, arrays)]


HALF_IN, HALF_OUT = D_MODEL // 2, D_MIX // 8


def _scatter_piece(kind, ref, d):
    j, r = d >> 1, d & 1
    if kind == "win":
        return ref.at[pl.ds(HALF_IN * r, HALF_IN), pl.ds(WIN_STRIDE * j, WIN_W)]
    if kind == "wout":
        return ref.at[pl.ds(2 * HALF_OUT * j + HALF_OUT * r, HALF_OUT)]
    return ref.at[d]


def _scatter_start(arrays, kinds, name):
    na = len(arrays)

    def issue(srcs, lands, send_sems, recv_sems):
        me = _me()
        my = _dev_index(me)
        for d in range(N_DEV):
            dev = (d >> 2, (d >> 1) & 1, d & 1)
            for a in range(na):
                @pl.when(my != d)
                def _(a=a, d=d, dev=dev):
                    pltpu.make_async_remote_copy(
                        src_ref=_scatter_piece(kinds[a], srcs[a], d), dst_ref=lands[a].at[my],
                        send_sem=send_sems.at[a * N_DEV + d], recv_sem=recv_sems.at[a * N_DEV + my],
                        device_id=dev, device_id_type=MESH).start()

    shape_of = {"win": (N_DEV, HALF_IN, WIN_W), "wout": (N_DEV, HALF_OUT, D_MODEL)}
    shapes = [jax.ShapeDtypeStruct(shape_of.get(k, t.shape), t.dtype) for t, k in zip(arrays, kinds)]
    return _split_start(name, arrays, shapes, na * N_DEV, issue)


def _scatter_finish(started, arrays, kinds, after, name):
    na = len(arrays)

    def finish(srcs, lands, send_sems, recv_sems):
        me = _me()
        my = _dev_index(me)
        for s in range(N_DEV):
            for a in range(na):
                @pl.when(my != s)
                def _(a=a, s=s):
                    cp = pltpu.make_async_remote_copy(
                        src_ref=_scatter_piece(kinds[a], srcs[a], s), dst_ref=lands[a].at[s],
                        send_sem=send_sems.at[a * N_DEV + s], recv_sem=recv_sems.at[a * N_DEV + s],
                        device_id=me, device_id_type=MESH)
                    cp.wait_recv()
                    cp.wait_send()

    got = _split_wait(name, started, na, after, finish)
    x, y, c = _me()
    chip, my = 2 * x + y, 4 * x + 2 * y + c
    own = {"win": lambda t: lax.dynamic_slice(t, (HALF_IN * c, WIN_STRIDE * chip), (HALF_IN, WIN_W)),
           "wout": lambda t: lax.dynamic_slice(t, (2 * HALF_OUT * chip + HALF_OUT * c, 0), (HALF_OUT, D_MODEL)),
           "small": lambda t: lax.dynamic_index_in_dim(t, my, 0, keepdims=False)}
    return [_put_own(g, own[k](t), my) for g, t, k in zip(got, arrays, kinds)]


def _share_reduced(pair_arrays, small, name):
    arrays = list(pair_arrays) + ([small] if small is not None else [])
    na, npair = len(arrays), len(pair_arrays)
    nm = len(ALL_MASKS)

    def body(*refs):
        srcs, dsts = refs[:na], refs[na:2 * na]
        send_sems, recv_sems = refs[2 * na:]
        me = _me()
        sib = _flip(me, (0, 0, 1))

        def copy(a, k, sender, to):
            slot = sender[2] if a < npair else _dev_index(sender)
            return pltpu.make_async_remote_copy(
                src_ref=srcs[a], dst_ref=dsts[a].at[slot], send_sem=send_sems.at[k], recv_sem=recv_sems.at[k],
                device_id=to, device_id_type=MESH)

        sends = [copy(a, a, me, sib) for a in range(npair)]
        if small is not None:
            sends += [copy(npair, npair + mi, me, _flip(me, mask)) for mi, mask in enumerate(ALL_MASKS)]
        for cp in sends:
            cp.start()
        for a in range(npair):
            copy(a, a, sib, me).wait_recv()
        if small is not None:
            for mi, mask in enumerate(ALL_MASKS):
                copy(npair, npair + mi, _flip(me, mask), me).wait_recv()
        for cp in sends:
            cp.wait_send()

    shapes = [jax.ShapeDtypeStruct((2,) + t.shape, t.dtype) for t in pair_arrays]
    if small is not None:
        shapes.append(jax.ShapeDtypeStruct((N_DEV,) + small.shape, small.dtype))
    got = _comm_call(name, body, arrays, shapes, npair + (nm if small is not None else 0))
    x, y, c = _me()
    slots = [c] * npair + [4 * x + 2 * y + c]
    return [_put_own(g, t, slot) for g, t, slot in zip(got, arrays, slots)]


def _sum_parts(parts, name):
    _, r, c = parts.shape
    tr = r
    while tr * c * 4 * N_DEV > (8 << 20) and tr % 32 == 0:
        tr //= 2

    def body(p_ref, o_ref):
        total = p_ref[0].astype(f32)
        for d in range(1, N_DEV):
            total = total + p_ref[d].astype(f32)
        o_ref[...] = total

    return pl.pallas_call(
        body, name=name, grid=(r // tr,), in_specs=[pl.BlockSpec((N_DEV, tr, c), lambda i: (0, i, 0))],
        out_specs=pl.BlockSpec((tr, c), lambda i: (i, 0)), out_shape=jax.ShapeDtypeStruct((r, c), f32),
        compiler_params=_params(("parallel",)),
    )(parts)


def _adamw(w, g, m, v, name):
    shape = w.shape
    cols = shape[-1]
    slabs = w.ndim == 3 and shape[1] < 8
    rows = w.size // cols
    tr = rows
    while tr * cols * 4 > (1 << 20) and tr % 16 == 0:
        tr //= 2
    c1 = 1.0 / (1.0 - ADAM_B1 ** ADAM_STEP)
    c2 = 1.0 / (1.0 - ADAM_B2 ** ADAM_STEP)

    def body(w_ref, g_ref, m_ref, v_ref, d_ref, nm_ref, nv_ref, *g_out):
        gv = g_ref[...]
        mv = ADAM_B1 * m_ref[...] + (1.0 - ADAM_B1) * gv
        vv = ADAM_B2 * v_ref[...] + (1.0 - ADAM_B2) * (gv * gv)
        nm_ref[...] = mv
        nv_ref[...] = vv
        d_ref[...] = -ADAM_LR * ((mv * c1) / (jnp.sqrt(vv * c2) + ADAM_EPS) + ADAM_WD * w_ref[...])
        for ref in g_out:
            ref[...] = gv

    if slabs:
        tl = max(d for d in range(1, shape[0] + 1) if shape[0] % d == 0 and d * shape[1] * cols * 4 <= (3 << 19))
        spec = pl.BlockSpec((tl, shape[1], cols), lambda i: (i, 0, 0))
        return tuple(pl.pallas_call(
            body, name=name, grid=(shape[0] // tl,), in_specs=[spec] * 4, out_specs=[spec] * 4,
            out_shape=[jax.ShapeDtypeStruct(shape, f32)] * 4, compiler_params=_params(("parallel",)),
        )(w, g, m, v))
    spec = pl.BlockSpec((tr, cols), lambda i: (i, 0))
    outs = pl.pallas_call(
        body, name=name, grid=(rows // tr,), in_specs=[spec] * 4, out_specs=[spec] * 3,
        out_shape=[jax.ShapeDtypeStruct((rows, cols), f32)] * 3, compiler_params=_params(("parallel",)),
    )(*[t.reshape(rows, cols) for t in (w, g, m, v)])
    return tuple(t.reshape(shape) for t in outs)


def _pad_lanes(t, n=HD):
    return jnp.pad(t, [(0, 0)] * (t.ndim - 1) + [(0, n - t.shape[-1])])


def _rows128(t, rows=None):
    t = t.reshape(-1, HD)
    rows = rows or -(-t.shape[0] // 8) * 8
    return jnp.pad(t, ((0, rows - t.shape[0]), (0, 0)))


def kernel(x, norm_w, w_in, lru_conv_w, lru_conv_b, lru_wa, lru_ba, lru_wx, lru_bx, lru_lambda, lru_norm_w, dn_conv_w, dn_A_log, dn_dt_bias, dn_norm_w, w_out, final_norm_w, loss_target, m_norm_w, m_w_in, m_lru_conv_w, m_lru_conv_b, m_lru_wa, m_lru_ba, m_lru_wx, m_lru_bx, m_lru_lambda, m_lru_norm_w, m_dn_conv_w, m_dn_A_log, m_dn_dt_bias, m_dn_norm_w, m_w_out, m_final_norm_w, v_norm_w, v_w_in, v_lru_conv_w, v_lru_conv_b, v_lru_wa, v_lru_ba, v_lru_wx, v_lru_bx, v_lru_lambda, v_lru_norm_w, v_dn_conv_w, v_dn_A_log, v_dn_dt_bias, v_dn_norm_w, v_w_out, v_final_norm_w):
    depth = norm_w.shape[0]
    xs = x[0]
    target = loss_target[0]
    chip = 2 * lax.axis_index("x") + lax.axis_index("y")

    win_b = [w_in[l].astype(bf16) for l in range(depth)]
    wout_b = [w_out[l].astype(bf16) for l in range(depth)]
    got_in0 = _gather_halves([win_b[0]], [], "gather_weights0")[0]
    rest0 = [wout_b[0], lru_conv_w, dn_conv_w]
    rest0_started = _gather_start(rest0, "gather_rest0_start")
    gathered, later = {}, {}
    pad_cols = jnp.zeros((D_MODEL, D_INP - D_IN), bf16)

    def weights_of(l, after):
        if l == 0:
            got, token = got_in0, rest0_started[-1]
        else:
            gathered[l] = _gather_finish(later[l], [win_b[l], wout_b[l]], after, f"gather_weights{l}_wait")
            got, token = gathered[l][0], None
        return jnp.concatenate([got[j] for j in range(4)] + [pad_cols], axis=1), token

    def rest_of(l, after):
        token = None
        if "rest0" not in gathered:
            gathered["rest0"] = _gather_finish(rest0_started, rest0, after, "gather_rest0_wait")
            for k in range(1, depth):
                later[k] = _gather_start([win_b[k], wout_b[k]], f"gather_weights{k}_start", after=gathered["rest0"][0])
                token = later[k][-1] if token is None else token + later[k][-1]
        g_out, g_lcw, g_dcw = gathered["rest0"]
        if l > 0:
            g_out = gathered[l][1]
        lcw_full = g_lcw.transpose(1, 2, 0, 3).reshape(depth, 4, D_MODEL)
        dcw_full = g_dcw.transpose(1, 2, 0, 3).reshape(depth, 4, 3 * D_MODEL)
        return g_out.reshape(D_MIX, D_MODEL), _behind(lcw_full[l], token), dcw_full[l]

    in_flight = {}

    def on_grad(kind, l, g):
        if l == depth - 1 and depth > 1:
            if kind == "wout":
                in_flight["held"] = g
                return None
            srcs, kinds, key = [g, in_flight.pop("held")], ["win", "wout"], ("both", l)
        else:
            srcs, kinds, key = [g], [kind], (kind, l)
        started = _scatter_start(srcs, kinds, f"scatter_{key[0]}{key[1]}_start")
        in_flight[key] = (started, srcs, kinds)
        return started[-1]

    vecs, dnvs = [], []
    zrow = jnp.zeros((D_MODEL,), f32)
    for l in range(depth):
        vecs.append(jnp.stack([lru_conv_b[l], lru_ba[l], lru_bx[l], lru_lambda[l], lru_norm_w[l], zrow, zrow, zrow]))
        dnvs.append(jnp.concatenate([_pad_lanes(dn_A_log[l][None]), _pad_lanes(dn_dt_bias[l][None]),
                                     dn_norm_w[l][None], jnp.zeros((5, HD), f32)], axis=0))

    loss_part, grad_x, d_fnw, g_nw, g_vec, g_wa, g_wx, g_lcw_, g_dcw_, g_dsc = _local_step(
        xs, target, norm_w, final_norm_w, weights_of, rest_of, on_grad, vecs, dnvs, lru_wa, lru_wx)
    loss = lax.psum(loss_part[0, 0], ("x", "y", "c"))
    grads = _reduce_grads(chip, grad_x, in_flight, d_fnw, g_nw, g_vec, g_wa, g_wx, g_lcw_, g_dcw_, g_dsc)

    names = ["norm_w", "w_in", "lru_conv_w", "lru_conv_b", "lru_wa", "lru_ba", "lru_wx", "lru_bx", "lru_lambda",
             "lru_norm_w", "dn_conv_w", "dn_A_log", "dn_dt_bias", "dn_norm_w", "w_out", "final_norm_w"]
    ws = dict(zip(names, [norm_w, w_in, lru_conv_w, lru_conv_b, lru_wa, lru_ba, lru_wx, lru_bx, lru_lambda, lru_norm_w,
                          dn_conv_w, dn_A_log, dn_dt_bias, dn_norm_w, w_out, final_norm_w]))
    ms = dict(zip(names, [m_norm_w, m_w_in, m_lru_conv_w, m_lru_conv_b, m_lru_wa, m_lru_ba, m_lru_wx, m_lru_bx,
                          m_lru_lambda, m_lru_norm_w, m_dn_conv_w, m_dn_A_log, m_dn_dt_bias, m_dn_norm_w, m_w_out,
                          m_final_norm_w]))
    vs = dict(zip(names, [v_norm_w, v_w_in, v_lru_conv_w, v_lru_conv_b, v_lru_wa, v_lru_ba, v_lru_wx, v_lru_bx,
                          v_lru_lambda, v_lru_norm_w, v_dn_conv_w, v_dn_A_log, v_dn_dt_bias, v_dn_norm_w, v_w_out,
                          v_final_norm_w]))
    to_cols = lambda t: jnp.transpose(t, (2, 0, 1))
    from_cols = lambda t: jnp.transpose(t, (1, 2, 0))
    deltas, new_m, new_v = [], [], []
    for n in names:
        if n in ("w_in", "lru_conv_w", "dn_conv_w"):
            view, back = (to_cols, from_cols) if n == "w_in" else (lambda t: t, lambda t: t)
            d, nm_, nv_, grads[n] = (back(t) for t in _adamw(view(ws[n]), view(grads[n]), view(ms[n]), view(vs[n]),
                                                            f"adamw_{n}"))
        else:
            d, nm_, nv_ = _adamw(ws[n], grads[n], ms[n], vs[n], f"adamw_{n}")
        deltas.append(d)
        new_m.append(nm_)
        new_v.append(nv_)
    return (loss, grad_x[None], *[grads[n] for n in names], *deltas, *new_m, *new_v)


def _behind(t, token):
    return t if token is None else t + token[0:1, 0:1]


def _local_step(xs, target, norm_w, final_norm_w, weights_of, rest_of, on_grad, vecs, dnvs, lru_wa, lru_wx):
    depth = norm_w.shape[0]
    saved = []
    h = xs
    for l in range(depth):
        wp_l, token = weights_of(l, h)
        hn, hnt = _rms_fwd(h, _behind(norm_w[l][None], token), f"rms_fwd{l}")
        proj = _matmul(hn, wp_l, tm=2048, tn=896, tk=D_MODEL, name=f"in_proj{l}")
        wo_l, lcw_l, dcw_l = rest_of(l, proj)
        y, hl = _lru_fwd(proj, lcw_l, vecs[l], lru_wa[l], lru_wx[l], f"lru_fwd{l}")
        y, o, states, tinvs, cpres = _dn_fwd(proj, dcw_l, dnvs[l], y, f"dn_fwd{l}")
        out = _matmul(y, wo_l, tm=1024, tn=D_MODEL, tk=D_MIX, res=h, name=f"out_proj{l}")
        saved.append((h, hnt, proj, hl, o, states, tinvs, cpres, y, wp_l, wo_l, lcw_l, dcw_l))
        h = out

    dh, d_fnw, loss_part = _final_loss(h, final_norm_w[None], target, "final_loss")

    g_nw, g_vec, g_wa, g_wx, g_lcw_, g_dcw_, g_dsc = ([None] * depth for _ in range(7))
    for l in reversed(range(depth)):
        hin, hnt, proj, hl, o, states, tinvs, cpres, y, wp_l, wo_l, lcw_l, dcw_l = saved[l]
        dy = _matmul(dh, wo_l, tb=True, tm=2048, tn=1024, tk=D_MODEL, name=f"d_y{l}")
        g_wout = _matmul(y, dh, ta=True, tm=1024, tn=D_MODEL, tk=512, out_dtype=bf16, name=f"d_wout{l}")
        token = on_grad("wout", l, g_wout)
        dlx, dlz, g_lcw_[l], g_vec[l], g_wa[l], g_wx[l] = _lru_bwd(dy, proj, hl, lcw_l, _behind(vecs[l], token),
                                                                 lru_wa[l], lru_wx[l], f"lru_bwd{l}")
        dq, dk, dv, dz, dba, dcw8, g_dsc[l] = _dn_bwd(dy, proj, o, states, tinvs, cpres, dcw_l, dnvs[l], f"dn_bwd{l}")
        g_dcw_[l] = dcw8.reshape(HEADS, 4, 3, HD).transpose(1, 2, 0, 3).reshape(4, 3 * D_MODEL)
        dxs = [dlx, dlz, dq, dk, dv, dz]
        g_win = _d_win(hnt, dxs, dba, f"d_win{l}")
        token = on_grad("win", l, g_win)
        dh, g_nw[l] = _d_hn(dxs, _behind(dba, token), wp_l, hin, norm_w[l][None], dh, f"d_hn{l}")
    return loss_part, dh, d_fnw, g_nw, g_vec, g_wa, g_wx, g_lcw_, g_dcw_, g_dsc


def _reduce_grads(chip, after, in_flight, d_fnw, g_nw, g_vec, g_wa, g_wx, g_lcw_, g_dcw_, g_dsc):
    depth = len(g_nw)
    both = lambda f: jnp.concatenate([f(l) for l in range(depth)])
    small = [
        both(lambda l: _rows128(g_nw[l])),
        both(lambda l: _rows128(g_vec[l][0])), both(lambda l: _rows128(g_vec[l][1])),
        both(lambda l: _rows128(g_vec[l][2])), both(lambda l: _rows128(g_vec[l][3])),
        both(lambda l: _rows128(g_vec[l][4])),
        both(lambda l: _rows128(g_wa[l])), both(lambda l: _rows128(g_wx[l])),
        both(lambda l: _rows128(g_dsc[l][0:1])), both(lambda l: _rows128(g_dsc[l][1:2])),
        both(lambda l: _rows128(g_dsc[l][2:3])),
        _rows128(d_fnw),
        both(lambda l: _rows128(g_lcw_[l])), both(lambda l: _rows128(g_dcw_[l])),
    ]
    counts = [t.shape[0] for t in small]
    total = sum(counts)
    per = -(-total // (8 * N_DEV)) * 8
    small = jnp.concatenate(small + [jnp.zeros((per * N_DEV - total, HD), f32)]).reshape(N_DEV, per, HD)

    small_started = _scatter_start([small], ["small"], "scatter_small_start")
    reduced, pairs = {}, [None] * depth
    for key in sorted(in_flight, key=lambda k: -k[1]):
        started, srcs, kinds = in_flight[key]
        got = _scatter_finish(started, srcs, kinds, after, f"scatter_{key[0]}{key[1]}_wait")
        for kind, part in zip(kinds, got):
            reduced[(kind, key[1])] = _sum_parts(part, f"sum_{kind}{key[1]}")
        after = reduced[(kinds[-1], key[1])]
        if key[1] > 0 and ("win", key[1]) in reduced and ("wout", key[1]) in reduced:
            pairs[key[1]] = _share_reduced([reduced[("win", key[1])], reduced[("wout", key[1])]], None,
                                           f"share_grads{key[1]}")
            after = pairs[key[1]][0]
    got = _scatter_finish(small_started, [small], ["small"], after, "scatter_small_wait")
    shared = _share_reduced([reduced[("win", 0)], reduced[("wout", 0)]], _sum_parts(got[0], "sum_small"), "share_grads0")
    pairs[0] = shared
    a_small = shared[2].reshape(N_DEV * per, HD)
    grad_w_in = jnp.stack([lax.dynamic_slice_in_dim(pairs[l][0].reshape(D_MODEL, WIN_W), 4 * chip, SHARD_IN, 1)
                           for l in range(depth)])
    grad_w_out = jnp.stack([pairs[l][1].reshape(D_MIX // 4, D_MODEL) for l in range(depth)])

    offs = [0]
    for c in counts:
        offs.append(offs[-1] + c)

    def piece(k, rows_per_layer=None):
        t = a_small[offs[k]:offs[k + 1]]
        if rows_per_layer is None:
            return t
        return t.reshape(depth, -1, HD)[:, :rows_per_layer]

    vec = lambda k: piece(k).reshape(depth, D_MODEL)
    return {
        "norm_w": vec(0), "lru_conv_b": vec(1), "lru_ba": vec(2), "lru_bx": vec(3), "lru_lambda": vec(4),
        "lru_norm_w": vec(5),
        "lru_wa": piece(6).reshape(depth, HEADS, HD, HD), "lru_wx": piece(7).reshape(depth, HEADS, HD, HD),
        "dn_A_log": piece(8, 1)[:, 0, :HEADS], "dn_dt_bias": piece(9, 1)[:, 0, :HEADS], "dn_norm_w": piece(10, 1)[:, 0],
        "final_norm_w": piece(11).reshape(D_MODEL),
        "lru_conv_w": lax.dynamic_slice_in_dim(piece(12).reshape(depth, 4, D_MODEL), chip * (D_MODEL // 4), D_MODEL // 4, 2),
        "dn_conv_w": lax.dynamic_slice_in_dim(piece(13).reshape(depth, 4, 3 * D_MODEL), chip * (3 * D_MODEL // 4),
                                              3 * D_MODEL // 4, 2),
        "w_in": grad_w_in, "w_out": grad_w_out,
    }
```
